```python
import math
import jax, jax.numpy as jnp
from jax import lax
import numpy as np

D_MODEL = 2048
BATCH = 4
SEQ = 2048
DEPTH = 2
DEC_BATCH = 128
DEC_SEQ = 1
PAST_LEN = 16384
PAGE_SIZE = 128

ML_HEADS = 4
ML_DV = D_MODEL // 8
ML_DK = ML_DV
ML_WIDTH = ML_HEADS * ML_DV
S5_CH = 16
S5_WIDTH = D_MODEL // 4
S5_GROUPS = S5_WIDTH // S5_CH
S5_STATE = 64
SSD_HEAD_DIM = 64
SSD_WIDTH = D_MODEL // 4
SSD_HEADS = SSD_WIDTH // SSD_HEAD_DIM
SSD_GROUPS = 2
SSD_STATE = 128
SSD_CONV = 4
SSD_CONV_CH = SSD_WIDTH + 2 * SSD_GROUPS * SSD_STATE
MIX_WIDTH = ML_WIDTH + S5_WIDTH + SSD_WIDTH
IN_SIZES = (ML_HEADS * ML_DK, ML_HEADS * ML_DK, ML_WIDTH, ML_WIDTH, ML_HEADS, ML_HEADS,
            S5_WIDTH, SSD_WIDTH, SSD_CONV_CH, SSD_HEADS)
CHUNK = 64
D_FF = 5632
N_EXPERTS = 8
TOP_K = 2
D_FF_EXPERT = 7168
PLE_DIM = 256
RMS_EPS = 1e-6

kernel_name = 'hybrid_mlstm_s5_ssd_decoder_step'


def rmsnorm(x, g):
    xf = x.astype(jnp.float32)
    y = xf * lax.rsqrt(jnp.mean(xf * xf, axis=-1, keepdims=True) + RMS_EPS)
    return (y * g.astype(jnp.float32)).astype(x.dtype)


def swiglu(c, wg, wu, wd):
    return (jax.nn.silu(c @ wg) * (c @ wu)) @ wd


def moe_swiglu(c, w_router, b_router, wg, wu, wd):
    f32 = jnp.float32
    logits = c.astype(f32) @ w_router.astype(f32) + b_router.astype(f32)
    top_v, top_i = lax.top_k(logits, TOP_K)
    gates = jax.nn.softmax(top_v, axis=-1)
    combine = jnp.einsum('blk,blke->ble', gates, jax.nn.one_hot(top_i, N_EXPERTS, dtype=f32))
    out = jnp.zeros_like(c)
    for e in range(N_EXPERTS):
        out = out + combine[..., e:e + 1].astype(c.dtype) * swiglu(c, wg[e], wu[e], wd[e])
    return out


def _to_chunks(t, cl):
    b, l = t.shape[:2]
    t = t.reshape((b, l // cl, cl) + t.shape[2:])
    return jnp.swapaxes(jnp.moveaxis(t, 1, 0), 2, 3)


def _from_chunks(t):
    t = jnp.moveaxis(jnp.swapaxes(t, 2, 3), 0, 1)
    return t.reshape((t.shape[0], t.shape[1] * t.shape[2]) + t.shape[3:])


def mlstm_scan(q, k, v, i_pre, f_pre, C0, n0, m0):
    L = q.shape[1]
    cl = math.gcd(L, CHUNK)
    causal = jnp.tril(jnp.ones((cl, cl), dtype=bool))
    xs = tuple(_to_chunks(t, cl) for t in (q, k, v, i_pre, f_pre))

    def step(carry, inp):
        C, n, m = carry
        qc, kc, vc, ic, fc = inp
        b = jnp.cumsum(jax.nn.log_sigmoid(fc), axis=-1)
        d = jnp.where(causal, b[..., :, None] - b[..., None, :] + ic[..., None, :], -jnp.inf)
        inter = b + m[..., None]
        m_t = jnp.maximum(inter, jnp.max(d, axis=-1))
        w = jnp.exp(d - m_t[..., None])
        g = jnp.exp(inter - m_t)
        s = jnp.einsum('bhtd,bhsd->bhts', qc, kc) * w
        num = jnp.einsum('bhts,bhsv->bhtv', s, vc) + g[..., None] * jnp.einsum('bhtd,bhdv->bhtv', qc, C)
        den = jnp.sum(s, axis=-1) + g * jnp.einsum('bhtd,bhd->bht', qc, n)
        h = num / jnp.maximum(jnp.abs(den), jnp.exp(-m_t))[..., None]
        dl = b[..., -1:] - b + ic
        m_new = jnp.maximum(b[..., -1] + m, jnp.max(dl, axis=-1))
        ws = jnp.exp(dl - m_new[..., None])
        gl = jnp.exp(b[..., -1] + m - m_new)
        kw = kc * ws[..., None]
        C_new = gl[..., None, None] * C + jnp.einsum('bhsd,bhsv->bhdv', kw, vc)
        n_new = gl[..., None] * n + jnp.sum(kw, axis=-2)
        return (C_new, n_new, m_new), h

    (C, n, m), h = lax.scan(step, (C0, n0, m0), xs)
    return _from_chunks(h), C, n, m


def s5_scan(u, lam_re, lam_im, log_dt, b_re, b_im, c_re, c_im, d_skip, sre0, sim0):
    f32 = jnp.float32
    lam = lax.complex(lam_re.astype(f32), lam_im.astype(f32))
    dt = jnp.exp(log_dt.astype(f32))[:, None]
    lam_bar = jnp.exp(lam * dt)
    b_bar = ((lam_bar - 1.0) / lam)[..., None] * lax.complex(b_re.astype(f32), b_im.astype(f32))
    bu = jnp.einsum('gpc,blgc->blgp', b_bar, u.astype(jnp.complex64))
    s0 = lax.complex(sre0.astype(f32), sim0.astype(f32))
    bu = bu.at[:, 0].add(lam_bar * s0)
    a_el = jnp.broadcast_to(lam_bar, bu.shape)

    def comb(e1, e2):
        a1, x1 = e1
        a2, x2 = e2
        return a1 * a2, a2 * x1 + x2

    _, xs = lax.associative_scan(comb, (a_el, bu), axis=1)
    c_c = lax.complex(c_re.astype(f32), c_im.astype(f32))
    y = jnp.real(jnp.einsum('gcp,blgp->blgc', c_c, xs)) + d_skip.astype(f32) * u
    last = xs[:, -1]
    return y, jnp.real(last), jnp.imag(last)


def ssd_scan(x, dt, A, bm, cm, S0):
    L = x.shape[1]
    cl = math.gcd(L, CHUNK)
    causal = jnp.tril(jnp.ones((cl, cl), dtype=bool))
    xs = tuple(_to_chunks(t, cl) for t in (x * dt[..., None], dt * A, bm, cm))

    def step(S, inp):
        xc, ac, bc, cc = inp
        cum = jnp.cumsum(ac, axis=-1)
        seg = jnp.exp(jnp.where(causal, cum[..., :, None] - cum[..., None, :], -jnp.inf))
        scores = jnp.einsum('bhtn,bhsn->bhts', cc, bc) * seg
        y = jnp.einsum('bhts,bhsp->bhtp', scores, xc) + jnp.exp(cum)[..., None] * jnp.einsum('bhtn,bhpn->bhtp', cc, S)
        w_end = jnp.exp(cum[..., -1:] - cum)
        S_new = jnp.exp(cum[..., -1])[..., None, None] * S + jnp.einsum('bhsp,bhsn->bhpn', xc * w_end[..., None], bc)
        return S_new, y

    S, y = lax.scan(step, S0, xs)
    return _from_chunks(y), S


def _hybrid_mixer(a, C0, n0, m0, sre0, sim0, ssd0, conv0, w_in, b_ig, b_fg, g_ml,
                  lam_re, lam_im, log_dt, s5_b_re, s5_b_im, s5_c_re, s5_c_im, s5_d, w_glu, b_glu, g_s5,
                  conv_w, conv_b, dt_bias, a_log, ssd_d, g_ssd, w_out):
    f32 = jnp.float32
    bsz, L, _ = a.shape
    zin = (a @ w_in).astype(f32)
    offs = np.cumsum(IN_SIZES)[:-1].tolist()
    q, k, v, o, ig, fg, u, z, xbc, dtr = jnp.split(zin, offs, axis=-1)

    q = q.reshape(bsz, L, ML_HEADS, ML_DK)
    k = k.reshape(bsz, L, ML_HEADS, ML_DK) * (ML_DK ** -0.5)
    v = v.reshape(bsz, L, ML_HEADS, ML_DV)
    h_ml, C1, n1, m1 = mlstm_scan(q, k, v, ig + b_ig.astype(f32), fg + b_fg.astype(f32),
                                  C0.astype(f32), n0.astype(f32), m0.astype(f32))
    h_ml = jax.nn.sigmoid(o) * rmsnorm(h_ml, g_ml).reshape(bsz, L, ML_WIDTH)

    y5, sre1, sim1 = s5_scan(u.reshape(bsz, L, S5_GROUPS, S5_CH), lam_re, lam_im, log_dt,
                             s5_b_re, s5_b_im, s5_c_re, s5_c_im, s5_d, sre0, sim0)
    y5 = jax.nn.gelu(y5.reshape(bsz, L, S5_WIDTH))
    y5 = rmsnorm(y5 * jax.nn.sigmoid(y5 @ w_glu.astype(f32) + b_glu.astype(f32)), g_s5)

    xp = jnp.concatenate([conv0.astype(f32), xbc], axis=1)
    cw = conv_w.astype(f32)
    xc = conv_b.astype(f32) + sum(cw[j] * xp[:, j:j + L] for j in range(SSD_CONV))
    conv1 = xp[:, L:]
    xc = jax.nn.silu(xc)
    xs_, bm, cm = jnp.split(xc, [SSD_WIDTH, SSD_WIDTH + SSD_GROUPS * SSD_STATE], axis=-1)
    xs_ = xs_.reshape(bsz, L, SSD_HEADS, SSD_HEAD_DIM)
    rep = SSD_HEADS // SSD_GROUPS
    bm = jnp.repeat(bm.reshape(bsz, L, SSD_GROUPS, SSD_STATE), rep, axis=2)
    cm = jnp.repeat(cm.reshape(bsz, L, SSD_GROUPS, SSD_STATE), rep, axis=2)
    dt = jax.nn.softplus(dtr + dt_bias.astype(f32))
    A = -jnp.exp(a_log.astype(f32))
    y_ssd, ssd1 = ssd_scan(xs_, dt, A, bm, cm, ssd0.astype(f32))
    y_ssd = y_ssd + ssd_d.astype(f32)[:, None] * xs_
    y_ssd = rmsnorm(y_ssd.reshape(bsz, L, SSD_WIDTH) * jax.nn.silu(z), g_ssd)

    out = jnp.concatenate([h_ml, y5, y_ssd], axis=-1).astype(a.dtype) @ w_out
    return out, (C1, n1, m1, sre1, sim1, ssd1, conv1)


def _trunk(x, p, C0, n0, m0, sre0, sim0, ssd0, conv0, wts):
    (g_mix, w_in, b_igate, b_fgate, g_ml, s5_lam_re, s5_lam_im, s5_log_dt, s5_b_re, s5_b_im,
     s5_c_re, s5_c_im, s5_d, s5_w_glu, s5_b_glu, g_s5, ssd_conv_w, ssd_conv_b, ssd_dt_bias,
     ssd_a_log, ssd_d, g_ssd, w_out, g_ffn, ffn_w_gate, ffn_w_up, ffn_w_down, w_router, b_router,
     moe_w_gate, moe_w_up, moe_w_down, g_ple, w_ple, w_ple_gate, g_final) = wts
    h = x
    new_states = [[] for _ in range(7)]
    for i in range(DEPTH):
        mix, st = _hybrid_mixer(rmsnorm(h, g_mix[i]), C0[i], n0[i], m0[i], sre0[i], sim0[i], ssd0[i], conv0[i],
                                w_in[i], b_igate[i], b_fgate[i], g_ml[i], s5_lam_re[i], s5_lam_im[i], s5_log_dt[i],
                                s5_b_re[i], s5_b_im[i], s5_c_re[i], s5_c_im[i], s5_d[i], s5_w_glu[i], s5_b_glu[i],
                                g_s5[i], ssd_conv_w[i], ssd_conv_b[i], ssd_dt_bias[i], ssd_a_log[i], ssd_d[i],
                                g_ssd[i], w_out[i])
        for lst, s in zip(new_states, st):
            lst.append(s)
        h = h + mix
        c = rmsnorm(h, g_ffn[i])
        j = i // 2
        if i % 2 == 0:
            h = h + swiglu(c, ffn_w_gate[j], ffn_w_up[j], ffn_w_down[j])
        else:
            h = h + moe_swiglu(c, w_router[j], b_router[j], moe_w_gate[j], moe_w_up[j], moe_w_down[j])
        e = rmsnorm(h, g_ple[i])
        h = h + (p[i] @ w_ple[i]) * jax.nn.sigmoid(e @ w_ple_gate[i])
    return rmsnorm(h, g_final), tuple(jnp.stack(lst) for lst in new_states)


def setup_inputs(seed: int = 0) -> dict:
    key = jax.random.key(seed)
    keys = iter(jax.random.split(key, 64))
    f32 = jnp.float32

    def nrm(shape, scale):
        return jax.random.normal(next(keys), shape, f32) * scale

    def unif(shape, lo, hi):
        return jax.random.uniform(next(keys), shape, f32, lo, hi)

    n_dense = (DEPTH + 1) // 2
    n_moe = DEPTH // 2
    n_in = sum(IN_SIZES)
    dt_ssd = jnp.exp(unif((DEPTH, SSD_HEADS), math.log(1e-3), math.log(1e-1)))
    return {
        'x_prompt': nrm((BATCH, SEQ, D_MODEL), 1.0),
        'x_sample': nrm((DEC_BATCH, DEC_SEQ, D_MODEL), 1.0),
        'state_mlstm_C': nrm((DEPTH, DEC_BATCH, ML_HEADS, ML_DK, ML_DV), 0.1),
        'state_mlstm_n': nrm((DEPTH, DEC_BATCH, ML_HEADS, ML_DK), 0.3),
        'state_mlstm_m': nrm((DEPTH, DEC_BATCH, ML_HEADS), 1.0),
        'state_s5_re': nrm((DEPTH, DEC_BATCH, S5_GROUPS, S5_STATE), 0.1),
        'state_s5_im': nrm((DEPTH, DEC_BATCH, S5_GROUPS, S5_STATE), 0.1),
        'state_ssd': nrm((DEPTH, DEC_BATCH, SSD_HEADS, SSD_HEAD_DIM, SSD_STATE), 0.3),
        'cache_conv': nrm((DEPTH, DEC_BATCH, SSD_CONV - 1, SSD_CONV_CH), 1.0),
        'p_prompt': nrm((DEPTH, BATCH, SEQ, PLE_DIM), 1.0),
        'p_sample': nrm((DEPTH, DEC_BATCH, DEC_SEQ, PLE_DIM), 1.0),
        'g_mix': 1.0 + nrm((DEPTH, D_MODEL), 0.02),
        'w_in': nrm((DEPTH, D_MODEL, n_in), D_MODEL ** -0.5),
        'b_igate': nrm((DEPTH, ML_HEADS), 0.1),
        'b_fgate': jnp.linspace(3.0, 6.0, ML_HEADS, dtype=f32) + nrm((DEPTH, ML_HEADS), 0.1),
        'g_ml': 1.0 + nrm((DEPTH, ML_HEADS, ML_DV), 0.02),
        's5_lam_re': -0.5 + nrm((DEPTH, S5_GROUPS, S5_STATE), 0.01),
        's5_lam_im': jnp.pi * jnp.arange(S5_STATE, dtype=f32) + nrm((DEPTH, S5_GROUPS, S5_STATE), 0.01),
        's5_log_dt': unif((DEPTH, S5_GROUPS), math.log(1e-3), math.log(1e-1)),
        's5_b_re': nrm((DEPTH, S5_GROUPS, S5_STATE, S5_CH), (2 * S5_CH) ** -0.5),
        's5_b_im': nrm((DEPTH, S5_GROUPS, S5_STATE, S5_CH), (2 * S5_CH) ** -0.5),
        's5_c_re': nrm((DEPTH, S5_GROUPS, S5_CH, S5_STATE), (2 * S5_STATE) ** -0.5),
        's5_c_im': nrm((DEPTH, S5_GROUPS, S5_CH, S5_STATE), (2 * S5_STATE) ** -0.5),
        's5_d': nrm((DEPTH, S5_GROUPS, S5_CH), 1.0),
        's5_w_glu': nrm((DEPTH, S5_WIDTH, S5_WIDTH), S5_WIDTH ** -0.5),
        's5_b_glu': nrm((DEPTH, S5_WIDTH), 0.02),
        'g_s5': 1.0 + nrm((DEPTH, S5_WIDTH), 0.02),
        'ssd_conv_w': nrm((DEPTH, SSD_CONV, SSD_CONV_CH), SSD_CONV ** -0.5),
        'ssd_conv_b': nrm((DEPTH, SSD_CONV_CH), 0.02),
        'ssd_dt_bias': dt_ssd + jnp.log(-jnp.expm1(-dt_ssd)),
        'ssd_a_log': jnp.log(unif((DEPTH, SSD_HEADS), 1.0, 16.0)),
        'ssd_d': 1.0 + nrm((DEPTH, SSD_HEADS), 0.1),
        'g_ssd': 1.0 + nrm((DEPTH, SSD_WIDTH), 0.02),
        'w_out': nrm((DEPTH, MIX_WIDTH, D_MODEL), MIX_WIDTH ** -0.5),
        'g_ffn': 1.0 + nrm((DEPTH, D_MODEL), 0.02),
        'ffn_w_gate': nrm((n_dense, D_MODEL, D_FF), D_MODEL ** -0.5),
        'ffn_w_up': nrm((n_dense, D_MODEL, D_FF), D_MODEL ** -0.5),
        'ffn_w_down': nrm((n_dense, D_FF, D_MODEL), D_FF ** -0.5),
        'w_router': nrm((n_moe, D_MODEL, N_EXPERTS), D_MODEL ** -0.5),
        'b_router': nrm((n_moe, N_EXPERTS), 0.01),
        'moe_w_gate': nrm((n_moe, N_EXPERTS, D_MODEL, D_FF_EXPERT), D_MODEL ** -0.5),
        'moe_w_up': nrm((n_moe, N_EXPERTS, D_MODEL, D_FF_EXPERT), D_MODEL ** -0.5),
        'moe_w_down': nrm((n_moe, N_EXPERTS, D_FF_EXPERT, D_MODEL), D_FF_EXPERT ** -0.5),
        'g_ple': 1.0 + nrm((DEPTH, D_MODEL), 0.02),
        'w_ple': nrm((DEPTH, PLE_DIM, D_MODEL), PLE_DIM ** -0.5),
        'w_ple_gate': nrm((DEPTH, D_MODEL, D_MODEL), D_MODEL ** -0.5),
        'g_final': 1.0 + nrm((D_MODEL,), 0.02),
    }


def reference(x_prompt, x_sample, state_mlstm_C, state_mlstm_n, state_mlstm_m, state_s5_re, state_s5_im,
              state_ssd, cache_conv, p_prompt, p_sample, g_mix, w_in, b_igate, b_fgate, g_ml,
              s5_lam_re, s5_lam_im, s5_log_dt, s5_b_re, s5_b_im, s5_c_re, s5_c_im, s5_d, s5_w_glu, s5_b_glu,
              g_s5, ssd_conv_w, ssd_conv_b, ssd_dt_bias, ssd_a_log, ssd_d, g_ssd, w_out, g_ffn,
              ffn_w_gate, ffn_w_up, ffn_w_down, w_router, b_router, moe_w_gate, moe_w_up, moe_w_down,
              g_ple, w_ple, w_ple_gate, g_final):
    wts = (g_mix, w_in, b_igate, b_fgate, g_ml, s5_lam_re, s5_lam_im, s5_log_dt, s5_b_re, s5_b_im,
           s5_c_re, s5_c_im, s5_d, s5_w_glu, s5_b_glu, g_s5, ssd_conv_w, ssd_conv_b, ssd_dt_bias,
           ssd_a_log, ssd_d, g_ssd, w_out, g_ffn, ffn_w_gate, ffn_w_up, ffn_w_down, w_router, b_router,
           moe_w_gate, moe_w_up, moe_w_down, g_ple, w_ple, w_ple_gate, g_final)
    bp = x_prompt.shape[0]

    def zeros(*shape):
        return jnp.zeros((DEPTH, bp) + shape, jnp.float32)

    y_prompt, (C_p, n_p, m_p, s5re_p, s5im_p, ssd_p, conv_p) = _trunk(
        x_prompt, p_prompt, zeros(ML_HEADS, ML_DK, ML_DV), zeros(ML_HEADS, ML_DK), zeros(ML_HEADS),
        zeros(S5_GROUPS, S5_STATE), zeros(S5_GROUPS, S5_STATE),
        zeros(SSD_HEADS, SSD_HEAD_DIM, SSD_STATE), zeros(SSD_CONV - 1, SSD_CONV_CH), wts)
    y_sample, (C_s, n_s, m_s, s5re_s, s5im_s, ssd_s, conv_s) = _trunk(
        x_sample, p_sample, state_mlstm_C, state_mlstm_n, state_mlstm_m, state_s5_re, state_s5_im,
        state_ssd, cache_conv, wts)
    return (y_prompt, y_sample, C_p, n_p, m_p, s5re_p, s5im_p, ssd_p, conv_p,
            C_s, n_s, m_s, s5re_s, s5im_s, ssd_s, conv_s)
```

```python
import functools
import math

import jax
import jax.numpy as jnp
from jax import lax
from jax.experimental import pallas as pl
from jax.experimental.pallas import tpu as pltpu

f32 = jnp.float32
bf16 = jnp.bfloat16
HI = lax.Precision.HIGHEST

D_MODEL = 2048
DEPTH = 2
ML_HEADS = 4
ML_DK = 256
ML_DV = 256
ML_WIDTH = ML_HEADS * ML_DV
S5_CH = 16
S5_WIDTH = 512
S5_GROUPS = 32
S5_STATE = 64
SSD_HEAD_DIM = 64
SSD_WIDTH = 512
SSD_HEADS = 8
SSD_GROUPS = 2
SSD_STATE = 128
SSD_CONV = 4
SSD_CONV_CH = 1024
CHUNK = 64
S5_CHUNK = 32
N_EXPERTS = 8
TOP_K = 2
RMS_EPS = 1e-6

_OFF_GATES = 4 * ML_WIDTH
_OFF_U = _OFF_GATES + 2 * ML_HEADS
_OFF_DT = _OFF_U + S5_WIDTH + SSD_WIDTH + SSD_CONV_CH

_VMEM_LIMIT = 56 * 1024 * 1024
_NT = (((1,), (1,)), ((), ()))
_TN = (((0,), (0,)), ((), ()))


def _params(*sem):
    return pltpu.CompilerParams(dimension_semantics=sem, vmem_limit_bytes=_VMEM_LIMIT)


def _rms(x, g):
    return x * lax.rsqrt(jnp.mean(x * x, axis=-1, keepdims=True) + RMS_EPS) * g


def _rmsnorm_body(x_ref, g_ref, o_ref):
    o_ref[...] = _rms(x_ref[...], g_ref[...]).astype(o_ref.dtype)


def _rmsnorm(x, g, out_dtype, tm):
    m, d = x.shape
    return pl.pallas_call(
        _rmsnorm_body, grid=(m // tm,),
        in_specs=[pl.BlockSpec((tm, d), lambda i: (i, 0)), pl.BlockSpec((1, d), lambda i: (0, 0))],
        out_specs=pl.BlockSpec((tm, d), lambda i: (i, 0)),
        out_shape=jax.ShapeDtypeStruct((m, d), out_dtype),
        compiler_params=_params("parallel"), name="rmsnorm")(x, g.reshape(1, d))


def _cast_weight_once(w_ref, wb_ref):
    @pl.when(pl.program_id(1) == 0)
    def _():
        wb_ref[...] = w_ref[...].astype(bf16)


def _mm_plain_body(a_ref, w_ref, o_ref, wb_ref):
    _cast_weight_once(w_ref, wb_ref)
    o_ref[...] = jnp.dot(a_ref[...], wb_ref[...], preferred_element_type=f32).astype(o_ref.dtype)


def _mm_resid_body(a_ref, w_ref, r_ref, o_ref, wb_ref):
    _cast_weight_once(w_ref, wb_ref)
    o_ref[...] = r_ref[...] + jnp.dot(a_ref[...], wb_ref[...], preferred_element_type=f32)


def _mm_swiglu_body(a_ref, wg_ref, wu_ref, o_ref, wgb_ref, wub_ref):
    _cast_weight_once(wg_ref, wgb_ref)
    _cast_weight_once(wu_ref, wub_ref)
    a = a_ref[...]
    g = jnp.dot(a, wgb_ref[...], preferred_element_type=f32)
    u = jnp.dot(a, wub_ref[...], preferred_element_type=f32)
    o_ref[...] = (g * jax.nn.sigmoid(g) * u).astype(o_ref.dtype)


def _mm_ple_body(e_ref, p_ref, wg_ref, wp_ref, r_ref, o_ref, wgb_ref, wpb_ref):
    _cast_weight_once(wg_ref, wgb_ref)
    _cast_weight_once(wp_ref, wpb_ref)
    gate = jnp.dot(e_ref[...], wgb_ref[...], preferred_element_type=f32)
    emb = jnp.dot(p_ref[...], wpb_ref[...], preferred_element_type=f32)
    o_ref[...] = r_ref[...] + emb * jax.nn.sigmoid(gate)


def _wspec(w, layer, k, tn, col_block_off=0):
    if w.ndim == 2:
        return pl.BlockSpec((k, tn), lambda j, i: (0, j + col_block_off))
    return pl.BlockSpec((None, k, tn), lambda j, i: (layer, 0, j + col_block_off))


def _mm(a, w, *, tm, tn, layer=0, n_cols=None, col_off=0, resid=None, out_dtype=f32, name="mm"):
    m, k = a.shape
    n = n_cols if n_cols is not None else w.shape[-1]
    grid = (n // tn, m // tm)
    a_spec = pl.BlockSpec((tm, k), lambda j, i: (i, 0))
    o_spec = pl.BlockSpec((tm, tn), lambda j, i: (i, j))
    w_spec = _wspec(w, layer, k, tn, col_off // tn)
    scratch = [pltpu.VMEM((k, tn), bf16)]
    if resid is None:
        body, ins, specs = _mm_plain_body, (a, w), [a_spec, w_spec]
    else:
        body, ins, specs = _mm_resid_body, (a, w, resid), [a_spec, w_spec, o_spec]
    return pl.pallas_call(
        body, grid=grid, in_specs=specs, out_specs=o_spec,
        out_shape=jax.ShapeDtypeStruct((m, n), out_dtype), scratch_shapes=scratch,
        compiler_params=_params("arbitrary", "arbitrary"), name=name)(*ins)


def _mm_swiglu(a, wg, wu, *, tm, tn, layer=0, name="mm_swiglu"):
    m, k = a.shape
    n = wg.shape[-1]
    a_spec = pl.BlockSpec((tm, k), lambda j, i: (i, 0))
    o_spec = pl.BlockSpec((tm, tn), lambda j, i: (i, j))
    return pl.pallas_call(
        _mm_swiglu_body, grid=(n // tn, m // tm),
        in_specs=[a_spec, _wspec(wg, layer, k, tn), _wspec(wu, layer, k, tn)], out_specs=o_spec,
        out_shape=jax.ShapeDtypeStruct((m, n), bf16),
        scratch_shapes=[pltpu.VMEM((k, tn), bf16), pltpu.VMEM((k, tn), bf16)],
        compiler_params=_params("arbitrary", "arbitrary"), name=name)(a, wg, wu)


def _mm_ple(e, p, w_gate, w_ple, resid, *, tm, tn, layer, name="mm_ple"):
    m, k = e.shape
    kp = p.shape[1]
    n = w_gate.shape[-1]
    o_spec = pl.BlockSpec((tm, tn), lambda j, i: (i, j))
    return pl.pallas_call(
        _mm_ple_body, grid=(n // tn, m // tm),
        in_specs=[pl.BlockSpec((tm, k), lambda j, i: (i, 0)), pl.BlockSpec((tm, kp), lambda j, i: (i, 0)),
                  _wspec(w_gate, layer, k, tn), _wspec(w_ple, layer, kp, tn), o_spec],
        out_specs=o_spec, out_shape=jax.ShapeDtypeStruct((m, n), f32),
        scratch_shapes=[pltpu.VMEM((k, tn), bf16), pltpu.VMEM((kp, tn), bf16)],
        compiler_params=_params("arbitrary", "arbitrary"), name=name)(e, p, w_gate, w_ple, resid)


def _col_from_row(row, eye):
    n = row.shape[1]
    return jnp.sum(jnp.where(eye, jnp.broadcast_to(row, (n, n)), 0.0), axis=1, keepdims=True)


def _mlstm_prompt_body(q_ref, k_ref, v_ref, o_ref, gt_ref, bias_ref, gml_ref,
                       h_ref, c_out, n_out, m_out, c_s, n_s, m_s, *, heads, dk, dv, cl):
    c = pl.program_id(1)

    @pl.when(c == 0)
    def _():
        c_s[...] = jnp.zeros_like(c_s)
        n_s[...] = jnp.zeros_like(n_s)
        m_s[...] = jnp.zeros_like(m_s)

    row = lax.broadcasted_iota(jnp.int32, (cl, cl), 0)
    col = lax.broadcasted_iota(jnp.int32, (cl, cl), 1)
    causal = col <= row
    eye = col == row
    triu = (row <= col).astype(f32)
    gt = gt_ref[0, 0] + bias_ref[...]
    lf = jax.nn.log_sigmoid(gt[heads:2 * heads])
    b_all = jnp.dot(lf, triu, precision=HI, preferred_element_type=f32)
    for hh in range(heads):
        ig = gt[hh:hh + 1]
        b_row = b_all[hh:hh + 1]
        b_col = _col_from_row(b_row, eye)
        m_prev = m_s[hh]
        d = jnp.where(causal, b_col - b_row + ig, -jnp.inf)
        inter = b_col + m_prev
        m_t = jnp.maximum(inter, jnp.max(d, axis=1, keepdims=True))
        w = jnp.exp(d - m_t)
        g = jnp.exp(inter - m_t)
        q = q_ref[:, hh * dk:(hh + 1) * dk]
        k = k_ref[:, hh * dk:(hh + 1) * dk] * (dk ** -0.5)
        vb = v_ref[:, hh * dv:(hh + 1) * dv].astype(bf16)
        qb = q.astype(bf16)
        cmat = c_s[hh]
        n_row = n_s[hh]
        s = lax.dot_general(qb, k.astype(bf16), _NT, preferred_element_type=f32) * w
        num = (jnp.dot(s.astype(bf16), vb, preferred_element_type=f32)
               + g * jnp.dot(qb, cmat.astype(bf16), preferred_element_type=f32))
        den = jnp.sum(s, axis=1, keepdims=True) + g * jnp.sum(q * n_row, axis=1, keepdims=True)
        hraw = num / jnp.maximum(jnp.abs(den), jnp.exp(-m_t))
        hn = _rms(hraw, gml_ref[hh:hh + 1, :])
        ogate = jax.nn.sigmoid(o_ref[:, hh * dv:(hh + 1) * dv])
        h_ref[:, hh * dv:(hh + 1) * dv] = (ogate * hn).astype(h_ref.dtype)
        b_last = b_row[:, cl - 1:cl]
        dl = b_last - b_row + ig
        m_new = jnp.maximum(b_last + m_prev, jnp.max(dl, axis=1, keepdims=True))
        ws_col = _col_from_row(jnp.exp(dl - m_new), eye)
        gl = jnp.exp(b_last + m_prev - m_new)
        kw = k * ws_col
        c_s[hh] = gl * cmat + lax.dot_general(kw.astype(bf16), vb, _TN, preferred_element_type=f32)
        n_s[hh] = gl * n_row + jnp.sum(kw, axis=0, keepdims=True)
        m_s[hh] = m_new

    @pl.when(c == pl.num_programs(1) - 1)
    def _():
        c_out[0] = c_s[...]
        n_out[0] = n_s[...]
        m_out[0] = m_s[...]


def _mlstm_prompt(qkvo, gates_t, bias_col, g_ml, *, batch, seq):
    heads, dk, dv, cl = ML_HEADS, ML_DK, ML_DV, math.gcd(seq, CHUNK)
    nc = seq // cl
    wq = heads * dk
    rows = lambda b, c: (b * nc + c, 0)
    body = functools.partial(_mlstm_prompt_body, heads=heads, dk=dk, dv=dv, cl=cl)
    return pl.pallas_call(
        body, grid=(batch, nc),
        in_specs=[pl.BlockSpec((cl, wq), lambda b, c: (b * nc + c, 0)),
                  pl.BlockSpec((cl, wq), lambda b, c: (b * nc + c, 1)),
                  pl.BlockSpec((cl, wq), lambda b, c: (b * nc + c, 2)),
                  pl.BlockSpec((cl, wq), lambda b, c: (b * nc + c, 3)),
                  pl.BlockSpec((1, 1, 2 * heads, cl), lambda b, c: (b, c, 0, 0)),
                  pl.BlockSpec((2 * heads, 1), lambda b, c: (0, 0)),
                  pl.BlockSpec((heads, dv), lambda b, c: (0, 0))],
        out_specs=[pl.BlockSpec((cl, heads * dv), rows),
                   pl.BlockSpec((1, heads, dk, dv), lambda b, c: (b, 0, 0, 0)),
                   pl.BlockSpec((1, heads, 1, dk), lambda b, c: (b, 0, 0, 0)),
                   pl.BlockSpec((1, heads, 1, 1), lambda b, c: (b, 0, 0, 0))],
        out_shape=[jax.ShapeDtypeStruct((batch * seq, heads * dv), bf16),
                   jax.ShapeDtypeStruct((batch, heads, dk, dv), f32),
                   jax.ShapeDtypeStruct((batch, heads, 1, dk), f32),
                   jax.ShapeDtypeStruct((batch, heads, 1, 1), f32)],
        scratch_shapes=[pltpu.VMEM((heads, dk, dv), f32), pltpu.VMEM((heads, 1, dk), f32),
                        pltpu.VMEM((heads, 1, 1), f32)],
        compiler_params=_params("arbitrary", "arbitrary"), name="mlstm_prompt",
    )(qkvo, qkvo, qkvo, qkvo, gates_t, bias_col, g_ml)


def _lane_pick(pieces, width):
    lane = lax.broadcasted_iota(jnp.int32, (1, width), 1)
    out = jnp.zeros((1, width), f32)
    for i, p in enumerate(pieces):
        out = jnp.where(lane == i, p, out)
    return out


def _mlstm_sample_body(x_ref, gd_ref, bias_ref, gml_ref, c0_ref, n0_ref, m0_ref,
                       h_ref, c_out, n_out, m_out, *, heads, dk, dv):
    wq = heads * dk
    g8 = gd_ref[0][:, 0:2 * heads] + bias_ref[...]
    m_new_all = []
    for hh in range(heads):
        ig = g8[:, hh:hh + 1]
        lf = jax.nn.log_sigmoid(g8[:, heads + hh:heads + hh + 1])
        m0 = m0_ref[0][:, hh:hh + 1]
        m_t = jnp.maximum(lf + m0, ig)
        w = jnp.exp(ig - m_t)
        g = jnp.exp(lf + m0 - m_t)
        q = x_ref[0][:, hh * dk:(hh + 1) * dk]
        k = x_ref[0][:, wq + hh * dk:wq + (hh + 1) * dk] * (dk ** -0.5)
        v = x_ref[0][:, 2 * wq + hh * dv:2 * wq + (hh + 1) * dv]
        o = x_ref[0][:, 3 * wq + hh * dv:3 * wq + (hh + 1) * dv]
        cmat = c0_ref[0, hh]
        n0 = n0_ref[0, hh]
        q_col = jnp.broadcast_to(q, (8, dk)).T[:, 0:1]
        k_col = jnp.broadcast_to(k, (8, dk)).T[:, 0:1]
        s = jnp.sum(q * k, axis=1, keepdims=True) * w
        num = s * v + g * jnp.sum(q_col * cmat, axis=0, keepdims=True)
        den = s + g * jnp.sum(q * n0, axis=1, keepdims=True)
        hraw = num / jnp.maximum(jnp.abs(den), jnp.exp(-m_t))
        hn = _rms(hraw, gml_ref[hh:hh + 1, :])
        h_ref[0, :, hh * dv:(hh + 1) * dv] = jax.nn.sigmoid(o) * hn
        c_out[0, hh] = g * cmat + (k_col * w) * v
        n_out[0, hh] = g * n0 + k * w
        m_new_all.append(m_t)
    m_out[0] = _lane_pick(m_new_all, heads)


def _mlstm_sample(qkvo_s, gd_s, bias_row, g_ml, c0, n0, m0, *, layer):
    heads, dk, dv = ML_HEADS, ML_DK, ML_DV
    b = qkvo_s.shape[0]
    body = functools.partial(_mlstm_sample_body, heads=heads, dk=dk, dv=dv)
    return pl.pallas_call(
        body, grid=(b,),
        in_specs=[pl.BlockSpec((1, 1, 4 * heads * dk), lambda i: (i, 0, 0)),
                  pl.BlockSpec((1, 1, gd_s.shape[-1]), lambda i: (i, 0, 0)),
                  pl.BlockSpec((1, 2 * heads), lambda i: (0, 0)),
                  pl.BlockSpec((heads, dv), lambda i: (0, 0)),
                  pl.BlockSpec((None, 1, heads, dk, dv), lambda i: (layer, i, 0, 0, 0)),
                  pl.BlockSpec((None, 1, heads, 1, dk), lambda i: (layer, i, 0, 0, 0)),
                  pl.BlockSpec((None, 1, 1, heads), lambda i: (layer, i, 0, 0))],
        out_specs=[pl.BlockSpec((1, 1, heads * dv), lambda i: (i, 0, 0)),
                   pl.BlockSpec((1, heads, dk, dv), lambda i: (i, 0, 0, 0)),
                   pl.BlockSpec((1, heads, 1, dk), lambda i: (i, 0, 0, 0)),
                   pl.BlockSpec((1, 1, heads), lambda i: (i, 0, 0))],
        out_shape=[jax.ShapeDtypeStruct((b, 1, heads * dv), f32),
                   jax.ShapeDtypeStruct((b, heads, dk, dv), f32),
                   jax.ShapeDtypeStruct((b, heads, 1, dk), f32),
                   jax.ShapeDtypeStruct((b, 1, heads), f32)],
        compiler_params=_params("parallel"), name="mlstm_sample",
    )(qkvo_s, gd_s, bias_row, g_ml, c0, n0, m0)


def _s5_body(u_ref, m_ref, wre_ref, wim_ref, vre_ref, vim_ref, lre_ref, lim_ref, x0re_ref, x0im_ref,
             y_ref, xre_out, xim_out, xs_re, xs_im, *, nc, batch):
    u = u_ref[0]
    xin_re = jnp.dot(u, wre_ref[0], preferred_element_type=f32)
    xin_im = jnp.dot(u, wim_ref[0], preferred_element_type=f32)
    lre = lre_ref[0]
    lim = lim_ref[0]
    xr = x0re_ref[0]
    xi = x0im_ref[0]
    for k in range(nc):
        sl = slice(k * batch, (k + 1) * batch)
        xs_re[sl, :] = xr
        xs_im[sl, :] = xi
        xr, xi = (lre * xr - lim * xi + xin_re[sl, :], lre * xi + lim * xr + xin_im[sl, :])
    xre_out[0] = xr
    xim_out[0] = xi
    y_ref[0] = (jnp.dot(u, m_ref[0], preferred_element_type=f32)
                + jnp.dot(xs_re[...].astype(bf16), vre_ref[0], preferred_element_type=f32)
                + jnp.dot(xs_im[...].astype(bf16), vim_ref[0], preferred_element_type=f32))


def _s5_scan(u_g, mats, x0re, x0im, *, nc, batch):
    m_mat, w_re, w_im, v_re, v_im, l_re, l_im = mats
    g, rows, tc = u_g.shape
    p = w_re.shape[-1]
    blk = lambda *s: pl.BlockSpec((1,) + s, lambda i: (i,) + (0,) * len(s))
    body = functools.partial(_s5_body, nc=nc, batch=batch)
    return pl.pallas_call(
        body, grid=(g,),
        in_specs=[blk(rows, tc), blk(tc, tc), blk(tc, p), blk(tc, p), blk(p, tc), blk(p, tc),
                  blk(1, p), blk(1, p), blk(batch, p), blk(batch, p)],
        out_specs=[blk(rows, tc), blk(batch, p), blk(batch, p)],
        out_shape=[jax.ShapeDtypeStruct((g, rows, tc), f32),
                   jax.ShapeDtypeStruct((g, batch, p), f32), jax.ShapeDtypeStruct((g, batch, p), f32)],
        scratch_shapes=[pltpu.VMEM((rows, p), f32), pltpu.VMEM((rows, p), f32)],
        compiler_params=_params("parallel"), name=f"s5_scan_t{tc // S5_CH}",
    )(u_g, m_mat, w_re, w_im, v_re, v_im, l_re, l_im, x0re, x0im)


def _s5_matrices(lam_re, lam_im, log_dt, b_re, b_im, c_re, c_im, t):
    g, p = lam_re.shape
    ch = b_re.shape[-1]
    lam = lax.complex(lam_re, lam_im)
    ldt = lam * jnp.exp(log_dt)[:, None]
    lam_bar = jnp.exp(ldt)
    b_bar = ((lam_bar - 1.0) / lam)[..., None] * lax.complex(b_re, b_im)
    c_c = lax.complex(c_re, c_im)
    taus = jnp.arange(t + 1, dtype=f32)
    pw = jnp.exp(ldt[:, None, :] * taus[None, :, None])
    cb = c_c[:, :, None, :] * jnp.swapaxes(b_bar, 1, 2)[:, None, :, :]
    kern = (jnp.einsum("gcdp,gtp->gtcd", jnp.real(cb), jnp.real(pw[:, :t]), precision=HI)
            - jnp.einsum("gcdp,gtp->gtcd", jnp.imag(cb), jnp.imag(pw[:, :t]), precision=HI))
    diff = jnp.arange(t)[None, :] - jnp.arange(t)[:, None]
    kst = jnp.where((diff >= 0)[None, :, :, None, None], kern[:, jnp.clip(diff, 0, t - 1)], 0.0)
    m_mat = jnp.transpose(kst, (0, 1, 4, 2, 3)).reshape(g, t * ch, t * ch)
    wc = pw[:, t - 1 - jnp.arange(t)][:, :, None, :] * jnp.swapaxes(b_bar, 1, 2)[:, None, :, :]
    wc = wc.reshape(g, t * ch, p)
    cl = jnp.swapaxes(c_c, 1, 2)[:, :, None, :] * jnp.swapaxes(pw[:, 1:t + 1], 1, 2)[:, :, :, None]
    cl = cl.reshape(g, p, t * ch)
    lam_t = pw[:, t]
    return (m_mat.astype(bf16), jnp.real(wc).astype(bf16), jnp.imag(wc).astype(bf16),
            jnp.real(cl).astype(bf16), (-jnp.imag(cl)).astype(bf16),
            jnp.real(lam_t)[:, None, :], jnp.imag(lam_t)[:, None, :])


def _s5_glu_body(y_ref, u_ref, d_ref, w_ref, b_ref, g_ref, o_ref):
    y5 = jax.nn.gelu(y_ref[...] + d_ref[...] * u_ref[...])
    gate = jax.nn.sigmoid(jnp.dot(y5.astype(bf16), w_ref[...].astype(bf16), preferred_element_type=f32) + b_ref[...])
    o_ref[...] = _rms(y5 * gate, g_ref[...]).astype(o_ref.dtype)


def _s5_glu(y_raw, uzx, d_skip, w_glu, b_glu, g_s5, *, layer, tm):
    m, wdt = y_raw.shape
    row = lambda i: (i, 0)
    fix = lambda i: (0, 0)
    return pl.pallas_call(
        _s5_glu_body, grid=(m // tm,),
        in_specs=[pl.BlockSpec((tm, wdt), row), pl.BlockSpec((tm, wdt), row), pl.BlockSpec((1, wdt), fix),
                  pl.BlockSpec((None, wdt, wdt), lambda i: (layer, 0, 0)), pl.BlockSpec((1, wdt), fix),
                  pl.BlockSpec((1, wdt), fix)],
        out_specs=pl.BlockSpec((tm, wdt), row), out_shape=jax.ShapeDtypeStruct((m, wdt), bf16),
        compiler_params=_params("parallel"), name="s5_glu",
    )(y_raw, uzx, d_skip, w_glu, b_glu, g_s5)


def _ssd_prompt_body(xbc_ref, z_ref, gd_ref, dtt_ref, cw_ref, cb_ref, dtb_row, dtb_col, alog_row, alog_col,
                     dskip_ref, gssd_ref, y_ref, s_out, s_s, xp_s, ys_s, *, heads, hd, ns, groups, cl, width):
    c = pl.program_id(1)

    @pl.when(c == 0)
    def _():
        s_s[...] = jnp.zeros_like(s_s)
        xp_s[0:8, :] = jnp.zeros((8, xp_s.shape[1]), f32)

    xp_s[8:8 + cl, :] = xbc_ref[...]
    xc = cb_ref[...] + sum(cw_ref[j:j + 1, :] * xp_s[5 + j:5 + j + cl, :] for j in range(SSD_CONV))
    xp_s[0:8, :] = xp_s[cl:cl + 8, :]
    xc = xc * jax.nn.sigmoid(xc)
    row = lax.broadcasted_iota(jnp.int32, (cl, cl), 0)
    col = lax.broadcasted_iota(jnp.int32, (cl, cl), 1)
    causal = col <= row
    tril = causal.astype(f32)
    triu = (row <= col).astype(f32)
    dt_col = jax.nn.softplus(gd_ref[:, 8:8 + heads] + dtb_row[...])
    dt_row = jax.nn.softplus(dtt_ref[0, 0] + dtb_col[...])
    cum_col = jnp.dot(tril, dt_col * -jnp.exp(alog_row[...]), precision=HI, preferred_element_type=f32)
    cum_row = jnp.dot(dt_row * -jnp.exp(alog_col[...]), triu, precision=HI, preferred_element_type=f32)
    rep = heads // groups
    for gi in range(groups):
        bm = xc[:, width + gi * ns:width + (gi + 1) * ns].astype(bf16)
        cm = xc[:, width + (groups + gi) * ns:width + (groups + gi + 1) * ns].astype(bf16)
        scores = lax.dot_general(cm, bm, _NT, preferred_element_type=f32)
        for hh in range(gi * rep, (gi + 1) * rep):
            cc = cum_col[:, hh:hh + 1]
            cr = cum_row[hh:hh + 1, :]
            seg = jnp.exp(jnp.where(causal, cc - cr, -jnp.inf))
            xh = xc[:, hh * hd:(hh + 1) * hd]
            xdt = xh * dt_col[:, hh:hh + 1]
            smat = s_s[hh]
            y = (jnp.dot((scores * seg).astype(bf16), xdt.astype(bf16), preferred_element_type=f32)
                 + jnp.exp(cc) * lax.dot_general(cm, smat.astype(bf16), _NT, preferred_element_type=f32))
            c_last = cr[:, cl - 1:cl]
            xw = (xdt * jnp.exp(c_last - cc)).astype(bf16)
            s_s[hh] = jnp.exp(c_last) * smat + lax.dot_general(xw, bm, _TN, preferred_element_type=f32)
            ys_s[:, hh * hd:(hh + 1) * hd] = y + dskip_ref[:, hh:hh + 1] * xh
    z = z_ref[...]
    y_ref[...] = _rms(ys_s[...] * (z * jax.nn.sigmoid(z)), gssd_ref[...]).astype(y_ref.dtype)

    @pl.when(c == pl.num_programs(1) - 1)
    def _():
        s_out[0] = s_s[...]


def _ssd_prompt(uzx, gd, dt_t, conv_w, conv_b, dt_bias, a_log, d_skip, g_ssd, *, batch, seq):
    heads, hd, ns, groups, width = SSD_HEADS, SSD_HEAD_DIM, SSD_STATE, SSD_GROUPS, SSD_WIDTH
    cl = math.gcd(seq, CHUNK)
    nc = seq // cl
    cch = SSD_CONV_CH
    rows = lambda b, c: (b * nc + c, 0)
    fix = lambda b, c: (0, 0)
    body = functools.partial(_ssd_prompt_body, heads=heads, hd=hd, ns=ns, groups=groups, cl=cl, width=width)
    return pl.pallas_call(
        body, grid=(batch, nc),
        in_specs=[pl.BlockSpec((cl, cch), lambda b, c: (b * nc + c, 1)),
                  pl.BlockSpec((cl, width), lambda b, c: (b * nc + c, 1)),
                  pl.BlockSpec((cl, gd.shape[1]), rows),
                  pl.BlockSpec((1, 1, heads, cl), lambda b, c: (b, c, 0, 0)),
                  pl.BlockSpec((SSD_CONV, cch), fix), pl.BlockSpec((1, cch), fix),
                  pl.BlockSpec((1, heads), fix), pl.BlockSpec((heads, 1), fix),
                  pl.BlockSpec((1, heads), fix), pl.BlockSpec((heads, 1), fix),
                  pl.BlockSpec((1, heads), fix), pl.BlockSpec((1, width), fix)],
        out_specs=[pl.BlockSpec((cl, width), rows),
                   pl.BlockSpec((1, heads, hd, ns), lambda b, c: (b, 0, 0, 0))],
        out_shape=[jax.ShapeDtypeStruct((batch * seq, width), bf16),
                   jax.ShapeDtypeStruct((batch, heads, hd, ns), f32)],
        scratch_shapes=[pltpu.VMEM((heads, hd, ns), f32), pltpu.VMEM((cl + 8, cch), f32),
                        pltpu.VMEM((cl, width), f32)],
        compiler_params=_params("arbitrary", "arbitrary"), name="ssd_prompt",
    )(uzx, uzx, gd, dt_t, conv_w, conv_b.reshape(1, cch), dt_bias.reshape(1, heads), dt_bias.reshape(heads, 1),
      a_log.reshape(1, heads), a_log.reshape(heads, 1), d_skip.reshape(1, heads), g_ssd.reshape(1, width))


def _ssd_sample_body(x_ref, conv0_ref, gd_ref, cw_ref, cb_ref, dtb_ref, alog_ref, dskip_ref, gssd_ref, s0_ref,
                     y_ref, s_out, ys_s, *, heads, hd, ns, groups, width):
    xrow = x_ref[0]
    xbc = xrow[:, 2 * width:]
    z = xrow[:, width:2 * width]
    conv0 = conv0_ref[0]
    xc = cb_ref[...] + cw_ref[SSD_CONV - 1:SSD_CONV, :] * xbc
    for j in range(SSD_CONV - 1):
        xc = xc + cw_ref[j:j + 1, :] * conv0[j:j + 1, :]
    xc = xc * jax.nn.sigmoid(xc)
    dt = jax.nn.softplus(gd_ref[0][:, 8:8 + heads] + dtb_ref[...])
    ea = jnp.exp(dt * -jnp.exp(alog_ref[...]))
    rep = heads // groups
    for hh in range(heads):
        gi = hh // rep
        b_row = xc[:, width + gi * ns:width + (gi + 1) * ns]
        c_row = xc[:, width + (groups + gi) * ns:width + (groups + gi + 1) * ns]
        xh = xc[:, hh * hd:(hh + 1) * hd]
        xdt = xh * dt[:, hh:hh + 1]
        smat = s0_ref[0, hh]
        sc_row = lax.dot_general(jnp.broadcast_to(c_row, (8, ns)).astype(bf16), smat.astype(bf16), _NT,
                                 preferred_element_type=f32)[0:1, :]
        cb_dot = jnp.sum(c_row * b_row, axis=1, keepdims=True)
        eah = ea[:, hh:hh + 1]
        ys_s[:, hh * hd:(hh + 1) * hd] = cb_dot * xdt + eah * sc_row + dskip_ref[:, hh:hh + 1] * xh
        xdt_col = jnp.broadcast_to(xdt, (8, hd)).T[:, 0:1]
        s_out[0, hh] = eah * smat + xdt_col * b_row
    y_ref[0] = _rms(ys_s[...] * (z * jax.nn.sigmoid(z)), gssd_ref[...])


def _ssd_sample(uzx_s, conv0, gd_s, conv_w, conv_b, dt_bias, a_log, d_skip, g_ssd, s0, *, layer):
    heads, hd, ns, groups, width = SSD_HEADS, SSD_HEAD_DIM, SSD_STATE, SSD_GROUPS, SSD_WIDTH
    cch = SSD_CONV_CH
    b = uzx_s.shape[0]
    fix = lambda i: (0, 0)
    body = functools.partial(_ssd_sample_body, heads=heads, hd=hd, ns=ns, groups=groups, width=width)
    return pl.pallas_call(
        body, grid=(b,),
        in_specs=[pl.BlockSpec((1, 1, uzx_s.shape[-1]), lambda i: (i, 0, 0)),
                  pl.BlockSpec((None, 1, SSD_CONV - 1, cch), lambda i: (layer, i, 0, 0)),
                  pl.BlockSpec((1, 1, gd_s.shape[-1]), lambda i: (i, 0, 0)),
                  pl.BlockSpec((SSD_CONV, cch), fix), pl.BlockSpec((1, cch), fix),
                  pl.BlockSpec((1, heads), fix), pl.BlockSpec((1, heads), fix), pl.BlockSpec((1, heads), fix),
                  pl.BlockSpec((1, width), fix),
                  pl.BlockSpec((None, 1, heads, hd, ns), lambda i: (layer, i, 0, 0, 0))],
        out_specs=[pl.BlockSpec((1, 1, width), lambda i: (i, 0, 0)),
                   pl.BlockSpec((1, heads, hd, ns), lambda i: (i, 0, 0, 0))],
        out_shape=[jax.ShapeDtypeStruct((b, 1, width), f32), jax.ShapeDtypeStruct((b, heads, hd, ns), f32)],
        scratch_shapes=[pltpu.VMEM((1, width), f32)],
        compiler_params=_params("parallel"), name="ssd_sample",
    )(uzx_s, conv0, gd_s, conv_w, conv_b.reshape(1, cch), dt_bias.reshape(1, heads), a_log.reshape(1, heads),
      d_skip.reshape(1, heads), g_ssd.reshape(1, width), s0)


def _router_body(h_ref, g_ref, wr_ref, br_ref, cf_ref, idx_ref, gate_ref, *, n_exp):
    cf = _rms(h_ref[...], g_ref[...])
    cf_ref[...] = cf
    logits = jnp.dot(cf, wr_ref[...], precision=HI, preferred_element_type=f32) + br_ref[...]
    lane = lax.broadcasted_iota(jnp.int32, logits.shape, 1)
    m1 = jnp.max(logits, axis=1, keepdims=True)
    i1 = jnp.min(jnp.where(logits == m1, lane, n_exp), axis=1, keepdims=True)
    rest = jnp.where(lane == i1, -jnp.inf, logits)
    m2 = jnp.max(rest, axis=1, keepdims=True)
    i2 = jnp.min(jnp.where(rest == m2, lane, n_exp), axis=1, keepdims=True)
    e2 = jnp.exp(m2 - m1)
    g1 = 1.0 / (1.0 + e2)
    two = lax.broadcasted_iota(jnp.int32, (logits.shape[0], TOP_K), 1)
    idx_ref[...] = jnp.where(two == 0, i1, i2)
    gate_ref[...] = jnp.where(two == 0, g1, e2 * g1)


def _router(h, g_ffn, w_router, b_router, *, tm):
    m, d = h.shape
    e = w_router.shape[-1]
    row = lambda i: (i, 0)
    fix = lambda i: (0, 0)
    return pl.pallas_call(
        functools.partial(_router_body, n_exp=e), grid=(m // tm,),
        in_specs=[pl.BlockSpec((tm, d), row), pl.BlockSpec((1, d), fix), pl.BlockSpec((d, e), fix),
                  pl.BlockSpec((1, e), fix)],
        out_specs=[pl.BlockSpec((tm, d), row), pl.BlockSpec((tm, TOP_K), row),
                   pl.BlockSpec((tm, TOP_K), row)],
        out_shape=[jax.ShapeDtypeStruct((m, d), f32),
                   jax.ShapeDtypeStruct((m, TOP_K), jnp.int32), jax.ShapeDtypeStruct((m, TOP_K), f32)],
        compiler_params=_params("parallel"), name="router",
    )(h, g_ffn.reshape(1, d), w_router, b_router.reshape(1, e))


def _row_copy(src_hbm, dst, sem, src_row, dst_row):
    return pltpu.make_async_copy(src_hbm.at[pl.ds(src_row, 1)], dst.at[pl.ds(dst_row, 1)], sem)


def _gather_body(idx_ref, x_hbm, o_ref, buf, sem, *, tg):
    base = pl.program_id(0) * tg

    def start(r, carry):
        _row_copy(x_hbm, buf, sem, idx_ref[base + r], r).start()
        return carry

    def wait(r, carry):
        _row_copy(x_hbm, buf, sem, 0, r).wait()
        return carry

    lax.fori_loop(0, tg, start, 0)
    lax.fori_loop(0, tg, wait, 0)
    o_ref[...] = buf[...].astype(o_ref.dtype)


def _gather_rows(x, row_idx, *, tg, out_dtype):
    r = row_idx.shape[0]
    d = x.shape[1]
    return pl.pallas_call(
        functools.partial(_gather_body, tg=tg),
        grid_spec=pltpu.PrefetchScalarGridSpec(
            num_scalar_prefetch=1, grid=(r // tg,),
            in_specs=[pl.BlockSpec(memory_space=pl.ANY)],
            out_specs=pl.BlockSpec((tg, d), lambda i, idx: (i, 0)),
            scratch_shapes=[pltpu.VMEM((tg, d), x.dtype), pltpu.SemaphoreType.DMA(())]),
        out_shape=jax.ShapeDtypeStruct((r, d), out_dtype),
        compiler_params=_params("arbitrary"), name="gather_rows")(row_idx, x)


def _cast_expert_weight(first_ref, t, w_ref, wb_ref):
    @pl.when(first_ref[t] == 1)
    def _():
        wb_ref[...] = w_ref[0].astype(bf16)


def _gmm_swiglu_body(te_ref, first_ref, nv_ref, a_ref, wg_ref, wu_ref, o_ref, wgb_ref, wub_ref):
    t = pl.program_id(1)

    @pl.when(t < nv_ref[0])
    def _():
        _cast_expert_weight(first_ref, t, wg_ref, wgb_ref)
        _cast_expert_weight(first_ref, t, wu_ref, wub_ref)
        a = a_ref[...]
        g = jnp.dot(a, wgb_ref[...], preferred_element_type=f32)
        u = jnp.dot(a, wub_ref[...], preferred_element_type=f32)
        o_ref[...] = (g * jax.nn.sigmoid(g) * u).astype(o_ref.dtype)

    @pl.when(t >= nv_ref[0])
    def _():
        o_ref[...] = jnp.zeros_like(o_ref)


def _gmm_down_body(te_ref, first_ref, nv_ref, a_ref, w_ref, o_ref, wb_ref):
    t = pl.program_id(1)

    @pl.when(t < nv_ref[0])
    def _():
        _cast_expert_weight(first_ref, t, w_ref, wb_ref)
        o_ref[...] = jnp.dot(a_ref[...], wb_ref[...], preferred_element_type=f32)

    @pl.when(t >= nv_ref[0])
    def _():
        o_ref[...] = jnp.zeros_like(o_ref)


def _gmm_swiglu(x_sorted, wg, wu, tile_expert, tile_first, n_valid, *, tm, tn):
    r, k = x_sorted.shape
    n = wg.shape[-1]
    wspec = pl.BlockSpec((1, k, tn), lambda j, t, te, fi, nv: (te[t], 0, j))
    return pl.pallas_call(
        _gmm_swiglu_body,
        grid_spec=pltpu.PrefetchScalarGridSpec(
            num_scalar_prefetch=3, grid=(n // tn, r // tm),
            in_specs=[pl.BlockSpec((tm, k), lambda j, t, te, fi, nv: (t, 0)), wspec, wspec],
            out_specs=pl.BlockSpec((tm, tn), lambda j, t, te, fi, nv: (t, j)),
            scratch_shapes=[pltpu.VMEM((k, tn), bf16), pltpu.VMEM((k, tn), bf16)]),
        out_shape=jax.ShapeDtypeStruct((r, n), bf16),
        compiler_params=_params("arbitrary", "arbitrary"), name="moe_gate_up",
    )(tile_expert, tile_first, n_valid, x_sorted, wg, wu)


def _gmm_down(h_sorted, wd, tile_expert, tile_first, n_valid, *, tm, tn):
    r, k = h_sorted.shape
    n = wd.shape[-1]
    return pl.pallas_call(
        _gmm_down_body,
        grid_spec=pltpu.PrefetchScalarGridSpec(
            num_scalar_prefetch=3, grid=(n // tn, r // tm),
            in_specs=[pl.BlockSpec((tm, k), lambda j, t, te, fi, nv: (t, 0)),
                      pl.BlockSpec((1, k, tn), lambda j, t, te, fi, nv: (te[t], 0, j))],
            out_specs=pl.BlockSpec((tm, tn), lambda j, t, te, fi, nv: (t, j)),
            scratch_shapes=[pltpu.VMEM((k, tn), bf16)]),
        out_shape=jax.ShapeDtypeStruct((r, n), f32),
        compiler_params=_params("arbitrary", "arbitrary"), name="moe_down",
    )(tile_expert, tile_first, n_valid, h_sorted, wd)


def _combine_body(pos_ref, y_hbm, gate_ref, r_ref, o_ref, buf, sem, *, tc):
    base = pl.program_id(0) * tc

    def start(r, carry):
        for kk in range(TOP_K):
            _row_copy(y_hbm, buf.at[kk], sem, pos_ref[(base + r) * TOP_K + kk], r).start()
        return carry

    def wait(r, carry):
        for kk in range(TOP_K):
            _row_copy(y_hbm, buf.at[kk], sem, 0, r).wait()
        return carry

    lax.fori_loop(0, tc, start, 0)
    lax.fori_loop(0, tc, wait, 0)
    gate = gate_ref[...]
    o_ref[...] = r_ref[...] + gate[:, 0:1] * buf[0] + gate[:, 1:2] * buf[1]


def _combine(y_sorted, pos_flat, gates, resid, *, tc):
    m, d = resid.shape
    return pl.pallas_call(
        functools.partial(_combine_body, tc=tc),
        grid_spec=pltpu.PrefetchScalarGridSpec(
            num_scalar_prefetch=1, grid=(m // tc,),
            in_specs=[pl.BlockSpec(memory_space=pl.ANY),
                      pl.BlockSpec((tc, TOP_K), lambda i, pos: (i, 0)),
                      pl.BlockSpec((tc, d), lambda i, pos: (i, 0))],
            out_specs=pl.BlockSpec((tc, d), lambda i, pos: (i, 0)),
            scratch_shapes=[pltpu.VMEM((TOP_K, tc, d), f32), pltpu.SemaphoreType.DMA(())]),
        out_shape=jax.ShapeDtypeStruct((m, d), f32),
        compiler_params=_params("arbitrary"), name="moe_combine")(pos_flat, y_sorted, gates, resid)


def _routing_tables(top_i, n_exp, tm, n_tiles):
    m = top_i.shape[0]
    e_flat = top_i.reshape(-1)
    onehot = (e_flat[:, None] == jnp.arange(n_exp, dtype=jnp.int32)[None, :]).astype(jnp.int32)
    rank = jnp.take_along_axis(jnp.cumsum(onehot, axis=0), e_flat[:, None], axis=1)[:, 0] - 1
    counts = jnp.sum(onehot, axis=0)
    tiles_per = (counts + tm - 1) // tm
    tile_end = jnp.cumsum(tiles_per)
    tile_start = tile_end - tiles_per
    pos = tile_start[e_flat] * tm + rank
    token = jnp.arange(m * TOP_K, dtype=jnp.int32) // TOP_K
    row_token = jnp.zeros((n_tiles * tm,), jnp.int32).at[pos].set(token)
    n_valid = tile_end[-1]
    tid = jnp.minimum(jnp.arange(n_tiles, dtype=jnp.int32), n_valid - 1)
    tile_expert = jnp.sum((tid[:, None] >= tile_end[None, :]).astype(jnp.int32), axis=1)
    tile_first = jnp.concatenate([jnp.ones((1,), jnp.int32),
                                  (tile_expert[1:] != tile_expert[:-1]).astype(jnp.int32)])
    return (pos.astype(jnp.int32), row_token, tile_expert.astype(jnp.int32), tile_first,
            n_valid.reshape(1).astype(jnp.int32))


def _moe_ffn(h, g_ffn, w_router, b_router, wg, wu, wd, *, tm_tok, tm, tn_up, tn_down):
    m = h.shape[0]
    n_exp = wg.shape[0]
    c_f32, top_i, top_g = _router(h, g_ffn, w_router, b_router, tm=tm_tok)
    n_tiles = (m * TOP_K) // tm + n_exp
    pos, row_token, tile_expert, tile_first, n_valid = _routing_tables(top_i, n_exp, tm, n_tiles)
    x_sorted = _gather_rows(c_f32, row_token, tg=tm, out_dtype=bf16)
    h_sorted = _gmm_swiglu(x_sorted, wg, wu, tile_expert, tile_first, n_valid, tm=tm, tn=tn_up)
    y_sorted = _gmm_down(h_sorted, wd, tile_expert, tile_first, n_valid, tm=tm, tn=tn_down)
    return _combine(y_sorted, pos, top_g, h, tc=tm_tok // 2)


def kernel(x_prompt, x_sample, state_mlstm_C, state_mlstm_n, state_mlstm_m, state_s5_re, state_s5_im, state_ssd, cache_conv, p_prompt, p_sample, g_mix, w_in, b_igate, b_fgate, g_ml, s5_lam_re, s5_lam_im, s5_log_dt, s5_b_re, s5_b_im, s5_c_re, s5_c_im, s5_d, s5_w_glu, s5_b_glu, g_s5, ssd_conv_w, ssd_conv_b, ssd_dt_bias, ssd_a_log, ssd_d, g_ssd, w_out, g_ffn, ffn_w_gate, ffn_w_up, ffn_w_down, w_router, b_router, moe_w_gate, moe_w_up, moe_w_down, g_ple, w_ple, w_ple_gate, g_final):
    bp, seq, d = x_prompt.shape
    bs = x_sample.shape[0]
    tp = bp * seq
    m = tp + bs
    tm = m // 10 if (m % 10 == 0 and (m // 10) % 16 == 0) else 8
    heads = ML_HEADS
    t5 = math.gcd(seq, S5_CHUNK)
    nc5 = seq // t5
    cl = math.gcd(seq, CHUNK)
    nc = seq // cl

    h = jnp.concatenate([x_prompt.reshape(tp, d), x_sample.reshape(bs, d)], axis=0)
    p_all = jnp.concatenate([p_prompt.reshape(DEPTH, tp, -1), p_sample.reshape(DEPTH, bs, -1)], axis=1).astype(bf16)
    n0_in = state_mlstm_n.reshape(DEPTH, bs, heads, 1, ML_DK)
    m0_in = state_mlstm_m.reshape(DEPTH, bs, 1, heads)
    zeros5 = jnp.zeros((S5_GROUPS, bp, S5_STATE), f32)
    outs_p = [[] for _ in range(7)]
    outs_s = [[] for _ in range(7)]

    for i in range(DEPTH):
        a = _rmsnorm(h, g_mix[i], bf16, tm)
        qkvo = _mm(a, w_in, tm=tm, tn=512, layer=i, n_cols=_OFF_GATES, name="in_proj_qkvo")
        w_rest = w_in[i][:, _OFF_U:_OFF_DT]
        uzx = _mm(a, w_rest, tm=tm, tn=512, name="in_proj_uzx")
        w_small = jnp.concatenate([w_in[i][:, _OFF_GATES:_OFF_U], w_in[i][:, _OFF_DT:],
                                   jnp.zeros((d, 128 - 2 * heads - SSD_HEADS), f32)], axis=1)
        gd = _mm(a, w_small, tm=tm, tn=128, name="in_proj_gates")

        gates_t = jnp.transpose(gd[:tp, :2 * heads].reshape(bp, nc, cl, 2 * heads), (0, 1, 3, 2))
        bias8 = jnp.concatenate([b_igate[i], b_fgate[i]])
        h_ml_p, c_p, n_p, m_p = _mlstm_prompt(qkvo, gates_t, bias8.reshape(2 * heads, 1), g_ml[i], batch=bp, seq=seq)
        qkvo_s = qkvo[tp:].reshape(bs, 1, -1)
        gd_s = gd[tp:].reshape(bs, 1, -1)
        h_ml_s, c_s, n_s, m_s = _mlstm_sample(qkvo_s, gd_s, bias8.reshape(1, 2 * heads), g_ml[i],
                                              state_mlstm_C, n0_in, m0_in, layer=i)

        mats_p = _s5_matrices(s5_lam_re[i], s5_lam_im[i], s5_log_dt[i], s5_b_re[i], s5_b_im[i],
                              s5_c_re[i], s5_c_im[i], t5)
        mats_s = _s5_matrices(s5_lam_re[i], s5_lam_im[i], s5_log_dt[i], s5_b_re[i], s5_b_im[i],
                              s5_c_re[i], s5_c_im[i], 1)
        u_p = uzx[:tp, :S5_WIDTH].reshape(bp, nc5, t5, S5_GROUPS, S5_CH)
        u_p = jnp.transpose(u_p, (3, 1, 0, 2, 4)).reshape(S5_GROUPS, nc5 * bp, t5 * S5_CH).astype(bf16)
        y_p, s5re_p, s5im_p = _s5_scan(u_p, mats_p, zeros5, zeros5, nc=nc5, batch=bp)
        y_p = jnp.transpose(y_p.reshape(S5_GROUPS, nc5, bp, t5, S5_CH), (2, 1, 3, 0, 4)).reshape(tp, S5_WIDTH)
        u_s = jnp.transpose(uzx[tp:, :S5_WIDTH].reshape(bs, S5_GROUPS, S5_CH), (1, 0, 2)).astype(bf16)
        y_s, s5re_s, s5im_s = _s5_scan(u_s, mats_s, jnp.swapaxes(state_s5_re[i], 0, 1),
                                       jnp.swapaxes(state_s5_im[i], 0, 1), nc=1, batch=bs)
        y_s = jnp.transpose(y_s, (1, 0, 2)).reshape(bs, S5_WIDTH)
        y5 = _s5_glu(jnp.concatenate([y_p, y_s], axis=0), uzx, s5_d[i].reshape(1, S5_WIDTH), s5_w_glu,
                     s5_b_glu[i].reshape(1, S5_WIDTH), g_s5[i].reshape(1, S5_WIDTH), layer=i, tm=tm)

        dt_t = jnp.transpose(gd[:tp, 2 * heads:2 * heads + SSD_HEADS].reshape(bp, nc, cl, SSD_HEADS), (0, 1, 3, 2))
        y_ssd_p, ssd_p = _ssd_prompt(uzx, gd, dt_t, ssd_conv_w[i], ssd_conv_b[i], ssd_dt_bias[i], ssd_a_log[i],
                                     ssd_d[i], g_ssd[i], batch=bp, seq=seq)
        uzx_s = uzx[tp:].reshape(bs, 1, -1)
        y_ssd_s, ssd_s = _ssd_sample(uzx_s, cache_conv, gd_s, ssd_conv_w[i], ssd_conv_b[i], ssd_dt_bias[i],
                                     ssd_a_log[i], ssd_d[i], g_ssd[i], state_ssd, layer=i)
        xbc = uzx[:, S5_WIDTH + SSD_WIDTH:]
        conv_p = xbc[:tp].reshape(bp, seq, -1)[:, seq - (SSD_CONV - 1):]
        conv_s = jnp.concatenate([cache_conv[i][:, 1:], xbc[tp:].reshape(bs, 1, -1)], axis=1)

        for lst, s in zip(outs_p, (c_p, n_p.reshape(bp, heads, ML_DK), m_p.reshape(bp, heads),
                                   jnp.swapaxes(s5re_p, 0, 1), jnp.swapaxes(s5im_p, 0, 1), ssd_p, conv_p)):
            lst.append(s)
        for lst, s in zip(outs_s, (c_s, n_s.reshape(bs, heads, ML_DK), m_s.reshape(bs, heads),
                                   jnp.swapaxes(s5re_s, 0, 1), jnp.swapaxes(s5im_s, 0, 1), ssd_s, conv_s)):
            lst.append(s)

        mix_in = jnp.concatenate([
            jnp.concatenate([h_ml_p, y5[:tp], y_ssd_p], axis=1),
            jnp.concatenate([h_ml_s.reshape(bs, -1).astype(bf16), y5[tp:], y_ssd_s.reshape(bs, -1).astype(bf16)], axis=1),
        ], axis=0)
        h = _mm(mix_in, w_out, tm=tm, tn=512, layer=i, resid=h, name="out_proj")

        j = i // 2
        if i % 2 == 0:
            cn = _rmsnorm(h, g_ffn[i], bf16, tm)
            hid = _mm_swiglu(cn, ffn_w_gate, ffn_w_up, tm=tm, tn=512, layer=j, name="ffn_gate_up")
            h = _mm(hid, ffn_w_down, tm=tm, tn=256, layer=j, resid=h, name="ffn_down")
        else:
            n_moe = moe_w_gate.shape[0]
            sel = lambda w: w.reshape((n_moe * N_EXPERTS,) + w.shape[2:])[j * N_EXPERTS:(j + 1) * N_EXPERTS] if n_moe > 1 else w.reshape(w.shape[1:])
            h = _moe_ffn(h, g_ffn[i], w_router[j], b_router[j], sel(moe_w_gate), sel(moe_w_up), sel(moe_w_down),
                         tm_tok=tm, tm=512, tn_up=512, tn_down=256)

        e = _rmsnorm(h, g_ple[i], bf16, tm)
        h = _mm_ple(e, p_all[i], w_ple_gate, w_ple, h, tm=tm, tn=512, layer=i)

    y = _rmsnorm(h, g_final, f32, tm)
    stack = lambda lst: jnp.stack(lst)
    return ((y[:tp].reshape(bp, seq, d), y[tp:].reshape(bs, 1, d))
            + tuple(stack(l) for l in outs_p) + tuple(stack(l) for l in outs_s))
```

```python
import functools
import math

import jax
import jax.numpy as jnp
from jax import lax
from jax.experimental import pallas as pl
from jax.experimental.pallas import tpu as pltpu

f32 = jnp.float32
bf16 = jnp.bfloat16
HI = lax.Precision.HIGHEST

D_MODEL = 2048
DEPTH = 2
ML_HEADS = 4
ML_DK = 256
ML_DV = 256
ML_WIDTH = ML_HEADS * ML_DV
S5_CH = 16
S5_WIDTH = 512
S5_GROUPS = 32
S5_STATE = 64
SSD_HEAD_DIM = 64
SSD_WIDTH = 512
SSD_HEADS = 8
SSD_GROUPS = 2
SSD_STATE = 128
SSD_CONV = 4
SSD_CONV_CH = 1024
CHUNK = 64
S5_CHUNK = 32
N_EXPERTS = 8
TOP_K = 2
RMS_EPS = 1e-6

_OFF_GATES = 4 * ML_WIDTH
_OFF_U = _OFF_GATES + 2 * ML_HEADS
_OFF_DT = _OFF_U + S5_WIDTH + SSD_WIDTH + SSD_CONV_CH

_VMEM_LIMIT = 56 * 1024 * 1024
_NT = (((1,), (1,)), ((), ()))
_TN = (((0,), (0,)), ((), ()))


def _params(*sem):
    return pltpu.CompilerParams(dimension_semantics=sem, vmem_limit_bytes=_VMEM_LIMIT)


def _rms(x, g):
    return x * lax.rsqrt(jnp.mean(x * x, axis=-1, keepdims=True) + RMS_EPS) * g


def _rmsnorm_body(x_ref, g_ref, o_ref):
    o_ref[...] = _rms(x_ref[...], g_ref[...]).astype(o_ref.dtype)


def _rmsnorm(x, g, out_dtype, tm):
    m, d = x.shape
    return pl.pallas_call(
        _rmsnorm_body, grid=(m // tm,),
        in_specs=[pl.BlockSpec((tm, d), lambda i: (i, 0)), pl.BlockSpec((1, d), lambda i: (0, 0))],
        out_specs=pl.BlockSpec((tm, d), lambda i: (i, 0)),
        out_shape=jax.ShapeDtypeStruct((m, d), out_dtype),
        compiler_params=_params("parallel"), name="rmsnorm")(x, g.reshape(1, d))


def _cast_weight_once(w_ref, wb_ref):
    @pl.when(pl.program_id(1) == 0)
    def _():
        wb_ref[...] = w_ref[...].astype(bf16)


def _mm_plain_body(a_ref, w_ref, o_ref, wb_ref):
    _cast_weight_once(w_ref, wb_ref)
    o_ref[...] = jnp.dot(a_ref[...], wb_ref[...], preferred_element_type=f32).astype(o_ref.dtype)


def _mm_resid_body(a_ref, w_ref, r_ref, o_ref, wb_ref):
    _cast_weight_once(w_ref, wb_ref)
    o_ref[...] = r_ref[...] + jnp.dot(a_ref[...], wb_ref[...], preferred_element_type=f32)


def _mm_swiglu_body(a_ref, wg_ref, wu_ref, o_ref, wgb_ref, wub_ref):
    _cast_weight_once(wg_ref, wgb_ref)
    _cast_weight_once(wu_ref, wub_ref)
    a = a_ref[...]
    g = jnp.dot(a, wgb_ref[...], preferred_element_type=f32)
    u = jnp.dot(a, wub_ref[...], preferred_element_type=f32)
    o_ref[...] = (g * jax.nn.sigmoid(g) * u).astype(o_ref.dtype)


def _mm_ple_body(e_ref, p_ref, wg_ref, wp_ref, r_ref, o_ref, wgb_ref, wpb_ref):
    _cast_weight_once(wg_ref, wgb_ref)
    _cast_weight_once(wp_ref, wpb_ref)
    gate = jnp.dot(e_ref[...], wgb_ref[...], preferred_element_type=f32)
    emb = jnp.dot(p_ref[...], wpb_ref[...], preferred_element_type=f32)
    o_ref[...] = r_ref[...] + emb * jax.nn.sigmoid(gate)


def _wspec(w, layer, k, tn, col_block_off=0):
    if w.ndim == 2:
        return pl.BlockSpec((k, tn), lambda j, i: (0, j + col_block_off))
    return pl.BlockSpec((None, k, tn), lambda j, i: (layer, 0, j + col_block_off))


def _mm(a, w, *, tm, tn, layer=0, n_cols=None, col_off=0, resid=None, out_dtype=f32, name="mm"):
    m, k = a.shape
    n = n_cols if n_cols is not None else w.shape[-1]
    grid = (n // tn, m // tm)
    a_spec = pl.BlockSpec((tm, k), lambda j, i: (i, 0))
    o_spec = pl.BlockSpec((tm, tn), lambda j, i: (i, j))
    w_spec = _wspec(w, layer, k, tn, col_off // tn)
    scratch = [pltpu.VMEM((k, tn), bf16)]
    if resid is None:
        body, ins, specs = _mm_plain_body, (a, w), [a_spec, w_spec]
    else:
        body, ins, specs = _mm_resid_body, (a, w, resid), [a_spec, w_spec, o_spec]
    return pl.pallas_call(
        body, grid=grid, in_specs=specs, out_specs=o_spec,
        out_shape=jax.ShapeDtypeStruct((m, n), out_dtype), scratch_shapes=scratch,
        compiler_params=_params("arbitrary", "arbitrary"), name=name)(*ins)


def _mm_nt_body(a_ref, wt_ref, o_ref, wb_ref):
    @pl.when(pl.program_id(1) == 0)
    def _():
        wb_ref[...] = wt_ref[0].astype(bf16)

    o_ref[...] = lax.dot_general(a_ref[...], wb_ref[...], _NT, preferred_element_type=f32)


def _wt_rows(layer, k, n_rows, row_of):
    return pl.BlockSpec((pl.Element(1), pl.Element(n_rows), pl.Element(k)),
                        lambda *idx: (layer, row_of(*idx), 0))


def _mm_nt(a, wt, *, tm, tn, layer, row_off, n_rows, name):
    m, k = a.shape
    return pl.pallas_call(
        _mm_nt_body, grid=(n_rows // tn, m // tm),
        in_specs=[pl.BlockSpec((tm, k), lambda j, i: (i, 0)),
                  _wt_rows(layer, k, tn, lambda j, i: pl.multiple_of(row_off + tn * j, 8))],
        out_specs=pl.BlockSpec((tm, tn), lambda j, i: (i, j)),
        out_shape=jax.ShapeDtypeStruct((m, n_rows), f32), scratch_shapes=[pltpu.VMEM((tn, k), bf16)],
        compiler_params=_params("arbitrary", "arbitrary"), name=name)(a, wt)


def _mm_gates_body(a_ref, w1_ref, w2_ref, o_ref):
    a = a_ref[...]
    n1 = w1_ref.shape[1]
    o_ref[:, 0:n1] = lax.dot_general(a, w1_ref[0].astype(bf16), _NT, preferred_element_type=f32)
    o_ref[:, n1:] = lax.dot_general(a, w2_ref[0].astype(bf16), _NT, preferred_element_type=f32)


def _mm_gates(a, wt, *, tm, layer, off1, n1, off2, n2, name):
    m, k = a.shape
    return pl.pallas_call(
        _mm_gates_body, grid=(m // tm,),
        in_specs=[pl.BlockSpec((tm, k), lambda i: (i, 0)),
                  _wt_rows(layer, k, n1, lambda i: off1), _wt_rows(layer, k, n2, lambda i: off2)],
        out_specs=pl.BlockSpec((tm, n1 + n2), lambda i: (i, 0)),
        out_shape=jax.ShapeDtypeStruct((m, n1 + n2), f32),
        compiler_params=_params("parallel"), name=name)(a, wt, wt)


def _mm_swiglu(a, wg, wu, *, tm, tn, layer=0, name="mm_swiglu"):
    m, k = a.shape
    n = wg.shape[-1]
    a_spec = pl.BlockSpec((tm, k), lambda j, i: (i, 0))
    o_spec = pl.BlockSpec((tm, tn), lambda j, i: (i, j))
    return pl.pallas_call(
        _mm_swiglu_body, grid=(n // tn, m // tm),
        in_specs=[a_spec, _wspec(wg, layer, k, tn), _wspec(wu, layer, k, tn)], out_specs=o_spec,
        out_shape=jax.ShapeDtypeStruct((m, n), bf16),
        scratch_shapes=[pltpu.VMEM((k, tn), bf16), pltpu.VMEM((k, tn), bf16)],
        compiler_params=_params("arbitrary", "arbitrary"), name=name)(a, wg, wu)


def _mm_ple(e, p, w_gate, w_ple, resid, *, tm, tn, layer, name="mm_ple"):
    m, k = e.shape
    kp = p.shape[1]
    n = w_gate.shape[-1]
    o_spec = pl.BlockSpec((tm, tn), lambda j, i: (i, j))
    return pl.pallas_call(
        _mm_ple_body, grid=(n // tn, m // tm),
        in_specs=[pl.BlockSpec((tm, k), lambda j, i: (i, 0)), pl.BlockSpec((tm, kp), lambda j, i: (i, 0)),
                  _wspec(w_gate, layer, k, tn), _wspec(w_ple, layer, kp, tn), o_spec],
        out_specs=o_spec, out_shape=jax.ShapeDtypeStruct((m, n), f32),
        scratch_shapes=[pltpu.VMEM((k, tn), bf16), pltpu.VMEM((kp, tn), bf16)],
        compiler_params=_params("arbitrary", "arbitrary"), name=name)(e, p, w_gate, w_ple, resid)


def _col_from_row(row, eye):
    n = row.shape[1]
    return jnp.sum(jnp.where(eye, jnp.broadcast_to(row, (n, n)), 0.0), axis=1, keepdims=True)


def _mlstm_prompt_body(q_ref, k_ref, v_ref, o_ref, gt_ref, bias_ref, gml_ref,
                       h_ref, c_out, n_out, m_out, c_s, n_s, m_s, *, heads, dk, dv, cl):
    c = pl.program_id(1)

    @pl.when(c == 0)
    def _():
        c_s[...] = jnp.zeros_like(c_s)
        n_s[...] = jnp.zeros_like(n_s)
        m_s[...] = jnp.zeros_like(m_s)

    row = lax.broadcasted_iota(jnp.int32, (cl, cl), 0)
    col = lax.broadcasted_iota(jnp.int32, (cl, cl), 1)
    causal = col <= row
    eye = col == row
    triu = (row <= col).astype(f32)
    gt = gt_ref[0, 0] + bias_ref[...]
    lf = jax.nn.log_sigmoid(gt[heads:2 * heads])
    b_all = jnp.dot(lf, triu, precision=HI, preferred_element_type=f32)
    for hh in range(heads):
        ig = gt[hh:hh + 1]
        b_row = b_all[hh:hh + 1]
        b_col = _col_from_row(b_row, eye)
        m_prev = m_s[hh]
        d = jnp.where(causal, b_col - b_row + ig, -jnp.inf)
        inter = b_col + m_prev
        m_t = jnp.maximum(inter, jnp.max(d, axis=1, keepdims=True))
        w = jnp.exp(d - m_t)
        g = jnp.exp(inter - m_t)
        q = q_ref[:, hh * dk:(hh + 1) * dk]
        k = k_ref[:, hh * dk:(hh + 1) * dk] * (dk ** -0.5)
        vb = v_ref[:, hh * dv:(hh + 1) * dv].astype(bf16)
        qb = q.astype(bf16)
        cmat = c_s[hh]
        n_row = n_s[hh]
        s = lax.dot_general(qb, k.astype(bf16), _NT, preferred_element_type=f32) * w
        num = (jnp.dot(s.astype(bf16), vb, preferred_element_type=f32)
               + g * jnp.dot(qb, cmat.astype(bf16), preferred_element_type=f32))
        den = jnp.sum(s, axis=1, keepdims=True) + g * jnp.sum(q * n_row, axis=1, keepdims=True)
        hraw = num / jnp.maximum(jnp.abs(den), jnp.exp(-m_t))
        hn = _rms(hraw, gml_ref[hh:hh + 1, :])
        ogate = jax.nn.sigmoid(o_ref[:, hh * dv:(hh + 1) * dv])
        h_ref[:, hh * dv:(hh + 1) * dv] = (ogate * hn).astype(h_ref.dtype)
        b_last = b_row[:, cl - 1:cl]
        dl = b_last - b_row + ig
        m_new = jnp.maximum(b_last + m_prev, jnp.max(dl, axis=1, keepdims=True))
        ws_col = _col_from_row(jnp.exp(dl - m_new), eye)
        gl = jnp.exp(b_last + m_prev - m_new)
        kw = k * ws_col
        c_s[hh] = gl * cmat + lax.dot_general(kw.astype(bf16), vb, _TN, preferred_element_type=f32)
        n_s[hh] = gl * n_row + jnp.sum(kw, axis=0, keepdims=True)
        m_s[hh] = m_new

    @pl.when(c == pl.num_programs(1) - 1)
    def _():
        c_out[0] = c_s[...]
        n_out[0] = n_s[...]
        m_out[0] = m_s[...]


def _mlstm_prompt(qkvo, gates_t, bias_col, g_ml, *, batch, seq):
    heads, dk, dv, cl = ML_HEADS, ML_DK, ML_DV, math.gcd(seq, CHUNK)
    nc = seq // cl
    wq = heads * dk
    rows = lambda b, c: (b * nc + c, 0)
    body = functools.partial(_mlstm_prompt_body, heads=heads, dk=dk, dv=dv, cl=cl)
    return pl.pallas_call(
        body, grid=(batch, nc),
        in_specs=[pl.BlockSpec((cl, wq), lambda b, c: (b * nc + c, 0)),
                  pl.BlockSpec((cl, wq), lambda b, c: (b * nc + c, 1)),
                  pl.BlockSpec((cl, wq), lambda b, c: (b * nc + c, 2)),
                  pl.BlockSpec((cl, wq), lambda b, c: (b * nc + c, 3)),
                  pl.BlockSpec((1, 1, 2 * heads, cl), lambda b, c: (b, c, 0, 0)),
                  pl.BlockSpec((2 * heads, 1), lambda b, c: (0, 0)),
                  pl.BlockSpec((heads, dv), lambda b, c: (0, 0))],
        out_specs=[pl.BlockSpec((cl, heads * dv), rows),
                   pl.BlockSpec((1, heads, dk, dv), lambda b, c: (b, 0, 0, 0)),
                   pl.BlockSpec((1, heads, 1, dk), lambda b, c: (b, 0, 0, 0)),
                   pl.BlockSpec((1, heads, 1, 1), lambda b, c: (b, 0, 0, 0))],
        out_shape=[jax.ShapeDtypeStruct((batch * seq, heads * dv), bf16),
                   jax.ShapeDtypeStruct((batch, heads, dk, dv), f32),
                   jax.ShapeDtypeStruct((batch, heads, 1, dk), f32),
                   jax.ShapeDtypeStruct((batch, heads, 1, 1), f32)],
        scratch_shapes=[pltpu.VMEM((heads, dk, dv), f32), pltpu.VMEM((heads, 1, dk), f32),
                        pltpu.VMEM((heads, 1, 1), f32)],
        compiler_params=_params("arbitrary", "arbitrary"), name="mlstm_prompt",
    )(qkvo, qkvo, qkvo, qkvo, gates_t, bias_col, g_ml)


def _lane_pick(pieces, width):
    lane = lax.broadcasted_iota(jnp.int32, (1, width), 1)
    out = jnp.zeros((1, width), f32)
    for i, p in enumerate(pieces):
        out = jnp.where(lane == i, p, out)
    return out


def _mlstm_sample_body(*refs, heads, dk, dv, aliased):
    if aliased:
        refs = refs[:7] + refs[8:]
    x_ref, gd_ref, bias_ref, gml_ref, c0_ref, n0_ref, m0_ref, h_ref, c_out, n_out, m_out = refs

    @pl.when(pl.program_id(0) > 0)
    def _():
        c_out[...] = jnp.zeros_like(c_out)

    @pl.when(pl.program_id(0) == 0)
    def _():
        wq = heads * dk
        g8 = gd_ref[0][:, 0:2 * heads] + bias_ref[...]
        m_new_all = []
        for hh in range(heads):
            ig = g8[:, hh:hh + 1]
            lf = jax.nn.log_sigmoid(g8[:, heads + hh:heads + hh + 1])
            m0 = m0_ref[0][:, hh:hh + 1]
            m_t = jnp.maximum(lf + m0, ig)
            w = jnp.exp(ig - m_t)
            g = jnp.exp(lf + m0 - m_t)
            q = x_ref[0][:, hh * dk:(hh + 1) * dk]
            k = x_ref[0][:, wq + hh * dk:wq + (hh + 1) * dk] * (dk ** -0.5)
            v = x_ref[0][:, 2 * wq + hh * dv:2 * wq + (hh + 1) * dv]
            o = x_ref[0][:, 3 * wq + hh * dv:3 * wq + (hh + 1) * dv]
            cmat = c0_ref[0, hh]
            n0 = n0_ref[0, hh]
            q_col = jnp.broadcast_to(q, (8, dk)).T[:, 0:1]
            k_col = jnp.broadcast_to(k, (8, dk)).T[:, 0:1]
            s = jnp.sum(q * k, axis=1, keepdims=True) * w
            num = s * v + g * jnp.sum(q_col * cmat, axis=0, keepdims=True)
            den = s + g * jnp.sum(q * n0, axis=1, keepdims=True)
            hraw = num / jnp.maximum(jnp.abs(den), jnp.exp(-m_t))
            hn = _rms(hraw, gml_ref[hh:hh + 1, :])
            h_ref[0, :, hh * dv:(hh + 1) * dv] = jax.nn.sigmoid(o) * hn
            c_out[0, hh] = g * cmat + (k_col * w) * v
            n_out[0, hh] = g * n0 + k * w
            m_new_all.append(m_t)
        m_out[0] = _lane_pick(m_new_all, heads)


def _stacked_state_grid(layer, b, stack):
    n_l = DEPTH if stack is None else 1
    row = lambda l, i: jnp.where(l == 0, i, b - 1)
    out_layer = lambda l: (layer + l) % DEPTH
    return (n_l, b), row, out_layer


def _mlstm_sample(qkvo_s, gd_s, bias_row, g_ml, c0, n0, m0, *, layer, c_stack=None):
    heads, dk, dv = ML_HEADS, ML_DK, ML_DV
    b = qkvo_s.shape[0]
    grid, row, out_layer = _stacked_state_grid(layer, b, c_stack)
    aliased = c_stack is not None
    body = functools.partial(_mlstm_sample_body, heads=heads, dk=dk, dv=dv, aliased=aliased)
    in_specs = [pl.BlockSpec((1, 1, 4 * heads * dk), lambda l, i: (row(l, i), 0, 0)),
                pl.BlockSpec((1, 1, gd_s.shape[-1]), lambda l, i: (row(l, i), 0, 0)),
                pl.BlockSpec((1, 2 * heads), lambda l, i: (0, 0)),
                pl.BlockSpec((heads, dv), lambda l, i: (0, 0)),
                pl.BlockSpec((None, 1, heads, dk, dv), lambda l, i: (layer, row(l, i), 0, 0, 0)),
                pl.BlockSpec((None, 1, heads, 1, dk), lambda l, i: (layer, row(l, i), 0, 0, 0)),
                pl.BlockSpec((None, 1, 1, heads), lambda l, i: (layer, row(l, i), 0, 0))]
    args = [qkvo_s, gd_s, bias_row, g_ml, c0, n0, m0]
    if aliased:
        in_specs.append(pl.BlockSpec(memory_space=pl.ANY))
        args.append(c_stack)
    return pl.pallas_call(
        body, grid=grid, in_specs=in_specs,
        out_specs=[pl.BlockSpec((1, 1, heads * dv), lambda l, i: (row(l, i), 0, 0)),
                   pl.BlockSpec((None, 1, heads, dk, dv), lambda l, i: (out_layer(l), i, 0, 0, 0)),
                   pl.BlockSpec((1, heads, 1, dk), lambda l, i: (row(l, i), 0, 0, 0)),
                   pl.BlockSpec((1, 1, heads), lambda l, i: (row(l, i), 0, 0))],
        out_shape=[jax.ShapeDtypeStruct((b, 1, heads * dv), f32),
                   jax.ShapeDtypeStruct((DEPTH, b, heads, dk, dv), f32),
                   jax.ShapeDtypeStruct((b, heads, 1, dk), f32),
                   jax.ShapeDtypeStruct((b, 1, heads), f32)],
        input_output_aliases={7: 1} if aliased else {},
        compiler_params=_params("arbitrary", "arbitrary"), name="mlstm_sample",
    )(*args)


def _s5_body(u_ref, m_ref, wre_ref, wim_ref, vre_ref, vim_ref, lre_ref, lim_ref, x0re_ref, x0im_ref,
             y_ref, xre_out, xim_out, xs_re, xs_im, *, nc, batch):
    u = u_ref[0]
    xin_re = jnp.dot(u, wre_ref[0], preferred_element_type=f32)
    xin_im = jnp.dot(u, wim_ref[0], preferred_element_type=f32)
    lre = lre_ref[0]
    lim = lim_ref[0]
    xr = x0re_ref[0]
    xi = x0im_ref[0]
    for k in range(nc):
        sl = slice(k * batch, (k + 1) * batch)
        xs_re[sl, :] = xr
        xs_im[sl, :] = xi
        xr, xi = (lre * xr - lim * xi + xin_re[sl, :], lre * xi + lim * xr + xin_im[sl, :])
    xre_out[0] = xr
    xim_out[0] = xi
    y_ref[0] = (jnp.dot(u, m_ref[0], preferred_element_type=f32)
                + jnp.dot(xs_re[...].astype(bf16), vre_ref[0], preferred_element_type=f32)
                + jnp.dot(xs_im[...].astype(bf16), vim_ref[0], preferred_element_type=f32))


def _s5_scan(u_g, mats, x0re, x0im, *, nc, batch):
    m_mat, w_re, w_im, v_re, v_im, l_re, l_im = mats
    g, rows, tc = u_g.shape
    p = w_re.shape[-1]
    blk = lambda *s: pl.BlockSpec((1,) + s, lambda i: (i,) + (0,) * len(s))
    body = functools.partial(_s5_body, nc=nc, batch=batch)
    return pl.pallas_call(
        body, grid=(g,),
        in_specs=[blk(rows, tc), blk(tc, tc), blk(tc, p), blk(tc, p), blk(p, tc), blk(p, tc),
                  blk(1, p), blk(1, p), blk(batch, p), blk(batch, p)],
        out_specs=[blk(rows, tc), blk(batch, p), blk(batch, p)],
        out_shape=[jax.ShapeDtypeStruct((g, rows, tc), f32),
                   jax.ShapeDtypeStruct((g, batch, p), f32), jax.ShapeDtypeStruct((g, batch, p), f32)],
        scratch_shapes=[pltpu.VMEM((rows, p), f32), pltpu.VMEM((rows, p), f32)],
        compiler_params=_params("parallel"), name=f"s5_scan_t{tc // S5_CH}",
    )(u_g, m_mat, w_re, w_im, v_re, v_im, l_re, l_im, x0re, x0im)


def _s5_matrices(lam_re, lam_im, log_dt, b_re, b_im, c_re, c_im, t):
    g, p = lam_re.shape
    ch = b_re.shape[-1]
    dt = jnp.exp(log_dt)[:, None]
    ar, ai = lam_re * dt, lam_im * dt

    def powers(tau):
        mag = jnp.exp(ar[:, None, :] * tau[None, :, None])
        ang = ai[:, None, :] * tau[None, :, None]
        return mag * jnp.cos(ang), mag * jnp.sin(ang)

    lbr, lbi = jnp.exp(ar) * jnp.cos(ai), jnp.exp(ar) * jnp.sin(ai)
    den = lam_re * lam_re + lam_im * lam_im
    fr = ((lbr - 1.0) * lam_re + lbi * lam_im) / den
    fi = (lbi * lam_re - (lbr - 1.0) * lam_im) / den
    bbr = jnp.swapaxes(fr[..., None] * b_re - fi[..., None] * b_im, 1, 2)
    bbi = jnp.swapaxes(fr[..., None] * b_im + fi[..., None] * b_re, 1, 2)
    cbr = c_re[:, :, None, :] * bbr[:, None, :, :] - c_im[:, :, None, :] * bbi[:, None, :, :]
    cbi = c_re[:, :, None, :] * bbi[:, None, :, :] + c_im[:, :, None, :] * bbr[:, None, :, :]
    steps = jnp.arange(t, dtype=f32)
    diff = steps[None, :] - steps[:, None]
    lag = jnp.maximum(diff, 0.0)
    mag = jnp.exp(ar[:, None, None, :] * lag[None, :, :, None])
    ang = ai[:, None, None, :] * lag[None, :, :, None]
    keep = (diff >= 0.0)[None, :, :, None]
    e2 = jnp.concatenate([jnp.where(keep, mag * jnp.cos(ang), 0.0),
                          jnp.where(keep, mag * jnp.sin(ang), 0.0)], axis=-1)
    cb2 = jnp.concatenate([cbr, -cbi], axis=-1)
    m_mat = jnp.einsum("gstp,gcdp->gsdtc", e2, cb2, precision=HI).reshape(g, t * ch, t * ch)
    pr, pi = powers(t - 1.0 - steps)
    w_re = (pr[:, :, None, :] * bbr[:, None, :, :] - pi[:, :, None, :] * bbi[:, None, :, :]).reshape(g, t * ch, p)
    w_im = (pr[:, :, None, :] * bbi[:, None, :, :] + pi[:, :, None, :] * bbr[:, None, :, :]).reshape(g, t * ch, p)
    qr, qi = powers(steps + 1.0)
    qr, qi = jnp.swapaxes(qr, 1, 2)[..., None], jnp.swapaxes(qi, 1, 2)[..., None]
    ctr, cti = jnp.swapaxes(c_re, 1, 2)[:, :, None, :], jnp.swapaxes(c_im, 1, 2)[:, :, None, :]
    v_re = (ctr * qr - cti * qi).reshape(g, p, t * ch)
    v_im = -(ctr * qi + cti * qr).reshape(g, p, t * ch)
    ltr, lti = powers(jnp.full((1,), float(t), f32))
    return (m_mat.astype(bf16), w_re.astype(bf16), w_im.astype(bf16), v_re.astype(bf16), v_im.astype(bf16),
            ltr, lti)


def _s5_glu_body(y_ref, u_ref, d_ref, w_ref, b_ref, g_ref, o_ref):
    y5 = jax.nn.gelu(y_ref[...] + d_ref[...] * u_ref[...])
    gate = jax.nn.sigmoid(jnp.dot(y5.astype(bf16), w_ref[...].astype(bf16), preferred_element_type=f32) + b_ref[...])
    o_ref[...] = _rms(y5 * gate, g_ref[...]).astype(o_ref.dtype)


def _s5_glu(y_raw, uzx, d_skip, w_glu, b_glu, g_s5, *, layer, tm):
    m, wdt = y_raw.shape
    row = lambda i: (i, 0)
    fix = lambda i: (0, 0)
    return pl.pallas_call(
        _s5_glu_body, grid=(m // tm,),
        in_specs=[pl.BlockSpec((tm, wdt), row), pl.BlockSpec((tm, wdt), row), pl.BlockSpec((1, wdt), fix),
                  pl.BlockSpec((None, wdt, wdt), lambda i: (layer, 0, 0)), pl.BlockSpec((1, wdt), fix),
                  pl.BlockSpec((1, wdt), fix)],
        out_specs=pl.BlockSpec((tm, wdt), row), out_shape=jax.ShapeDtypeStruct((m, wdt), bf16),
        compiler_params=_params("parallel"), name="s5_glu",
    )(y_raw, uzx, d_skip, w_glu, b_glu, g_s5)


def _ssd_prompt_body(xbc_ref, z_ref, gd_ref, dtt_ref, cw_ref, cb_ref, dtb_row, dtb_col, alog_row, alog_col,
                     dskip_ref, gssd_ref, y_ref, s_out, s_s, xp_s, ys_s, *, heads, hd, ns, groups, cl, width):
    c = pl.program_id(1)

    @pl.when(c == 0)
    def _():
        s_s[...] = jnp.zeros_like(s_s)
        xp_s[0:8, :] = jnp.zeros((8, xp_s.shape[1]), f32)

    xp_s[8:8 + cl, :] = xbc_ref[...]
    xc = cb_ref[...] + sum(cw_ref[j:j + 1, :] * xp_s[5 + j:5 + j + cl, :] for j in range(SSD_CONV))
    xp_s[0:8, :] = xp_s[cl:cl + 8, :]
    xc = xc * jax.nn.sigmoid(xc)
    row = lax.broadcasted_iota(jnp.int32, (cl, cl), 0)
    col = lax.broadcasted_iota(jnp.int32, (cl, cl), 1)
    causal = col <= row
    tril = causal.astype(f32)
    triu = (row <= col).astype(f32)
    dt_col = jax.nn.softplus(gd_ref[:, 8:8 + heads] + dtb_row[...])
    dt_row = jax.nn.softplus(dtt_ref[0, 0] + dtb_col[...])
    cum_col = jnp.dot(tril, dt_col * -jnp.exp(alog_row[...]), precision=HI, preferred_element_type=f32)
    cum_row = jnp.dot(dt_row * -jnp.exp(alog_col[...]), triu, precision=HI, preferred_element_type=f32)
    rep = heads // groups
    for gi in range(groups):
        bm = xc[:, width + gi * ns:width + (gi + 1) * ns].astype(bf16)
        cm = xc[:, width + (groups + gi) * ns:width + (groups + gi + 1) * ns].astype(bf16)
        scores = lax.dot_general(cm, bm, _NT, preferred_element_type=f32)
        for hh in range(gi * rep, (gi + 1) * rep):
            cc = cum_col[:, hh:hh + 1]
            cr = cum_row[hh:hh + 1, :]
            seg = jnp.exp(jnp.where(causal, cc - cr, -jnp.inf))
            xh = xc[:, hh * hd:(hh + 1) * hd]
            xdt = xh * dt_col[:, hh:hh + 1]
            smat = s_s[hh]
            y = (jnp.dot((scores * seg).astype(bf16), xdt.astype(bf16), preferred_element_type=f32)
                 + jnp.exp(cc) * lax.dot_general(cm, smat.astype(bf16), _NT, preferred_element_type=f32))
            c_last = cr[:, cl - 1:cl]
            xw = (xdt * jnp.exp(c_last - cc)).astype(bf16)
            s_s[hh] = jnp.exp(c_last) * smat + lax.dot_general(xw, bm, _TN, preferred_element_type=f32)
            ys_s[:, hh * hd:(hh + 1) * hd] = y + dskip_ref[:, hh:hh + 1] * xh
    z = z_ref[...]
    y_ref[...] = _rms(ys_s[...] * (z * jax.nn.sigmoid(z)), gssd_ref[...]).astype(y_ref.dtype)

    @pl.when(c == pl.num_programs(1) - 1)
    def _():
        s_out[0] = s_s[...]


def _ssd_prompt(uzx, gd, dt_t, conv_w, conv_b, dt_bias, a_log, d_skip, g_ssd, *, batch, seq):
    heads, hd, ns, groups, width = SSD_HEADS, SSD_HEAD_DIM, SSD_STATE, SSD_GROUPS, SSD_WIDTH
    cl = math.gcd(seq, CHUNK)
    nc = seq // cl
    cch = SSD_CONV_CH
    rows = lambda b, c: (b * nc + c, 0)
    fix = lambda b, c: (0, 0)
    body = functools.partial(_ssd_prompt_body, heads=heads, hd=hd, ns=ns, groups=groups, cl=cl, width=width)
    return pl.pallas_call(
        body, grid=(batch, nc),
        in_specs=[pl.BlockSpec((cl, cch), lambda b, c: (b * nc + c, 1)),
                  pl.BlockSpec((cl, width), lambda b, c: (b * nc + c, 1)),
                  pl.BlockSpec((cl, gd.shape[1]), rows),
                  pl.BlockSpec((1, 1, heads, cl), lambda b, c: (b, c, 0, 0)),
                  pl.BlockSpec((SSD_CONV, cch), fix), pl.BlockSpec((1, cch), fix),
                  pl.BlockSpec((1, heads), fix), pl.BlockSpec((heads, 1), fix),
                  pl.BlockSpec((1, heads), fix), pl.BlockSpec((heads, 1), fix),
                  pl.BlockSpec((1, heads), fix), pl.BlockSpec((1, width), fix)],
        out_specs=[pl.BlockSpec((cl, width), rows),
                   pl.BlockSpec((1, heads, hd, ns), lambda b, c: (b, 0, 0, 0))],
        out_shape=[jax.ShapeDtypeStruct((batch * seq, width), bf16),
                   jax.ShapeDtypeStruct((batch, heads, hd, ns), f32)],
        scratch_shapes=[pltpu.VMEM((heads, hd, ns), f32), pltpu.VMEM((cl + 8, cch), f32),
                        pltpu.VMEM((cl, width), f32)],
        compiler_params=_params("arbitrary", "arbitrary"), name="ssd_prompt",
    )(uzx, uzx, gd, dt_t, conv_w, conv_b.reshape(1, cch), dt_bias.reshape(1, heads), dt_bias.reshape(heads, 1),
      a_log.reshape(1, heads), a_log.reshape(heads, 1), d_skip.reshape(1, heads), g_ssd.reshape(1, width))


def _ssd_sample_body(*refs, heads, hd, ns, groups, width, aliased):
    if aliased:
        refs = refs[:10] + refs[11:]
    (x_ref, conv0_ref, gd_ref, cw_ref, cb_ref, dtb_ref, alog_ref, dskip_ref, gssd_ref, s0_ref,
     y_ref, s_out, ys_s) = refs

    @pl.when(pl.program_id(0) > 0)
    def _():
        s_out[...] = jnp.zeros_like(s_out)

    @pl.when(pl.program_id(0) == 0)
    def _():
        xrow = x_ref[0]
        xbc = xrow[:, 2 * width:]
        z = xrow[:, width:2 * width]
        conv0 = conv0_ref[0]
        xc = cb_ref[...] + cw_ref[SSD_CONV - 1:SSD_CONV, :] * xbc
        for j in range(SSD_CONV - 1):
            xc = xc + cw_ref[j:j + 1, :] * conv0[j:j + 1, :]
        xc = xc * jax.nn.sigmoid(xc)
        dt = jax.nn.softplus(gd_ref[0][:, 8:8 + heads] + dtb_ref[...])
        ea = jnp.exp(dt * -jnp.exp(alog_ref[...]))
        rep = heads // groups
        for hh in range(heads):
            gi = hh // rep
            b_row = xc[:, width + gi * ns:width + (gi + 1) * ns]
            c_row = xc[:, width + (groups + gi) * ns:width + (groups + gi + 1) * ns]
            xh = xc[:, hh * hd:(hh + 1) * hd]
            xdt = xh * dt[:, hh:hh + 1]
            smat = s0_ref[0, hh]
            sc_row = lax.dot_general(jnp.broadcast_to(c_row, (8, ns)).astype(bf16), smat.astype(bf16), _NT,
                                     preferred_element_type=f32)[0:1, :]
            cb_dot = jnp.sum(c_row * b_row, axis=1, keepdims=True)
            eah = ea[:, hh:hh + 1]
            ys_s[:, hh * hd:(hh + 1) * hd] = cb_dot * xdt + eah * sc_row + dskip_ref[:, hh:hh + 1] * xh
            xdt_col = jnp.broadcast_to(xdt, (8, hd)).T[:, 0:1]
            s_out[0, hh] = eah * smat + xdt_col * b_row
        y_ref[0] = _rms(ys_s[...] * (z * jax.nn.sigmoid(z)), gssd_ref[...])


def _ssd_sample(uzx_s, conv0, gd_s, conv_w, conv_b, dt_bias, a_log, d_skip, g_ssd, s0, *, layer, s_stack=None):
    heads, hd, ns, groups, width = SSD_HEADS, SSD_HEAD_DIM, SSD_STATE, SSD_GROUPS, SSD_WIDTH
    cch = SSD_CONV_CH
    b = uzx_s.shape[0]
    fix = lambda l, i: (0, 0)
    grid, row, out_layer = _stacked_state_grid(layer, b, s_stack)
    aliased = s_stack is not None
    body = functools.partial(_ssd_sample_body, heads=heads, hd=hd, ns=ns, groups=groups, width=width,
                             aliased=aliased)
    in_specs = [pl.BlockSpec((1, 1, uzx_s.shape[-1]), lambda l, i: (row(l, i), 0, 0)),
                pl.BlockSpec((None, 1, SSD_CONV - 1, cch), lambda l, i: (layer, row(l, i), 0, 0)),
                pl.BlockSpec((1, 1, gd_s.shape[-1]), lambda l, i: (row(l, i), 0, 0)),
                pl.BlockSpec((SSD_CONV, cch), fix), pl.BlockSpec((1, cch), fix),
                pl.BlockSpec((1, heads), fix), pl.BlockSpec((1, heads), fix), pl.BlockSpec((1, heads), fix),
                pl.BlockSpec((1, width), fix),
                pl.BlockSpec((None, 1, heads, hd, ns), lambda l, i: (layer, row(l, i), 0, 0, 0))]
    args = [uzx_s, conv0, gd_s, conv_w, conv_b.reshape(1, cch), dt_bias.reshape(1, heads), a_log.reshape(1, heads),
            d_skip.reshape(1, heads), g_ssd.reshape(1, width), s0]
    if aliased:
        in_specs.append(pl.BlockSpec(memory_space=pl.ANY))
        args.append(s_stack)
    return pl.pallas_call(
        body, grid=grid, in_specs=in_specs,
        out_specs=[pl.BlockSpec((1, 1, width), lambda l, i: (row(l, i), 0, 0)),
                   pl.BlockSpec((None, 1, heads, hd, ns), lambda l, i: (out_layer(l), i, 0, 0, 0))],
        out_shape=[jax.ShapeDtypeStruct((b, 1, width), f32),
                   jax.ShapeDtypeStruct((DEPTH, b, heads, hd, ns), f32)],
        scratch_shapes=[pltpu.VMEM((1, width), f32)],
        input_output_aliases={10: 1} if aliased else {},
        compiler_params=_params("arbitrary", "arbitrary"), name="ssd_sample",
    )(*args)


def _router_body(h_ref, g_ref, wr_ref, br_ref, cf_ref, idx_ref, gate_ref, *, n_exp):
    cf = _rms(h_ref[...], g_ref[...])
    cf_ref[...] = cf
    logits = jnp.dot(cf, wr_ref[...], precision=HI, preferred_element_type=f32) + br_ref[...]
    lane = lax.broadcasted_iota(jnp.int32, logits.shape, 1)
    m1 = jnp.max(logits, axis=1, keepdims=True)
    i1 = jnp.min(jnp.where(logits == m1, lane, n_exp), axis=1, keepdims=True)
    rest = jnp.where(lane == i1, -jnp.inf, logits)
    m2 = jnp.max(rest, axis=1, keepdims=True)
    i2 = jnp.min(jnp.where(rest == m2, lane, n_exp), axis=1, keepdims=True)
    e2 = jnp.exp(m2 - m1)
    g1 = 1.0 / (1.0 + e2)
    two = lax.broadcasted_iota(jnp.int32, (logits.shape[0], TOP_K), 1)
    idx_ref[...] = jnp.where(two == 0, i1, i2)
    gate_ref[...] = jnp.where(two == 0, g1, e2 * g1)


def _router(h, g_ffn, w_router, b_router, *, tm):
    m, d = h.shape
    e = w_router.shape[-1]
    row = lambda i: (i, 0)
    fix = lambda i: (0, 0)
    return pl.pallas_call(
        functools.partial(_router_body, n_exp=e), grid=(m // tm,),
        in_specs=[pl.BlockSpec((tm, d), row), pl.BlockSpec((1, d), fix), pl.BlockSpec((d, e), fix),
                  pl.BlockSpec((1, e), fix)],
        out_specs=[pl.BlockSpec((tm, d), row), pl.BlockSpec((tm, TOP_K), row),
                   pl.BlockSpec((tm, TOP_K), row)],
        out_shape=[jax.ShapeDtypeStruct((m, d), f32),
                   jax.ShapeDtypeStruct((m, TOP_K), jnp.int32), jax.ShapeDtypeStruct((m, TOP_K), f32)],
        compiler_params=_params("parallel"), name="router",
    )(h, g_ffn.reshape(1, d), w_router, b_router.reshape(1, e))


def _row_copy(src_hbm, dst, sem, src_row, dst_row):
    return pltpu.make_async_copy(src_hbm.at[pl.ds(src_row, 1)], dst.at[pl.ds(dst_row, 1)], sem)


_DMA_UNROLL = 8


def _gather_body(idx_ref, x_hbm, o_ref, buf, sem, *, tg):
    i = pl.program_id(0)

    def issue(tile, slot):
        base = tile * tg

        def start(r, carry):
            _row_copy(x_hbm, buf.at[slot], sem.at[slot], idx_ref[base + r], r).start()
            return carry

        lax.fori_loop(0, tg, start, 0, unroll=_DMA_UNROLL)

    @pl.when(i == 0)
    def _():
        issue(0, 0)

    @pl.when(i + 1 < pl.num_programs(0))
    def _():
        issue(i + 1, (i + 1) % 2)

    slot = i % 2

    def wait(r, carry):
        _row_copy(x_hbm, buf.at[slot], sem.at[slot], 0, r).wait()
        return carry

    lax.fori_loop(0, tg, wait, 0, unroll=_DMA_UNROLL)
    o_ref[...] = buf[slot].astype(o_ref.dtype)


def _gather_rows(x, row_idx, *, tg, out_dtype):
    r = row_idx.shape[0]
    d = x.shape[1]
    return pl.pallas_call(
        functools.partial(_gather_body, tg=tg),
        grid_spec=pltpu.PrefetchScalarGridSpec(
            num_scalar_prefetch=1, grid=(r // tg,),
            in_specs=[pl.BlockSpec(memory_space=pl.ANY)],
            out_specs=pl.BlockSpec((tg, d), lambda i, idx: (i, 0)),
            scratch_shapes=[pltpu.VMEM((2, tg, d), x.dtype), pltpu.SemaphoreType.DMA((2,))]),
        out_shape=jax.ShapeDtypeStruct((r, d), out_dtype),
        compiler_params=_params("arbitrary"), name="gather_rows")(row_idx, x)


def _cast_expert_weight(first_ref, t, w_ref, wb_ref):
    @pl.when(first_ref[t] == 1)
    def _():
        wb_ref[...] = w_ref[0].astype(bf16)


def _gmm_swiglu_body(te_ref, first_ref, nv_ref, a_ref, wg_ref, wu_ref, o_ref, wgb_ref, wub_ref):
    t = pl.program_id(1)

    @pl.when(t < nv_ref[0])
    def _():
        _cast_expert_weight(first_ref, t, wg_ref, wgb_ref)
        _cast_expert_weight(first_ref, t, wu_ref, wub_ref)
        a = a_ref[...]
        g = jnp.dot(a, wgb_ref[...], preferred_element_type=f32)
        u = jnp.dot(a, wub_ref[...], preferred_element_type=f32)
        o_ref[...] = (g * jax.nn.sigmoid(g) * u).astype(o_ref.dtype)

    @pl.when(t >= nv_ref[0])
    def _():
        o_ref[...] = jnp.zeros_like(o_ref)


def _gmm_down_body(te_ref, first_ref, nv_ref, a_ref, w_ref, o_ref, wb_ref):
    t = pl.program_id(1)

    @pl.when(t < nv_ref[0])
    def _():
        _cast_expert_weight(first_ref, t, w_ref, wb_ref)
        o_ref[...] = jnp.dot(a_ref[...], wb_ref[...], preferred_element_type=f32)

    @pl.when(t >= nv_ref[0])
    def _():
        o_ref[...] = jnp.zeros_like(o_ref)


def _gmm_swiglu(x_sorted, wg, wu, tile_expert, tile_first, n_valid, *, tm, tn):
    r, k = x_sorted.shape
    n = wg.shape[-1]
    wspec = pl.BlockSpec((1, k, tn), lambda j, t, te, fi, nv: (te[t], 0, j))
    return pl.pallas_call(
        _gmm_swiglu_body,
        grid_spec=pltpu.PrefetchScalarGridSpec(
            num_scalar_prefetch=3, grid=(n // tn, r // tm),
            in_specs=[pl.BlockSpec((tm, k), lambda j, t, te, fi, nv: (t, 0)), wspec, wspec],
            out_specs=pl.BlockSpec((tm, tn), lambda j, t, te, fi, nv: (t, j)),
            scratch_shapes=[pltpu.VMEM((k, tn), bf16), pltpu.VMEM((k, tn), bf16)]),
        out_shape=jax.ShapeDtypeStruct((r, n), bf16),
        compiler_params=_params("arbitrary", "arbitrary"), name="moe_gate_up",
    )(tile_expert, tile_first, n_valid, x_sorted, wg, wu)


def _gmm_down(h_sorted, wd, tile_expert, tile_first, n_valid, *, tm, tn):
    r, k = h_sorted.shape
    n = wd.shape[-1]
    return pl.pallas_call(
        _gmm_down_body,
        grid_spec=pltpu.PrefetchScalarGridSpec(
            num_scalar_prefetch=3, grid=(n // tn, r // tm),
            in_specs=[pl.BlockSpec((tm, k), lambda j, t, te, fi, nv: (t, 0)),
                      pl.BlockSpec((1, k, tn), lambda j, t, te, fi, nv: (te[t], 0, j))],
            out_specs=pl.BlockSpec((tm, tn), lambda j, t, te, fi, nv: (t, j)),
            scratch_shapes=[pltpu.VMEM((k, tn), bf16)]),
        out_shape=jax.ShapeDtypeStruct((r, n), f32),
        compiler_params=_params("arbitrary", "arbitrary"), name="moe_down",
    )(tile_expert, tile_first, n_valid, h_sorted, wd)


def _combine_body(pos_ref, y_hbm, gate_ref, r_ref, o_ref, buf, sem, *, tc):
    i = pl.program_id(0)

    def issue(tile, slot):
        base = tile * tc

        def start(r, carry):
            for kk in range(TOP_K):
                _row_copy(y_hbm, buf.at[slot, kk], sem.at[slot], pos_ref[(base + r) * TOP_K + kk], r).start()
            return carry

        lax.fori_loop(0, tc, start, 0, unroll=_DMA_UNROLL)

    @pl.when(i == 0)
    def _():
        issue(0, 0)

    @pl.when(i + 1 < pl.num_programs(0))
    def _():
        issue(i + 1, (i + 1) % 2)

    slot = i % 2

    def wait(r, carry):
        for kk in range(TOP_K):
            _row_copy(y_hbm, buf.at[slot, kk], sem.at[slot], 0, r).wait()
        return carry

    lax.fori_loop(0, tc, wait, 0, unroll=_DMA_UNROLL)
    gate = gate_ref[...]
    o_ref[...] = r_ref[...] + gate[:, 0:1] * buf[slot, 0] + gate[:, 1:2] * buf[slot, 1]


def _combine(y_sorted, pos_flat, gates, resid, *, tc):
    m, d = resid.shape
    return pl.pallas_call(
        functools.partial(_combine_body, tc=tc),
        grid_spec=pltpu.PrefetchScalarGridSpec(
            num_scalar_prefetch=1, grid=(m // tc,),
            in_specs=[pl.BlockSpec(memory_space=pl.ANY),
                      pl.BlockSpec((tc, TOP_K), lambda i, pos: (i, 0)),
                      pl.BlockSpec((tc, d), lambda i, pos: (i, 0))],
            out_specs=pl.BlockSpec((tc, d), lambda i, pos: (i, 0)),
            scratch_shapes=[pltpu.VMEM((2, TOP_K, tc, d), f32), pltpu.SemaphoreType.DMA((2,))]),
        out_shape=jax.ShapeDtypeStruct((m, d), f32),
        compiler_params=_params("arbitrary"), name="moe_combine")(pos_flat, y_sorted, gates, resid)


def _routing_tables(top_i, n_exp, tm, n_tiles):
    m = top_i.shape[0]
    e_flat = top_i.reshape(-1)
    onehot = (e_flat[:, None] == jnp.arange(n_exp, dtype=jnp.int32)[None, :]).astype(jnp.int32)
    rank = jnp.take_along_axis(jnp.cumsum(onehot, axis=0), e_flat[:, None], axis=1)[:, 0] - 1
    counts = jnp.sum(onehot, axis=0)
    tiles_per = (counts + tm - 1) // tm
    tile_end = jnp.cumsum(tiles_per)
    tile_start = tile_end - tiles_per
    pos = tile_start[e_flat] * tm + rank
    token = jnp.arange(m * TOP_K, dtype=jnp.int32) // TOP_K
    row_token = jnp.zeros((n_tiles * tm,), jnp.int32).at[pos].set(token)
    n_valid = tile_end[-1]
    tid = jnp.minimum(jnp.arange(n_tiles, dtype=jnp.int32), n_valid - 1)
    tile_expert = jnp.sum((tid[:, None] >= tile_end[None, :]).astype(jnp.int32), axis=1)
    tile_first = jnp.concatenate([jnp.ones((1,), jnp.int32),
                                  (tile_expert[1:] != tile_expert[:-1]).astype(jnp.int32)])
    return (pos.astype(jnp.int32), row_token, tile_expert.astype(jnp.int32), tile_first,
            n_valid.reshape(1).astype(jnp.int32))


def _moe_ffn(h, g_ffn, w_router, b_router, wg, wu, wd, *, tm_tok, tm, tn_up, tn_down):
    m = h.shape[0]
    n_exp = wg.shape[0]
    c_f32, top_i, top_g = _router(h, g_ffn, w_router, b_router, tm=tm_tok)
    n_tiles = (m * TOP_K) // tm + n_exp
    pos, row_token, tile_expert, tile_first, n_valid = _routing_tables(top_i, n_exp, tm, n_tiles)
    x_sorted = _gather_rows(c_f32, row_token, tg=tm, out_dtype=bf16)
    h_sorted = _gmm_swiglu(x_sorted, wg, wu, tile_expert, tile_first, n_valid, tm=tm, tn=tn_up)
    y_sorted = _gmm_down(h_sorted, wd, tile_expert, tile_first, n_valid, tm=tm, tn=tn_down)
    return _combine(y_sorted, pos, top_g, h, tc=tm_tok // 2)


def kernel(x_prompt, x_sample, state_mlstm_C, state_mlstm_n, state_mlstm_m, state_s5_re, state_s5_im, state_ssd, cache_conv, p_prompt, p_sample, g_mix, w_in, b_igate, b_fgate, g_ml, s5_lam_re, s5_lam_im, s5_log_dt, s5_b_re, s5_b_im, s5_c_re, s5_c_im, s5_d, s5_w_glu, s5_b_glu, g_s5, ssd_conv_w, ssd_conv_b, ssd_dt_bias, ssd_a_log, ssd_d, g_ssd, w_out, g_ffn, ffn_w_gate, ffn_w_up, ffn_w_down, w_router, b_router, moe_w_gate, moe_w_up, moe_w_down, g_ple, w_ple, w_ple_gate, g_final):
    bp, seq, d = x_prompt.shape
    bs = x_sample.shape[0]
    tp = bp * seq
    m = tp + bs
    tm = m // 10 if (m % 10 == 0 and (m // 10) % 16 == 0) else 8
    heads = ML_HEADS
    t5 = math.gcd(seq, S5_CHUNK)
    nc5 = seq // t5
    cl = math.gcd(seq, CHUNK)
    nc = seq // cl

    h = jnp.concatenate([x_prompt.reshape(tp, d), x_sample.reshape(bs, d)], axis=0)
    p_all = jnp.concatenate([p_prompt.reshape(DEPTH, tp, -1), p_sample.reshape(DEPTH, bs, -1)], axis=1).astype(bf16)
    n0_in = state_mlstm_n.reshape(DEPTH, bs, heads, 1, ML_DK)
    m0_in = state_mlstm_m.reshape(DEPTH, bs, 1, heads)
    zeros5 = jnp.zeros((S5_GROUPS, bp, S5_STATE), f32)
    w_in_t = jnp.swapaxes(w_in, 1, 2)
    outs_p = [[] for _ in range(7)]
    outs_s = [[] for _ in range(5)]
    c_stack = s_stack = None

    for i in range(DEPTH):
        a = _rmsnorm(h, g_mix[i], bf16, tm)
        qkvo = _mm_nt(a, w_in_t, tm=tm, tn=512, layer=i, row_off=0, n_rows=_OFF_GATES, name="in_proj_qkvo")
        uzx = _mm_nt(a, w_in_t, tm=tm, tn=512, layer=i, row_off=_OFF_U, n_rows=_OFF_DT - _OFF_U, name="in_proj_uzx")
        gd = _mm_gates(a, w_in_t, tm=tm, layer=i, off1=_OFF_GATES, n1=2 * heads, off2=_OFF_DT, n2=SSD_HEADS,
                       name="in_proj_gates")

        gates_t = jnp.transpose(gd[:tp, :2 * heads].reshape(bp, nc, cl, 2 * heads), (0, 1, 3, 2))
        bias8 = jnp.concatenate([b_igate[i], b_fgate[i]])
        h_ml_p, c_p, n_p, m_p = _mlstm_prompt(qkvo, gates_t, bias8.reshape(2 * heads, 1), g_ml[i], batch=bp, seq=seq)
        qkvo_s = qkvo[tp:].reshape(bs, 1, -1)
        gd_s = gd[tp:].reshape(bs, 1, -1)
        h_ml_s, c_stack, n_s, m_s = _mlstm_sample(qkvo_s, gd_s, bias8.reshape(1, 2 * heads), g_ml[i],
                                                  state_mlstm_C, n0_in, m0_in, layer=i, c_stack=c_stack)

        mats_p = _s5_matrices(s5_lam_re[i], s5_lam_im[i], s5_log_dt[i], s5_b_re[i], s5_b_im[i],
                              s5_c_re[i], s5_c_im[i], t5)
        mats_s = _s5_matrices(s5_lam_re[i], s5_lam_im[i], s5_log_dt[i], s5_b_re[i], s5_b_im[i],
                              s5_c_re[i], s5_c_im[i], 1)
        u_p = uzx[:tp, :S5_WIDTH].reshape(bp, nc5, t5, S5_GROUPS, S5_CH)
        u_p = jnp.transpose(u_p, (3, 1, 0, 2, 4)).reshape(S5_GROUPS, nc5 * bp, t5 * S5_CH).astype(bf16)
        y_p, s5re_p, s5im_p = _s5_scan(u_p, mats_p, zeros5, zeros5, nc=nc5, batch=bp)
        y_p = jnp.transpose(y_p.reshape(S5_GROUPS, nc5, bp, t5, S5_CH), (2, 1, 3, 0, 4)).reshape(tp, S5_WIDTH)
        u_s = jnp.transpose(uzx[tp:, :S5_WIDTH].reshape(bs, S5_GROUPS, S5_CH), (1, 0, 2)).astype(bf16)
        y_s, s5re_s, s5im_s = _s5_scan(u_s, mats_s, jnp.swapaxes(state_s5_re[i], 0, 1),
                                       jnp.swapaxes(state_s5_im[i], 0, 1), nc=1, batch=bs)
        y_s = jnp.transpose(y_s, (1, 0, 2)).reshape(bs, S5_WIDTH)
        y5 = _s5_glu(jnp.concatenate([y_p, y_s], axis=0), uzx, s5_d[i].reshape(1, S5_WIDTH), s5_w_glu,
                     s5_b_glu[i].reshape(1, S5_WIDTH), g_s5[i].reshape(1, S5_WIDTH), layer=i, tm=tm)

        dt_t = jnp.transpose(gd[:tp, 2 * heads:2 * heads + SSD_HEADS].reshape(bp, nc, cl, SSD_HEADS), (0, 1, 3, 2))
        y_ssd_p, ssd_p = _ssd_prompt(uzx, gd, dt_t, ssd_conv_w[i], ssd_conv_b[i], ssd_dt_bias[i], ssd_a_log[i],
                                     ssd_d[i], g_ssd[i], batch=bp, seq=seq)
        uzx_s = uzx[tp:].reshape(bs, 1, -1)
        y_ssd_s, s_stack = _ssd_sample(uzx_s, cache_conv, gd_s, ssd_conv_w[i], ssd_conv_b[i], ssd_dt_bias[i],
                                       ssd_a_log[i], ssd_d[i], g_ssd[i], state_ssd, layer=i, s_stack=s_stack)
        xbc = uzx[:, S5_WIDTH + SSD_WIDTH:]
        conv_p = xbc[:tp].reshape(bp, seq, -1)[:, seq - (SSD_CONV - 1):]
        conv_s = jnp.concatenate([cache_conv[i][:, 1:], xbc[tp:].reshape(bs, 1, -1)], axis=1)

        for lst, s in zip(outs_p, (c_p, n_p.reshape(bp, heads, ML_DK), m_p.reshape(bp, heads),
                                   jnp.swapaxes(s5re_p, 0, 1), jnp.swapaxes(s5im_p, 0, 1), ssd_p, conv_p)):
            lst.append(s)
        for lst, s in zip(outs_s, (n_s.reshape(bs, heads, ML_DK), m_s.reshape(bs, heads),
                                   jnp.swapaxes(s5re_s, 0, 1), jnp.swapaxes(s5im_s, 0, 1), conv_s)):
            lst.append(s)

        mix_in = jnp.concatenate([
            jnp.concatenate([h_ml_p, y5[:tp], y_ssd_p], axis=1),
            jnp.concatenate([h_ml_s.reshape(bs, -1).astype(bf16), y5[tp:], y_ssd_s.reshape(bs, -1).astype(bf16)], axis=1),
        ], axis=0)
        h = _mm(mix_in, w_out, tm=tm, tn=512, layer=i, resid=h, name="out_proj")

        j = i // 2
        if i % 2 == 0:
            cn = _rmsnorm(h, g_ffn[i], bf16, tm)
            hid = _mm_swiglu(cn, ffn_w_gate, ffn_w_up, tm=tm, tn=512, layer=j, name="ffn_gate_up")
            h = _mm(hid, ffn_w_down, tm=tm // 2, tn=512, layer=j, resid=h, name="ffn_down")
        else:
            n_moe = moe_w_gate.shape[0]
            sel = lambda w: w.reshape((n_moe * N_EXPERTS,) + w.shape[2:])[j * N_EXPERTS:(j + 1) * N_EXPERTS] if n_moe > 1 else w.reshape(w.shape[1:])
            h = _moe_ffn(h, g_ffn[i], w_router[j], b_router[j], sel(moe_w_gate), sel(moe_w_up), sel(moe_w_down),
                         tm_tok=tm, tm=256, tn_up=1024, tn_down=512)

        e = _rmsnorm(h, g_ple[i], bf16, tm)
        h = _mm_ple(e, p_all[i], w_ple_gate, w_ple, h, tm=tm, tn=512, layer=i)

    y = _rmsnorm(h, g_final, f32, tm)
    n_s, m_s, s5re_s, s5im_s, conv_s = (jnp.stack(l) for l in outs_s)
    return ((y[:tp].reshape(bp, seq, d), y[tp:].reshape(bs, 1, d)) + tuple(jnp.stack(l) for l in outs_p)
            + (c_stack, n_s, m_s, s5re_s, s5im_s, s_stack, conv_s))
```

```python
import functools
import math

import jax
import jax.numpy as jnp
from jax import lax
from jax.experimental import pallas as pl
from jax.experimental.pallas import tpu as pltpu

f32 = jnp.float32
bf16 = jnp.bfloat16
HI = lax.Precision.HIGHEST

D_MODEL = 2048
DEPTH = 2
ML_HEADS = 4
ML_DK = 256
ML_DV = 256
ML_WIDTH = ML_HEADS * ML_DV
S5_CH = 16
S5_WIDTH = 512
S5_GROUPS = 32
S5_STATE = 64
SSD_HEAD_DIM = 64
SSD_WIDTH = 512
SSD_HEADS = 8
SSD_GROUPS = 2
SSD_STATE = 128
SSD_CONV = 4
SSD_CONV_CH = 1024
CHUNK = 64
S5_CHUNK = 16
N_EXPERTS = 8
TOP_K = 2
RMS_EPS = 1e-6

_OFF_GATES = 4 * ML_WIDTH
_OFF_U = _OFF_GATES + 2 * ML_HEADS
_OFF_DT = _OFF_U + S5_WIDTH + SSD_WIDTH + SSD_CONV_CH

_VMEM_LIMIT = 56 * 1024 * 1024
_NT = (((1,), (1,)), ((), ()))
_TN = (((0,), (0,)), ((), ()))


def _params(*sem):
    return pltpu.CompilerParams(dimension_semantics=sem, vmem_limit_bytes=_VMEM_LIMIT)


def _rms(x, g):
    return x * lax.rsqrt(jnp.mean(x * x, axis=-1, keepdims=True) + RMS_EPS) * g


def _rmsnorm_body(x_ref, g_ref, o_ref):
    o_ref[...] = _rms(x_ref[...], g_ref[...]).astype(o_ref.dtype)


def _last_tile(p_tile, s_rows, n_keep):
    return jnp.concatenate([p_tile[0:n_keep, :], s_rows], axis=0)


def _split_rows(tp, bs, tm):
    n_tiles = (tp + bs) // tm
    n_keep = tp - (n_tiles - 1) * tm
    assert n_tiles * tm == tp + bs and 0 < n_keep and n_keep + bs == tm and n_keep % 16 == 0
    return n_tiles, n_keep


def _token_tile(tp, bs):
    m = tp + bs
    for n_tiles in (10, 8, 5, 4, 2, 1):
        tm = m // n_tiles
        if m % n_tiles == 0 and tm % 32 == 0 and bs < tm and (tp - (n_tiles - 1) * tm) % 16 == 0:
            return tm
    raise ValueError("no row tiling for these token counts")


def _rmsnorm_in_body(xp_ref, xs_ref, g_ref, h_ref, a_ref, *, n_keep):
    last = pl.num_programs(0) - 1

    def run(x):
        h_ref[...] = x
        a_ref[...] = _rms(x, g_ref[...]).astype(a_ref.dtype)

    @pl.when(pl.program_id(0) < last)
    def _():
        run(xp_ref[...])

    @pl.when(pl.program_id(0) == last)
    def _():
        run(_last_tile(xp_ref, xs_ref[...], n_keep))


def _rmsnorm_in(xp, xs, g, tm):
    tp, d = xp.shape
    bs = xs.shape[0]
    n_tiles, n_keep = _split_rows(tp, bs, tm)
    row = lambda i: (i, 0)
    fix = lambda i: (0, 0)
    return pl.pallas_call(
        functools.partial(_rmsnorm_in_body, n_keep=n_keep), grid=(n_tiles,),
        in_specs=[pl.BlockSpec((tm, d), row), pl.BlockSpec((bs, d), fix), pl.BlockSpec((1, d), fix)],
        out_specs=[pl.BlockSpec((tm, d), row), pl.BlockSpec((tm, d), row)],
        out_shape=[jax.ShapeDtypeStruct((tp + bs, d), f32), jax.ShapeDtypeStruct((tp + bs, d), bf16)],
        compiler_params=_params("parallel"), name="rmsnorm_in")(xp, xs, g.reshape(1, d))


def _rmsnorm_out_body(x_ref, g_ref, yp_ref, ys_ref, *, n_keep):
    y = _rms(x_ref[...], g_ref[...])
    yp_ref[...] = y

    @pl.when(pl.program_id(0) == pl.num_programs(0) - 1)
    def _():
        ys_ref[...] = y[n_keep:, :]


def _rmsnorm_out(x, g, tp, tm):
    m, d = x.shape
    bs = m - tp
    n_tiles, n_keep = _split_rows(tp, bs, tm)
    return pl.pallas_call(
        functools.partial(_rmsnorm_out_body, n_keep=n_keep), grid=(n_tiles,),
        in_specs=[pl.BlockSpec((tm, d), lambda i: (i, 0)), pl.BlockSpec((1, d), lambda i: (0, 0))],
        out_specs=[pl.BlockSpec((tm, d), lambda i: (i, 0)), pl.BlockSpec((bs, d), lambda i: (0, 0))],
        out_shape=[jax.ShapeDtypeStruct((tp, d), f32), jax.ShapeDtypeStruct((bs, d), f32)],
        compiler_params=_params("arbitrary"), name="rmsnorm_out")(x, g.reshape(1, d))


def _rmsnorm(x, g, out_dtype, tm):
    m, d = x.shape
    return pl.pallas_call(
        _rmsnorm_body, grid=(m // tm,),
        in_specs=[pl.BlockSpec((tm, d), lambda i: (i, 0)), pl.BlockSpec((1, d), lambda i: (0, 0))],
        out_specs=pl.BlockSpec((tm, d), lambda i: (i, 0)),
        out_shape=jax.ShapeDtypeStruct((m, d), out_dtype),
        compiler_params=_params("parallel"), name="rmsnorm")(x, g.reshape(1, d))


def _cast_weight_once(w_ref, wb_ref):
    @pl.when(pl.program_id(1) == 0)
    def _():
        wb_ref[...] = w_ref[...].astype(bf16)


def _mm_plain_body(a_ref, w_ref, o_ref, wb_ref):
    _cast_weight_once(w_ref, wb_ref)
    o_ref[...] = jnp.dot(a_ref[...], wb_ref[...], preferred_element_type=f32).astype(o_ref.dtype)


def _mm_resid_body(a_ref, w_ref, r_ref, o_ref, wb_ref):
    _cast_weight_once(w_ref, wb_ref)
    o_ref[...] = r_ref[...] + jnp.dot(a_ref[...], wb_ref[...], preferred_element_type=f32)


def _mm_swiglu_body(a_ref, wg_ref, wu_ref, o_ref, wgb_ref, wub_ref):
    _cast_weight_once(wg_ref, wgb_ref)
    _cast_weight_once(wu_ref, wub_ref)
    a = a_ref[...]
    g = jnp.dot(a, wgb_ref[...], preferred_element_type=f32)
    u = jnp.dot(a, wub_ref[...], preferred_element_type=f32)
    o_ref[...] = (g * jax.nn.sigmoid(g) * u).astype(o_ref.dtype)


def _mm_ple_body(e_ref, pp_ref, ps_ref, wg_ref, wp_ref, r_ref, o_ref, wgb_ref, wpb_ref, *, n_keep):
    _cast_weight_once(wg_ref, wgb_ref)
    _cast_weight_once(wp_ref, wpb_ref)
    last = pl.num_programs(1) - 1

    def run(p):
        gate = jnp.dot(e_ref[...], wgb_ref[...], preferred_element_type=f32)
        emb = jnp.dot(p.astype(bf16), wpb_ref[...], preferred_element_type=f32)
        o_ref[...] = r_ref[...] + emb * jax.nn.sigmoid(gate)

    @pl.when(pl.program_id(1) < last)
    def _():
        run(pp_ref[...])

    @pl.when(pl.program_id(1) == last)
    def _():
        run(_last_tile(pp_ref, ps_ref[...], n_keep))


def _mm_mix_body(a1p_ref, a1s_ref, a2_ref, a3p_ref, a3s_ref, w_ref, r_ref, o_ref, wb_ref, *, n_keep):
    _cast_weight_once(w_ref, wb_ref)
    last = pl.num_programs(1) - 1
    k1 = a1p_ref.shape[1]
    k2 = a2_ref.shape[1]

    def run(a1, a3):
        acc = jnp.dot(a1, wb_ref[0:k1, :], preferred_element_type=f32)
        acc += jnp.dot(a2_ref[...], wb_ref[k1:k1 + k2, :], preferred_element_type=f32)
        acc += jnp.dot(a3, wb_ref[k1 + k2:, :], preferred_element_type=f32)
        o_ref[...] = r_ref[...] + acc

    @pl.when(pl.program_id(1) < last)
    def _():
        run(a1p_ref[...], a3p_ref[...])

    @pl.when(pl.program_id(1) == last)
    def _():
        run(_last_tile(a1p_ref, a1s_ref[...].astype(bf16), n_keep),
            _last_tile(a3p_ref, a3s_ref[...].astype(bf16), n_keep))


def _mm_mix(a1p, a1s, a2, a3p, a3s, w, resid, *, tm, tn, layer, name):
    m, n = resid.shape
    tp, k1 = a1p.shape
    bs = a1s.shape[0]
    k2, k3 = a2.shape[1], a3p.shape[1]
    n_tiles, n_keep = _split_rows(tp, bs, tm)
    rows = lambda j, i: (i, 0)
    fix = lambda j, i: (0, 0)
    o_spec = pl.BlockSpec((tm, tn), lambda j, i: (i, j))
    return pl.pallas_call(
        functools.partial(_mm_mix_body, n_keep=n_keep), grid=(n // tn, n_tiles),
        in_specs=[pl.BlockSpec((tm, k1), rows), pl.BlockSpec((bs, k1), fix), pl.BlockSpec((tm, k2), rows),
                  pl.BlockSpec((tm, k3), rows), pl.BlockSpec((bs, k3), fix),
                  _wspec(w, layer, k1 + k2 + k3, tn), o_spec],
        out_specs=o_spec, out_shape=jax.ShapeDtypeStruct((m, n), f32),
        scratch_shapes=[pltpu.VMEM((k1 + k2 + k3, tn), bf16)],
        compiler_params=_params("arbitrary", "arbitrary"), name=name)(a1p, a1s, a2, a3p, a3s, w, resid)


def _wspec(w, layer, k, tn, col_block_off=0):
    if w.ndim == 2:
        return pl.BlockSpec((k, tn), lambda j, i: (0, j + col_block_off))
    return pl.BlockSpec((None, k, tn), lambda j, i: (layer, 0, j + col_block_off))


def _mm(a, w, *, tm, tn, layer=0, n_cols=None, col_off=0, resid=None, out_dtype=f32, name="mm"):
    m, k = a.shape
    n = n_cols if n_cols is not None else w.shape[-1]
    grid = (n // tn, m // tm)
    a_spec = pl.BlockSpec((tm, k), lambda j, i: (i, 0))
    o_spec = pl.BlockSpec((tm, tn), lambda j, i: (i, j))
    w_spec = _wspec(w, layer, k, tn, col_off // tn)
    scratch = [pltpu.VMEM((k, tn), bf16)]
    if resid is None:
        body, ins, specs = _mm_plain_body, (a, w), [a_spec, w_spec]
    else:
        body, ins, specs = _mm_resid_body, (a, w, resid), [a_spec, w_spec, o_spec]
    return pl.pallas_call(
        body, grid=grid, in_specs=specs, out_specs=o_spec,
        out_shape=jax.ShapeDtypeStruct((m, n), out_dtype), scratch_shapes=scratch,
        compiler_params=_params("arbitrary", "arbitrary"), name=name)(*ins)


def _mm_nt_body(a_ref, wt_ref, o_ref, wb_ref):
    @pl.when(pl.program_id(1) == 0)
    def _():
        wb_ref[...] = wt_ref[0].astype(bf16)

    o_ref[...] = lax.dot_general(a_ref[...], wb_ref[...], _NT, preferred_element_type=f32)


def _wt_rows(layer, k, n_rows, row_of):
    return pl.BlockSpec((pl.Element(1), pl.Element(n_rows), pl.Element(k)),
                        lambda *idx: (layer, row_of(*idx), 0))


def _mm_nt(a, wt, *, tm, tn, layer, row_off, n_rows, name):
    m, k = a.shape
    return pl.pallas_call(
        _mm_nt_body, grid=(n_rows // tn, m // tm),
        in_specs=[pl.BlockSpec((tm, k), lambda j, i: (i, 0)),
                  _wt_rows(layer, k, tn, lambda j, i: pl.multiple_of(row_off + tn * j, 8))],
        out_specs=pl.BlockSpec((tm, tn), lambda j, i: (i, j)),
        out_shape=jax.ShapeDtypeStruct((m, n_rows), f32), scratch_shapes=[pltpu.VMEM((tn, k), bf16)],
        compiler_params=_params("arbitrary", "arbitrary"), name=name)(a, wt)


def _mm_gates_body(a_ref, w1_ref, w2_ref, o_ref):
    a = a_ref[...]
    n1 = w1_ref.shape[1]
    o_ref[:, 0:n1] = lax.dot_general(a, w1_ref[0].astype(bf16), _NT, preferred_element_type=f32)
    o_ref[:, n1:] = lax.dot_general(a, w2_ref[0].astype(bf16), _NT, preferred_element_type=f32)


def _mm_gates(a, wt, *, tm, layer, off1, n1, off2, n2, name):
    m, k = a.shape
    return pl.pallas_call(
        _mm_gates_body, grid=(m // tm,),
        in_specs=[pl.BlockSpec((tm, k), lambda i: (i, 0)),
                  _wt_rows(layer, k, n1, lambda i: off1), _wt_rows(layer, k, n2, lambda i: off2)],
        out_specs=pl.BlockSpec((tm, n1 + n2), lambda i: (i, 0)),
        out_shape=jax.ShapeDtypeStruct((m, n1 + n2), f32),
        compiler_params=_params("parallel"), name=name)(a, wt, wt)


def _mm_swiglu(a, wg, wu, *, tm, tn, layer=0, name="mm_swiglu"):
    m, k = a.shape
    n = wg.shape[-1]
    a_spec = pl.BlockSpec((tm, k), lambda j, i: (i, 0))
    o_spec = pl.BlockSpec((tm, tn), lambda j, i: (i, j))
    return pl.pallas_call(
        _mm_swiglu_body, grid=(n // tn, m // tm),
        in_specs=[a_spec, _wspec(wg, layer, k, tn), _wspec(wu, layer, k, tn)], out_specs=o_spec,
        out_shape=jax.ShapeDtypeStruct((m, n), bf16),
        scratch_shapes=[pltpu.VMEM((k, tn), bf16), pltpu.VMEM((k, tn), bf16)],
        compiler_params=_params("arbitrary", "arbitrary"), name=name)(a, wg, wu)


def _mm_ple(e, p_prompt, p_sample, w_gate, w_ple, resid, *, tm, tn, layer, name="mm_ple"):
    m, k = e.shape
    _, tp, kp = p_prompt.shape
    bs = p_sample.shape[1]
    n = w_gate.shape[-1]
    n_tiles, n_keep = _split_rows(tp, bs, tm)
    o_spec = pl.BlockSpec((tm, tn), lambda j, i: (i, j))
    return pl.pallas_call(
        functools.partial(_mm_ple_body, n_keep=n_keep), grid=(n // tn, n_tiles),
        in_specs=[pl.BlockSpec((tm, k), lambda j, i: (i, 0)),
                  pl.BlockSpec((None, tm, kp), lambda j, i: (layer, i, 0)),
                  pl.BlockSpec((None, bs, kp), lambda j, i: (layer, 0, 0)),
                  _wspec(w_gate, layer, k, tn), _wspec(w_ple, layer, kp, tn), o_spec],
        out_specs=o_spec, out_shape=jax.ShapeDtypeStruct((m, n), f32),
        scratch_shapes=[pltpu.VMEM((k, tn), bf16), pltpu.VMEM((kp, tn), bf16)],
        compiler_params=_params("arbitrary", "arbitrary"), name=name)(e, p_prompt, p_sample, w_gate, w_ple, resid)


def _col_from_row(row, eye):
    n = row.shape[1]
    return jnp.sum(jnp.where(eye, jnp.broadcast_to(row, (n, n)), 0.0), axis=1, keepdims=True)


def _mlstm_prompt_body(q_ref, k_ref, v_ref, o_ref, gt_ref, bias_ref, gml_ref,
                       h_ref, c_out, n_out, m_out, c_s, n_s, m_s, *, heads, dk, dv, cl):
    c = pl.program_id(1)

    @pl.when(c == 0)
    def _():
        c_s[...] = jnp.zeros_like(c_s)
        n_s[...] = jnp.zeros_like(n_s)
        m_s[...] = jnp.zeros_like(m_s)

    row = lax.broadcasted_iota(jnp.int32, (cl, cl), 0)
    col = lax.broadcasted_iota(jnp.int32, (cl, cl), 1)
    causal = col <= row
    eye = col == row
    triu = (row <= col).astype(f32)
    gt = gt_ref[0, 0] + bias_ref[...]
    lf = jax.nn.log_sigmoid(gt[heads:2 * heads])
    b_all = jnp.dot(lf, triu, precision=HI, preferred_element_type=f32)
    for hh in range(heads):
        ig = gt[hh:hh + 1]
        b_row = b_all[hh:hh + 1]
        b_col = _col_from_row(b_row, eye)
        m_prev = m_s[hh]
        d = jnp.where(causal, b_col - b_row + ig, -jnp.inf)
        inter = b_col + m_prev
        m_t = jnp.maximum(inter, jnp.max(d, axis=1, keepdims=True))
        w = jnp.exp(d - m_t)
        g = jnp.exp(inter - m_t)
        q = q_ref[:, hh * dk:(hh + 1) * dk]
        k = k_ref[:, hh * dk:(hh + 1) * dk] * (dk ** -0.5)
        vb = v_ref[:, hh * dv:(hh + 1) * dv].astype(bf16)
        qb = q.astype(bf16)
        cmat = c_s[hh]
        n_row = n_s[hh]
        s = lax.dot_general(qb, k.astype(bf16), _NT, preferred_element_type=f32) * w
        num = (jnp.dot(s.astype(bf16), vb, preferred_element_type=f32)
               + g * jnp.dot(qb, cmat.astype(bf16), preferred_element_type=f32))
        den = jnp.sum(s, axis=1, keepdims=True) + g * jnp.sum(q * n_row, axis=1, keepdims=True)
        hraw = num / jnp.maximum(jnp.abs(den), jnp.exp(-m_t))
        hn = _rms(hraw, gml_ref[hh:hh + 1, :])
        ogate = jax.nn.sigmoid(o_ref[:, hh * dv:(hh + 1) * dv])
        h_ref[:, hh * dv:(hh + 1) * dv] = (ogate * hn).astype(h_ref.dtype)
        b_last = b_row[:, cl - 1:cl]
        dl = b_last - b_row + ig
        m_new = jnp.maximum(b_last + m_prev, jnp.max(dl, axis=1, keepdims=True))
        ws_col = _col_from_row(jnp.exp(dl - m_new), eye)
        gl = jnp.exp(b_last + m_prev - m_new)
        kw = k * ws_col
        c_s[hh] = gl * cmat + lax.dot_general(kw.astype(bf16), vb, _TN, preferred_element_type=f32)
        n_s[hh] = gl * n_row + jnp.sum(kw, axis=0, keepdims=True)
        m_s[hh] = m_new

    @pl.when(c == pl.num_programs(1) - 1)
    def _():
        c_out[0] = c_s[...]
        n_out[0] = n_s[...]
        m_out[0] = m_s[...]


def _mlstm_prompt(qkvo, gates_t, bias_col, g_ml, *, batch, seq):
    heads, dk, dv, cl = ML_HEADS, ML_DK, ML_DV, math.gcd(seq, CHUNK)
    nc = seq // cl
    wq = heads * dk
    rows = lambda b, c: (b * nc + c, 0)
    body = functools.partial(_mlstm_prompt_body, heads=heads, dk=dk, dv=dv, cl=cl)
    return pl.pallas_call(
        body, grid=(batch, nc),
        in_specs=[pl.BlockSpec((cl, wq), lambda b, c: (b * nc + c, 0)),
                  pl.BlockSpec((cl, wq), lambda b, c: (b * nc + c, 1)),
                  pl.BlockSpec((cl, wq), lambda b, c: (b * nc + c, 2)),
                  pl.BlockSpec((cl, wq), lambda b, c: (b * nc + c, 3)),
                  pl.BlockSpec((1, 1, 2 * heads, cl), lambda b, c: (b, c, 0, 0)),
                  pl.BlockSpec((2 * heads, 1), lambda b, c: (0, 0)),
                  pl.BlockSpec((heads, dv), lambda b, c: (0, 0))],
        out_specs=[pl.BlockSpec((cl, heads * dv), rows),
                   pl.BlockSpec((1, heads, dk, dv), lambda b, c: (b, 0, 0, 0)),
                   pl.BlockSpec((1, heads, 1, dk), lambda b, c: (b, 0, 0, 0)),
                   pl.BlockSpec((1, heads, 1, 1), lambda b, c: (b, 0, 0, 0))],
        out_shape=[jax.ShapeDtypeStruct((batch * seq, heads * dv), bf16),
                   jax.ShapeDtypeStruct((batch, heads, dk, dv), f32),
                   jax.ShapeDtypeStruct((batch, heads, 1, dk), f32),
                   jax.ShapeDtypeStruct((batch, heads, 1, 1), f32)],
        scratch_shapes=[pltpu.VMEM((heads, dk, dv), f32), pltpu.VMEM((heads, 1, dk), f32),
                        pltpu.VMEM((heads, 1, 1), f32)],
        compiler_params=_params("arbitrary", "arbitrary"), name="mlstm_prompt",
    )(qkvo, qkvo, qkvo, qkvo, gates_t, bias_col, g_ml)


def _lane_pick(pieces, width):
    lane = lax.broadcasted_iota(jnp.int32, (1, width), 1)
    out = jnp.zeros((1, width), f32)
    for i, p in enumerate(pieces):
        out = jnp.where(lane == i, p, out)
    return out


_SAMPLE_BLOCK = 8


def _mlstm_sample_body(*refs, heads, dk, dv, aliased):
    if aliased:
        refs = refs[:10] + refs[11:]
    (q_ref, k_ref, v_ref, o_ref, gd_ref, bias_ref, gml_ref, c0_ref, n0_ref, m0_ref,
     h_ref, c_out, n_out, m_out) = refs
    bt = q_ref.shape[0]

    @pl.when(pl.program_id(0) > 0)
    def _():
        c_out[...] = jnp.zeros_like(c_out)

    @pl.when(pl.program_id(0) == 0)
    def _():
        g8 = gd_ref[:, 0:2 * heads] + bias_ref[...]
        lane = lax.broadcasted_iota(jnp.int32, (bt, heads), 1)
        m_new = jnp.zeros((bt, heads), f32)
        for hh in range(heads):
            ig = g8[:, hh:hh + 1]
            lf = jax.nn.log_sigmoid(g8[:, heads + hh:heads + hh + 1])
            m0 = m0_ref[:, hh:hh + 1]
            m_t = jnp.maximum(lf + m0, ig)
            w = jnp.exp(ig - m_t)
            g = jnp.exp(lf + m0 - m_t)
            q = q_ref[:, hh * dk:(hh + 1) * dk]
            k = k_ref[:, hh * dk:(hh + 1) * dk] * (dk ** -0.5)
            v = v_ref[:, hh * dv:(hh + 1) * dv]
            o = o_ref[:, hh * dv:(hh + 1) * dv]
            n0 = n0_ref[:, hh, :]
            kw = k * w
            q_t = q.T
            kw_t = kw.T
            qc_rows = []
            for b in range(bt):
                cmat = c0_ref[b, hh]
                qc_rows.append(jnp.sum(q_t[:, b:b + 1] * cmat, axis=0, keepdims=True))
                c_out[b, hh] = g[b:b + 1, :] * cmat + kw_t[:, b:b + 1] * v[b:b + 1, :]
            qc = jnp.concatenate(qc_rows, axis=0)
            s = jnp.sum(q * k, axis=1, keepdims=True) * w
            num = s * v + g * qc
            den = s + g * jnp.sum(q * n0, axis=1, keepdims=True)
            hraw = num / jnp.maximum(jnp.abs(den), jnp.exp(-m_t))
            hn = _rms(hraw, gml_ref[hh:hh + 1, :])
            h_ref[:, hh * dv:(hh + 1) * dv] = jax.nn.sigmoid(o) * hn
            n_out[:, hh, :] = g * n0 + kw
            m_new = jnp.where(lane == hh, m_t, m_new)
        m_out[...] = m_new


def _stacked_state_grid(layer, b, stack):
    n_l = DEPTH if stack is None else 1
    row = lambda l, i: jnp.where(l == 0, i, b - 1)
    out_layer = lambda l: (layer + l) % DEPTH
    return (n_l, b), row, out_layer


def _mlstm_sample(qkvo, gd, bias_row, g_ml, c0, n0, m0, *, layer, row0, c_stack=None):
    heads, dk, dv = ML_HEADS, ML_DK, ML_DV
    b = c0.shape[1]
    bt = _SAMPLE_BLOCK
    wq = heads * dk
    blk0 = row0 // bt
    grid, row, out_layer = _stacked_state_grid(layer, b // bt, c_stack)
    aliased = c_stack is not None
    body = functools.partial(_mlstm_sample_body, heads=heads, dk=dk, dv=dv, aliased=aliased)
    in_specs = [pl.BlockSpec((bt, wq), lambda l, i, c=c: (blk0 + row(l, i), c)) for c in range(4)]
    in_specs += [pl.BlockSpec((bt, gd.shape[-1]), lambda l, i: (blk0 + row(l, i), 0)),
                 pl.BlockSpec((1, 2 * heads), lambda l, i: (0, 0)),
                 pl.BlockSpec((heads, dv), lambda l, i: (0, 0)),
                 pl.BlockSpec((None, bt, heads, dk, dv), lambda l, i: (layer, row(l, i), 0, 0, 0)),
                 pl.BlockSpec((None, bt, heads, dk), lambda l, i: (layer, row(l, i), 0, 0)),
                 pl.BlockSpec((None, bt, heads), lambda l, i: (layer, row(l, i), 0))]
    args = [qkvo, qkvo, qkvo, qkvo, gd, bias_row, g_ml, c0, n0, m0]
    if aliased:
        in_specs.append(pl.BlockSpec(memory_space=pl.ANY))
        args.append(c_stack)
    return pl.pallas_call(
        body, grid=grid, in_specs=in_specs,
        out_specs=[pl.BlockSpec((bt, heads * dv), lambda l, i: (row(l, i), 0)),
                   pl.BlockSpec((None, bt, heads, dk, dv), lambda l, i: (out_layer(l), i, 0, 0, 0)),
                   pl.BlockSpec((bt, heads, dk), lambda l, i: (row(l, i), 0, 0)),
                   pl.BlockSpec((bt, heads), lambda l, i: (row(l, i), 0))],
        out_shape=[jax.ShapeDtypeStruct((b, heads * dv), f32),
                   jax.ShapeDtypeStruct((DEPTH, b, heads, dk, dv), f32),
                   jax.ShapeDtypeStruct((b, heads, dk), f32),
                   jax.ShapeDtypeStruct((b, heads), f32)],
        input_output_aliases={10: 1} if aliased else {},
        compiler_params=_params("arbitrary", "arbitrary"), name="mlstm_sample",
    )(*args)


def _s5_body(u_ref, m_ref, wre_ref, wim_ref, vre_ref, vim_ref, lre_ref, lim_ref, x0re_ref, x0im_ref,
             y_ref, xre_out, xim_out, xs_re, xs_im, *, nc, batch):
    u = u_ref[0]
    xin_re = jnp.dot(u, wre_ref[0], preferred_element_type=f32)
    xin_im = jnp.dot(u, wim_ref[0], preferred_element_type=f32)
    lre = lre_ref[0]
    lim = lim_ref[0]
    xr = x0re_ref[0]
    xi = x0im_ref[0]
    for k in range(nc):
        sl = slice(k * batch, (k + 1) * batch)
        xs_re[sl, :] = xr
        xs_im[sl, :] = xi
        xr, xi = (lre * xr - lim * xi + xin_re[sl, :], lre * xi + lim * xr + xin_im[sl, :])
    xre_out[0] = xr
    xim_out[0] = xi
    y_ref[0] = (jnp.dot(u, m_ref[0], preferred_element_type=f32)
                + jnp.dot(xs_re[...].astype(bf16), vre_ref[0], preferred_element_type=f32)
                + jnp.dot(xs_im[...].astype(bf16), vim_ref[0], preferred_element_type=f32))


def _s5_scan(u_g, mats, x0re, x0im, *, nc, batch):
    m_mat, w_re, w_im, v_re, v_im, l_re, l_im = mats
    g, rows, tc = u_g.shape
    p = w_re.shape[-1]
    blk = lambda *s: pl.BlockSpec((1,) + s, lambda i: (i,) + (0,) * len(s))
    body = functools.partial(_s5_body, nc=nc, batch=batch)
    return pl.pallas_call(
        body, grid=(g,),
        in_specs=[blk(rows, tc), blk(tc, tc), blk(tc, p), blk(tc, p), blk(p, tc), blk(p, tc),
                  blk(1, p), blk(1, p), blk(batch, p), blk(batch, p)],
        out_specs=[blk(rows, tc), blk(batch, p), blk(batch, p)],
        out_shape=[jax.ShapeDtypeStruct((g, rows, tc), f32),
                   jax.ShapeDtypeStruct((g, batch, p), f32), jax.ShapeDtypeStruct((g, batch, p), f32)],
        scratch_shapes=[pltpu.VMEM((rows, p), f32), pltpu.VMEM((rows, p), f32)],
        compiler_params=_params("parallel"), name=f"s5_scan_t{tc // S5_CH}",
    )(u_g, m_mat, w_re, w_im, v_re, v_im, l_re, l_im, x0re, x0im)


def _s5_matrices(lam_re, lam_im, log_dt, b_re, b_im, c_re, c_im, t):
    g, p = lam_re.shape
    ch = b_re.shape[-1]
    dt = jnp.exp(log_dt)[:, None]
    ar, ai = lam_re * dt, lam_im * dt

    def powers(tau):
        mag = jnp.exp(ar[:, None, :] * tau[None, :, None])
        ang = ai[:, None, :] * tau[None, :, None]
        return mag * jnp.cos(ang), mag * jnp.sin(ang)

    lbr, lbi = jnp.exp(ar) * jnp.cos(ai), jnp.exp(ar) * jnp.sin(ai)
    den = lam_re * lam_re + lam_im * lam_im
    fr = ((lbr - 1.0) * lam_re + lbi * lam_im) / den
    fi = (lbi * lam_re - (lbr - 1.0) * lam_im) / den
    bbr = jnp.swapaxes(fr[..., None] * b_re - fi[..., None] * b_im, 1, 2)
    bbi = jnp.swapaxes(fr[..., None] * b_im + fi[..., None] * b_re, 1, 2)
    cbr = c_re[:, :, None, :] * bbr[:, None, :, :] - c_im[:, :, None, :] * bbi[:, None, :, :]
    cbi = c_re[:, :, None, :] * bbi[:, None, :, :] + c_im[:, :, None, :] * bbr[:, None, :, :]
    steps = jnp.arange(t, dtype=f32)
    diff = steps[None, :] - steps[:, None]
    lag = jnp.maximum(diff, 0.0)
    mag = jnp.exp(ar[:, None, None, :] * lag[None, :, :, None])
    ang = ai[:, None, None, :] * lag[None, :, :, None]
    keep = (diff >= 0.0)[None, :, :, None]
    e2 = jnp.concatenate([jnp.where(keep, mag * jnp.cos(ang), 0.0),
                          jnp.where(keep, mag * jnp.sin(ang), 0.0)], axis=-1)
    cb2 = jnp.concatenate([cbr, -cbi], axis=-1)
    m_mat = jnp.einsum("gstp,gcdp->gsdtc", e2, cb2, precision=HI).reshape(g, t * ch, t * ch)
    pr, pi = powers(t - 1.0 - steps)
    w_re = (pr[:, :, None, :] * bbr[:, None, :, :] - pi[:, :, None, :] * bbi[:, None, :, :]).reshape(g, t * ch, p)
    w_im = (pr[:, :, None, :] * bbi[:, None, :, :] + pi[:, :, None, :] * bbr[:, None, :, :]).reshape(g, t * ch, p)
    qr, qi = powers(steps + 1.0)
    qr, qi = jnp.swapaxes(qr, 1, 2)[..., None], jnp.swapaxes(qi, 1, 2)[..., None]
    ctr, cti = jnp.swapaxes(c_re, 1, 2)[:, :, None, :], jnp.swapaxes(c_im, 1, 2)[:, :, None, :]
    v_re = (ctr * qr - cti * qi).reshape(g, p, t * ch)
    v_im = -(ctr * qi + cti * qr).reshape(g, p, t * ch)
    ltr, lti = powers(jnp.full((1,), float(t), f32))
    return (m_mat.astype(bf16), w_re.astype(bf16), w_im.astype(bf16), v_re.astype(bf16), v_im.astype(bf16),
            ltr, lti)


def _s5_glu_body(y_ref, u_ref, d_ref, w_ref, b_ref, g_ref, o_ref):
    y5 = jax.nn.gelu(y_ref[...] + d_ref[...] * u_ref[...])
    gate = jax.nn.sigmoid(jnp.dot(y5.astype(bf16), w_ref[...].astype(bf16), preferred_element_type=f32) + b_ref[...])
    o_ref[...] = _rms(y5 * gate, g_ref[...]).astype(o_ref.dtype)


def _s5_glu(y_raw, uzx, d_skip, w_glu, b_glu, g_s5, *, layer, tm):
    m, wdt = y_raw.shape
    row = lambda i: (i, 0)
    fix = lambda i: (0, 0)
    return pl.pallas_call(
        _s5_glu_body, grid=(m // tm,),
        in_specs=[pl.BlockSpec((tm, wdt), row), pl.BlockSpec((tm, wdt), row), pl.BlockSpec((1, wdt), fix),
                  pl.BlockSpec((None, wdt, wdt), lambda i: (layer, 0, 0)), pl.BlockSpec((1, wdt), fix),
                  pl.BlockSpec((1, wdt), fix)],
        out_specs=pl.BlockSpec((tm, wdt), row), out_shape=jax.ShapeDtypeStruct((m, wdt), bf16),
        compiler_params=_params("parallel"), name="s5_glu",
    )(y_raw, uzx, d_skip, w_glu, b_glu, g_s5)


def _ssd_prompt_body(xbc_ref, z_ref, gd_ref, dtt_ref, cw_ref, cb_ref, dtb_row, dtb_col, alog_row, alog_col,
                     dskip_ref, gssd_ref, y_ref, s_out, s_s, xp_s, ys_s, *, heads, hd, ns, groups, cl, width):
    c = pl.program_id(1)

    @pl.when(c == 0)
    def _():
        s_s[...] = jnp.zeros_like(s_s)
        xp_s[0:8, :] = jnp.zeros((8, xp_s.shape[1]), f32)

    xp_s[8:8 + cl, :] = xbc_ref[...]
    xc = cb_ref[...] + sum(cw_ref[j:j + 1, :] * xp_s[5 + j:5 + j + cl, :] for j in range(SSD_CONV))
    xp_s[0:8, :] = xp_s[cl:cl + 8, :]
    xc = xc * jax.nn.sigmoid(xc)
    row = lax.broadcasted_iota(jnp.int32, (cl, cl), 0)
    col = lax.broadcasted_iota(jnp.int32, (cl, cl), 1)
    causal = col <= row
    tril = causal.astype(f32)
    triu = (row <= col).astype(f32)
    dt_col = jax.nn.softplus(gd_ref[:, 8:8 + heads] + dtb_row[...])
    dt_row = jax.nn.softplus(dtt_ref[0, 0] + dtb_col[...])
    cum_col = jnp.dot(tril, dt_col * -jnp.exp(alog_row[...]), precision=HI, preferred_element_type=f32)
    cum_row = jnp.dot(dt_row * -jnp.exp(alog_col[...]), triu, precision=HI, preferred_element_type=f32)
    rep = heads // groups
    for gi in range(groups):
        bm = xc[:, width + gi * ns:width + (gi + 1) * ns].astype(bf16)
        cm = xc[:, width + (groups + gi) * ns:width + (groups + gi + 1) * ns].astype(bf16)
        scores = lax.dot_general(cm, bm, _NT, preferred_element_type=f32)
        for hh in range(gi * rep, (gi + 1) * rep):
            cc = cum_col[:, hh:hh + 1]
            cr = cum_row[hh:hh + 1, :]
            seg = jnp.exp(jnp.where(causal, cc - cr, -jnp.inf))
            xh = xc[:, hh * hd:(hh + 1) * hd]
            xdt = xh * dt_col[:, hh:hh + 1]
            smat = s_s[hh]
            y = (jnp.dot((scores * seg).astype(bf16), xdt.astype(bf16), preferred_element_type=f32)
                 + jnp.exp(cc) * lax.dot_general(cm, smat.astype(bf16), _NT, preferred_element_type=f32))
            c_last = cr[:, cl - 1:cl]
            xw = (xdt * jnp.exp(c_last - cc)).astype(bf16)
            s_s[hh] = jnp.exp(c_last) * smat + lax.dot_general(xw, bm, _TN, preferred_element_type=f32)
            ys_s[:, hh * hd:(hh + 1) * hd] = y + dskip_ref[:, hh:hh + 1] * xh
    z = z_ref[...]
    y_ref[...] = _rms(ys_s[...] * (z * jax.nn.sigmoid(z)), gssd_ref[...]).astype(y_ref.dtype)

    @pl.when(c == pl.num_programs(1) - 1)
    def _():
        s_out[0] = s_s[...]


def _ssd_prompt(uzx, gd, dt_t, conv_w, conv_b, dt_bias, a_log, d_skip, g_ssd, *, batch, seq):
    heads, hd, ns, groups, width = SSD_HEADS, SSD_HEAD_DIM, SSD_STATE, SSD_GROUPS, SSD_WIDTH
    cl = math.gcd(seq, CHUNK)
    nc = seq // cl
    cch = SSD_CONV_CH
    rows = lambda b, c: (b * nc + c, 0)
    fix = lambda b, c: (0, 0)
    body = functools.partial(_ssd_prompt_body, heads=heads, hd=hd, ns=ns, groups=groups, cl=cl, width=width)
    return pl.pallas_call(
        body, grid=(batch, nc),
        in_specs=[pl.BlockSpec((cl, cch), lambda b, c: (b * nc + c, 1)),
                  pl.BlockSpec((cl, width), lambda b, c: (b * nc + c, 1)),
                  pl.BlockSpec((cl, gd.shape[1]), rows),
                  pl.BlockSpec((1, 1, heads, cl), lambda b, c: (b, c, 0, 0)),
                  pl.BlockSpec((SSD_CONV, cch), fix), pl.BlockSpec((1, cch), fix),
                  pl.BlockSpec((1, heads), fix), pl.BlockSpec((heads, 1), fix),
                  pl.BlockSpec((1, heads), fix), pl.BlockSpec((heads, 1), fix),
                  pl.BlockSpec((1, heads), fix), pl.BlockSpec((1, width), fix)],
        out_specs=[pl.BlockSpec((cl, width), rows),
                   pl.BlockSpec((1, heads, hd, ns), lambda b, c: (b, 0, 0, 0))],
        out_shape=[jax.ShapeDtypeStruct((batch * seq, width), bf16),
                   jax.ShapeDtypeStruct((batch, heads, hd, ns), f32)],
        scratch_shapes=[pltpu.VMEM((heads, hd, ns), f32), pltpu.VMEM((cl + 8, cch), f32),
                        pltpu.VMEM((cl, width), f32)],
        compiler_params=_params("arbitrary", "arbitrary"), name="ssd_prompt",
    )(uzx, uzx, gd, dt_t, conv_w, conv_b.reshape(1, cch), dt_bias.reshape(1, heads), dt_bias.reshape(heads, 1),
      a_log.reshape(1, heads), a_log.reshape(heads, 1), d_skip.reshape(1, heads), g_ssd.reshape(1, width))


def _ssd_sample_body(*refs, heads, hd, ns, groups, width, aliased):
    if aliased:
        refs = refs[:10] + refs[11:]
    (x_ref, conv0_ref, gd_ref, cw_ref, cb_ref, dtb_ref, alog_ref, dskip_ref, gssd_ref, s0_ref,
     y_ref, s_out, ys_s) = refs
    bt = x_ref.shape[0]

    @pl.when(pl.program_id(0) > 0)
    def _():
        s_out[...] = jnp.zeros_like(s_out)

    @pl.when(pl.program_id(0) == 0)
    def _():
        z = x_ref[:, width:2 * width]
        xc = cb_ref[...] + cw_ref[SSD_CONV - 1:SSD_CONV, :] * x_ref[:, 2 * width:]
        for j in range(SSD_CONV - 1):
            xc = xc + cw_ref[j:j + 1, :] * conv0_ref[:, j, :]
        xc = xc * jax.nn.sigmoid(xc)
        dt = jax.nn.softplus(gd_ref[:, 8:8 + heads] + dtb_ref[...])
        ea = jnp.exp(dt * -jnp.exp(alog_ref[...]))
        rep = heads // groups
        for gi in range(groups):
            bm = xc[:, width + gi * ns:width + (gi + 1) * ns]
            cm = xc[:, width + (groups + gi) * ns:width + (groups + gi + 1) * ns]
            cb_dot = jnp.sum(cm * bm, axis=1, keepdims=True)
            cmb = cm.astype(bf16)
            for hh in range(gi * rep, (gi + 1) * rep):
                xh = xc[:, hh * hd:(hh + 1) * hd]
                xdt = xh * dt[:, hh:hh + 1]
                eah = ea[:, hh:hh + 1]
                xdt_t = xdt.T
                sc_rows = []
                for b in range(bt):
                    smat = s0_ref[b, hh]
                    sc_rows.append(lax.dot_general(cmb, smat.astype(bf16), _NT,
                                                   preferred_element_type=f32)[b:b + 1, :])
                    s_out[b, hh] = eah[b:b + 1, :] * smat + xdt_t[:, b:b + 1] * bm[b:b + 1, :]
                sc = jnp.concatenate(sc_rows, axis=0)
                ys_s[:, hh * hd:(hh + 1) * hd] = cb_dot * xdt + eah * sc + dskip_ref[:, hh:hh + 1] * xh
        y_ref[...] = _rms(ys_s[...] * (z * jax.nn.sigmoid(z)), gssd_ref[...])


def _ssd_sample(uzx, conv0, gd, conv_w, conv_b, dt_bias, a_log, d_skip, g_ssd, s0, *, layer, row0, s_stack=None):
    heads, hd, ns, groups, width = SSD_HEADS, SSD_HEAD_DIM, SSD_STATE, SSD_GROUPS, SSD_WIDTH
    cch = SSD_CONV_CH
    b = s0.shape[1]
    bt = _SAMPLE_BLOCK
    blk0 = row0 // bt
    fix = lambda l, i: (0, 0)
    grid, row, out_layer = _stacked_state_grid(layer, b // bt, s_stack)
    aliased = s_stack is not None
    body = functools.partial(_ssd_sample_body, heads=heads, hd=hd, ns=ns, groups=groups, width=width,
                             aliased=aliased)
    in_specs = [pl.BlockSpec((bt, uzx.shape[-1]), lambda l, i: (blk0 + row(l, i), 0)),
                pl.BlockSpec((None, bt, SSD_CONV - 1, cch), lambda l, i: (layer, row(l, i), 0, 0)),
                pl.BlockSpec((bt, gd.shape[-1]), lambda l, i: (blk0 + row(l, i), 0)),
                pl.BlockSpec((SSD_CONV, cch), fix), pl.BlockSpec((1, cch), fix),
                pl.BlockSpec((1, heads), fix), pl.BlockSpec((1, heads), fix), pl.BlockSpec((1, heads), fix),
                pl.BlockSpec((1, width), fix),
                pl.BlockSpec((None, bt, heads, hd, ns), lambda l, i: (layer, row(l, i), 0, 0, 0))]
    args = [uzx, conv0, gd, conv_w, conv_b.reshape(1, cch), dt_bias.reshape(1, heads), a_log.reshape(1, heads),
            d_skip.reshape(1, heads), g_ssd.reshape(1, width), s0]
    if aliased:
        in_specs.append(pl.BlockSpec(memory_space=pl.ANY))
        args.append(s_stack)
    return pl.pallas_call(
        body, grid=grid, in_specs=in_specs,
        out_specs=[pl.BlockSpec((bt, width), lambda l, i: (row(l, i), 0)),
                   pl.BlockSpec((None, bt, heads, hd, ns), lambda l, i: (out_layer(l), i, 0, 0, 0))],
        out_shape=[jax.ShapeDtypeStruct((b, width), f32),
                   jax.ShapeDtypeStruct((DEPTH, b, heads, hd, ns), f32)],
        scratch_shapes=[pltpu.VMEM((bt, width), f32)],
        input_output_aliases={10: 1} if aliased else {},
        compiler_params=_params("arbitrary", "arbitrary"), name="ssd_sample",
    )(*args)


def _router_body(h_ref, g_ref, wr_ref, br_ref, cf_ref, idx_ref, gate_ref, *, n_exp):
    cf = _rms(h_ref[...], g_ref[...])
    cf_ref[...] = cf
    logits = jnp.dot(cf, wr_ref[...], precision=HI, preferred_element_type=f32) + br_ref[...]
    lane = lax.broadcasted_iota(jnp.int32, logits.shape, 1)
    m1 = jnp.max(logits, axis=1, keepdims=True)
    i1 = jnp.min(jnp.where(logits == m1, lane, n_exp), axis=1, keepdims=True)
    rest = jnp.where(lane == i1, -jnp.inf, logits)
    m2 = jnp.max(rest, axis=1, keepdims=True)
    i2 = jnp.min(jnp.where(rest == m2, lane, n_exp), axis=1, keepdims=True)
    e2 = jnp.exp(m2 - m1)
    g1 = 1.0 / (1.0 + e2)
    two = lax.broadcasted_iota(jnp.int32, (logits.shape[0], TOP_K), 1)
    idx_ref[...] = jnp.where(two == 0, i1, i2)
    gate_ref[...] = jnp.where(two == 0, g1, e2 * g1)


def _router(h, g_ffn, w_router, b_router, *, tm):
    m, d = h.shape
    e = w_router.shape[-1]
    row = lambda i: (i, 0)
    fix = lambda i: (0, 0)
    return pl.pallas_call(
        functools.partial(_router_body, n_exp=e), grid=(m // tm,),
        in_specs=[pl.BlockSpec((tm, d), row), pl.BlockSpec((1, d), fix), pl.BlockSpec((d, e), fix),
                  pl.BlockSpec((1, e), fix)],
        out_specs=[pl.BlockSpec((tm, d), row), pl.BlockSpec((tm, TOP_K), row),
                   pl.BlockSpec((tm, TOP_K), row)],
        out_shape=[jax.ShapeDtypeStruct((m, d), f32),
                   jax.ShapeDtypeStruct((m, TOP_K), jnp.int32), jax.ShapeDtypeStruct((m, TOP_K), f32)],
        compiler_params=_params("parallel"), name="router",
    )(h, g_ffn.reshape(1, d), w_router, b_router.reshape(1, e))


def _row_copy(src_hbm, dst, sem, src_row, dst_row):
    return pltpu.make_async_copy(src_hbm.at[pl.ds(src_row, 1)], dst.at[pl.ds(dst_row, 1)], sem)


_DMA_UNROLL = 8


def _gather_body(idx_ref, x_hbm, o_ref, buf, sem, *, tg):
    i = pl.program_id(0)

    def issue(tile, slot):
        base = tile * tg

        def start(r, carry):
            _row_copy(x_hbm, buf.at[slot], sem.at[slot], idx_ref[base + r], r).start()
            return carry

        lax.fori_loop(0, tg, start, 0, unroll=_DMA_UNROLL)

    @pl.when(i == 0)
    def _():
        issue(0, 0)

    @pl.when(i + 1 < pl.num_programs(0))
    def _():
        issue(i + 1, (i + 1) % 2)

    slot = i % 2

    def wait(r, carry):
        _row_copy(x_hbm, buf.at[slot], sem.at[slot], 0, r).wait()
        return carry

    lax.fori_loop(0, tg, wait, 0, unroll=_DMA_UNROLL)
    o_ref[...] = buf[slot].astype(o_ref.dtype)


def _gather_rows(x, row_idx, *, tg, out_dtype):
    r = row_idx.shape[0]
    d = x.shape[1]
    return pl.pallas_call(
        functools.partial(_gather_body, tg=tg),
        grid_spec=pltpu.PrefetchScalarGridSpec(
            num_scalar_prefetch=1, grid=(r // tg,),
            in_specs=[pl.BlockSpec(memory_space=pl.ANY)],
            out_specs=pl.BlockSpec((tg, d), lambda i, idx: (i, 0)),
            scratch_shapes=[pltpu.VMEM((2, tg, d), x.dtype), pltpu.SemaphoreType.DMA((2,))]),
        out_shape=jax.ShapeDtypeStruct((r, d), out_dtype),
        compiler_params=_params("arbitrary"), name="gather_rows")(row_idx, x)


def _cast_expert_weight(first_ref, t, w_ref, wb_ref):
    @pl.when(first_ref[t] == 1)
    def _():
        wb_ref[...] = w_ref[0].astype(bf16)


def _gmm_swiglu_body(te_ref, first_ref, nv_ref, a_ref, wg_ref, wu_ref, o_ref, wgb_ref, wub_ref):
    t = pl.program_id(1)

    @pl.when(t < nv_ref[0])
    def _():
        _cast_expert_weight(first_ref, t, wg_ref, wgb_ref)
        _cast_expert_weight(first_ref, t, wu_ref, wub_ref)
        a = a_ref[...]
        g = jnp.dot(a, wgb_ref[...], preferred_element_type=f32)
        u = jnp.dot(a, wub_ref[...], preferred_element_type=f32)
        o_ref[...] = (g * jax.nn.sigmoid(g) * u).astype(o_ref.dtype)

    @pl.when(t >= nv_ref[0])
    def _():
        o_ref[...] = jnp.zeros_like(o_ref)


def _gmm_down_body(te_ref, first_ref, nv_ref, a_ref, w_ref, o_ref, wb_ref):
    t = pl.program_id(1)

    @pl.when(t < nv_ref[0])
    def _():
        _cast_expert_weight(first_ref, t, w_ref, wb_ref)
        o_ref[...] = jnp.dot(a_ref[...], wb_ref[...], preferred_element_type=f32)

    @pl.when(t >= nv_ref[0])
    def _():
        o_ref[...] = jnp.zeros_like(o_ref)


def _gmm_swiglu(x_sorted, wg, wu, tile_expert, tile_first, n_valid, *, tm, tn):
    r, k = x_sorted.shape
    n = wg.shape[-1]
    wspec = pl.BlockSpec((1, k, tn), lambda j, t, te, fi, nv: (te[t], 0, j))
    return pl.pallas_call(
        _gmm_swiglu_body,
        grid_spec=pltpu.PrefetchScalarGridSpec(
            num_scalar_prefetch=3, grid=(n // tn, r // tm),
            in_specs=[pl.BlockSpec((tm, k), lambda j, t, te, fi, nv: (t, 0)), wspec, wspec],
            out_specs=pl.BlockSpec((tm, tn), lambda j, t, te, fi, nv: (t, j)),
            scratch_shapes=[pltpu.VMEM((k, tn), bf16), pltpu.VMEM((k, tn), bf16)]),
        out_shape=jax.ShapeDtypeStruct((r, n), bf16),
        compiler_params=_params("arbitrary", "arbitrary"), name="moe_gate_up",
    )(tile_expert, tile_first, n_valid, x_sorted, wg, wu)


def _gmm_down(h_sorted, wd, tile_expert, tile_first, n_valid, *, tm, tn):
    r, k = h_sorted.shape
    n = wd.shape[-1]
    return pl.pallas_call(
        _gmm_down_body,
        grid_spec=pltpu.PrefetchScalarGridSpec(
            num_scalar_prefetch=3, grid=(n // tn, r // tm),
            in_specs=[pl.BlockSpec((tm, k), lambda j, t, te, fi, nv: (t, 0)),
                      pl.BlockSpec((1, k, tn), lambda j, t, te, fi, nv: (te[t], 0, j))],
            out_specs=pl.BlockSpec((tm, tn), lambda j, t, te, fi, nv: (t, j)),
            scratch_shapes=[pltpu.VMEM((k, tn), bf16)]),
        out_shape=jax.ShapeDtypeStruct((r, n), f32),
        compiler_params=_params("arbitrary", "arbitrary"), name="moe_down",
    )(tile_expert, tile_first, n_valid, h_sorted, wd)


def _combine_body(pos_ref, y_hbm, gate_ref, r_ref, o_ref, buf, sem, *, tc):
    i = pl.program_id(0)

    def issue(tile, slot):
        base = tile * tc

        def start(r, carry):
            for kk in range(TOP_K):
                _row_copy(y_hbm, buf.at[slot, kk], sem.at[slot], pos_ref[(base + r) * TOP_K + kk], r).start()
            return carry

        lax.fori_loop(0, tc, start, 0, unroll=_DMA_UNROLL)

    @pl.when(i == 0)
    def _():
        issue(0, 0)

    @pl.when(i + 1 < pl.num_programs(0))
    def _():
        issue(i + 1, (i + 1) % 2)

    slot = i % 2

    def wait(r, carry):
        for kk in range(TOP_K):
            _row_copy(y_hbm, buf.at[slot, kk], sem.at[slot], 0, r).wait()
        return carry

    lax.fori_loop(0, tc, wait, 0, unroll=_DMA_UNROLL)
    gate = gate_ref[...]
    o_ref[...] = r_ref[...] + gate[:, 0:1] * buf[slot, 0] + gate[:, 1:2] * buf[slot, 1]


def _combine(y_sorted, pos_flat, gates, resid, *, tc):
    m, d = resid.shape
    return pl.pallas_call(
        functools.partial(_combine_body, tc=tc),
        grid_spec=pltpu.PrefetchScalarGridSpec(
            num_scalar_prefetch=1, grid=(m // tc,),
            in_specs=[pl.BlockSpec(memory_space=pl.ANY),
                      pl.BlockSpec((tc, TOP_K), lambda i, pos: (i, 0)),
                      pl.BlockSpec((tc, d), lambda i, pos: (i, 0))],
            out_specs=pl.BlockSpec((tc, d), lambda i, pos: (i, 0)),
            scratch_shapes=[pltpu.VMEM((2, TOP_K, tc, d), f32), pltpu.SemaphoreType.DMA((2,))]),
        out_shape=jax.ShapeDtypeStruct((m, d), f32),
        compiler_params=_params("arbitrary"), name="moe_combine")(pos_flat, y_sorted, gates, resid)


def _routing_tables(top_i, n_exp, tm, n_tiles):
    m = top_i.shape[0]
    e_flat = top_i.reshape(-1)
    onehot = (e_flat[:, None] == jnp.arange(n_exp, dtype=jnp.int32)[None, :]).astype(jnp.int32)
    rank = jnp.take_along_axis(jnp.cumsum(onehot, axis=0), e_flat[:, None], axis=1)[:, 0] - 1
    counts = jnp.sum(onehot, axis=0)
    tiles_per = (counts + tm - 1) // tm
    tile_end = jnp.cumsum(tiles_per)
    tile_start = tile_end - tiles_per
    pos = tile_start[e_flat] * tm + rank
    token = jnp.arange(m * TOP_K, dtype=jnp.int32) // TOP_K
    row_token = jnp.zeros((n_tiles * tm,), jnp.int32).at[pos].set(token)
    n_valid = tile_end[-1]
    tid = jnp.minimum(jnp.arange(n_tiles, dtype=jnp.int32), n_valid - 1)
    tile_expert = jnp.sum((tid[:, None] >= tile_end[None, :]).astype(jnp.int32), axis=1)
    tile_first = jnp.concatenate([jnp.ones((1,), jnp.int32),
                                  (tile_expert[1:] != tile_expert[:-1]).astype(jnp.int32)])
    return (pos.astype(jnp.int32), row_token, tile_expert.astype(jnp.int32), tile_first,
            n_valid.reshape(1).astype(jnp.int32))


def _moe_ffn(h, g_ffn, w_router, b_router, wg, wu, wd, *, tm_tok, tm, tn_up, tn_down):
    m = h.shape[0]
    n_exp = wg.shape[0]
    c_f32, top_i, top_g = _router(h, g_ffn, w_router, b_router, tm=tm_tok)
    n_tiles = (m * TOP_K) // tm + n_exp
    pos, row_token, tile_expert, tile_first, n_valid = _routing_tables(top_i, n_exp, tm, n_tiles)
    x_sorted = _gather_rows(c_f32, row_token, tg=tm, out_dtype=bf16)
    h_sorted = _gmm_swiglu(x_sorted, wg, wu, tile_expert, tile_first, n_valid, tm=tm, tn=tn_up)
    y_sorted = _gmm_down(h_sorted, wd, tile_expert, tile_first, n_valid, tm=tm, tn=tn_down)
    return _combine(y_sorted, pos, top_g, h, tc=tm_tok // 2)


def kernel(x_prompt, x_sample, state_mlstm_C, state_mlstm_n, state_mlstm_m, state_s5_re, state_s5_im, state_ssd, cache_conv, p_prompt, p_sample, g_mix, w_in, b_igate, b_fgate, g_ml, s5_lam_re, s5_lam_im, s5_log_dt, s5_b_re, s5_b_im, s5_c_re, s5_c_im, s5_d, s5_w_glu, s5_b_glu, g_s5, ssd_conv_w, ssd_conv_b, ssd_dt_bias, ssd_a_log, ssd_d, g_ssd, w_out, g_ffn, ffn_w_gate, ffn_w_up, ffn_w_down, w_router, b_router, moe_w_gate, moe_w_up, moe_w_down, g_ple, w_ple, w_ple_gate, g_final):
    bp, seq, d = x_prompt.shape
    bs = x_sample.shape[0]
    tp = bp * seq
    m = tp + bs
    tm = _token_tile(tp, bs)
    heads = ML_HEADS
    t5 = math.gcd(seq, S5_CHUNK)
    nc5 = seq // t5
    cl = math.gcd(seq, CHUNK)
    nc = seq // cl

    p_p = p_prompt.reshape(DEPTH, tp, -1)
    p_s = p_sample.reshape(DEPTH, bs, -1)
    zeros5 = jnp.zeros((S5_GROUPS, bp, S5_STATE), f32)
    w_in_t = jnp.swapaxes(w_in, 1, 2)
    outs_p = [[] for _ in range(7)]
    outs_s = [[] for _ in range(5)]
    c_stack = s_stack = None

    for i in range(DEPTH):
        if i == 0:
            h, a = _rmsnorm_in(x_prompt.reshape(tp, d), x_sample.reshape(bs, d), g_mix[i], tm)
        else:
            a = _rmsnorm(h, g_mix[i], bf16, tm)
        qkvo = _mm_nt(a, w_in_t, tm=tm, tn=512, layer=i, row_off=0, n_rows=_OFF_GATES, name="in_proj_qkvo")
        uzx = _mm_nt(a, w_in_t, tm=tm, tn=512, layer=i, row_off=_OFF_U, n_rows=_OFF_DT - _OFF_U, name="in_proj_uzx")
        gd = _mm_gates(a, w_in_t, tm=tm, layer=i, off1=_OFF_GATES, n1=2 * heads, off2=_OFF_DT, n2=SSD_HEADS,
                       name="in_proj_gates")

        gates_t = jnp.transpose(gd[:tp, :2 * heads].reshape(bp, nc, cl, 2 * heads), (0, 1, 3, 2))
        bias8 = jnp.concatenate([b_igate[i], b_fgate[i]])
        h_ml_p, c_p, n_p, m_p = _mlstm_prompt(qkvo, gates_t, bias8.reshape(2 * heads, 1), g_ml[i], batch=bp, seq=seq)
        h_ml_s, c_stack, n_s, m_s = _mlstm_sample(qkvo, gd, bias8.reshape(1, 2 * heads), g_ml[i], state_mlstm_C,
                                                  state_mlstm_n, state_mlstm_m, layer=i, row0=tp, c_stack=c_stack)

        mats_p = _s5_matrices(s5_lam_re[i], s5_lam_im[i], s5_log_dt[i], s5_b_re[i], s5_b_im[i],
                              s5_c_re[i], s5_c_im[i], t5)
        mats_s = _s5_matrices(s5_lam_re[i], s5_lam_im[i], s5_log_dt[i], s5_b_re[i], s5_b_im[i],
                              s5_c_re[i], s5_c_im[i], 1)
        u_p = uzx[:tp, :S5_WIDTH].reshape(bp, nc5, t5, S5_GROUPS, S5_CH)
        u_p = jnp.transpose(u_p, (3, 1, 0, 2, 4)).reshape(S5_GROUPS, nc5 * bp, t5 * S5_CH).astype(bf16)
        y_p, s5re_p, s5im_p = _s5_scan(u_p, mats_p, zeros5, zeros5, nc=nc5, batch=bp)
        y_p = jnp.transpose(y_p.reshape(S5_GROUPS, nc5, bp, t5, S5_CH), (2, 1, 3, 0, 4)).reshape(tp, S5_WIDTH)
        u_s = jnp.transpose(uzx[tp:, :S5_WIDTH].reshape(bs, S5_GROUPS, S5_CH), (1, 0, 2)).astype(bf16)
        y_s, s5re_s, s5im_s = _s5_scan(u_s, mats_s, jnp.swapaxes(state_s5_re[i], 0, 1),
                                       jnp.swapaxes(state_s5_im[i], 0, 1), nc=1, batch=bs)
        y_s = jnp.transpose(y_s, (1, 0, 2)).reshape(bs, S5_WIDTH)
        y5 = _s5_glu(jnp.concatenate([y_p, y_s], axis=0), uzx, s5_d[i].reshape(1, S5_WIDTH), s5_w_glu,
                     s5_b_glu[i].reshape(1, S5_WIDTH), g_s5[i].reshape(1, S5_WIDTH), layer=i, tm=tm)

        dt_t = jnp.transpose(gd[:tp, 2 * heads:2 * heads + SSD_HEADS].reshape(bp, nc, cl, SSD_HEADS), (0, 1, 3, 2))
        y_ssd_p, ssd_p = _ssd_prompt(uzx, gd, dt_t, ssd_conv_w[i], ssd_conv_b[i], ssd_dt_bias[i], ssd_a_log[i],
                                     ssd_d[i], g_ssd[i], batch=bp, seq=seq)
        y_ssd_s, s_stack = _ssd_sample(uzx, cache_conv, gd, ssd_conv_w[i], ssd_conv_b[i], ssd_dt_bias[i],
                                       ssd_a_log[i], ssd_d[i], g_ssd[i], state_ssd, layer=i, row0=tp,
                                       s_stack=s_stack)
        xbc = uzx[:, S5_WIDTH + SSD_WIDTH:]
        conv_p = xbc[:tp].reshape(bp, seq, -1)[:, seq - (SSD_CONV - 1):]
        conv_s = jnp.concatenate([cache_conv[i][:, 1:], xbc[tp:].reshape(bs, 1, -1)], axis=1)

        for lst, s in zip(outs_p, (c_p, n_p.reshape(bp, heads, ML_DK), m_p.reshape(bp, heads),
                                   jnp.swapaxes(s5re_p, 0, 1), jnp.swapaxes(s5im_p, 0, 1), ssd_p, conv_p)):
            lst.append(s)
        for lst, s in zip(outs_s, (n_s, m_s, jnp.swapaxes(s5re_s, 0, 1), jnp.swapaxes(s5im_s, 0, 1), conv_s)):
            lst.append(s)

        h = _mm_mix(h_ml_p, h_ml_s, y5, y_ssd_p, y_ssd_s, w_out, h, tm=tm, tn=512, layer=i, name="out_proj")

        j = i // 2
        if i % 2 == 0:
            cn = _rmsnorm(h, g_ffn[i], bf16, tm)
            hid = _mm_swiglu(cn, ffn_w_gate, ffn_w_up, tm=tm, tn=512, layer=j, name="ffn_gate_up")
            h = _mm(hid, ffn_w_down, tm=tm // 2, tn=512, layer=j, resid=h, name="ffn_down")
        else:
            n_moe = moe_w_gate.shape[0]
            sel = lambda w: w.reshape((n_moe * N_EXPERTS,) + w.shape[2:])[j * N_EXPERTS:(j + 1) * N_EXPERTS] if n_moe > 1 else w.reshape(w.shape[1:])
            h = _moe_ffn(h, g_ffn[i], w_router[j], b_router[j], sel(moe_w_gate), sel(moe_w_up), sel(moe_w_down),
                         tm_tok=tm, tm=256, tn_up=1024, tn_down=512)

        e = _rmsnorm(h, g_ple[i], bf16, tm)
        h = _mm_ple(e, p_p, p_s, w_ple_gate, w_ple, h, tm=tm, tn=512, layer=i)

    y_p, y_s = _rmsnorm_out(h, g_final, tp, tm)
    n_s, m_s, s5re_s, s5im_s, conv_s = (jnp.stack(l) for l in outs_s)
    return ((y_p.reshape(bp, seq, d), y_s.reshape(bs, 1, d)) + tuple(jnp.stack(l) for l in outs_p)
            + (c_stack, n_s, m_s, s5re_s, s5im_s, s_stack, conv_s))
```

```python
import functools
import math

import jax
import jax.numpy as jnp
from jax import lax
from jax.experimental import pallas as pl
from jax.experimental.pallas import tpu as pltpu

f32 = jnp.float32
bf16 = jnp.bfloat16
HI = lax.Precision.HIGHEST

D_MODEL = 2048
DEPTH = 2
ML_HEADS = 4
ML_DK = 256
ML_DV = 256
ML_WIDTH = ML_HEADS * ML_DV
S5_CH = 16
S5_WIDTH = 512
S5_GROUPS = 32
S5_STATE = 64
SSD_HEAD_DIM = 64
SSD_WIDTH = 512
SSD_HEADS = 8
SSD_GROUPS = 2
SSD_STATE = 128
SSD_CONV = 4
SSD_CONV_CH = 1024
CHUNK = 64
S5_CHUNK = 16
N_EXPERTS = 8
TOP_K = 2
RMS_EPS = 1e-6

_OFF_GATES = 4 * ML_WIDTH
_OFF_U = _OFF_GATES + 2 * ML_HEADS
_OFF_DT = _OFF_U + S5_WIDTH + SSD_WIDTH + SSD_CONV_CH

_VMEM_LIMIT = 56 * 1024 * 1024
_NT = (((1,), (1,)), ((), ()))
_TN = (((0,), (0,)), ((), ()))


def _params(*sem):
    return pltpu.CompilerParams(dimension_semantics=sem, vmem_limit_bytes=_VMEM_LIMIT)


def _rms(x, g):
    return x * lax.rsqrt(jnp.mean(x * x, axis=-1, keepdims=True) + RMS_EPS) * g


def _rmsnorm_body(x_ref, g_ref, o_ref):
    o_ref[...] = _rms(x_ref[...], g_ref[...]).astype(o_ref.dtype)


def _last_tile(p_tile, s_rows, n_keep):
    return jnp.concatenate([p_tile[0:n_keep, :], s_rows], axis=0)


def _split_rows(tp, bs, tm):
    n_tiles = (tp + bs) // tm
    n_keep = tp - (n_tiles - 1) * tm
    assert n_tiles * tm == tp + bs and 0 < n_keep and n_keep + bs == tm and n_keep % 16 == 0
    return n_tiles, n_keep


def _token_tile(tp, bs):
    m = tp + bs
    for n_tiles in (10, 8, 5, 4, 2, 1):
        tm = m // n_tiles
        if m % n_tiles == 0 and tm % 32 == 0 and bs < tm and (tp - (n_tiles - 1) * tm) % 16 == 0:
            return tm
    raise ValueError("no row tiling for these token counts")


def _rmsnorm_in_body(xp_ref, xs_ref, g_ref, h_ref, a_ref, *, n_keep):
    last = pl.num_programs(0) - 1

    def run(x):
        h_ref[...] = x
        a_ref[...] = _rms(x, g_ref[...]).astype(a_ref.dtype)

    @pl.when(pl.program_id(0) < last)
    def _():
        run(xp_ref[...])

    @pl.when(pl.program_id(0) == last)
    def _():
        run(_last_tile(xp_ref, xs_ref[...], n_keep))


def _rmsnorm_in(xp, xs, g, tm):
    tp, d = xp.shape
    bs = xs.shape[0]
    n_tiles, n_keep = _split_rows(tp, bs, tm)
    row = lambda i: (i, 0)
    fix = lambda i: (0, 0)
    return pl.pallas_call(
        functools.partial(_rmsnorm_in_body, n_keep=n_keep), grid=(n_tiles,),
        in_specs=[pl.BlockSpec((tm, d), row), pl.BlockSpec((bs, d), fix), pl.BlockSpec((1, d), fix)],
        out_specs=[pl.BlockSpec((tm, d), row), pl.BlockSpec((tm, d), row)],
        out_shape=[jax.ShapeDtypeStruct((tp + bs, d), f32), jax.ShapeDtypeStruct((tp + bs, d), bf16)],
        compiler_params=_params("parallel"), name="rmsnorm_in")(xp, xs, g.reshape(1, d))


def _rmsnorm_out_body(x_ref, g_ref, yp_ref, ys_ref, *, n_keep):
    y = _rms(x_ref[...], g_ref[...])
    yp_ref[...] = y

    @pl.when(pl.program_id(0) == pl.num_programs(0) - 1)
    def _():
        ys_ref[...] = y[n_keep:, :]


def _rmsnorm_out(x, g, tp, tm):
    m, d = x.shape
    bs = m - tp
    n_tiles, n_keep = _split_rows(tp, bs, tm)
    return pl.pallas_call(
        functools.partial(_rmsnorm_out_body, n_keep=n_keep), grid=(n_tiles,),
        in_specs=[pl.BlockSpec((tm, d), lambda i: (i, 0)), pl.BlockSpec((1, d), lambda i: (0, 0))],
        out_specs=[pl.BlockSpec((tm, d), lambda i: (i, 0)), pl.BlockSpec((bs, d), lambda i: (0, 0))],
        out_shape=[jax.ShapeDtypeStruct((tp, d), f32), jax.ShapeDtypeStruct((bs, d), f32)],
        compiler_params=_params("arbitrary"), name="rmsnorm_out")(x, g.reshape(1, d))


def _rmsnorm(x, g, out_dtype, tm):
    m, d = x.shape
    return pl.pallas_call(
        _rmsnorm_body, grid=(m // tm,),
        in_specs=[pl.BlockSpec((tm, d), lambda i: (i, 0)), pl.BlockSpec((1, d), lambda i: (0, 0))],
        out_specs=pl.BlockSpec((tm, d), lambda i: (i, 0)),
        out_shape=jax.ShapeDtypeStruct((m, d), out_dtype),
        compiler_params=_params("parallel"), name="rmsnorm")(x, g.reshape(1, d))


def _cast_weight_once(w_ref, wb_ref):
    @pl.when(pl.program_id(1) == 0)
    def _():
        wb_ref[...] = w_ref[...].astype(bf16)


def _mm_plain_body(a_ref, w_ref, o_ref, wb_ref):
    _cast_weight_once(w_ref, wb_ref)
    o_ref[...] = jnp.dot(a_ref[...], wb_ref[...], preferred_element_type=f32).astype(o_ref.dtype)


def _mm_resid_body(a_ref, w_ref, r_ref, o_ref, wb_ref):
    _cast_weight_once(w_ref, wb_ref)
    o_ref[...] = r_ref[...] + jnp.dot(a_ref[...], wb_ref[...], preferred_element_type=f32)


def _mm_swiglu_body(a_ref, wg_ref, wu_ref, o_ref, wgb_ref, wub_ref):
    _cast_weight_once(wg_ref, wgb_ref)
    _cast_weight_once(wu_ref, wub_ref)
    a = a_ref[...]
    g = jnp.dot(a, wgb_ref[...], preferred_element_type=f32)
    u = jnp.dot(a, wub_ref[...], preferred_element_type=f32)
    o_ref[...] = (g * jax.nn.sigmoid(g) * u).astype(o_ref.dtype)


def _mm_ple_body(e_ref, pp_ref, ps_ref, wg_ref, wp_ref, r_ref, o_ref, wgb_ref, wpb_ref, *, n_keep):
    _cast_weight_once(wg_ref, wgb_ref)
    _cast_weight_once(wp_ref, wpb_ref)
    last = pl.num_programs(1) - 1

    def run(p):
        gate = jnp.dot(e_ref[...], wgb_ref[...], preferred_element_type=f32)
        emb = jnp.dot(p.astype(bf16), wpb_ref[...], preferred_element_type=f32)
        o_ref[...] = r_ref[...] + emb * jax.nn.sigmoid(gate)

    @pl.when(pl.program_id(1) < last)
    def _():
        run(pp_ref[...])

    @pl.when(pl.program_id(1) == last)
    def _():
        run(_last_tile(pp_ref, ps_ref[...], n_keep))


def _mm_mix_body(a1p_ref, a1s_ref, a2_ref, a3p_ref, a3s_ref, w_ref, r_ref, o_ref, wb_ref, *, n_keep):
    _cast_weight_once(w_ref, wb_ref)
    last = pl.num_programs(1) - 1
    k1 = a1p_ref.shape[1]
    k2 = a2_ref.shape[1]

    def run(a1, a3):
        acc = jnp.dot(a1, wb_ref[0:k1, :], preferred_element_type=f32)
        acc += jnp.dot(a2_ref[...], wb_ref[k1:k1 + k2, :], preferred_element_type=f32)
        acc += jnp.dot(a3, wb_ref[k1 + k2:, :], preferred_element_type=f32)
        o_ref[...] = r_ref[...] + acc

    @pl.when(pl.program_id(1) < last)
    def _():
        run(a1p_ref[...], a3p_ref[...])

    @pl.when(pl.program_id(1) == last)
    def _():
        run(_last_tile(a1p_ref, a1s_ref[...].astype(bf16), n_keep),
            _last_tile(a3p_ref, a3s_ref[...].astype(bf16), n_keep))


def _mm_mix(a1p, a1s, a2, a3p, a3s, w, resid, *, tm, tn, layer, name):
    m, n = resid.shape
    tp, k1 = a1p.shape
    bs = a1s.shape[0]
    k2, k3 = a2.shape[1], a3p.shape[1]
    n_tiles, n_keep = _split_rows(tp, bs, tm)
    rows = lambda j, i: (i, 0)
    fix = lambda j, i: (0, 0)
    o_spec = pl.BlockSpec((tm, tn), lambda j, i: (i, j))
    return pl.pallas_call(
        functools.partial(_mm_mix_body, n_keep=n_keep), grid=(n // tn, n_tiles),
        in_specs=[pl.BlockSpec((tm, k1), rows), pl.BlockSpec((bs, k1), fix), pl.BlockSpec((tm, k2), rows),
                  pl.BlockSpec((tm, k3), rows), pl.BlockSpec((bs, k3), fix),
                  _wspec(w, layer, k1 + k2 + k3, tn), o_spec],
        out_specs=o_spec, out_shape=jax.ShapeDtypeStruct((m, n), f32),
        scratch_shapes=[pltpu.VMEM((k1 + k2 + k3, tn), bf16)],
        compiler_params=_params("arbitrary", "arbitrary"), name=name)(a1p, a1s, a2, a3p, a3s, w, resid)


def _wspec(w, layer, k, tn, col_block_off=0):
    if w.ndim == 2:
        return pl.BlockSpec((k, tn), lambda j, i: (0, j + col_block_off))
    return pl.BlockSpec((None, k, tn), lambda j, i: (layer, 0, j + col_block_off))


def _mm(a, w, *, tm, tn, layer=0, n_cols=None, col_off=0, resid=None, out_dtype=f32, name="mm"):
    m, k = a.shape
    n = n_cols if n_cols is not None else w.shape[-1]
    grid = (n // tn, m // tm)
    a_spec = pl.BlockSpec((tm, k), lambda j, i: (i, 0))
    o_spec = pl.BlockSpec((tm, tn), lambda j, i: (i, j))
    w_spec = _wspec(w, layer, k, tn, col_off // tn)
    scratch = [pltpu.VMEM((k, tn), bf16)]
    if resid is None:
        body, ins, specs = _mm_plain_body, (a, w), [a_spec, w_spec]
    else:
        body, ins, specs = _mm_resid_body, (a, w, resid), [a_spec, w_spec, o_spec]
    return pl.pallas_call(
        body, grid=grid, in_specs=specs, out_specs=o_spec,
        out_shape=jax.ShapeDtypeStruct((m, n), out_dtype), scratch_shapes=scratch,
        compiler_params=_params("arbitrary", "arbitrary"), name=name)(*ins)


def _mm_nt_body(a_ref, wt_ref, o_ref, wb_ref):
    @pl.when(pl.program_id(1) == 0)
    def _():
        wb_ref[...] = wt_ref[0].T.astype(bf16)

    o_ref[...] = jnp.dot(a_ref[...], wb_ref[...], preferred_element_type=f32)


def _wt_rows(layer, k, n_rows, row_of):
    return pl.BlockSpec((pl.Element(1), pl.Element(n_rows), pl.Element(k)),
                        lambda *idx: (layer, row_of(*idx), 0))


def _mm_nt(a, wt, *, tm, tn, layer, row_off, n_rows, name):
    m, k = a.shape
    return pl.pallas_call(
        _mm_nt_body, grid=(n_rows // tn, m // tm),
        in_specs=[pl.BlockSpec((tm, k), lambda j, i: (i, 0)),
                  _wt_rows(layer, k, tn, lambda j, i: pl.multiple_of(row_off + tn * j, 8))],
        out_specs=pl.BlockSpec((tm, tn), lambda j, i: (i, j)),
        out_shape=jax.ShapeDtypeStruct((m, n_rows), f32), scratch_shapes=[pltpu.VMEM((k, tn), bf16)],
        compiler_params=_params("arbitrary", "arbitrary"), name=name)(a, wt)


def _mm_gates_body(a_ref, w1_ref, w2_ref, o_ref):
    a = a_ref[...]
    n1 = w1_ref.shape[1]
    o_ref[:, 0:n1] = lax.dot_general(a, w1_ref[0].astype(bf16), _NT, preferred_element_type=f32)
    o_ref[:, n1:] = lax.dot_general(a, w2_ref[0].astype(bf16), _NT, preferred_element_type=f32)


def _mm_gates(a, wt, *, tm, layer, off1, n1, off2, n2, name):
    m, k = a.shape
    return pl.pallas_call(
        _mm_gates_body, grid=(m // tm,),
        in_specs=[pl.BlockSpec((tm, k), lambda i: (i, 0)),
                  _wt_rows(layer, k, n1, lambda i: off1), _wt_rows(layer, k, n2, lambda i: off2)],
        out_specs=pl.BlockSpec((tm, n1 + n2), lambda i: (i, 0)),
        out_shape=jax.ShapeDtypeStruct((m, n1 + n2), f32),
        compiler_params=_params("parallel"), name=name)(a, wt, wt)


def _mm_swiglu(a, wg, wu, *, tm, tn, layer=0, name="mm_swiglu"):
    m, k = a.shape
    n = wg.shape[-1]
    a_spec = pl.BlockSpec((tm, k), lambda j, i: (i, 0))
    o_spec = pl.BlockSpec((tm, tn), lambda j, i: (i, j))
    return pl.pallas_call(
        _mm_swiglu_body, grid=(n // tn, m // tm),
        in_specs=[a_spec, _wspec(wg, layer, k, tn), _wspec(wu, layer, k, tn)], out_specs=o_spec,
        out_shape=jax.ShapeDtypeStruct((m, n), bf16),
        scratch_shapes=[pltpu.VMEM((k, tn), bf16), pltpu.VMEM((k, tn), bf16)],
        compiler_params=_params("arbitrary", "arbitrary"), name=name)(a, wg, wu)


def _mm_ple(e, p_prompt, p_sample, w_gate, w_ple, resid, *, tm, tn, layer, name="mm_ple"):
    m, k = e.shape
    _, tp, kp = p_prompt.shape
    bs = p_sample.shape[1]
    n = w_gate.shape[-1]
    n_tiles, n_keep = _split_rows(tp, bs, tm)
    o_spec = pl.BlockSpec((tm, tn), lambda j, i: (i, j))
    return pl.pallas_call(
        functools.partial(_mm_ple_body, n_keep=n_keep), grid=(n // tn, n_tiles),
        in_specs=[pl.BlockSpec((tm, k), lambda j, i: (i, 0)),
                  pl.BlockSpec((None, tm, kp), lambda j, i: (layer, i, 0)),
                  pl.BlockSpec((None, bs, kp), lambda j, i: (layer, 0, 0)),
                  _wspec(w_gate, layer, k, tn), _wspec(w_ple, layer, kp, tn), o_spec],
        out_specs=o_spec, out_shape=jax.ShapeDtypeStruct((m, n), f32),
        scratch_shapes=[pltpu.VMEM((k, tn), bf16), pltpu.VMEM((kp, tn), bf16)],
        compiler_params=_params("arbitrary", "arbitrary"), name=name)(e, p_prompt, p_sample, w_gate, w_ple, resid)


def _col_from_row(row, eye):
    n = row.shape[1]
    return jnp.sum(jnp.where(eye, jnp.broadcast_to(row, (n, n)), 0.0), axis=1, keepdims=True)


def _mlstm_prompt_body(q_ref, k_ref, v_ref, o_ref, gt_ref, bias_ref, gml_ref,
                       h_ref, c_out, n_out, m_out, c_s, n_s, m_s, *, heads, dk, dv, cl):
    c = pl.program_id(1)

    @pl.when(c == 0)
    def _():
        c_s[...] = jnp.zeros_like(c_s)
        n_s[...] = jnp.zeros_like(n_s)
        m_s[...] = jnp.zeros_like(m_s)

    row = lax.broadcasted_iota(jnp.int32, (cl, cl), 0)
    col = lax.broadcasted_iota(jnp.int32, (cl, cl), 1)
    causal = col <= row
    eye = col == row
    triu = (row <= col).astype(f32)
    gt = gt_ref[0, 0] + bias_ref[...]
    lf = jax.nn.log_sigmoid(gt[heads:2 * heads])
    b_all = jnp.dot(lf, triu, precision=HI, preferred_element_type=f32)
    for hh in range(heads):
        ig = gt[hh:hh + 1]
        b_row = b_all[hh:hh + 1]
        b_col = _col_from_row(b_row, eye)
        m_prev = m_s[hh]
        d = jnp.where(causal, b_col - b_row + ig, -jnp.inf)
        inter = b_col + m_prev
        m_t = jnp.maximum(inter, jnp.max(d, axis=1, keepdims=True))
        w = jnp.exp(d - m_t)
        g = jnp.exp(inter - m_t)
        q = q_ref[:, hh * dk:(hh + 1) * dk]
        k = k_ref[:, hh * dk:(hh + 1) * dk] * (dk ** -0.5)
        vb = v_ref[:, hh * dv:(hh + 1) * dv].astype(bf16)
        qb = q.astype(bf16)
        cmat = c_s[hh]
        n_row = n_s[hh]
        s = lax.dot_general(qb, k.astype(bf16), _NT, preferred_element_type=f32) * w
        num = (jnp.dot(s.astype(bf16), vb, preferred_element_type=f32)
               + g * jnp.dot(qb, cmat.astype(bf16), preferred_element_type=f32))
        den = jnp.sum(s, axis=1, keepdims=True) + g * jnp.sum(q * n_row, axis=1, keepdims=True)
        hraw = num / jnp.maximum(jnp.abs(den), jnp.exp(-m_t))
        hn = _rms(hraw, gml_ref[hh:hh + 1, :])
        ogate = jax.nn.sigmoid(o_ref[:, hh * dv:(hh + 1) * dv])
        h_ref[:, hh * dv:(hh + 1) * dv] = (ogate * hn).astype(h_ref.dtype)
        b_last = b_row[:, cl - 1:cl]
        dl = b_last - b_row + ig
        m_new = jnp.maximum(b_last + m_prev, jnp.max(dl, axis=1, keepdims=True))
        ws_col = _col_from_row(jnp.exp(dl - m_new), eye)
        gl = jnp.exp(b_last + m_prev - m_new)
        kw = k * ws_col
        c_s[hh] = gl * cmat + lax.dot_general(kw.astype(bf16), vb, _TN, preferred_element_type=f32)
        n_s[hh] = gl * n_row + jnp.sum(kw, axis=0, keepdims=True)
        m_s[hh] = m_new

    @pl.when(c == pl.num_programs(1) - 1)
    def _():
        c_out[0] = c_s[...]
        n_out[0] = n_s[...]
        m_out[0] = m_s[...]


def _mlstm_prompt(qkvo, gates_t, bias_col, g_ml, *, batch, seq):
    heads, dk, dv, cl = ML_HEADS, ML_DK, ML_DV, math.gcd(seq, CHUNK)
    nc = seq // cl
    wq = heads * dk
    rows = lambda b, c: (b * nc + c, 0)
    body = functools.partial(_mlstm_prompt_body, heads=heads, dk=dk, dv=dv, cl=cl)
    return pl.pallas_call(
        body, grid=(batch, nc),
        in_specs=[pl.BlockSpec((cl, wq), lambda b, c: (b * nc + c, 0)),
                  pl.BlockSpec((cl, wq), lambda b, c: (b * nc + c, 1)),
                  pl.BlockSpec((cl, wq), lambda b, c: (b * nc + c, 2)),
                  pl.BlockSpec((cl, wq), lambda b, c: (b * nc + c, 3)),
                  pl.BlockSpec((1, 1, 2 * heads, cl), lambda b, c: (b, c, 0, 0)),
                  pl.BlockSpec((2 * heads, 1), lambda b, c: (0, 0)),
                  pl.BlockSpec((heads, dv), lambda b, c: (0, 0))],
        out_specs=[pl.BlockSpec((cl, heads * dv), rows),
                   pl.BlockSpec((1, heads, dk, dv), lambda b, c: (b, 0, 0, 0)),
                   pl.BlockSpec((1, heads, 1, dk), lambda b, c: (b, 0, 0, 0)),
                   pl.BlockSpec((1, heads, 1, 1), lambda b, c: (b, 0, 0, 0))],
        out_shape=[jax.ShapeDtypeStruct((batch * seq, heads * dv), bf16),
                   jax.ShapeDtypeStruct((batch, heads, dk, dv), f32),
                   jax.ShapeDtypeStruct((batch, heads, 1, dk), f32),
                   jax.ShapeDtypeStruct((batch, heads, 1, 1), f32)],
        scratch_shapes=[pltpu.VMEM((heads, dk, dv), f32), pltpu.VMEM((heads, 1, dk), f32),
                        pltpu.VMEM((heads, 1, 1), f32)],
        compiler_params=_params("arbitrary", "arbitrary"), name="mlstm_prompt",
    )(qkvo, qkvo, qkvo, qkvo, gates_t, bias_col, g_ml)


def _lane_pick(pieces, width):
    lane = lax.broadcasted_iota(jnp.int32, (1, width), 1)
    out = jnp.zeros((1, width), f32)
    for i, p in enumerate(pieces):
        out = jnp.where(lane == i, p, out)
    return out


_SAMPLE_BLOCK = 8


def _mlstm_sample_body(*refs, heads, dk, dv, aliased):
    if aliased:
        refs = refs[:10] + refs[11:]
    (q_ref, k_ref, v_ref, o_ref, gd_ref, bias_ref, gml_ref, c0_ref, n0_ref, m0_ref,
     h_ref, c_out, n_out, m_out) = refs
    bt = q_ref.shape[0]

    @pl.when(pl.program_id(0) > 0)
    def _():
        c_out[...] = jnp.zeros_like(c_out)

    @pl.when(pl.program_id(0) == 0)
    def _():
        g8 = gd_ref[:, 0:2 * heads] + bias_ref[...]
        lane = lax.broadcasted_iota(jnp.int32, (bt, heads), 1)
        m_new = jnp.zeros((bt, heads), f32)
        for hh in range(heads):
            ig = g8[:, hh:hh + 1]
            lf = jax.nn.log_sigmoid(g8[:, heads + hh:heads + hh + 1])
            m0 = m0_ref[:, hh:hh + 1]
            m_t = jnp.maximum(lf + m0, ig)
            w = jnp.exp(ig - m_t)
            g = jnp.exp(lf + m0 - m_t)
            q = q_ref[:, hh * dk:(hh + 1) * dk]
            k = k_ref[:, hh * dk:(hh + 1) * dk] * (dk ** -0.5)
            v = v_ref[:, hh * dv:(hh + 1) * dv]
            o = o_ref[:, hh * dv:(hh + 1) * dv]
            n0 = n0_ref[:, hh, :]
            kw = k * w
            q_t = q.T
            kw_t = kw.T
            qc_rows = []
            for b in range(bt):
                cmat = c0_ref[b, hh]
                qc_rows.append(jnp.sum(q_t[:, b:b + 1] * cmat, axis=0, keepdims=True))
                c_out[b, hh] = g[b:b + 1, :] * cmat + kw_t[:, b:b + 1] * v[b:b + 1, :]
            qc = jnp.concatenate(qc_rows, axis=0)
            s = jnp.sum(q * k, axis=1, keepdims=True) * w
            num = s * v + g * qc
            den = s + g * jnp.sum(q * n0, axis=1, keepdims=True)
            hraw = num / jnp.maximum(jnp.abs(den), jnp.exp(-m_t))
            hn = _rms(hraw, gml_ref[hh:hh + 1, :])
            h_ref[:, hh * dv:(hh + 1) * dv] = jax.nn.sigmoid(o) * hn
            n_out[:, hh, :] = g * n0 + kw
            m_new = jnp.where(lane == hh, m_t, m_new)
        m_out[...] = m_new


def _stacked_state_grid(layer, b, stack):
    n_l = DEPTH if stack is None else 1
    row = lambda l, i: jnp.where(l == 0, i, b - 1)
    out_layer = lambda l: (layer + l) % DEPTH
    return (n_l, b), row, out_layer


def _mlstm_sample(qkvo, gd, bias_row, g_ml, c0, n0, m0, *, layer, row0, c_stack=None):
    heads, dk, dv = ML_HEADS, ML_DK, ML_DV
    b = c0.shape[1]
    bt = _SAMPLE_BLOCK
    wq = heads * dk
    blk0 = row0 // bt
    grid, row, out_layer = _stacked_state_grid(layer, b // bt, c_stack)
    aliased = c_stack is not None
    body = functools.partial(_mlstm_sample_body, heads=heads, dk=dk, dv=dv, aliased=aliased)
    in_specs = [pl.BlockSpec((bt, wq), lambda l, i, c=c: (blk0 + row(l, i), c)) for c in range(4)]
    in_specs += [pl.BlockSpec((bt, gd.shape[-1]), lambda l, i: (blk0 + row(l, i), 0)),
                 pl.BlockSpec((1, 2 * heads), lambda l, i: (0, 0)),
                 pl.BlockSpec((heads, dv), lambda l, i: (0, 0)),
                 pl.BlockSpec((None, bt, heads, dk, dv), lambda l, i: (layer, row(l, i), 0, 0, 0)),
                 pl.BlockSpec((None, bt, heads, dk), lambda l, i: (layer, row(l, i), 0, 0)),
                 pl.BlockSpec((None, bt, heads), lambda l, i: (layer, row(l, i), 0))]
    args = [qkvo, qkvo, qkvo, qkvo, gd, bias_row, g_ml, c0, n0, m0]
    if aliased:
        in_specs.append(pl.BlockSpec(memory_space=pl.ANY))
        args.append(c_stack)
    return pl.pallas_call(
        body, grid=grid, in_specs=in_specs,
        out_specs=[pl.BlockSpec((bt, heads * dv), lambda l, i: (row(l, i), 0)),
                   pl.BlockSpec((None, bt, heads, dk, dv), lambda l, i: (out_layer(l), i, 0, 0, 0)),
                   pl.BlockSpec((bt, heads, dk), lambda l, i: (row(l, i), 0, 0)),
                   pl.BlockSpec((bt, heads), lambda l, i: (row(l, i), 0))],
        out_shape=[jax.ShapeDtypeStruct((b, heads * dv), f32),
                   jax.ShapeDtypeStruct((DEPTH, b, heads, dk, dv), f32),
                   jax.ShapeDtypeStruct((b, heads, dk), f32),
                   jax.ShapeDtypeStruct((b, heads), f32)],
        input_output_aliases={10: 1} if aliased else {},
        compiler_params=_params("arbitrary", "arbitrary"), name="mlstm_sample",
    )(*args)


def _s5_body(u_ref, m_ref, wre_ref, wim_ref, vre_ref, vim_ref, lre_ref, lim_ref, x0re_ref, x0im_ref,
             y_ref, xre_out, xim_out, xs_re, xs_im, *, nc, batch):
    u = u_ref[0]
    xin_re = jnp.dot(u, wre_ref[0], preferred_element_type=f32)
    xin_im = jnp.dot(u, wim_ref[0], preferred_element_type=f32)
    lre = lre_ref[0]
    lim = lim_ref[0]
    xr = x0re_ref[0]
    xi = x0im_ref[0]
    for k in range(nc):
        sl = slice(k * batch, (k + 1) * batch)
        xs_re[sl, :] = xr
        xs_im[sl, :] = xi
        xr, xi = (lre * xr - lim * xi + xin_re[sl, :], lre * xi + lim * xr + xin_im[sl, :])
    xre_out[0] = xr
    xim_out[0] = xi
    y_ref[0] = (jnp.dot(u, m_ref[0], preferred_element_type=f32)
                + jnp.dot(xs_re[...].astype(bf16), vre_ref[0], preferred_element_type=f32)
                + jnp.dot(xs_im[...].astype(bf16), vim_ref[0], preferred_element_type=f32))


def _s5_scan(u_g, mats, x0re, x0im, *, nc, batch):
    m_mat, w_re, w_im, v_re, v_im, l_re, l_im = mats
    g, rows, tc = u_g.shape
    p = w_re.shape[-1]
    blk = lambda *s: pl.BlockSpec((1,) + s, lambda i: (i,) + (0,) * len(s))
    body = functools.partial(_s5_body, nc=nc, batch=batch)
    return pl.pallas_call(
        body, grid=(g,),
        in_specs=[blk(rows, tc), blk(tc, tc), blk(tc, p), blk(tc, p), blk(p, tc), blk(p, tc),
                  blk(1, p), blk(1, p), blk(batch, p), blk(batch, p)],
        out_specs=[blk(rows, tc), blk(batch, p), blk(batch, p)],
        out_shape=[jax.ShapeDtypeStruct((g, rows, tc), f32),
                   jax.ShapeDtypeStruct((g, batch, p), f32), jax.ShapeDtypeStruct((g, batch, p), f32)],
        scratch_shapes=[pltpu.VMEM((rows, p), f32), pltpu.VMEM((rows, p), f32)],
        compiler_params=_params("parallel"), name=f"s5_scan_t{tc // S5_CH}",
    )(u_g, m_mat, w_re, w_im, v_re, v_im, l_re, l_im, x0re, x0im)


def _s5_matrices(lam_re, lam_im, log_dt, b_re, b_im, c_re, c_im, t):
    g, p = lam_re.shape
    ch = b_re.shape[-1]
    dt = jnp.exp(log_dt)[:, None]
    ar, ai = lam_re * dt, lam_im * dt

    def powers(tau):
        mag = jnp.exp(ar[:, None, :] * tau[None, :, None])
        ang = ai[:, None, :] * tau[None, :, None]
        return mag * jnp.cos(ang), mag * jnp.sin(ang)

    lbr, lbi = jnp.exp(ar) * jnp.cos(ai), jnp.exp(ar) * jnp.sin(ai)
    den = lam_re * lam_re + lam_im * lam_im
    fr = ((lbr - 1.0) * lam_re + lbi * lam_im) / den
    fi = (lbi * lam_re - (lbr - 1.0) * lam_im) / den
    bbr = jnp.swapaxes(fr[..., None] * b_re - fi[..., None] * b_im, 1, 2)
    bbi = jnp.swapaxes(fr[..., None] * b_im + fi[..., None] * b_re, 1, 2)
    cbr = c_re[:, :, None, :] * bbr[:, None, :, :] - c_im[:, :, None, :] * bbi[:, None, :, :]
    cbi = c_re[:, :, None, :] * bbi[:, None, :, :] + c_im[:, :, None, :] * bbr[:, None, :, :]
    steps = jnp.arange(t, dtype=f32)
    diff = steps[None, :] - steps[:, None]
    lag = jnp.maximum(diff, 0.0)
    mag = jnp.exp(ar[:, None, None, :] * lag[None, :, :, None])
    ang = ai[:, None, None, :] * lag[None, :, :, None]
    keep = (diff >= 0.0)[None, :, :, None]
    e2 = jnp.concatenate([jnp.where(keep, mag * jnp.cos(ang), 0.0),
                          jnp.where(keep, mag * jnp.sin(ang), 0.0)], axis=-1)
    cb2 = jnp.concatenate([cbr, -cbi], axis=-1)
    m_mat = jnp.einsum("gstp,gcdp->gsdtc", e2, cb2, precision=HI).reshape(g, t * ch, t * ch)
    pr, pi = powers(t - 1.0 - steps)
    w_re = (pr[:, :, None, :] * bbr[:, None, :, :] - pi[:, :, None, :] * bbi[:, None, :, :]).reshape(g, t * ch, p)
    w_im = (pr[:, :, None, :] * bbi[:, None, :, :] + pi[:, :, None, :] * bbr[:, None, :, :]).reshape(g, t * ch, p)
    qr, qi = powers(steps + 1.0)
    qr, qi = jnp.swapaxes(qr, 1, 2)[..., None], jnp.swapaxes(qi, 1, 2)[..., None]
    ctr, cti = jnp.swapaxes(c_re, 1, 2)[:, :, None, :], jnp.swapaxes(c_im, 1, 2)[:, :, None, :]
    v_re = (ctr * qr - cti * qi).reshape(g, p, t * ch)
    v_im = -(ctr * qi + cti * qr).reshape(g, p, t * ch)
    ltr, lti = powers(jnp.full((1,), float(t), f32))
    return (m_mat.astype(bf16), w_re.astype(bf16), w_im.astype(bf16), v_re.astype(bf16), v_im.astype(bf16),
            ltr, lti)


def _s5_glu_body(y_ref, u_ref, d_ref, w_ref, b_ref, g_ref, o_ref):
    y5 = jax.nn.gelu(y_ref[...] + d_ref[...] * u_ref[...])
    gate = jax.nn.sigmoid(jnp.dot(y5.astype(bf16), w_ref[...].astype(bf16), preferred_element_type=f32) + b_ref[...])
    o_ref[...] = _rms(y5 * gate, g_ref[...]).astype(o_ref.dtype)


def _s5_glu(y_raw, uzx, d_skip, w_glu, b_glu, g_s5, *, layer, tm):
    m, wdt = y_raw.shape
    row = lambda i: (i, 0)
    fix = lambda i: (0, 0)
    return pl.pallas_call(
        _s5_glu_body, grid=(m // tm,),
        in_specs=[pl.BlockSpec((tm, wdt), row), pl.BlockSpec((tm, wdt), row), pl.BlockSpec((1, wdt), fix),
                  pl.BlockSpec((None, wdt, wdt), lambda i: (layer, 0, 0)), pl.BlockSpec((1, wdt), fix),
                  pl.BlockSpec((1, wdt), fix)],
        out_specs=pl.BlockSpec((tm, wdt), row), out_shape=jax.ShapeDtypeStruct((m, wdt), bf16),
        compiler_params=_params("parallel"), name="s5_glu",
    )(y_raw, uzx, d_skip, w_glu, b_glu, g_s5)


def _ssd_prompt_body(xbc_ref, z_ref, gd_ref, dtt_ref, cw_ref, cb_ref, dtb_row, dtb_col, alog_row, alog_col,
                     dskip_ref, gssd_ref, y_ref, s_out, s_s, xp_s, ys_s, *, heads, hd, ns, groups, cl, width):
    c = pl.program_id(1)

    @pl.when(c == 0)
    def _():
        s_s[...] = jnp.zeros_like(s_s)
        xp_s[0:8, :] = jnp.zeros((8, xp_s.shape[1]), f32)

    xp_s[8:8 + cl, :] = xbc_ref[...]
    xc = cb_ref[...] + sum(cw_ref[j:j + 1, :] * xp_s[5 + j:5 + j + cl, :] for j in range(SSD_CONV))
    xp_s[0:8, :] = xp_s[cl:cl + 8, :]
    xc = xc * jax.nn.sigmoid(xc)
    row = lax.broadcasted_iota(jnp.int32, (cl, cl), 0)
    col = lax.broadcasted_iota(jnp.int32, (cl, cl), 1)
    causal = col <= row
    tril = causal.astype(f32)
    triu = (row <= col).astype(f32)
    dt_col = jax.nn.softplus(gd_ref[:, 8:8 + heads] + dtb_row[...])
    dt_row = jax.nn.softplus(dtt_ref[0, 0] + dtb_col[...])
    cum_col = jnp.dot(tril, dt_col * -jnp.exp(alog_row[...]), precision=HI, preferred_element_type=f32)
    cum_row = jnp.dot(dt_row * -jnp.exp(alog_col[...]), triu, precision=HI, preferred_element_type=f32)
    rep = heads // groups
    for gi in range(groups):
        bm = xc[:, width + gi * ns:width + (gi + 1) * ns].astype(bf16)
        cm = xc[:, width + (groups + gi) * ns:width + (groups + gi + 1) * ns].astype(bf16)
        scores = lax.dot_general(cm, bm, _NT, preferred_element_type=f32)
        for hh in range(gi * rep, (gi + 1) * rep):
            cc = cum_col[:, hh:hh + 1]
            cr = cum_row[hh:hh + 1, :]
            seg = jnp.exp(jnp.where(causal, cc - cr, -jnp.inf))
            xh = xc[:, hh * hd:(hh + 1) * hd]
            xdt = xh * dt_col[:, hh:hh + 1]
            smat = s_s[hh]
            y = (jnp.dot((scores * seg).astype(bf16), xdt.astype(bf16), preferred_element_type=f32)
                 + jnp.exp(cc) * lax.dot_general(cm, smat.astype(bf16), _NT, preferred_element_type=f32))
            c_last = cr[:, cl - 1:cl]
            xw = (xdt * jnp.exp(c_last - cc)).astype(bf16)
            s_s[hh] = jnp.exp(c_last) * smat + lax.dot_general(xw, bm, _TN, preferred_element_type=f32)
            ys_s[:, hh * hd:(hh + 1) * hd] = y + dskip_ref[:, hh:hh + 1] * xh
    z = z_ref[...]
    y_ref[...] = _rms(ys_s[...] * (z * jax.nn.sigmoid(z)), gssd_ref[...]).astype(y_ref.dtype)

    @pl.when(c == pl.num_programs(1) - 1)
    def _():
        s_out[0] = s_s[...]


def _ssd_prompt(uzx, gd, dt_t, conv_w, conv_b, dt_bias, a_log, d_skip, g_ssd, *, batch, seq):
    heads, hd, ns, groups, width = SSD_HEADS, SSD_HEAD_DIM, SSD_STATE, SSD_GROUPS, SSD_WIDTH
    cl = math.gcd(seq, CHUNK)
    nc = seq // cl
    cch = SSD_CONV_CH
    rows = lambda b, c: (b * nc + c, 0)
    fix = lambda b, c: (0, 0)
    body = functools.partial(_ssd_prompt_body, heads=heads, hd=hd, ns=ns, groups=groups, cl=cl, width=width)
    return pl.pallas_call(
        body, grid=(batch, nc),
        in_specs=[pl.BlockSpec((cl, cch), lambda b, c: (b * nc + c, 1)),
                  pl.BlockSpec((cl, width), lambda b, c: (b * nc + c, 1)),
                  pl.BlockSpec((cl, gd.shape[1]), rows),
                  pl.BlockSpec((1, 1, heads, cl), lambda b, c: (b, c, 0, 0)),
                  pl.BlockSpec((SSD_CONV, cch), fix), pl.BlockSpec((1, cch), fix),
                  pl.BlockSpec((1, heads), fix), pl.BlockSpec((heads, 1), fix),
                  pl.BlockSpec((1, heads), fix), pl.BlockSpec((heads, 1), fix),
                  pl.BlockSpec((1, heads), fix), pl.BlockSpec((1, width), fix)],
        out_specs=[pl.BlockSpec((cl, width), rows),
                   pl.BlockSpec((1, heads, hd, ns), lambda b, c: (b, 0, 0, 0))],
        out_shape=[jax.ShapeDtypeStruct((batch * seq, width), bf16),
                   jax.ShapeDtypeStruct((batch, heads, hd, ns), f32)],
        scratch_shapes=[pltpu.VMEM((heads, hd, ns), f32), pltpu.VMEM((cl + 8, cch), f32),
                        pltpu.VMEM((cl, width), f32)],
        compiler_params=_params("arbitrary", "arbitrary"), name="ssd_prompt",
    )(uzx, uzx, gd, dt_t, conv_w, conv_b.reshape(1, cch), dt_bias.reshape(1, heads), dt_bias.reshape(heads, 1),
      a_log.reshape(1, heads), a_log.reshape(heads, 1), d_skip.reshape(1, heads), g_ssd.reshape(1, width))


def _ssd_sample_body(*refs, heads, hd, ns, groups, width, aliased):
    if aliased:
        refs = refs[:10] + refs[11:]
    (x_ref, conv0_ref, gd_ref, cw_ref, cb_ref, dtb_ref, alog_ref, dskip_ref, gssd_ref, s0_ref,
     y_ref, s_out, ys_s) = refs
    bt = x_ref.shape[0]

    @pl.when(pl.program_id(0) > 0)
    def _():
        s_out[...] = jnp.zeros_like(s_out)

    @pl.when(pl.program_id(0) == 0)
    def _():
        z = x_ref[:, width:2 * width]
        xc = cb_ref[...] + cw_ref[SSD_CONV - 1:SSD_CONV, :] * x_ref[:, 2 * width:]
        for j in range(SSD_CONV - 1):
            xc = xc + cw_ref[j:j + 1, :] * conv0_ref[:, j, :]
        xc = xc * jax.nn.sigmoid(xc)
        dt = jax.nn.softplus(gd_ref[:, 8:8 + heads] + dtb_ref[...])
        ea = jnp.exp(dt * -jnp.exp(alog_ref[...]))
        rep = heads // groups
        for gi in range(groups):
            bm = xc[:, width + gi * ns:width + (gi + 1) * ns]
            cm = xc[:, width + (groups + gi) * ns:width + (groups + gi + 1) * ns]
            cb_dot = jnp.sum(cm * bm, axis=1, keepdims=True)
            cmb = cm.astype(bf16)
            for hh in range(gi * rep, (gi + 1) * rep):
                xh = xc[:, hh * hd:(hh + 1) * hd]
                xdt = xh * dt[:, hh:hh + 1]
                eah = ea[:, hh:hh + 1]
                xdt_t = xdt.T
                sc_rows = []
                for b in range(bt):
                    smat = s0_ref[b, hh]
                    sc_rows.append(lax.dot_general(cmb, smat.astype(bf16), _NT,
                                                   preferred_element_type=f32)[b:b + 1, :])
                    s_out[b, hh] = eah[b:b + 1, :] * smat + xdt_t[:, b:b + 1] * bm[b:b + 1, :]
                sc = jnp.concatenate(sc_rows, axis=0)
                ys_s[:, hh * hd:(hh + 1) * hd] = cb_dot * xdt + eah * sc + dskip_ref[:, hh:hh + 1] * xh
        y_ref[...] = _rms(ys_s[...] * (z * jax.nn.sigmoid(z)), gssd_ref[...])


def _ssd_sample(uzx, conv0, gd, conv_w, conv_b, dt_bias, a_log, d_skip, g_ssd, s0, *, layer, row0, s_stack=None):
    heads, hd, ns, groups, width = SSD_HEADS, SSD_HEAD_DIM, SSD_STATE, SSD_GROUPS, SSD_WIDTH
    cch = SSD_CONV_CH
    b = s0.shape[1]
    bt = _SAMPLE_BLOCK
    blk0 = row0 // bt
    fix = lambda l, i: (0, 0)
    grid, row, out_layer = _stacked_state_grid(layer, b // bt, s_stack)
    aliased = s_stack is not None
    body = functools.partial(_ssd_sample_body, heads=heads, hd=hd, ns=ns, groups=groups, width=width,
                             aliased=aliased)
    in_specs = [pl.BlockSpec((bt, uzx.shape[-1]), lambda l, i: (blk0 + row(l, i), 0)),
                pl.BlockSpec((None, bt, SSD_CONV - 1, cch), lambda l, i: (layer, row(l, i), 0, 0)),
                pl.BlockSpec((bt, gd.shape[-1]), lambda l, i: (blk0 + row(l, i), 0)),
                pl.BlockSpec((SSD_CONV, cch), fix), pl.BlockSpec((1, cch), fix),
                pl.BlockSpec((1, heads), fix), pl.BlockSpec((1, heads), fix), pl.BlockSpec((1, heads), fix),
                pl.BlockSpec((1, width), fix),
                pl.BlockSpec((None, bt, heads, hd, ns), lambda l, i: (layer, row(l, i), 0, 0, 0))]
    args = [uzx, conv0, gd, conv_w, conv_b.reshape(1, cch), dt_bias.reshape(1, heads), a_log.reshape(1, heads),
            d_skip.reshape(1, heads), g_ssd.reshape(1, width), s0]
    if aliased:
        in_specs.append(pl.BlockSpec(memory_space=pl.ANY))
        args.append(s_stack)
    return pl.pallas_call(
        body, grid=grid, in_specs=in_specs,
        out_specs=[pl.BlockSpec((bt, width), lambda l, i: (row(l, i), 0)),
                   pl.BlockSpec((None, bt, heads, hd, ns), lambda l, i: (out_layer(l), i, 0, 0, 0))],
        out_shape=[jax.ShapeDtypeStruct((b, width), f32),
                   jax.ShapeDtypeStruct((DEPTH, b, heads, hd, ns), f32)],
        scratch_shapes=[pltpu.VMEM((bt, width), f32)],
        input_output_aliases={10: 1} if aliased else {},
        compiler_params=_params("arbitrary", "arbitrary"), name="ssd_sample",
    )(*args)


def _router_body(h_ref, g_ref, wr_ref, br_ref, cf_ref, idx_ref, gate_ref, *, n_exp):
    cf = _rms(h_ref[...], g_ref[...])
    cf_ref[...] = cf
    logits = jnp.dot(cf, wr_ref[...], precision=HI, preferred_element_type=f32) + br_ref[...]
    lane = lax.broadcasted_iota(jnp.int32, logits.shape, 1)
    m1 = jnp.max(logits, axis=1, keepdims=True)
    i1 = jnp.min(jnp.where(logits == m1, lane, n_exp), axis=1, keepdims=True)
    rest = jnp.where(lane == i1, -jnp.inf, logits)
    m2 = jnp.max(rest, axis=1, keepdims=True)
    i2 = jnp.min(jnp.where(rest == m2, lane, n_exp), axis=1, keepdims=True)
    e2 = jnp.exp(m2 - m1)
    g1 = 1.0 / (1.0 + e2)
    two = lax.broadcasted_iota(jnp.int32, (logits.shape[0], TOP_K), 1)
    idx_ref[...] = jnp.where(two == 0, i1, i2)
    gate_ref[...] = jnp.where(two == 0, g1, e2 * g1)


def _router(h, g_ffn, w_router, b_router, *, tm):
    m, d = h.shape
    e = w_router.shape[-1]
    row = lambda i: (i, 0)
    fix = lambda i: (0, 0)
    return pl.pallas_call(
        functools.partial(_router_body, n_exp=e), grid=(m // tm,),
        in_specs=[pl.BlockSpec((tm, d), row), pl.BlockSpec((1, d), fix), pl.BlockSpec((d, e), fix),
                  pl.BlockSpec((1, e), fix)],
        out_specs=[pl.BlockSpec((tm, d), row), pl.BlockSpec((tm, TOP_K), row),
                   pl.BlockSpec((tm, TOP_K), row)],
        out_shape=[jax.ShapeDtypeStruct((m, d), f32),
                   jax.ShapeDtypeStruct((m, TOP_K), jnp.int32), jax.ShapeDtypeStruct((m, TOP_K), f32)],
        compiler_params=_params("parallel"), name="router",
    )(h, g_ffn.reshape(1, d), w_router, b_router.reshape(1, e))


def _row_copy(src_hbm, dst, sem, src_row, dst_row):
    return pltpu.make_async_copy(src_hbm.at[pl.ds(src_row, 1)], dst.at[pl.ds(dst_row, 1)], sem)


_DMA_UNROLL = 8


def _gather_body(idx_ref, x_hbm, o_ref, buf, sem, *, tg):
    i = pl.program_id(0)

    def issue(tile, slot):
        base = tile * tg

        def start(r, carry):
            _row_copy(x_hbm, buf.at[slot], sem.at[slot], idx_ref[base + r], r).start()
            return carry

        lax.fori_loop(0, tg, start, 0, unroll=_DMA_UNROLL)

    @pl.when(i == 0)
    def _():
        issue(0, 0)

    @pl.when(i + 1 < pl.num_programs(0))
    def _():
        issue(i + 1, (i + 1) % 2)

    slot = i % 2

    def wait(r, carry):
        _row_copy(x_hbm, buf.at[slot], sem.at[slot], 0, r).wait()
        return carry

    lax.fori_loop(0, tg, wait, 0, unroll=_DMA_UNROLL)
    o_ref[...] = buf[slot].astype(o_ref.dtype)


def _gather_rows(x, row_idx, *, tg, out_dtype):
    r = row_idx.shape[0]
    d = x.shape[1]
    return pl.pallas_call(
        functools.partial(_gather_body, tg=tg),
        grid_spec=pltpu.PrefetchScalarGridSpec(
            num_scalar_prefetch=1, grid=(r // tg,),
            in_specs=[pl.BlockSpec(memory_space=pl.ANY)],
            out_specs=pl.BlockSpec((tg, d), lambda i, idx: (i, 0)),
            scratch_shapes=[pltpu.VMEM((2, tg, d), x.dtype), pltpu.SemaphoreType.DMA((2,))]),
        out_shape=jax.ShapeDtypeStruct((r, d), out_dtype),
        compiler_params=_params("arbitrary"), name="gather_rows")(row_idx, x)


def _gmm_body(te_ref, first_ref, next_ref, nv_ref, a_ref, *rest, n_w, swiglu):
    w_hbm = rest[:n_w]
    o_ref, wf_ref, wb_ref, sem, run_ref = rest[n_w:]
    j = pl.program_id(0)
    t = pl.program_id(1)
    tn = o_ref.shape[1]

    def tile_copies(expert, col_tile, slot):
        col = pl.multiple_of(col_tile * tn, 128)
        return [pltpu.make_async_copy(w_hbm[i].at[expert, :, pl.ds(col, tn)], wf_ref.at[slot, i], sem.at[slot, i])
                for i in range(n_w)]

    @pl.when((j == 0) & (t == 0))
    def _():
        run_ref[0] = 0
        for c in tile_copies(te_ref[0], 0, 0):
            c.start()

    @pl.when((t < nv_ref[0]) & (first_ref[t] == 1))
    def _():
        slot = run_ref[0] % 2
        for c in tile_copies(te_ref[t], j, slot):
            c.wait()
        for i in range(n_w):
            wb_ref[i] = wf_ref[slot, i].astype(bf16)
        nxt = next_ref[t]

        @pl.when(nxt >= 0)
        def _():
            for c in tile_copies(te_ref[nxt], j, 1 - slot):
                c.start()

        @pl.when((nxt < 0) & (j + 1 < pl.num_programs(0)))
        def _():
            for c in tile_copies(te_ref[0], j + 1, 1 - slot):
                c.start()

        run_ref[0] = run_ref[0] + 1

    @pl.when(t < nv_ref[0])
    def _():
        a = a_ref[...]
        if swiglu:
            g = jnp.dot(a, wb_ref[0], preferred_element_type=f32)
            u = jnp.dot(a, wb_ref[1], preferred_element_type=f32)
            o_ref[...] = (g * jax.nn.sigmoid(g) * u).astype(o_ref.dtype)
        else:
            o_ref[...] = jnp.dot(a, wb_ref[0], preferred_element_type=f32).astype(o_ref.dtype)

    @pl.when(t >= nv_ref[0])
    def _():
        o_ref[...] = jnp.zeros_like(o_ref)


def _gmm(a_sorted, weights, tables, *, tm, tn, swiglu, out_dtype, name):
    tile_expert, tile_first, tile_next, n_valid = tables
    r, k = a_sorted.shape
    n = weights[0].shape[-1]
    n_w = len(weights)
    return pl.pallas_call(
        functools.partial(_gmm_body, n_w=n_w, swiglu=swiglu),
        grid_spec=pltpu.PrefetchScalarGridSpec(
            num_scalar_prefetch=4, grid=(n // tn, r // tm),
            in_specs=[pl.BlockSpec((tm, k), lambda j, t, *_: (t, 0))] + [pl.BlockSpec(memory_space=pl.ANY)] * n_w,
            out_specs=pl.BlockSpec((tm, tn), lambda j, t, *_: (t, j)),
            scratch_shapes=[pltpu.VMEM((2, n_w, k, tn), f32), pltpu.VMEM((n_w, k, tn), bf16),
                            pltpu.SemaphoreType.DMA((2, n_w)), pltpu.SMEM((1,), jnp.int32)]),
        out_shape=jax.ShapeDtypeStruct((r, n), out_dtype),
        compiler_params=_params("arbitrary", "arbitrary"), name=name,
    )(tile_expert, tile_first, tile_next, n_valid, a_sorted, *weights)


def _combine_body(pos_ref, y_hbm, gate_ref, r_ref, o_ref, buf, sem, *, tc):
    i = pl.program_id(0)

    def issue(tile, slot):
        base = tile * tc

        def start(r, carry):
            for kk in range(TOP_K):
                _row_copy(y_hbm, buf.at[slot, kk], sem.at[slot], pos_ref[(base + r) * TOP_K + kk], r).start()
            return carry

        lax.fori_loop(0, tc, start, 0, unroll=_DMA_UNROLL)

    @pl.when(i == 0)
    def _():
        issue(0, 0)

    @pl.when(i + 1 < pl.num_programs(0))
    def _():
        issue(i + 1, (i + 1) % 2)

    slot = i % 2

    def wait(r, carry):
        for kk in range(TOP_K):
            _row_copy(y_hbm, buf.at[slot, kk], sem.at[slot], 0, r).wait()
        return carry

    lax.fori_loop(0, tc, wait, 0, unroll=_DMA_UNROLL)
    gate = gate_ref[...]
    o_ref[...] = r_ref[...] + gate[:, 0:1] * buf[slot, 0] + gate[:, 1:2] * buf[slot, 1]


def _combine(y_sorted, pos_flat, gates, resid, *, tc):
    m, d = resid.shape
    return pl.pallas_call(
        functools.partial(_combine_body, tc=tc),
        grid_spec=pltpu.PrefetchScalarGridSpec(
            num_scalar_prefetch=1, grid=(m // tc,),
            in_specs=[pl.BlockSpec(memory_space=pl.ANY),
                      pl.BlockSpec((tc, TOP_K), lambda i, pos: (i, 0)),
                      pl.BlockSpec((tc, d), lambda i, pos: (i, 0))],
            out_specs=pl.BlockSpec((tc, d), lambda i, pos: (i, 0)),
            scratch_shapes=[pltpu.VMEM((2, TOP_K, tc, d), f32), pltpu.SemaphoreType.DMA((2,))]),
        out_shape=jax.ShapeDtypeStruct((m, d), f32),
        compiler_params=_params("arbitrary"), name="moe_combine")(pos_flat, y_sorted, gates, resid)


def _routing_tables(top_i, n_exp, tm, n_tiles):
    m = top_i.shape[0]
    e_flat = top_i.reshape(-1)
    onehot = (e_flat[:, None] == jnp.arange(n_exp, dtype=jnp.int32)[None, :]).astype(jnp.int32)
    rank = jnp.take_along_axis(jnp.cumsum(onehot, axis=0), e_flat[:, None], axis=1)[:, 0] - 1
    counts = jnp.sum(onehot, axis=0)
    tiles_per = (counts + tm - 1) // tm
    tile_end = jnp.cumsum(tiles_per)
    tile_start = tile_end - tiles_per
    pos = tile_start[e_flat] * tm + rank
    token = jnp.arange(m * TOP_K, dtype=jnp.int32) // TOP_K
    row_token = jnp.zeros((n_tiles * tm,), jnp.int32).at[pos].set(token)
    n_valid = tile_end[-1]
    tid = jnp.minimum(jnp.arange(n_tiles, dtype=jnp.int32), n_valid - 1)
    tile_expert = jnp.sum((tid[:, None] >= tile_end[None, :]).astype(jnp.int32), axis=1)
    tile_first = jnp.concatenate([jnp.ones((1,), jnp.int32),
                                  (tile_expert[1:] != tile_expert[:-1]).astype(jnp.int32)])
    ids = jnp.arange(n_tiles, dtype=jnp.int32)
    starts = jnp.where((tile_first == 1) & (ids < n_valid), ids, n_tiles)
    later = jnp.concatenate([lax.cummin(starts, reverse=True)[1:], jnp.full((1,), n_tiles, jnp.int32)])
    tile_next = jnp.where(later >= n_tiles, -1, later).astype(jnp.int32)
    return (pos.astype(jnp.int32), row_token,
            (tile_expert.astype(jnp.int32), tile_first, tile_next, n_valid.reshape(1).astype(jnp.int32)))


def _moe_ffn(h, g_ffn, w_router, b_router, wg, wu, wd, *, tm_tok, tm, tn_up, tn_down):
    m = h.shape[0]
    n_exp = wg.shape[0]
    c_f32, top_i, top_g = _router(h, g_ffn, w_router, b_router, tm=tm_tok)
    n_tiles = (m * TOP_K) // tm + n_exp
    pos, row_token, tables = _routing_tables(top_i, n_exp, tm, n_tiles)
    x_sorted = _gather_rows(c_f32, row_token, tg=tm, out_dtype=bf16)
    h_sorted = _gmm(x_sorted, (wg, wu), tables, tm=tm, tn=tn_up, swiglu=True, out_dtype=bf16, name="moe_gate_up")
    y_sorted = _gmm(h_sorted, (wd,), tables, tm=tm, tn=tn_down, swiglu=False, out_dtype=f32, name="moe_down")
    return _combine(y_sorted, pos, top_g, h, tc=tm_tok // 2)


def kernel(x_prompt, x_sample, state_mlstm_C, state_mlstm_n, state_mlstm_m, state_s5_re, state_s5_im, state_ssd, cache_conv, p_prompt, p_sample, g_mix, w_in, b_igate, b_fgate, g_ml, s5_lam_re, s5_lam_im, s5_log_dt, s5_b_re, s5_b_im, s5_c_re, s5_c_im, s5_d, s5_w_glu, s5_b_glu, g_s5, ssd_conv_w, ssd_conv_b, ssd_dt_bias, ssd_a_log, ssd_d, g_ssd, w_out, g_ffn, ffn_w_gate, ffn_w_up, ffn_w_down, w_router, b_router, moe_w_gate, moe_w_up, moe_w_down, g_ple, w_ple, w_ple_gate, g_final):
    bp, seq, d = x_prompt.shape
    bs = x_sample.shape[0]
    tp = bp * seq
    m = tp + bs
    tm = _token_tile(tp, bs)
    heads = ML_HEADS
    t5 = math.gcd(seq, S5_CHUNK)
    nc5 = seq // t5
    cl = math.gcd(seq, CHUNK)
    nc = seq // cl

    p_p = p_prompt.reshape(DEPTH, tp, -1)
    p_s = p_sample.reshape(DEPTH, bs, -1)
    zeros5 = jnp.zeros((S5_GROUPS, bp, S5_STATE), f32)
    w_in_t = jnp.swapaxes(w_in, 1, 2)
    outs_p = [[] for _ in range(7)]
    outs_s = [[] for _ in range(5)]
    c_stack = s_stack = None

    for i in range(DEPTH):
        if i == 0:
            h, a = _rmsnorm_in(x_prompt.reshape(tp, d), x_sample.reshape(bs, d), g_mix[i], tm)
        else:
            a = _rmsnorm(h, g_mix[i], bf16, tm)
        qkvo = _mm_nt(a, w_in_t, tm=tm, tn=512, layer=i, row_off=0, n_rows=_OFF_GATES, name="in_proj_qkvo")
        uzx = _mm_nt(a, w_in_t, tm=tm, tn=512, layer=i, row_off=_OFF_U, n_rows=_OFF_DT - _OFF_U, name="in_proj_uzx")
        gd = _mm_gates(a, w_in_t, tm=tm, layer=i, off1=_OFF_GATES, n1=2 * heads, off2=_OFF_DT, n2=SSD_HEADS,
                       name="in_proj_gates")

        gates_t = jnp.transpose(gd[:tp, :2 * heads].reshape(bp, nc, cl, 2 * heads), (0, 1, 3, 2))
        bias8 = jnp.concatenate([b_igate[i], b_fgate[i]])
        h_ml_p, c_p, n_p, m_p = _mlstm_prompt(qkvo, gates_t, bias8.reshape(2 * heads, 1), g_ml[i], batch=bp, seq=seq)
        h_ml_s, c_stack, n_s, m_s = _mlstm_sample(qkvo, gd, bias8.reshape(1, 2 * heads), g_ml[i], state_mlstm_C,
                                                  state_mlstm_n, state_mlstm_m, layer=i, row0=tp, c_stack=c_stack)

        mats_p = _s5_matrices(s5_lam_re[i], s5_lam_im[i], s5_log_dt[i], s5_b_re[i], s5_b_im[i],
                              s5_c_re[i], s5_c_im[i], t5)
        mats_s = _s5_matrices(s5_lam_re[i], s5_lam_im[i], s5_log_dt[i], s5_b_re[i], s5_b_im[i],
                              s5_c_re[i], s5_c_im[i], 1)
        u_p = uzx[:tp, :S5_WIDTH].reshape(bp, nc5, t5, S5_GROUPS, S5_CH)
        u_p = jnp.transpose(u_p, (3, 1, 0, 2, 4)).reshape(S5_GROUPS, nc5 * bp, t5 * S5_CH).astype(bf16)
        y_p, s5re_p, s5im_p = _s5_scan(u_p, mats_p, zeros5, zeros5, nc=nc5, batch=bp)
        y_p = jnp.transpose(y_p.reshape(S5_GROUPS, nc5, bp, t5, S5_CH), (2, 1, 3, 0, 4)).reshape(tp, S5_WIDTH)
        u_s = jnp.transpose(uzx[tp:, :S5_WIDTH].reshape(bs, S5_GROUPS, S5_CH), (1, 0, 2)).astype(bf16)
        y_s, s5re_s, s5im_s = _s5_scan(u_s, mats_s, jnp.swapaxes(state_s5_re[i], 0, 1),
                                       jnp.swapaxes(state_s5_im[i], 0, 1), nc=1, batch=bs)
        y_s = jnp.transpose(y_s, (1, 0, 2)).reshape(bs, S5_WIDTH)
        y5 = _s5_glu(jnp.concatenate([y_p, y_s], axis=0), uzx, s5_d[i].reshape(1, S5_WIDTH), s5_w_glu,
                     s5_b_glu[i].reshape(1, S5_WIDTH), g_s5[i].reshape(1, S5_WIDTH), layer=i, tm=tm)

        dt_t = jnp.transpose(gd[:tp, 2 * heads:2 * heads + SSD_HEADS].reshape(bp, nc, cl, SSD_HEADS), (0, 1, 3, 2))
        y_ssd_p, ssd_p = _ssd_prompt(uzx, gd, dt_t, ssd_conv_w[i], ssd_conv_b[i], ssd_dt_bias[i], ssd_a_log[i],
                                     ssd_d[i], g_ssd[i], batch=bp, seq=seq)
        y_ssd_s, s_stack = _ssd_sample(uzx, cache_conv, gd, ssd_conv_w[i], ssd_conv_b[i], ssd_dt_bias[i],
                                       ssd_a_log[i], ssd_d[i], g_ssd[i], state_ssd, layer=i, row0=tp,
                                       s_stack=s_stack)
        xbc = uzx[:, S5_WIDTH + SSD_WIDTH:]
        conv_p = xbc[:tp].reshape(bp, seq, -1)[:, seq - (SSD_CONV - 1):]
        conv_s = jnp.concatenate([cache_conv[i][:, 1:], xbc[tp:].reshape(bs, 1, -1)], axis=1)

        for lst, s in zip(outs_p, (c_p, n_p.reshape(bp, heads, ML_DK), m_p.reshape(bp, heads),
                                   jnp.swapaxes(s5re_p, 0, 1), jnp.swapaxes(s5im_p, 0, 1), ssd_p, conv_p)):
            lst.append(s)
        for lst, s in zip(outs_s, (n_s, m_s, jnp.swapaxes(s5re_s, 0, 1), jnp.swapaxes(s5im_s, 0, 1), conv_s)):
            lst.append(s)

        h = _mm_mix(h_ml_p, h_ml_s, y5, y_ssd_p, y_ssd_s, w_out, h, tm=tm, tn=512, layer=i, name="out_proj")

        j = i // 2
        if i % 2 == 0:
            cn = _rmsnorm(h, g_ffn[i], bf16, tm)
            hid = _mm_swiglu(cn, ffn_w_gate, ffn_w_up, tm=tm, tn=512, layer=j, name="ffn_gate_up")
            h = _mm(hid, ffn_w_down, tm=tm // 2, tn=512, layer=j, resid=h, name="ffn_down")
        else:
            n_moe = moe_w_gate.shape[0]
            sel = lambda w: w.reshape((n_moe * N_EXPERTS,) + w.shape[2:])[j * N_EXPERTS:(j + 1) * N_EXPERTS] if n_moe > 1 else w.reshape(w.shape[1:])
            h = _moe_ffn(h, g_ffn[i], w_router[j], b_router[j], sel(moe_w_gate), sel(moe_w_up), sel(moe_w_down),
                         tm_tok=tm, tm=256, tn_up=1024, tn_down=512)

        e = _rmsnorm(h, g_ple[i], bf16, tm)
        h = _mm_ple(e, p_p, p_s, w_ple_gate, w_ple, h, tm=tm, tn=512, layer=i)

    y_p, y_s = _rmsnorm_out(h, g_final, tp, tm)
    n_s, m_s, s5re_s, s5im_s, conv_s = (jnp.stack(l) for l in outs_s)
    return ((y_p.reshape(bp, seq, d), y_s.reshape(bs, 1, d)) + tuple(jnp.stack(l) for l in outs_p)
            + (c_stack, n_s, m_s, s5re_s, s5im_s, s_stack, conv_s))
```

```python
import functools
import math

import jax
import jax.numpy as jnp
from jax import lax
from jax.experimental import pallas as pl
from jax.experimental.pallas import tpu as pltpu

f32 = jnp.float32
bf16 = jnp.bfloat16
HI = lax.Precision.HIGHEST

D_MODEL = 2048
DEPTH = 2
ML_HEADS = 4
ML_DK = 256
ML_DV = 256
ML_WIDTH = ML_HEADS * ML_DV
S5_CH = 16
S5_WIDTH = 512
S5_GROUPS = 32
S5_STATE = 64
SSD_HEAD_DIM = 64
SSD_WIDTH = 512
SSD_HEADS = 8
SSD_GROUPS = 2
SSD_STATE = 128
SSD_CONV = 4
SSD_CONV_CH = 1024
CHUNK = 64
S5_CHUNK = 16
N_EXPERTS = 8
TOP_K = 2
RMS_EPS = 1e-6

_OFF_GATES = 4 * ML_WIDTH
_OFF_U = _OFF_GATES + 2 * ML_HEADS
_OFF_DT = _OFF_U + S5_WIDTH + SSD_WIDTH + SSD_CONV_CH

_VMEM_LIMIT = 56 * 1024 * 1024
_NT = (((1,), (1,)), ((), ()))
_TN = (((0,), (0,)), ((), ()))


def _params(*sem):
    return pltpu.CompilerParams(dimension_semantics=sem, vmem_limit_bytes=_VMEM_LIMIT)


def _rms(x, g):
    return x * lax.rsqrt(jnp.mean(x * x, axis=-1, keepdims=True) + RMS_EPS) * g


def _rmsnorm_body(x_ref, g_ref, o_ref):
    o_ref[...] = _rms(x_ref[...], g_ref[...]).astype(o_ref.dtype)


def _last_tile(p_tile, s_rows, n_keep):
    return jnp.concatenate([p_tile[0:n_keep, :], s_rows], axis=0)


def _split_rows(tp, bs, tm):
    n_tiles = (tp + bs) // tm
    n_keep = tp - (n_tiles - 1) * tm
    assert n_tiles * tm == tp + bs and 0 < n_keep and n_keep + bs == tm and n_keep % 16 == 0
    return n_tiles, n_keep


def _token_tile(tp, bs):
    m = tp + bs
    for n_tiles in (10, 8, 5, 4, 2, 1):
        tm = m // n_tiles
        if m % n_tiles == 0 and tm % 32 == 0 and bs < tm and (tp - (n_tiles - 1) * tm) % 16 == 0:
            return tm
    raise ValueError("no row tiling for these token counts")


def _rmsnorm_in_body(xp_ref, xs_ref, g_ref, h_ref, a_ref, *, n_keep):
    last = pl.num_programs(0) - 1

    def run(x):
        h_ref[...] = x
        a_ref[...] = _rms(x, g_ref[...]).astype(a_ref.dtype)

    @pl.when(pl.program_id(0) < last)
    def _():
        run(xp_ref[...])

    @pl.when(pl.program_id(0) == last)
    def _():
        run(_last_tile(xp_ref, xs_ref[...], n_keep))


def _rmsnorm_in(xp, xs, g, tm):
    tp, d = xp.shape
    bs = xs.shape[0]
    n_tiles, n_keep = _split_rows(tp, bs, tm)
    row = lambda i: (i, 0)
    fix = lambda i: (0, 0)
    return pl.pallas_call(
        functools.partial(_rmsnorm_in_body, n_keep=n_keep), grid=(n_tiles,),
        in_specs=[pl.BlockSpec((tm, d), row), pl.BlockSpec((bs, d), fix), pl.BlockSpec((1, d), fix)],
        out_specs=[pl.BlockSpec((tm, d), row), pl.BlockSpec((tm, d), row)],
        out_shape=[jax.ShapeDtypeStruct((tp + bs, d), f32), jax.ShapeDtypeStruct((tp + bs, d), bf16)],
        compiler_params=_params("parallel"), name="rmsnorm_in")(xp, xs, g.reshape(1, d))


def _rmsnorm_out_body(x_ref, g_ref, yp_ref, ys_ref, *, n_keep):
    y = _rms(x_ref[...], g_ref[...])
    yp_ref[...] = y

    @pl.when(pl.program_id(0) == pl.num_programs(0) - 1)
    def _():
        ys_ref[...] = y[n_keep:, :]


def _rmsnorm_out(x, g, tp, tm):
    m, d = x.shape
    bs = m - tp
    n_tiles, n_keep = _split_rows(tp, bs, tm)
    return pl.pallas_call(
        functools.partial(_rmsnorm_out_body, n_keep=n_keep), grid=(n_tiles,),
        in_specs=[pl.BlockSpec((tm, d), lambda i: (i, 0)), pl.BlockSpec((1, d), lambda i: (0, 0))],
        out_specs=[pl.BlockSpec((tm, d), lambda i: (i, 0)), pl.BlockSpec((bs, d), lambda i: (0, 0))],
        out_shape=[jax.ShapeDtypeStruct((tp, d), f32), jax.ShapeDtypeStruct((bs, d), f32)],
        compiler_params=_params("arbitrary"), name="rmsnorm_out")(x, g.reshape(1, d))


def _rmsnorm(x, g, out_dtype, tm):
    m, d = x.shape
    return pl.pallas_call(
        _rmsnorm_body, grid=(m // tm,),
        in_specs=[pl.BlockSpec((tm, d), lambda i: (i, 0)), pl.BlockSpec((1, d), lambda i: (0, 0))],
        out_specs=pl.BlockSpec((tm, d), lambda i: (i, 0)),
        out_shape=jax.ShapeDtypeStruct((m, d), out_dtype),
        compiler_params=_params("parallel"), name="rmsnorm")(x, g.reshape(1, d))


def _cast_weight_once(w_ref, wb_ref):
    @pl.when(pl.program_id(1) == 0)
    def _():
        wb_ref[...] = w_ref[...].astype(bf16)


def _mm_plain_body(a_ref, w_ref, o_ref, wb_ref):
    _cast_weight_once(w_ref, wb_ref)
    o_ref[...] = jnp.dot(a_ref[...], wb_ref[...], preferred_element_type=f32).astype(o_ref.dtype)


def _mm_resid_body(a_ref, w_ref, r_ref, o_ref, wb_ref):
    _cast_weight_once(w_ref, wb_ref)
    o_ref[...] = r_ref[...] + jnp.dot(a_ref[...], wb_ref[...], preferred_element_type=f32)


def _mm_swiglu_body(a_ref, wg_ref, wu_ref, o_ref, wgb_ref, wub_ref):
    _cast_weight_once(wg_ref, wgb_ref)
    _cast_weight_once(wu_ref, wub_ref)
    a = a_ref[...]
    g = jnp.dot(a, wgb_ref[...], preferred_element_type=f32)
    u = jnp.dot(a, wub_ref[...], preferred_element_type=f32)
    o_ref[...] = (g * jax.nn.sigmoid(g) * u).astype(o_ref.dtype)


def _mm_ple_body(e_ref, pp_ref, ps_ref, wg_ref, wp_ref, r_ref, o_ref, wgb_ref, wpb_ref, *, n_keep):
    _cast_weight_once(wg_ref, wgb_ref)
    _cast_weight_once(wp_ref, wpb_ref)
    last = pl.num_programs(1) - 1

    def run(p):
        gate = jnp.dot(e_ref[...], wgb_ref[...], preferred_element_type=f32)
        emb = jnp.dot(p.astype(bf16), wpb_ref[...], preferred_element_type=f32)
        o_ref[...] = r_ref[...] + emb * jax.nn.sigmoid(gate)

    @pl.when(pl.program_id(1) < last)
    def _():
        run(pp_ref[...])

    @pl.when(pl.program_id(1) == last)
    def _():
        run(_last_tile(pp_ref, ps_ref[...], n_keep))


def _mm_mix_body(a1p_ref, a1s_ref, a2_ref, a3p_ref, a3s_ref, w_ref, r_ref, o_ref, wb_ref, *, n_keep):
    _cast_weight_once(w_ref, wb_ref)
    last = pl.num_programs(1) - 1
    k1 = a1p_ref.shape[1]
    k2 = a2_ref.shape[1]

    def run(a1, a3):
        acc = jnp.dot(a1, wb_ref[0:k1, :], preferred_element_type=f32)
        acc += jnp.dot(a2_ref[...], wb_ref[k1:k1 + k2, :], preferred_element_type=f32)
        acc += jnp.dot(a3, wb_ref[k1 + k2:, :], preferred_element_type=f32)
        o_ref[...] = r_ref[...] + acc

    @pl.when(pl.program_id(1) < last)
    def _():
        run(a1p_ref[...], a3p_ref[...])

    @pl.when(pl.program_id(1) == last)
    def _():
        run(_last_tile(a1p_ref, a1s_ref[...].astype(bf16), n_keep),
            _last_tile(a3p_ref, a3s_ref[...].astype(bf16), n_keep))


def _mm_mix(a1p, a1s, a2, a3p, a3s, w, resid, *, tm, tn, layer, name):
    m, n = resid.shape
    tp, k1 = a1p.shape
    bs = a1s.shape[0]
    k2, k3 = a2.shape[1], a3p.shape[1]
    n_tiles, n_keep = _split_rows(tp, bs, tm)
    rows = lambda j, i: (i, 0)
    fix = lambda j, i: (0, 0)
    o_spec = pl.BlockSpec((tm, tn), lambda j, i: (i, j))
    return pl.pallas_call(
        functools.partial(_mm_mix_body, n_keep=n_keep), grid=(n // tn, n_tiles),
        in_specs=[pl.BlockSpec((tm, k1), rows), pl.BlockSpec((bs, k1), fix), pl.BlockSpec((tm, k2), rows),
                  pl.BlockSpec((tm, k3), rows), pl.BlockSpec((bs, k3), fix),
                  _wspec(w, layer, k1 + k2 + k3, tn), o_spec],
        out_specs=o_spec, out_shape=jax.ShapeDtypeStruct((m, n), f32),
        scratch_shapes=[pltpu.VMEM((k1 + k2 + k3, tn), bf16)],
        compiler_params=_params("arbitrary", "arbitrary"), name=name)(a1p, a1s, a2, a3p, a3s, w, resid)


def _wspec(w, layer, k, tn, col_block_off=0):
    if w.ndim == 2:
        return pl.BlockSpec((k, tn), lambda j, i: (0, j + col_block_off))
    return pl.BlockSpec((None, k, tn), lambda j, i: (layer, 0, j + col_block_off))


def _mm(a, w, *, tm, tn, layer=0, n_cols=None, col_off=0, resid=None, out_dtype=f32, name="mm"):
    m, k = a.shape
    n = n_cols if n_cols is not None else w.shape[-1]
    grid = (n // tn, m // tm)
    a_spec = pl.BlockSpec((tm, k), lambda j, i: (i, 0))
    o_spec = pl.BlockSpec((tm, tn), lambda j, i: (i, j))
    w_spec = _wspec(w, layer, k, tn, col_off // tn)
    scratch = [pltpu.VMEM((k, tn), bf16)]
    if resid is None:
        body, ins, specs = _mm_plain_body, (a, w), [a_spec, w_spec]
    else:
        body, ins, specs = _mm_resid_body, (a, w, resid), [a_spec, w_spec, o_spec]
    return pl.pallas_call(
        body, grid=grid, in_specs=specs, out_specs=o_spec,
        out_shape=jax.ShapeDtypeStruct((m, n), out_dtype), scratch_shapes=scratch,
        compiler_params=_params("arbitrary", "arbitrary"), name=name)(*ins)


def _mm_nt_body(a_ref, wt_ref, o_ref, wb_ref):
    @pl.when(pl.program_id(1) == 0)
    def _():
        wb_ref[...] = wt_ref[0].T.astype(bf16)

    o_ref[...] = jnp.dot(a_ref[...], wb_ref[...], preferred_element_type=f32)


def _wt_rows(layer, k, n_rows, row_of):
    return pl.BlockSpec((pl.Element(1), pl.Element(n_rows), pl.Element(k)),
                        lambda *idx: (layer, row_of(*idx), 0))


def _mm_nt(a, wt, *, tm, tn, layer, row_off, n_rows, name):
    m, k = a.shape
    return pl.pallas_call(
        _mm_nt_body, grid=(n_rows // tn, m // tm),
        in_specs=[pl.BlockSpec((tm, k), lambda j, i: (i, 0)),
                  _wt_rows(layer, k, tn, lambda j, i: pl.multiple_of(row_off + tn * j, 8))],
        out_specs=pl.BlockSpec((tm, tn), lambda j, i: (i, j)),
        out_shape=jax.ShapeDtypeStruct((m, n_rows), f32), scratch_shapes=[pltpu.VMEM((k, tn), bf16)],
        compiler_params=_params("arbitrary", "arbitrary"), name=name)(a, wt)


def _mm_gates_body(a_ref, w1_ref, w2_ref, o_ref):
    a = a_ref[...]
    n1 = w1_ref.shape[1]
    o_ref[:, 0:n1] = lax.dot_general(a, w1_ref[0].astype(bf16), _NT, preferred_element_type=f32)
    o_ref[:, n1:] = lax.dot_general(a, w2_ref[0].astype(bf16), _NT, preferred_element_type=f32)


def _mm_gates(a, wt, *, tm, layer, off1, n1, off2, n2, name):
    m, k = a.shape
    return pl.pallas_call(
        _mm_gates_body, grid=(m // tm,),
        in_specs=[pl.BlockSpec((tm, k), lambda i: (i, 0)),
                  _wt_rows(layer, k, n1, lambda i: off1), _wt_rows(layer, k, n2, lambda i: off2)],
        out_specs=pl.BlockSpec((tm, n1 + n2), lambda i: (i, 0)),
        out_shape=jax.ShapeDtypeStruct((m, n1 + n2), f32),
        compiler_params=_params("parallel"), name=name)(a, wt, wt)


def _mm_swiglu(a, wg, wu, *, tm, tn, layer=0, name="mm_swiglu"):
    m, k = a.shape
    n = wg.shape[-1]
    a_spec = pl.BlockSpec((tm, k), lambda j, i: (i, 0))
    o_spec = pl.BlockSpec((tm, tn), lambda j, i: (i, j))
    return pl.pallas_call(
        _mm_swiglu_body, grid=(n // tn, m // tm),
        in_specs=[a_spec, _wspec(wg, layer, k, tn), _wspec(wu, layer, k, tn)], out_specs=o_spec,
        out_shape=jax.ShapeDtypeStruct((m, n), bf16),
        scratch_shapes=[pltpu.VMEM((k, tn), bf16), pltpu.VMEM((k, tn), bf16)],
        compiler_params=_params("arbitrary", "arbitrary"), name=name)(a, wg, wu)


def _mm_ple(e, p_prompt, p_sample, w_gate, w_ple, resid, *, tm, tn, layer, name="mm_ple"):
    m, k = e.shape
    _, tp, kp = p_prompt.shape
    bs = p_sample.shape[1]
    n = w_gate.shape[-1]
    n_tiles, n_keep = _split_rows(tp, bs, tm)
    o_spec = pl.BlockSpec((tm, tn), lambda j, i: (i, j))
    return pl.pallas_call(
        functools.partial(_mm_ple_body, n_keep=n_keep), grid=(n // tn, n_tiles),
        in_specs=[pl.BlockSpec((tm, k), lambda j, i: (i, 0)),
                  pl.BlockSpec((None, tm, kp), lambda j, i: (layer, i, 0)),
                  pl.BlockSpec((None, bs, kp), lambda j, i: (layer, 0, 0)),
                  _wspec(w_gate, layer, k, tn), _wspec(w_ple, layer, kp, tn), o_spec],
        out_specs=o_spec, out_shape=jax.ShapeDtypeStruct((m, n), f32),
        scratch_shapes=[pltpu.VMEM((k, tn), bf16), pltpu.VMEM((kp, tn), bf16)],
        compiler_params=_params("arbitrary", "arbitrary"), name=name)(e, p_prompt, p_sample, w_gate, w_ple, resid)


def _col_from_row(row, eye):
    n = row.shape[1]
    return jnp.sum(jnp.where(eye, jnp.broadcast_to(row, (n, n)), 0.0), axis=1, keepdims=True)


def _mlstm_prompt_body(*refs, heads, dk, dv, cl, nb):
    q_refs, k_refs, v_refs, o_refs = (refs[i * nb:(i + 1) * nb] for i in range(4))
    gt_ref, bias_ref, gml_ref, h_ref, c_out, n_out, m_out, c_s, n_s, m_s = refs[4 * nb:]
    c = pl.program_id(1)

    @pl.when(c == 0)
    def _():
        c_s[...] = jnp.zeros_like(c_s)
        n_s[...] = jnp.zeros_like(n_s)
        m_s[...] = jnp.zeros_like(m_s)

    row = lax.broadcasted_iota(jnp.int32, (cl, cl), 0)
    col = lax.broadcasted_iota(jnp.int32, (cl, cl), 1)
    causal = col <= row
    eye = col == row
    triu = (row <= col).astype(f32)
    for r in range(nb):
        gt = gt_ref[r, 0] + bias_ref[...]
        lf = jax.nn.log_sigmoid(gt[heads:2 * heads])
        b_all = jnp.dot(lf, triu, precision=HI, preferred_element_type=f32)
        for hh in range(heads):
            st = r * heads + hh
            ig = gt[hh:hh + 1]
            b_row = b_all[hh:hh + 1]
            b_col = _col_from_row(b_row, eye)
            m_prev = m_s[st]
            d = jnp.where(causal, b_col - b_row + ig, -jnp.inf)
            inter = b_col + m_prev
            m_t = jnp.maximum(inter, jnp.max(d, axis=1, keepdims=True))
            w = jnp.exp(d - m_t)
            g = jnp.exp(inter - m_t)
            q = q_refs[r][:, hh * dk:(hh + 1) * dk]
            k = k_refs[r][:, hh * dk:(hh + 1) * dk] * (dk ** -0.5)
            vb = v_refs[r][:, hh * dv:(hh + 1) * dv].astype(bf16)
            qb = q.astype(bf16)
            cmat = c_s[st]
            n_row = n_s[st]
            s = lax.dot_general(qb, k.astype(bf16), _NT, preferred_element_type=f32) * w
            num = (jnp.dot(s.astype(bf16), vb, preferred_element_type=f32)
                   + g * jnp.dot(qb, cmat.astype(bf16), preferred_element_type=f32))
            den = jnp.sum(s, axis=1, keepdims=True) + g * jnp.sum(q * n_row, axis=1, keepdims=True)
            hraw = num / jnp.maximum(jnp.abs(den), jnp.exp(-m_t))
            hn = _rms(hraw, gml_ref[hh:hh + 1, :])
            ogate = jax.nn.sigmoid(o_refs[r][:, hh * dv:(hh + 1) * dv])
            h_ref[r, :, hh * dv:(hh + 1) * dv] = (ogate * hn).astype(h_ref.dtype)
            b_last = b_row[:, cl - 1:cl]
            dl = b_last - b_row + ig
            m_new = jnp.maximum(b_last + m_prev, jnp.max(dl, axis=1, keepdims=True))
            ws_col = _col_from_row(jnp.exp(dl - m_new), eye)
            gl = jnp.exp(b_last + m_prev - m_new)
            kw = k * ws_col
            c_s[st] = gl * cmat + lax.dot_general(kw.astype(bf16), vb, _TN, preferred_element_type=f32)
            n_s[st] = gl * n_row + jnp.sum(kw, axis=0, keepdims=True)
            m_s[st] = m_new

    @pl.when(c == pl.num_programs(1) - 1)
    def _():
        for r in range(nb):
            c_out[r] = c_s[r * heads:(r + 1) * heads]
            n_out[r] = n_s[r * heads:(r + 1) * heads]
            m_out[r] = m_s[r * heads:(r + 1) * heads]


_PROMPT_SEQS_PER_STEP = 4


def _mlstm_prompt(qkvo, gates_t, bias_col, g_ml, *, batch, seq):
    heads, dk, dv, cl = ML_HEADS, ML_DK, ML_DV, math.gcd(seq, CHUNK)
    nc = seq // cl
    wq = heads * dk
    nb = math.gcd(batch, _PROMPT_SEQS_PER_STEP)
    body = functools.partial(_mlstm_prompt_body, heads=heads, dk=dk, dv=dv, cl=cl, nb=nb)
    qkvo_specs = [pl.BlockSpec((cl, wq), lambda b, c, r=r, part=part: ((b * nb + r) * nc + c, part))
                  for part in range(4) for r in range(nb)]
    return pl.pallas_call(
        body, grid=(batch // nb, nc),
        in_specs=qkvo_specs + [pl.BlockSpec((nb, 1, 2 * heads, cl), lambda b, c: (b, c, 0, 0)),
                               pl.BlockSpec((2 * heads, 1), lambda b, c: (0, 0)),
                               pl.BlockSpec((heads, dv), lambda b, c: (0, 0))],
        out_specs=[pl.BlockSpec((nb, cl, heads * dv), lambda b, c: (b, c, 0)),
                   pl.BlockSpec((nb, heads, dk, dv), lambda b, c: (b, 0, 0, 0)),
                   pl.BlockSpec((nb, heads, 1, dk), lambda b, c: (b, 0, 0, 0)),
                   pl.BlockSpec((nb, heads, 1, 1), lambda b, c: (b, 0, 0, 0))],
        out_shape=[jax.ShapeDtypeStruct((batch, seq, heads * dv), bf16),
                   jax.ShapeDtypeStruct((batch, heads, dk, dv), f32),
                   jax.ShapeDtypeStruct((batch, heads, 1, dk), f32),
                   jax.ShapeDtypeStruct((batch, heads, 1, 1), f32)],
        scratch_shapes=[pltpu.VMEM((nb * heads, dk, dv), f32), pltpu.VMEM((nb * heads, 1, dk), f32),
                        pltpu.VMEM((nb * heads, 1, 1), f32)],
        compiler_params=_params("arbitrary", "arbitrary"), name="mlstm_prompt",
    )(*([qkvo] * (4 * nb)), gates_t, bias_col, g_ml)


def _lane_pick(pieces, width):
    lane = lax.broadcasted_iota(jnp.int32, (1, width), 1)
    out = jnp.zeros((1, width), f32)
    for i, p in enumerate(pieces):
        out = jnp.where(lane == i, p, out)
    return out


_SAMPLE_BLOCK = 8


def _mlstm_sample_body(*refs, heads, dk, dv, aliased):
    if aliased:
        refs = refs[:10] + refs[11:]
    (q_ref, k_ref, v_ref, o_ref, gd_ref, bias_ref, gml_ref, c0_ref, n0_ref, m0_ref,
     h_ref, c_out, n_out, m_out) = refs
    bt = q_ref.shape[0]

    @pl.when(pl.program_id(0) > 0)
    def _():
        c_out[...] = jnp.zeros_like(c_out)

    @pl.when(pl.program_id(0) == 0)
    def _():
        g8 = gd_ref[:, 0:2 * heads] + bias_ref[...]
        lane = lax.broadcasted_iota(jnp.int32, (bt, heads), 1)
        m_new = jnp.zeros((bt, heads), f32)
        for hh in range(heads):
            ig = g8[:, hh:hh + 1]
            lf = jax.nn.log_sigmoid(g8[:, heads + hh:heads + hh + 1])
            m0 = m0_ref[:, hh:hh + 1]
            m_t = jnp.maximum(lf + m0, ig)
            w = jnp.exp(ig - m_t)
            g = jnp.exp(lf + m0 - m_t)
            q = q_ref[:, hh * dk:(hh + 1) * dk]
            k = k_ref[:, hh * dk:(hh + 1) * dk] * (dk ** -0.5)
            v = v_ref[:, hh * dv:(hh + 1) * dv]
            o = o_ref[:, hh * dv:(hh + 1) * dv]
            n0 = n0_ref[:, hh, :]
            kw = k * w
            q_t = q.T
            kw_t = kw.T
            qc_rows = []
            for b in range(bt):
                cmat = c0_ref[b, hh]
                qc_rows.append(jnp.sum(q_t[:, b:b + 1] * cmat, axis=0, keepdims=True))
                c_out[b, hh] = g[b:b + 1, :] * cmat + kw_t[:, b:b + 1] * v[b:b + 1, :]
            qc = jnp.concatenate(qc_rows, axis=0)
            s = jnp.sum(q * k, axis=1, keepdims=True) * w
            num = s * v + g * qc
            den = s + g * jnp.sum(q * n0, axis=1, keepdims=True)
            hraw = num / jnp.maximum(jnp.abs(den), jnp.exp(-m_t))
            hn = _rms(hraw, gml_ref[hh:hh + 1, :])
            h_ref[:, hh * dv:(hh + 1) * dv] = jax.nn.sigmoid(o) * hn
            n_out[:, hh, :] = g * n0 + kw
            m_new = jnp.where(lane == hh, m_t, m_new)
        m_out[...] = m_new


def _stacked_state_grid(layer, b, stack):
    n_l = DEPTH if stack is None else 1
    row = lambda l, i: jnp.where(l == 0, i, b - 1)
    out_layer = lambda l: (layer + l) % DEPTH
    return (n_l, b), row, out_layer


def _mlstm_sample(qkvo, gd, bias_row, g_ml, c0, n0, m0, *, layer, row0, c_stack=None):
    heads, dk, dv = ML_HEADS, ML_DK, ML_DV
    b = c0.shape[1]
    bt = _SAMPLE_BLOCK
    wq = heads * dk
    blk0 = row0 // bt
    grid, row, out_layer = _stacked_state_grid(layer, b // bt, c_stack)
    aliased = c_stack is not None
    body = functools.partial(_mlstm_sample_body, heads=heads, dk=dk, dv=dv, aliased=aliased)
    in_specs = [pl.BlockSpec((bt, wq), lambda l, i, c=c: (blk0 + row(l, i), c)) for c in range(4)]
    in_specs += [pl.BlockSpec((bt, gd.shape[-1]), lambda l, i: (blk0 + row(l, i), 0)),
                 pl.BlockSpec((1, 2 * heads), lambda l, i: (0, 0)),
                 pl.BlockSpec((heads, dv), lambda l, i: (0, 0)),
                 pl.BlockSpec((None, bt, heads, dk, dv), lambda l, i: (layer, row(l, i), 0, 0, 0)),
                 pl.BlockSpec((None, bt, heads, dk), lambda l, i: (layer, row(l, i), 0, 0)),
                 pl.BlockSpec((None, bt, heads), lambda l, i: (layer, row(l, i), 0))]
    args = [qkvo, qkvo, qkvo, qkvo, gd, bias_row, g_ml, c0, n0, m0]
    if aliased:
        in_specs.append(pl.BlockSpec(memory_space=pl.ANY))
        args.append(c_stack)
    return pl.pallas_call(
        body, grid=grid, in_specs=in_specs,
        out_specs=[pl.BlockSpec((bt, heads * dv), lambda l, i: (row(l, i), 0)),
                   pl.BlockSpec((None, bt, heads, dk, dv), lambda l, i: (out_layer(l), i, 0, 0, 0)),
                   pl.BlockSpec((bt, heads, dk), lambda l, i: (row(l, i), 0, 0)),
                   pl.BlockSpec((bt, heads), lambda l, i: (row(l, i), 0))],
        out_shape=[jax.ShapeDtypeStruct((b, heads * dv), f32),
                   jax.ShapeDtypeStruct((DEPTH, b, heads, dk, dv), f32),
                   jax.ShapeDtypeStruct((b, heads, dk), f32),
                   jax.ShapeDtypeStruct((b, heads), f32)],
        input_output_aliases={10: 1} if aliased else {},
        compiler_params=_params("arbitrary", "arbitrary"), name="mlstm_sample",
    )(*args)


def _s5_advance(x, l1, l2):
    return l1 * x + l2 * pltpu.roll(x, x.shape[-1] // 2, axis=1)


def _s5_body(u_ref, m_ref, w_ref, v_ref, l1_ref, l2_ref, x0_ref, y_ref, x_out, xs_s, *, nc, batch):
    u = u_ref[0]
    xin = jnp.dot(u, w_ref[0], preferred_element_type=f32)
    l1 = l1_ref[0]
    l2 = l2_ref[0]
    x = x0_ref[0]
    for k in range(nc):
        sl = slice(k * batch, (k + 1) * batch)
        xs_s[sl, :] = x
        x = _s5_advance(x, l1, l2) + xin[sl, :]
    x_out[0] = x
    y_ref[0] = (jnp.dot(u, m_ref[0], preferred_element_type=f32)
                + jnp.dot(xs_s[...].astype(bf16), v_ref[0], preferred_element_type=f32))


def _s5_scan(u_g, mats, x0, *, nc, batch):
    m_mat, w_pk, v_pk, l1, l2 = mats
    g, rows, tc = u_g.shape
    p2 = w_pk.shape[-1]
    blk = lambda *s: pl.BlockSpec((1,) + s, lambda i: (i,) + (0,) * len(s))
    body = functools.partial(_s5_body, nc=nc, batch=batch)
    return pl.pallas_call(
        body, grid=(g,),
        in_specs=[blk(rows, tc), blk(tc, tc), blk(tc, p2), blk(p2, tc), blk(1, p2), blk(1, p2), blk(batch, p2)],
        out_specs=[blk(rows, tc), blk(batch, p2)],
        out_shape=[jax.ShapeDtypeStruct((g, rows, tc), f32), jax.ShapeDtypeStruct((g, batch, p2), f32)],
        scratch_shapes=[pltpu.VMEM((rows, p2), f32)],
        compiler_params=_params("parallel"), name=f"s5_scan_t{tc // S5_CH}",
    )(u_g, m_mat, w_pk, v_pk, l1, l2, x0)


def _s5_prompt_body(x_ref, m_ref, w_ref, v_ref, l1_ref, l2_ref, y_ref, x_out, u_s, y_s, xin_s, xs_s,
                    *, gw, batch, seq, t, ch):
    nc = seq // t
    for b in range(batch):
        for tt in range(t):
            blk = x_ref[pl.ds(b * seq + tt, nc, stride=t), :]
            for gl in range(gw):
                u_s[gl, b * nc:(b + 1) * nc, tt * ch:(tt + 1) * ch] = blk[:, gl * ch:(gl + 1) * ch]
    for gl in range(gw):
        xin_s[gl] = jnp.dot(u_s[gl].astype(bf16), w_ref[gl], preferred_element_type=f32)
    xs = [jnp.zeros((batch, xin_s.shape[-1]), f32)] * gw
    for k in range(nc):
        for gl in range(gw):
            xs_s[gl, pl.ds(k, batch, stride=nc), :] = xs[gl]
            xs[gl] = _s5_advance(xs[gl], l1_ref[gl], l2_ref[gl]) + xin_s[gl, pl.ds(k, batch, stride=nc), :]
    for gl in range(gw):
        x_out[gl] = xs[gl]
        y_s[gl] = (jnp.dot(u_s[gl].astype(bf16), m_ref[gl], preferred_element_type=f32)
                   + jnp.dot(xs_s[gl].astype(bf16), v_ref[gl], preferred_element_type=f32))
    for b in range(batch):
        for tt in range(t):
            y_ref[pl.ds(b * seq + tt, nc, stride=t), :] = jnp.concatenate(
                [y_s[gl, b * nc:(b + 1) * nc, tt * ch:(tt + 1) * ch] for gl in range(gw)], axis=1)


def _s5_prompt(uzx, mats, *, batch, seq):
    m_mat, w_pk, v_pk, l1, l2 = mats
    g, tc, p2 = w_pk.shape
    ch = S5_CH
    t = tc // ch
    gw = 128 // ch
    rows = batch * (seq // t)
    tp = batch * seq
    win = lambda *s: pl.BlockSpec((gw,) + s, lambda i: (i,) + (0,) * len(s))
    body = functools.partial(_s5_prompt_body, gw=gw, batch=batch, seq=seq, t=t, ch=ch)
    return pl.pallas_call(
        body, grid=(g // gw,),
        in_specs=[pl.BlockSpec((tp, gw * ch), lambda i: (0, i)), win(tc, tc), win(tc, p2), win(p2, tc),
                  win(1, p2), win(1, p2)],
        out_specs=[pl.BlockSpec((tp, gw * ch), lambda i: (0, i)), win(batch, p2)],
        out_shape=[jax.ShapeDtypeStruct((tp, g * ch), f32), jax.ShapeDtypeStruct((g, batch, p2), f32)],
        scratch_shapes=[pltpu.VMEM((gw, rows, tc), f32), pltpu.VMEM((gw, rows, tc), f32),
                        pltpu.VMEM((gw, rows, p2), f32), pltpu.VMEM((gw, rows, p2), f32)],
        compiler_params=_params("parallel"), name="s5_prompt",
    )(uzx, m_mat, w_pk, v_pk, l1, l2)


def _s5_matrices(lam_re, lam_im, log_dt, b_re, b_im, c_re, c_im, t):
    g, p = lam_re.shape
    ch = b_re.shape[-1]
    dt = jnp.exp(log_dt)[:, None]
    ar, ai = lam_re * dt, lam_im * dt

    def powers(tau):
        mag = jnp.exp(ar[:, None, :] * tau[None, :, None])
        ang = ai[:, None, :] * tau[None, :, None]
        return mag * jnp.cos(ang), mag * jnp.sin(ang)

    lbr, lbi = jnp.exp(ar) * jnp.cos(ai), jnp.exp(ar) * jnp.sin(ai)
    den = lam_re * lam_re + lam_im * lam_im
    fr = ((lbr - 1.0) * lam_re + lbi * lam_im) / den
    fi = (lbi * lam_re - (lbr - 1.0) * lam_im) / den
    bbr = jnp.swapaxes(fr[..., None] * b_re - fi[..., None] * b_im, 1, 2)
    bbi = jnp.swapaxes(fr[..., None] * b_im + fi[..., None] * b_re, 1, 2)
    cbr = c_re[:, :, None, :] * bbr[:, None, :, :] - c_im[:, :, None, :] * bbi[:, None, :, :]
    cbi = c_re[:, :, None, :] * bbi[:, None, :, :] + c_im[:, :, None, :] * bbr[:, None, :, :]
    steps = jnp.arange(t, dtype=f32)
    diff = steps[None, :] - steps[:, None]
    lag = jnp.maximum(diff, 0.0)
    mag = jnp.exp(ar[:, None, None, :] * lag[None, :, :, None])
    ang = ai[:, None, None, :] * lag[None, :, :, None]
    keep = (diff >= 0.0)[None, :, :, None]
    e2 = jnp.concatenate([jnp.where(keep, mag * jnp.cos(ang), 0.0),
                          jnp.where(keep, mag * jnp.sin(ang), 0.0)], axis=-1)
    cb2 = jnp.concatenate([cbr, -cbi], axis=-1)
    m_mat = jnp.einsum("gstp,gcdp->gsdtc", e2, cb2, precision=HI).reshape(g, t * ch, t * ch)
    pr, pi = powers(t - 1.0 - steps)
    w_re = (pr[:, :, None, :] * bbr[:, None, :, :] - pi[:, :, None, :] * bbi[:, None, :, :]).reshape(g, t * ch, p)
    w_im = (pr[:, :, None, :] * bbi[:, None, :, :] + pi[:, :, None, :] * bbr[:, None, :, :]).reshape(g, t * ch, p)
    qr, qi = powers(steps + 1.0)
    qr, qi = jnp.swapaxes(qr, 1, 2)[..., None], jnp.swapaxes(qi, 1, 2)[..., None]
    ctr, cti = jnp.swapaxes(c_re, 1, 2)[:, :, None, :], jnp.swapaxes(c_im, 1, 2)[:, :, None, :]
    v_re = (ctr * qr - cti * qi).reshape(g, p, t * ch)
    v_im = -(ctr * qi + cti * qr).reshape(g, p, t * ch)
    ltr, lti = powers(jnp.full((1,), float(t), f32))
    return (m_mat.astype(bf16), jnp.concatenate([w_re, w_im], axis=-1).astype(bf16),
            jnp.concatenate([v_re, v_im], axis=1).astype(bf16),
            jnp.concatenate([ltr, ltr], axis=-1), jnp.concatenate([-lti, lti], axis=-1))


def _s5_glu_body(yp_ref, ys_ref, u_ref, d_ref, w_ref, b_ref, g_ref, o_ref, *, n_keep):
    last = pl.num_programs(0) - 1

    def run(y_raw):
        y5 = jax.nn.gelu(y_raw + d_ref[...] * u_ref[...])
        gate = jax.nn.sigmoid(jnp.dot(y5.astype(bf16), w_ref[...].astype(bf16), preferred_element_type=f32)
                              + b_ref[...])
        o_ref[...] = _rms(y5 * gate, g_ref[...]).astype(o_ref.dtype)

    @pl.when(pl.program_id(0) < last)
    def _():
        run(yp_ref[...])

    @pl.when(pl.program_id(0) == last)
    def _():
        run(_last_tile(yp_ref, ys_ref[...], n_keep))


def _s5_glu(y_prompt, y_sample, uzx, d_skip, w_glu, b_glu, g_s5, *, layer, tm):
    tp, wdt = y_prompt.shape
    bs = y_sample.shape[0]
    n_tiles, n_keep = _split_rows(tp, bs, tm)
    row = lambda i: (i, 0)
    fix = lambda i: (0, 0)
    return pl.pallas_call(
        functools.partial(_s5_glu_body, n_keep=n_keep), grid=(n_tiles,),
        in_specs=[pl.BlockSpec((tm, wdt), row), pl.BlockSpec((bs, wdt), fix), pl.BlockSpec((tm, wdt), row),
                  pl.BlockSpec((1, wdt), fix), pl.BlockSpec((None, wdt, wdt), lambda i: (layer, 0, 0)),
                  pl.BlockSpec((1, wdt), fix), pl.BlockSpec((1, wdt), fix)],
        out_specs=pl.BlockSpec((tm, wdt), row), out_shape=jax.ShapeDtypeStruct((tp + bs, wdt), bf16),
        compiler_params=_params("parallel"), name="s5_glu",
    )(y_prompt, y_sample, uzx, d_skip, w_glu, b_glu, g_s5)


def _ssd_prompt_body(*refs, heads, hd, ns, groups, cl, width, nb):
    xbc_refs, z_refs, gd_refs = (refs[i * nb:(i + 1) * nb] for i in range(3))
    (dtt_ref, cw_ref, cb_ref, dtb_row, dtb_col, alog_row, alog_col, dskip_ref, gssd_ref,
     y_ref, s_out, s_s, xp_s, ys_s) = refs[3 * nb:]
    c = pl.program_id(1)

    @pl.when(c == 0)
    def _():
        s_s[...] = jnp.zeros_like(s_s)
        for r in range(nb):
            xp_s[r, 0:8, :] = jnp.zeros((8, xp_s.shape[2]), f32)

    row = lax.broadcasted_iota(jnp.int32, (cl, cl), 0)
    col = lax.broadcasted_iota(jnp.int32, (cl, cl), 1)
    causal = col <= row
    tril = causal.astype(f32)
    triu = (row <= col).astype(f32)
    rep = heads // groups
    for r in range(nb):
        xp_s[r, 8:8 + cl, :] = xbc_refs[r][...]
        xc = cb_ref[...] + sum(cw_ref[j:j + 1, :] * xp_s[r, 5 + j:5 + j + cl, :] for j in range(SSD_CONV))
        xp_s[r, 0:8, :] = xp_s[r, cl:cl + 8, :]
        xc = xc * jax.nn.sigmoid(xc)
        dt_col = jax.nn.softplus(gd_refs[r][:, 8:8 + heads] + dtb_row[...])
        dt_row = jax.nn.softplus(dtt_ref[r, 0] + dtb_col[...])
        cum_col = jnp.dot(tril, dt_col * -jnp.exp(alog_row[...]), precision=HI, preferred_element_type=f32)
        cum_row = jnp.dot(dt_row * -jnp.exp(alog_col[...]), triu, precision=HI, preferred_element_type=f32)
        for gi in range(groups):
            bm = xc[:, width + gi * ns:width + (gi + 1) * ns].astype(bf16)
            cm = xc[:, width + (groups + gi) * ns:width + (groups + gi + 1) * ns].astype(bf16)
            scores = lax.dot_general(cm, bm, _NT, preferred_element_type=f32)
            for hh in range(gi * rep, (gi + 1) * rep):
                st = r * heads + hh
                cc = cum_col[:, hh:hh + 1]
                cr = cum_row[hh:hh + 1, :]
                seg = jnp.exp(jnp.where(causal, cc - cr, -jnp.inf))
                xh = xc[:, hh * hd:(hh + 1) * hd]
                xdt = xh * dt_col[:, hh:hh + 1]
                smat = s_s[st]
                y = (jnp.dot((scores * seg).astype(bf16), xdt.astype(bf16), preferred_element_type=f32)
                     + jnp.exp(cc) * lax.dot_general(cm, smat.astype(bf16), _NT, preferred_element_type=f32))
                c_last = cr[:, cl - 1:cl]
                xw = (xdt * jnp.exp(c_last - cc)).astype(bf16)
                s_s[st] = jnp.exp(c_last) * smat + lax.dot_general(xw, bm, _TN, preferred_element_type=f32)
                ys_s[r, :, hh * hd:(hh + 1) * hd] = y + dskip_ref[:, hh:hh + 1] * xh
        z = z_refs[r][...]
        y_ref[r] = _rms(ys_s[r] * (z * jax.nn.sigmoid(z)), gssd_ref[...]).astype(y_ref.dtype)

    @pl.when(c == pl.num_programs(1) - 1)
    def _():
        for r in range(nb):
            s_out[r] = s_s[r * heads:(r + 1) * heads]


def _ssd_prompt(uzx, gd, dt_t, conv_w, conv_b, dt_bias, a_log, d_skip, g_ssd, *, batch, seq):
    heads, hd, ns, groups, width = SSD_HEADS, SSD_HEAD_DIM, SSD_STATE, SSD_GROUPS, SSD_WIDTH
    cl = math.gcd(seq, CHUNK)
    nc = seq // cl
    cch = SSD_CONV_CH
    nb = 1
    fix = lambda b, c: (0, 0)
    body = functools.partial(_ssd_prompt_body, heads=heads, hd=hd, ns=ns, groups=groups, cl=cl, width=width, nb=nb)
    seq_rows = lambda r, part: (lambda b, c: ((b * nb + r) * nc + c, part))
    in_specs = ([pl.BlockSpec((cl, cch), seq_rows(r, 1)) for r in range(nb)]
                + [pl.BlockSpec((cl, width), seq_rows(r, 1)) for r in range(nb)]
                + [pl.BlockSpec((cl, gd.shape[1]), seq_rows(r, 0)) for r in range(nb)]
                + [pl.BlockSpec((nb, 1, heads, cl), lambda b, c: (b, c, 0, 0)),
                   pl.BlockSpec((SSD_CONV, cch), fix), pl.BlockSpec((1, cch), fix),
                   pl.BlockSpec((1, heads), fix), pl.BlockSpec((heads, 1), fix),
                   pl.BlockSpec((1, heads), fix), pl.BlockSpec((heads, 1), fix),
                   pl.BlockSpec((1, heads), fix), pl.BlockSpec((1, width), fix)])
    return pl.pallas_call(
        body, grid=(batch // nb, nc), in_specs=in_specs,
        out_specs=[pl.BlockSpec((nb, cl, width), lambda b, c: (b, c, 0)),
                   pl.BlockSpec((nb, heads, hd, ns), lambda b, c: (b, 0, 0, 0))],
        out_shape=[jax.ShapeDtypeStruct((batch, seq, width), bf16),
                   jax.ShapeDtypeStruct((batch, heads, hd, ns), f32)],
        scratch_shapes=[pltpu.VMEM((nb * heads, hd, ns), f32), pltpu.VMEM((nb, cl + 8, cch), f32),
                        pltpu.VMEM((nb, cl, width), f32)],
        compiler_params=_params("arbitrary", "arbitrary"), name="ssd_prompt",
    )(*([uzx] * (2 * nb)), *([gd] * nb), dt_t, conv_w, conv_b.reshape(1, cch), dt_bias.reshape(1, heads),
      dt_bias.reshape(heads, 1), a_log.reshape(1, heads), a_log.reshape(heads, 1), d_skip.reshape(1, heads),
      g_ssd.reshape(1, width))


def _ssd_sample_body(*refs, heads, hd, ns, groups, width, aliased):
    if aliased:
        refs = refs[:10] + refs[11:]
    (x_ref, conv0_ref, gd_ref, cw_ref, cb_ref, dtb_ref, alog_ref, dskip_ref, gssd_ref, s0_ref,
     y_ref, s_out, ys_s) = refs
    bt = x_ref.shape[0]

    @pl.when(pl.program_id(0) > 0)
    def _():
        s_out[...] = jnp.zeros_like(s_out)

    @pl.when(pl.program_id(0) == 0)
    def _():
        z = x_ref[:, width:2 * width]
        xc = cb_ref[...] + cw_ref[SSD_CONV - 1:SSD_CONV, :] * x_ref[:, 2 * width:]
        for j in range(SSD_CONV - 1):
            xc = xc + cw_ref[j:j + 1, :] * conv0_ref[:, j, :]
        xc = xc * jax.nn.sigmoid(xc)
        dt = jax.nn.softplus(gd_ref[:, 8:8 + heads] + dtb_ref[...])
        ea = jnp.exp(dt * -jnp.exp(alog_ref[...]))
        rep = heads // groups
        for gi in range(groups):
            bm = xc[:, width + gi * ns:width + (gi + 1) * ns]
            cm = xc[:, width + (groups + gi) * ns:width + (groups + gi + 1) * ns]
            cb_dot = jnp.sum(cm * bm, axis=1, keepdims=True)
            cmb = cm.astype(bf16)
            for hh in range(gi * rep, (gi + 1) * rep):
                xh = xc[:, hh * hd:(hh + 1) * hd]
                xdt = xh * dt[:, hh:hh + 1]
                eah = ea[:, hh:hh + 1]
                xdt_t = xdt.T
                sc_rows = []
                for b in range(bt):
                    smat = s0_ref[b, hh]
                    sc_rows.append(lax.dot_general(cmb, smat.astype(bf16), _NT,
                                                   preferred_element_type=f32)[b:b + 1, :])
                    s_out[b, hh] = eah[b:b + 1, :] * smat + xdt_t[:, b:b + 1] * bm[b:b + 1, :]
                sc = jnp.concatenate(sc_rows, axis=0)
                ys_s[:, hh * hd:(hh + 1) * hd] = cb_dot * xdt + eah * sc + dskip_ref[:, hh:hh + 1] * xh
        y_ref[...] = _rms(ys_s[...] * (z * jax.nn.sigmoid(z)), gssd_ref[...])


def _ssd_sample(uzx, conv0, gd, conv_w, conv_b, dt_bias, a_log, d_skip, g_ssd, s0, *, layer, row0, s_stack=None):
    heads, hd, ns, groups, width = SSD_HEADS, SSD_HEAD_DIM, SSD_STATE, SSD_GROUPS, SSD_WIDTH
    cch = SSD_CONV_CH
    b = s0.shape[1]
    bt = _SAMPLE_BLOCK
    blk0 = row0 // bt
    fix = lambda l, i: (0, 0)
    grid, row, out_layer = _stacked_state_grid(layer, b // bt, s_stack)
    aliased = s_stack is not None
    body = functools.partial(_ssd_sample_body, heads=heads, hd=hd, ns=ns, groups=groups, width=width,
                             aliased=aliased)
    in_specs = [pl.BlockSpec((bt, uzx.shape[-1]), lambda l, i: (blk0 + row(l, i), 0)),
                pl.BlockSpec((None, bt, SSD_CONV - 1, cch), lambda l, i: (layer, row(l, i), 0, 0)),
                pl.BlockSpec((bt, gd.shape[-1]), lambda l, i: (blk0 + row(l, i), 0)),
                pl.BlockSpec((SSD_CONV, cch), fix), pl.BlockSpec((1, cch), fix),
                pl.BlockSpec((1, heads), fix), pl.BlockSpec((1, heads), fix), pl.BlockSpec((1, heads), fix),
                pl.BlockSpec((1, width), fix),
                pl.BlockSpec((None, bt, heads, hd, ns), lambda l, i: (layer, row(l, i), 0, 0, 0))]
    args = [uzx, conv0, gd, conv_w, conv_b.reshape(1, cch), dt_bias.reshape(1, heads), a_log.reshape(1, heads),
            d_skip.reshape(1, heads), g_ssd.reshape(1, width), s0]
    if aliased:
        in_specs.append(pl.BlockSpec(memory_space=pl.ANY))
        args.append(s_stack)
    return pl.pallas_call(
        body, grid=grid, in_specs=in_specs,
        out_specs=[pl.BlockSpec((bt, width), lambda l, i: (row(l, i), 0)),
                   pl.BlockSpec((None, bt, heads, hd, ns), lambda l, i: (out_layer(l), i, 0, 0, 0))],
        out_shape=[jax.ShapeDtypeStruct((b, width), f32),
                   jax.ShapeDtypeStruct((DEPTH, b, heads, hd, ns), f32)],
        scratch_shapes=[pltpu.VMEM((bt, width), f32)],
        input_output_aliases={10: 1} if aliased else {},
        compiler_params=_params("arbitrary", "arbitrary"), name="ssd_sample",
    )(*args)


def _router_body(h_ref, g_ref, wr_ref, br_ref, cf_ref, idx_ref, gate_ref, *, n_exp):
    cf = _rms(h_ref[...], g_ref[...])
    cf_ref[...] = cf
    logits = jnp.dot(cf, wr_ref[...], precision=HI, preferred_element_type=f32) + br_ref[...]
    lane = lax.broadcasted_iota(jnp.int32, logits.shape, 1)
    m1 = jnp.max(logits, axis=1, keepdims=True)
    i1 = jnp.min(jnp.where(logits == m1, lane, n_exp), axis=1, keepdims=True)
    rest = jnp.where(lane == i1, -jnp.inf, logits)
    m2 = jnp.max(rest, axis=1, keepdims=True)
    i2 = jnp.min(jnp.where(rest == m2, lane, n_exp), axis=1, keepdims=True)
    e2 = jnp.exp(m2 - m1)
    g1 = 1.0 / (1.0 + e2)
    two = lax.broadcasted_iota(jnp.int32, (logits.shape[0], TOP_K), 1)
    idx_ref[...] = jnp.where(two == 0, i1, i2)
    gate_ref[...] = jnp.where(two == 0, g1, e2 * g1)


def _router(h, g_ffn, w_router, b_router, *, tm):
    m, d = h.shape
    e = w_router.shape[-1]
    row = lambda i: (i, 0)
    fix = lambda i: (0, 0)
    return pl.pallas_call(
        functools.partial(_router_body, n_exp=e), grid=(m // tm,),
        in_specs=[pl.BlockSpec((tm, d), row), pl.BlockSpec((1, d), fix), pl.BlockSpec((d, e), fix),
                  pl.BlockSpec((1, e), fix)],
        out_specs=[pl.BlockSpec((tm, d), row), pl.BlockSpec((tm, TOP_K), row),
                   pl.BlockSpec((tm, TOP_K), row)],
        out_shape=[jax.ShapeDtypeStruct((m, d), f32),
                   jax.ShapeDtypeStruct((m, TOP_K), jnp.int32), jax.ShapeDtypeStruct((m, TOP_K), f32)],
        compiler_params=_params("parallel"), name="router",
    )(h, g_ffn.reshape(1, d), w_router, b_router.reshape(1, e))


def _row_copy(src_hbm, dst, sem, src_row, dst_row):
    return pltpu.make_async_copy(src_hbm.at[pl.ds(src_row, 1)], dst.at[pl.ds(dst_row, 1)], sem)


_DMA_UNROLL = 8


def _gather_body(idx_ref, x_hbm, o_ref, buf, sem, *, tg):
    i = pl.program_id(0)

    def issue(tile, slot):
        base = tile * tg

        def start(r, carry):
            _row_copy(x_hbm, buf.at[slot], sem.at[slot], idx_ref[base + r], r).start()
            return carry

        lax.fori_loop(0, tg, start, 0, unroll=_DMA_UNROLL)

    @pl.when(i == 0)
    def _():
        issue(0, 0)

    @pl.when(i + 1 < pl.num_programs(0))
    def _():
        issue(i + 1, (i + 1) % 2)

    slot = i % 2

    def wait(r, carry):
        _row_copy(x_hbm, buf.at[slot], sem.at[slot], 0, r).wait()
        return carry

    lax.fori_loop(0, tg, wait, 0, unroll=_DMA_UNROLL)
    o_ref[...] = buf[slot].astype(o_ref.dtype)


def _gather_rows(x, row_idx, *, tg, out_dtype):
    r = row_idx.shape[0]
    d = x.shape[1]
    return pl.pallas_call(
        functools.partial(_gather_body, tg=tg),
        grid_spec=pltpu.PrefetchScalarGridSpec(
            num_scalar_prefetch=1, grid=(r // tg,),
            in_specs=[pl.BlockSpec(memory_space=pl.ANY)],
            out_specs=pl.BlockSpec((tg, d), lambda i, idx: (i, 0)),
            scratch_shapes=[pltpu.VMEM((2, tg, d), x.dtype), pltpu.SemaphoreType.DMA((2,))]),
        out_shape=jax.ShapeDtypeStruct((r, d), out_dtype),
        compiler_params=_params("arbitrary"), name="gather_rows")(row_idx, x)


def _gmm_body(te_ref, first_ref, next_ref, nv_ref, a_ref, *rest, n_w, swiglu):
    w_hbm = rest[:n_w]
    o_ref, wf_ref, wb_ref, sem, run_ref = rest[n_w:]
    j = pl.program_id(0)
    t = pl.program_id(1)
    tn = o_ref.shape[1]

    def tile_copies(expert, col_tile, slot):
        col = pl.multiple_of(col_tile * tn, 128)
        return [pltpu.make_async_copy(w_hbm[i].at[expert, :, pl.ds(col, tn)], wf_ref.at[slot, i], sem.at[slot, i])
                for i in range(n_w)]

    @pl.when((j == 0) & (t == 0))
    def _():
        run_ref[0] = 0
        for c in tile_copies(te_ref[0], 0, 0):
            c.start()

    @pl.when((t < nv_ref[0]) & (first_ref[t] == 1))
    def _():
        slot = run_ref[0] % 2
        for c in tile_copies(te_ref[t], j, slot):
            c.wait()
        for i in range(n_w):
            wb_ref[i] = wf_ref[slot, i].astype(bf16)
        nxt = next_ref[t]

        @pl.when(nxt >= 0)
        def _():
            for c in tile_copies(te_ref[nxt], j, 1 - slot):
                c.start()

        @pl.when((nxt < 0) & (j + 1 < pl.num_programs(0)))
        def _():
            for c in tile_copies(te_ref[0], j + 1, 1 - slot):
                c.start()

        run_ref[0] = run_ref[0] + 1

    @pl.when(t < nv_ref[0])
    def _():
        a = a_ref[...]
        if swiglu:
            g = jnp.dot(a, wb_ref[0], preferred_element_type=f32)
            u = jnp.dot(a, wb_ref[1], preferred_element_type=f32)
            o_ref[...] = (g * jax.nn.sigmoid(g) * u).astype(o_ref.dtype)
        else:
            o_ref[...] = jnp.dot(a, wb_ref[0], preferred_element_type=f32).astype(o_ref.dtype)

    @pl.when(t >= nv_ref[0])
    def _():
        o_ref[...] = jnp.zeros_like(o_ref)


def _gmm(a_sorted, weights, tables, *, tm, tn, swiglu, out_dtype, name):
    tile_expert, tile_first, tile_next, n_valid = tables
    r, k = a_sorted.shape
    n = weights[0].shape[-1]
    n_w = len(weights)
    return pl.pallas_call(
        functools.partial(_gmm_body, n_w=n_w, swiglu=swiglu),
        grid_spec=pltpu.PrefetchScalarGridSpec(
            num_scalar_prefetch=4, grid=(n // tn, r // tm),
            in_specs=[pl.BlockSpec((tm, k), lambda j, t, *_: (t, 0))] + [pl.BlockSpec(memory_space=pl.ANY)] * n_w,
            out_specs=pl.BlockSpec((tm, tn), lambda j, t, *_: (t, j)),
            scratch_shapes=[pltpu.VMEM((2, n_w, k, tn), f32), pltpu.VMEM((n_w, k, tn), bf16),
                            pltpu.SemaphoreType.DMA((2, n_w)), pltpu.SMEM((1,), jnp.int32)]),
        out_shape=jax.ShapeDtypeStruct((r, n), out_dtype),
        compiler_params=_params("arbitrary", "arbitrary"), name=name,
    )(tile_expert, tile_first, tile_next, n_valid, a_sorted, *weights)


def _combine_body(pos_ref, y_hbm, gate_ref, r_ref, o_ref, buf, sem, *, tc):
    i = pl.program_id(0)

    def issue(tile, slot):
        base = tile * tc

        def start(r, carry):
            for kk in range(TOP_K):
                _row_copy(y_hbm, buf.at[slot, kk], sem.at[slot], pos_ref[(base + r) * TOP_K + kk], r).start()
            return carry

        lax.fori_loop(0, tc, start, 0, unroll=_DMA_UNROLL)

    @pl.when(i == 0)
    def _():
        issue(0, 0)

    @pl.when(i + 1 < pl.num_programs(0))
    def _():
        issue(i + 1, (i + 1) % 2)

    slot = i % 2

    def wait(r, carry):
        for kk in range(TOP_K):
            _row_copy(y_hbm, buf.at[slot, kk], sem.at[slot], 0, r).wait()
        return carry

    lax.fori_loop(0, tc, wait, 0, unroll=_DMA_UNROLL)
    gate = gate_ref[...]
    o_ref[...] = r_ref[...] + gate[:, 0:1] * buf[slot, 0] + gate[:, 1:2] * buf[slot, 1]


def _combine(y_sorted, pos_flat, gates, resid, *, tc):
    m, d = resid.shape
    return pl.pallas_call(
        functools.partial(_combine_body, tc=tc),
        grid_spec=pltpu.PrefetchScalarGridSpec(
            num_scalar_prefetch=1, grid=(m // tc,),
            in_specs=[pl.BlockSpec(memory_space=pl.ANY),
                      pl.BlockSpec((tc, TOP_K), lambda i, pos: (i, 0)),
                      pl.BlockSpec((tc, d), lambda i, pos: (i, 0))],
            out_specs=pl.BlockSpec((tc, d), lambda i, pos: (i, 0)),
            scratch_shapes=[pltpu.VMEM((2, TOP_K, tc, d), f32), pltpu.SemaphoreType.DMA((2,))]),
        out_shape=jax.ShapeDtypeStruct((m, d), f32),
        compiler_params=_params("arbitrary"), name="moe_combine")(pos_flat, y_sorted, gates, resid)


def _routing_tables(top_i, n_exp, tm, n_tiles):
    m = top_i.shape[0]
    e_flat = top_i.reshape(-1)
    onehot = (e_flat[:, None] == jnp.arange(n_exp, dtype=jnp.int32)[None, :]).astype(jnp.int32)
    rank = jnp.take_along_axis(jnp.cumsum(onehot, axis=0), e_flat[:, None], axis=1)[:, 0] - 1
    counts = jnp.sum(onehot, axis=0)
    tiles_per = (counts + tm - 1) // tm
    tile_end = jnp.cumsum(tiles_per)
    tile_start = tile_end - tiles_per
    pos = tile_start[e_flat] * tm + rank
    token = jnp.arange(m * TOP_K, dtype=jnp.int32) // TOP_K
    row_token = jnp.zeros((n_tiles * tm,), jnp.int32).at[pos].set(token)
    n_valid = tile_end[-1]
    tid = jnp.minimum(jnp.arange(n_tiles, dtype=jnp.int32), n_valid - 1)
    tile_expert = jnp.sum((tid[:, None] >= tile_end[None, :]).astype(jnp.int32), axis=1)
    tile_first = jnp.concatenate([jnp.ones((1,), jnp.int32),
                                  (tile_expert[1:] != tile_expert[:-1]).astype(jnp.int32)])
    ids = jnp.arange(n_tiles, dtype=jnp.int32)
    starts = jnp.where((tile_first == 1) & (ids < n_valid), ids, n_tiles)
    later = jnp.concatenate([lax.cummin(starts, reverse=True)[1:], jnp.full((1,), n_tiles, jnp.int32)])
    tile_next = jnp.where(later >= n_tiles, -1, later).astype(jnp.int32)
    return (pos.astype(jnp.int32), row_token,
            (tile_expert.astype(jnp.int32), tile_first, tile_next, n_valid.reshape(1).astype(jnp.int32)))


def _moe_ffn(h, g_ffn, w_router, b_router, wg, wu, wd, *, tm_tok, tm, tn_up, tn_down):
    m = h.shape[0]
    n_exp = wg.shape[0]
    c_f32, top_i, top_g = _router(h, g_ffn, w_router, b_router, tm=tm_tok)
    n_tiles = (m * TOP_K) // tm + n_exp
    pos, row_token, tables = _routing_tables(top_i, n_exp, tm, n_tiles)
    x_sorted = _gather_rows(c_f32, row_token, tg=tm, out_dtype=bf16)
    h_sorted = _gmm(x_sorted, (wg, wu), tables, tm=tm, tn=tn_up, swiglu=True, out_dtype=bf16, name="moe_gate_up")
    y_sorted = _gmm(h_sorted, (wd,), tables, tm=tm, tn=tn_down, swiglu=False, out_dtype=f32, name="moe_down")
    return _combine(y_sorted, pos, top_g, h, tc=tm_tok // 2)


def kernel(x_prompt, x_sample, state_mlstm_C, state_mlstm_n, state_mlstm_m, state_s5_re, state_s5_im, state_ssd, cache_conv, p_prompt, p_sample, g_mix, w_in, b_igate, b_fgate, g_ml, s5_lam_re, s5_lam_im, s5_log_dt, s5_b_re, s5_b_im, s5_c_re, s5_c_im, s5_d, s5_w_glu, s5_b_glu, g_s5, ssd_conv_w, ssd_conv_b, ssd_dt_bias, ssd_a_log, ssd_d, g_ssd, w_out, g_ffn, ffn_w_gate, ffn_w_up, ffn_w_down, w_router, b_router, moe_w_gate, moe_w_up, moe_w_down, g_ple, w_ple, w_ple_gate, g_final):
    bp, seq, d = x_prompt.shape
    bs = x_sample.shape[0]
    tp = bp * seq
    m = tp + bs
    tm = _token_tile(tp, bs)
    heads = ML_HEADS
    t5 = math.gcd(seq, S5_CHUNK)
    nc5 = seq // t5
    cl = math.gcd(seq, CHUNK)
    nc = seq // cl

    p_p = p_prompt.reshape(DEPTH, tp, -1)
    p_s = p_sample.reshape(DEPTH, bs, -1)
    w_in_t = jnp.swapaxes(w_in, 1, 2)
    outs_p = [[] for _ in range(7)]
    outs_s = [[] for _ in range(5)]
    c_stack = s_stack = None

    for i in range(DEPTH):
        if i == 0:
            h, a = _rmsnorm_in(x_prompt.reshape(tp, d), x_sample.reshape(bs, d), g_mix[i], tm)
        else:
            a = _rmsnorm(h, g_mix[i], bf16, tm)
        qkvo = _mm_nt(a, w_in_t, tm=tm, tn=1024, layer=i, row_off=0, n_rows=_OFF_GATES, name="in_proj_qkvo")
        uzx = _mm_nt(a, w_in_t, tm=tm, tn=1024, layer=i, row_off=_OFF_U, n_rows=_OFF_DT - _OFF_U, name="in_proj_uzx")
        gd = _mm_gates(a, w_in_t, tm=tm, layer=i, off1=_OFF_GATES, n1=2 * heads, off2=_OFF_DT, n2=SSD_HEADS,
                       name="in_proj_gates")

        gates_t = jnp.transpose(gd[:tp, :2 * heads].reshape(bp, nc, cl, 2 * heads), (0, 1, 3, 2))
        bias8 = jnp.concatenate([b_igate[i], b_fgate[i]])
        h_ml_p, c_p, n_p, m_p = _mlstm_prompt(qkvo, gates_t, bias8.reshape(2 * heads, 1), g_ml[i], batch=bp, seq=seq)
        h_ml_s, c_stack, n_s, m_s = _mlstm_sample(qkvo, gd, bias8.reshape(1, 2 * heads), g_ml[i], state_mlstm_C,
                                                  state_mlstm_n, state_mlstm_m, layer=i, row0=tp, c_stack=c_stack)

        mats_p = _s5_matrices(s5_lam_re[i], s5_lam_im[i], s5_log_dt[i], s5_b_re[i], s5_b_im[i],
                              s5_c_re[i], s5_c_im[i], t5)
        mats_s = _s5_matrices(s5_lam_re[i], s5_lam_im[i], s5_log_dt[i], s5_b_re[i], s5_b_im[i],
                              s5_c_re[i], s5_c_im[i], 1)
        y_p, x5_p = _s5_prompt(uzx, mats_p, batch=bp, seq=seq)
        s5re_p, s5im_p = x5_p[..., :S5_STATE], x5_p[..., S5_STATE:]
        u_s = jnp.transpose(uzx[tp:, :S5_WIDTH].reshape(bs, S5_GROUPS, S5_CH), (1, 0, 2)).astype(bf16)
        x0_s = jnp.swapaxes(jnp.concatenate([state_s5_re[i], state_s5_im[i]], axis=-1), 0, 1)
        y_s, x5_s = _s5_scan(u_s, mats_s, x0_s, nc=1, batch=bs)
        s5re_s, s5im_s = x5_s[..., :S5_STATE], x5_s[..., S5_STATE:]
        y_s = jnp.transpose(y_s, (1, 0, 2)).reshape(bs, S5_WIDTH)
        y5 = _s5_glu(y_p, y_s, uzx, s5_d[i].reshape(1, S5_WIDTH), s5_w_glu,
                     s5_b_glu[i].reshape(1, S5_WIDTH), g_s5[i].reshape(1, S5_WIDTH), layer=i, tm=tm)

        dt_t = jnp.transpose(gd[:tp, 2 * heads:2 * heads + SSD_HEADS].reshape(bp, nc, cl, SSD_HEADS), (0, 1, 3, 2))
        y_ssd_p, ssd_p = _ssd_prompt(uzx, gd, dt_t, ssd_conv_w[i], ssd_conv_b[i], ssd_dt_bias[i], ssd_a_log[i],
                                     ssd_d[i], g_ssd[i], batch=bp, seq=seq)
        y_ssd_s, s_stack = _ssd_sample(uzx, cache_conv, gd, ssd_conv_w[i], ssd_conv_b[i], ssd_dt_bias[i],
                                       ssd_a_log[i], ssd_d[i], g_ssd[i], state_ssd, layer=i, row0=tp,
                                       s_stack=s_stack)
        xbc = uzx[:, S5_WIDTH + SSD_WIDTH:]
        conv_p = xbc[:tp].reshape(bp, seq, -1)[:, seq - (SSD_CONV - 1):]
        conv_s = jnp.concatenate([cache_conv[i][:, 1:], xbc[tp:].reshape(bs, 1, -1)], axis=1)

        for lst, s in zip(outs_p, (c_p, n_p.reshape(bp, heads, ML_DK), m_p.reshape(bp, heads),
                                   jnp.swapaxes(s5re_p, 0, 1), jnp.swapaxes(s5im_p, 0, 1), ssd_p, conv_p)):
            lst.append(s)
        for lst, s in zip(outs_s, (n_s, m_s, jnp.swapaxes(s5re_s, 0, 1), jnp.swapaxes(s5im_s, 0, 1), conv_s)):
            lst.append(s)

        h = _mm_mix(h_ml_p.reshape(tp, -1), h_ml_s, y5, y_ssd_p.reshape(tp, -1), y_ssd_s, w_out, h, tm=tm, tn=1024,
                    layer=i, name="out_proj")

        j = i // 2
        if i % 2 == 0:
            cn = _rmsnorm(h, g_ffn[i], bf16, tm)
            hid = _mm_swiglu(cn, ffn_w_gate, ffn_w_up, tm=tm, tn=512, layer=j, name="ffn_gate_up")
            h = _mm(hid, ffn_w_down, tm=tm // 2, tn=512, layer=j, resid=h, name="ffn_down")
        else:
            n_moe = moe_w_gate.shape[0]
            sel = lambda w: w.reshape((n_moe * N_EXPERTS,) + w.shape[2:])[j * N_EXPERTS:(j + 1) * N_EXPERTS] if n_moe > 1 else w.reshape(w.shape[1:])
            h = _moe_ffn(h, g_ffn[i], w_router[j], b_router[j], sel(moe_w_gate), sel(moe_w_up), sel(moe_w_down),
                         tm_tok=tm, tm=256, tn_up=1024, tn_down=512)

        e = _rmsnorm(h, g_ple[i], bf16, tm)
        h = _mm_ple(e, p_p, p_s, w_ple_gate, w_ple, h, tm=tm, tn=1024, layer=i)

    y_p, y_s = _rmsnorm_out(h, g_final, tp, tm)
    n_s, m_s, s5re_s, s5im_s, conv_s = (jnp.stack(l) for l in outs_s)
    return ((y_p.reshape(bp, seq, d), y_s.reshape(bs, 1, d)) + tuple(jnp.stack(l) for l in outs_p)
            + (c_stack, n_s, m_s, s5re_s, s5im_s, s_stack, conv_s))
```

```python
import functools
import math

import jax
import jax.numpy as jnp
from jax import lax
from jax.experimental import pallas as pl
from jax.experimental.pallas import tpu as pltpu

f32 = jnp.float32
bf16 = jnp.bfloat16
HI = lax.Precision.HIGHEST

D_MODEL = 2048
DEPTH = 2
ML_HEADS = 4
ML_DK = 256
ML_DV = 256
ML_WIDTH = ML_HEADS * ML_DV
S5_CH = 16
S5_WIDTH = 512
S5_GROUPS = 32
S5_STATE = 64
SSD_HEAD_DIM = 64
SSD_WIDTH = 512
SSD_HEADS = 8
SSD_GROUPS = 2
SSD_STATE = 128
SSD_CONV = 4
SSD_CONV_CH = 1024
CHUNK = 64
S5_CHUNK = 16
N_EXPERTS = 8
TOP_K = 2
RMS_EPS = 1e-6

_OFF_GATES = 4 * ML_WIDTH
_OFF_U = _OFF_GATES + 2 * ML_HEADS
_OFF_DT = _OFF_U + S5_WIDTH + SSD_WIDTH + SSD_CONV_CH

_VMEM_LIMIT = 56 * 1024 * 1024
_NT = (((1,), (1,)), ((), ()))
_TN = (((0,), (0,)), ((), ()))


def _params(*sem):
    return pltpu.CompilerParams(dimension_semantics=sem, vmem_limit_bytes=_VMEM_LIMIT)


def _rms(x, g):
    return x * lax.rsqrt(jnp.mean(x * x, axis=-1, keepdims=True) + RMS_EPS) * g


def _rmsnorm_body(x_ref, g_ref, o_ref):
    o_ref[...] = _rms(x_ref[...], g_ref[...]).astype(o_ref.dtype)


def _last_tile(p_tile, s_rows, n_keep):
    return jnp.concatenate([p_tile[0:n_keep, :], s_rows], axis=0)


def _split_rows(tp, bs, tm):
    n_tiles = (tp + bs) // tm
    n_keep = tp - (n_tiles - 1) * tm
    assert n_tiles * tm == tp + bs and 0 < n_keep and n_keep + bs == tm and n_keep % 16 == 0
    return n_tiles, n_keep


def _token_tile(tp, bs):
    m = tp + bs
    for n_tiles in (10, 8, 5, 4, 2, 1):
        tm = m // n_tiles
        if m % n_tiles == 0 and tm % 32 == 0 and bs < tm and (tp - (n_tiles - 1) * tm) % 16 == 0:
            return tm
    raise ValueError("no row tiling for these token counts")


def _rmsnorm_in_body(xp_ref, xs_ref, g_ref, h_ref, a_ref, *, n_keep):
    last = pl.num_programs(0) - 1

    def run(x):
        h_ref[...] = x
        a_ref[...] = _rms(x, g_ref[...]).astype(a_ref.dtype)

    @pl.when(pl.program_id(0) < last)
    def _():
        run(xp_ref[...])

    @pl.when(pl.program_id(0) == last)
    def _():
        run(_last_tile(xp_ref, xs_ref[...], n_keep))


def _rmsnorm_in(xp, xs, g, tm):
    tp, d = xp.shape
    bs = xs.shape[0]
    n_tiles, n_keep = _split_rows(tp, bs, tm)
    row = lambda i: (i, 0)
    fix = lambda i: (0, 0)
    return pl.pallas_call(
        functools.partial(_rmsnorm_in_body, n_keep=n_keep), grid=(n_tiles,),
        in_specs=[pl.BlockSpec((tm, d), row), pl.BlockSpec((bs, d), fix), pl.BlockSpec((1, d), fix)],
        out_specs=[pl.BlockSpec((tm, d), row), pl.BlockSpec((tm, d), row)],
        out_shape=[jax.ShapeDtypeStruct((tp + bs, d), f32), jax.ShapeDtypeStruct((tp + bs, d), bf16)],
        compiler_params=_params("parallel"), name="rmsnorm_in")(xp, xs, g.reshape(1, d))


def _rmsnorm_out_body(x_ref, g_ref, yp_ref, ys_ref, *, n_keep):
    y = _rms(x_ref[...], g_ref[...])
    yp_ref[...] = y

    @pl.when(pl.program_id(0) == pl.num_programs(0) - 1)
    def _():
        ys_ref[...] = y[n_keep:, :]


def _rmsnorm_out(x, g, tp, tm):
    m, d = x.shape
    bs = m - tp
    n_tiles, n_keep = _split_rows(tp, bs, tm)
    return pl.pallas_call(
        functools.partial(_rmsnorm_out_body, n_keep=n_keep), grid=(n_tiles,),
        in_specs=[pl.BlockSpec((tm, d), lambda i: (i, 0)), pl.BlockSpec((1, d), lambda i: (0, 0))],
        out_specs=[pl.BlockSpec((tm, d), lambda i: (i, 0)), pl.BlockSpec((bs, d), lambda i: (0, 0))],
        out_shape=[jax.ShapeDtypeStruct((tp, d), f32), jax.ShapeDtypeStruct((bs, d), f32)],
        compiler_params=_params("arbitrary"), name="rmsnorm_out")(x, g.reshape(1, d))


def _rmsnorm(x, g, out_dtype, tm):
    m, d = x.shape
    return pl.pallas_call(
        _rmsnorm_body, grid=(m // tm,),
        in_specs=[pl.BlockSpec((tm, d), lambda i: (i, 0)), pl.BlockSpec((1, d), lambda i: (0, 0))],
        out_specs=pl.BlockSpec((tm, d), lambda i: (i, 0)),
        out_shape=jax.ShapeDtypeStruct((m, d), out_dtype),
        compiler_params=_params("parallel"), name="rmsnorm")(x, g.reshape(1, d))


def _cast_weight_once(w_ref, wb_ref):
    @pl.when(pl.program_id(1) == 0)
    def _():
        wb_ref[...] = w_ref[...].astype(bf16)


def _mm_plain_body(a_ref, w_ref, o_ref, wb_ref):
    _cast_weight_once(w_ref, wb_ref)
    o_ref[...] = jnp.dot(a_ref[...], wb_ref[...], preferred_element_type=f32).astype(o_ref.dtype)


def _mm_resid_body(a_ref, w_ref, r_ref, o_ref, wb_ref):
    _cast_weight_once(w_ref, wb_ref)
    o_ref[...] = r_ref[...] + jnp.dot(a_ref[...], wb_ref[...], preferred_element_type=f32)


def _mm_swiglu_body(a_ref, wg_ref, wu_ref, o_ref, wgb_ref, wub_ref):
    _cast_weight_once(wg_ref, wgb_ref)
    _cast_weight_once(wu_ref, wub_ref)
    a = a_ref[...]
    g = jnp.dot(a, wgb_ref[...], preferred_element_type=f32)
    u = jnp.dot(a, wub_ref[...], preferred_element_type=f32)
    o_ref[...] = (g * jax.nn.sigmoid(g) * u).astype(o_ref.dtype)


def _mm_ple_body(e_ref, pp_ref, ps_ref, wg_ref, wp_ref, r_ref, o_ref, wgb_ref, wpb_ref, *, n_keep):
    _cast_weight_once(wg_ref, wgb_ref)
    _cast_weight_once(wp_ref, wpb_ref)
    last = pl.num_programs(1) - 1

    def run(p):
        gate = jnp.dot(e_ref[...], wgb_ref[...], preferred_element_type=f32)
        emb = jnp.dot(p.astype(bf16), wpb_ref[...], preferred_element_type=f32)
        o_ref[...] = r_ref[...] + emb * jax.nn.sigmoid(gate)

    @pl.when(pl.program_id(1) < last)
    def _():
        run(pp_ref[...])

    @pl.when(pl.program_id(1) == last)
    def _():
        run(_last_tile(pp_ref, ps_ref[...], n_keep))


def _mm_mix_body(a1p_ref, a1s_ref, a2_ref, a3p_ref, a3s_ref, w_ref, r_ref, o_ref, wb_ref, *, n_keep):
    _cast_weight_once(w_ref, wb_ref)
    last = pl.num_programs(1) - 1
    k1 = a1p_ref.shape[1]
    k2 = a2_ref.shape[1]

    def run(a1, a3):
        acc = jnp.dot(a1, wb_ref[0:k1, :], preferred_element_type=f32)
        acc += jnp.dot(a2_ref[...], wb_ref[k1:k1 + k2, :], preferred_element_type=f32)
        acc += jnp.dot(a3, wb_ref[k1 + k2:, :], preferred_element_type=f32)
        o_ref[...] = r_ref[...] + acc

    @pl.when(pl.program_id(1) < last)
    def _():
        run(a1p_ref[...], a3p_ref[...])

    @pl.when(pl.program_id(1) == last)
    def _():
        run(_last_tile(a1p_ref, a1s_ref[...].astype(bf16), n_keep),
            _last_tile(a3p_ref, a3s_ref[...].astype(bf16), n_keep))


def _mm_mix(a1p, a1s, a2, a3p, a3s, w, resid, *, tm, tn, layer, name):
    m, n = resid.shape
    tp, k1 = a1p.shape
    bs = a1s.shape[0]
    k2, k3 = a2.shape[1], a3p.shape[1]
    n_tiles, n_keep = _split_rows(tp, bs, tm)
    rows = lambda j, i: (i, 0)
    fix = lambda j, i: (0, 0)
    o_spec = pl.BlockSpec((tm, tn), lambda j, i: (i, j))
    return pl.pallas_call(
        functools.partial(_mm_mix_body, n_keep=n_keep), grid=(n // tn, n_tiles),
        in_specs=[pl.BlockSpec((tm, k1), rows), pl.BlockSpec((bs, k1), fix), pl.BlockSpec((tm, k2), rows),
                  pl.BlockSpec((tm, k3), rows), pl.BlockSpec((bs, k3), fix),
                  _wspec(w, layer, k1 + k2 + k3, tn), o_spec],
        out_specs=o_spec, out_shape=jax.ShapeDtypeStruct((m, n), f32),
        scratch_shapes=[pltpu.VMEM((k1 + k2 + k3, tn), bf16)],
        compiler_params=_params("arbitrary", "arbitrary"), name=name)(a1p, a1s, a2, a3p, a3s, w, resid)


def _wspec(w, layer, k, tn, col_block_off=0):
    if w.ndim == 2:
        return pl.BlockSpec((k, tn), lambda j, i: (0, j + col_block_off))
    return pl.BlockSpec((None, k, tn), lambda j, i: (layer, 0, j + col_block_off))


def _mm(a, w, *, tm, tn, layer=0, n_cols=None, col_off=0, resid=None, out_dtype=f32, name="mm"):
    m, k = a.shape
    n = n_cols if n_cols is not None else w.shape[-1]
    grid = (n // tn, m // tm)
    a_spec = pl.BlockSpec((tm, k), lambda j, i: (i, 0))
    o_spec = pl.BlockSpec((tm, tn), lambda j, i: (i, j))
    w_spec = _wspec(w, layer, k, tn, col_off // tn)
    scratch = [pltpu.VMEM((k, tn), bf16)]
    if resid is None:
        body, ins, specs = _mm_plain_body, (a, w), [a_spec, w_spec]
    else:
        body, ins, specs = _mm_resid_body, (a, w, resid), [a_spec, w_spec, o_spec]
    return pl.pallas_call(
        body, grid=grid, in_specs=specs, out_specs=o_spec,
        out_shape=jax.ShapeDtypeStruct((m, n), out_dtype), scratch_shapes=scratch,
        compiler_params=_params("arbitrary", "arbitrary"), name=name)(*ins)


def _mm_nt_body(a_ref, wt_ref, o_ref, wb_ref):
    @pl.when(pl.program_id(1) == 0)
    def _():
        wb_ref[...] = wt_ref[0].T.astype(bf16)

    o_ref[...] = jnp.dot(a_ref[...], wb_ref[...], preferred_element_type=f32)


def _wt_rows(layer, k, n_rows, row_of):
    return pl.BlockSpec((pl.Element(1), pl.Element(n_rows), pl.Element(k)),
                        lambda *idx: (layer, row_of(*idx), 0))


def _mm_nt(a, wt, *, tm, tn, layer, row_off, n_rows, name):
    m, k = a.shape
    return pl.pallas_call(
        _mm_nt_body, grid=(n_rows // tn, m // tm),
        in_specs=[pl.BlockSpec((tm, k), lambda j, i: (i, 0)),
                  _wt_rows(layer, k, tn, lambda j, i: pl.multiple_of(row_off + tn * j, 8))],
        out_specs=pl.BlockSpec((tm, tn), lambda j, i: (i, j)),
        out_shape=jax.ShapeDtypeStruct((m, n_rows), f32), scratch_shapes=[pltpu.VMEM((k, tn), bf16)],
        compiler_params=_params("arbitrary", "arbitrary"), name=name)(a, wt)


def _mm_gates_body(a_ref, w1_ref, w2_ref, o_ref):
    a = a_ref[...]
    n1 = w1_ref.shape[1]
    o_ref[:, 0:n1] = lax.dot_general(a, w1_ref[0].astype(bf16), _NT, preferred_element_type=f32)
    o_ref[:, n1:] = lax.dot_general(a, w2_ref[0].astype(bf16), _NT, preferred_element_type=f32)


def _mm_gates(a, wt, *, tm, layer, off1, n1, off2, n2, name):
    m, k = a.shape
    return pl.pallas_call(
        _mm_gates_body, grid=(m // tm,),
        in_specs=[pl.BlockSpec((tm, k), lambda i: (i, 0)),
                  _wt_rows(layer, k, n1, lambda i: off1), _wt_rows(layer, k, n2, lambda i: off2)],
        out_specs=pl.BlockSpec((tm, n1 + n2), lambda i: (i, 0)),
        out_shape=jax.ShapeDtypeStruct((m, n1 + n2), f32),
        compiler_params=_params("parallel"), name=name)(a, wt, wt)


def _mm_swiglu(a, wg, wu, *, tm, tn, layer=0, name="mm_swiglu"):
    m, k = a.shape
    n = wg.shape[-1]
    a_spec = pl.BlockSpec((tm, k), lambda j, i: (i, 0))
    o_spec = pl.BlockSpec((tm, tn), lambda j, i: (i, j))
    return pl.pallas_call(
        _mm_swiglu_body, grid=(n // tn, m // tm),
        in_specs=[a_spec, _wspec(wg, layer, k, tn), _wspec(wu, layer, k, tn)], out_specs=o_spec,
        out_shape=jax.ShapeDtypeStruct((m, n), bf16),
        scratch_shapes=[pltpu.VMEM((k, tn), bf16), pltpu.VMEM((k, tn), bf16)],
        compiler_params=_params("arbitrary", "arbitrary"), name=name)(a, wg, wu)


def _mm_ple(e, p_prompt, p_sample, w_gate, w_ple, resid, *, tm, tn, layer, name="mm_ple"):
    m, k = e.shape
    _, tp, kp = p_prompt.shape
    bs = p_sample.shape[1]
    n = w_gate.shape[-1]
    n_tiles, n_keep = _split_rows(tp, bs, tm)
    o_spec = pl.BlockSpec((tm, tn), lambda j, i: (i, j))
    return pl.pallas_call(
        functools.partial(_mm_ple_body, n_keep=n_keep), grid=(n // tn, n_tiles),
        in_specs=[pl.BlockSpec((tm, k), lambda j, i: (i, 0)),
                  pl.BlockSpec((None, tm, kp), lambda j, i: (layer, i, 0)),
                  pl.BlockSpec((None, bs, kp), lambda j, i: (layer, 0, 0)),
                  _wspec(w_gate, layer, k, tn), _wspec(w_ple, layer, kp, tn), o_spec],
        out_specs=o_spec, out_shape=jax.ShapeDtypeStruct((m, n), f32),
        scratch_shapes=[pltpu.VMEM((k, tn), bf16), pltpu.VMEM((kp, tn), bf16)],
        compiler_params=_params("arbitrary", "arbitrary"), name=name)(e, p_prompt, p_sample, w_gate, w_ple, resid)


def _col_from_row(row, eye):
    n = row.shape[1]
    return jnp.sum(jnp.where(eye, jnp.broadcast_to(row, (n, n)), 0.0), axis=1, keepdims=True)


def _mlstm_prompt_body(*refs, heads, dk, dv, cl, nb):
    q_refs, k_refs, v_refs, o_refs = (refs[i * nb:(i + 1) * nb] for i in range(4))
    gt_ref, bias_ref, gml_ref, h_ref, c_out, n_out, m_out, c_s, n_s, m_s = refs[4 * nb:]
    c = pl.program_id(1)

    @pl.when(c == 0)
    def _():
        c_s[...] = jnp.zeros_like(c_s)
        n_s[...] = jnp.zeros_like(n_s)
        m_s[...] = jnp.zeros_like(m_s)

    row = lax.broadcasted_iota(jnp.int32, (cl, cl), 0)
    col = lax.broadcasted_iota(jnp.int32, (cl, cl), 1)
    causal = col <= row
    eye = col == row
    triu = (row <= col).astype(f32)
    for r in range(nb):
        gt = gt_ref[r, 0] + bias_ref[...]
        lf = jax.nn.log_sigmoid(gt[heads:2 * heads])
        b_all = jnp.dot(lf, triu, precision=HI, preferred_element_type=f32)
        for hh in range(heads):
            st = r * heads + hh
            ig = gt[hh:hh + 1]
            b_row = b_all[hh:hh + 1]
            b_col = _col_from_row(b_row, eye)
            m_prev = m_s[st]
            d = jnp.where(causal, b_col - b_row + ig, -jnp.inf)
            inter = b_col + m_prev
            m_t = jnp.maximum(inter, jnp.max(d, axis=1, keepdims=True))
            w = jnp.exp(d - m_t)
            g = jnp.exp(inter - m_t)
            q = q_refs[r][:, hh * dk:(hh + 1) * dk]
            k = k_refs[r][:, hh * dk:(hh + 1) * dk] * (dk ** -0.5)
            vb = v_refs[r][:, hh * dv:(hh + 1) * dv].astype(bf16)
            qb = q.astype(bf16)
            cmat = c_s[st]
            n_row = n_s[st]
            s = lax.dot_general(qb, k.astype(bf16), _NT, preferred_element_type=f32) * w
            num = (jnp.dot(s.astype(bf16), vb, preferred_element_type=f32)
                   + g * jnp.dot(qb, cmat.astype(bf16), preferred_element_type=f32))
            den = jnp.sum(s, axis=1, keepdims=True) + g * jnp.sum(q * n_row, axis=1, keepdims=True)
            hraw = num / jnp.maximum(jnp.abs(den), jnp.exp(-m_t))
            hn = _rms(hraw, gml_ref[hh:hh + 1, :])
            ogate = jax.nn.sigmoid(o_refs[r][:, hh * dv:(hh + 1) * dv])
            h_ref[r, :, hh * dv:(hh + 1) * dv] = (ogate * hn).astype(h_ref.dtype)
            b_last = b_row[:, cl - 1:cl]
            dl = b_last - b_row + ig
            m_new = jnp.maximum(b_last + m_prev, jnp.max(dl, axis=1, keepdims=True))
            ws_col = _col_from_row(jnp.exp(dl - m_new), eye)
            gl = jnp.exp(b_last + m_prev - m_new)
            kw = k * ws_col
            c_s[st] = gl * cmat + lax.dot_general(kw.astype(bf16), vb, _TN, preferred_element_type=f32)
            n_s[st] = gl * n_row + jnp.sum(kw, axis=0, keepdims=True)
            m_s[st] = m_new

    @pl.when(c == pl.num_programs(1) - 1)
    def _():
        for r in range(nb):
            c_out[r] = c_s[r * heads:(r + 1) * heads]
            n_out[r] = n_s[r * heads:(r + 1) * heads]
            m_out[r] = m_s[r * heads:(r + 1) * heads]


_PROMPT_SEQS_PER_STEP = 4


def _mlstm_prompt(qkvo, gates_t, bias_col, g_ml, *, batch, seq):
    heads, dk, dv, cl = ML_HEADS, ML_DK, ML_DV, math.gcd(seq, CHUNK)
    nc = seq // cl
    wq = heads * dk
    nb = math.gcd(batch, _PROMPT_SEQS_PER_STEP)
    body = functools.partial(_mlstm_prompt_body, heads=heads, dk=dk, dv=dv, cl=cl, nb=nb)
    qkvo_specs = [pl.BlockSpec((cl, wq), lambda b, c, r=r, part=part: ((b * nb + r) * nc + c, part))
                  for part in range(4) for r in range(nb)]
    return pl.pallas_call(
        body, grid=(batch // nb, nc),
        in_specs=qkvo_specs + [pl.BlockSpec((nb, 1, 2 * heads, cl), lambda b, c: (b, c, 0, 0)),
                               pl.BlockSpec((2 * heads, 1), lambda b, c: (0, 0)),
                               pl.BlockSpec((heads, dv), lambda b, c: (0, 0))],
        out_specs=[pl.BlockSpec((nb, cl, heads * dv), lambda b, c: (b, c, 0)),
                   pl.BlockSpec((nb, heads, dk, dv), lambda b, c: (b, 0, 0, 0)),
                   pl.BlockSpec((nb, heads, 1, dk), lambda b, c: (b, 0, 0, 0)),
                   pl.BlockSpec((nb, heads, 1, 1), lambda b, c: (b, 0, 0, 0))],
        out_shape=[jax.ShapeDtypeStruct((batch, seq, heads * dv), bf16),
                   jax.ShapeDtypeStruct((batch, heads, dk, dv), f32),
                   jax.ShapeDtypeStruct((batch, heads, 1, dk), f32),
                   jax.ShapeDtypeStruct((batch, heads, 1, 1), f32)],
        scratch_shapes=[pltpu.VMEM((nb * heads, dk, dv), f32), pltpu.VMEM((nb * heads, 1, dk), f32),
                        pltpu.VMEM((nb * heads, 1, 1), f32)],
        compiler_params=_params("arbitrary", "arbitrary"), name="mlstm_prompt",
    )(*([qkvo] * (4 * nb)), gates_t, bias_col, g_ml)


def _lane_pick(pieces, width):
    lane = lax.broadcasted_iota(jnp.int32, (1, width), 1)
    out = jnp.zeros((1, width), f32)
    for i, p in enumerate(pieces):
        out = jnp.where(lane == i, p, out)
    return out


_SAMPLE_BLOCK = 8


def _mlstm_sample_body(*refs, heads, dk, dv, aliased):
    if aliased:
        refs = refs[:10] + refs[11:]
    (q_ref, k_ref, v_ref, o_ref, gd_ref, bias_ref, gml_ref, c0_ref, n0_ref, m0_ref,
     h_ref, c_out, n_out, m_out) = refs
    bt = q_ref.shape[0]

    @pl.when(pl.program_id(0) > 0)
    def _():
        c_out[...] = jnp.zeros_like(c_out)

    @pl.when(pl.program_id(0) == 0)
    def _():
        g8 = gd_ref[:, 0:2 * heads] + bias_ref[...]
        lane = lax.broadcasted_iota(jnp.int32, (bt, heads), 1)
        m_new = jnp.zeros((bt, heads), f32)
        for hh in range(heads):
            ig = g8[:, hh:hh + 1]
            lf = jax.nn.log_sigmoid(g8[:, heads + hh:heads + hh + 1])
            m0 = m0_ref[:, hh:hh + 1]
            m_t = jnp.maximum(lf + m0, ig)
            w = jnp.exp(ig - m_t)
            g = jnp.exp(lf + m0 - m_t)
            q = q_ref[:, hh * dk:(hh + 1) * dk]
            k = k_ref[:, hh * dk:(hh + 1) * dk] * (dk ** -0.5)
            v = v_ref[:, hh * dv:(hh + 1) * dv]
            o = o_ref[:, hh * dv:(hh + 1) * dv]
            n0 = n0_ref[:, hh, :]
            kw = k * w
            q_t = q.T
            kw_t = kw.T
            qc_rows = []
            for b in range(bt):
                cmat = c0_ref[b, hh]
                qc_rows.append(jnp.sum(q_t[:, b:b + 1] * cmat, axis=0, keepdims=True))
                c_out[b, hh] = g[b:b + 1, :] * cmat + kw_t[:, b:b + 1] * v[b:b + 1, :]
            qc = jnp.concatenate(qc_rows, axis=0)
            s = jnp.sum(q * k, axis=1, keepdims=True) * w
            num = s * v + g * qc
            den = s + g * jnp.sum(q * n0, axis=1, keepdims=True)
            hraw = num / jnp.maximum(jnp.abs(den), jnp.exp(-m_t))
            hn = _rms(hraw, gml_ref[hh:hh + 1, :])
            h_ref[:, hh * dv:(hh + 1) * dv] = jax.nn.sigmoid(o) * hn
            n_out[:, hh, :] = g * n0 + kw
            m_new = jnp.where(lane == hh, m_t, m_new)
        m_out[...] = m_new


def _stacked_state_grid(layer, b, stack):
    n_l = DEPTH if stack is None else 1
    row = lambda l, i: jnp.where(l == 0, i, b - 1)
    out_layer = lambda l: (layer + l) % DEPTH
    return (n_l, b), row, out_layer


def _mlstm_sample(qkvo, gd, bias_row, g_ml, c0, n0, m0, *, layer, row0, c_stack=None):
    heads, dk, dv = ML_HEADS, ML_DK, ML_DV
    b = c0.shape[1]
    bt = _SAMPLE_BLOCK
    wq = heads * dk
    blk0 = row0 // bt
    grid, row, out_layer = _stacked_state_grid(layer, b // bt, c_stack)
    aliased = c_stack is not None
    body = functools.partial(_mlstm_sample_body, heads=heads, dk=dk, dv=dv, aliased=aliased)
    in_specs = [pl.BlockSpec((bt, wq), lambda l, i, c=c: (blk0 + row(l, i), c)) for c in range(4)]
    in_specs += [pl.BlockSpec((bt, gd.shape[-1]), lambda l, i: (blk0 + row(l, i), 0)),
                 pl.BlockSpec((1, 2 * heads), lambda l, i: (0, 0)),
                 pl.BlockSpec((heads, dv), lambda l, i: (0, 0)),
                 pl.BlockSpec((None, bt, heads, dk, dv), lambda l, i: (layer, row(l, i), 0, 0, 0)),
                 pl.BlockSpec((None, bt, heads, dk), lambda l, i: (layer, row(l, i), 0, 0)),
                 pl.BlockSpec((None, bt, heads), lambda l, i: (layer, row(l, i), 0))]
    args = [qkvo, qkvo, qkvo, qkvo, gd, bias_row, g_ml, c0, n0, m0]
    if aliased:
        in_specs.append(pl.BlockSpec(memory_space=pl.ANY))
        args.append(c_stack)
    return pl.pallas_call(
        body, grid=grid, in_specs=in_specs,
        out_specs=[pl.BlockSpec((bt, heads * dv), lambda l, i: (row(l, i), 0)),
                   pl.BlockSpec((None, bt, heads, dk, dv), lambda l, i: (out_layer(l), i, 0, 0, 0)),
                   pl.BlockSpec((bt, heads, dk), lambda l, i: (row(l, i), 0, 0)),
                   pl.BlockSpec((bt, heads), lambda l, i: (row(l, i), 0))],
        out_shape=[jax.ShapeDtypeStruct((b, heads * dv), f32),
                   jax.ShapeDtypeStruct((DEPTH, b, heads, dk, dv), f32),
                   jax.ShapeDtypeStruct((b, heads, dk), f32),
                   jax.ShapeDtypeStruct((b, heads), f32)],
        input_output_aliases={10: 1} if aliased else {},
        compiler_params=_params("arbitrary", "arbitrary"), name="mlstm_sample",
    )(*args)


def _s5_advance(x, l1, l2):
    return l1 * x + l2 * pltpu.roll(x, x.shape[-1] // 2, axis=1)


def _toeplitz_operator(krow):
    ch, tc = krow.shape
    lane = lax.broadcasted_iota(jnp.int32, krow.shape, 1)
    blocks = [krow] + [jnp.where(lane >= ch * s, pltpu.roll(krow, ch * s, axis=1), 0.0)
                       for s in range(1, tc // ch)]
    return jnp.concatenate(blocks, axis=0).astype(bf16)


def _s5_body(u_ref, m_ref, w_ref, v_ref, l1_ref, l2_ref, x0_ref, y_ref, x_out, xs_s, *, nc, batch):
    u = u_ref[0]
    xin = jnp.dot(u, w_ref[0], preferred_element_type=f32)
    l1 = l1_ref[0]
    l2 = l2_ref[0]
    x = x0_ref[0]
    for k in range(nc):
        sl = slice(k * batch, (k + 1) * batch)
        xs_s[sl, :] = x
        x = _s5_advance(x, l1, l2) + xin[sl, :]
    x_out[0] = x
    y_ref[0] = (jnp.dot(u, _toeplitz_operator(m_ref[0]), preferred_element_type=f32)
                + jnp.dot(xs_s[...].astype(bf16), v_ref[0], preferred_element_type=f32))


def _s5_scan(u_g, mats, x0, *, nc, batch):
    m_mat, w_pk, v_pk, l1, l2 = mats
    g, rows, tc = u_g.shape
    p2 = w_pk.shape[-1]
    blk = lambda *s: pl.BlockSpec((1,) + s, lambda i: (i,) + (0,) * len(s))
    body = functools.partial(_s5_body, nc=nc, batch=batch)
    return pl.pallas_call(
        body, grid=(g,),
        in_specs=[blk(rows, tc), blk(S5_CH, tc), blk(tc, p2), blk(p2, tc), blk(1, p2), blk(1, p2), blk(batch, p2)],
        out_specs=[blk(rows, tc), blk(batch, p2)],
        out_shape=[jax.ShapeDtypeStruct((g, rows, tc), f32), jax.ShapeDtypeStruct((g, batch, p2), f32)],
        scratch_shapes=[pltpu.VMEM((rows, p2), f32)],
        compiler_params=_params("parallel"), name=f"s5_scan_t{tc // S5_CH}",
    )(u_g, m_mat, w_pk, v_pk, l1, l2, x0)


def _s5_prompt_body(x_ref, m_ref, w_ref, v_ref, l1_ref, l2_ref, y_ref, x_out, u_s, y_s, xin_s, xs_s,
                    *, gw, batch, seq, t, ch):
    nc = seq // t
    for b in range(batch):
        for tt in range(t):
            blk = x_ref[pl.ds(b * seq + tt, nc, stride=t), :]
            for gl in range(gw):
                u_s[gl, b * nc:(b + 1) * nc, tt * ch:(tt + 1) * ch] = blk[:, gl * ch:(gl + 1) * ch]
    for gl in range(gw):
        xin_s[gl] = jnp.dot(u_s[gl].astype(bf16), w_ref[gl], preferred_element_type=f32)
    xs = [jnp.zeros((batch, xin_s.shape[-1]), f32)] * gw
    for k in range(nc):
        for gl in range(gw):
            xs_s[gl, pl.ds(k, batch, stride=nc), :] = xs[gl]
            xs[gl] = _s5_advance(xs[gl], l1_ref[gl], l2_ref[gl]) + xin_s[gl, pl.ds(k, batch, stride=nc), :]
    for gl in range(gw):
        x_out[gl] = xs[gl]
        y_s[gl] = (jnp.dot(u_s[gl].astype(bf16), _toeplitz_operator(m_ref[gl]), preferred_element_type=f32)
                   + jnp.dot(xs_s[gl].astype(bf16), v_ref[gl], preferred_element_type=f32))
    for b in range(batch):
        for tt in range(t):
            y_ref[pl.ds(b * seq + tt, nc, stride=t), :] = jnp.concatenate(
                [y_s[gl, b * nc:(b + 1) * nc, tt * ch:(tt + 1) * ch] for gl in range(gw)], axis=1)


def _s5_prompt(uzx, mats, *, batch, seq):
    m_mat, w_pk, v_pk, l1, l2 = mats
    g, tc, p2 = w_pk.shape
    ch = S5_CH
    t = tc // ch
    gw = 128 // ch
    rows = batch * (seq // t)
    tp = batch * seq
    win = lambda *s: pl.BlockSpec((gw,) + s, lambda i: (i,) + (0,) * len(s))
    body = functools.partial(_s5_prompt_body, gw=gw, batch=batch, seq=seq, t=t, ch=ch)
    return pl.pallas_call(
        body, grid=(g // gw,),
        in_specs=[pl.BlockSpec((tp, gw * ch), lambda i: (0, i)), win(ch, tc), win(tc, p2), win(p2, tc),
                  win(1, p2), win(1, p2)],
        out_specs=[pl.BlockSpec((tp, gw * ch), lambda i: (0, i)), win(batch, p2)],
        out_shape=[jax.ShapeDtypeStruct((tp, g * ch), f32), jax.ShapeDtypeStruct((g, batch, p2), f32)],
        scratch_shapes=[pltpu.VMEM((gw, rows, tc), f32), pltpu.VMEM((gw, rows, tc), f32),
                        pltpu.VMEM((gw, rows, p2), f32), pltpu.VMEM((gw, rows, p2), f32)],
        compiler_params=_params("parallel"), name="s5_prompt",
    )(uzx, m_mat, w_pk, v_pk, l1, l2)


def _s5_matrices(lam_re, lam_im, log_dt, b_re, b_im, c_re, c_im, t):
    g, p = lam_re.shape
    ch = b_re.shape[-1]
    dt = jnp.exp(log_dt)[:, None]
    ar, ai = lam_re * dt, lam_im * dt

    def powers(tau):
        mag = jnp.exp(ar[:, None, :] * tau[None, :, None])
        ang = ai[:, None, :] * tau[None, :, None]
        return mag * jnp.cos(ang), mag * jnp.sin(ang)

    lbr, lbi = jnp.exp(ar) * jnp.cos(ai), jnp.exp(ar) * jnp.sin(ai)
    den = lam_re * lam_re + lam_im * lam_im
    fr = ((lbr - 1.0) * lam_re + lbi * lam_im) / den
    fi = (lbi * lam_re - (lbr - 1.0) * lam_im) / den
    bbr = jnp.swapaxes(fr[..., None] * b_re - fi[..., None] * b_im, 1, 2)
    bbi = jnp.swapaxes(fr[..., None] * b_im + fi[..., None] * b_re, 1, 2)
    cbr = c_re[:, :, None, :] * bbr[:, None, :, :] - c_im[:, :, None, :] * bbi[:, None, :, :]
    cbi = c_re[:, :, None, :] * bbi[:, None, :, :] + c_im[:, :, None, :] * bbr[:, None, :, :]
    steps = jnp.arange(t, dtype=f32)
    lr, li = powers(steps)
    krow = jnp.einsum("gcdp,gtp->gdtc", jnp.concatenate([cbr, -cbi], axis=-1),
                      jnp.concatenate([lr, li], axis=-1), precision=HI).reshape(g, ch, t * ch)
    pr, pi = powers(t - 1.0 - steps)
    w_re = (pr[:, :, None, :] * bbr[:, None, :, :] - pi[:, :, None, :] * bbi[:, None, :, :]).reshape(g, t * ch, p)
    w_im = (pr[:, :, None, :] * bbi[:, None, :, :] + pi[:, :, None, :] * bbr[:, None, :, :]).reshape(g, t * ch, p)
    qr, qi = powers(steps + 1.0)
    qr, qi = jnp.swapaxes(qr, 1, 2)[..., None], jnp.swapaxes(qi, 1, 2)[..., None]
    ctr, cti = jnp.swapaxes(c_re, 1, 2)[:, :, None, :], jnp.swapaxes(c_im, 1, 2)[:, :, None, :]
    v_re = (ctr * qr - cti * qi).reshape(g, p, t * ch)
    v_im = -(ctr * qi + cti * qr).reshape(g, p, t * ch)
    ltr, lti = powers(jnp.full((1,), float(t), f32))
    return (krow, jnp.concatenate([w_re, w_im], axis=-1).astype(bf16),
            jnp.concatenate([v_re, v_im], axis=1).astype(bf16),
            jnp.concatenate([ltr, ltr], axis=-1), jnp.concatenate([-lti, lti], axis=-1))


def _s5_glu_body(yp_ref, ys_ref, u_ref, d_ref, w_ref, b_ref, g_ref, o_ref, *, n_keep):
    last = pl.num_programs(0) - 1

    def run(y_raw):
        y5 = jax.nn.gelu(y_raw + d_ref[...] * u_ref[...])
        gate = jax.nn.sigmoid(jnp.dot(y5.astype(bf16), w_ref[...].astype(bf16), preferred_element_type=f32)
                              + b_ref[...])
        o_ref[...] = _rms(y5 * gate, g_ref[...]).astype(o_ref.dtype)

    @pl.when(pl.program_id(0) < last)
    def _():
        run(yp_ref[...])

    @pl.when(pl.program_id(0) == last)
    def _():
        run(_last_tile(yp_ref, ys_ref[...], n_keep))


def _s5_glu(y_prompt, y_sample, uzx, d_skip, w_glu, b_glu, g_s5, *, layer, tm):
    tp, wdt = y_prompt.shape
    bs = y_sample.shape[0]
    n_tiles, n_keep = _split_rows(tp, bs, tm)
    row = lambda i: (i, 0)
    fix = lambda i: (0, 0)
    return pl.pallas_call(
        functools.partial(_s5_glu_body, n_keep=n_keep), grid=(n_tiles,),
        in_specs=[pl.BlockSpec((tm, wdt), row), pl.BlockSpec((bs, wdt), fix), pl.BlockSpec((tm, wdt), row),
                  pl.BlockSpec((1, wdt), fix), pl.BlockSpec((None, wdt, wdt), lambda i: (layer, 0, 0)),
                  pl.BlockSpec((1, wdt), fix), pl.BlockSpec((1, wdt), fix)],
        out_specs=pl.BlockSpec((tm, wdt), row), out_shape=jax.ShapeDtypeStruct((tp + bs, wdt), bf16),
        compiler_params=_params("parallel"), name="s5_glu",
    )(y_prompt, y_sample, uzx, d_skip, w_glu, b_glu, g_s5)


def _ssd_prompt_body(*refs, heads, hd, ns, groups, cl, width, nb):
    xbc_refs, z_refs, gd_refs = (refs[i * nb:(i + 1) * nb] for i in range(3))
    (dtt_ref, cw_ref, cb_ref, dtb_row, dtb_col, alog_row, alog_col, dskip_ref, gssd_ref,
     y_ref, s_out, s_s, xp_s, ys_s) = refs[3 * nb:]
    c = pl.program_id(1)

    @pl.when(c == 0)
    def _():
        s_s[...] = jnp.zeros_like(s_s)
        for r in range(nb):
            xp_s[r, 0:8, :] = jnp.zeros((8, xp_s.shape[2]), f32)

    row = lax.broadcasted_iota(jnp.int32, (cl, cl), 0)
    col = lax.broadcasted_iota(jnp.int32, (cl, cl), 1)
    causal = col <= row
    tril = causal.astype(f32)
    triu = (row <= col).astype(f32)
    rep = heads // groups
    for r in range(nb):
        xp_s[r, 8:8 + cl, :] = xbc_refs[r][...]
        xc = cb_ref[...] + sum(cw_ref[j:j + 1, :] * xp_s[r, 5 + j:5 + j + cl, :] for j in range(SSD_CONV))
        xp_s[r, 0:8, :] = xp_s[r, cl:cl + 8, :]
        xc = xc * jax.nn.sigmoid(xc)
        dt_col = jax.nn.softplus(gd_refs[r][:, 8:8 + heads] + dtb_row[...])
        dt_row = jax.nn.softplus(dtt_ref[r, 0] + dtb_col[...])
        cum_col = jnp.dot(tril, dt_col * -jnp.exp(alog_row[...]), precision=HI, preferred_element_type=f32)
        cum_row = jnp.dot(dt_row * -jnp.exp(alog_col[...]), triu, precision=HI, preferred_element_type=f32)
        for gi in range(groups):
            bm = xc[:, width + gi * ns:width + (gi + 1) * ns].astype(bf16)
            cm = xc[:, width + (groups + gi) * ns:width + (groups + gi + 1) * ns].astype(bf16)
            scores = lax.dot_general(cm, bm, _NT, preferred_element_type=f32)
            for hh in range(gi * rep, (gi + 1) * rep):
                st = r * heads + hh
                cc = cum_col[:, hh:hh + 1]
                cr = cum_row[hh:hh + 1, :]
                seg = jnp.exp(jnp.where(causal, cc - cr, -jnp.inf))
                xh = xc[:, hh * hd:(hh + 1) * hd]
                xdt = xh * dt_col[:, hh:hh + 1]
                smat = s_s[st]
                y = (jnp.dot((scores * seg).astype(bf16), xdt.astype(bf16), preferred_element_type=f32)
                     + jnp.exp(cc) * lax.dot_general(cm, smat.astype(bf16), _NT, preferred_element_type=f32))
                c_last = cr[:, cl - 1:cl]
                xw = (xdt * jnp.exp(c_last - cc)).astype(bf16)
                s_s[st] = jnp.exp(c_last) * smat + lax.dot_general(xw, bm, _TN, preferred_element_type=f32)
                ys_s[r, :, hh * hd:(hh + 1) * hd] = y + dskip_ref[:, hh:hh + 1] * xh
        z = z_refs[r][...]
        y_ref[r] = _rms(ys_s[r] * (z * jax.nn.sigmoid(z)), gssd_ref[...]).astype(y_ref.dtype)

    @pl.when(c == pl.num_programs(1) - 1)
    def _():
        for r in range(nb):
            s_out[r] = s_s[r * heads:(r + 1) * heads]


def _ssd_prompt(uzx, gd, dt_t, conv_w, conv_b, dt_bias, a_log, d_skip, g_ssd, *, batch, seq):
    heads, hd, ns, groups, width = SSD_HEADS, SSD_HEAD_DIM, SSD_STATE, SSD_GROUPS, SSD_WIDTH
    cl = math.gcd(seq, CHUNK)
    nc = seq // cl
    cch = SSD_CONV_CH
    nb = 1
    fix = lambda b, c: (0, 0)
    body = functools.partial(_ssd_prompt_body, heads=heads, hd=hd, ns=ns, groups=groups, cl=cl, width=width, nb=nb)
    seq_rows = lambda r, part: (lambda b, c: ((b * nb + r) * nc + c, part))
    in_specs = ([pl.BlockSpec((cl, cch), seq_rows(r, 1)) for r in range(nb)]
                + [pl.BlockSpec((cl, width), seq_rows(r, 1)) for r in range(nb)]
                + [pl.BlockSpec((cl, gd.shape[1]), seq_rows(r, 0)) for r in range(nb)]
                + [pl.BlockSpec((nb, 1, heads, cl), lambda b, c: (b, c, 0, 0)),
                   pl.BlockSpec((SSD_CONV, cch), fix), pl.BlockSpec((1, cch), fix),
                   pl.BlockSpec((1, heads), fix), pl.BlockSpec((heads, 1), fix),
                   pl.BlockSpec((1, heads), fix), pl.BlockSpec((heads, 1), fix),
                   pl.BlockSpec((1, heads), fix), pl.BlockSpec((1, width), fix)])
    return pl.pallas_call(
        body, grid=(batch // nb, nc), in_specs=in_specs,
        out_specs=[pl.BlockSpec((nb, cl, width), lambda b, c: (b, c, 0)),
                   pl.BlockSpec((nb, heads, hd, ns), lambda b, c: (b, 0, 0, 0))],
        out_shape=[jax.ShapeDtypeStruct((batch, seq, width), bf16),
                   jax.ShapeDtypeStruct((batch, heads, hd, ns), f32)],
        scratch_shapes=[pltpu.VMEM((nb * heads, hd, ns), f32), pltpu.VMEM((nb, cl + 8, cch), f32),
                        pltpu.VMEM((nb, cl, width), f32)],
        compiler_params=_params("arbitrary", "arbitrary"), name="ssd_prompt",
    )(*([uzx] * (2 * nb)), *([gd] * nb), dt_t, conv_w, conv_b.reshape(1, cch), dt_bias.reshape(1, heads),
      dt_bias.reshape(heads, 1), a_log.reshape(1, heads), a_log.reshape(heads, 1), d_skip.reshape(1, heads),
      g_ssd.reshape(1, width))


def _ssd_sample_body(*refs, heads, hd, ns, groups, width, aliased):
    if aliased:
        refs = refs[:10] + refs[11:]
    (x_ref, conv0_ref, gd_ref, cw_ref, cb_ref, dtb_ref, alog_ref, dskip_ref, gssd_ref, s0_ref,
     y_ref, s_out, ys_s) = refs
    bt = x_ref.shape[0]

    @pl.when(pl.program_id(0) > 0)
    def _():
        s_out[...] = jnp.zeros_like(s_out)

    @pl.when(pl.program_id(0) == 0)
    def _():
        z = x_ref[:, width:2 * width]
        xc = cb_ref[...] + cw_ref[SSD_CONV - 1:SSD_CONV, :] * x_ref[:, 2 * width:]
        for j in range(SSD_CONV - 1):
            xc = xc + cw_ref[j:j + 1, :] * conv0_ref[:, j, :]
        xc = xc * jax.nn.sigmoid(xc)
        dt = jax.nn.softplus(gd_ref[:, 8:8 + heads] + dtb_ref[...])
        ea = jnp.exp(dt * -jnp.exp(alog_ref[...]))
        rep = heads // groups
        for gi in range(groups):
            bm = xc[:, width + gi * ns:width + (gi + 1) * ns]
            cm = xc[:, width + (groups + gi) * ns:width + (groups + gi + 1) * ns]
            cb_dot = jnp.sum(cm * bm, axis=1, keepdims=True)
            cmb = cm.astype(bf16)
            for hh in range(gi * rep, (gi + 1) * rep):
                xh = xc[:, hh * hd:(hh + 1) * hd]
                xdt = xh * dt[:, hh:hh + 1]
                eah = ea[:, hh:hh + 1]
                xdt_t = xdt.T
                sc_rows = []
                for b in range(bt):
                    smat = s0_ref[b, hh]
                    sc_rows.append(lax.dot_general(cmb, smat.astype(bf16), _NT,
                                                   preferred_element_type=f32)[b:b + 1, :])
                    s_out[b, hh] = eah[b:b + 1, :] * smat + xdt_t[:, b:b + 1] * bm[b:b + 1, :]
                sc = jnp.concatenate(sc_rows, axis=0)
                ys_s[:, hh * hd:(hh + 1) * hd] = cb_dot * xdt + eah * sc + dskip_ref[:, hh:hh + 1] * xh
        y_ref[...] = _rms(ys_s[...] * (z * jax.nn.sigmoid(z)), gssd_ref[...])


def _ssd_sample(uzx, conv0, gd, conv_w, conv_b, dt_bias, a_log, d_skip, g_ssd, s0, *, layer, row0, s_stack=None):
    heads, hd, ns, groups, width = SSD_HEADS, SSD_HEAD_DIM, SSD_STATE, SSD_GROUPS, SSD_WIDTH
    cch = SSD_CONV_CH
    b = s0.shape[1]
    bt = _SAMPLE_BLOCK
    blk0 = row0 // bt
    fix = lambda l, i: (0, 0)
    grid, row, out_layer = _stacked_state_grid(layer, b // bt, s_stack)
    aliased = s_stack is not None
    body = functools.partial(_ssd_sample_body, heads=heads, hd=hd, ns=ns, groups=groups, width=width,
                             aliased=aliased)
    in_specs = [pl.BlockSpec((bt, uzx.shape[-1]), lambda l, i: (blk0 + row(l, i), 0)),
                pl.BlockSpec((None, bt, SSD_CONV - 1, cch), lambda l, i: (layer, row(l, i), 0, 0)),
                pl.BlockSpec((bt, gd.shape[-1]), lambda l, i: (blk0 + row(l, i), 0)),
                pl.BlockSpec((SSD_CONV, cch), fix), pl.BlockSpec((1, cch), fix),
                pl.BlockSpec((1, heads), fix), pl.BlockSpec((1, heads), fix), pl.BlockSpec((1, heads), fix),
                pl.BlockSpec((1, width), fix),
                pl.BlockSpec((None, bt, heads, hd, ns), lambda l, i: (layer, row(l, i), 0, 0, 0))]
    args = [uzx, conv0, gd, conv_w, conv_b.reshape(1, cch), dt_bias.reshape(1, heads), a_log.reshape(1, heads),
            d_skip.reshape(1, heads), g_ssd.reshape(1, width), s0]
    if aliased:
        in_specs.append(pl.BlockSpec(memory_space=pl.ANY))
        args.append(s_stack)
    return pl.pallas_call(
        body, grid=grid, in_specs=in_specs,
        out_specs=[pl.BlockSpec((bt, width), lambda l, i: (row(l, i), 0)),
                   pl.BlockSpec((None, bt, heads, hd, ns), lambda l, i: (out_layer(l), i, 0, 0, 0))],
        out_shape=[jax.ShapeDtypeStruct((b, width), f32),
                   jax.ShapeDtypeStruct((DEPTH, b, heads, hd, ns), f32)],
        scratch_shapes=[pltpu.VMEM((bt, width), f32)],
        input_output_aliases={10: 1} if aliased else {},
        compiler_params=_params("arbitrary", "arbitrary"), name="ssd_sample",
    )(*args)


def _router_body(h_ref, g_ref, wr_ref, br_ref, cf_ref, idx_ref, gate_ref, *, n_exp):
    cf = _rms(h_ref[...], g_ref[...])
    cf_ref[...] = cf
    logits = jnp.dot(cf, wr_ref[...], precision=HI, preferred_element_type=f32) + br_ref[...]
    lane = lax.broadcasted_iota(jnp.int32, logits.shape, 1)
    m1 = jnp.max(logits, axis=1, keepdims=True)
    i1 = jnp.min(jnp.where(logits == m1, lane, n_exp), axis=1, keepdims=True)
    rest = jnp.where(lane == i1, -jnp.inf, logits)
    m2 = jnp.max(rest, axis=1, keepdims=True)
    i2 = jnp.min(jnp.where(rest == m2, lane, n_exp), axis=1, keepdims=True)
    e2 = jnp.exp(m2 - m1)
    g1 = 1.0 / (1.0 + e2)
    two = lax.broadcasted_iota(jnp.int32, (logits.shape[0], TOP_K), 1)
    idx_ref[...] = jnp.where(two == 0, i1, i2)
    gate_ref[...] = jnp.where(two == 0, g1, e2 * g1)


def _router(h, g_ffn, w_router, b_router, *, tm):
    m, d = h.shape
    e = w_router.shape[-1]
    row = lambda i: (i, 0)
    fix = lambda i: (0, 0)
    return pl.pallas_call(
        functools.partial(_router_body, n_exp=e), grid=(m // tm,),
        in_specs=[pl.BlockSpec((tm, d), row), pl.BlockSpec((1, d), fix), pl.BlockSpec((d, e), fix),
                  pl.BlockSpec((1, e), fix)],
        out_specs=[pl.BlockSpec((tm, d), row), pl.BlockSpec((tm, TOP_K), row),
                   pl.BlockSpec((tm, TOP_K), row)],
        out_shape=[jax.ShapeDtypeStruct((m, d), f32),
                   jax.ShapeDtypeStruct((m, TOP_K), jnp.int32), jax.ShapeDtypeStruct((m, TOP_K), f32)],
        compiler_params=_params("parallel"), name="router",
    )(h, g_ffn.reshape(1, d), w_router, b_router.reshape(1, e))


def _row_copy(src_hbm, dst, sem, src_row, dst_row):
    return pltpu.make_async_copy(src_hbm.at[pl.ds(src_row, 1)], dst.at[pl.ds(dst_row, 1)], sem)


_DMA_UNROLL = 8


def _gather_body(idx_ref, x_hbm, o_ref, buf, sem, *, tg):
    i = pl.program_id(0)

    def issue(tile, slot):
        base = tile * tg

        def start(r, carry):
            _row_copy(x_hbm, buf.at[slot], sem.at[slot], idx_ref[base + r], r).start()
            return carry

        lax.fori_loop(0, tg, start, 0, unroll=_DMA_UNROLL)

    @pl.when(i == 0)
    def _():
        issue(0, 0)

    @pl.when(i + 1 < pl.num_programs(0))
    def _():
        issue(i + 1, (i + 1) % 2)

    slot = i % 2
    pltpu.make_async_copy(x_hbm.at[pl.ds(0, tg)], buf.at[slot], sem.at[slot]).wait()
    o_ref[...] = buf[slot].astype(o_ref.dtype)


def _gather_rows(x, row_idx, *, tg, out_dtype):
    r = row_idx.shape[0]
    d = x.shape[1]
    return pl.pallas_call(
        functools.partial(_gather_body, tg=tg),
        grid_spec=pltpu.PrefetchScalarGridSpec(
            num_scalar_prefetch=1, grid=(r // tg,),
            in_specs=[pl.BlockSpec(memory_space=pl.ANY)],
            out_specs=pl.BlockSpec((tg, d), lambda i, idx: (i, 0)),
            scratch_shapes=[pltpu.VMEM((2, tg, d), x.dtype), pltpu.SemaphoreType.DMA((2,))]),
        out_shape=jax.ShapeDtypeStruct((r, d), out_dtype),
        compiler_params=_params("arbitrary"), name="gather_rows")(row_idx, x)


def _gmm_body(te_ref, first_ref, next_ref, nv_ref, a_ref, *rest, n_w, swiglu):
    w_hbm = rest[:n_w]
    o_ref, wf_ref, wb_ref, sem, run_ref = rest[n_w:]
    j = pl.program_id(0)
    t = pl.program_id(1)
    tn = o_ref.shape[1]

    def tile_copies(expert, col_tile, slot):
        col = pl.multiple_of(col_tile * tn, 128)
        return [pltpu.make_async_copy(w_hbm[i].at[expert, :, pl.ds(col, tn)], wf_ref.at[slot, i], sem.at[slot, i])
                for i in range(n_w)]

    @pl.when((j == 0) & (t == 0))
    def _():
        run_ref[0] = 0
        for c in tile_copies(te_ref[0], 0, 0):
            c.start()

    @pl.when((t < nv_ref[0]) & (first_ref[t] == 1))
    def _():
        slot = run_ref[0] % 2
        for c in tile_copies(te_ref[t], j, slot):
            c.wait()
        for i in range(n_w):
            wb_ref[i] = wf_ref[slot, i].astype(bf16)
        nxt = next_ref[t]

        @pl.when(nxt >= 0)
        def _():
            for c in tile_copies(te_ref[nxt], j, 1 - slot):
                c.start()

        @pl.when((nxt < 0) & (j + 1 < pl.num_programs(0)))
        def _():
            for c in tile_copies(te_ref[0], j + 1, 1 - slot):
                c.start()

        run_ref[0] = run_ref[0] + 1

    @pl.when(t < nv_ref[0])
    def _():
        a = a_ref[...]
        if swiglu:
            g = jnp.dot(a, wb_ref[0], preferred_element_type=f32)
            u = jnp.dot(a, wb_ref[1], preferred_element_type=f32)
            o_ref[...] = (g * jax.nn.sigmoid(g) * u).astype(o_ref.dtype)
        else:
            o_ref[...] = jnp.dot(a, wb_ref[0], preferred_element_type=f32).astype(o_ref.dtype)

    @pl.when(t >= nv_ref[0])
    def _():
        o_ref[...] = jnp.zeros_like(o_ref)


def _gmm(a_sorted, weights, tables, *, tm, tn, swiglu, out_dtype, name):
    tile_expert, tile_first, tile_next, n_valid = tables
    r, k = a_sorted.shape
    n = weights[0].shape[-1]
    n_w = len(weights)
    return pl.pallas_call(
        functools.partial(_gmm_body, n_w=n_w, swiglu=swiglu),
        grid_spec=pltpu.PrefetchScalarGridSpec(
            num_scalar_prefetch=4, grid=(n // tn, r // tm),
            in_specs=[pl.BlockSpec((tm, k), lambda j, t, *_: (t, 0))] + [pl.BlockSpec(memory_space=pl.ANY)] * n_w,
            out_specs=pl.BlockSpec((tm, tn), lambda j, t, *_: (t, j)),
            scratch_shapes=[pltpu.VMEM((2, n_w, k, tn), f32), pltpu.VMEM((n_w, k, tn), bf16),
                            pltpu.SemaphoreType.DMA((2, n_w)), pltpu.SMEM((1,), jnp.int32)]),
        out_shape=jax.ShapeDtypeStruct((r, n), out_dtype),
        compiler_params=_params("arbitrary", "arbitrary"), name=name,
    )(tile_expert, tile_first, tile_next, n_valid, a_sorted, *weights)


def _combine_body(pos_ref, y_hbm, gate_ref, r_ref, gn_ref, o_ref, e_ref, buf, sem, *, tc):
    i = pl.program_id(0)

    def issue(tile, slot):
        base = tile * tc

        def start(r, carry):
            for kk in range(TOP_K):
                _row_copy(y_hbm, buf.at[slot, kk], sem.at[slot], pos_ref[(base + r) * TOP_K + kk], r).start()
            return carry

        lax.fori_loop(0, tc, start, 0, unroll=_DMA_UNROLL)

    @pl.when(i == 0)
    def _():
        issue(0, 0)

    @pl.when(i + 1 < pl.num_programs(0))
    def _():
        issue(i + 1, (i + 1) % 2)

    slot = i % 2
    for kk in range(TOP_K):
        pltpu.make_async_copy(y_hbm.at[pl.ds(0, tc)], buf.at[slot, kk], sem.at[slot]).wait()
    gate = gate_ref[...]
    out = r_ref[...] + gate[:, 0:1] * buf[slot, 0] + gate[:, 1:2] * buf[slot, 1]
    o_ref[...] = out
    e_ref[...] = _rms(out, gn_ref[...]).astype(e_ref.dtype)


def _combine(y_sorted, pos_flat, gates, resid, g_next, *, tc):
    m, d = resid.shape
    rows = lambda i, pos: (i, 0)
    return pl.pallas_call(
        functools.partial(_combine_body, tc=tc),
        grid_spec=pltpu.PrefetchScalarGridSpec(
            num_scalar_prefetch=1, grid=(m // tc,),
            in_specs=[pl.BlockSpec(memory_space=pl.ANY), pl.BlockSpec((tc, TOP_K), rows),
                      pl.BlockSpec((tc, d), rows), pl.BlockSpec((1, d), lambda i, pos: (0, 0))],
            out_specs=[pl.BlockSpec((tc, d), rows), pl.BlockSpec((tc, d), rows)],
            scratch_shapes=[pltpu.VMEM((2, TOP_K, tc, d), f32), pltpu.SemaphoreType.DMA((2,))]),
        out_shape=[jax.ShapeDtypeStruct((m, d), f32), jax.ShapeDtypeStruct((m, d), bf16)],
        compiler_params=_params("arbitrary"), name="moe_combine",
    )(pos_flat, y_sorted, gates, resid, g_next.reshape(1, d))


def _routing_tables(top_i, n_exp, tm, n_tiles):
    m = top_i.shape[0]
    e_flat = top_i.reshape(-1)
    onehot = (e_flat[:, None] == jnp.arange(n_exp, dtype=jnp.int32)[None, :]).astype(jnp.int32)
    rank = jnp.take_along_axis(jnp.cumsum(onehot, axis=0), e_flat[:, None], axis=1)[:, 0] - 1
    counts = jnp.sum(onehot, axis=0)
    tiles_per = (counts + tm - 1) // tm
    tile_end = jnp.cumsum(tiles_per)
    tile_start = tile_end - tiles_per
    pos = tile_start[e_flat] * tm + rank
    token = jnp.arange(m * TOP_K, dtype=jnp.int32) // TOP_K
    row_token = jnp.zeros((n_tiles * tm,), jnp.int32).at[pos].set(token)
    n_valid = tile_end[-1]
    tid = jnp.minimum(jnp.arange(n_tiles, dtype=jnp.int32), n_valid - 1)
    tile_expert = jnp.sum((tid[:, None] >= tile_end[None, :]).astype(jnp.int32), axis=1)
    tile_first = jnp.concatenate([jnp.ones((1,), jnp.int32),
                                  (tile_expert[1:] != tile_expert[:-1]).astype(jnp.int32)])
    ids = jnp.arange(n_tiles, dtype=jnp.int32)
    starts = jnp.where((tile_first == 1) & (ids < n_valid), ids, n_tiles)
    later = jnp.concatenate([lax.cummin(starts, reverse=True)[1:], jnp.full((1,), n_tiles, jnp.int32)])
    tile_next = jnp.where(later >= n_tiles, -1, later).astype(jnp.int32)
    return (pos.astype(jnp.int32), row_token,
            (tile_expert.astype(jnp.int32), tile_first, tile_next, n_valid.reshape(1).astype(jnp.int32)))


def _moe_ffn(h, g_ffn, w_router, b_router, wg, wu, wd, g_next, *, tm_tok, tm, tn_up, tn_down):
    m = h.shape[0]
    n_exp = wg.shape[0]
    c_f32, top_i, top_g = _router(h, g_ffn, w_router, b_router, tm=tm_tok)
    n_tiles = (m * TOP_K) // tm + n_exp
    pos, row_token, tables = _routing_tables(top_i, n_exp, tm, n_tiles)
    x_sorted = _gather_rows(c_f32, row_token, tg=tm, out_dtype=bf16)
    h_sorted = _gmm(x_sorted, (wg, wu), tables, tm=tm, tn=tn_up, swiglu=True, out_dtype=bf16, name="moe_gate_up")
    y_sorted = _gmm(h_sorted, (wd,), tables, tm=tm, tn=tn_down, swiglu=False, out_dtype=f32, name="moe_down")
    return _combine(y_sorted, pos, top_g, h, g_next, tc=tm_tok // 2)


def kernel(x_prompt, x_sample, state_mlstm_C, state_mlstm_n, state_mlstm_m, state_s5_re, state_s5_im, state_ssd, cache_conv, p_prompt, p_sample, g_mix, w_in, b_igate, b_fgate, g_ml, s5_lam_re, s5_lam_im, s5_log_dt, s5_b_re, s5_b_im, s5_c_re, s5_c_im, s5_d, s5_w_glu, s5_b_glu, g_s5, ssd_conv_w, ssd_conv_b, ssd_dt_bias, ssd_a_log, ssd_d, g_ssd, w_out, g_ffn, ffn_w_gate, ffn_w_up, ffn_w_down, w_router, b_router, moe_w_gate, moe_w_up, moe_w_down, g_ple, w_ple, w_ple_gate, g_final):
    bp, seq, d = x_prompt.shape
    bs = x_sample.shape[0]
    tp = bp * seq
    m = tp + bs
    tm = _token_tile(tp, bs)
    heads = ML_HEADS
    t5 = math.gcd(seq, S5_CHUNK)
    nc5 = seq // t5
    cl = math.gcd(seq, CHUNK)
    nc = seq // cl

    p_p = p_prompt.reshape(DEPTH, tp, -1)
    p_s = p_sample.reshape(DEPTH, bs, -1)
    w_in_t = jnp.swapaxes(w_in, 1, 2)
    outs_p = [[] for _ in range(7)]
    outs_s = [[] for _ in range(5)]
    c_stack = s_stack = None

    for i in range(DEPTH):
        if i == 0:
            h, a = _rmsnorm_in(x_prompt.reshape(tp, d), x_sample.reshape(bs, d), g_mix[i], tm)
        else:
            a = _rmsnorm(h, g_mix[i], bf16, tm)
        qkvo = _mm_nt(a, w_in_t, tm=tm, tn=1024, layer=i, row_off=0, n_rows=_OFF_GATES, name="in_proj_qkvo")
        uzx = _mm_nt(a, w_in_t, tm=tm, tn=1024, layer=i, row_off=_OFF_U, n_rows=_OFF_DT - _OFF_U, name="in_proj_uzx")
        gd = _mm_gates(a, w_in_t, tm=tm, layer=i, off1=_OFF_GATES, n1=2 * heads, off2=_OFF_DT, n2=SSD_HEADS,
                       name="in_proj_gates")

        gates_t = jnp.transpose(gd[:tp, :2 * heads].reshape(bp, nc, cl, 2 * heads), (0, 1, 3, 2))
        bias8 = jnp.concatenate([b_igate[i], b_fgate[i]])
        h_ml_p, c_p, n_p, m_p = _mlstm_prompt(qkvo, gates_t, bias8.reshape(2 * heads, 1), g_ml[i], batch=bp, seq=seq)
        h_ml_s, c_stack, n_s, m_s = _mlstm_sample(qkvo, gd, bias8.reshape(1, 2 * heads), g_ml[i], state_mlstm_C,
                                                  state_mlstm_n, state_mlstm_m, layer=i, row0=tp, c_stack=c_stack)

        mats_p = _s5_matrices(s5_lam_re[i], s5_lam_im[i], s5_log_dt[i], s5_b_re[i], s5_b_im[i],
                              s5_c_re[i], s5_c_im[i], t5)
        mats_s = _s5_matrices(s5_lam_re[i], s5_lam_im[i], s5_log_dt[i], s5_b_re[i], s5_b_im[i],
                              s5_c_re[i], s5_c_im[i], 1)
        y_p, x5_p = _s5_prompt(uzx, mats_p, batch=bp, seq=seq)
        s5re_p, s5im_p = x5_p[..., :S5_STATE], x5_p[..., S5_STATE:]
        u_s = jnp.transpose(uzx[tp:, :S5_WIDTH].reshape(bs, S5_GROUPS, S5_CH), (1, 0, 2)).astype(bf16)
        x0_s = jnp.swapaxes(jnp.concatenate([state_s5_re[i], state_s5_im[i]], axis=-1), 0, 1)
        y_s, x5_s = _s5_scan(u_s, mats_s, x0_s, nc=1, batch=bs)
        s5re_s, s5im_s = x5_s[..., :S5_STATE], x5_s[..., S5_STATE:]
        y_s = jnp.transpose(y_s, (1, 0, 2)).reshape(bs, S5_WIDTH)
        y5 = _s5_glu(y_p, y_s, uzx, s5_d[i].reshape(1, S5_WIDTH), s5_w_glu,
                     s5_b_glu[i].reshape(1, S5_WIDTH), g_s5[i].reshape(1, S5_WIDTH), layer=i, tm=tm)

        dt_t = jnp.transpose(gd[:tp, 2 * heads:2 * heads + SSD_HEADS].reshape(bp, nc, cl, SSD_HEADS), (0, 1, 3, 2))
        y_ssd_p, ssd_p = _ssd_prompt(uzx, gd, dt_t, ssd_conv_w[i], ssd_conv_b[i], ssd_dt_bias[i], ssd_a_log[i],
                                     ssd_d[i], g_ssd[i], batch=bp, seq=seq)
        y_ssd_s, s_stack = _ssd_sample(uzx, cache_conv, gd, ssd_conv_w[i], ssd_conv_b[i], ssd_dt_bias[i],
                                       ssd_a_log[i], ssd_d[i], g_ssd[i], state_ssd, layer=i, row0=tp,
                                       s_stack=s_stack)
        xbc = uzx[:, S5_WIDTH + SSD_WIDTH:]
        conv_p = xbc[:tp].reshape(bp, seq, -1)[:, seq - (SSD_CONV - 1):]
        conv_s = jnp.concatenate([cache_conv[i][:, 1:], xbc[tp:].reshape(bs, 1, -1)], axis=1)

        for lst, s in zip(outs_p, (c_p, n_p.reshape(bp, heads, ML_DK), m_p.reshape(bp, heads),
                                   jnp.swapaxes(s5re_p, 0, 1), jnp.swapaxes(s5im_p, 0, 1), ssd_p, conv_p)):
            lst.append(s)
        for lst, s in zip(outs_s, (n_s, m_s, jnp.swapaxes(s5re_s, 0, 1), jnp.swapaxes(s5im_s, 0, 1), conv_s)):
            lst.append(s)

        h = _mm_mix(h_ml_p.reshape(tp, -1), h_ml_s, y5, y_ssd_p.reshape(tp, -1), y_ssd_s, w_out, h, tm=tm, tn=1024,
                    layer=i, name="out_proj")

        j = i // 2
        if i % 2 == 0:
            cn = _rmsnorm(h, g_ffn[i], bf16, tm)
            hid = _mm_swiglu(cn, ffn_w_gate, ffn_w_up, tm=tm, tn=512, layer=j, name="ffn_gate_up")
            h = _mm(hid, ffn_w_down, tm=tm // 2, tn=512, layer=j, resid=h, name="ffn_down")
            e = _rmsnorm(h, g_ple[i], bf16, tm)
        else:
            n_moe = moe_w_gate.shape[0]
            sel = lambda w: w.reshape((n_moe * N_EXPERTS,) + w.shape[2:])[j * N_EXPERTS:(j + 1) * N_EXPERTS] if n_moe > 1 else w.reshape(w.shape[1:])
            h, e = _moe_ffn(h, g_ffn[i], w_router[j], b_router[j], sel(moe_w_gate), sel(moe_w_up), sel(moe_w_down),
                            g_ple[i], tm_tok=tm, tm=256, tn_up=1024, tn_down=512)

        h = _mm_ple(e, p_p, p_s, w_ple_gate, w_ple, h, tm=tm, tn=1024, layer=i)

    y_p, y_s = _rmsnorm_out(h, g_final, tp, tm)
    n_s, m_s, s5re_s, s5im_s, conv_s = (jnp.stack(l) for l in outs_s)
    return ((y_p.reshape(bp, seq, d), y_s.reshape(bs, 1, d)) + tuple(jnp.stack(l) for l in outs_p)
            + (c_stack, n_s, m_s, s5re_s, s5im_s, s_stack, conv_s))
```

```python
import functools
import math

import jax
import jax.numpy as jnp
from jax import lax
from jax.experimental import pallas as pl
from jax.experimental.pallas import tpu as pltpu

f32 = jnp.float32
bf16 = jnp.bfloat16
HI = lax.Precision.HIGHEST

D_MODEL = 2048
DEPTH = 2
ML_HEADS = 4
ML_DK = 256
ML_DV = 256
ML_WIDTH = ML_HEADS * ML_DV
S5_CH = 16
S5_WIDTH = 512
S5_GROUPS = 32
S5_STATE = 64
SSD_HEAD_DIM = 64
SSD_WIDTH = 512
SSD_HEADS = 8
SSD_GROUPS = 2
SSD_STATE = 128
SSD_CONV = 4
SSD_CONV_CH = 1024
CHUNK = 64
S5_CHUNK = 16
N_EXPERTS = 8
TOP_K = 2
RMS_EPS = 1e-6

_OFF_GATES = 4 * ML_WIDTH
_OFF_U = _OFF_GATES + 2 * ML_HEADS
_OFF_DT = _OFF_U + S5_WIDTH + SSD_WIDTH + SSD_CONV_CH

_VMEM_LIMIT = 56 * 1024 * 1024
_NT = (((1,), (1,)), ((), ()))
_TN = (((0,), (0,)), ((), ()))


def _params(*sem):
    return pltpu.CompilerParams(dimension_semantics=sem, vmem_limit_bytes=_VMEM_LIMIT)


def _rms(x, g):
    return x * lax.rsqrt(jnp.mean(x * x, axis=-1, keepdims=True) + RMS_EPS) * g


def _rmsnorm_body(x_ref, g_ref, o_ref):
    o_ref[...] = _rms(x_ref[...], g_ref[...]).astype(o_ref.dtype)


def _last_tile(p_tile, s_rows, n_keep):
    return jnp.concatenate([p_tile[0:n_keep, :], s_rows], axis=0)


def _split_rows(tp, bs, tm):
    n_tiles = (tp + bs) // tm
    n_keep = tp - (n_tiles - 1) * tm
    assert n_tiles * tm == tp + bs and 0 < n_keep and n_keep + bs == tm and n_keep % 16 == 0
    return n_tiles, n_keep


def _token_tile(tp, bs):
    m = tp + bs
    for n_tiles in (10, 8, 5, 4, 2, 1):
        tm = m // n_tiles
        if m % n_tiles == 0 and tm % 32 == 0 and bs < tm and (tp - (n_tiles - 1) * tm) % 16 == 0:
            return tm
    raise ValueError("no row tiling for these token counts")


def _rmsnorm_in_body(xp_ref, xs_ref, g_ref, h_ref, a_ref, *, n_keep):
    last = pl.num_programs(0) - 1

    def run(x):
        h_ref[...] = x
        a_ref[...] = _rms(x, g_ref[...]).astype(a_ref.dtype)

    @pl.when(pl.program_id(0) < last)
    def _():
        run(xp_ref[...])

    @pl.when(pl.program_id(0) == last)
    def _():
        run(_last_tile(xp_ref, xs_ref[...], n_keep))


def _rmsnorm_in(xp, xs, g, tm):
    tp, d = xp.shape
    bs = xs.shape[0]
    n_tiles, n_keep = _split_rows(tp, bs, tm)
    row = lambda i: (i, 0)
    fix = lambda i: (0, 0)
    return pl.pallas_call(
        functools.partial(_rmsnorm_in_body, n_keep=n_keep), grid=(n_tiles,),
        in_specs=[pl.BlockSpec((tm, d), row), pl.BlockSpec((bs, d), fix), pl.BlockSpec((1, d), fix)],
        out_specs=[pl.BlockSpec((tm, d), row), pl.BlockSpec((tm, d), row)],
        out_shape=[jax.ShapeDtypeStruct((tp + bs, d), f32), jax.ShapeDtypeStruct((tp + bs, d), bf16)],
        compiler_params=_params("parallel"), name="rmsnorm_in")(xp, xs, g.reshape(1, d))


def _rmsnorm_out_body(x_ref, g_ref, yp_ref, ys_ref, *, n_keep):
    y = _rms(x_ref[...], g_ref[...])
    yp_ref[...] = y

    @pl.when(pl.program_id(0) == pl.num_programs(0) - 1)
    def _():
        ys_ref[...] = y[n_keep:, :]


def _rmsnorm_out(x, g, tp, tm):
    m, d = x.shape
    bs = m - tp
    n_tiles, n_keep = _split_rows(tp, bs, tm)
    return pl.pallas_call(
        functools.partial(_rmsnorm_out_body, n_keep=n_keep), grid=(n_tiles,),
        in_specs=[pl.BlockSpec((tm, d), lambda i: (i, 0)), pl.BlockSpec((1, d), lambda i: (0, 0))],
        out_specs=[pl.BlockSpec((tm, d), lambda i: (i, 0)), pl.BlockSpec((bs, d), lambda i: (0, 0))],
        out_shape=[jax.ShapeDtypeStruct((tp, d), f32), jax.ShapeDtypeStruct((bs, d), f32)],
        compiler_params=_params("arbitrary"), name="rmsnorm_out")(x, g.reshape(1, d))


def _rmsnorm(x, g, out_dtype, tm):
    m, d = x.shape
    return pl.pallas_call(
        _rmsnorm_body, grid=(m // tm,),
        in_specs=[pl.BlockSpec((tm, d), lambda i: (i, 0)), pl.BlockSpec((1, d), lambda i: (0, 0))],
        out_specs=pl.BlockSpec((tm, d), lambda i: (i, 0)),
        out_shape=jax.ShapeDtypeStruct((m, d), out_dtype),
        compiler_params=_params("parallel"), name="rmsnorm")(x, g.reshape(1, d))


def _cast_weight_once(w_ref, wb_ref):
    @pl.when(pl.program_id(1) == 0)
    def _():
        wb_ref[...] = w_ref[...].astype(bf16)


def _mm_plain_body(a_ref, w_ref, o_ref, wb_ref):
    _cast_weight_once(w_ref, wb_ref)
    o_ref[...] = jnp.dot(a_ref[...], wb_ref[...], preferred_element_type=f32).astype(o_ref.dtype)


def _mm_resid_body(a_ref, w_ref, r_ref, o_ref, wb_ref):
    _cast_weight_once(w_ref, wb_ref)
    o_ref[...] = r_ref[...] + jnp.dot(a_ref[...], wb_ref[...], preferred_element_type=f32)


def _mm_swiglu_body(a_ref, wg_ref, wu_ref, o_ref, wgb_ref, wub_ref):
    _cast_weight_once(wg_ref, wgb_ref)
    _cast_weight_once(wu_ref, wub_ref)
    a = a_ref[...]
    g = jnp.dot(a, wgb_ref[...], preferred_element_type=f32)
    u = jnp.dot(a, wub_ref[...], preferred_element_type=f32)
    o_ref[...] = (g * jax.nn.sigmoid(g) * u).astype(o_ref.dtype)


def _mm_ple_body(e_ref, pp_ref, ps_ref, wg_ref, wp_ref, r_ref, o_ref, wgb_ref, wpb_ref, *, n_keep):
    _cast_weight_once(wg_ref, wgb_ref)
    _cast_weight_once(wp_ref, wpb_ref)
    last = pl.num_programs(1) - 1

    def run(p):
        gate = jnp.dot(e_ref[...], wgb_ref[...], preferred_element_type=f32)
        emb = jnp.dot(p.astype(bf16), wpb_ref[...], preferred_element_type=f32)
        o_ref[...] = r_ref[...] + emb * jax.nn.sigmoid(gate)

    @pl.when(pl.program_id(1) < last)
    def _():
        run(pp_ref[...])

    @pl.when(pl.program_id(1) == last)
    def _():
        run(_last_tile(pp_ref, ps_ref[...], n_keep))


def _mm_mix_body(a1p_ref, a1s_ref, a2_ref, a3p_ref, a3s_ref, w_ref, r_ref, o_ref, wb_ref, *, n_keep):
    _cast_weight_once(w_ref, wb_ref)
    last = pl.num_programs(1) - 1
    k1 = a1p_ref.shape[1]
    k2 = a2_ref.shape[1]

    def run(a1, a3):
        acc = jnp.dot(a1, wb_ref[0:k1, :], preferred_element_type=f32)
        acc += jnp.dot(a2_ref[...], wb_ref[k1:k1 + k2, :], preferred_element_type=f32)
        acc += jnp.dot(a3, wb_ref[k1 + k2:, :], preferred_element_type=f32)
        o_ref[...] = r_ref[...] + acc

    @pl.when(pl.program_id(1) < last)
    def _():
        run(a1p_ref[...], a3p_ref[...])

    @pl.when(pl.program_id(1) == last)
    def _():
        run(_last_tile(a1p_ref, a1s_ref[...].astype(bf16), n_keep),
            _last_tile(a3p_ref, a3s_ref[...].astype(bf16), n_keep))


def _mm_mix(a1p, a1s, a2, a3p, a3s, w, resid, *, tm, tn, layer, name):
    m, n = resid.shape
    tp, k1 = a1p.shape
    bs = a1s.shape[0]
    k2, k3 = a2.shape[1], a3p.shape[1]
    n_tiles, n_keep = _split_rows(tp, bs, tm)
    rows = lambda j, i: (i, 0)
    fix = lambda j, i: (0, 0)
    o_spec = pl.BlockSpec((tm, tn), lambda j, i: (i, j))
    return pl.pallas_call(
        functools.partial(_mm_mix_body, n_keep=n_keep), grid=(n // tn, n_tiles),
        in_specs=[pl.BlockSpec((tm, k1), rows), pl.BlockSpec((bs, k1), fix), pl.BlockSpec((tm, k2), rows),
                  pl.BlockSpec((tm, k3), rows), pl.BlockSpec((bs, k3), fix),
                  _wspec(w, layer, k1 + k2 + k3, tn), o_spec],
        out_specs=o_spec, out_shape=jax.ShapeDtypeStruct((m, n), f32),
        scratch_shapes=[pltpu.VMEM((k1 + k2 + k3, tn), bf16)],
        compiler_params=_params("arbitrary", "arbitrary"), name=name)(a1p, a1s, a2, a3p, a3s, w, resid)


def _wspec(w, layer, k, tn, col_block_off=0):
    if w.ndim == 2:
        return pl.BlockSpec((k, tn), lambda j, i: (0, j + col_block_off))
    return pl.BlockSpec((None, k, tn), lambda j, i: (layer, 0, j + col_block_off))


def _mm(a, w, *, tm, tn, layer=0, n_cols=None, col_off=0, resid=None, out_dtype=f32, name="mm"):
    m, k = a.shape
    n = n_cols if n_cols is not None else w.shape[-1]
    grid = (n // tn, m // tm)
    a_spec = pl.BlockSpec((tm, k), lambda j, i: (i, 0))
    o_spec = pl.BlockSpec((tm, tn), lambda j, i: (i, j))
    w_spec = _wspec(w, layer, k, tn, col_off // tn)
    scratch = [pltpu.VMEM((k, tn), bf16)]
    if resid is None:
        body, ins, specs = _mm_plain_body, (a, w), [a_spec, w_spec]
    else:
        body, ins, specs = _mm_resid_body, (a, w, resid), [a_spec, w_spec, o_spec]
    return pl.pallas_call(
        body, grid=grid, in_specs=specs, out_specs=o_spec,
        out_shape=jax.ShapeDtypeStruct((m, n), out_dtype), scratch_shapes=scratch,
        compiler_params=_params("arbitrary", "arbitrary"), name=name)(*ins)


def _mm_nt_body(a_ref, wt_ref, o_ref, wb_ref):
    @pl.when(pl.program_id(1) == 0)
    def _():
        wb_ref[...] = wt_ref[0].T.astype(bf16)

    o_ref[...] = jnp.dot(a_ref[...], wb_ref[...], preferred_element_type=f32)


def _wt_rows(layer, k, n_rows, row_of):
    return pl.BlockSpec((pl.Element(1), pl.Element(n_rows), pl.Element(k)),
                        lambda *idx: (layer, row_of(*idx), 0))


def _mm_nt(a, wt, *, tm, tn, layer, row_off, n_rows, name):
    m, k = a.shape
    return pl.pallas_call(
        _mm_nt_body, grid=(n_rows // tn, m // tm),
        in_specs=[pl.BlockSpec((tm, k), lambda j, i: (i, 0)),
                  _wt_rows(layer, k, tn, lambda j, i: pl.multiple_of(row_off + tn * j, 8))],
        out_specs=pl.BlockSpec((tm, tn), lambda j, i: (i, j)),
        out_shape=jax.ShapeDtypeStruct((m, n_rows), f32), scratch_shapes=[pltpu.VMEM((k, tn), bf16)],
        compiler_params=_params("arbitrary", "arbitrary"), name=name)(a, wt)


def _mm_gates_body(a_ref, w1_ref, w2_ref, o_ref):
    a = a_ref[...]
    n1 = w1_ref.shape[1]
    o_ref[:, 0:n1] = lax.dot_general(a, w1_ref[0].astype(bf16), _NT, preferred_element_type=f32)
    o_ref[:, n1:] = lax.dot_general(a, w2_ref[0].astype(bf16), _NT, preferred_element_type=f32)


def _mm_gates(a, wt, *, tm, layer, off1, n1, off2, n2, name):
    m, k = a.shape
    return pl.pallas_call(
        _mm_gates_body, grid=(m // tm,),
        in_specs=[pl.BlockSpec((tm, k), lambda i: (i, 0)),
                  _wt_rows(layer, k, n1, lambda i: off1), _wt_rows(layer, k, n2, lambda i: off2)],
        out_specs=pl.BlockSpec((tm, n1 + n2), lambda i: (i, 0)),
        out_shape=jax.ShapeDtypeStruct((m, n1 + n2), f32),
        compiler_params=_params("parallel"), name=name)(a, wt, wt)


def _mm_swiglu(a, wg, wu, *, tm, tn, layer=0, name="mm_swiglu"):
    m, k = a.shape
    n = wg.shape[-1]
    a_spec = pl.BlockSpec((tm, k), lambda j, i: (i, 0))
    o_spec = pl.BlockSpec((tm, tn), lambda j, i: (i, j))
    return pl.pallas_call(
        _mm_swiglu_body, grid=(n // tn, m // tm),
        in_specs=[a_spec, _wspec(wg, layer, k, tn), _wspec(wu, layer, k, tn)], out_specs=o_spec,
        out_shape=jax.ShapeDtypeStruct((m, n), bf16),
        scratch_shapes=[pltpu.VMEM((k, tn), bf16), pltpu.VMEM((k, tn), bf16)],
        compiler_params=_params("arbitrary", "arbitrary"), name=name)(a, wg, wu)


def _mm_ple(e, p_prompt, p_sample, w_gate, w_ple, resid, *, tm, tn, layer, name="mm_ple"):
    m, k = e.shape
    _, tp, kp = p_prompt.shape
    bs = p_sample.shape[1]
    n = w_gate.shape[-1]
    n_tiles, n_keep = _split_rows(tp, bs, tm)
    o_spec = pl.BlockSpec((tm, tn), lambda j, i: (i, j))
    return pl.pallas_call(
        functools.partial(_mm_ple_body, n_keep=n_keep), grid=(n // tn, n_tiles),
        in_specs=[pl.BlockSpec((tm, k), lambda j, i: (i, 0)),
                  pl.BlockSpec((None, tm, kp), lambda j, i: (layer, i, 0)),
                  pl.BlockSpec((None, bs, kp), lambda j, i: (layer, 0, 0)),
                  _wspec(w_gate, layer, k, tn), _wspec(w_ple, layer, kp, tn), o_spec],
        out_specs=o_spec, out_shape=jax.ShapeDtypeStruct((m, n), f32),
        scratch_shapes=[pltpu.VMEM((k, tn), bf16), pltpu.VMEM((kp, tn), bf16)],
        compiler_params=_params("arbitrary", "arbitrary"), name=name)(e, p_prompt, p_sample, w_gate, w_ple, resid)


def _col_from_row(row, eye):
    n = row.shape[1]
    return jnp.sum(jnp.where(eye, jnp.broadcast_to(row, (n, n)), 0.0), axis=1, keepdims=True)


def _mlstm_prompt_body(*refs, heads, dk, dv, cl, nb):
    q_refs, k_refs, v_refs, o_refs = (refs[i * nb:(i + 1) * nb] for i in range(4))
    gt_ref, bias_ref, gml_ref, h_ref, c_out, n_out, m_out, c_s, n_s, m_s = refs[4 * nb:]
    c = pl.program_id(1)

    @pl.when(c == 0)
    def _():
        c_s[...] = jnp.zeros_like(c_s)
        n_s[...] = jnp.zeros_like(n_s)
        m_s[...] = jnp.zeros_like(m_s)

    row = lax.broadcasted_iota(jnp.int32, (cl, cl), 0)
    col = lax.broadcasted_iota(jnp.int32, (cl, cl), 1)
    causal = col <= row
    eye = col == row
    triu = (row <= col).astype(f32)
    for r in range(nb):
        gt = gt_ref[r, 0] + bias_ref[...]
        lf = jax.nn.log_sigmoid(gt[heads:2 * heads])
        b_all = jnp.dot(lf, triu, precision=HI, preferred_element_type=f32)
        for hh in range(heads):
            st = r * heads + hh
            ig = gt[hh:hh + 1]
            b_row = b_all[hh:hh + 1]
            b_col = _col_from_row(b_row, eye)
            m_prev = m_s[st]
            d = jnp.where(causal, b_col - b_row + ig, -jnp.inf)
            inter = b_col + m_prev
            m_t = jnp.maximum(inter, jnp.max(d, axis=1, keepdims=True))
            w = jnp.exp(d - m_t)
            g = jnp.exp(inter - m_t)
            q = q_refs[r][:, hh * dk:(hh + 1) * dk]
            k = k_refs[r][:, hh * dk:(hh + 1) * dk] * (dk ** -0.5)
            vb = v_refs[r][:, hh * dv:(hh + 1) * dv].astype(bf16)
            qb = q.astype(bf16)
            cmat = c_s[st]
            n_row = n_s[st]
            s = lax.dot_general(qb, k.astype(bf16), _NT, preferred_element_type=f32) * w
            num = (jnp.dot(s.astype(bf16), vb, preferred_element_type=f32)
                   + g * jnp.dot(qb, cmat.astype(bf16), preferred_element_type=f32))
            den = jnp.sum(s, axis=1, keepdims=True) + g * jnp.sum(q * n_row, axis=1, keepdims=True)
            hraw = num / jnp.maximum(jnp.abs(den), jnp.exp(-m_t))
            hn = _rms(hraw, gml_ref[hh:hh + 1, :])
            ogate = jax.nn.sigmoid(o_refs[r][:, hh * dv:(hh + 1) * dv])
            h_ref[r, :, hh * dv:(hh + 1) * dv] = (ogate * hn).astype(h_ref.dtype)
            b_last = b_row[:, cl - 1:cl]
            dl = b_last - b_row + ig
            m_new = jnp.maximum(b_last + m_prev, jnp.max(dl, axis=1, keepdims=True))
            ws_col = _col_from_row(jnp.exp(dl - m_new), eye)
            gl = jnp.exp(b_last + m_prev - m_new)
            kw = k * ws_col
            c_s[st] = gl * cmat + lax.dot_general(kw.astype(bf16), vb, _TN, preferred_element_type=f32)
            n_s[st] = gl * n_row + jnp.sum(kw, axis=0, keepdims=True)
            m_s[st] = m_new

    @pl.when(c == pl.num_programs(1) - 1)
    def _():
        for r in range(nb):
            c_out[r] = c_s[r * heads:(r + 1) * heads]
            n_out[r] = n_s[r * heads:(r + 1) * heads]
            m_out[r] = m_s[r * heads:(r + 1) * heads]


_PROMPT_SEQS_PER_STEP = 4


def _mlstm_prompt(qkvo, gates_t, bias_col, g_ml, *, batch, seq):
    heads, dk, dv, cl = ML_HEADS, ML_DK, ML_DV, math.gcd(seq, CHUNK)
    nc = seq // cl
    wq = heads * dk
    nb = math.gcd(batch, _PROMPT_SEQS_PER_STEP)
    body = functools.partial(_mlstm_prompt_body, heads=heads, dk=dk, dv=dv, cl=cl, nb=nb)
    qkvo_specs = [pl.BlockSpec((cl, wq), lambda b, c, r=r, part=part: ((b * nb + r) * nc + c, part))
                  for part in range(4) for r in range(nb)]
    return pl.pallas_call(
        body, grid=(batch // nb, nc),
        in_specs=qkvo_specs + [pl.BlockSpec((nb, 1, 2 * heads, cl), lambda b, c: (b, c, 0, 0)),
                               pl.BlockSpec((2 * heads, 1), lambda b, c: (0, 0)),
                               pl.BlockSpec((heads, dv), lambda b, c: (0, 0))],
        out_specs=[pl.BlockSpec((nb, cl, heads * dv), lambda b, c: (b, c, 0)),
                   pl.BlockSpec((nb, heads, dk, dv), lambda b, c: (b, 0, 0, 0)),
                   pl.BlockSpec((nb, heads, 1, dk), lambda b, c: (b, 0, 0, 0)),
                   pl.BlockSpec((nb, heads, 1, 1), lambda b, c: (b, 0, 0, 0))],
        out_shape=[jax.ShapeDtypeStruct((batch, seq, heads * dv), bf16),
                   jax.ShapeDtypeStruct((batch, heads, dk, dv), f32),
                   jax.ShapeDtypeStruct((batch, heads, 1, dk), f32),
                   jax.ShapeDtypeStruct((batch, heads, 1, 1), f32)],
        scratch_shapes=[pltpu.VMEM((nb * heads, dk, dv), f32), pltpu.VMEM((nb * heads, 1, dk), f32),
                        pltpu.VMEM((nb * heads, 1, 1), f32)],
        compiler_params=_params("arbitrary", "arbitrary"), name="mlstm_prompt",
    )(*([qkvo] * (4 * nb)), gates_t, bias_col, g_ml)


def _lane_pick(pieces, width):
    lane = lax.broadcasted_iota(jnp.int32, (1, width), 1)
    out = jnp.zeros((1, width), f32)
    for i, p in enumerate(pieces):
        out = jnp.where(lane == i, p, out)
    return out


_SAMPLE_BLOCK = 8


def _mlstm_sample_body(*refs, heads, dk, dv, aliased):
    if aliased:
        refs = refs[:10] + refs[11:]
    (q_ref, k_ref, v_ref, o_ref, gd_ref, bias_ref, gml_ref, c0_ref, n0_ref, m0_ref,
     h_ref, c_out, n_out, m_out) = refs
    bt = q_ref.shape[0]

    @pl.when(pl.program_id(0) > 0)
    def _():
        c_out[...] = jnp.zeros_like(c_out)

    @pl.when(pl.program_id(0) == 0)
    def _():
        g8 = gd_ref[:, 0:2 * heads] + bias_ref[...]
        lane = lax.broadcasted_iota(jnp.int32, (bt, heads), 1)
        m_new = jnp.zeros((bt, heads), f32)
        for hh in range(heads):
            ig = g8[:, hh:hh + 1]
            lf = jax.nn.log_sigmoid(g8[:, heads + hh:heads + hh + 1])
            m0 = m0_ref[:, hh:hh + 1]
            m_t = jnp.maximum(lf + m0, ig)
            w = jnp.exp(ig - m_t)
            g = jnp.exp(lf + m0 - m_t)
            q = q_ref[:, hh * dk:(hh + 1) * dk]
            k = k_ref[:, hh * dk:(hh + 1) * dk] * (dk ** -0.5)
            v = v_ref[:, hh * dv:(hh + 1) * dv]
            o = o_ref[:, hh * dv:(hh + 1) * dv]
            n0 = n0_ref[:, hh, :]
            kw = k * w
            q_t = q.T
            kw_t = kw.T
            qc_rows = []
            for b in range(bt):
                cmat = c0_ref[b, hh]
                qc_rows.append(jnp.sum(q_t[:, b:b + 1] * cmat, axis=0, keepdims=True))
                c_out[b, hh] = g[b:b + 1, :] * cmat + kw_t[:, b:b + 1] * v[b:b + 1, :]
            qc = jnp.concatenate(qc_rows, axis=0)
            s = jnp.sum(q * k, axis=1, keepdims=True) * w
            num = s * v + g * qc
            den = s + g * jnp.sum(q * n0, axis=1, keepdims=True)
            hraw = num / jnp.maximum(jnp.abs(den), jnp.exp(-m_t))
            hn = _rms(hraw, gml_ref[hh:hh + 1, :])
            h_ref[:, hh * dv:(hh + 1) * dv] = jax.nn.sigmoid(o) * hn
            n_out[:, hh, :] = g * n0 + kw
            m_new = jnp.where(lane == hh, m_t, m_new)
        m_out[...] = m_new


def _stacked_state_grid(layer, b, stack):
    n_l = DEPTH if stack is None else 1
    row = lambda l, i: jnp.where(l == 0, i, b - 1)
    out_layer = lambda l: (layer + l) % DEPTH
    return (n_l, b), row, out_layer


def _mlstm_sample(qkvo, gd, bias_row, g_ml, c0, n0, m0, *, layer, row0, c_stack=None):
    heads, dk, dv = ML_HEADS, ML_DK, ML_DV
    b = c0.shape[1]
    bt = _SAMPLE_BLOCK
    wq = heads * dk
    blk0 = row0 // bt
    grid, row, out_layer = _stacked_state_grid(layer, b // bt, c_stack)
    aliased = c_stack is not None
    body = functools.partial(_mlstm_sample_body, heads=heads, dk=dk, dv=dv, aliased=aliased)
    in_specs = [pl.BlockSpec((bt, wq), lambda l, i, c=c: (blk0 + row(l, i), c)) for c in range(4)]
    in_specs += [pl.BlockSpec((bt, gd.shape[-1]), lambda l, i: (blk0 + row(l, i), 0)),
                 pl.BlockSpec((1, 2 * heads), lambda l, i: (0, 0)),
                 pl.BlockSpec((heads, dv), lambda l, i: (0, 0)),
                 pl.BlockSpec((None, bt, heads, dk, dv), lambda l, i: (layer, row(l, i), 0, 0, 0)),
                 pl.BlockSpec((None, bt, heads, dk), lambda l, i: (layer, row(l, i), 0, 0)),
                 pl.BlockSpec((None, bt, heads), lambda l, i: (layer, row(l, i), 0))]
    args = [qkvo, qkvo, qkvo, qkvo, gd, bias_row, g_ml, c0, n0, m0]
    if aliased:
        in_specs.append(pl.BlockSpec(memory_space=pl.ANY))
        args.append(c_stack)
    return pl.pallas_call(
        body, grid=grid, in_specs=in_specs,
        out_specs=[pl.BlockSpec((bt, heads * dv), lambda l, i: (row(l, i), 0)),
                   pl.BlockSpec((None, bt, heads, dk, dv), lambda l, i: (out_layer(l), i, 0, 0, 0)),
                   pl.BlockSpec((bt, heads, dk), lambda l, i: (row(l, i), 0, 0)),
                   pl.BlockSpec((bt, heads), lambda l, i: (row(l, i), 0))],
        out_shape=[jax.ShapeDtypeStruct((b, heads * dv), f32),
                   jax.ShapeDtypeStruct((DEPTH, b, heads, dk, dv), f32),
                   jax.ShapeDtypeStruct((b, heads, dk), f32),
                   jax.ShapeDtypeStruct((b, heads), f32)],
        input_output_aliases={10: 1} if aliased else {},
        compiler_params=_params("arbitrary", "arbitrary"), name="mlstm_sample",
    )(*args)


def _s5_advance(x, l1, l2):
    return l1 * x + l2 * pltpu.roll(x, x.shape[-1] // 2, axis=1)


def _toeplitz_operator(krow):
    ch, tc = krow.shape
    lane = lax.broadcasted_iota(jnp.int32, krow.shape, 1)
    blocks = [krow] + [jnp.where(lane >= ch * s, pltpu.roll(krow, ch * s, axis=1), 0.0)
                       for s in range(1, tc // ch)]
    return jnp.concatenate(blocks, axis=0).astype(bf16)


def _s5_body(u_ref, m_ref, w_ref, v_ref, l1_ref, l2_ref, x0_ref, y_ref, x_out, xs_s, *, nc, batch):
    u = u_ref[0]
    xin = jnp.dot(u, w_ref[0], preferred_element_type=f32)
    l1 = l1_ref[0]
    l2 = l2_ref[0]
    x = x0_ref[0]
    for k in range(nc):
        sl = slice(k * batch, (k + 1) * batch)
        xs_s[sl, :] = x
        x = _s5_advance(x, l1, l2) + xin[sl, :]
    x_out[0] = x
    y_ref[0] = (jnp.dot(u, _toeplitz_operator(m_ref[0]), preferred_element_type=f32)
                + jnp.dot(xs_s[...].astype(bf16), v_ref[0], preferred_element_type=f32))


def _s5_scan(u_g, mats, x0, *, nc, batch, layer):
    m_mat, w_pk, v_pk, l1, l2 = mats
    g, rows, tc = u_g.shape
    p2 = w_pk.shape[-1]
    blk = lambda *s: pl.BlockSpec((1,) + s, lambda i: (i,) + (0,) * len(s))
    op = lambda *s: pl.BlockSpec((None, 1) + s, lambda i: (layer, i) + (0,) * len(s))
    body = functools.partial(_s5_body, nc=nc, batch=batch)
    return pl.pallas_call(
        body, grid=(g,),
        in_specs=[blk(rows, tc), op(S5_CH, tc), op(tc, p2), op(p2, tc), op(1, p2), op(1, p2), blk(batch, p2)],
        out_specs=[blk(rows, tc), blk(batch, p2)],
        out_shape=[jax.ShapeDtypeStruct((g, rows, tc), f32), jax.ShapeDtypeStruct((g, batch, p2), f32)],
        scratch_shapes=[pltpu.VMEM((rows, p2), f32)],
        compiler_params=_params("parallel"), name=f"s5_scan_t{tc // S5_CH}",
    )(u_g, m_mat, w_pk, v_pk, l1, l2, x0)


def _s5_prompt_body(x_ref, m_ref, w_ref, v_ref, l1_ref, l2_ref, y_ref, x_out, u_s, y_s, xin_s, xs_s,
                    *, gw, batch, seq, t, ch):
    nc = seq // t
    for b in range(batch):
        for tt in range(t):
            blk = x_ref[pl.ds(b * seq + tt, nc, stride=t), :]
            for gl in range(gw):
                u_s[gl, b * nc:(b + 1) * nc, tt * ch:(tt + 1) * ch] = blk[:, gl * ch:(gl + 1) * ch]
    for gl in range(gw):
        xin_s[gl] = jnp.dot(u_s[gl].astype(bf16), w_ref[gl], preferred_element_type=f32)
    xs = [jnp.zeros((batch, xin_s.shape[-1]), f32)] * gw
    for k in range(nc):
        for gl in range(gw):
            xs_s[gl, pl.ds(k, batch, stride=nc), :] = xs[gl]
            xs[gl] = _s5_advance(xs[gl], l1_ref[gl], l2_ref[gl]) + xin_s[gl, pl.ds(k, batch, stride=nc), :]
    for gl in range(gw):
        x_out[gl] = xs[gl]
        y_s[gl] = (jnp.dot(u_s[gl].astype(bf16), _toeplitz_operator(m_ref[gl]), preferred_element_type=f32)
                   + jnp.dot(xs_s[gl].astype(bf16), v_ref[gl], preferred_element_type=f32))
    for b in range(batch):
        for tt in range(t):
            y_ref[pl.ds(b * seq + tt, nc, stride=t), :] = jnp.concatenate(
                [y_s[gl, b * nc:(b + 1) * nc, tt * ch:(tt + 1) * ch] for gl in range(gw)], axis=1)


def _s5_prompt(uzx, mats, *, batch, seq, layer):
    m_mat, w_pk, v_pk, l1, l2 = mats
    _, g, tc, p2 = w_pk.shape
    ch = S5_CH
    t = tc // ch
    gw = 128 // ch
    rows = batch * (seq // t)
    tp = batch * seq
    win = lambda *s: pl.BlockSpec((gw,) + s, lambda i: (i,) + (0,) * len(s))
    op = lambda *s: pl.BlockSpec((None, gw) + s, lambda i: (layer, i) + (0,) * len(s))
    body = functools.partial(_s5_prompt_body, gw=gw, batch=batch, seq=seq, t=t, ch=ch)
    return pl.pallas_call(
        body, grid=(g // gw,),
        in_specs=[pl.BlockSpec((tp, gw * ch), lambda i: (0, i)), op(ch, tc), op(tc, p2), op(p2, tc),
                  op(1, p2), op(1, p2)],
        out_specs=[pl.BlockSpec((tp, gw * ch), lambda i: (0, i)), win(batch, p2)],
        out_shape=[jax.ShapeDtypeStruct((tp, g * ch), f32), jax.ShapeDtypeStruct((g, batch, p2), f32)],
        scratch_shapes=[pltpu.VMEM((gw, rows, tc), f32), pltpu.VMEM((gw, rows, tc), f32),
                        pltpu.VMEM((gw, rows, p2), f32), pltpu.VMEM((gw, rows, p2), f32)],
        compiler_params=_params("parallel"), name="s5_prompt",
    )(uzx, m_mat, w_pk, v_pk, l1, l2)


def _s5_matrices(lam_re, lam_im, log_dt, b_re, b_im, c_re, c_im, t):
    g, p = lam_re.shape
    ch = b_re.shape[-1]
    dt = jnp.exp(log_dt)[:, None]
    ar, ai = lam_re * dt, lam_im * dt

    def powers(tau):
        mag = jnp.exp(ar[:, None, :] * tau[None, :, None])
        ang = ai[:, None, :] * tau[None, :, None]
        return mag * jnp.cos(ang), mag * jnp.sin(ang)

    lbr, lbi = jnp.exp(ar) * jnp.cos(ai), jnp.exp(ar) * jnp.sin(ai)
    den = lam_re * lam_re + lam_im * lam_im
    fr = ((lbr - 1.0) * lam_re + lbi * lam_im) / den
    fi = (lbi * lam_re - (lbr - 1.0) * lam_im) / den
    bbr = jnp.swapaxes(fr[..., None] * b_re - fi[..., None] * b_im, 1, 2)
    bbi = jnp.swapaxes(fr[..., None] * b_im + fi[..., None] * b_re, 1, 2)
    cbr = c_re[:, :, None, :] * bbr[:, None, :, :] - c_im[:, :, None, :] * bbi[:, None, :, :]
    cbi = c_re[:, :, None, :] * bbi[:, None, :, :] + c_im[:, :, None, :] * bbr[:, None, :, :]
    steps = jnp.arange(t, dtype=f32)
    lr, li = powers(steps)
    krow = jnp.einsum("gcdp,gtp->gdtc", jnp.concatenate([cbr, -cbi], axis=-1),
                      jnp.concatenate([lr, li], axis=-1), precision=HI).reshape(g, ch, t * ch)
    pr, pi = powers(t - 1.0 - steps)
    w_re = (pr[:, :, None, :] * bbr[:, None, :, :] - pi[:, :, None, :] * bbi[:, None, :, :]).reshape(g, t * ch, p)
    w_im = (pr[:, :, None, :] * bbi[:, None, :, :] + pi[:, :, None, :] * bbr[:, None, :, :]).reshape(g, t * ch, p)
    qr, qi = powers(steps + 1.0)
    qr, qi = jnp.swapaxes(qr, 1, 2)[..., None], jnp.swapaxes(qi, 1, 2)[..., None]
    ctr, cti = jnp.swapaxes(c_re, 1, 2)[:, :, None, :], jnp.swapaxes(c_im, 1, 2)[:, :, None, :]
    v_re = (ctr * qr - cti * qi).reshape(g, p, t * ch)
    v_im = -(ctr * qi + cti * qr).reshape(g, p, t * ch)
    ltr, lti = powers(jnp.full((1,), float(t), f32))
    w_pk = jnp.concatenate([w_re, w_im], axis=-1).astype(bf16)
    v_pk = jnp.concatenate([v_re, v_im], axis=1).astype(bf16)
    chunk_ops = (krow, w_pk, v_pk, jnp.concatenate([ltr, ltr], axis=-1), jnp.concatenate([-lti, lti], axis=-1))
    lb_r, lb_i = lbr[:, None, :], lbi[:, None, :]
    step_ops = (krow[:, :, :ch], w_pk[:, (t - 1) * ch:, :], v_pk[:, :, :ch],
                jnp.concatenate([lb_r, lb_r], axis=-1), jnp.concatenate([-lb_i, lb_i], axis=-1))
    return chunk_ops, step_ops


def _s5_glu_body(yp_ref, ys_ref, u_ref, d_ref, w_ref, b_ref, g_ref, o_ref, *, n_keep):
    last = pl.num_programs(0) - 1

    def run(y_raw):
        y5 = jax.nn.gelu(y_raw + d_ref[...] * u_ref[...])
        gate = jax.nn.sigmoid(jnp.dot(y5.astype(bf16), w_ref[...].astype(bf16), preferred_element_type=f32)
                              + b_ref[...])
        o_ref[...] = _rms(y5 * gate, g_ref[...]).astype(o_ref.dtype)

    @pl.when(pl.program_id(0) < last)
    def _():
        run(yp_ref[...])

    @pl.when(pl.program_id(0) == last)
    def _():
        run(_last_tile(yp_ref, ys_ref[...], n_keep))


def _s5_glu(y_prompt, y_sample, uzx, d_skip, w_glu, b_glu, g_s5, *, layer, tm):
    tp, wdt = y_prompt.shape
    bs = y_sample.shape[0]
    n_tiles, n_keep = _split_rows(tp, bs, tm)
    row = lambda i: (i, 0)
    fix = lambda i: (0, 0)
    return pl.pallas_call(
        functools.partial(_s5_glu_body, n_keep=n_keep), grid=(n_tiles,),
        in_specs=[pl.BlockSpec((tm, wdt), row), pl.BlockSpec((bs, wdt), fix), pl.BlockSpec((tm, wdt), row),
                  pl.BlockSpec((1, wdt), fix), pl.BlockSpec((None, wdt, wdt), lambda i: (layer, 0, 0)),
                  pl.BlockSpec((1, wdt), fix), pl.BlockSpec((1, wdt), fix)],
        out_specs=pl.BlockSpec((tm, wdt), row), out_shape=jax.ShapeDtypeStruct((tp + bs, wdt), bf16),
        compiler_params=_params("parallel"), name="s5_glu",
    )(y_prompt, y_sample, uzx, d_skip, w_glu, b_glu, g_s5)


def _ssd_prompt_body(xbc_ref, z_ref, gd_ref, dtt_ref, cw_ref, cb_ref, dtb_row, dtb_col, alog_row, alog_col,
                     dskip_ref, gssd_ref, y_ref, s_out, s_s, xp_s, ys_s, *, heads, hd, ns, groups, cl, width):
    c = pl.program_id(1)

    @pl.when(c == 0)
    def _():
        s_s[...] = jnp.zeros_like(s_s)
        xp_s[0:8, :] = jnp.zeros((8, xp_s.shape[1]), f32)

    xp_s[8:8 + cl, :] = xbc_ref[...]
    xc = cb_ref[...] + sum(cw_ref[j:j + 1, :] * xp_s[5 + j:5 + j + cl, :] for j in range(SSD_CONV))
    xp_s[0:8, :] = xp_s[cl:cl + 8, :]
    xc = xc * jax.nn.sigmoid(xc)
    row = lax.broadcasted_iota(jnp.int32, (cl, cl), 0)
    col = lax.broadcasted_iota(jnp.int32, (cl, cl), 1)
    tril = (col <= row).astype(f32)
    triu = (row <= col).astype(f32)
    row2 = lax.broadcasted_iota(jnp.int32, (cl, 2 * cl), 0)
    lane2 = lax.broadcasted_iota(jnp.int32, (cl, 2 * cl), 1)
    left = lane2 < cl
    causal2 = jnp.where(left, lane2, lane2 - cl) <= row2
    left_row = left[0:1, :]
    top = lax.broadcasted_iota(jnp.int32, (2 * hd, 1), 0) < hd
    dt_col = jax.nn.softplus(gd_ref[:, 8:8 + heads] + dtb_row[...])
    dt_row = jax.nn.softplus(dtt_ref[0, 0] + dtb_col[...])
    cum_col = jnp.dot(tril, dt_col * -jnp.exp(alog_row[...]), precision=HI, preferred_element_type=f32)
    cum_row = jnp.dot(dt_row * -jnp.exp(alog_col[...]), triu, precision=HI, preferred_element_type=f32)
    exp_col = jnp.exp(cum_col)
    pick = lambda cols, h0: jnp.where(left, cols[:, h0:h0 + 1], cols[:, h0 + 1:h0 + 2])
    rep = heads // groups
    for gi in range(groups):
        bm = xc[:, width + gi * ns:width + (gi + 1) * ns].astype(bf16)
        cm = xc[:, width + (groups + gi) * ns:width + (groups + gi + 1) * ns].astype(bf16)
        scores = lax.dot_general(cm, bm, _NT, preferred_element_type=f32)
        scores2 = jnp.concatenate([scores, scores], axis=1)
        for h0 in range(gi * rep, (gi + 1) * rep, 2):
            lo, hi = h0 * hd, (h0 + 2) * hd
            cc2 = pick(cum_col, h0)
            cr2 = jnp.concatenate([cum_row[h0:h0 + 1, :], cum_row[h0 + 1:h0 + 2, :]], axis=1)
            seg2 = jnp.exp(jnp.where(causal2, cc2 - cr2, -jnp.inf))
            x2 = xc[:, lo:hi]
            xdt2 = x2 * pick(dt_col, h0)
            xbd = jnp.concatenate([jnp.where(left, xdt2, 0.0), jnp.where(left, 0.0, xdt2)], axis=0)
            smat2 = s_s[lo:hi, :]
            y2 = (jnp.dot((scores2 * seg2).astype(bf16), xbd.astype(bf16), preferred_element_type=f32)
                  + pick(exp_col, h0) * lax.dot_general(cm, smat2.astype(bf16), _NT, preferred_element_type=f32))
            last0 = cum_row[h0:h0 + 1, cl - 1:cl]
            last1 = cum_row[h0 + 1:h0 + 2, cl - 1:cl]
            xw2 = (xdt2 * jnp.exp(jnp.where(left_row, last0, last1) - cc2)).astype(bf16)
            s_s[lo:hi, :] = (jnp.where(top, jnp.exp(last0), jnp.exp(last1)) * smat2
                             + lax.dot_general(xw2, bm, _TN, preferred_element_type=f32))
            dsk2 = jnp.where(left_row, dskip_ref[:, h0:h0 + 1], dskip_ref[:, h0 + 1:h0 + 2])
            ys_s[:, lo:hi] = y2 + dsk2 * x2
    z = z_ref[...]
    y_ref[...] = _rms(ys_s[...] * (z * jax.nn.sigmoid(z)), gssd_ref[...]).astype(y_ref.dtype)

    @pl.when(c == pl.num_programs(1) - 1)
    def _():
        s_out[0] = s_s[...]


def _ssd_prompt(uzx, gd, dt_t, conv_w, conv_b, dt_bias, a_log, d_skip, g_ssd, *, batch, seq):
    heads, hd, ns, groups, width = SSD_HEADS, SSD_HEAD_DIM, SSD_STATE, SSD_GROUPS, SSD_WIDTH
    cl = math.gcd(seq, CHUNK)
    assert cl == hd and (heads // groups) % 2 == 0
    nc = seq // cl
    cch = SSD_CONV_CH
    rows = lambda b, c: (b * nc + c, 0)
    fix = lambda b, c: (0, 0)
    body = functools.partial(_ssd_prompt_body, heads=heads, hd=hd, ns=ns, groups=groups, cl=cl, width=width)
    return pl.pallas_call(
        body, grid=(batch, nc),
        in_specs=[pl.BlockSpec((cl, cch), lambda b, c: (b * nc + c, 1)),
                  pl.BlockSpec((cl, width), lambda b, c: (b * nc + c, 1)),
                  pl.BlockSpec((cl, gd.shape[1]), rows),
                  pl.BlockSpec((1, 1, heads, cl), lambda b, c: (b, c, 0, 0)),
                  pl.BlockSpec((SSD_CONV, cch), fix), pl.BlockSpec((1, cch), fix),
                  pl.BlockSpec((1, heads), fix), pl.BlockSpec((heads, 1), fix),
                  pl.BlockSpec((1, heads), fix), pl.BlockSpec((heads, 1), fix),
                  pl.BlockSpec((1, heads), fix), pl.BlockSpec((1, width), fix)],
        out_specs=[pl.BlockSpec((cl, width), rows),
                   pl.BlockSpec((1, heads * hd, ns), lambda b, c: (b, 0, 0))],
        out_shape=[jax.ShapeDtypeStruct((batch * seq, width), bf16),
                   jax.ShapeDtypeStruct((batch, heads * hd, ns), f32)],
        scratch_shapes=[pltpu.VMEM((heads * hd, ns), f32), pltpu.VMEM((cl + 8, cch), f32),
                        pltpu.VMEM((cl, width), f32)],
        compiler_params=_params("arbitrary", "arbitrary"), name="ssd_prompt",
    )(uzx, uzx, gd, dt_t, conv_w, conv_b.reshape(1, cch), dt_bias.reshape(1, heads), dt_bias.reshape(heads, 1),
      a_log.reshape(1, heads), a_log.reshape(heads, 1), d_skip.reshape(1, heads), g_ssd.reshape(1, width))


def _ssd_sample_body(*refs, heads, hd, ns, groups, width, aliased):
    if aliased:
        refs = refs[:10] + refs[11:]
    (x_ref, conv0_ref, gd_ref, cw_ref, cb_ref, dtb_ref, alog_ref, dskip_ref, gssd_ref, s0_ref,
     y_ref, s_out, ys_s) = refs
    bt = x_ref.shape[0]

    @pl.when(pl.program_id(0) > 0)
    def _():
        s_out[...] = jnp.zeros_like(s_out)

    @pl.when(pl.program_id(0) == 0)
    def _():
        z = x_ref[:, width:2 * width]
        xc = cb_ref[...] + cw_ref[SSD_CONV - 1:SSD_CONV, :] * x_ref[:, 2 * width:]
        for j in range(SSD_CONV - 1):
            xc = xc + cw_ref[j:j + 1, :] * conv0_ref[:, j, :]
        xc = xc * jax.nn.sigmoid(xc)
        dt = jax.nn.softplus(gd_ref[:, 8:8 + heads] + dtb_ref[...])
        ea = jnp.exp(dt * -jnp.exp(alog_ref[...]))
        rep = heads // groups
        for gi in range(groups):
            bm = xc[:, width + gi * ns:width + (gi + 1) * ns]
            cm = xc[:, width + (groups + gi) * ns:width + (groups + gi + 1) * ns]
            cb_dot = jnp.sum(cm * bm, axis=1, keepdims=True)
            cmb = cm.astype(bf16)
            for hh in range(gi * rep, (gi + 1) * rep):
                xh = xc[:, hh * hd:(hh + 1) * hd]
                xdt = xh * dt[:, hh:hh + 1]
                eah = ea[:, hh:hh + 1]
                xdt_t = xdt.T
                sc_rows = []
                for b in range(bt):
                    smat = s0_ref[b, hh]
                    sc_rows.append(lax.dot_general(cmb, smat.astype(bf16), _NT,
                                                   preferred_element_type=f32)[b:b + 1, :])
                    s_out[b, hh] = eah[b:b + 1, :] * smat + xdt_t[:, b:b + 1] * bm[b:b + 1, :]
                sc = jnp.concatenate(sc_rows, axis=0)
                ys_s[:, hh * hd:(hh + 1) * hd] = cb_dot * xdt + eah * sc + dskip_ref[:, hh:hh + 1] * xh
        y_ref[...] = _rms(ys_s[...] * (z * jax.nn.sigmoid(z)), gssd_ref[...])


def _ssd_sample(uzx, conv0, gd, conv_w, conv_b, dt_bias, a_log, d_skip, g_ssd, s0, *, layer, row0, s_stack=None):
    heads, hd, ns, groups, width = SSD_HEADS, SSD_HEAD_DIM, SSD_STATE, SSD_GROUPS, SSD_WIDTH
    cch = SSD_CONV_CH
    b = s0.shape[1]
    bt = _SAMPLE_BLOCK
    blk0 = row0 // bt
    fix = lambda l, i: (0, 0)
    grid, row, out_layer = _stacked_state_grid(layer, b // bt, s_stack)
    aliased = s_stack is not None
    body = functools.partial(_ssd_sample_body, heads=heads, hd=hd, ns=ns, groups=groups, width=width,
                             aliased=aliased)
    in_specs = [pl.BlockSpec((bt, uzx.shape[-1]), lambda l, i: (blk0 + row(l, i), 0)),
                pl.BlockSpec((None, bt, SSD_CONV - 1, cch), lambda l, i: (layer, row(l, i), 0, 0)),
                pl.BlockSpec((bt, gd.shape[-1]), lambda l, i: (blk0 + row(l, i), 0)),
                pl.BlockSpec((SSD_CONV, cch), fix), pl.BlockSpec((1, cch), fix),
                pl.BlockSpec((1, heads), fix), pl.BlockSpec((1, heads), fix), pl.BlockSpec((1, heads), fix),
                pl.BlockSpec((1, width), fix),
                pl.BlockSpec((None, bt, heads, hd, ns), lambda l, i: (layer, row(l, i), 0, 0, 0))]
    args = [uzx, conv0, gd, conv_w, conv_b.reshape(1, cch), dt_bias.reshape(1, heads), a_log.reshape(1, heads),
            d_skip.reshape(1, heads), g_ssd.reshape(1, width), s0]
    if aliased:
        in_specs.append(pl.BlockSpec(memory_space=pl.ANY))
        args.append(s_stack)
    return pl.pallas_call(
        body, grid=grid, in_specs=in_specs,
        out_specs=[pl.BlockSpec((bt, width), lambda l, i: (row(l, i), 0)),
                   pl.BlockSpec((None, bt, heads, hd, ns), lambda l, i: (out_layer(l), i, 0, 0, 0))],
        out_shape=[jax.ShapeDtypeStruct((b, width), f32),
                   jax.ShapeDtypeStruct((DEPTH, b, heads, hd, ns), f32)],
        scratch_shapes=[pltpu.VMEM((bt, width), f32)],
        input_output_aliases={10: 1} if aliased else {},
        compiler_params=_params("arbitrary", "arbitrary"), name="ssd_sample",
    )(*args)


def _router_body(h_ref, g_ref, wr_ref, br_ref, cf_ref, idx_ref, gate_ref, *, n_exp):
    cf = _rms(h_ref[...], g_ref[...])
    cf_ref[...] = cf
    logits = jnp.dot(cf, wr_ref[...], precision=HI, preferred_element_type=f32) + br_ref[...]
    lane = lax.broadcasted_iota(jnp.int32, logits.shape, 1)
    m1 = jnp.max(logits, axis=1, keepdims=True)
    i1 = jnp.min(jnp.where(logits == m1, lane, n_exp), axis=1, keepdims=True)
    rest = jnp.where(lane == i1, -jnp.inf, logits)
    m2 = jnp.max(rest, axis=1, keepdims=True)
    i2 = jnp.min(jnp.where(rest == m2, lane, n_exp), axis=1, keepdims=True)
    e2 = jnp.exp(m2 - m1)
    g1 = 1.0 / (1.0 + e2)
    two = lax.broadcasted_iota(jnp.int32, (logits.shape[0], TOP_K), 1)
    idx_ref[...] = jnp.where(two == 0, i1, i2)
    gate_ref[...] = jnp.where(two == 0, g1, e2 * g1)


def _router(h, g_ffn, w_router, b_router, *, tm):
    m, d = h.shape
    e = w_router.shape[-1]
    row = lambda i: (i, 0)
    fix = lambda i: (0, 0)
    return pl.pallas_call(
        functools.partial(_router_body, n_exp=e), grid=(m // tm,),
        in_specs=[pl.BlockSpec((tm, d), row), pl.BlockSpec((1, d), fix), pl.BlockSpec((d, e), fix),
                  pl.BlockSpec((1, e), fix)],
        out_specs=[pl.BlockSpec((tm, d), row), pl.BlockSpec((tm, TOP_K), row),
                   pl.BlockSpec((tm, TOP_K), row)],
        out_shape=[jax.ShapeDtypeStruct((m, d), f32),
                   jax.ShapeDtypeStruct((m, TOP_K), jnp.int32), jax.ShapeDtypeStruct((m, TOP_K), f32)],
        compiler_params=_params("parallel"), name="router",
    )(h, g_ffn.reshape(1, d), w_router, b_router.reshape(1, e))


def _row_copy(src_hbm, dst, sem, src_row, dst_row):
    return pltpu.make_async_copy(src_hbm.at[pl.ds(src_row, 1)], dst.at[pl.ds(dst_row, 1)], sem)


_DMA_UNROLL = 8


def _gather_body(idx_ref, x_hbm, o_ref, buf, sem, *, tg):
    i = pl.program_id(0)

    def issue(tile, slot):
        base = tile * tg

        def start(r, carry):
            _row_copy(x_hbm, buf.at[slot], sem.at[slot], idx_ref[base + r], r).start()
            return carry

        lax.fori_loop(0, tg, start, 0, unroll=_DMA_UNROLL)

    @pl.when(i == 0)
    def _():
        issue(0, 0)

    @pl.when(i + 1 < pl.num_programs(0))
    def _():
        issue(i + 1, (i + 1) % 2)

    slot = i % 2
    pltpu.make_async_copy(x_hbm.at[pl.ds(0, tg)], buf.at[slot], sem.at[slot]).wait()
    o_ref[...] = buf[slot].astype(o_ref.dtype)


def _gather_rows(x, row_idx, *, tg, out_dtype):
    r = row_idx.shape[0]
    d = x.shape[1]
    return pl.pallas_call(
        functools.partial(_gather_body, tg=tg),
        grid_spec=pltpu.PrefetchScalarGridSpec(
            num_scalar_prefetch=1, grid=(r // tg,),
            in_specs=[pl.BlockSpec(memory_space=pl.ANY)],
            out_specs=pl.BlockSpec((tg, d), lambda i, idx: (i, 0)),
            scratch_shapes=[pltpu.VMEM((2, tg, d), x.dtype), pltpu.SemaphoreType.DMA((2,))]),
        out_shape=jax.ShapeDtypeStruct((r, d), out_dtype),
        compiler_params=_params("arbitrary"), name="gather_rows")(row_idx, x)


def _gmm_body(te_ref, first_ref, next_ref, nv_ref, a_ref, *rest, n_w, swiglu):
    w_hbm = rest[:n_w]
    o_ref, wf_ref, wb_ref, sem, run_ref = rest[n_w:]
    j = pl.program_id(0)
    t = pl.program_id(1)
    tn = o_ref.shape[1]

    def tile_copies(expert, col_tile, slot):
        col = pl.multiple_of(col_tile * tn, 128)
        return [pltpu.make_async_copy(w_hbm[i].at[expert, :, pl.ds(col, tn)], wf_ref.at[slot, i], sem.at[slot, i])
                for i in range(n_w)]

    @pl.when((j == 0) & (t == 0))
    def _():
        run_ref[0] = 0
        for c in tile_copies(te_ref[0], 0, 0):
            c.start()

    @pl.when((t < nv_ref[0]) & (first_ref[t] == 1))
    def _():
        slot = run_ref[0] % 2
        for c in tile_copies(te_ref[t], j, slot):
            c.wait()
        for i in range(n_w):
            wb_ref[i] = wf_ref[slot, i].astype(bf16)
        nxt = next_ref[t]

        @pl.when(nxt >= 0)
        def _():
            for c in tile_copies(te_ref[nxt], j, 1 - slot):
                c.start()

        @pl.when((nxt < 0) & (j + 1 < pl.num_programs(0)))
        def _():
            for c in tile_copies(te_ref[0], j + 1, 1 - slot):
                c.start()

        run_ref[0] = run_ref[0] + 1

    @pl.when(t < nv_ref[0])
    def _():
        a = a_ref[...]
        if swiglu:
            g = jnp.dot(a, wb_ref[0], preferred_element_type=f32)
            u = jnp.dot(a, wb_ref[1], preferred_element_type=f32)
            o_ref[...] = (g * jax.nn.sigmoid(g) * u).astype(o_ref.dtype)
        else:
            o_ref[...] = jnp.dot(a, wb_ref[0], preferred_element_type=f32).astype(o_ref.dtype)

    @pl.when(t >= nv_ref[0])
    def _():
        o_ref[...] = jnp.zeros_like(o_ref)


def _gmm(a_sorted, weights, tables, *, tm, tn, swiglu, out_dtype, name):
    tile_expert, tile_first, tile_next, n_valid = tables
    r, k = a_sorted.shape
    n = weights[0].shape[-1]
    n_w = len(weights)
    return pl.pallas_call(
        functools.partial(_gmm_body, n_w=n_w, swiglu=swiglu),
        grid_spec=pltpu.PrefetchScalarGridSpec(
            num_scalar_prefetch=4, grid=(n // tn, r // tm),
            in_specs=[pl.BlockSpec((tm, k), lambda j, t, *_: (t, 0))] + [pl.BlockSpec(memory_space=pl.ANY)] * n_w,
            out_specs=pl.BlockSpec((tm, tn), lambda j, t, *_: (t, j)),
            scratch_shapes=[pltpu.VMEM((2, n_w, k, tn), f32), pltpu.VMEM((n_w, k, tn), bf16),
                            pltpu.SemaphoreType.DMA((2, n_w)), pltpu.SMEM((1,), jnp.int32)]),
        out_shape=jax.ShapeDtypeStruct((r, n), out_dtype),
        compiler_params=_params("arbitrary", "arbitrary"), name=name,
    )(tile_expert, tile_first, tile_next, n_valid, a_sorted, *weights)


def _combine_body(pos_ref, y_hbm, gate_ref, r_ref, gn_ref, o_ref, e_ref, buf, sem, *, tc):
    i = pl.program_id(0)

    def issue(tile, slot):
        base = tile * tc

        def start(r, carry):
            for kk in range(TOP_K):
                _row_copy(y_hbm, buf.at[slot, kk], sem.at[slot], pos_ref[(base + r) * TOP_K + kk], r).start()
            return carry

        lax.fori_loop(0, tc, start, 0, unroll=_DMA_UNROLL)

    @pl.when(i == 0)
    def _():
        issue(0, 0)

    @pl.when(i + 1 < pl.num_programs(0))
    def _():
        issue(i + 1, (i + 1) % 2)

    slot = i % 2
    for kk in range(TOP_K):
        pltpu.make_async_copy(y_hbm.at[pl.ds(0, tc)], buf.at[slot, kk], sem.at[slot]).wait()
    gate = gate_ref[...]
    out = r_ref[...] + gate[:, 0:1] * buf[slot, 0] + gate[:, 1:2] * buf[slot, 1]
    o_ref[...] = out
    e_ref[...] = _rms(out, gn_ref[...]).astype(e_ref.dtype)


def _combine(y_sorted, pos_flat, gates, resid, g_next, *, tc):
    m, d = resid.shape
    rows = lambda i, pos: (i, 0)
    return pl.pallas_call(
        functools.partial(_combine_body, tc=tc),
        grid_spec=pltpu.PrefetchScalarGridSpec(
            num_scalar_prefetch=1, grid=(m // tc,),
            in_specs=[pl.BlockSpec(memory_space=pl.ANY), pl.BlockSpec((tc, TOP_K), rows),
                      pl.BlockSpec((tc, d), rows), pl.BlockSpec((1, d), lambda i, pos: (0, 0))],
            out_specs=[pl.BlockSpec((tc, d), rows), pl.BlockSpec((tc, d), rows)],
            scratch_shapes=[pltpu.VMEM((2, TOP_K, tc, d), f32), pltpu.SemaphoreType.DMA((2,))]),
        out_shape=[jax.ShapeDtypeStruct((m, d), f32), jax.ShapeDtypeStruct((m, d), bf16)],
        compiler_params=_params("arbitrary"), name="moe_combine",
    )(pos_flat, y_sorted, gates, resid, g_next.reshape(1, d))


def _routing_tables(top_i, n_exp, tm, n_tiles):
    m = top_i.shape[0]
    e_flat = top_i.reshape(-1)
    onehot = (e_flat[:, None] == jnp.arange(n_exp, dtype=jnp.int32)[None, :]).astype(jnp.int32)
    rank = jnp.take_along_axis(jnp.cumsum(onehot, axis=0), e_flat[:, None], axis=1)[:, 0] - 1
    counts = jnp.sum(onehot, axis=0)
    tiles_per = (counts + tm - 1) // tm
    tile_end = jnp.cumsum(tiles_per)
    tile_start = tile_end - tiles_per
    pos = tile_start[e_flat] * tm + rank
    token = jnp.arange(m * TOP_K, dtype=jnp.int32) // TOP_K
    row_token = jnp.zeros((n_tiles * tm,), jnp.int32).at[pos].set(token)
    n_valid = tile_end[-1]
    tid = jnp.minimum(jnp.arange(n_tiles, dtype=jnp.int32), n_valid - 1)
    tile_expert = jnp.sum((tid[:, None] >= tile_end[None, :]).astype(jnp.int32), axis=1)
    tile_first = jnp.concatenate([jnp.ones((1,), jnp.int32),
                                  (tile_expert[1:] != tile_expert[:-1]).astype(jnp.int32)])
    ids = jnp.arange(n_tiles, dtype=jnp.int32)
    starts = jnp.where((tile_first == 1) & (ids < n_valid), ids, n_tiles)
    later = jnp.concatenate([lax.cummin(starts, reverse=True)[1:], jnp.full((1,), n_tiles, jnp.int32)])
    tile_next = jnp.where(later >= n_tiles, -1, later).astype(jnp.int32)
    return (pos.astype(jnp.int32), row_token,
            (tile_expert.astype(jnp.int32), tile_first, tile_next, n_valid.reshape(1).astype(jnp.int32)))


def _moe_ffn(h, g_ffn, w_router, b_router, wg, wu, wd, g_next, *, tm_tok, tm, tn_up, tn_down):
    m = h.shape[0]
    n_exp = wg.shape[0]
    c_f32, top_i, top_g = _router(h, g_ffn, w_router, b_router, tm=tm_tok)
    n_tiles = (m * TOP_K) // tm + n_exp
    pos, row_token, tables = _routing_tables(top_i, n_exp, tm, n_tiles)
    x_sorted = _gather_rows(c_f32, row_token, tg=tm, out_dtype=bf16)
    h_sorted = _gmm(x_sorted, (wg, wu), tables, tm=tm, tn=tn_up, swiglu=True, out_dtype=bf16, name="moe_gate_up")
    y_sorted = _gmm(h_sorted, (wd,), tables, tm=tm, tn=tn_down, swiglu=False, out_dtype=f32, name="moe_down")
    return _combine(y_sorted, pos, top_g, h, g_next, tc=tm_tok // 2)


def kernel(x_prompt, x_sample, state_mlstm_C, state_mlstm_n, state_mlstm_m, state_s5_re, state_s5_im, state_ssd, cache_conv, p_prompt, p_sample, g_mix, w_in, b_igate, b_fgate, g_ml, s5_lam_re, s5_lam_im, s5_log_dt, s5_b_re, s5_b_im, s5_c_re, s5_c_im, s5_d, s5_w_glu, s5_b_glu, g_s5, ssd_conv_w, ssd_conv_b, ssd_dt_bias, ssd_a_log, ssd_d, g_ssd, w_out, g_ffn, ffn_w_gate, ffn_w_up, ffn_w_down, w_router, b_router, moe_w_gate, moe_w_up, moe_w_down, g_ple, w_ple, w_ple_gate, g_final):
    bp, seq, d = x_prompt.shape
    bs = x_sample.shape[0]
    tp = bp * seq
    m = tp + bs
    tm = _token_tile(tp, bs)
    heads = ML_HEADS
    t5 = math.gcd(seq, S5_CHUNK)
    nc5 = seq // t5
    cl = math.gcd(seq, CHUNK)
    nc = seq // cl

    p_p = p_prompt.reshape(DEPTH, tp, -1)
    p_s = p_sample.reshape(DEPTH, bs, -1)
    w_in_t = jnp.swapaxes(w_in, 1, 2)
    outs_p = [[] for _ in range(7)]
    outs_s = [[] for _ in range(5)]
    c_stack = s_stack = None
    mats_p, mats_s = jax.vmap(functools.partial(_s5_matrices, t=t5))(
        s5_lam_re, s5_lam_im, s5_log_dt, s5_b_re, s5_b_im, s5_c_re, s5_c_im)

    for i in range(DEPTH):
        if i == 0:
            h, a = _rmsnorm_in(x_prompt.reshape(tp, d), x_sample.reshape(bs, d), g_mix[i], tm)
        else:
            a = _rmsnorm(h, g_mix[i], bf16, tm)
        qkvo = _mm_nt(a, w_in_t, tm=tm, tn=1024, layer=i, row_off=0, n_rows=_OFF_GATES, name="in_proj_qkvo")
        uzx = _mm_nt(a, w_in_t, tm=tm, tn=1024, layer=i, row_off=_OFF_U, n_rows=_OFF_DT - _OFF_U, name="in_proj_uzx")
        gd = _mm_gates(a, w_in_t, tm=tm, layer=i, off1=_OFF_GATES, n1=2 * heads, off2=_OFF_DT, n2=SSD_HEADS,
                       name="in_proj_gates")

        gates_t = jnp.transpose(gd[:tp, :2 * heads].reshape(bp, nc, cl, 2 * heads), (0, 1, 3, 2))
        bias8 = jnp.concatenate([b_igate[i], b_fgate[i]])
        h_ml_p, c_p, n_p, m_p = _mlstm_prompt(qkvo, gates_t, bias8.reshape(2 * heads, 1), g_ml[i], batch=bp, seq=seq)
        h_ml_s, c_stack, n_s, m_s = _mlstm_sample(qkvo, gd, bias8.reshape(1, 2 * heads), g_ml[i], state_mlstm_C,
                                                  state_mlstm_n, state_mlstm_m, layer=i, row0=tp, c_stack=c_stack)

        y_p, x5_p = _s5_prompt(uzx, mats_p, batch=bp, seq=seq, layer=i)
        s5re_p, s5im_p = x5_p[..., :S5_STATE], x5_p[..., S5_STATE:]
        u_s = jnp.transpose(uzx[tp:, :S5_WIDTH].reshape(bs, S5_GROUPS, S5_CH), (1, 0, 2)).astype(bf16)
        x0_s = jnp.swapaxes(jnp.concatenate([state_s5_re[i], state_s5_im[i]], axis=-1), 0, 1)
        y_s, x5_s = _s5_scan(u_s, mats_s, x0_s, nc=1, batch=bs, layer=i)
        s5re_s, s5im_s = x5_s[..., :S5_STATE], x5_s[..., S5_STATE:]
        y_s = jnp.transpose(y_s, (1, 0, 2)).reshape(bs, S5_WIDTH)
        y5 = _s5_glu(y_p, y_s, uzx, s5_d[i].reshape(1, S5_WIDTH), s5_w_glu,
                     s5_b_glu[i].reshape(1, S5_WIDTH), g_s5[i].reshape(1, S5_WIDTH), layer=i, tm=tm)

        dt_t = jnp.transpose(gd[:tp, 2 * heads:2 * heads + SSD_HEADS].reshape(bp, nc, cl, SSD_HEADS), (0, 1, 3, 2))
        y_ssd_p, ssd_p = _ssd_prompt(uzx, gd, dt_t, ssd_conv_w[i], ssd_conv_b[i], ssd_dt_bias[i], ssd_a_log[i],
                                     ssd_d[i], g_ssd[i], batch=bp, seq=seq)
        y_ssd_s, s_stack = _ssd_sample(uzx, cache_conv, gd, ssd_conv_w[i], ssd_conv_b[i], ssd_dt_bias[i],
                                       ssd_a_log[i], ssd_d[i], g_ssd[i], state_ssd, layer=i, row0=tp,
                                       s_stack=s_stack)
        xbc = uzx[:, S5_WIDTH + SSD_WIDTH:]
        conv_p = xbc[:tp].reshape(bp, seq, -1)[:, seq - (SSD_CONV - 1):]
        conv_s = jnp.concatenate([cache_conv[i][:, 1:], xbc[tp:].reshape(bs, 1, -1)], axis=1)

        for lst, s in zip(outs_p, (c_p, n_p.reshape(bp, heads, ML_DK), m_p.reshape(bp, heads),
                                   jnp.swapaxes(s5re_p, 0, 1), jnp.swapaxes(s5im_p, 0, 1),
                                   ssd_p.reshape(bp, SSD_HEADS, SSD_HEAD_DIM, SSD_STATE), conv_p)):
            lst.append(s)
        for lst, s in zip(outs_s, (n_s, m_s, jnp.swapaxes(s5re_s, 0, 1), jnp.swapaxes(s5im_s, 0, 1), conv_s)):
            lst.append(s)

        h = _mm_mix(h_ml_p.reshape(tp, -1), h_ml_s, y5, y_ssd_p.reshape(tp, -1), y_ssd_s, w_out, h, tm=tm, tn=1024,
                    layer=i, name="out_proj")

        j = i // 2
        if i % 2 == 0:
            cn = _rmsnorm(h, g_ffn[i], bf16, tm)
            hid = _mm_swiglu(cn, ffn_w_gate, ffn_w_up, tm=tm, tn=512, layer=j, name="ffn_gate_up")
            h = _mm(hid, ffn_w_down, tm=tm // 2, tn=512, layer=j, resid=h, name="ffn_down")
            e = _rmsnorm(h, g_ple[i], bf16, tm)
        else:
            n_moe = moe_w_gate.shape[0]
            sel = lambda w: w.reshape((n_moe * N_EXPERTS,) + w.shape[2:])[j * N_EXPERTS:(j + 1) * N_EXPERTS] if n_moe > 1 else w.reshape(w.shape[1:])
            h, e = _moe_ffn(h, g_ffn[i], w_router[j], b_router[j], sel(moe_w_gate), sel(moe_w_up), sel(moe_w_down),
                            g_ple[i], tm_tok=tm, tm=256, tn_up=1024, tn_down=512)

        h = _mm_ple(e, p_p, p_s, w_ple_gate, w_ple, h, tm=tm, tn=1024, layer=i)

    y_p, y_s = _rmsnorm_out(h, g_final, tp, tm)
    n_s, m_s, s5re_s, s5im_s, conv_s = (jnp.stack(l) for l in outs_s)
    return ((y_p.reshape(bp, seq, d), y_s.reshape(bs, 1, d)) + tuple(jnp.stack(l) for l in outs_p)
            + (c_stack, n_s, m_s, s5re_s, s5im_s, s_stack, conv_s))
```

```python
import functools
import math

import jax
import jax.numpy as jnp
from jax import lax
from jax.experimental import pallas as pl
from jax.experimental.pallas import tpu as pltpu

f32 = jnp.float32
bf16 = jnp.bfloat16
HI = lax.Precision.HIGHEST

DEPTH = 2
ML_HEADS = 4
ML_DK = 256
ML_DV = 256
ML_WIDTH = ML_HEADS * ML_DV
S5_CH = 16
S5_WIDTH = 512
S5_GROUPS = 32
S5_STATE = 64
SSD_HEAD_DIM = 64
SSD_WIDTH = 512
SSD_HEADS = 8
SSD_GROUPS = 2
SSD_STATE = 128
SSD_CONV = 4
SSD_CONV_CH = 1024
CHUNK = 64
S5_CHUNK = 16
N_EXPERTS = 8
TOP_K = 2
RMS_EPS = 1e-6

_OFF_GATES = 4 * ML_WIDTH
_OFF_U = _OFF_GATES + 2 * ML_HEADS
_OFF_DT = _OFF_U + S5_WIDTH + SSD_WIDTH + SSD_CONV_CH

_VMEM_LIMIT = 56 * 1024 * 1024
_VMEM_LIMIT_GMM = 60 * 1024 * 1024
_NT = (((1,), (1,)), ((), ()))
_TN = (((0,), (0,)), ((), ()))


def _params(*sem, vmem=_VMEM_LIMIT):
    return pltpu.CompilerParams(dimension_semantics=sem, vmem_limit_bytes=vmem)


def _rms(x, g):
    return x * lax.rsqrt(jnp.mean(x * x, axis=-1, keepdims=True) + RMS_EPS) * g


def _rmsnorm_body(x_ref, g_ref, o_ref):
    o_ref[...] = _rms(x_ref[...], g_ref[...]).astype(o_ref.dtype)


def _last_tile(p_tile, s_rows, n_keep):
    return jnp.concatenate([p_tile[0:n_keep, :], s_rows], axis=0)


def _split_rows(tp, bs, tm):
    n_tiles = (tp + bs) // tm
    n_keep = tp - (n_tiles - 1) * tm
    assert n_tiles * tm == tp + bs and 0 < n_keep and n_keep + bs == tm and n_keep % 16 == 0
    return n_tiles, n_keep


def _token_tile(tp, bs):
    m = tp + bs
    for n_tiles in (10, 8, 5, 4, 2, 1):
        tm = m // n_tiles
        if m % n_tiles == 0 and tm % 32 == 0 and bs < tm and (tp - (n_tiles - 1) * tm) % 16 == 0:
            return tm
    raise ValueError("no row tiling for these token counts")


def _rmsnorm_in_body(xp_ref, xs_ref, g_ref, h_ref, a_ref, *, n_keep):
    last = pl.num_programs(0) - 1

    def run(x):
        h_ref[...] = x
        a_ref[...] = _rms(x, g_ref[...]).astype(a_ref.dtype)

    @pl.when(pl.program_id(0) < last)
    def _():
        run(xp_ref[...])

    @pl.when(pl.program_id(0) == last)
    def _():
        run(_last_tile(xp_ref, xs_ref[...], n_keep))


def _rmsnorm_in(xp, xs, g, tm):
    tp, d = xp.shape
    bs = xs.shape[0]
    n_tiles, n_keep = _split_rows(tp, bs, tm)
    row = lambda i: (i, 0)
    fix = lambda i: (0, 0)
    return pl.pallas_call(
        functools.partial(_rmsnorm_in_body, n_keep=n_keep), grid=(n_tiles,),
        in_specs=[pl.BlockSpec((tm, d), row), pl.BlockSpec((bs, d), fix), pl.BlockSpec((1, d), fix)],
        out_specs=[pl.BlockSpec((tm, d), row), pl.BlockSpec((tm, d), row)],
        out_shape=[jax.ShapeDtypeStruct((tp + bs, d), f32), jax.ShapeDtypeStruct((tp + bs, d), bf16)],
        compiler_params=_params("parallel"), name="rmsnorm_in")(xp, xs, g.reshape(1, d))


def _rmsnorm_out_body(x_ref, g_ref, yp_ref, ys_ref, *, n_keep):
    y = _rms(x_ref[...], g_ref[...])
    yp_ref[...] = y

    @pl.when(pl.program_id(0) == pl.num_programs(0) - 1)
    def _():
        ys_ref[...] = y[n_keep:, :]


def _rmsnorm_out(x, g, tp, tm):
    m, d = x.shape
    bs = m - tp
    n_tiles, n_keep = _split_rows(tp, bs, tm)
    return pl.pallas_call(
        functools.partial(_rmsnorm_out_body, n_keep=n_keep), grid=(n_tiles,),
        in_specs=[pl.BlockSpec((tm, d), lambda i: (i, 0)), pl.BlockSpec((1, d), lambda i: (0, 0))],
        out_specs=[pl.BlockSpec((tm, d), lambda i: (i, 0)), pl.BlockSpec((bs, d), lambda i: (0, 0))],
        out_shape=[jax.ShapeDtypeStruct((tp, d), f32), jax.ShapeDtypeStruct((bs, d), f32)],
        compiler_params=_params("arbitrary"), name="rmsnorm_out")(x, g.reshape(1, d))


def _rmsnorm(x, g, out_dtype, tm):
    m, d = x.shape
    return pl.pallas_call(
        _rmsnorm_body, grid=(m // tm,),
        in_specs=[pl.BlockSpec((tm, d), lambda i: (i, 0)), pl.BlockSpec((1, d), lambda i: (0, 0))],
        out_specs=pl.BlockSpec((tm, d), lambda i: (i, 0)),
        out_shape=jax.ShapeDtypeStruct((m, d), out_dtype),
        compiler_params=_params("parallel"), name="rmsnorm")(x, g.reshape(1, d))


def _cast_weight_once(w_ref, wb_ref):
    @pl.when(pl.program_id(1) == 0)
    def _():
        wb_ref[...] = w_ref[...].astype(bf16)


def _mm_plain_body(a_ref, w_ref, o_ref, wb_ref):
    _cast_weight_once(w_ref, wb_ref)
    o_ref[...] = jnp.dot(a_ref[...], wb_ref[...], preferred_element_type=f32).astype(o_ref.dtype)


def _mm_resid_body(a_ref, w_ref, r_ref, o_ref, wb_ref):
    _cast_weight_once(w_ref, wb_ref)
    o_ref[...] = r_ref[...] + jnp.dot(a_ref[...], wb_ref[...], preferred_element_type=f32)


def _mm_swiglu_body(a_ref, wg_ref, wu_ref, o_ref, wgb_ref, wub_ref):
    _cast_weight_once(wg_ref, wgb_ref)
    _cast_weight_once(wu_ref, wub_ref)
    a = a_ref[...]
    g = jnp.dot(a, wgb_ref[...], preferred_element_type=f32)
    u = jnp.dot(a, wub_ref[...], preferred_element_type=f32)
    o_ref[...] = (g * jax.nn.sigmoid(g) * u).astype(o_ref.dtype)


def _mm_ple_body(e_ref, pp_ref, ps_ref, wg_ref, wp_ref, r_ref, o_ref, wgb_ref, wpb_ref, *, n_keep):
    _cast_weight_once(wg_ref, wgb_ref)
    _cast_weight_once(wp_ref, wpb_ref)
    last = pl.num_programs(1) - 1

    def run(p):
        gate = jnp.dot(e_ref[...], wgb_ref[...], preferred_element_type=f32)
        emb = jnp.dot(p.astype(bf16), wpb_ref[...], preferred_element_type=f32)
        o_ref[...] = r_ref[...] + emb * jax.nn.sigmoid(gate)

    @pl.when(pl.program_id(1) < last)
    def _():
        run(pp_ref[...])

    @pl.when(pl.program_id(1) == last)
    def _():
        run(_last_tile(pp_ref, ps_ref[...], n_keep))


def _mm_mix_body(a1p_ref, a1s_ref, a2_ref, a3p_ref, a3s_ref, w_ref, r_ref, o_ref, wb_ref, *, n_keep):
    _cast_weight_once(w_ref, wb_ref)
    last = pl.num_programs(1) - 1
    k1 = a1p_ref.shape[1]
    k2 = a2_ref.shape[1]

    def run(a1, a3):
        acc = jnp.dot(a1, wb_ref[0:k1, :], preferred_element_type=f32)
        acc += jnp.dot(a2_ref[...], wb_ref[k1:k1 + k2, :], preferred_element_type=f32)
        acc += jnp.dot(a3, wb_ref[k1 + k2:, :], preferred_element_type=f32)
        o_ref[...] = r_ref[...] + acc

    @pl.when(pl.program_id(1) < last)
    def _():
        run(a1p_ref[...], a3p_ref[...])

    @pl.when(pl.program_id(1) == last)
    def _():
        run(_last_tile(a1p_ref, a1s_ref[...].astype(bf16), n_keep),
            _last_tile(a3p_ref, a3s_ref[...].astype(bf16), n_keep))


def _mm_mix(a1p, a1s, a2, a3p, a3s, w, resid, *, tm, tn, layer, name):
    m, n = resid.shape
    tp, k1 = a1p.shape
    bs = a1s.shape[0]
    k2, k3 = a2.shape[1], a3p.shape[1]
    n_tiles, n_keep = _split_rows(tp, bs, tm)
    rows = lambda j, i: (i, 0)
    fix = lambda j, i: (0, 0)
    o_spec = pl.BlockSpec((tm, tn), lambda j, i: (i, j))
    return pl.pallas_call(
        functools.partial(_mm_mix_body, n_keep=n_keep), grid=(n // tn, n_tiles),
        in_specs=[pl.BlockSpec((tm, k1), rows), pl.BlockSpec((bs, k1), fix), pl.BlockSpec((tm, k2), rows),
                  pl.BlockSpec((tm, k3), rows), pl.BlockSpec((bs, k3), fix),
                  _wspec(w, layer, k1 + k2 + k3, tn), o_spec],
        out_specs=o_spec, out_shape=jax.ShapeDtypeStruct((m, n), f32),
        scratch_shapes=[pltpu.VMEM((k1 + k2 + k3, tn), bf16)],
        compiler_params=_params("arbitrary", "arbitrary"), name=name)(a1p, a1s, a2, a3p, a3s, w, resid)


def _wspec(w, layer, k, tn, col_block_off=0):
    if w.ndim == 2:
        return pl.BlockSpec((k, tn), lambda j, i: (0, j + col_block_off))
    return pl.BlockSpec((None, k, tn), lambda j, i: (layer, 0, j + col_block_off))


def _mm(a, w, *, tm, tn, layer=0, n_cols=None, col_off=0, resid=None, out_dtype=f32, name="mm"):
    m, k = a.shape
    n = n_cols if n_cols is not None else w.shape[-1]
    grid = (n // tn, m // tm)
    a_spec = pl.BlockSpec((tm, k), lambda j, i: (i, 0))
    o_spec = pl.BlockSpec((tm, tn), lambda j, i: (i, j))
    w_spec = _wspec(w, layer, k, tn, col_off // tn)
    scratch = [pltpu.VMEM((k, tn), bf16)]
    if resid is None:
        body, ins, specs = _mm_plain_body, (a, w), [a_spec, w_spec]
    else:
        body, ins, specs = _mm_resid_body, (a, w, resid), [a_spec, w_spec, o_spec]
    return pl.pallas_call(
        body, grid=grid, in_specs=specs, out_specs=o_spec,
        out_shape=jax.ShapeDtypeStruct((m, n), out_dtype), scratch_shapes=scratch,
        compiler_params=_params("arbitrary", "arbitrary"), name=name)(*ins)


def _mm_nt_body(a_ref, wt_ref, o_ref, wb_ref):
    @pl.when(pl.program_id(1) == 0)
    def _():
        wb_ref[...] = wt_ref[0].T.astype(bf16)

    o_ref[...] = jnp.dot(a_ref[...], wb_ref[...], preferred_element_type=f32)


def _wt_rows(layer, k, n_rows, row_of):
    return pl.BlockSpec((pl.Element(1), pl.Element(n_rows), pl.Element(k)),
                        lambda *idx: (layer, row_of(*idx), 0))


def _mm_nt(a, wt, *, tm, tn, layer, row_off, n_rows, name):
    m, k = a.shape
    return pl.pallas_call(
        _mm_nt_body, grid=(n_rows // tn, m // tm),
        in_specs=[pl.BlockSpec((tm, k), lambda j, i: (i, 0)),
                  _wt_rows(layer, k, tn, lambda j, i: pl.multiple_of(row_off + tn * j, 8))],
        out_specs=pl.BlockSpec((tm, tn), lambda j, i: (i, j)),
        out_shape=jax.ShapeDtypeStruct((m, n_rows), f32), scratch_shapes=[pltpu.VMEM((k, tn), bf16)],
        compiler_params=_params("arbitrary", "arbitrary"), name=name)(a, wt)


def _mm_gates_body(a_ref, w1_ref, w2_ref, o_ref):
    a = a_ref[...]
    n1 = w1_ref.shape[1]
    o_ref[:, 0:n1] = lax.dot_general(a, w1_ref[0].astype(bf16), _NT, preferred_element_type=f32)
    o_ref[:, n1:] = lax.dot_general(a, w2_ref[0].astype(bf16), _NT, preferred_element_type=f32)


def _mm_gates(a, wt, *, tm, layer, off1, n1, off2, n2, name):
    m, k = a.shape
    return pl.pallas_call(
        _mm_gates_body, grid=(m // tm,),
        in_specs=[pl.BlockSpec((tm, k), lambda i: (i, 0)),
                  _wt_rows(layer, k, n1, lambda i: off1), _wt_rows(layer, k, n2, lambda i: off2)],
        out_specs=pl.BlockSpec((tm, n1 + n2), lambda i: (i, 0)),
        out_shape=jax.ShapeDtypeStruct((m, n1 + n2), f32),
        compiler_params=_params("parallel"), name=name)(a, wt, wt)


def _mm_swiglu(a, wg, wu, *, tm, tn, layer=0, name="mm_swiglu"):
    m, k = a.shape
    n = wg.shape[-1]
    a_spec = pl.BlockSpec((tm, k), lambda j, i: (i, 0))
    o_spec = pl.BlockSpec((tm, tn), lambda j, i: (i, j))
    return pl.pallas_call(
        _mm_swiglu_body, grid=(n // tn, m // tm),
        in_specs=[a_spec, _wspec(wg, layer, k, tn), _wspec(wu, layer, k, tn)], out_specs=o_spec,
        out_shape=jax.ShapeDtypeStruct((m, n), bf16),
        scratch_shapes=[pltpu.VMEM((k, tn), bf16), pltpu.VMEM((k, tn), bf16)],
        compiler_params=_params("arbitrary", "arbitrary"), name=name)(a, wg, wu)


def _mm_ple(e, p_prompt, p_sample, w_gate, w_ple, resid, *, tm, tn, layer, name="mm_ple"):
    m, k = e.shape
    _, tp, kp = p_prompt.shape
    bs = p_sample.shape[1]
    n = w_gate.shape[-1]
    n_tiles, n_keep = _split_rows(tp, bs, tm)
    o_spec = pl.BlockSpec((tm, tn), lambda j, i: (i, j))
    return pl.pallas_call(
        functools.partial(_mm_ple_body, n_keep=n_keep), grid=(n // tn, n_tiles),
        in_specs=[pl.BlockSpec((tm, k), lambda j, i: (i, 0)),
                  pl.BlockSpec((None, tm, kp), lambda j, i: (layer, i, 0)),
                  pl.BlockSpec((None, bs, kp), lambda j, i: (layer, 0, 0)),
                  _wspec(w_gate, layer, k, tn), _wspec(w_ple, layer, kp, tn), o_spec],
        out_specs=o_spec, out_shape=jax.ShapeDtypeStruct((m, n), f32),
        scratch_shapes=[pltpu.VMEM((k, tn), bf16), pltpu.VMEM((kp, tn), bf16)],
        compiler_params=_params("arbitrary", "arbitrary"), name=name)(e, p_prompt, p_sample, w_gate, w_ple, resid)


def _col_from_row(row, eye):
    n = row.shape[1]
    return jnp.sum(jnp.where(eye, jnp.broadcast_to(row, (n, n)), 0.0), axis=1, keepdims=True)


def _mlstm_prompt_body(*refs, heads, dk, dv, cl, nb):
    q_refs, k_refs, v_refs, o_refs = (refs[i * nb:(i + 1) * nb] for i in range(4))
    gt_ref, bias_ref, gml_ref, h_ref, c_out, n_out, m_out, c_s, n_s, m_s = refs[4 * nb:]
    c = pl.program_id(1)

    @pl.when(c == 0)
    def _():
        c_s[...] = jnp.zeros_like(c_s)
        n_s[...] = jnp.zeros_like(n_s)
        m_s[...] = jnp.zeros_like(m_s)

    row = lax.broadcasted_iota(jnp.int32, (cl, cl), 0)
    col = lax.broadcasted_iota(jnp.int32, (cl, cl), 1)
    causal = col <= row
    eye = col == row
    triu = (row <= col).astype(f32)
    for r in range(nb):
        gt = gt_ref[r, 0] + bias_ref[...]
        lf = jax.nn.log_sigmoid(gt[heads:2 * heads])
        b_all = jnp.dot(lf, triu, precision=HI, preferred_element_type=f32)
        for hh in range(heads):
            st = r * heads + hh
            ig = gt[hh:hh + 1]
            b_row = b_all[hh:hh + 1]
            b_col = _col_from_row(b_row, eye)
            m_prev = m_s[st]
            d = jnp.where(causal, b_col - b_row + ig, -jnp.inf)
            inter = b_col + m_prev
            m_t = jnp.maximum(inter, jnp.max(d, axis=1, keepdims=True))
            w = jnp.exp(d - m_t)
            g = jnp.exp(inter - m_t)
            q = q_refs[r][:, hh * dk:(hh + 1) * dk]
            k = k_refs[r][:, hh * dk:(hh + 1) * dk] * (dk ** -0.5)
            vb = v_refs[r][:, hh * dv:(hh + 1) * dv].astype(bf16)
            qb = q.astype(bf16)
            cmat = c_s[st]
            n_row = n_s[st]
            s = lax.dot_general(qb, k.astype(bf16), _NT, preferred_element_type=f32) * w
            num = (jnp.dot(s.astype(bf16), vb, preferred_element_type=f32)
                   + g * jnp.dot(qb, cmat.astype(bf16), preferred_element_type=f32))
            den = jnp.sum(s, axis=1, keepdims=True) + g * jnp.sum(q * n_row, axis=1, keepdims=True)
            hraw = num / jnp.maximum(jnp.abs(den), jnp.exp(-m_t))
            hn = _rms(hraw, gml_ref[hh:hh + 1, :])
            ogate = jax.nn.sigmoid(o_refs[r][:, hh * dv:(hh + 1) * dv])
            h_ref[r, :, hh * dv:(hh + 1) * dv] = (ogate * hn).astype(h_ref.dtype)
            b_last = b_row[:, cl - 1:cl]
            dl = b_last - b_row + ig
            m_new = jnp.maximum(b_last + m_prev, jnp.max(dl, axis=1, keepdims=True))
            ws_col = _col_from_row(jnp.exp(dl - m_new), eye)
            gl = jnp.exp(b_last + m_prev - m_new)
            kw = k * ws_col
            c_s[st] = gl * cmat + lax.dot_general(kw.astype(bf16), vb, _TN, preferred_element_type=f32)
            n_s[st] = gl * n_row + jnp.sum(kw, axis=0, keepdims=True)
            m_s[st] = m_new

    @pl.when(c == pl.num_programs(1) - 1)
    def _():
        for r in range(nb):
            c_out[r] = c_s[r * heads:(r + 1) * heads]
            n_out[r] = n_s[r * heads:(r + 1) * heads]
            m_out[r] = m_s[r * heads:(r + 1) * heads]


_PROMPT_SEQS_PER_STEP = 4


def _mlstm_prompt(qkvo, gates_t, bias_col, g_ml, *, batch, seq):
    heads, dk, dv, cl = ML_HEADS, ML_DK, ML_DV, math.gcd(seq, CHUNK)
    nc = seq // cl
    wq = heads * dk
    nb = math.gcd(batch, _PROMPT_SEQS_PER_STEP)
    body = functools.partial(_mlstm_prompt_body, heads=heads, dk=dk, dv=dv, cl=cl, nb=nb)
    qkvo_specs = [pl.BlockSpec((cl, wq), lambda b, c, r=r, part=part: ((b * nb + r) * nc + c, part))
                  for part in range(4) for r in range(nb)]
    return pl.pallas_call(
        body, grid=(batch // nb, nc),
        in_specs=qkvo_specs + [pl.BlockSpec((nb, 1, 2 * heads, cl), lambda b, c: (b, c, 0, 0)),
                               pl.BlockSpec((2 * heads, 1), lambda b, c: (0, 0)),
                               pl.BlockSpec((heads, dv), lambda b, c: (0, 0))],
        out_specs=[pl.BlockSpec((nb, cl, heads * dv), lambda b, c: (b, c, 0)),
                   pl.BlockSpec((nb, heads, dk, dv), lambda b, c: (b, 0, 0, 0)),
                   pl.BlockSpec((nb, heads, 1, dk), lambda b, c: (b, 0, 0, 0)),
                   pl.BlockSpec((nb, heads, 1, 1), lambda b, c: (b, 0, 0, 0))],
        out_shape=[jax.ShapeDtypeStruct((batch, seq, heads * dv), bf16),
                   jax.ShapeDtypeStruct((batch, heads, dk, dv), f32),
                   jax.ShapeDtypeStruct((batch, heads, 1, dk), f32),
                   jax.ShapeDtypeStruct((batch, heads, 1, 1), f32)],
        scratch_shapes=[pltpu.VMEM((nb * heads, dk, dv), f32), pltpu.VMEM((nb * heads, 1, dk), f32),
                        pltpu.VMEM((nb * heads, 1, 1), f32)],
        compiler_params=_params("arbitrary", "arbitrary"), name="mlstm_prompt",
    )(*([qkvo] * (4 * nb)), gates_t, bias_col, g_ml)


_SAMPLE_BLOCK = 8


def _mlstm_sample_body(*refs, heads, dk, dv, aliased):
    if aliased:
        refs = refs[:10] + refs[11:]
    (q_ref, k_ref, v_ref, o_ref, gd_ref, bias_ref, gml_ref, c0_ref, n0_ref, m0_ref,
     h_ref, c_out, n_out, m_out) = refs
    bt = q_ref.shape[0]

    @pl.when(pl.program_id(0) > 0)
    def _():
        c_out[...] = jnp.zeros_like(c_out)

    @pl.when(pl.program_id(0) == 0)
    def _():
        g8 = gd_ref[:, 0:2 * heads] + bias_ref[...]
        lane = lax.broadcasted_iota(jnp.int32, (bt, heads), 1)
        m_new = jnp.zeros((bt, heads), f32)
        for hh in range(heads):
            ig = g8[:, hh:hh + 1]
            lf = jax.nn.log_sigmoid(g8[:, heads + hh:heads + hh + 1])
            m0 = m0_ref[:, hh:hh + 1]
            m_t = jnp.maximum(lf + m0, ig)
            w = jnp.exp(ig - m_t)
            g = jnp.exp(lf + m0 - m_t)
            q = q_ref[:, hh * dk:(hh + 1) * dk]
            k = k_ref[:, hh * dk:(hh + 1) * dk] * (dk ** -0.5)
            v = v_ref[:, hh * dv:(hh + 1) * dv]
            o = o_ref[:, hh * dv:(hh + 1) * dv]
            n0 = n0_ref[:, hh, :]
            kw = k * w
            q_t = q.T
            kw_t = kw.T
            qc_rows = []
            for b in range(bt):
                cmat = c0_ref[b, hh]
                qc_rows.append(jnp.sum(q_t[:, b:b + 1] * cmat, axis=0, keepdims=True))
                c_out[b, hh] = g[b:b + 1, :] * cmat + kw_t[:, b:b + 1] * v[b:b + 1, :]
            qc = jnp.concatenate(qc_rows, axis=0)
            s = jnp.sum(q * k, axis=1, keepdims=True) * w
            num = s * v + g * qc
            den = s + g * jnp.sum(q * n0, axis=1, keepdims=True)
            hraw = num / jnp.maximum(jnp.abs(den), jnp.exp(-m_t))
            hn = _rms(hraw, gml_ref[hh:hh + 1, :])
            h_ref[:, hh * dv:(hh + 1) * dv] = jax.nn.sigmoid(o) * hn
            n_out[:, hh, :] = g * n0 + kw
            m_new = jnp.where(lane == hh, m_t, m_new)
        m_out[...] = m_new


def _stacked_state_grid(layer, b, stack):
    n_l = DEPTH if stack is None else 1
    row = lambda l, i: jnp.where(l == 0, i, b - 1)
    out_layer = lambda l: (layer + l) % DEPTH
    return (n_l, b), row, out_layer


def _mlstm_sample(qkvo, gd, bias_row, g_ml, c0, n0, m0, *, layer, row0, c_stack=None):
    heads, dk, dv = ML_HEADS, ML_DK, ML_DV
    b = c0.shape[1]
    bt = _SAMPLE_BLOCK
    wq = heads * dk
    blk0 = row0 // bt
    grid, row, out_layer = _stacked_state_grid(layer, b // bt, c_stack)
    aliased = c_stack is not None
    body = functools.partial(_mlstm_sample_body, heads=heads, dk=dk, dv=dv, aliased=aliased)
    in_specs = [pl.BlockSpec((bt, wq), lambda l, i, c=c: (blk0 + row(l, i), c)) for c in range(4)]
    in_specs += [pl.BlockSpec((bt, gd.shape[-1]), lambda l, i: (blk0 + row(l, i), 0)),
                 pl.BlockSpec((1, 2 * heads), lambda l, i: (0, 0)),
                 pl.BlockSpec((heads, dv), lambda l, i: (0, 0)),
                 pl.BlockSpec((None, bt, heads, dk, dv), lambda l, i: (layer, row(l, i), 0, 0, 0)),
                 pl.BlockSpec((None, bt, heads, dk), lambda l, i: (layer, row(l, i), 0, 0)),
                 pl.BlockSpec((None, bt, heads), lambda l, i: (layer, row(l, i), 0))]
    args = [qkvo, qkvo, qkvo, qkvo, gd, bias_row, g_ml, c0, n0, m0]
    if aliased:
        in_specs.append(pl.BlockSpec(memory_space=pl.ANY))
        args.append(c_stack)
    return pl.pallas_call(
        body, grid=grid, in_specs=in_specs,
        out_specs=[pl.BlockSpec((bt, heads * dv), lambda l, i: (row(l, i), 0)),
                   pl.BlockSpec((None, bt, heads, dk, dv), lambda l, i: (out_layer(l), i, 0, 0, 0)),
                   pl.BlockSpec((bt, heads, dk), lambda l, i: (row(l, i), 0, 0)),
                   pl.BlockSpec((bt, heads), lambda l, i: (row(l, i), 0))],
        out_shape=[jax.ShapeDtypeStruct((b, heads * dv), f32),
                   jax.ShapeDtypeStruct((DEPTH, b, heads, dk, dv), f32),
                   jax.ShapeDtypeStruct((b, heads, dk), f32),
                   jax.ShapeDtypeStruct((b, heads), f32)],
        input_output_aliases={10: 1} if aliased else {},
        compiler_params=_params("arbitrary", "arbitrary"), name="mlstm_sample",
    )(*args)


def _s5_advance(x, l1, l2):
    return l1 * x + l2 * pltpu.roll(x, x.shape[-1] // 2, axis=1)


def _toeplitz_operator(krow):
    ch, tc = krow.shape
    lane = lax.broadcasted_iota(jnp.int32, krow.shape, 1)
    blocks = [krow] + [jnp.where(lane >= ch * s, pltpu.roll(krow, ch * s, axis=1), 0.0)
                       for s in range(1, tc // ch)]
    return jnp.concatenate(blocks, axis=0).astype(bf16)


def _s5_body(u_ref, m_ref, w_ref, v_ref, l1_ref, l2_ref, x0_ref, y_ref, x_out, xs_s, *, nc, batch):
    u = u_ref[0]
    xin = jnp.dot(u, w_ref[0], preferred_element_type=f32)
    l1 = l1_ref[0]
    l2 = l2_ref[0]
    x = x0_ref[0]
    for k in range(nc):
        sl = slice(k * batch, (k + 1) * batch)
        xs_s[sl, :] = x
        x = _s5_advance(x, l1, l2) + xin[sl, :]
    x_out[0] = x
    y_ref[0] = (jnp.dot(u, _toeplitz_operator(m_ref[0]), preferred_element_type=f32)
                + jnp.dot(xs_s[...].astype(bf16), v_ref[0], preferred_element_type=f32))


def _s5_scan(u_g, mats, x0, *, nc, batch, layer):
    m_mat, w_pk, v_pk, l1, l2 = mats
    g, rows, tc = u_g.shape
    p2 = w_pk.shape[-1]
    blk = lambda *s: pl.BlockSpec((1,) + s, lambda i: (i,) + (0,) * len(s))
    op = lambda *s: pl.BlockSpec((None, 1) + s, lambda i: (layer, i) + (0,) * len(s))
    body = functools.partial(_s5_body, nc=nc, batch=batch)
    return pl.pallas_call(
        body, grid=(g,),
        in_specs=[blk(rows, tc), op(S5_CH, tc), op(tc, p2), op(p2, tc), op(1, p2), op(1, p2), blk(batch, p2)],
        out_specs=[blk(rows, tc), blk(batch, p2)],
        out_shape=[jax.ShapeDtypeStruct((g, rows, tc), f32), jax.ShapeDtypeStruct((g, batch, p2), f32)],
        scratch_shapes=[pltpu.VMEM((rows, p2), f32)],
        compiler_params=_params("parallel"), name=f"s5_scan_t{tc // S5_CH}",
    )(u_g, m_mat, w_pk, v_pk, l1, l2, x0)


def _s5_prompt_body(x_ref, m_ref, w_ref, v_ref, l1_ref, l2_ref, y_ref, x_out, u_s, y_s, xin_s, xs_s,
                    *, gw, batch, seq, t, ch):
    nc = seq // t
    for b in range(batch):
        for tt in range(t):
            blk = x_ref[pl.ds(b * seq + tt, nc, stride=t), :]
            for gl in range(gw):
                u_s[gl, b * nc:(b + 1) * nc, tt * ch:(tt + 1) * ch] = blk[:, gl * ch:(gl + 1) * ch]
    for gl in range(gw):
        xin_s[gl] = jnp.dot(u_s[gl].astype(bf16), w_ref[gl], preferred_element_type=f32)
    xs = [jnp.zeros((batch, xin_s.shape[-1]), f32)] * gw
    for k in range(nc):
        for gl in range(gw):
            xs_s[gl, pl.ds(k, batch, stride=nc), :] = xs[gl]
            xs[gl] = _s5_advance(xs[gl], l1_ref[gl], l2_ref[gl]) + xin_s[gl, pl.ds(k, batch, stride=nc), :]
    for gl in range(gw):
        x_out[gl] = xs[gl]
        y_s[gl] = (jnp.dot(u_s[gl].astype(bf16), _toeplitz_operator(m_ref[gl]), preferred_element_type=f32)
                   + jnp.dot(xs_s[gl].astype(bf16), v_ref[gl], preferred_element_type=f32))
    for b in range(batch):
        for tt in range(t):
            y_ref[pl.ds(b * seq + tt, nc, stride=t), :] = jnp.concatenate(
                [y_s[gl, b * nc:(b + 1) * nc, tt * ch:(tt + 1) * ch] for gl in range(gw)], axis=1)


def _s5_prompt(uzx, mats, *, batch, seq, layer):
    m_mat, w_pk, v_pk, l1, l2 = mats
    _, g, tc, p2 = w_pk.shape
    ch = S5_CH
    t = tc // ch
    gw = 128 // ch
    rows = batch * (seq // t)
    tp = batch * seq
    win = lambda *s: pl.BlockSpec((gw,) + s, lambda i: (i,) + (0,) * len(s))
    op = lambda *s: pl.BlockSpec((None, gw) + s, lambda i: (layer, i) + (0,) * len(s))
    body = functools.partial(_s5_prompt_body, gw=gw, batch=batch, seq=seq, t=t, ch=ch)
    return pl.pallas_call(
        body, grid=(g // gw,),
        in_specs=[pl.BlockSpec((tp, gw * ch), lambda i: (0, i)), op(ch, tc), op(tc, p2), op(p2, tc),
                  op(1, p2), op(1, p2)],
        out_specs=[pl.BlockSpec((tp, gw * ch), lambda i: (0, i)), win(batch, p2)],
        out_shape=[jax.ShapeDtypeStruct((tp, g * ch), f32), jax.ShapeDtypeStruct((g, batch, p2), f32)],
        scratch_shapes=[pltpu.VMEM((gw, rows, tc), f32), pltpu.VMEM((gw, rows, tc), f32),
                        pltpu.VMEM((gw, rows, p2), f32), pltpu.VMEM((gw, rows, p2), f32)],
        compiler_params=_params("parallel"), name="s5_prompt",
    )(uzx, m_mat, w_pk, v_pk, l1, l2)


def _s5_matrices(lam_re, lam_im, log_dt, b_re, b_im, c_re, c_im, t):
    g, p = lam_re.shape
    ch = b_re.shape[-1]
    dt = jnp.exp(log_dt)[:, None]
    ar, ai = lam_re * dt, lam_im * dt

    def powers(tau):
        mag = jnp.exp(ar[:, None, :] * tau[None, :, None])
        ang = ai[:, None, :] * tau[None, :, None]
        return mag * jnp.cos(ang), mag * jnp.sin(ang)

    lbr, lbi = jnp.exp(ar) * jnp.cos(ai), jnp.exp(ar) * jnp.sin(ai)
    den = lam_re * lam_re + lam_im * lam_im
    fr = ((lbr - 1.0) * lam_re + lbi * lam_im) / den
    fi = (lbi * lam_re - (lbr - 1.0) * lam_im) / den
    bbr = jnp.swapaxes(fr[..., None] * b_re - fi[..., None] * b_im, 1, 2)
    bbi = jnp.swapaxes(fr[..., None] * b_im + fi[..., None] * b_re, 1, 2)
    cbr = c_re[:, :, None, :] * bbr[:, None, :, :] - c_im[:, :, None, :] * bbi[:, None, :, :]
    cbi = c_re[:, :, None, :] * bbi[:, None, :, :] + c_im[:, :, None, :] * bbr[:, None, :, :]
    steps = jnp.arange(t, dtype=f32)
    lr, li = powers(steps)
    krow = jnp.einsum("gcdp,gtp->gdtc", jnp.concatenate([cbr, -cbi], axis=-1),
                      jnp.concatenate([lr, li], axis=-1), precision=HI).reshape(g, ch, t * ch)
    pr, pi = powers(t - 1.0 - steps)
    w_re = (pr[:, :, None, :] * bbr[:, None, :, :] - pi[:, :, None, :] * bbi[:, None, :, :]).reshape(g, t * ch, p)
    w_im = (pr[:, :, None, :] * bbi[:, None, :, :] + pi[:, :, None, :] * bbr[:, None, :, :]).reshape(g, t * ch, p)
    qr, qi = powers(steps + 1.0)
    qr, qi = jnp.swapaxes(qr, 1, 2)[..., None], jnp.swapaxes(qi, 1, 2)[..., None]
    ctr, cti = jnp.swapaxes(c_re, 1, 2)[:, :, None, :], jnp.swapaxes(c_im, 1, 2)[:, :, None, :]
    v_re = (ctr * qr - cti * qi).reshape(g, p, t * ch)
    v_im = -(ctr * qi + cti * qr).reshape(g, p, t * ch)
    ltr, lti = powers(jnp.full((1,), float(t), f32))
    w_pk = jnp.concatenate([w_re, w_im], axis=-1).astype(bf16)
    v_pk = jnp.concatenate([v_re, v_im], axis=1).astype(bf16)
    chunk_ops = (krow, w_pk, v_pk, jnp.concatenate([ltr, ltr], axis=-1), jnp.concatenate([-lti, lti], axis=-1))
    lb_r, lb_i = lbr[:, None, :], lbi[:, None, :]
    step_ops = (krow[:, :, :ch], w_pk[:, (t - 1) * ch:, :], v_pk[:, :, :ch],
                jnp.concatenate([lb_r, lb_r], axis=-1), jnp.concatenate([-lb_i, lb_i], axis=-1))
    return chunk_ops, step_ops


def _s5_glu_body(yp_ref, ys_ref, u_ref, d_ref, w_ref, b_ref, g_ref, o_ref, *, n_keep):
    last = pl.num_programs(0) - 1

    def run(y_raw):
        y5 = jax.nn.gelu(y_raw + d_ref[...] * u_ref[...])
        gate = jax.nn.sigmoid(jnp.dot(y5.astype(bf16), w_ref[...].astype(bf16), preferred_element_type=f32)
                              + b_ref[...])
        o_ref[...] = _rms(y5 * gate, g_ref[...]).astype(o_ref.dtype)

    @pl.when(pl.program_id(0) < last)
    def _():
        run(yp_ref[...])

    @pl.when(pl.program_id(0) == last)
    def _():
        run(_last_tile(yp_ref, ys_ref[...], n_keep))


def _s5_glu(y_prompt, y_sample, uzx, d_skip, w_glu, b_glu, g_s5, *, layer, tm):
    tp, wdt = y_prompt.shape
    bs = y_sample.shape[0]
    n_tiles, n_keep = _split_rows(tp, bs, tm)
    row = lambda i: (i, 0)
    fix = lambda i: (0, 0)
    return pl.pallas_call(
        functools.partial(_s5_glu_body, n_keep=n_keep), grid=(n_tiles,),
        in_specs=[pl.BlockSpec((tm, wdt), row), pl.BlockSpec((bs, wdt), fix), pl.BlockSpec((tm, wdt), row),
                  pl.BlockSpec((1, wdt), fix), pl.BlockSpec((None, wdt, wdt), lambda i: (layer, 0, 0)),
                  pl.BlockSpec((1, wdt), fix), pl.BlockSpec((1, wdt), fix)],
        out_specs=pl.BlockSpec((tm, wdt), row), out_shape=jax.ShapeDtypeStruct((tp + bs, wdt), bf16),
        compiler_params=_params("parallel"), name="s5_glu",
    )(y_prompt, y_sample, uzx, d_skip, w_glu, b_glu, g_s5)


def _ssd_prompt_body(xbc_ref, z_ref, gd_ref, dtt_ref, cw_ref, cb_ref, dtb_row, dtb_col, alog_row, alog_col,
                     dskip_ref, gssd_ref, y_ref, s_out, s_s, xp_s, ys_s, *, heads, hd, ns, groups, cl, width):
    c = pl.program_id(1)

    @pl.when(c == 0)
    def _():
        s_s[...] = jnp.zeros_like(s_s)
        xp_s[0:8, :] = jnp.zeros((8, xp_s.shape[1]), f32)

    xp_s[8:8 + cl, :] = xbc_ref[...]
    xc = cb_ref[...] + sum(cw_ref[j:j + 1, :] * xp_s[5 + j:5 + j + cl, :] for j in range(SSD_CONV))
    xp_s[0:8, :] = xp_s[cl:cl + 8, :]
    xc = xc * jax.nn.sigmoid(xc)
    row = lax.broadcasted_iota(jnp.int32, (cl, cl), 0)
    col = lax.broadcasted_iota(jnp.int32, (cl, cl), 1)
    tril = (col <= row).astype(f32)
    triu = (row <= col).astype(f32)
    row2 = lax.broadcasted_iota(jnp.int32, (cl, 2 * cl), 0)
    lane2 = lax.broadcasted_iota(jnp.int32, (cl, 2 * cl), 1)
    left = lane2 < cl
    causal2 = jnp.where(left, lane2, lane2 - cl) <= row2
    left_row = left[0:1, :]
    top = lax.broadcasted_iota(jnp.int32, (2 * hd, 1), 0) < hd
    dt_col = jax.nn.softplus(gd_ref[:, 8:8 + heads] + dtb_row[...])
    dt_row = jax.nn.softplus(dtt_ref[0, 0] + dtb_col[...])
    cum_col = jnp.dot(tril, dt_col * -jnp.exp(alog_row[...]), precision=HI, preferred_element_type=f32)
    cum_row = jnp.dot(dt_row * -jnp.exp(alog_col[...]), triu, precision=HI, preferred_element_type=f32)
    exp_col = jnp.exp(cum_col)
    pick = lambda cols, h0: jnp.where(left, cols[:, h0:h0 + 1], cols[:, h0 + 1:h0 + 2])
    rep = heads // groups
    for gi in range(groups):
        bm = xc[:, width + gi * ns:width + (gi + 1) * ns].astype(bf16)
        cm = xc[:, width + (groups + gi) * ns:width + (groups + gi + 1) * ns].astype(bf16)
        scores = lax.dot_general(cm, bm, _NT, preferred_element_type=f32)
        scores2 = jnp.concatenate([scores, scores], axis=1)
        for h0 in range(gi * rep, (gi + 1) * rep, 2):
            lo, hi = h0 * hd, (h0 + 2) * hd
            cc2 = pick(cum_col, h0)
            cr2 = jnp.concatenate([cum_row[h0:h0 + 1, :], cum_row[h0 + 1:h0 + 2, :]], axis=1)
            seg2 = jnp.exp(jnp.where(causal2, cc2 - cr2, -jnp.inf))
            x2 = xc[:, lo:hi]
            xdt2 = x2 * pick(dt_col, h0)
            xbd = jnp.concatenate([jnp.where(left, xdt2, 0.0), jnp.where(left, 0.0, xdt2)], axis=0)
            smat2 = s_s[lo:hi, :]
            y2 = (jnp.dot((scores2 * seg2).astype(bf16), xbd.astype(bf16), preferred_element_type=f32)
                  + pick(exp_col, h0) * lax.dot_general(cm, smat2.astype(bf16), _NT, preferred_element_type=f32))
            last0 = cum_row[h0:h0 + 1, cl - 1:cl]
            last1 = cum_row[h0 + 1:h0 + 2, cl - 1:cl]
            xw2 = (xdt2 * jnp.exp(jnp.where(left_row, last0, last1) - cc2)).astype(bf16)
            s_s[lo:hi, :] = (jnp.where(top, jnp.exp(last0), jnp.exp(last1)) * smat2
                             + lax.dot_general(xw2, bm, _TN, preferred_element_type=f32))
            dsk2 = jnp.where(left_row, dskip_ref[:, h0:h0 + 1], dskip_ref[:, h0 + 1:h0 + 2])
            ys_s[:, lo:hi] = y2 + dsk2 * x2
    z = z_ref[...]
    y_ref[...] = _rms(ys_s[...] * (z * jax.nn.sigmoid(z)), gssd_ref[...]).astype(y_ref.dtype)

    @pl.when(c == pl.num_programs(1) - 1)
    def _():
        s_out[0] = s_s[...]


def _ssd_prompt(uzx, gd, dt_t, conv_w, conv_b, dt_bias, a_log, d_skip, g_ssd, *, batch, seq):
    heads, hd, ns, groups, width = SSD_HEADS, SSD_HEAD_DIM, SSD_STATE, SSD_GROUPS, SSD_WIDTH
    cl = math.gcd(seq, CHUNK)
    assert cl == hd and (heads // groups) % 2 == 0
    nc = seq // cl
    cch = SSD_CONV_CH
    rows = lambda b, c: (b * nc + c, 0)
    fix = lambda b, c: (0, 0)
    body = functools.partial(_ssd_prompt_body, heads=heads, hd=hd, ns=ns, groups=groups, cl=cl, width=width)
    return pl.pallas_call(
        body, grid=(batch, nc),
        in_specs=[pl.BlockSpec((cl, cch), lambda b, c: (b * nc + c, 1)),
                  pl.BlockSpec((cl, width), lambda b, c: (b * nc + c, 1)),
                  pl.BlockSpec((cl, gd.shape[1]), rows),
                  pl.BlockSpec((1, 1, heads, cl), lambda b, c: (b, c, 0, 0)),
                  pl.BlockSpec((SSD_CONV, cch), fix), pl.BlockSpec((1, cch), fix),
                  pl.BlockSpec((1, heads), fix), pl.BlockSpec((heads, 1), fix),
                  pl.BlockSpec((1, heads), fix), pl.BlockSpec((heads, 1), fix),
                  pl.BlockSpec((1, heads), fix), pl.BlockSpec((1, width), fix)],
        out_specs=[pl.BlockSpec((cl, width), rows),
                   pl.BlockSpec((1, heads * hd, ns), lambda b, c: (b, 0, 0))],
        out_shape=[jax.ShapeDtypeStruct((batch * seq, width), bf16),
                   jax.ShapeDtypeStruct((batch, heads * hd, ns), f32)],
        scratch_shapes=[pltpu.VMEM((heads * hd, ns), f32), pltpu.VMEM((cl + 8, cch), f32),
                        pltpu.VMEM((cl, width), f32)],
        compiler_params=_params("arbitrary", "arbitrary"), name="ssd_prompt",
    )(uzx, uzx, gd, dt_t, conv_w, conv_b.reshape(1, cch), dt_bias.reshape(1, heads), dt_bias.reshape(heads, 1),
      a_log.reshape(1, heads), a_log.reshape(heads, 1), d_skip.reshape(1, heads), g_ssd.reshape(1, width))


def _ssd_sample_body(*refs, heads, hd, ns, groups, width, aliased):
    if aliased:
        refs = refs[:10] + refs[11:]
    (x_ref, conv0_ref, gd_ref, cw_ref, cb_ref, dtb_ref, alog_ref, dskip_ref, gssd_ref, s0_ref,
     y_ref, s_out, ys_s) = refs
    bt = x_ref.shape[0]

    @pl.when(pl.program_id(0) > 0)
    def _():
        s_out[...] = jnp.zeros_like(s_out)

    @pl.when(pl.program_id(0) == 0)
    def _():
        z = x_ref[:, width:2 * width]
        xc = cb_ref[...] + cw_ref[SSD_CONV - 1:SSD_CONV, :] * x_ref[:, 2 * width:]
        for j in range(SSD_CONV - 1):
            xc = xc + cw_ref[j:j + 1, :] * conv0_ref[:, j, :]
        xc = xc * jax.nn.sigmoid(xc)
        dt = jax.nn.softplus(gd_ref[:, 8:8 + heads] + dtb_ref[...])
        ea = jnp.exp(dt * -jnp.exp(alog_ref[...]))
        rep = heads // groups
        for gi in range(groups):
            bm = xc[:, width + gi * ns:width + (gi + 1) * ns]
            cm = xc[:, width + (groups + gi) * ns:width + (groups + gi + 1) * ns]
            cb_dot = jnp.sum(cm * bm, axis=1, keepdims=True)
            cmb = cm.astype(bf16)
            for hh in range(gi * rep, (gi + 1) * rep):
                xh = xc[:, hh * hd:(hh + 1) * hd]
                xdt = xh * dt[:, hh:hh + 1]
                eah = ea[:, hh:hh + 1]
                xdt_t = xdt.T
                sc_rows = []
                for b in range(bt):
                    smat = s0_ref[b, hh]
                    sc_rows.append(lax.dot_general(cmb, smat.astype(bf16), _NT,
                                                   preferred_element_type=f32)[b:b + 1, :])
                    s_out[b, hh] = eah[b:b + 1, :] * smat + xdt_t[:, b:b + 1] * bm[b:b + 1, :]
                sc = jnp.concatenate(sc_rows, axis=0)
                ys_s[:, hh * hd:(hh + 1) * hd] = cb_dot * xdt + eah * sc + dskip_ref[:, hh:hh + 1] * xh
        y_ref[...] = _rms(ys_s[...] * (z * jax.nn.sigmoid(z)), gssd_ref[...])


def _ssd_sample(uzx, conv0, gd, conv_w, conv_b, dt_bias, a_log, d_skip, g_ssd, s0, *, layer, row0, s_stack=None):
    heads, hd, ns, groups, width = SSD_HEADS, SSD_HEAD_DIM, SSD_STATE, SSD_GROUPS, SSD_WIDTH
    cch = SSD_CONV_CH
    b = s0.shape[1]
    bt = _SAMPLE_BLOCK
    blk0 = row0 // bt
    fix = lambda l, i: (0, 0)
    grid, row, out_layer = _stacked_state_grid(layer, b // bt, s_stack)
    aliased = s_stack is not None
    body = functools.partial(_ssd_sample_body, heads=heads, hd=hd, ns=ns, groups=groups, width=width,
                             aliased=aliased)
    in_specs = [pl.BlockSpec((bt, uzx.shape[-1]), lambda l, i: (blk0 + row(l, i), 0)),
                pl.BlockSpec((None, bt, SSD_CONV - 1, cch), lambda l, i: (layer, row(l, i), 0, 0)),
                pl.BlockSpec((bt, gd.shape[-1]), lambda l, i: (blk0 + row(l, i), 0)),
                pl.BlockSpec((SSD_CONV, cch), fix), pl.BlockSpec((1, cch), fix),
                pl.BlockSpec((1, heads), fix), pl.BlockSpec((1, heads), fix), pl.BlockSpec((1, heads), fix),
                pl.BlockSpec((1, width), fix),
                pl.BlockSpec((None, bt, heads, hd, ns), lambda l, i: (layer, row(l, i), 0, 0, 0))]
    args = [uzx, conv0, gd, conv_w, conv_b.reshape(1, cch), dt_bias.reshape(1, heads), a_log.reshape(1, heads),
            d_skip.reshape(1, heads), g_ssd.reshape(1, width), s0]
    if aliased:
        in_specs.append(pl.BlockSpec(memory_space=pl.ANY))
        args.append(s_stack)
    return pl.pallas_call(
        body, grid=grid, in_specs=in_specs,
        out_specs=[pl.BlockSpec((bt, width), lambda l, i: (row(l, i), 0)),
                   pl.BlockSpec((None, bt, heads, hd, ns), lambda l, i: (out_layer(l), i, 0, 0, 0))],
        out_shape=[jax.ShapeDtypeStruct((b, width), f32),
                   jax.ShapeDtypeStruct((DEPTH, b, heads, hd, ns), f32)],
        scratch_shapes=[pltpu.VMEM((bt, width), f32)],
        input_output_aliases={10: 1} if aliased else {},
        compiler_params=_params("arbitrary", "arbitrary"), name="ssd_sample",
    )(*args)


def _router_body(h_ref, g_ref, wr_ref, br_ref, cf_ref, idx_ref, gate_ref, *, n_exp):
    cf = _rms(h_ref[...], g_ref[...])
    cf_ref[...] = cf
    logits = jnp.dot(cf, wr_ref[...], precision=HI, preferred_element_type=f32) + br_ref[...]
    lane = lax.broadcasted_iota(jnp.int32, logits.shape, 1)
    m1 = jnp.max(logits, axis=1, keepdims=True)
    i1 = jnp.min(jnp.where(logits == m1, lane, n_exp), axis=1, keepdims=True)
    rest = jnp.where(lane == i1, -jnp.inf, logits)
    m2 = jnp.max(rest, axis=1, keepdims=True)
    i2 = jnp.min(jnp.where(rest == m2, lane, n_exp), axis=1, keepdims=True)
    e2 = jnp.exp(m2 - m1)
    g1 = 1.0 / (1.0 + e2)
    two = lax.broadcasted_iota(jnp.int32, (logits.shape[0], TOP_K), 1)
    idx_ref[...] = jnp.where(two == 0, i1, i2)
    gate_ref[...] = jnp.where(two == 0, g1, e2 * g1)


def _router(h, g_ffn, w_router, b_router, *, tm):
    m, d = h.shape
    e = w_router.shape[-1]
    row = lambda i: (i, 0)
    fix = lambda i: (0, 0)
    return pl.pallas_call(
        functools.partial(_router_body, n_exp=e), grid=(m // tm,),
        in_specs=[pl.BlockSpec((tm, d), row), pl.BlockSpec((1, d), fix), pl.BlockSpec((d, e), fix),
                  pl.BlockSpec((1, e), fix)],
        out_specs=[pl.BlockSpec((tm, d), row), pl.BlockSpec((tm, TOP_K), row),
                   pl.BlockSpec((tm, TOP_K), row)],
        out_shape=[jax.ShapeDtypeStruct((m, d), f32),
                   jax.ShapeDtypeStruct((m, TOP_K), jnp.int32), jax.ShapeDtypeStruct((m, TOP_K), f32)],
        compiler_params=_params("parallel"), name="router",
    )(h, g_ffn.reshape(1, d), w_router, b_router.reshape(1, e))


def _row_copy(src_hbm, dst, sem, src_row, dst_row):
    return pltpu.make_async_copy(src_hbm.at[pl.ds(src_row, 1)], dst.at[pl.ds(dst_row, 1)], sem)


_DMA_UNROLL = 8


def _gather_body(idx_ref, x_hbm, o_ref, buf, sem, *, tg):
    i = pl.program_id(0)

    def issue(tile, slot):
        base = tile * tg

        def start(r, carry):
            _row_copy(x_hbm, buf.at[slot], sem.at[slot], idx_ref[base + r], r).start()
            return carry

        lax.fori_loop(0, tg, start, 0, unroll=_DMA_UNROLL)

    @pl.when(i == 0)
    def _():
        issue(0, 0)

    @pl.when(i + 1 < pl.num_programs(0))
    def _():
        issue(i + 1, (i + 1) % 2)

    slot = i % 2
    pltpu.make_async_copy(x_hbm.at[pl.ds(0, tg)], buf.at[slot], sem.at[slot]).wait()
    o_ref[...] = buf[slot].astype(o_ref.dtype)


def _gather_rows(x, row_idx, *, tg, out_dtype):
    r = row_idx.shape[0]
    d = x.shape[1]
    return pl.pallas_call(
        functools.partial(_gather_body, tg=tg),
        grid_spec=pltpu.PrefetchScalarGridSpec(
            num_scalar_prefetch=1, grid=(r // tg,),
            in_specs=[pl.BlockSpec(memory_space=pl.ANY)],
            out_specs=pl.BlockSpec((tg, d), lambda i, idx: (i, 0)),
            scratch_shapes=[pltpu.VMEM((2, tg, d), x.dtype), pltpu.SemaphoreType.DMA((2,))]),
        out_shape=jax.ShapeDtypeStruct((r, d), out_dtype),
        compiler_params=_params("arbitrary"), name="gather_rows")(row_idx, x)


def _gmm_body(te_ref, first_ref, next_ref, nv_ref, a_ref, *rest, n_w, swiglu, n_sub):
    w_hbm = rest[:n_w]
    o_ref, wf_ref, wb_ref, sem, run_ref = rest[n_w:]
    j = pl.program_id(0)
    tn = o_ref.shape[1]
    tm = o_ref.shape[0] // n_sub

    def tile_copies(expert, col_tile, slot):
        col = pl.multiple_of(col_tile * tn, 128)
        return [pltpu.make_async_copy(w_hbm[i].at[expert, :, pl.ds(col, tn)], wf_ref.at[slot, i], sem.at[slot, i])
                for i in range(n_w)]

    @pl.when((j == 0) & (pl.program_id(1) == 0))
    def _():
        run_ref[0] = 0
        for c in tile_copies(te_ref[0], 0, 0):
            c.start()

    for sub in range(n_sub):
        t = pl.program_id(1) * n_sub + sub
        rows = slice(sub * tm, (sub + 1) * tm)

        @pl.when((t < nv_ref[0]) & (first_ref[t] == 1))
        def _():
            slot = run_ref[0] % 2
            for c in tile_copies(te_ref[t], j, slot):
                c.wait()
            for i in range(n_w):
                wb_ref[i] = wf_ref[slot, i].astype(bf16)
            nxt = next_ref[t]

            @pl.when(nxt >= 0)
            def _():
                for c in tile_copies(te_ref[nxt], j, 1 - slot):
                    c.start()

            @pl.when((nxt < 0) & (j + 1 < pl.num_programs(0)))
            def _():
                for c in tile_copies(te_ref[0], j + 1, 1 - slot):
                    c.start()

            run_ref[0] = run_ref[0] + 1

        @pl.when(t < nv_ref[0])
        def _():
            a = a_ref[rows, :]
            if swiglu:
                g = jnp.dot(a, wb_ref[0], preferred_element_type=f32)
                u = jnp.dot(a, wb_ref[1], preferred_element_type=f32)
                o_ref[rows, :] = (g * jax.nn.sigmoid(g) * u).astype(o_ref.dtype)
            else:
                o_ref[rows, :] = jnp.dot(a, wb_ref[0], preferred_element_type=f32).astype(o_ref.dtype)

        @pl.when(t >= nv_ref[0])
        def _():
            o_ref[rows, :] = jnp.zeros((tm, tn), o_ref.dtype)


def _gmm(a_sorted, weights, tables, *, tm, tn, n_sub, swiglu, out_dtype, name):
    tile_expert, tile_first, tile_next, n_valid = tables
    r, k = a_sorted.shape
    n = weights[0].shape[-1]
    n_w = len(weights)
    assert (r // tm) % n_sub == 0
    return pl.pallas_call(
        functools.partial(_gmm_body, n_w=n_w, swiglu=swiglu, n_sub=n_sub),
        grid_spec=pltpu.PrefetchScalarGridSpec(
            num_scalar_prefetch=4, grid=(n // tn, r // (tm * n_sub)),
            in_specs=([pl.BlockSpec((tm * n_sub, k), lambda j, t, *_: (t, 0))]
                      + [pl.BlockSpec(memory_space=pl.ANY)] * n_w),
            out_specs=pl.BlockSpec((tm * n_sub, tn), lambda j, t, *_: (t, j)),
            scratch_shapes=[pltpu.VMEM((2, n_w, k, tn), f32), pltpu.VMEM((n_w, k, tn), bf16),
                            pltpu.SemaphoreType.DMA((2, n_w)), pltpu.SMEM((1,), jnp.int32)]),
        out_shape=jax.ShapeDtypeStruct((r, n), out_dtype),
        compiler_params=_params("arbitrary", "arbitrary", vmem=_VMEM_LIMIT_GMM), name=name,
    )(tile_expert, tile_first, tile_next, n_valid, a_sorted, *weights)


def _combine_body(pos_ref, y_hbm, gate_ref, r_ref, gn_ref, o_ref, e_ref, buf, sem, *, tc):
    i = pl.program_id(0)

    def issue(tile, slot):
        base = tile * tc

        def start(r, carry):
            for kk in range(TOP_K):
                _row_copy(y_hbm, buf.at[slot, kk], sem.at[slot], pos_ref[(base + r) * TOP_K + kk], r).start()
            return carry

        lax.fori_loop(0, tc, start, 0, unroll=_DMA_UNROLL)

    @pl.when(i == 0)
    def _():
        issue(0, 0)

    @pl.when(i + 1 < pl.num_programs(0))
    def _():
        issue(i + 1, (i + 1) % 2)

    slot = i % 2
    for kk in range(TOP_K):
        pltpu.make_async_copy(y_hbm.at[pl.ds(0, tc)], buf.at[slot, kk], sem.at[slot]).wait()
    gate = gate_ref[...]
    out = r_ref[...] + gate[:, 0:1] * buf[slot, 0] + gate[:, 1:2] * buf[slot, 1]
    o_ref[...] = out
    e_ref[...] = _rms(out, gn_ref[...]).astype(e_ref.dtype)


def _combine(y_sorted, pos_flat, gates, resid, g_next, *, tc):
    m, d = resid.shape
    rows = lambda i, pos: (i, 0)
    return pl.pallas_call(
        functools.partial(_combine_body, tc=tc),
        grid_spec=pltpu.PrefetchScalarGridSpec(
            num_scalar_prefetch=1, grid=(m // tc,),
            in_specs=[pl.BlockSpec(memory_space=pl.ANY), pl.BlockSpec((tc, TOP_K), rows),
                      pl.BlockSpec((tc, d), rows), pl.BlockSpec((1, d), lambda i, pos: (0, 0))],
            out_specs=[pl.BlockSpec((tc, d), rows), pl.BlockSpec((tc, d), rows)],
            scratch_shapes=[pltpu.VMEM((2, TOP_K, tc, d), f32), pltpu.SemaphoreType.DMA((2,))]),
        out_shape=[jax.ShapeDtypeStruct((m, d), f32), jax.ShapeDtypeStruct((m, d), bf16)],
        compiler_params=_params("arbitrary"), name="moe_combine",
    )(pos_flat, y_sorted, gates, resid, g_next.reshape(1, d))


def _routing_tables(top_i, n_exp, tm, n_tiles):
    m = top_i.shape[0]
    e_flat = top_i.reshape(-1)
    onehot = (e_flat[:, None] == jnp.arange(n_exp, dtype=jnp.int32)[None, :]).astype(jnp.int32)
    rank = jnp.take_along_axis(jnp.cumsum(onehot, axis=0), e_flat[:, None], axis=1)[:, 0] - 1
    counts = jnp.sum(onehot, axis=0)
    tiles_per = (counts + tm - 1) // tm
    tile_end = jnp.cumsum(tiles_per)
    tile_start = tile_end - tiles_per
    pos = tile_start[e_flat] * tm + rank
    token = jnp.arange(m * TOP_K, dtype=jnp.int32) // TOP_K
    row_token = jnp.zeros((n_tiles * tm,), jnp.int32).at[pos].set(token)
    n_valid = tile_end[-1]
    tid = jnp.minimum(jnp.arange(n_tiles, dtype=jnp.int32), n_valid - 1)
    tile_expert = jnp.sum((tid[:, None] >= tile_end[None, :]).astype(jnp.int32), axis=1)
    tile_first = jnp.concatenate([jnp.ones((1,), jnp.int32),
                                  (tile_expert[1:] != tile_expert[:-1]).astype(jnp.int32)])
    ids = jnp.arange(n_tiles, dtype=jnp.int32)
    starts = jnp.where((tile_first == 1) & (ids < n_valid), ids, n_tiles)
    later = jnp.concatenate([lax.cummin(starts, reverse=True)[1:], jnp.full((1,), n_tiles, jnp.int32)])
    tile_next = jnp.where(later >= n_tiles, -1, later).astype(jnp.int32)
    return (pos.astype(jnp.int32), row_token,
            (tile_expert.astype(jnp.int32), tile_first, tile_next, n_valid.reshape(1).astype(jnp.int32)))


def _moe_ffn(h, g_ffn, w_router, b_router, wg, wu, wd, g_next, *, tm_tok, tm, tn_up, tn_down):
    m = h.shape[0]
    n_exp = wg.shape[0]
    c_f32, top_i, top_g = _router(h, g_ffn, w_router, b_router, tm=tm_tok)
    n_tiles = 2 * (((m * TOP_K) // tm + n_exp + 1) // 2)
    pos, row_token, tables = _routing_tables(top_i, n_exp, tm, n_tiles)
    x_sorted = _gather_rows(c_f32, row_token, tg=tm, out_dtype=bf16)
    h_sorted = _gmm(x_sorted, (wg, wu), tables, tm=tm, tn=tn_up, n_sub=2, swiglu=True, out_dtype=bf16,
                    name="moe_gate_up")
    y_sorted = _gmm(h_sorted, (wd,), tables, tm=tm, tn=tn_down, n_sub=1, swiglu=False, out_dtype=f32,
                    name="moe_down")
    return _combine(y_sorted, pos, top_g, h, g_next, tc=tm_tok // 2)


def kernel(x_prompt, x_sample, state_mlstm_C, state_mlstm_n, state_mlstm_m, state_s5_re, state_s5_im, state_ssd, cache_conv, p_prompt, p_sample, g_mix, w_in, b_igate, b_fgate, g_ml, s5_lam_re, s5_lam_im, s5_log_dt, s5_b_re, s5_b_im, s5_c_re, s5_c_im, s5_d, s5_w_glu, s5_b_glu, g_s5, ssd_conv_w, ssd_conv_b, ssd_dt_bias, ssd_a_log, ssd_d, g_ssd, w_out, g_ffn, ffn_w_gate, ffn_w_up, ffn_w_down, w_router, b_router, moe_w_gate, moe_w_up, moe_w_down, g_ple, w_ple, w_ple_gate, g_final):
    bp, seq, d = x_prompt.shape
    bs = x_sample.shape[0]
    tp = bp * seq
    m = tp + bs
    tm = _token_tile(tp, bs)
    heads = ML_HEADS
    t5 = math.gcd(seq, S5_CHUNK)
    cl = math.gcd(seq, CHUNK)
    nc = seq // cl

    p_p = p_prompt.reshape(DEPTH, tp, -1)
    p_s = p_sample.reshape(DEPTH, bs, -1)
    w_in_t = jnp.swapaxes(w_in, 1, 2)
    outs_p = [[] for _ in range(7)]
    outs_s = [[] for _ in range(5)]
    c_stack = s_stack = None
    mats_p, mats_s = jax.vmap(functools.partial(_s5_matrices, t=t5))(
        s5_lam_re, s5_lam_im, s5_log_dt, s5_b_re, s5_b_im, s5_c_re, s5_c_im)

    for i in range(DEPTH):
        if i == 0:
            h, a = _rmsnorm_in(x_prompt.reshape(tp, d), x_sample.reshape(bs, d), g_mix[i], tm)
        else:
            a = _rmsnorm(h, g_mix[i], bf16, tm)
        qkvo = _mm_nt(a, w_in_t, tm=tm, tn=1024, layer=i, row_off=0, n_rows=_OFF_GATES, name="in_proj_qkvo")
        uzx = _mm_nt(a, w_in_t, tm=tm, tn=1024, layer=i, row_off=_OFF_U, n_rows=_OFF_DT - _OFF_U, name="in_proj_uzx")
        gd = _mm_gates(a, w_in_t, tm=tm, layer=i, off1=_OFF_GATES, n1=2 * heads, off2=_OFF_DT, n2=SSD_HEADS,
                       name="in_proj_gates")

        gates_t = jnp.transpose(gd[:tp, :2 * heads].reshape(bp, nc, cl, 2 * heads), (0, 1, 3, 2))
        bias8 = jnp.concatenate([b_igate[i], b_fgate[i]])
        h_ml_p, c_p, n_p, m_p = _mlstm_prompt(qkvo, gates_t, bias8.reshape(2 * heads, 1), g_ml[i], batch=bp, seq=seq)
        h_ml_s, c_stack, n_s, m_s = _mlstm_sample(qkvo, gd, bias8.reshape(1, 2 * heads), g_ml[i], state_mlstm_C,
                                                  state_mlstm_n, state_mlstm_m, layer=i, row0=tp, c_stack=c_stack)

        y_p, x5_p = _s5_prompt(uzx, mats_p, batch=bp, seq=seq, layer=i)
        s5re_p, s5im_p = x5_p[..., :S5_STATE], x5_p[..., S5_STATE:]
        u_s = jnp.transpose(uzx[tp:, :S5_WIDTH].reshape(bs, S5_GROUPS, S5_CH), (1, 0, 2)).astype(bf16)
        x0_s = jnp.swapaxes(jnp.concatenate([state_s5_re[i], state_s5_im[i]], axis=-1), 0, 1)
        y_s, x5_s = _s5_scan(u_s, mats_s, x0_s, nc=1, batch=bs, layer=i)
        s5re_s, s5im_s = x5_s[..., :S5_STATE], x5_s[..., S5_STATE:]
        y_s = jnp.transpose(y_s, (1, 0, 2)).reshape(bs, S5_WIDTH)
        y5 = _s5_glu(y_p, y_s, uzx, s5_d[i].reshape(1, S5_WIDTH), s5_w_glu,
                     s5_b_glu[i].reshape(1, S5_WIDTH), g_s5[i].reshape(1, S5_WIDTH), layer=i, tm=tm)

        dt_t = jnp.transpose(gd[:tp, 2 * heads:2 * heads + SSD_HEADS].reshape(bp, nc, cl, SSD_HEADS), (0, 1, 3, 2))
        y_ssd_p, ssd_p = _ssd_prompt(uzx, gd, dt_t, ssd_conv_w[i], ssd_conv_b[i], ssd_dt_bias[i], ssd_a_log[i],
                                     ssd_d[i], g_ssd[i], batch=bp, seq=seq)
        y_ssd_s, s_stack = _ssd_sample(uzx, cache_conv, gd, ssd_conv_w[i], ssd_conv_b[i], ssd_dt_bias[i],
                                       ssd_a_log[i], ssd_d[i], g_ssd[i], state_ssd, layer=i, row0=tp,
                                       s_stack=s_stack)
        xbc = uzx[:, S5_WIDTH + SSD_WIDTH:]
        conv_p = xbc[:tp].reshape(bp, seq, -1)[:, seq - (SSD_CONV - 1):]
        conv_s = jnp.concatenate([cache_conv[i][:, 1:], xbc[tp:].reshape(bs, 1, -1)], axis=1)

        for lst, s in zip(outs_p, (c_p, n_p.reshape(bp, heads, ML_DK), m_p.reshape(bp, heads),
                                   jnp.swapaxes(s5re_p, 0, 1), jnp.swapaxes(s5im_p, 0, 1),
                                   ssd_p.reshape(bp, SSD_HEADS, SSD_HEAD_DIM, SSD_STATE), conv_p)):
            lst.append(s)
        for lst, s in zip(outs_s, (n_s, m_s, jnp.swapaxes(s5re_s, 0, 1), jnp.swapaxes(s5im_s, 0, 1), conv_s)):
            lst.append(s)

        h = _mm_mix(h_ml_p.reshape(tp, -1), h_ml_s, y5, y_ssd_p.reshape(tp, -1), y_ssd_s, w_out, h, tm=tm, tn=1024,
                    layer=i, name="out_proj")

        j = i // 2
        if i % 2 == 0:
            cn = _rmsnorm(h, g_ffn[i], bf16, tm)
            hid = _mm_swiglu(cn, ffn_w_gate, ffn_w_up, tm=tm, tn=512, layer=j, name="ffn_gate_up")
            h = _mm(hid, ffn_w_down, tm=tm // 2, tn=512, layer=j, resid=h, name="ffn_down")
            e = _rmsnorm(h, g_ple[i], bf16, tm)
        else:
            n_moe = moe_w_gate.shape[0]
            sel = lambda w: w.reshape((n_moe * N_EXPERTS,) + w.shape[2:])[j * N_EXPERTS:(j + 1) * N_EXPERTS] if n_moe > 1 else w.reshape(w.shape[1:])
            h, e = _moe_ffn(h, g_ffn[i], w_router[j], b_router[j], sel(moe_w_gate), sel(moe_w_up), sel(moe_w_down),
                            g_ple[i], tm_tok=tm, tm=256, tn_up=1024, tn_down=512)

        h = _mm_ple(e, p_p, p_s, w_ple_gate, w_ple, h, tm=tm, tn=1024, layer=i)

    y_p, y_s = _rmsnorm_out(h, g_final, tp, tm)
    n_s, m_s, s5re_s, s5im_s, conv_s = (jnp.stack(l) for l in outs_s)
    return ((y_p.reshape(bp, seq, d), y_s.reshape(bs, 1, d)) + tuple(jnp.stack(l) for l in outs_p)
            + (c_stack, n_s, m_s, s5re_s, s5im_s, s_stack, conv_s))
```

```python
import functools
import math

import jax
import jax.numpy as jnp
from jax import lax
from jax.experimental import pallas as pl
from jax.experimental.pallas import tpu as pltpu

f32 = jnp.float32
bf16 = jnp.bfloat16
HI = lax.Precision.HIGHEST

DEPTH = 2
ML_HEADS = 4
ML_DK = 256
ML_DV = 256
ML_WIDTH = ML_HEADS * ML_DV
S5_CH = 16
S5_WIDTH = 512
S5_GROUPS = 32
S5_STATE = 64
SSD_HEAD_DIM = 64
SSD_WIDTH = 512
SSD_HEADS = 8
SSD_GROUPS = 2
SSD_STATE = 128
SSD_CONV = 4
SSD_CONV_CH = 1024
CHUNK = 64
S5_CHUNK = 32
N_EXPERTS = 8
TOP_K = 2
RMS_EPS = 1e-6

_OFF_GATES = 4 * ML_WIDTH
_OFF_U = _OFF_GATES + 2 * ML_HEADS
_OFF_DT = _OFF_U + S5_WIDTH + SSD_WIDTH + SSD_CONV_CH

_VMEM_LIMIT = 56 * 1024 * 1024
_VMEM_LIMIT_GMM = 60 * 1024 * 1024
_NT = (((1,), (1,)), ((), ()))
_TN = (((0,), (0,)), ((), ()))


def _params(*sem, vmem=_VMEM_LIMIT):
    return pltpu.CompilerParams(dimension_semantics=sem, vmem_limit_bytes=vmem)


def _rms(x, g):
    return x * lax.rsqrt(jnp.mean(x * x, axis=-1, keepdims=True) + RMS_EPS) * g


def _rmsnorm_body(x_ref, g_ref, o_ref):
    o_ref[...] = _rms(x_ref[...], g_ref[...]).astype(o_ref.dtype)


def _last_tile(p_tile, s_rows, n_keep):
    return jnp.concatenate([p_tile[0:n_keep, :], s_rows], axis=0)


def _split_rows(tp, bs, tm):
    n_tiles = (tp + bs) // tm
    n_keep = tp - (n_tiles - 1) * tm
    assert n_tiles * tm == tp + bs and 0 < n_keep and n_keep + bs == tm and n_keep % 16 == 0
    return n_tiles, n_keep


def _token_tile(tp, bs):
    m = tp + bs
    for n_tiles in (10, 8, 5, 4, 2, 1):
        tm = m // n_tiles
        if m % n_tiles == 0 and tm % 32 == 0 and bs < tm and (tp - (n_tiles - 1) * tm) % 16 == 0:
            return tm
    raise ValueError("no row tiling for these token counts")


def _rmsnorm_in_body(xp_ref, xs_ref, g_ref, h_ref, a_ref, *, n_keep):
    last = pl.num_programs(0) - 1

    def run(x):
        h_ref[...] = x
        a_ref[...] = _rms(x, g_ref[...]).astype(a_ref.dtype)

    @pl.when(pl.program_id(0) < last)
    def _():
        run(xp_ref[...])

    @pl.when(pl.program_id(0) == last)
    def _():
        run(_last_tile(xp_ref, xs_ref[...], n_keep))


def _rmsnorm_in(xp, xs, g, tm):
    tp, d = xp.shape
    bs = xs.shape[0]
    n_tiles, n_keep = _split_rows(tp, bs, tm)
    row = lambda i: (i, 0)
    fix = lambda i: (0, 0)
    return pl.pallas_call(
        functools.partial(_rmsnorm_in_body, n_keep=n_keep), grid=(n_tiles,),
        in_specs=[pl.BlockSpec((tm, d), row), pl.BlockSpec((bs, d), fix), pl.BlockSpec((1, d), fix)],
        out_specs=[pl.BlockSpec((tm, d), row), pl.BlockSpec((tm, d), row)],
        out_shape=[jax.ShapeDtypeStruct((tp + bs, d), f32), jax.ShapeDtypeStruct((tp + bs, d), bf16)],
        compiler_params=_params("parallel"), name="rmsnorm_in")(xp, xs, g.reshape(1, d))


def _rmsnorm_out_body(x_ref, g_ref, yp_ref, ys_ref, *, n_keep):
    y = _rms(x_ref[...], g_ref[...])
    yp_ref[...] = y

    @pl.when(pl.program_id(0) == pl.num_programs(0) - 1)
    def _():
        ys_ref[...] = y[n_keep:, :]


def _rmsnorm_out(x, g, tp, tm):
    m, d = x.shape
    bs = m - tp
    n_tiles, n_keep = _split_rows(tp, bs, tm)
    return pl.pallas_call(
        functools.partial(_rmsnorm_out_body, n_keep=n_keep), grid=(n_tiles,),
        in_specs=[pl.BlockSpec((tm, d), lambda i: (i, 0)), pl.BlockSpec((1, d), lambda i: (0, 0))],
        out_specs=[pl.BlockSpec((tm, d), lambda i: (i, 0)), pl.BlockSpec((bs, d), lambda i: (0, 0))],
        out_shape=[jax.ShapeDtypeStruct((tp, d), f32), jax.ShapeDtypeStruct((bs, d), f32)],
        compiler_params=_params("arbitrary"), name="rmsnorm_out")(x, g.reshape(1, d))


def _rmsnorm(x, g, out_dtype, tm):
    m, d = x.shape
    return pl.pallas_call(
        _rmsnorm_body, grid=(m // tm,),
        in_specs=[pl.BlockSpec((tm, d), lambda i: (i, 0)), pl.BlockSpec((1, d), lambda i: (0, 0))],
        out_specs=pl.BlockSpec((tm, d), lambda i: (i, 0)),
        out_shape=jax.ShapeDtypeStruct((m, d), out_dtype),
        compiler_params=_params("parallel"), name="rmsnorm")(x, g.reshape(1, d))


def _cast_weight_once(w_ref, wb_ref):
    @pl.when(pl.program_id(1) == 0)
    def _():
        wb_ref[...] = w_ref[...].astype(bf16)


def _mm_plain_body(a_ref, w_ref, o_ref, wb_ref):
    _cast_weight_once(w_ref, wb_ref)
    o_ref[...] = jnp.dot(a_ref[...], wb_ref[...], preferred_element_type=f32).astype(o_ref.dtype)


def _mm_resid_body(a_ref, w_ref, r_ref, o_ref, wb_ref):
    _cast_weight_once(w_ref, wb_ref)
    o_ref[...] = r_ref[...] + jnp.dot(a_ref[...], wb_ref[...], preferred_element_type=f32)


def _mm_swiglu_body(a_ref, wg_ref, wu_ref, o_ref, wgb_ref, wub_ref):
    _cast_weight_once(wg_ref, wgb_ref)
    _cast_weight_once(wu_ref, wub_ref)
    a = a_ref[...]
    g = jnp.dot(a, wgb_ref[...], preferred_element_type=f32)
    u = jnp.dot(a, wub_ref[...], preferred_element_type=f32)
    o_ref[...] = (g * jax.nn.sigmoid(g) * u).astype(o_ref.dtype)


def _mm_ple_body(e_ref, pp_ref, ps_ref, wg_ref, wp_ref, r_ref, o_ref, wgb_ref, wpb_ref, *, n_keep):
    _cast_weight_once(wg_ref, wgb_ref)
    _cast_weight_once(wp_ref, wpb_ref)
    last = pl.num_programs(1) - 1

    def run(p):
        gate = jnp.dot(e_ref[...], wgb_ref[...], preferred_element_type=f32)
        emb = jnp.dot(p.astype(bf16), wpb_ref[...], preferred_element_type=f32)
        o_ref[...] = r_ref[...] + emb * jax.nn.sigmoid(gate)

    @pl.when(pl.program_id(1) < last)
    def _():
        run(pp_ref[...])

    @pl.when(pl.program_id(1) == last)
    def _():
        run(_last_tile(pp_ref, ps_ref[...], n_keep))


def _mm_mix_body(a1p_ref, a1s_ref, a2_ref, a3p_ref, a3s_ref, w_ref, r_ref, o_ref, wb_ref, *, n_keep):
    _cast_weight_once(w_ref, wb_ref)
    last = pl.num_programs(1) - 1
    k1 = a1p_ref.shape[1]
    k2 = a2_ref.shape[1]

    def run(a1, a3):
        acc = jnp.dot(a1, wb_ref[0:k1, :], preferred_element_type=f32)
        acc += jnp.dot(a2_ref[...], wb_ref[k1:k1 + k2, :], preferred_element_type=f32)
        acc += jnp.dot(a3, wb_ref[k1 + k2:, :], preferred_element_type=f32)
        o_ref[...] = r_ref[...] + acc

    @pl.when(pl.program_id(1) < last)
    def _():
        run(a1p_ref[...], a3p_ref[...])

    @pl.when(pl.program_id(1) == last)
    def _():
        run(_last_tile(a1p_ref, a1s_ref[...].astype(bf16), n_keep),
            _last_tile(a3p_ref, a3s_ref[...].astype(bf16), n_keep))


def _mm_mix(a1p, a1s, a2, a3p, a3s, w, resid, *, tm, tn, layer, name):
    m, n = resid.shape
    tp, k1 = a1p.shape
    bs = a1s.shape[0]
    k2, k3 = a2.shape[1], a3p.shape[1]
    n_tiles, n_keep = _split_rows(tp, bs, tm)
    rows = lambda j, i: (i, 0)
    fix = lambda j, i: (0, 0)
    o_spec = pl.BlockSpec((tm, tn), lambda j, i: (i, j))
    return pl.pallas_call(
        functools.partial(_mm_mix_body, n_keep=n_keep), grid=(n // tn, n_tiles),
        in_specs=[pl.BlockSpec((tm, k1), rows), pl.BlockSpec((bs, k1), fix), pl.BlockSpec((tm, k2), rows),
                  pl.BlockSpec((tm, k3), rows), pl.BlockSpec((bs, k3), fix),
                  _wspec(w, layer, k1 + k2 + k3, tn), o_spec],
        out_specs=o_spec, out_shape=jax.ShapeDtypeStruct((m, n), f32),
        scratch_shapes=[pltpu.VMEM((k1 + k2 + k3, tn), bf16)],
        compiler_params=_params("arbitrary", "arbitrary"), name=name)(a1p, a1s, a2, a3p, a3s, w, resid)


def _wspec(w, layer, k, tn, col_block_off=0):
    if w.ndim == 2:
        return pl.BlockSpec((k, tn), lambda j, i: (0, j + col_block_off))
    return pl.BlockSpec((None, k, tn), lambda j, i: (layer, 0, j + col_block_off))


def _mm(a, w, *, tm, tn, layer=0, n_cols=None, col_off=0, resid=None, out_dtype=f32, name="mm"):
    m, k = a.shape
    n = n_cols if n_cols is not None else w.shape[-1]
    grid = (n // tn, m // tm)
    a_spec = pl.BlockSpec((tm, k), lambda j, i: (i, 0))
    o_spec = pl.BlockSpec((tm, tn), lambda j, i: (i, j))
    w_spec = _wspec(w, layer, k, tn, col_off // tn)
    scratch = [pltpu.VMEM((k, tn), bf16)]
    if resid is None:
        body, ins, specs = _mm_plain_body, (a, w), [a_spec, w_spec]
    else:
        body, ins, specs = _mm_resid_body, (a, w, resid), [a_spec, w_spec, o_spec]
    return pl.pallas_call(
        body, grid=grid, in_specs=specs, out_specs=o_spec,
        out_shape=jax.ShapeDtypeStruct((m, n), out_dtype), scratch_shapes=scratch,
        compiler_params=_params("arbitrary", "arbitrary"), name=name)(*ins)


def _mm_nt_body(a_ref, wt_ref, o_ref, wb_ref):
    @pl.when(pl.program_id(1) == 0)
    def _():
        wb_ref[...] = wt_ref[0].T.astype(bf16)

    o_ref[...] = jnp.dot(a_ref[...], wb_ref[...], preferred_element_type=f32)


def _wt_rows(layer, k, n_rows, row_of):
    return pl.BlockSpec((pl.Element(1), pl.Element(n_rows), pl.Element(k)),
                        lambda *idx: (layer, row_of(*idx), 0))


def _mm_nt(a, wt, *, tm, tn, layer, row_off, n_rows, name):
    m, k = a.shape
    return pl.pallas_call(
        _mm_nt_body, grid=(n_rows // tn, m // tm),
        in_specs=[pl.BlockSpec((tm, k), lambda j, i: (i, 0)),
                  _wt_rows(layer, k, tn, lambda j, i: pl.multiple_of(row_off + tn * j, 8))],
        out_specs=pl.BlockSpec((tm, tn), lambda j, i: (i, j)),
        out_shape=jax.ShapeDtypeStruct((m, n_rows), f32), scratch_shapes=[pltpu.VMEM((k, tn), bf16)],
        compiler_params=_params("arbitrary", "arbitrary"), name=name)(a, wt)


def _mm_gates_body(a_ref, w1_ref, w2_ref, o_ref):
    a = a_ref[...]
    n1 = w1_ref.shape[1]
    o_ref[:, 0:n1] = lax.dot_general(a, w1_ref[0].astype(bf16), _NT, preferred_element_type=f32)
    o_ref[:, n1:] = lax.dot_general(a, w2_ref[0].astype(bf16), _NT, preferred_element_type=f32)


def _mm_gates(a, wt, *, tm, layer, off1, n1, off2, n2, name):
    m, k = a.shape
    return pl.pallas_call(
        _mm_gates_body, grid=(m // tm,),
        in_specs=[pl.BlockSpec((tm, k), lambda i: (i, 0)),
                  _wt_rows(layer, k, n1, lambda i: off1), _wt_rows(layer, k, n2, lambda i: off2)],
        out_specs=pl.BlockSpec((tm, n1 + n2), lambda i: (i, 0)),
        out_shape=jax.ShapeDtypeStruct((m, n1 + n2), f32),
        compiler_params=_params("parallel"), name=name)(a, wt, wt)


def _mm_swiglu(a, wg, wu, *, tm, tn, layer=0, name="mm_swiglu"):
    m, k = a.shape
    n = wg.shape[-1]
    a_spec = pl.BlockSpec((tm, k), lambda j, i: (i, 0))
    o_spec = pl.BlockSpec((tm, tn), lambda j, i: (i, j))
    return pl.pallas_call(
        _mm_swiglu_body, grid=(n // tn, m // tm),
        in_specs=[a_spec, _wspec(wg, layer, k, tn), _wspec(wu, layer, k, tn)], out_specs=o_spec,
        out_shape=jax.ShapeDtypeStruct((m, n), bf16),
        scratch_shapes=[pltpu.VMEM((k, tn), bf16), pltpu.VMEM((k, tn), bf16)],
        compiler_params=_params("arbitrary", "arbitrary"), name=name)(a, wg, wu)


def _mm_ple(e, p_prompt, p_sample, w_gate, w_ple, resid, *, tm, tn, layer, name="mm_ple"):
    m, k = e.shape
    _, tp, kp = p_prompt.shape
    bs = p_sample.shape[1]
    n = w_gate.shape[-1]
    n_tiles, n_keep = _split_rows(tp, bs, tm)
    o_spec = pl.BlockSpec((tm, tn), lambda j, i: (i, j))
    return pl.pallas_call(
        functools.partial(_mm_ple_body, n_keep=n_keep), grid=(n // tn, n_tiles),
        in_specs=[pl.BlockSpec((tm, k), lambda j, i: (i, 0)),
                  pl.BlockSpec((None, tm, kp), lambda j, i: (layer, i, 0)),
                  pl.BlockSpec((None, bs, kp), lambda j, i: (layer, 0, 0)),
                  _wspec(w_gate, layer, k, tn), _wspec(w_ple, layer, kp, tn), o_spec],
        out_specs=o_spec, out_shape=jax.ShapeDtypeStruct((m, n), f32),
        scratch_shapes=[pltpu.VMEM((k, tn), bf16), pltpu.VMEM((kp, tn), bf16)],
        compiler_params=_params("arbitrary", "arbitrary"), name=name)(e, p_prompt, p_sample, w_gate, w_ple, resid)


def _col_from_row(row, eye):
    n = row.shape[1]
    return jnp.sum(jnp.where(eye, jnp.broadcast_to(row, (n, n)), 0.0), axis=1, keepdims=True)


def _mlstm_prompt_body(*refs, heads, dk, dv, cl, nb):
    q_refs, k_refs, v_refs, o_refs = (refs[i * nb:(i + 1) * nb] for i in range(4))
    gt_ref, bias_ref, gml_ref, h_ref, c_out, n_out, m_out, c_s, n_s, m_s = refs[4 * nb:]
    c = pl.program_id(1)

    @pl.when(c == 0)
    def _():
        c_s[...] = jnp.zeros_like(c_s)
        n_s[...] = jnp.zeros_like(n_s)
        m_s[...] = jnp.zeros_like(m_s)

    row = lax.broadcasted_iota(jnp.int32, (cl, cl), 0)
    col = lax.broadcasted_iota(jnp.int32, (cl, cl), 1)
    causal = col <= row
    eye = col == row
    triu = (row <= col).astype(f32)
    for r in range(nb):
        gt = gt_ref[r, 0] + bias_ref[...]
        lf = jax.nn.log_sigmoid(gt[heads:2 * heads])
        b_all = jnp.dot(lf, triu, precision=HI, preferred_element_type=f32)
        for hh in range(heads):
            st = r * heads + hh
            ig = gt[hh:hh + 1]
            b_row = b_all[hh:hh + 1]
            b_col = _col_from_row(b_row, eye)
            m_prev = m_s[st]
            d = jnp.where(causal, b_col - b_row + ig, -jnp.inf)
            inter = b_col + m_prev
            m_t = jnp.maximum(inter, jnp.max(d, axis=1, keepdims=True))
            w = jnp.exp(d - m_t)
            g = jnp.exp(inter - m_t)
            q = q_refs[r][:, hh * dk:(hh + 1) * dk]
            k = k_refs[r][:, hh * dk:(hh + 1) * dk] * (dk ** -0.5)
            vb = v_refs[r][:, hh * dv:(hh + 1) * dv].astype(bf16)
            qb = q.astype(bf16)
            cmat = c_s[st]
            n_row = n_s[st]
            s = lax.dot_general(qb, k.astype(bf16), _NT, preferred_element_type=f32) * w
            num = (jnp.dot(s.astype(bf16), vb, preferred_element_type=f32)
                   + g * jnp.dot(qb, cmat.astype(bf16), preferred_element_type=f32))
            den = jnp.sum(s, axis=1, keepdims=True) + g * jnp.sum(q * n_row, axis=1, keepdims=True)
            hraw = num / jnp.maximum(jnp.abs(den), jnp.exp(-m_t))
            hn = _rms(hraw, gml_ref[hh:hh + 1, :])
            ogate = jax.nn.sigmoid(o_refs[r][:, hh * dv:(hh + 1) * dv])
            h_ref[r, :, hh * dv:(hh + 1) * dv] = (ogate * hn).astype(h_ref.dtype)
            b_last = b_row[:, cl - 1:cl]
            dl = b_last - b_row + ig
            m_new = jnp.maximum(b_last + m_prev, jnp.max(dl, axis=1, keepdims=True))
            ws_col = _col_from_row(jnp.exp(dl - m_new), eye)
            gl = jnp.exp(b_last + m_prev - m_new)
            kw = k * ws_col
            c_s[st] = gl * cmat + lax.dot_general(kw.astype(bf16), vb, _TN, preferred_element_type=f32)
            n_s[st] = gl * n_row + jnp.sum(kw, axis=0, keepdims=True)
            m_s[st] = m_new

    @pl.when(c == pl.num_programs(1) - 1)
    def _():
        for r in range(nb):
            c_out[r] = c_s[r * heads:(r + 1) * heads]
            n_out[r] = n_s[r * heads:(r + 1) * heads]
            m_out[r] = m_s[r * heads:(r + 1) * heads]


_PROMPT_SEQS_PER_STEP = 4


def _mlstm_prompt(qkvo, gates_t, bias_col, g_ml, *, batch, seq):
    heads, dk, dv, cl = ML_HEADS, ML_DK, ML_DV, math.gcd(seq, CHUNK)
    nc = seq // cl
    wq = heads * dk
    nb = math.gcd(batch, _PROMPT_SEQS_PER_STEP)
    body = functools.partial(_mlstm_prompt_body, heads=heads, dk=dk, dv=dv, cl=cl, nb=nb)
    qkvo_specs = [pl.BlockSpec((cl, wq), lambda b, c, r=r, part=part: ((b * nb + r) * nc + c, part))
                  for part in range(4) for r in range(nb)]
    return pl.pallas_call(
        body, grid=(batch // nb, nc),
        in_specs=qkvo_specs + [pl.BlockSpec((nb, 1, 2 * heads, cl), lambda b, c: (b, c, 0, 0)),
                               pl.BlockSpec((2 * heads, 1), lambda b, c: (0, 0)),
                               pl.BlockSpec((heads, dv), lambda b, c: (0, 0))],
        out_specs=[pl.BlockSpec((nb, cl, heads * dv), lambda b, c: (b, c, 0)),
                   pl.BlockSpec((nb, heads, dk, dv), lambda b, c: (b, 0, 0, 0)),
                   pl.BlockSpec((nb, heads, 1, dk), lambda b, c: (b, 0, 0, 0)),
                   pl.BlockSpec((nb, heads, 1, 1), lambda b, c: (b, 0, 0, 0))],
        out_shape=[jax.ShapeDtypeStruct((batch, seq, heads * dv), bf16),
                   jax.ShapeDtypeStruct((batch, heads, dk, dv), f32),
                   jax.ShapeDtypeStruct((batch, heads, 1, dk), f32),
                   jax.ShapeDtypeStruct((batch, heads, 1, 1), f32)],
        scratch_shapes=[pltpu.VMEM((nb * heads, dk, dv), f32), pltpu.VMEM((nb * heads, 1, dk), f32),
                        pltpu.VMEM((nb * heads, 1, 1), f32)],
        compiler_params=_params("arbitrary", "arbitrary"), name="mlstm_prompt",
    )(*([qkvo] * (4 * nb)), gates_t, bias_col, g_ml)


_SAMPLE_BLOCK = 8


def _mlstm_sample_body(*refs, heads, dk, dv, aliased):
    if aliased:
        refs = refs[:10] + refs[11:]
    (q_ref, k_ref, v_ref, o_ref, gd_ref, bias_ref, gml_ref, c0_ref, n0_ref, m0_ref,
     h_ref, c_out, n_out, m_out) = refs
    bt = q_ref.shape[0]

    @pl.when(pl.program_id(0) > 0)
    def _():
        c_out[...] = jnp.zeros_like(c_out)

    @pl.when(pl.program_id(0) == 0)
    def _():
        g8 = gd_ref[:, 0:2 * heads] + bias_ref[...]
        lane = lax.broadcasted_iota(jnp.int32, (bt, heads), 1)
        m_new = jnp.zeros((bt, heads), f32)
        for hh in range(heads):
            ig = g8[:, hh:hh + 1]
            lf = jax.nn.log_sigmoid(g8[:, heads + hh:heads + hh + 1])
            m0 = m0_ref[:, hh:hh + 1]
            m_t = jnp.maximum(lf + m0, ig)
            w = jnp.exp(ig - m_t)
            g = jnp.exp(lf + m0 - m_t)
            q = q_ref[:, hh * dk:(hh + 1) * dk]
            k = k_ref[:, hh * dk:(hh + 1) * dk] * (dk ** -0.5)
            v = v_ref[:, hh * dv:(hh + 1) * dv]
            o = o_ref[:, hh * dv:(hh + 1) * dv]
            n0 = n0_ref[:, hh, :]
            kw = k * w
            q_t = q.T
            kw_t = kw.T
            qc_rows = []
            for b in range(bt):
                cmat = c0_ref[b, hh]
                qc_rows.append(jnp.sum(q_t[:, b:b + 1] * cmat, axis=0, keepdims=True))
                c_out[b, hh] = g[b:b + 1, :] * cmat + kw_t[:, b:b + 1] * v[b:b + 1, :]
            qc = jnp.concatenate(qc_rows, axis=0)
            s = jnp.sum(q * k, axis=1, keepdims=True) * w
            num = s * v + g * qc
            den = s + g * jnp.sum(q * n0, axis=1, keepdims=True)
            hraw = num / jnp.maximum(jnp.abs(den), jnp.exp(-m_t))
            hn = _rms(hraw, gml_ref[hh:hh + 1, :])
            h_ref[:, hh * dv:(hh + 1) * dv] = jax.nn.sigmoid(o) * hn
            n_out[:, hh, :] = g * n0 + kw
            m_new = jnp.where(lane == hh, m_t, m_new)
        m_out[...] = m_new


def _stacked_state_grid(layer, b, stack):
    n_l = DEPTH if stack is None else 1
    row = lambda l, i: jnp.where(l == 0, i, b - 1)
    out_layer = lambda l: (layer + l) % DEPTH
    return (n_l, b), row, out_layer


def _mlstm_sample(qkvo, gd, bias_row, g_ml, c0, n0, m0, *, layer, row0, c_stack=None):
    heads, dk, dv = ML_HEADS, ML_DK, ML_DV
    b = c0.shape[1]
    bt = _SAMPLE_BLOCK
    wq = heads * dk
    blk0 = row0 // bt
    grid, row, out_layer = _stacked_state_grid(layer, b // bt, c_stack)
    aliased = c_stack is not None
    body = functools.partial(_mlstm_sample_body, heads=heads, dk=dk, dv=dv, aliased=aliased)
    in_specs = [pl.BlockSpec((bt, wq), lambda l, i, c=c: (blk0 + row(l, i), c)) for c in range(4)]
    in_specs += [pl.BlockSpec((bt, gd.shape[-1]), lambda l, i: (blk0 + row(l, i), 0)),
                 pl.BlockSpec((1, 2 * heads), lambda l, i: (0, 0)),
                 pl.BlockSpec((heads, dv), lambda l, i: (0, 0)),
                 pl.BlockSpec((None, bt, heads, dk, dv), lambda l, i: (layer, row(l, i), 0, 0, 0)),
                 pl.BlockSpec((None, bt, heads, dk), lambda l, i: (layer, row(l, i), 0, 0)),
                 pl.BlockSpec((None, bt, heads), lambda l, i: (layer, row(l, i), 0))]
    args = [qkvo, qkvo, qkvo, qkvo, gd, bias_row, g_ml, c0, n0, m0]
    if aliased:
        in_specs.append(pl.BlockSpec(memory_space=pl.ANY))
        args.append(c_stack)
    return pl.pallas_call(
        body, grid=grid, in_specs=in_specs,
        out_specs=[pl.BlockSpec((bt, heads * dv), lambda l, i: (row(l, i), 0)),
                   pl.BlockSpec((None, bt, heads, dk, dv), lambda l, i: (out_layer(l), i, 0, 0, 0)),
                   pl.BlockSpec((bt, heads, dk), lambda l, i: (row(l, i), 0, 0)),
                   pl.BlockSpec((bt, heads), lambda l, i: (row(l, i), 0))],
        out_shape=[jax.ShapeDtypeStruct((b, heads * dv), f32),
                   jax.ShapeDtypeStruct((DEPTH, b, heads, dk, dv), f32),
                   jax.ShapeDtypeStruct((b, heads, dk), f32),
                   jax.ShapeDtypeStruct((b, heads), f32)],
        input_output_aliases={10: 1} if aliased else {},
        compiler_params=_params("arbitrary", "arbitrary"), name="mlstm_sample",
    )(*args)


def _s5_advance(x, l1, l2):
    return l1 * x + l2 * pltpu.roll(x, x.shape[-1] // 2, axis=1)


def _toeplitz_operator(krow):
    ch, tc = krow.shape
    lane = lax.broadcasted_iota(jnp.int32, krow.shape, 1)
    blocks = [krow] + [jnp.where(lane >= ch * s, pltpu.roll(krow, ch * s, axis=1), 0.0)
                       for s in range(1, tc // ch)]
    return jnp.concatenate(blocks, axis=0).astype(bf16)


def _s5_body(u_ref, m_ref, w_ref, v_ref, l1_ref, l2_ref, x0_ref, y_ref, x_out, xs_s, *, nc, batch):
    u = u_ref[0]
    xin = jnp.dot(u, w_ref[0], preferred_element_type=f32)
    l1 = l1_ref[0]
    l2 = l2_ref[0]
    x = x0_ref[0]
    for k in range(nc):
        sl = slice(k * batch, (k + 1) * batch)
        xs_s[sl, :] = x
        x = _s5_advance(x, l1, l2) + xin[sl, :]
    x_out[0] = x
    y_ref[0] = (jnp.dot(u, _toeplitz_operator(m_ref[0]), preferred_element_type=f32)
                + jnp.dot(xs_s[...].astype(bf16), v_ref[0], preferred_element_type=f32))


def _s5_scan(u_g, mats, x0, *, nc, batch, layer):
    m_mat, w_pk, v_pk, l1, l2 = mats
    g, rows, tc = u_g.shape
    p2 = w_pk.shape[-1]
    blk = lambda *s: pl.BlockSpec((1,) + s, lambda i: (i,) + (0,) * len(s))
    op = lambda *s: pl.BlockSpec((None, 1) + s, lambda i: (layer, i) + (0,) * len(s))
    body = functools.partial(_s5_body, nc=nc, batch=batch)
    return pl.pallas_call(
        body, grid=(g,),
        in_specs=[blk(rows, tc), op(S5_CH, tc), op(tc, p2), op(p2, tc), op(1, p2), op(1, p2), blk(batch, p2)],
        out_specs=[blk(rows, tc), blk(batch, p2)],
        out_shape=[jax.ShapeDtypeStruct((g, rows, tc), f32), jax.ShapeDtypeStruct((g, batch, p2), f32)],
        scratch_shapes=[pltpu.VMEM((rows, p2), f32)],
        compiler_params=_params("parallel"), name=f"s5_scan_t{tc // S5_CH}",
    )(u_g, m_mat, w_pk, v_pk, l1, l2, x0)


def _s5_prompt_body(x_ref, m_ref, w_ref, v_ref, l1_ref, l2_ref, y_ref, x_out, u_s, y_s, xin_s, xs_s,
                    *, gw, batch, seq, t, ch):
    nc = seq // t
    for b in range(batch):
        for tt in range(t):
            blk = x_ref[pl.ds(b * seq + tt, nc, stride=t), :]
            for gl in range(gw):
                u_s[gl, b * nc:(b + 1) * nc, tt * ch:(tt + 1) * ch] = blk[:, gl * ch:(gl + 1) * ch]
    for gl in range(gw):
        xin_s[gl] = jnp.dot(u_s[gl].astype(bf16), w_ref[gl], preferred_element_type=f32)
    xs = [jnp.zeros((batch, xin_s.shape[-1]), f32)] * gw
    for k in range(nc):
        for gl in range(gw):
            xs_s[gl, pl.ds(k, batch, stride=nc), :] = xs[gl]
            xs[gl] = _s5_advance(xs[gl], l1_ref[gl], l2_ref[gl]) + xin_s[gl, pl.ds(k, batch, stride=nc), :]
    for gl in range(gw):
        x_out[gl] = xs[gl]
        y_s[gl] = (jnp.dot(u_s[gl].astype(bf16), _toeplitz_operator(m_ref[gl]), preferred_element_type=f32)
                   + jnp.dot(xs_s[gl].astype(bf16), v_ref[gl], preferred_element_type=f32))
    for b in range(batch):
        for tt in range(t):
            y_ref[pl.ds(b * seq + tt, nc, stride=t), :] = jnp.concatenate(
                [y_s[gl, b * nc:(b + 1) * nc, tt * ch:(tt + 1) * ch] for gl in range(gw)], axis=1)


def _s5_prompt(uzx, mats, *, batch, seq, layer):
    m_mat, w_pk, v_pk, l1, l2 = mats
    _, g, tc, p2 = w_pk.shape
    ch = S5_CH
    t = tc // ch
    gw = 128 // ch
    rows = batch * (seq // t)
    tp = batch * seq
    win = lambda *s: pl.BlockSpec((gw,) + s, lambda i: (i,) + (0,) * len(s))
    op = lambda *s: pl.BlockSpec((None, gw) + s, lambda i: (layer, i) + (0,) * len(s))
    body = functools.partial(_s5_prompt_body, gw=gw, batch=batch, seq=seq, t=t, ch=ch)
    return pl.pallas_call(
        body, grid=(g // gw,),
        in_specs=[pl.BlockSpec((tp, gw * ch), lambda i: (0, i)), op(ch, tc), op(tc, p2), op(p2, tc),
                  op(1, p2), op(1, p2)],
        out_specs=[pl.BlockSpec((tp, gw * ch), lambda i: (0, i)), win(batch, p2)],
        out_shape=[jax.ShapeDtypeStruct((tp, g * ch), f32), jax.ShapeDtypeStruct((g, batch, p2), f32)],
        scratch_shapes=[pltpu.VMEM((gw, rows, tc), f32), pltpu.VMEM((gw, rows, tc), f32),
                        pltpu.VMEM((gw, rows, p2), f32), pltpu.VMEM((gw, rows, p2), f32)],
        compiler_params=_params("parallel"), name="s5_prompt",
    )(uzx, m_mat, w_pk, v_pk, l1, l2)


def _s5_matrices(lam_re, lam_im, log_dt, b_re, b_im, c_re, c_im, t):
    g, p = lam_re.shape
    ch = b_re.shape[-1]
    dt = jnp.exp(log_dt)[:, None]
    ar, ai = lam_re * dt, lam_im * dt

    def powers(tau):
        mag = jnp.exp(ar[:, None, :] * tau[None, :, None])
        ang = ai[:, None, :] * tau[None, :, None]
        return mag * jnp.cos(ang), mag * jnp.sin(ang)

    lbr, lbi = jnp.exp(ar) * jnp.cos(ai), jnp.exp(ar) * jnp.sin(ai)
    den = lam_re * lam_re + lam_im * lam_im
    fr = ((lbr - 1.0) * lam_re + lbi * lam_im) / den
    fi = (lbi * lam_re - (lbr - 1.0) * lam_im) / den
    bbr = jnp.swapaxes(fr[..., None] * b_re - fi[..., None] * b_im, 1, 2)
    bbi = jnp.swapaxes(fr[..., None] * b_im + fi[..., None] * b_re, 1, 2)
    cbr = c_re[:, :, None, :] * bbr[:, None, :, :] - c_im[:, :, None, :] * bbi[:, None, :, :]
    cbi = c_re[:, :, None, :] * bbi[:, None, :, :] + c_im[:, :, None, :] * bbr[:, None, :, :]
    steps = jnp.arange(t, dtype=f32)
    lr, li = powers(steps)
    krow = jnp.einsum("gcdp,gtp->gdtc", jnp.concatenate([cbr, -cbi], axis=-1),
                      jnp.concatenate([lr, li], axis=-1), precision=lax.Precision.HIGH).reshape(g, ch, t * ch)
    pr, pi = powers(t - 1.0 - steps)
    w_re = (pr[:, :, None, :] * bbr[:, None, :, :] - pi[:, :, None, :] * bbi[:, None, :, :]).reshape(g, t * ch, p)
    w_im = (pr[:, :, None, :] * bbi[:, None, :, :] + pi[:, :, None, :] * bbr[:, None, :, :]).reshape(g, t * ch, p)
    qr, qi = powers(steps + 1.0)
    qr, qi = jnp.swapaxes(qr, 1, 2)[..., None], jnp.swapaxes(qi, 1, 2)[..., None]
    ctr, cti = jnp.swapaxes(c_re, 1, 2)[:, :, None, :], jnp.swapaxes(c_im, 1, 2)[:, :, None, :]
    v_re = (ctr * qr - cti * qi).reshape(g, p, t * ch)
    v_im = -(ctr * qi + cti * qr).reshape(g, p, t * ch)
    ltr, lti = powers(jnp.full((1,), float(t), f32))
    w_pk = jnp.concatenate([w_re, w_im], axis=-1).astype(bf16)
    v_pk = jnp.concatenate([v_re, v_im], axis=1).astype(bf16)
    chunk_ops = (krow, w_pk, v_pk, jnp.concatenate([ltr, ltr], axis=-1), jnp.concatenate([-lti, lti], axis=-1))
    lb_r, lb_i = lbr[:, None, :], lbi[:, None, :]
    step_ops = (krow[:, :, :ch], w_pk[:, (t - 1) * ch:, :], v_pk[:, :, :ch],
                jnp.concatenate([lb_r, lb_r], axis=-1), jnp.concatenate([-lb_i, lb_i], axis=-1))
    return chunk_ops, step_ops


def _s5_glu_body(yp_ref, ys_ref, u_ref, d_ref, w_ref, b_ref, g_ref, o_ref, *, n_keep):
    last = pl.num_programs(0) - 1

    def run(y_raw):
        y5 = jax.nn.gelu(y_raw + d_ref[...] * u_ref[...])
        gate = jax.nn.sigmoid(jnp.dot(y5.astype(bf16), w_ref[...].astype(bf16), preferred_element_type=f32)
                              + b_ref[...])
        o_ref[...] = _rms(y5 * gate, g_ref[...]).astype(o_ref.dtype)

    @pl.when(pl.program_id(0) < last)
    def _():
        run(yp_ref[...])

    @pl.when(pl.program_id(0) == last)
    def _():
        run(_last_tile(yp_ref, ys_ref[...], n_keep))


def _s5_glu(y_prompt, y_sample, uzx, d_skip, w_glu, b_glu, g_s5, *, layer, tm):
    tp, wdt = y_prompt.shape
    bs = y_sample.shape[0]
    n_tiles, n_keep = _split_rows(tp, bs, tm)
    row = lambda i: (i, 0)
    fix = lambda i: (0, 0)
    return pl.pallas_call(
        functools.partial(_s5_glu_body, n_keep=n_keep), grid=(n_tiles,),
        in_specs=[pl.BlockSpec((tm, wdt), row), pl.BlockSpec((bs, wdt), fix), pl.BlockSpec((tm, wdt), row),
                  pl.BlockSpec((1, wdt), fix), pl.BlockSpec((None, wdt, wdt), lambda i: (layer, 0, 0)),
                  pl.BlockSpec((1, wdt), fix), pl.BlockSpec((1, wdt), fix)],
        out_specs=pl.BlockSpec((tm, wdt), row), out_shape=jax.ShapeDtypeStruct((tp + bs, wdt), bf16),
        compiler_params=_params("parallel"), name="s5_glu",
    )(y_prompt, y_sample, uzx, d_skip, w_glu, b_glu, g_s5)


def _ssd_prompt_body(xbc_ref, z_ref, gd_ref, dtt_ref, cw_ref, cb_ref, dtb_row, dtb_col, alog_row, alog_col,
                     dskip_ref, gssd_ref, y_ref, s_out, s_s, xp_s, ys_s, *, heads, hd, ns, groups, cl, width):
    c = pl.program_id(1)

    @pl.when(c == 0)
    def _():
        s_s[...] = jnp.zeros_like(s_s)
        xp_s[0:8, :] = jnp.zeros((8, xp_s.shape[1]), f32)

    xp_s[8:8 + cl, :] = xbc_ref[...]
    xc = cb_ref[...] + sum(cw_ref[j:j + 1, :] * xp_s[5 + j:5 + j + cl, :] for j in range(SSD_CONV))
    xp_s[0:8, :] = xp_s[cl:cl + 8, :]
    xc = xc * jax.nn.sigmoid(xc)
    row = lax.broadcasted_iota(jnp.int32, (cl, cl), 0)
    col = lax.broadcasted_iota(jnp.int32, (cl, cl), 1)
    tril = (col <= row).astype(f32)
    triu = (row <= col).astype(f32)
    row2 = lax.broadcasted_iota(jnp.int32, (cl, 2 * cl), 0)
    lane2 = lax.broadcasted_iota(jnp.int32, (cl, 2 * cl), 1)
    left = lane2 < cl
    causal2 = jnp.where(left, lane2, lane2 - cl) <= row2
    left_row = left[0:1, :]
    top = lax.broadcasted_iota(jnp.int32, (2 * hd, 1), 0) < hd
    dt_col = jax.nn.softplus(gd_ref[:, 8:8 + heads] + dtb_row[...])
    dt_row = jax.nn.softplus(dtt_ref[0, 0] + dtb_col[...])
    cum_col = jnp.dot(tril, dt_col * -jnp.exp(alog_row[...]), precision=HI, preferred_element_type=f32)
    cum_row = jnp.dot(dt_row * -jnp.exp(alog_col[...]), triu, precision=HI, preferred_element_type=f32)
    exp_col = jnp.exp(cum_col)
    pick = lambda cols, h0: jnp.where(left, cols[:, h0:h0 + 1], cols[:, h0 + 1:h0 + 2])
    rep = heads // groups
    for gi in range(groups):
        bm = xc[:, width + gi * ns:width + (gi + 1) * ns].astype(bf16)
        cm = xc[:, width + (groups + gi) * ns:width + (groups + gi + 1) * ns].astype(bf16)
        scores = lax.dot_general(cm, bm, _NT, preferred_element_type=f32)
        scores2 = jnp.concatenate([scores, scores], axis=1)
        for h0 in range(gi * rep, (gi + 1) * rep, 2):
            lo, hi = h0 * hd, (h0 + 2) * hd
            cc2 = pick(cum_col, h0)
            cr2 = jnp.concatenate([cum_row[h0:h0 + 1, :], cum_row[h0 + 1:h0 + 2, :]], axis=1)
            seg2 = jnp.exp(jnp.where(causal2, cc2 - cr2, -jnp.inf))
            x2 = xc[:, lo:hi]
            xdt2 = x2 * pick(dt_col, h0)
            xbd = jnp.concatenate([jnp.where(left, xdt2, 0.0), jnp.where(left, 0.0, xdt2)], axis=0)
            smat2 = s_s[lo:hi, :]
            y2 = (jnp.dot((scores2 * seg2).astype(bf16), xbd.astype(bf16), preferred_element_type=f32)
                  + pick(exp_col, h0) * lax.dot_general(cm, smat2.astype(bf16), _NT, preferred_element_type=f32))
            last0 = cum_row[h0:h0 + 1, cl - 1:cl]
            last1 = cum_row[h0 + 1:h0 + 2, cl - 1:cl]
            xw2 = (xdt2 * jnp.exp(jnp.where(left_row, last0, last1) - cc2)).astype(bf16)
            s_s[lo:hi, :] = (jnp.where(top, jnp.exp(last0), jnp.exp(last1)) * smat2
                             + lax.dot_general(xw2, bm, _TN, preferred_element_type=f32))
            dsk2 = jnp.where(left_row, dskip_ref[:, h0:h0 + 1], dskip_ref[:, h0 + 1:h0 + 2])
            ys_s[:, lo:hi] = y2 + dsk2 * x2
    z = z_ref[...]
    y_ref[...] = _rms(ys_s[...] * (z * jax.nn.sigmoid(z)), gssd_ref[...]).astype(y_ref.dtype)

    @pl.when(c == pl.num_programs(1) - 1)
    def _():
        s_out[0] = s_s[...]


def _ssd_prompt(uzx, gd, dt_t, conv_w, conv_b, dt_bias, a_log, d_skip, g_ssd, *, batch, seq):
    heads, hd, ns, groups, width = SSD_HEADS, SSD_HEAD_DIM, SSD_STATE, SSD_GROUPS, SSD_WIDTH
    cl = math.gcd(seq, CHUNK)
    assert cl == hd and (heads // groups) % 2 == 0
    nc = seq // cl
    cch = SSD_CONV_CH
    rows = lambda b, c: (b * nc + c, 0)
    fix = lambda b, c: (0, 0)
    body = functools.partial(_ssd_prompt_body, heads=heads, hd=hd, ns=ns, groups=groups, cl=cl, width=width)
    return pl.pallas_call(
        body, grid=(batch, nc),
        in_specs=[pl.BlockSpec((cl, cch), lambda b, c: (b * nc + c, 1)),
                  pl.BlockSpec((cl, width), lambda b, c: (b * nc + c, 1)),
                  pl.BlockSpec((cl, gd.shape[1]), rows),
                  pl.BlockSpec((1, 1, heads, cl), lambda b, c: (b, c, 0, 0)),
                  pl.BlockSpec((SSD_CONV, cch), fix), pl.BlockSpec((1, cch), fix),
                  pl.BlockSpec((1, heads), fix), pl.BlockSpec((heads, 1), fix),
                  pl.BlockSpec((1, heads), fix), pl.BlockSpec((heads, 1), fix),
                  pl.BlockSpec((1, heads), fix), pl.BlockSpec((1, width), fix)],
        out_specs=[pl.BlockSpec((cl, width), rows),
                   pl.BlockSpec((1, heads * hd, ns), lambda b, c: (b, 0, 0))],
        out_shape=[jax.ShapeDtypeStruct((batch * seq, width), bf16),
                   jax.ShapeDtypeStruct((batch, heads * hd, ns), f32)],
        scratch_shapes=[pltpu.VMEM((heads * hd, ns), f32), pltpu.VMEM((cl + 8, cch), f32),
                        pltpu.VMEM((cl, width), f32)],
        compiler_params=_params("arbitrary", "arbitrary"), name="ssd_prompt",
    )(uzx, uzx, gd, dt_t, conv_w, conv_b.reshape(1, cch), dt_bias.reshape(1, heads), dt_bias.reshape(heads, 1),
      a_log.reshape(1, heads), a_log.reshape(heads, 1), d_skip.reshape(1, heads), g_ssd.reshape(1, width))


def _ssd_sample_body(*refs, heads, hd, ns, groups, width, aliased):
    if aliased:
        refs = refs[:10] + refs[11:]
    (x_ref, conv0_ref, gd_ref, cw_ref, cb_ref, dtb_ref, alog_ref, dskip_ref, gssd_ref, s0_ref,
     y_ref, s_out, ys_s) = refs
    bt = x_ref.shape[0]

    @pl.when(pl.program_id(0) > 0)
    def _():
        s_out[...] = jnp.zeros_like(s_out)

    @pl.when(pl.program_id(0) == 0)
    def _():
        z = x_ref[:, width:2 * width]
        xc = cb_ref[...] + cw_ref[SSD_CONV - 1:SSD_CONV, :] * x_ref[:, 2 * width:]
        for j in range(SSD_CONV - 1):
            xc = xc + cw_ref[j:j + 1, :] * conv0_ref[:, j, :]
        xc = xc * jax.nn.sigmoid(xc)
        dt = jax.nn.softplus(gd_ref[:, 8:8 + heads] + dtb_ref[...])
        ea = jnp.exp(dt * -jnp.exp(alog_ref[...]))
        rep = heads // groups
        for gi in range(groups):
            bm = xc[:, width + gi * ns:width + (gi + 1) * ns]
            cm = xc[:, width + (groups + gi) * ns:width + (groups + gi + 1) * ns]
            cb_dot = jnp.sum(cm * bm, axis=1, keepdims=True)
            cmb = cm.astype(bf16)
            for hh in range(gi * rep, (gi + 1) * rep):
                xh = xc[:, hh * hd:(hh + 1) * hd]
                xdt = xh * dt[:, hh:hh + 1]
                eah = ea[:, hh:hh + 1]
                xdt_t = xdt.T
                sc_rows = []
                for b in range(bt):
                    smat = s0_ref[b, hh]
                    sc_rows.append(lax.dot_general(cmb, smat.astype(bf16), _NT,
                                                   preferred_element_type=f32)[b:b + 1, :])
                    s_out[b, hh] = eah[b:b + 1, :] * smat + xdt_t[:, b:b + 1] * bm[b:b + 1, :]
                sc = jnp.concatenate(sc_rows, axis=0)
                ys_s[:, hh * hd:(hh + 1) * hd] = cb_dot * xdt + eah * sc + dskip_ref[:, hh:hh + 1] * xh
        y_ref[...] = _rms(ys_s[...] * (z * jax.nn.sigmoid(z)), gssd_ref[...])


def _ssd_sample(uzx, conv0, gd, conv_w, conv_b, dt_bias, a_log, d_skip, g_ssd, s0, *, layer, row0, s_stack=None):
    heads, hd, ns, groups, width = SSD_HEADS, SSD_HEAD_DIM, SSD_STATE, SSD_GROUPS, SSD_WIDTH
    cch = SSD_CONV_CH
    b = s0.shape[1]
    bt = _SAMPLE_BLOCK
    blk0 = row0 // bt
    fix = lambda l, i: (0, 0)
    grid, row, out_layer = _stacked_state_grid(layer, b // bt, s_stack)
    aliased = s_stack is not None
    body = functools.partial(_ssd_sample_body, heads=heads, hd=hd, ns=ns, groups=groups, width=width,
                             aliased=aliased)
    in_specs = [pl.BlockSpec((bt, uzx.shape[-1]), lambda l, i: (blk0 + row(l, i), 0)),
                pl.BlockSpec((None, bt, SSD_CONV - 1, cch), lambda l, i: (layer, row(l, i), 0, 0)),
                pl.BlockSpec((bt, gd.shape[-1]), lambda l, i: (blk0 + row(l, i), 0)),
                pl.BlockSpec((SSD_CONV, cch), fix), pl.BlockSpec((1, cch), fix),
                pl.BlockSpec((1, heads), fix), pl.BlockSpec((1, heads), fix), pl.BlockSpec((1, heads), fix),
                pl.BlockSpec((1, width), fix),
                pl.BlockSpec((None, bt, heads, hd, ns), lambda l, i: (layer, row(l, i), 0, 0, 0))]
    args = [uzx, conv0, gd, conv_w, conv_b.reshape(1, cch), dt_bias.reshape(1, heads), a_log.reshape(1, heads),
            d_skip.reshape(1, heads), g_ssd.reshape(1, width), s0]
    if aliased:
        in_specs.append(pl.BlockSpec(memory_space=pl.ANY))
        args.append(s_stack)
    return pl.pallas_call(
        body, grid=grid, in_specs=in_specs,
        out_specs=[pl.BlockSpec((bt, width), lambda l, i: (row(l, i), 0)),
                   pl.BlockSpec((None, bt, heads, hd, ns), lambda l, i: (out_layer(l), i, 0, 0, 0))],
        out_shape=[jax.ShapeDtypeStruct((b, width), f32),
                   jax.ShapeDtypeStruct((DEPTH, b, heads, hd, ns), f32)],
        scratch_shapes=[pltpu.VMEM((bt, width), f32)],
        input_output_aliases={10: 1} if aliased else {},
        compiler_params=_params("arbitrary", "arbitrary"), name="ssd_sample",
    )(*args)


def _router_body(h_ref, g_ref, wr_ref, br_ref, cf_ref, idx_ref, gate_ref, *, n_exp):
    cf = _rms(h_ref[...], g_ref[...])
    cf_ref[...] = cf
    logits = jnp.dot(cf, wr_ref[...], precision=HI, preferred_element_type=f32) + br_ref[...]
    lane = lax.broadcasted_iota(jnp.int32, logits.shape, 1)
    m1 = jnp.max(logits, axis=1, keepdims=True)
    i1 = jnp.min(jnp.where(logits == m1, lane, n_exp), axis=1, keepdims=True)
    rest = jnp.where(lane == i1, -jnp.inf, logits)
    m2 = jnp.max(rest, axis=1, keepdims=True)
    i2 = jnp.min(jnp.where(rest == m2, lane, n_exp), axis=1, keepdims=True)
    e2 = jnp.exp(m2 - m1)
    g1 = 1.0 / (1.0 + e2)
    two = lax.broadcasted_iota(jnp.int32, (logits.shape[0], TOP_K), 1)
    idx_ref[...] = jnp.where(two == 0, i1, i2)
    gate_ref[...] = jnp.where(two == 0, g1, e2 * g1)


def _router(h, g_ffn, w_router, b_router, *, tm):
    m, d = h.shape
    e = w_router.shape[-1]
    row = lambda i: (i, 0)
    fix = lambda i: (0, 0)
    return pl.pallas_call(
        functools.partial(_router_body, n_exp=e), grid=(m // tm,),
        in_specs=[pl.BlockSpec((tm, d), row), pl.BlockSpec((1, d), fix), pl.BlockSpec((d, e), fix),
                  pl.BlockSpec((1, e), fix)],
        out_specs=[pl.BlockSpec((tm, d), row), pl.BlockSpec((tm, TOP_K), row),
                   pl.BlockSpec((tm, TOP_K), row)],
        out_shape=[jax.ShapeDtypeStruct((m, d), f32),
                   jax.ShapeDtypeStruct((m, TOP_K), jnp.int32), jax.ShapeDtypeStruct((m, TOP_K), f32)],
        compiler_params=_params("parallel"), name="router",
    )(h, g_ffn.reshape(1, d), w_router, b_router.reshape(1, e))


def _row_copy(src_hbm, dst, sem, src_row, dst_row):
    return pltpu.make_async_copy(src_hbm.at[pl.ds(src_row, 1)], dst.at[pl.ds(dst_row, 1)], sem)


_DMA_UNROLL = 8


def _gather_body(idx_ref, x_hbm, o_ref, buf, sem, *, tg):
    i = pl.program_id(0)

    def issue(tile, slot):
        base = tile * tg

        def start(r, carry):
            _row_copy(x_hbm, buf.at[slot], sem.at[slot], idx_ref[base + r], r).start()
            return carry

        lax.fori_loop(0, tg, start, 0, unroll=_DMA_UNROLL)

    @pl.when(i == 0)
    def _():
        issue(0, 0)

    @pl.when(i + 1 < pl.num_programs(0))
    def _():
        issue(i + 1, (i + 1) % 2)

    slot = i % 2
    pltpu.make_async_copy(x_hbm.at[pl.ds(0, tg)], buf.at[slot], sem.at[slot]).wait()
    o_ref[...] = buf[slot].astype(o_ref.dtype)


def _gather_rows(x, row_idx, *, tg, out_dtype):
    r = row_idx.shape[0]
    d = x.shape[1]
    return pl.pallas_call(
        functools.partial(_gather_body, tg=tg),
        grid_spec=pltpu.PrefetchScalarGridSpec(
            num_scalar_prefetch=1, grid=(r // tg,),
            in_specs=[pl.BlockSpec(memory_space=pl.ANY)],
            out_specs=pl.BlockSpec((tg, d), lambda i, idx: (i, 0)),
            scratch_shapes=[pltpu.VMEM((2, tg, d), x.dtype), pltpu.SemaphoreType.DMA((2,))]),
        out_shape=jax.ShapeDtypeStruct((r, d), out_dtype),
        compiler_params=_params("arbitrary"), name="gather_rows")(row_idx, x)


def _gmm_body(te_ref, first_ref, next_ref, nv_ref, a_ref, *rest, n_w, swiglu, n_sub):
    w_hbm = rest[:n_w]
    o_ref, wf_ref, wb_ref, sem, run_ref = rest[n_w:]
    j = pl.program_id(0)
    tn = o_ref.shape[1]
    tm = o_ref.shape[0] // n_sub

    def tile_copies(expert, col_tile, slot):
        col = pl.multiple_of(col_tile * tn, 128)
        return [pltpu.make_async_copy(w_hbm[i].at[expert, :, pl.ds(col, tn)], wf_ref.at[slot, i], sem.at[slot, i])
                for i in range(n_w)]

    @pl.when((j == 0) & (pl.program_id(1) == 0))
    def _():
        run_ref[0] = 0
        for c in tile_copies(te_ref[0], 0, 0):
            c.start()

    for sub in range(n_sub):
        t = pl.program_id(1) * n_sub + sub
        rows = slice(sub * tm, (sub + 1) * tm)

        @pl.when((t < nv_ref[0]) & (first_ref[t] == 1))
        def _():
            slot = run_ref[0] % 2
            for c in tile_copies(te_ref[t], j, slot):
                c.wait()
            for i in range(n_w):
                wb_ref[i] = wf_ref[slot, i].astype(bf16)
            nxt = next_ref[t]

            @pl.when(nxt >= 0)
            def _():
                for c in tile_copies(te_ref[nxt], j, 1 - slot):
                    c.start()

            @pl.when((nxt < 0) & (j + 1 < pl.num_programs(0)))
            def _():
                for c in tile_copies(te_ref[0], j + 1, 1 - slot):
                    c.start()

            run_ref[0] = run_ref[0] + 1

        @pl.when(t < nv_ref[0])
        def _():
            a = a_ref[rows, :]
            if swiglu:
                g = jnp.dot(a, wb_ref[0], preferred_element_type=f32)
                u = jnp.dot(a, wb_ref[1], preferred_element_type=f32)
                o_ref[rows, :] = (g * jax.nn.sigmoid(g) * u).astype(o_ref.dtype)
            else:
                o_ref[rows, :] = jnp.dot(a, wb_ref[0], preferred_element_type=f32).astype(o_ref.dtype)

        @pl.when(t >= nv_ref[0])
        def _():
            o_ref[rows, :] = jnp.zeros((tm, tn), o_ref.dtype)


def _gmm(a_sorted, weights, tables, *, tm, tn, n_sub, swiglu, out_dtype, name):
    tile_expert, tile_first, tile_next, n_valid = tables
    r, k = a_sorted.shape
    n = weights[0].shape[-1]
    n_w = len(weights)
    assert (r // tm) % n_sub == 0
    return pl.pallas_call(
        functools.partial(_gmm_body, n_w=n_w, swiglu=swiglu, n_sub=n_sub),
        grid_spec=pltpu.PrefetchScalarGridSpec(
            num_scalar_prefetch=4, grid=(n // tn, r // (tm * n_sub)),
            in_specs=([pl.BlockSpec((tm * n_sub, k), lambda j, t, *_: (t, 0))]
                      + [pl.BlockSpec(memory_space=pl.ANY)] * n_w),
            out_specs=pl.BlockSpec((tm * n_sub, tn), lambda j, t, *_: (t, j)),
            scratch_shapes=[pltpu.VMEM((2, n_w, k, tn), f32), pltpu.VMEM((n_w, k, tn), bf16),
                            pltpu.SemaphoreType.DMA((2, n_w)), pltpu.SMEM((1,), jnp.int32)]),
        out_shape=jax.ShapeDtypeStruct((r, n), out_dtype),
        compiler_params=_params("arbitrary", "arbitrary", vmem=_VMEM_LIMIT_GMM), name=name,
    )(tile_expert, tile_first, tile_next, n_valid, a_sorted, *weights)


def _combine_body(pos_ref, y_hbm, gate_ref, r_ref, gn_ref, o_ref, e_ref, buf, sem, *, tc):
    i = pl.program_id(0)

    def issue(tile, slot):
        base = tile * tc

        def start(r, carry):
            for kk in range(TOP_K):
                _row_copy(y_hbm, buf.at[slot, kk], sem.at[slot], pos_ref[(base + r) * TOP_K + kk], r).start()
            return carry

        lax.fori_loop(0, tc, start, 0, unroll=_DMA_UNROLL)

    @pl.when(i == 0)
    def _():
        issue(0, 0)

    @pl.when(i + 1 < pl.num_programs(0))
    def _():
        issue(i + 1, (i + 1) % 2)

    slot = i % 2
    for kk in range(TOP_K):
        pltpu.make_async_copy(y_hbm.at[pl.ds(0, tc)], buf.at[slot, kk], sem.at[slot]).wait()
    gate = gate_ref[...]
    out = r_ref[...] + gate[:, 0:1] * buf[slot, 0] + gate[:, 1:2] * buf[slot, 1]
    o_ref[...] = out
    e_ref[...] = _rms(out, gn_ref[...]).astype(e_ref.dtype)


def _combine(y_sorted, pos_flat, gates, resid, g_next, *, tc):
    m, d = resid.shape
    rows = lambda i, pos: (i, 0)
    return pl.pallas_call(
        functools.partial(_combine_body, tc=tc),
        grid_spec=pltpu.PrefetchScalarGridSpec(
            num_scalar_prefetch=1, grid=(m // tc,),
            in_specs=[pl.BlockSpec(memory_space=pl.ANY), pl.BlockSpec((tc, TOP_K), rows),
                      pl.BlockSpec((tc, d), rows), pl.BlockSpec((1, d), lambda i, pos: (0, 0))],
            out_specs=[pl.BlockSpec((tc, d), rows), pl.BlockSpec((tc, d), rows)],
            scratch_shapes=[pltpu.VMEM((2, TOP_K, tc, d), f32), pltpu.SemaphoreType.DMA((2,))]),
        out_shape=[jax.ShapeDtypeStruct((m, d), f32), jax.ShapeDtypeStruct((m, d), bf16)],
        compiler_params=_params("arbitrary"), name="moe_combine",
    )(pos_flat, y_sorted, gates, resid, g_next.reshape(1, d))


def _routing_tables(top_i, n_exp, tm, n_tiles):
    m = top_i.shape[0]
    e_flat = top_i.reshape(-1)
    onehot = (e_flat[:, None] == jnp.arange(n_exp, dtype=jnp.int32)[None, :]).astype(jnp.int32)
    rank = jnp.take_along_axis(jnp.cumsum(onehot, axis=0), e_flat[:, None], axis=1)[:, 0] - 1
    counts = jnp.sum(onehot, axis=0)
    tiles_per = (counts + tm - 1) // tm
    tile_end = jnp.cumsum(tiles_per)
    tile_start = tile_end - tiles_per
    pos = tile_start[e_flat] * tm + rank
    token = jnp.arange(m * TOP_K, dtype=jnp.int32) // TOP_K
    row_token = jnp.zeros((n_tiles * tm,), jnp.int32).at[pos].set(token)
    n_valid = tile_end[-1]
    tid = jnp.minimum(jnp.arange(n_tiles, dtype=jnp.int32), n_valid - 1)
    tile_expert = jnp.sum((tid[:, None] >= tile_end[None, :]).astype(jnp.int32), axis=1)
    tile_first = jnp.concatenate([jnp.ones((1,), jnp.int32),
                                  (tile_expert[1:] != tile_expert[:-1]).astype(jnp.int32)])
    ids = jnp.arange(n_tiles, dtype=jnp.int32)
    starts = jnp.where((tile_first == 1) & (ids < n_valid), ids, n_tiles)
    later = jnp.concatenate([lax.cummin(starts, reverse=True)[1:], jnp.full((1,), n_tiles, jnp.int32)])
    tile_next = jnp.where(later >= n_tiles, -1, later).astype(jnp.int32)
    return (pos.astype(jnp.int32), row_token,
            (tile_expert.astype(jnp.int32), tile_first, tile_next, n_valid.reshape(1).astype(jnp.int32)))


def _moe_ffn(h, g_ffn, w_router, b_router, wg, wu, wd, g_next, *, tm_tok, tm, tn_up, tn_down):
    m = h.shape[0]
    n_exp = wg.shape[0]
    c_f32, top_i, top_g = _router(h, g_ffn, w_router, b_router, tm=tm_tok)
    n_tiles = 2 * (((m * TOP_K) // tm + n_exp + 1) // 2)
    pos, row_token, tables = _routing_tables(top_i, n_exp, tm, n_tiles)
    x_sorted = _gather_rows(c_f32, row_token, tg=tm, out_dtype=bf16)
    h_sorted = _gmm(x_sorted, (wg, wu), tables, tm=tm, tn=tn_up, n_sub=2, swiglu=True, out_dtype=bf16,
                    name="moe_gate_up")
    y_sorted = _gmm(h_sorted, (wd,), tables, tm=tm, tn=tn_down, n_sub=1, swiglu=False, out_dtype=f32,
                    name="moe_down")
    return _combine(y_sorted, pos, top_g, h, g_next, tc=tm_tok // 2)


def kernel(x_prompt, x_sample, state_mlstm_C, state_mlstm_n, state_mlstm_m, state_s5_re, state_s5_im, state_ssd, cache_conv, p_prompt, p_sample, g_mix, w_in, b_igate, b_fgate, g_ml, s5_lam_re, s5_lam_im, s5_log_dt, s5_b_re, s5_b_im, s5_c_re, s5_c_im, s5_d, s5_w_glu, s5_b_glu, g_s5, ssd_conv_w, ssd_conv_b, ssd_dt_bias, ssd_a_log, ssd_d, g_ssd, w_out, g_ffn, ffn_w_gate, ffn_w_up, ffn_w_down, w_router, b_router, moe_w_gate, moe_w_up, moe_w_down, g_ple, w_ple, w_ple_gate, g_final):
    bp, seq, d = x_prompt.shape
    bs = x_sample.shape[0]
    tp = bp * seq
    m = tp + bs
    tm = _token_tile(tp, bs)
    heads = ML_HEADS
    t5 = math.gcd(seq, S5_CHUNK)
    cl = math.gcd(seq, CHUNK)
    nc = seq // cl

    p_p = p_prompt.reshape(DEPTH, tp, -1)
    p_s = p_sample.reshape(DEPTH, bs, -1)
    w_in_t = jnp.swapaxes(w_in, 1, 2)
    outs_p = [[] for _ in range(7)]
    outs_s = [[] for _ in range(5)]
    c_stack = s_stack = None
    mats_p, mats_s = jax.vmap(functools.partial(_s5_matrices, t=t5))(
        s5_lam_re, s5_lam_im, s5_log_dt, s5_b_re, s5_b_im, s5_c_re, s5_c_im)

    for i in range(DEPTH):
        if i == 0:
            h, a = _rmsnorm_in(x_prompt.reshape(tp, d), x_sample.reshape(bs, d), g_mix[i], tm)
        else:
            a = _rmsnorm(h, g_mix[i], bf16, tm)
        qkvo = _mm_nt(a, w_in_t, tm=tm, tn=1024, layer=i, row_off=0, n_rows=_OFF_GATES, name="in_proj_qkvo")
        uzx = _mm_nt(a, w_in_t, tm=tm, tn=1024, layer=i, row_off=_OFF_U, n_rows=_OFF_DT - _OFF_U, name="in_proj_uzx")
        gd = _mm_gates(a, w_in_t, tm=tm, layer=i, off1=_OFF_GATES, n1=2 * heads, off2=_OFF_DT, n2=SSD_HEADS,
                       name="in_proj_gates")

        gates_t = jnp.transpose(gd[:tp, :2 * heads].reshape(bp, nc, cl, 2 * heads), (0, 1, 3, 2))
        bias8 = jnp.concatenate([b_igate[i], b_fgate[i]])
        h_ml_p, c_p, n_p, m_p = _mlstm_prompt(qkvo, gates_t, bias8.reshape(2 * heads, 1), g_ml[i], batch=bp, seq=seq)
        h_ml_s, c_stack, n_s, m_s = _mlstm_sample(qkvo, gd, bias8.reshape(1, 2 * heads), g_ml[i], state_mlstm_C,
                                                  state_mlstm_n, state_mlstm_m, layer=i, row0=tp, c_stack=c_stack)

        y_p, x5_p = _s5_prompt(uzx, mats_p, batch=bp, seq=seq, layer=i)
        s5re_p, s5im_p = x5_p[..., :S5_STATE], x5_p[..., S5_STATE:]
        u_s = jnp.transpose(uzx[tp:, :S5_WIDTH].reshape(bs, S5_GROUPS, S5_CH), (1, 0, 2)).astype(bf16)
        x0_s = jnp.swapaxes(jnp.concatenate([state_s5_re[i], state_s5_im[i]], axis=-1), 0, 1)
        y_s, x5_s = _s5_scan(u_s, mats_s, x0_s, nc=1, batch=bs, layer=i)
        s5re_s, s5im_s = x5_s[..., :S5_STATE], x5_s[..., S5_STATE:]
        y_s = jnp.transpose(y_s, (1, 0, 2)).reshape(bs, S5_WIDTH)
        y5 = _s5_glu(y_p, y_s, uzx, s5_d[i].reshape(1, S5_WIDTH), s5_w_glu,
                     s5_b_glu[i].reshape(1, S5_WIDTH), g_s5[i].reshape(1, S5_WIDTH), layer=i, tm=tm)

        dt_t = jnp.transpose(gd[:tp, 2 * heads:2 * heads + SSD_HEADS].reshape(bp, nc, cl, SSD_HEADS), (0, 1, 3, 2))
        y_ssd_p, ssd_p = _ssd_prompt(uzx, gd, dt_t, ssd_conv_w[i], ssd_conv_b[i], ssd_dt_bias[i], ssd_a_log[i],
                                     ssd_d[i], g_ssd[i], batch=bp, seq=seq)
        y_ssd_s, s_stack = _ssd_sample(uzx, cache_conv, gd, ssd_conv_w[i], ssd_conv_b[i], ssd_dt_bias[i],
                                       ssd_a_log[i], ssd_d[i], g_ssd[i], state_ssd, layer=i, row0=tp,
                                       s_stack=s_stack)
        conv_p = jnp.stack([uzx[(b + 1) * seq - (SSD_CONV - 1):(b + 1) * seq, S5_WIDTH + SSD_WIDTH:]
                            for b in range(bp)])
        conv_s = jnp.concatenate([cache_conv[i][:, 1:], uzx[tp:, S5_WIDTH + SSD_WIDTH:].reshape(bs, 1, -1)], axis=1)

        for lst, s in zip(outs_p, (c_p, n_p.reshape(bp, heads, ML_DK), m_p.reshape(bp, heads),
                                   jnp.swapaxes(s5re_p, 0, 1), jnp.swapaxes(s5im_p, 0, 1),
                                   ssd_p.reshape(bp, SSD_HEADS, SSD_HEAD_DIM, SSD_STATE), conv_p)):
            lst.append(s)
        for lst, s in zip(outs_s, (n_s, m_s, jnp.swapaxes(s5re_s, 0, 1), jnp.swapaxes(s5im_s, 0, 1), conv_s)):
            lst.append(s)

        h = _mm_mix(h_ml_p.reshape(tp, -1), h_ml_s, y5, y_ssd_p.reshape(tp, -1), y_ssd_s, w_out, h, tm=tm, tn=1024,
                    layer=i, name="out_proj")

        j = i // 2
        if i % 2 == 0:
            cn = _rmsnorm(h, g_ffn[i], bf16, tm)
            tm_up = 2 * tm if m % (2 * tm) == 0 else tm
            hid = _mm_swiglu(cn, ffn_w_gate, ffn_w_up, tm=tm_up, tn=512, layer=j, name="ffn_gate_up")
            h = _mm(hid, ffn_w_down, tm=tm // 2, tn=512, layer=j, resid=h, name="ffn_down")
            e = _rmsnorm(h, g_ple[i], bf16, tm)
        else:
            n_moe = moe_w_gate.shape[0]
            sel = lambda w: w.reshape((n_moe * N_EXPERTS,) + w.shape[2:])[j * N_EXPERTS:(j + 1) * N_EXPERTS] if n_moe > 1 else w.reshape(w.shape[1:])
            h, e = _moe_ffn(h, g_ffn[i], w_router[j], b_router[j], sel(moe_w_gate), sel(moe_w_up), sel(moe_w_down),
                            g_ple[i], tm_tok=tm, tm=256, tn_up=1024, tn_down=512)

        h = _mm_ple(e, p_p, p_s, w_ple_gate, w_ple, h, tm=tm, tn=1024, layer=i)

    y_p, y_s = _rmsnorm_out(h, g_final, tp, tm)
    n_s, m_s, s5re_s, s5im_s, conv_s = (jnp.stack(l) for l in outs_s)
    return ((y_p.reshape(bp, seq, d), y_s.reshape(bs, 1, d)) + tuple(jnp.stack(l) for l in outs_p)
            + (c_stack, n_s, m_s, s5re_s, s5im_s, s_stack, conv_s))
```

```python
import functools
import math

import jax
import jax.numpy as jnp
from jax import lax
from jax.experimental import pallas as pl
from jax.experimental.pallas import tpu as pltpu

f32 = jnp.float32
bf16 = jnp.bfloat16
HI = lax.Precision.HIGHEST

DEPTH = 2
ML_HEADS = 4
ML_DK = 256
ML_DV = 256
ML_WIDTH = ML_HEADS * ML_DV
S5_CH = 16
S5_WIDTH = 512
S5_GROUPS = 32
S5_STATE = 64
SSD_HEAD_DIM = 64
SSD_WIDTH = 512
SSD_HEADS = 8
SSD_GROUPS = 2
SSD_STATE = 128
SSD_CONV = 4
SSD_CONV_CH = 1024
CHUNK = 64
S5_CHUNK = 32
N_EXPERTS = 8
TOP_K = 2
RMS_EPS = 1e-6

_OFF_GATES = 4 * ML_WIDTH
_OFF_U = _OFF_GATES + 2 * ML_HEADS
_OFF_DT = _OFF_U + S5_WIDTH + SSD_WIDTH + SSD_CONV_CH

_VMEM_LIMIT = 56 * 1024 * 1024
_VMEM_LIMIT_GMM = 60 * 1024 * 1024
_NT = (((1,), (1,)), ((), ()))
_TN = (((0,), (0,)), ((), ()))


def _params(*sem, vmem=_VMEM_LIMIT):
    return pltpu.CompilerParams(dimension_semantics=sem, vmem_limit_bytes=vmem)


def _rms(x, g):
    return x * lax.rsqrt(jnp.mean(x * x, axis=-1, keepdims=True) + RMS_EPS) * g


def _rmsnorm_body(x_ref, g_ref, o_ref):
    o_ref[...] = _rms(x_ref[...], g_ref[...]).astype(o_ref.dtype)


def _last_tile(p_tile, s_rows, n_keep):
    return jnp.concatenate([p_tile[0:n_keep, :], s_rows], axis=0)


def _split_rows(tp, bs, tm):
    n_tiles = (tp + bs) // tm
    n_keep = tp - (n_tiles - 1) * tm
    assert n_tiles * tm == tp + bs and 0 < n_keep and n_keep + bs == tm and n_keep % 16 == 0
    return n_tiles, n_keep


def _token_tile(tp, bs):
    m = tp + bs
    for n_tiles in (10, 8, 5, 4, 2, 1):
        tm = m // n_tiles
        if m % n_tiles == 0 and tm % 32 == 0 and bs < tm and (tp - (n_tiles - 1) * tm) % 16 == 0:
            return tm
    raise ValueError("no row tiling for these token counts")


def _rmsnorm_in_body(xp_ref, xs_ref, g_ref, h_ref, a_ref, *, n_keep):
    last = pl.num_programs(0) - 1

    def run(x):
        h_ref[...] = x
        a_ref[...] = _rms(x, g_ref[...]).astype(a_ref.dtype)

    @pl.when(pl.program_id(0) < last)
    def _():
        run(xp_ref[...])

    @pl.when(pl.program_id(0) == last)
    def _():
        run(_last_tile(xp_ref, xs_ref[...], n_keep))


def _rmsnorm_in(xp, xs, g, tm):
    tp, d = xp.shape
    bs = xs.shape[0]
    n_tiles, n_keep = _split_rows(tp, bs, tm)
    row = lambda i: (i, 0)
    fix = lambda i: (0, 0)
    return pl.pallas_call(
        functools.partial(_rmsnorm_in_body, n_keep=n_keep), grid=(n_tiles,),
        in_specs=[pl.BlockSpec((tm, d), row), pl.BlockSpec((bs, d), fix), pl.BlockSpec((1, d), fix)],
        out_specs=[pl.BlockSpec((tm, d), row), pl.BlockSpec((tm, d), row)],
        out_shape=[jax.ShapeDtypeStruct((tp + bs, d), f32), jax.ShapeDtypeStruct((tp + bs, d), bf16)],
        compiler_params=_params("parallel"), name="rmsnorm_in")(xp, xs, g.reshape(1, d))


def _rmsnorm_out_body(x_ref, g_ref, yp_ref, ys_ref, *, n_keep):
    y = _rms(x_ref[...], g_ref[...])
    yp_ref[...] = y

    @pl.when(pl.program_id(0) == pl.num_programs(0) - 1)
    def _():
        ys_ref[...] = y[n_keep:, :]


def _rmsnorm_out(x, g, tp, tm):
    m, d = x.shape
    bs = m - tp
    n_tiles, n_keep = _split_rows(tp, bs, tm)
    return pl.pallas_call(
        functools.partial(_rmsnorm_out_body, n_keep=n_keep), grid=(n_tiles,),
        in_specs=[pl.BlockSpec((tm, d), lambda i: (i, 0)), pl.BlockSpec((1, d), lambda i: (0, 0))],
        out_specs=[pl.BlockSpec((tm, d), lambda i: (i, 0)), pl.BlockSpec((bs, d), lambda i: (0, 0))],
        out_shape=[jax.ShapeDtypeStruct((tp, d), f32), jax.ShapeDtypeStruct((bs, d), f32)],
        compiler_params=_params("arbitrary"), name="rmsnorm_out")(x, g.reshape(1, d))


def _rmsnorm(x, g, out_dtype, tm):
    m, d = x.shape
    return pl.pallas_call(
        _rmsnorm_body, grid=(m // tm,),
        in_specs=[pl.BlockSpec((tm, d), lambda i: (i, 0)), pl.BlockSpec((1, d), lambda i: (0, 0))],
        out_specs=pl.BlockSpec((tm, d), lambda i: (i, 0)),
        out_shape=jax.ShapeDtypeStruct((m, d), out_dtype),
        compiler_params=_params("parallel"), name="rmsnorm")(x, g.reshape(1, d))


def _cast_weight_once(w_ref, wb_ref):
    @pl.when(pl.program_id(1) == 0)
    def _():
        wb_ref[...] = w_ref[...].astype(bf16)


def _mm_plain_body(a_ref, w_ref, o_ref, wb_ref):
    _cast_weight_once(w_ref, wb_ref)
    o_ref[...] = jnp.dot(a_ref[...], wb_ref[...], preferred_element_type=f32).astype(o_ref.dtype)


def _mm_resid_body(a_ref, w_ref, r_ref, o_ref, wb_ref):
    _cast_weight_once(w_ref, wb_ref)
    o_ref[...] = r_ref[...] + jnp.dot(a_ref[...], wb_ref[...], preferred_element_type=f32)


def _mm_swiglu_body(a_ref, wg_ref, wu_ref, o_ref, wgb_ref, wub_ref):
    _cast_weight_once(wg_ref, wgb_ref)
    _cast_weight_once(wu_ref, wub_ref)
    a = a_ref[...]
    g = jnp.dot(a, wgb_ref[...], preferred_element_type=f32)
    u = jnp.dot(a, wub_ref[...], preferred_element_type=f32)
    o_ref[...] = (g * jax.nn.sigmoid(g) * u).astype(o_ref.dtype)


def _mm_ple_body(e_ref, pp_ref, ps_ref, wg_ref, wp_ref, r_ref, o_ref, wgb_ref, wpb_ref, *, n_keep):
    _cast_weight_once(wg_ref, wgb_ref)
    _cast_weight_once(wp_ref, wpb_ref)
    last = pl.num_programs(1) - 1

    def run(p):
        gate = jnp.dot(e_ref[...], wgb_ref[...], preferred_element_type=f32)
        emb = jnp.dot(p.astype(bf16), wpb_ref[...], preferred_element_type=f32)
        o_ref[...] = r_ref[...] + emb * jax.nn.sigmoid(gate)

    @pl.when(pl.program_id(1) < last)
    def _():
        run(pp_ref[...])

    @pl.when(pl.program_id(1) == last)
    def _():
        run(_last_tile(pp_ref, ps_ref[...], n_keep))


def _mm_mix_body(a1p_ref, a1s_ref, a2_ref, a3p_ref, a3s_ref, w_ref, r_ref, o_ref, wb_ref, *, n_keep):
    _cast_weight_once(w_ref, wb_ref)
    last = pl.num_programs(1) - 1
    k1 = a1p_ref.shape[1]
    k2 = a2_ref.shape[1]

    def run(a1, a3):
        acc = jnp.dot(a1, wb_ref[0:k1, :], preferred_element_type=f32)
        acc += jnp.dot(a2_ref[...], wb_ref[k1:k1 + k2, :], preferred_element_type=f32)
        acc += jnp.dot(a3, wb_ref[k1 + k2:, :], preferred_element_type=f32)
        o_ref[...] = r_ref[...] + acc

    @pl.when(pl.program_id(1) < last)
    def _():
        run(a1p_ref[...], a3p_ref[...])

    @pl.when(pl.program_id(1) == last)
    def _():
        run(_last_tile(a1p_ref, a1s_ref[...].astype(bf16), n_keep),
            _last_tile(a3p_ref, a3s_ref[...].astype(bf16), n_keep))


def _mm_mix(a1p, a1s, a2, a3p, a3s, w, resid, *, tm, tn, layer, name):
    m, n = resid.shape
    tp, k1 = a1p.shape
    bs = a1s.shape[0]
    k2, k3 = a2.shape[1], a3p.shape[1]
    n_tiles, n_keep = _split_rows(tp, bs, tm)
    rows = lambda j, i: (i, 0)
    fix = lambda j, i: (0, 0)
    o_spec = pl.BlockSpec((tm, tn), lambda j, i: (i, j))
    return pl.pallas_call(
        functools.partial(_mm_mix_body, n_keep=n_keep), grid=(n // tn, n_tiles),
        in_specs=[pl.BlockSpec((tm, k1), rows), pl.BlockSpec((bs, k1), fix), pl.BlockSpec((tm, k2), rows),
                  pl.BlockSpec((tm, k3), rows), pl.BlockSpec((bs, k3), fix),
                  _wspec(w, layer, k1 + k2 + k3, tn), o_spec],
        out_specs=o_spec, out_shape=jax.ShapeDtypeStruct((m, n), f32),
        scratch_shapes=[pltpu.VMEM((k1 + k2 + k3, tn), bf16)],
        compiler_params=_params("arbitrary", "arbitrary"), name=name)(a1p, a1s, a2, a3p, a3s, w, resid)


def _wspec(w, layer, k, tn, col_block_off=0):
    if w.ndim == 2:
        return pl.BlockSpec((k, tn), lambda j, i: (0, j + col_block_off))
    return pl.BlockSpec((None, k, tn), lambda j, i: (layer, 0, j + col_block_off))


def _mm(a, w, *, tm, tn, layer=0, n_cols=None, col_off=0, resid=None, out_dtype=f32, name="mm"):
    m, k = a.shape
    n = n_cols if n_cols is not None else w.shape[-1]
    grid = (n // tn, m // tm)
    a_spec = pl.BlockSpec((tm, k), lambda j, i: (i, 0))
    o_spec = pl.BlockSpec((tm, tn), lambda j, i: (i, j))
    w_spec = _wspec(w, layer, k, tn, col_off // tn)
    scratch = [pltpu.VMEM((k, tn), bf16)]
    if resid is None:
        body, ins, specs = _mm_plain_body, (a, w), [a_spec, w_spec]
    else:
        body, ins, specs = _mm_resid_body, (a, w, resid), [a_spec, w_spec, o_spec]
    return pl.pallas_call(
        body, grid=grid, in_specs=specs, out_specs=o_spec,
        out_shape=jax.ShapeDtypeStruct((m, n), out_dtype), scratch_shapes=scratch,
        compiler_params=_params("arbitrary", "arbitrary"), name=name)(*ins)


def _mm_nt_body(a_ref, wt_ref, o_ref, wb_ref):
    @pl.when(pl.program_id(1) == 0)
    def _():
        wb_ref[...] = wt_ref[0].T.astype(bf16)

    o_ref[...] = jnp.dot(a_ref[...], wb_ref[...], preferred_element_type=f32)


def _wt_rows(layer, k, n_rows, row_of):
    return pl.BlockSpec((pl.Element(1), pl.Element(n_rows), pl.Element(k)),
                        lambda *idx: (layer, row_of(*idx), 0))


def _mm_nt(a, wt, *, tm, tn, layer, row_off, n_rows, name):
    m, k = a.shape
    return pl.pallas_call(
        _mm_nt_body, grid=(n_rows // tn, m // tm),
        in_specs=[pl.BlockSpec((tm, k), lambda j, i: (i, 0)),
                  _wt_rows(layer, k, tn, lambda j, i: pl.multiple_of(row_off + tn * j, 8))],
        out_specs=pl.BlockSpec((tm, tn), lambda j, i: (i, j)),
        out_shape=jax.ShapeDtypeStruct((m, n_rows), f32), scratch_shapes=[pltpu.VMEM((k, tn), bf16)],
        compiler_params=_params("arbitrary", "arbitrary"), name=name)(a, wt)


def _mm_gates_body(a_ref, w1_ref, w2_ref, o_ref):
    a = a_ref[...]
    n1 = w1_ref.shape[1]
    o_ref[:, 0:n1] = lax.dot_general(a, w1_ref[0].astype(bf16), _NT, preferred_element_type=f32)
    o_ref[:, n1:] = lax.dot_general(a, w2_ref[0].astype(bf16), _NT, preferred_element_type=f32)


def _mm_gates(a, wt, *, tm, layer, off1, n1, off2, n2, name):
    m, k = a.shape
    return pl.pallas_call(
        _mm_gates_body, grid=(m // tm,),
        in_specs=[pl.BlockSpec((tm, k), lambda i: (i, 0)),
                  _wt_rows(layer, k, n1, lambda i: off1), _wt_rows(layer, k, n2, lambda i: off2)],
        out_specs=pl.BlockSpec((tm, n1 + n2), lambda i: (i, 0)),
        out_shape=jax.ShapeDtypeStruct((m, n1 + n2), f32),
        compiler_params=_params("parallel"), name=name)(a, wt, wt)


def _mm_swiglu(a, wg, wu, *, tm, tn, layer=0, name="mm_swiglu"):
    m, k = a.shape
    n = wg.shape[-1]
    a_spec = pl.BlockSpec((tm, k), lambda j, i: (i, 0))
    o_spec = pl.BlockSpec((tm, tn), lambda j, i: (i, j))
    return pl.pallas_call(
        _mm_swiglu_body, grid=(n // tn, m // tm),
        in_specs=[a_spec, _wspec(wg, layer, k, tn), _wspec(wu, layer, k, tn)], out_specs=o_spec,
        out_shape=jax.ShapeDtypeStruct((m, n), bf16),
        scratch_shapes=[pltpu.VMEM((k, tn), bf16), pltpu.VMEM((k, tn), bf16)],
        compiler_params=_params("arbitrary", "arbitrary"), name=name)(a, wg, wu)


def _mm_ple(e, p_prompt, p_sample, w_gate, w_ple, resid, *, tm, tn, layer, name="mm_ple"):
    m, k = e.shape
    _, tp, kp = p_prompt.shape
    bs = p_sample.shape[1]
    n = w_gate.shape[-1]
    n_tiles, n_keep = _split_rows(tp, bs, tm)
    o_spec = pl.BlockSpec((tm, tn), lambda j, i: (i, j))
    return pl.pallas_call(
        functools.partial(_mm_ple_body, n_keep=n_keep), grid=(n // tn, n_tiles),
        in_specs=[pl.BlockSpec((tm, k), lambda j, i: (i, 0)),
                  pl.BlockSpec((None, tm, kp), lambda j, i: (layer, i, 0)),
                  pl.BlockSpec((None, bs, kp), lambda j, i: (layer, 0, 0)),
                  _wspec(w_gate, layer, k, tn), _wspec(w_ple, layer, kp, tn), o_spec],
        out_specs=o_spec, out_shape=jax.ShapeDtypeStruct((m, n), f32),
        scratch_shapes=[pltpu.VMEM((k, tn), bf16), pltpu.VMEM((kp, tn), bf16)],
        compiler_params=_params("arbitrary", "arbitrary"), name=name)(e, p_prompt, p_sample, w_gate, w_ple, resid)


def _col_from_row(row, eye):
    n = row.shape[1]
    return jnp.sum(jnp.where(eye, jnp.broadcast_to(row, (n, n)), 0.0), axis=1, keepdims=True)


def _mlstm_prompt_body(*refs, heads, dk, dv, cl, nb):
    q_refs, k_refs, v_refs, o_refs = (refs[i * nb:(i + 1) * nb] for i in range(4))
    gt_ref, bias_ref, gml_ref, h_ref, c_out, n_out, m_out, c_s, n_s, m_s = refs[4 * nb:]
    c = pl.program_id(1)

    @pl.when(c == 0)
    def _():
        c_s[...] = jnp.zeros_like(c_s)
        n_s[...] = jnp.zeros_like(n_s)
        m_s[...] = jnp.zeros_like(m_s)

    row = lax.broadcasted_iota(jnp.int32, (cl, cl), 0)
    col = lax.broadcasted_iota(jnp.int32, (cl, cl), 1)
    causal = col <= row
    eye = col == row
    triu = (row <= col).astype(f32)
    for r in range(nb):
        gt = gt_ref[r, 0] + bias_ref[...]
        lf = jax.nn.log_sigmoid(gt[heads:2 * heads])
        b_all = jnp.dot(lf, triu, precision=HI, preferred_element_type=f32)
        for hh in range(heads):
            st = r * heads + hh
            ig = gt[hh:hh + 1]
            b_row = b_all[hh:hh + 1]
            b_col = _col_from_row(b_row, eye)
            m_prev = m_s[st]
            d = jnp.where(causal, b_col - b_row + ig, -jnp.inf)
            inter = b_col + m_prev
            m_t = jnp.maximum(inter, jnp.max(d, axis=1, keepdims=True))
            w = jnp.exp(d - m_t)
            g = jnp.exp(inter - m_t)
            q = q_refs[r][:, hh * dk:(hh + 1) * dk]
            k = k_refs[r][:, hh * dk:(hh + 1) * dk] * (dk ** -0.5)
            vb = v_refs[r][:, hh * dv:(hh + 1) * dv].astype(bf16)
            qb = q.astype(bf16)
            cmat = c_s[st]
            n_row = n_s[st]
            s = lax.dot_general(qb, k.astype(bf16), _NT, preferred_element_type=f32) * w
            num = (jnp.dot(s.astype(bf16), vb, preferred_element_type=f32)
                   + g * jnp.dot(qb, cmat.astype(bf16), preferred_element_type=f32))
            den = jnp.sum(s, axis=1, keepdims=True) + g * jnp.sum(q * n_row, axis=1, keepdims=True)
            hraw = num / jnp.maximum(jnp.abs(den), jnp.exp(-m_t))
            hn = _rms(hraw, gml_ref[hh:hh + 1, :])
            ogate = jax.nn.sigmoid(o_refs[r][:, hh * dv:(hh + 1) * dv])
            h_ref[r, :, hh * dv:(hh + 1) * dv] = (ogate * hn).astype(h_ref.dtype)
            b_last = b_row[:, cl - 1:cl]
            dl = b_last - b_row + ig
            m_new = jnp.maximum(b_last + m_prev, jnp.max(dl, axis=1, keepdims=True))
            ws_col = _col_from_row(jnp.exp(dl - m_new), eye)
            gl = jnp.exp(b_last + m_prev - m_new)
            kw = k * ws_col
            c_s[st] = gl * cmat + lax.dot_general(kw.astype(bf16), vb, _TN, preferred_element_type=f32)
            n_s[st] = gl * n_row + jnp.sum(kw, axis=0, keepdims=True)
            m_s[st] = m_new

    @pl.when(c == pl.num_programs(1) - 1)
    def _():
        for r in range(nb):
            c_out[r] = c_s[r * heads:(r + 1) * heads]
            n_out[r] = n_s[r * heads:(r + 1) * heads]
            m_out[r] = m_s[r * heads:(r + 1) * heads]


_PROMPT_SEQS_PER_STEP = 4


def _mlstm_prompt(qkvo, gates_t, bias_col, g_ml, *, batch, seq):
    heads, dk, dv, cl = ML_HEADS, ML_DK, ML_DV, math.gcd(seq, CHUNK)
    nc = seq // cl
    wq = heads * dk
    nb = math.gcd(batch, _PROMPT_SEQS_PER_STEP)
    body = functools.partial(_mlstm_prompt_body, heads=heads, dk=dk, dv=dv, cl=cl, nb=nb)
    qkvo_specs = [pl.BlockSpec((cl, wq), lambda b, c, r=r, part=part: ((b * nb + r) * nc + c, part))
                  for part in range(4) for r in range(nb)]
    return pl.pallas_call(
        body, grid=(batch // nb, nc),
        in_specs=qkvo_specs + [pl.BlockSpec((nb, 1, 2 * heads, cl), lambda b, c: (b, c, 0, 0)),
                               pl.BlockSpec((2 * heads, 1), lambda b, c: (0, 0)),
                               pl.BlockSpec((heads, dv), lambda b, c: (0, 0))],
        out_specs=[pl.BlockSpec((nb, cl, heads * dv), lambda b, c: (b, c, 0)),
                   pl.BlockSpec((nb, heads, dk, dv), lambda b, c: (b, 0, 0, 0)),
                   pl.BlockSpec((nb, heads, 1, dk), lambda b, c: (b, 0, 0, 0)),
                   pl.BlockSpec((nb, heads, 1, 1), lambda b, c: (b, 0, 0, 0))],
        out_shape=[jax.ShapeDtypeStruct((batch, seq, heads * dv), bf16),
                   jax.ShapeDtypeStruct((batch, heads, dk, dv), f32),
                   jax.ShapeDtypeStruct((batch, heads, 1, dk), f32),
                   jax.ShapeDtypeStruct((batch, heads, 1, 1), f32)],
        scratch_shapes=[pltpu.VMEM((nb * heads, dk, dv), f32), pltpu.VMEM((nb * heads, 1, dk), f32),
                        pltpu.VMEM((nb * heads, 1, 1), f32)],
        compiler_params=_params("arbitrary", "arbitrary"), name="mlstm_prompt",
    )(*([qkvo] * (4 * nb)), gates_t, bias_col, g_ml)


_SAMPLE_BLOCK = 8


def _mlstm_sample_body(*refs, heads, dk, dv, aliased):
    if aliased:
        refs = refs[:10] + refs[11:]
    (q_ref, k_ref, v_ref, o_ref, gd_ref, bias_ref, gml_ref, c0_ref, n0_ref, m0_ref,
     h_ref, c_out, n_out, m_out) = refs
    bt = q_ref.shape[0]

    @pl.when(pl.program_id(0) > 0)
    def _():
        c_out[...] = jnp.zeros_like(c_out)

    @pl.when(pl.program_id(0) == 0)
    def _():
        g8 = gd_ref[:, 0:2 * heads] + bias_ref[...]
        lane = lax.broadcasted_iota(jnp.int32, (bt, heads), 1)
        m_new = jnp.zeros((bt, heads), f32)
        for hh in range(heads):
            ig = g8[:, hh:hh + 1]
            lf = jax.nn.log_sigmoid(g8[:, heads + hh:heads + hh + 1])
            m0 = m0_ref[:, hh:hh + 1]
            m_t = jnp.maximum(lf + m0, ig)
            w = jnp.exp(ig - m_t)
            g = jnp.exp(lf + m0 - m_t)
            q = q_ref[:, hh * dk:(hh + 1) * dk]
            k = k_ref[:, hh * dk:(hh + 1) * dk] * (dk ** -0.5)
            v = v_ref[:, hh * dv:(hh + 1) * dv]
            o = o_ref[:, hh * dv:(hh + 1) * dv]
            n0 = n0_ref[:, hh, :]
            kw = k * w
            q_t = q.T
            kw_t = kw.T
            qc_rows = []
            for b in range(bt):
                cmat = c0_ref[b, hh]
                qc_rows.append(jnp.sum(q_t[:, b:b + 1] * cmat, axis=0, keepdims=True))
                c_out[b, hh] = g[b:b + 1, :] * cmat + kw_t[:, b:b + 1] * v[b:b + 1, :]
            qc = jnp.concatenate(qc_rows, axis=0)
            s = jnp.sum(q * k, axis=1, keepdims=True) * w
            num = s * v + g * qc
            den = s + g * jnp.sum(q * n0, axis=1, keepdims=True)
            hraw = num / jnp.maximum(jnp.abs(den), jnp.exp(-m_t))
            hn = _rms(hraw, gml_ref[hh:hh + 1, :])
            h_ref[:, hh * dv:(hh + 1) * dv] = jax.nn.sigmoid(o) * hn
            n_out[:, hh, :] = g * n0 + kw
            m_new = jnp.where(lane == hh, m_t, m_new)
        m_out[...] = m_new


def _stacked_state_grid(layer, b, stack):
    n_l = DEPTH if stack is None else 1
    row = lambda l, i: jnp.where(l == 0, i, b - 1)
    out_layer = lambda l: (layer + l) % DEPTH
    return (n_l, b), row, out_layer


def _mlstm_sample(qkvo, gd, bias_row, g_ml, c0, n0, m0, *, layer, row0, c_stack=None):
    heads, dk, dv = ML_HEADS, ML_DK, ML_DV
    b = c0.shape[1]
    bt = _SAMPLE_BLOCK
    wq = heads * dk
    blk0 = row0 // bt
    grid, row, out_layer = _stacked_state_grid(layer, b // bt, c_stack)
    aliased = c_stack is not None
    body = functools.partial(_mlstm_sample_body, heads=heads, dk=dk, dv=dv, aliased=aliased)
    in_specs = [pl.BlockSpec((bt, wq), lambda l, i, c=c: (blk0 + row(l, i), c)) for c in range(4)]
    in_specs += [pl.BlockSpec((bt, gd.shape[-1]), lambda l, i: (blk0 + row(l, i), 0)),
                 pl.BlockSpec((1, 2 * heads), lambda l, i: (0, 0)),
                 pl.BlockSpec((heads, dv), lambda l, i: (0, 0)),
                 pl.BlockSpec((None, bt, heads, dk, dv), lambda l, i: (layer, row(l, i), 0, 0, 0)),
                 pl.BlockSpec((None, bt, heads, dk), lambda l, i: (layer, row(l, i), 0, 0)),
                 pl.BlockSpec((None, bt, heads), lambda l, i: (layer, row(l, i), 0))]
    args = [qkvo, qkvo, qkvo, qkvo, gd, bias_row, g_ml, c0, n0, m0]
    if aliased:
        in_specs.append(pl.BlockSpec(memory_space=pl.ANY))
        args.append(c_stack)
    return pl.pallas_call(
        body, grid=grid, in_specs=in_specs,
        out_specs=[pl.BlockSpec((bt, heads * dv), lambda l, i: (row(l, i), 0)),
                   pl.BlockSpec((None, bt, heads, dk, dv), lambda l, i: (out_layer(l), i, 0, 0, 0)),
                   pl.BlockSpec((bt, heads, dk), lambda l, i: (row(l, i), 0, 0)),
                   pl.BlockSpec((bt, heads), lambda l, i: (row(l, i), 0))],
        out_shape=[jax.ShapeDtypeStruct((b, heads * dv), f32),
                   jax.ShapeDtypeStruct((DEPTH, b, heads, dk, dv), f32),
                   jax.ShapeDtypeStruct((b, heads, dk), f32),
                   jax.ShapeDtypeStruct((b, heads), f32)],
        input_output_aliases={10: 1} if aliased else {},
        compiler_params=_params("arbitrary", "arbitrary"), name="mlstm_sample",
    )(*args)


def _s5_advance(x, l1, l2):
    return l1 * x + l2 * pltpu.roll(x, x.shape[-1] // 2, axis=1)


def _toeplitz_operator(krow):
    ch, tc = krow.shape
    lane = lax.broadcasted_iota(jnp.int32, krow.shape, 1)
    blocks = [krow] + [jnp.where(lane >= ch * s, pltpu.roll(krow, ch * s, axis=1), 0.0)
                       for s in range(1, tc // ch)]
    return jnp.concatenate(blocks, axis=0).astype(bf16)


def _s5_body(u_ref, m_ref, w_ref, v_ref, l1_ref, l2_ref, x0_ref, y_ref, x_out, xs_s, *, nc, batch):
    u = u_ref[0]
    xin = jnp.dot(u, w_ref[0], preferred_element_type=f32)
    l1 = l1_ref[0]
    l2 = l2_ref[0]
    x = x0_ref[0]
    for k in range(nc):
        sl = slice(k * batch, (k + 1) * batch)
        xs_s[sl, :] = x
        x = _s5_advance(x, l1, l2) + xin[sl, :]
    x_out[0] = x
    y_ref[0] = (jnp.dot(u, _toeplitz_operator(m_ref[0]), preferred_element_type=f32)
                + jnp.dot(xs_s[...].astype(bf16), v_ref[0], preferred_element_type=f32))


def _s5_scan(u_g, mats, x0, *, nc, batch, layer):
    m_mat, w_pk, v_pk, l1, l2 = mats
    g, rows, tc = u_g.shape
    p2 = w_pk.shape[-1]
    blk = lambda *s: pl.BlockSpec((1,) + s, lambda i: (i,) + (0,) * len(s))
    op = lambda *s: pl.BlockSpec((None, 1) + s, lambda i: (layer, i) + (0,) * len(s))
    body = functools.partial(_s5_body, nc=nc, batch=batch)
    return pl.pallas_call(
        body, grid=(g,),
        in_specs=[blk(rows, tc), op(S5_CH, tc), op(tc, p2), op(p2, tc), op(1, p2), op(1, p2), blk(batch, p2)],
        out_specs=[blk(rows, tc), blk(batch, p2)],
        out_shape=[jax.ShapeDtypeStruct((g, rows, tc), f32), jax.ShapeDtypeStruct((g, batch, p2), f32)],
        scratch_shapes=[pltpu.VMEM((rows, p2), f32)],
        compiler_params=_params("parallel"), name=f"s5_scan_t{tc // S5_CH}",
    )(u_g, m_mat, w_pk, v_pk, l1, l2, x0)


def _s5_prompt_body(x_ref, m_ref, w_ref, v_ref, l1_ref, l2_ref, y_ref, x_out, u_s, y_s, xin_s, xs_s,
                    *, gw, batch, seq, t, ch):
    nc = seq // t
    for b in range(batch):
        for tt in range(t):
            blk = x_ref[pl.ds(b * seq + tt, nc, stride=t), :]
            for gl in range(gw):
                u_s[gl, b * nc:(b + 1) * nc, tt * ch:(tt + 1) * ch] = blk[:, gl * ch:(gl + 1) * ch]
    for gl in range(gw):
        xin_s[gl] = jnp.dot(u_s[gl].astype(bf16), w_ref[gl], preferred_element_type=f32)
    xs = [jnp.zeros((batch, xin_s.shape[-1]), f32)] * gw
    for k in range(nc):
        for gl in range(gw):
            xs_s[gl, pl.ds(k, batch, stride=nc), :] = xs[gl]
            xs[gl] = _s5_advance(xs[gl], l1_ref[gl], l2_ref[gl]) + xin_s[gl, pl.ds(k, batch, stride=nc), :]
    for gl in range(gw):
        x_out[gl] = xs[gl]
        y_s[gl] = (jnp.dot(u_s[gl].astype(bf16), _toeplitz_operator(m_ref[gl]), preferred_element_type=f32)
                   + jnp.dot(xs_s[gl].astype(bf16), v_ref[gl], preferred_element_type=f32))
    for b in range(batch):
        for tt in range(t):
            y_ref[pl.ds(b * seq + tt, nc, stride=t), :] = jnp.concatenate(
                [y_s[gl, b * nc:(b + 1) * nc, tt * ch:(tt + 1) * ch] for gl in range(gw)], axis=1)


def _s5_prompt(uzx, mats, *, batch, seq, layer):
    m_mat, w_pk, v_pk, l1, l2 = mats
    _, g, tc, p2 = w_pk.shape
    ch = S5_CH
    t = tc // ch
    gw = 128 // ch
    rows = batch * (seq // t)
    tp = batch * seq
    win = lambda *s: pl.BlockSpec((gw,) + s, lambda i: (i,) + (0,) * len(s))
    op = lambda *s: pl.BlockSpec((None, gw) + s, lambda i: (layer, i) + (0,) * len(s))
    body = functools.partial(_s5_prompt_body, gw=gw, batch=batch, seq=seq, t=t, ch=ch)
    return pl.pallas_call(
        body, grid=(g // gw,),
        in_specs=[pl.BlockSpec((tp, gw * ch), lambda i: (0, i)), op(ch, tc), op(tc, p2), op(p2, tc),
                  op(1, p2), op(1, p2)],
        out_specs=[pl.BlockSpec((tp, gw * ch), lambda i: (0, i)), win(batch, p2)],
        out_shape=[jax.ShapeDtypeStruct((tp, g * ch), f32), jax.ShapeDtypeStruct((g, batch, p2), f32)],
        scratch_shapes=[pltpu.VMEM((gw, rows, tc), f32), pltpu.VMEM((gw, rows, tc), f32),
                        pltpu.VMEM((gw, rows, p2), f32), pltpu.VMEM((gw, rows, p2), f32)],
        compiler_params=_params("parallel"), name="s5_prompt",
    )(uzx, m_mat, w_pk, v_pk, l1, l2)


def _s5_matrices(lam_re, lam_im, log_dt, b_re, b_im, c_re, c_im, t):
    g, p = lam_re.shape
    ch = b_re.shape[-1]
    dt = jnp.exp(log_dt)[:, None]
    ar, ai = lam_re * dt, lam_im * dt

    def powers(tau):
        mag = jnp.exp(ar[:, None, :] * tau[None, :, None])
        ang = ai[:, None, :] * tau[None, :, None]
        return mag * jnp.cos(ang), mag * jnp.sin(ang)

    lbr, lbi = jnp.exp(ar) * jnp.cos(ai), jnp.exp(ar) * jnp.sin(ai)
    den = lam_re * lam_re + lam_im * lam_im
    fr = ((lbr - 1.0) * lam_re + lbi * lam_im) / den
    fi = (lbi * lam_re - (lbr - 1.0) * lam_im) / den
    bbr = jnp.swapaxes(fr[..., None] * b_re - fi[..., None] * b_im, 1, 2)
    bbi = jnp.swapaxes(fr[..., None] * b_im + fi[..., None] * b_re, 1, 2)
    cbr = c_re[:, :, None, :] * bbr[:, None, :, :] - c_im[:, :, None, :] * bbi[:, None, :, :]
    cbi = c_re[:, :, None, :] * bbi[:, None, :, :] + c_im[:, :, None, :] * bbr[:, None, :, :]
    steps = jnp.arange(t, dtype=f32)
    lr, li = powers(steps)
    krow = jnp.einsum("gcdp,gtp->gdtc", jnp.concatenate([cbr, -cbi], axis=-1),
                      jnp.concatenate([lr, li], axis=-1), precision=lax.Precision.HIGH).reshape(g, ch, t * ch)
    pr, pi = powers(t - 1.0 - steps)
    w_re = (pr[:, :, None, :] * bbr[:, None, :, :] - pi[:, :, None, :] * bbi[:, None, :, :]).reshape(g, t * ch, p)
    w_im = (pr[:, :, None, :] * bbi[:, None, :, :] + pi[:, :, None, :] * bbr[:, None, :, :]).reshape(g, t * ch, p)
    qr, qi = powers(steps + 1.0)
    qr, qi = jnp.swapaxes(qr, 1, 2)[..., None], jnp.swapaxes(qi, 1, 2)[..., None]
    ctr, cti = jnp.swapaxes(c_re, 1, 2)[:, :, None, :], jnp.swapaxes(c_im, 1, 2)[:, :, None, :]
    v_re = (ctr * qr - cti * qi).reshape(g, p, t * ch)
    v_im = -(ctr * qi + cti * qr).reshape(g, p, t * ch)
    ltr, lti = powers(jnp.full((1,), float(t), f32))
    w_pk = jnp.concatenate([w_re, w_im], axis=-1).astype(bf16)
    v_pk = jnp.concatenate([v_re, v_im], axis=1).astype(bf16)
    chunk_ops = (krow, w_pk, v_pk, jnp.concatenate([ltr, ltr], axis=-1), jnp.concatenate([-lti, lti], axis=-1))
    lb_r, lb_i = lbr[:, None, :], lbi[:, None, :]
    step_ops = (krow[:, :, :ch], w_pk[:, (t - 1) * ch:, :], v_pk[:, :, :ch],
                jnp.concatenate([lb_r, lb_r], axis=-1), jnp.concatenate([-lb_i, lb_i], axis=-1))
    return chunk_ops, step_ops


def _s5_glu_body(yp_ref, ys_ref, u_ref, d_ref, w_ref, b_ref, g_ref, o_ref, *, n_keep):
    last = pl.num_programs(0) - 1

    def run(y_raw):
        y5 = jax.nn.gelu(y_raw + d_ref[...] * u_ref[...])
        gate = jax.nn.sigmoid(jnp.dot(y5.astype(bf16), w_ref[...].astype(bf16), preferred_element_type=f32)
                              + b_ref[...])
        o_ref[...] = _rms(y5 * gate, g_ref[...]).astype(o_ref.dtype)

    @pl.when(pl.program_id(0) < last)
    def _():
        run(yp_ref[...])

    @pl.when(pl.program_id(0) == last)
    def _():
        run(_last_tile(yp_ref, ys_ref[...], n_keep))


def _s5_glu(y_prompt, y_sample, uzx, d_skip, w_glu, b_glu, g_s5, *, layer, tm):
    tp, wdt = y_prompt.shape
    bs = y_sample.shape[0]
    n_tiles, n_keep = _split_rows(tp, bs, tm)
    row = lambda i: (i, 0)
    fix = lambda i: (0, 0)
    return pl.pallas_call(
        functools.partial(_s5_glu_body, n_keep=n_keep), grid=(n_tiles,),
        in_specs=[pl.BlockSpec((tm, wdt), row), pl.BlockSpec((bs, wdt), fix), pl.BlockSpec((tm, wdt), row),
                  pl.BlockSpec((1, wdt), fix), pl.BlockSpec((None, wdt, wdt), lambda i: (layer, 0, 0)),
                  pl.BlockSpec((1, wdt), fix), pl.BlockSpec((1, wdt), fix)],
        out_specs=pl.BlockSpec((tm, wdt), row), out_shape=jax.ShapeDtypeStruct((tp + bs, wdt), bf16),
        compiler_params=_params("parallel"), name="s5_glu",
    )(y_prompt, y_sample, uzx, d_skip, w_glu, b_glu, g_s5)


def _ssd_prompt_body(xbc_ref, z_ref, gd_ref, dtt_ref, cw_ref, cb_ref, dtb_row, dtb_col, alog_row, alog_col,
                     dskip_ref, gssd_ref, y_ref, s_out, s_s, xp_s, ys_s, *, heads, hd, ns, groups, cl, width):
    c = pl.program_id(1)

    @pl.when(c == 0)
    def _():
        s_s[...] = jnp.zeros_like(s_s)
        xp_s[0:8, :] = jnp.zeros((8, xp_s.shape[1]), f32)

    xp_s[8:8 + cl, :] = xbc_ref[...]
    xc = cb_ref[...] + sum(cw_ref[j:j + 1, :] * xp_s[5 + j:5 + j + cl, :] for j in range(SSD_CONV))
    xp_s[0:8, :] = xp_s[cl:cl + 8, :]
    xc = xc * jax.nn.sigmoid(xc)
    row = lax.broadcasted_iota(jnp.int32, (cl, cl), 0)
    col = lax.broadcasted_iota(jnp.int32, (cl, cl), 1)
    tril = (col <= row).astype(f32)
    triu = (row <= col).astype(f32)
    row2 = lax.broadcasted_iota(jnp.int32, (cl, 2 * cl), 0)
    lane2 = lax.broadcasted_iota(jnp.int32, (cl, 2 * cl), 1)
    left = lane2 < cl
    causal2 = jnp.where(left, lane2, lane2 - cl) <= row2
    left_row = left[0:1, :]
    top = lax.broadcasted_iota(jnp.int32, (2 * hd, 1), 0) < hd
    dt_col = jax.nn.softplus(gd_ref[:, 8:8 + heads] + dtb_row[...])
    dt_row = jax.nn.softplus(dtt_ref[0, 0] + dtb_col[...])
    cum_col = jnp.dot(tril, dt_col * -jnp.exp(alog_row[...]), precision=HI, preferred_element_type=f32)
    cum_row = jnp.dot(dt_row * -jnp.exp(alog_col[...]), triu, precision=HI, preferred_element_type=f32)
    exp_col = jnp.exp(cum_col)
    pick = lambda cols, h0: jnp.where(left, cols[:, h0:h0 + 1], cols[:, h0 + 1:h0 + 2])
    rep = heads // groups
    for gi in range(groups):
        bm = xc[:, width + gi * ns:width + (gi + 1) * ns].astype(bf16)
        cm = xc[:, width + (groups + gi) * ns:width + (groups + gi + 1) * ns].astype(bf16)
        scores = lax.dot_general(cm, bm, _NT, preferred_element_type=f32)
        scores2 = jnp.concatenate([scores, scores], axis=1)
        for h0 in range(gi * rep, (gi + 1) * rep, 2):
            lo, hi = h0 * hd, (h0 + 2) * hd
            cc2 = pick(cum_col, h0)
            cr2 = jnp.concatenate([cum_row[h0:h0 + 1, :], cum_row[h0 + 1:h0 + 2, :]], axis=1)
            seg2 = jnp.exp(jnp.where(causal2, cc2 - cr2, -jnp.inf))
            x2 = xc[:, lo:hi]
            xdt2 = x2 * pick(dt_col, h0)
            xbd = jnp.concatenate([jnp.where(left, xdt2, 0.0), jnp.where(left, 0.0, xdt2)], axis=0)
            smat2 = s_s[lo:hi, :]
            y2 = (jnp.dot((scores2 * seg2).astype(bf16), xbd.astype(bf16), preferred_element_type=f32)
                  + pick(exp_col, h0) * lax.dot_general(cm, smat2.astype(bf16), _NT, preferred_element_type=f32))
            last0 = cum_row[h0:h0 + 1, cl - 1:cl]
            last1 = cum_row[h0 + 1:h0 + 2, cl - 1:cl]
            xw2 = (xdt2 * jnp.exp(jnp.where(left_row, last0, last1) - cc2)).astype(bf16)
            s_s[lo:hi, :] = (jnp.where(top, jnp.exp(last0), jnp.exp(last1)) * smat2
                             + lax.dot_general(xw2, bm, _TN, preferred_element_type=f32))
            dsk2 = jnp.where(left_row, dskip_ref[:, h0:h0 + 1], dskip_ref[:, h0 + 1:h0 + 2])
            ys_s[:, lo:hi] = y2 + dsk2 * x2
    z = z_ref[...]
    y_ref[...] = _rms(ys_s[...] * (z * jax.nn.sigmoid(z)), gssd_ref[...]).astype(y_ref.dtype)

    @pl.when(c == pl.num_programs(1) - 1)
    def _():
        s_out[0] = s_s[...]


def _ssd_prompt(uzx, gd, dt_t, conv_w, conv_b, dt_bias, a_log, d_skip, g_ssd, *, batch, seq):
    heads, hd, ns, groups, width = SSD_HEADS, SSD_HEAD_DIM, SSD_STATE, SSD_GROUPS, SSD_WIDTH
    cl = math.gcd(seq, CHUNK)
    assert cl == hd and (heads // groups) % 2 == 0
    nc = seq // cl
    cch = SSD_CONV_CH
    rows = lambda b, c: (b * nc + c, 0)
    fix = lambda b, c: (0, 0)
    body = functools.partial(_ssd_prompt_body, heads=heads, hd=hd, ns=ns, groups=groups, cl=cl, width=width)
    return pl.pallas_call(
        body, grid=(batch, nc),
        in_specs=[pl.BlockSpec((cl, cch), lambda b, c: (b * nc + c, 1)),
                  pl.BlockSpec((cl, width), lambda b, c: (b * nc + c, 1)),
                  pl.BlockSpec((cl, gd.shape[1]), rows),
                  pl.BlockSpec((1, 1, heads, cl), lambda b, c: (b, c, 0, 0)),
                  pl.BlockSpec((SSD_CONV, cch), fix), pl.BlockSpec((1, cch), fix),
                  pl.BlockSpec((1, heads), fix), pl.BlockSpec((heads, 1), fix),
                  pl.BlockSpec((1, heads), fix), pl.BlockSpec((heads, 1), fix),
                  pl.BlockSpec((1, heads), fix), pl.BlockSpec((1, width), fix)],
        out_specs=[pl.BlockSpec((cl, width), rows),
                   pl.BlockSpec((1, heads * hd, ns), lambda b, c: (b, 0, 0))],
        out_shape=[jax.ShapeDtypeStruct((batch * seq, width), bf16),
                   jax.ShapeDtypeStruct((batch, heads * hd, ns), f32)],
        scratch_shapes=[pltpu.VMEM((heads * hd, ns), f32), pltpu.VMEM((cl + 8, cch), f32),
                        pltpu.VMEM((cl, width), f32)],
        compiler_params=_params("arbitrary", "arbitrary"), name="ssd_prompt",
    )(uzx, uzx, gd, dt_t, conv_w, conv_b.reshape(1, cch), dt_bias.reshape(1, heads), dt_bias.reshape(heads, 1),
      a_log.reshape(1, heads), a_log.reshape(heads, 1), d_skip.reshape(1, heads), g_ssd.reshape(1, width))


def _ssd_sample_body(*refs, heads, hd, ns, groups, width, aliased):
    if aliased:
        refs = refs[:10] + refs[11:]
    (x_ref, conv0_ref, gd_ref, cw_ref, cb_ref, dtb_ref, alog_ref, dskip_ref, gssd_ref, s0_ref,
     y_ref, s_out, ys_s) = refs
    bt = x_ref.shape[0]

    @pl.when(pl.program_id(0) > 0)
    def _():
        s_out[...] = jnp.zeros_like(s_out)

    @pl.when(pl.program_id(0) == 0)
    def _():
        z = x_ref[:, width:2 * width]
        xc = cb_ref[...] + cw_ref[SSD_CONV - 1:SSD_CONV, :] * x_ref[:, 2 * width:]
        for j in range(SSD_CONV - 1):
            xc = xc + cw_ref[j:j + 1, :] * conv0_ref[:, j, :]
        xc = xc * jax.nn.sigmoid(xc)
        dt = jax.nn.softplus(gd_ref[:, 8:8 + heads] + dtb_ref[...])
        ea = jnp.exp(dt * -jnp.exp(alog_ref[...]))
        rep = heads // groups
        for gi in range(groups):
            bm = xc[:, width + gi * ns:width + (gi + 1) * ns]
            cm = xc[:, width + (groups + gi) * ns:width + (groups + gi + 1) * ns]
            cb_dot = jnp.sum(cm * bm, axis=1, keepdims=True)
            cmb = cm.astype(bf16)
            for hh in range(gi * rep, (gi + 1) * rep):
                xh = xc[:, hh * hd:(hh + 1) * hd]
                xdt = xh * dt[:, hh:hh + 1]
                eah = ea[:, hh:hh + 1]
                xdt_t = xdt.T
                sc_rows = []
                for b in range(bt):
                    smat = s0_ref[b, hh]
                    sc_rows.append(lax.dot_general(cmb, smat.astype(bf16), _NT,
                                                   preferred_element_type=f32)[b:b + 1, :])
                    s_out[b, hh] = eah[b:b + 1, :] * smat + xdt_t[:, b:b + 1] * bm[b:b + 1, :]
                sc = jnp.concatenate(sc_rows, axis=0)
                ys_s[:, hh * hd:(hh + 1) * hd] = cb_dot * xdt + eah * sc + dskip_ref[:, hh:hh + 1] * xh
        y_ref[...] = _rms(ys_s[...] * (z * jax.nn.sigmoid(z)), gssd_ref[...])


def _ssd_sample(uzx, conv0, gd, conv_w, conv_b, dt_bias, a_log, d_skip, g_ssd, s0, *, layer, row0, s_stack=None):
    heads, hd, ns, groups, width = SSD_HEADS, SSD_HEAD_DIM, SSD_STATE, SSD_GROUPS, SSD_WIDTH
    cch = SSD_CONV_CH
    b = s0.shape[1]
    bt = _SAMPLE_BLOCK
    blk0 = row0 // bt
    fix = lambda l, i: (0, 0)
    grid, row, out_layer = _stacked_state_grid(layer, b // bt, s_stack)
    aliased = s_stack is not None
    body = functools.partial(_ssd_sample_body, heads=heads, hd=hd, ns=ns, groups=groups, width=width,
                             aliased=aliased)
    in_specs = [pl.BlockSpec((bt, uzx.shape[-1]), lambda l, i: (blk0 + row(l, i), 0)),
                pl.BlockSpec((None, bt, SSD_CONV - 1, cch), lambda l, i: (layer, row(l, i), 0, 0)),
                pl.BlockSpec((bt, gd.shape[-1]), lambda l, i: (blk0 + row(l, i), 0)),
                pl.BlockSpec((SSD_CONV, cch), fix), pl.BlockSpec((1, cch), fix),
                pl.BlockSpec((1, heads), fix), pl.BlockSpec((1, heads), fix), pl.BlockSpec((1, heads), fix),
                pl.BlockSpec((1, width), fix),
                pl.BlockSpec((None, bt, heads, hd, ns), lambda l, i: (layer, row(l, i), 0, 0, 0))]
    args = [uzx, conv0, gd, conv_w, conv_b.reshape(1, cch), dt_bias.reshape(1, heads), a_log.reshape(1, heads),
            d_skip.reshape(1, heads), g_ssd.reshape(1, width), s0]
    if aliased:
        in_specs.append(pl.BlockSpec(memory_space=pl.ANY))
        args.append(s_stack)
    return pl.pallas_call(
        body, grid=grid, in_specs=in_specs,
        out_specs=[pl.BlockSpec((bt, width), lambda l, i: (row(l, i), 0)),
                   pl.BlockSpec((None, bt, heads, hd, ns), lambda l, i: (out_layer(l), i, 0, 0, 0))],
        out_shape=[jax.ShapeDtypeStruct((b, width), f32),
                   jax.ShapeDtypeStruct((DEPTH, b, heads, hd, ns), f32)],
        scratch_shapes=[pltpu.VMEM((bt, width), f32)],
        input_output_aliases={10: 1} if aliased else {},
        compiler_params=_params("arbitrary", "arbitrary"), name="ssd_sample",
    )(*args)


def _router_body(h_ref, g_ref, wr_ref, br_ref, cf_ref, idx_ref, gate_ref, *, n_exp):
    cf = _rms(h_ref[...], g_ref[...])
    cf_ref[:, 0, :] = cf
    logits = jnp.dot(cf, wr_ref[...], precision=HI, preferred_element_type=f32) + br_ref[...]
    lane = lax.broadcasted_iota(jnp.int32, logits.shape, 1)
    m1 = jnp.max(logits, axis=1, keepdims=True)
    i1 = jnp.min(jnp.where(logits == m1, lane, n_exp), axis=1, keepdims=True)
    rest = jnp.where(lane == i1, -jnp.inf, logits)
    m2 = jnp.max(rest, axis=1, keepdims=True)
    i2 = jnp.min(jnp.where(rest == m2, lane, n_exp), axis=1, keepdims=True)
    e2 = jnp.exp(m2 - m1)
    g1 = 1.0 / (1.0 + e2)
    two = lax.broadcasted_iota(jnp.int32, (logits.shape[0], TOP_K), 1)
    idx_ref[...] = jnp.where(two == 0, i1, i2)
    gate_ref[...] = jnp.where(two == 0, g1, e2 * g1)


def _router(h, g_ffn, w_router, b_router, *, tm):
    m, d = h.shape
    e = w_router.shape[-1]
    row = lambda i: (i, 0)
    fix = lambda i: (0, 0)
    return pl.pallas_call(
        functools.partial(_router_body, n_exp=e), grid=(m // tm,),
        in_specs=[pl.BlockSpec((tm, d), row), pl.BlockSpec((1, d), fix), pl.BlockSpec((d, e), fix),
                  pl.BlockSpec((1, e), fix)],
        out_specs=[pl.BlockSpec((tm, 1, d), lambda i: (i, 0, 0)), pl.BlockSpec((tm, TOP_K), row),
                   pl.BlockSpec((tm, TOP_K), row)],
        out_shape=[jax.ShapeDtypeStruct((m, 1, d), f32),
                   jax.ShapeDtypeStruct((m, TOP_K), jnp.int32), jax.ShapeDtypeStruct((m, TOP_K), f32)],
        compiler_params=_params("parallel"), name="router",
    )(h, g_ffn.reshape(1, d), w_router, b_router.reshape(1, e))


def _row_copy(src_hbm, dst, sem, src_row, dst_row):
    return pltpu.make_async_copy(src_hbm.at[pl.ds(src_row, 1)], dst.at[pl.ds(dst_row, 1)], sem)


_DMA_UNROLL = 8


def _gather_body(idx_ref, x_hbm, o_ref, buf, sem, *, tg):
    i = pl.program_id(0)

    def issue(tile, slot):
        base = tile * tg

        def start(r, carry):
            _row_copy(x_hbm, buf.at[slot], sem.at[slot], idx_ref[base + r], r).start()
            return carry

        lax.fori_loop(0, tg, start, 0, unroll=_DMA_UNROLL)

    @pl.when(i == 0)
    def _():
        issue(0, 0)

    @pl.when(i + 1 < pl.num_programs(0))
    def _():
        issue(i + 1, (i + 1) % 2)

    slot = i % 2
    pltpu.make_async_copy(x_hbm.at[pl.ds(0, tg)], buf.at[slot], sem.at[slot]).wait()
    o_ref[...] = buf[slot, :, 0, :].astype(o_ref.dtype)


def _gather_rows(x, row_idx, *, tg, out_dtype):
    r = row_idx.shape[0]
    d = x.shape[-1]
    return pl.pallas_call(
        functools.partial(_gather_body, tg=tg),
        grid_spec=pltpu.PrefetchScalarGridSpec(
            num_scalar_prefetch=1, grid=(r // tg,),
            in_specs=[pl.BlockSpec(memory_space=pl.ANY)],
            out_specs=pl.BlockSpec((tg, d), lambda i, idx: (i, 0)),
            scratch_shapes=[pltpu.VMEM((2, tg, 1, d), x.dtype), pltpu.SemaphoreType.DMA((2,))]),
        out_shape=jax.ShapeDtypeStruct((r, d), out_dtype),
        compiler_params=_params("arbitrary"), name="gather_rows")(row_idx, x)


def _gmm_body(te_ref, first_ref, next_ref, nv_ref, a_ref, *rest, n_w, swiglu, n_sub):
    w_hbm = rest[:n_w]
    o_ref, wf_ref, wb_ref, sem, run_ref = rest[n_w:]
    j = pl.program_id(0)
    tn = o_ref.shape[-1]
    tm = o_ref.shape[0] // n_sub
    out_rows = (lambda rows: (rows, 0, slice(None))) if len(o_ref.shape) == 3 else (lambda rows: (rows, slice(None)))

    def tile_copies(expert, col_tile, slot):
        col = pl.multiple_of(col_tile * tn, 128)
        return [pltpu.make_async_copy(w_hbm[i].at[expert, :, pl.ds(col, tn)], wf_ref.at[slot, i], sem.at[slot, i])
                for i in range(n_w)]

    @pl.when((j == 0) & (pl.program_id(1) == 0))
    def _():
        run_ref[0] = 0
        for c in tile_copies(te_ref[0], 0, 0):
            c.start()

    for sub in range(n_sub):
        t = pl.program_id(1) * n_sub + sub
        rows = slice(sub * tm, (sub + 1) * tm)

        @pl.when((t < nv_ref[0]) & (first_ref[t] == 1))
        def _():
            slot = run_ref[0] % 2
            for c in tile_copies(te_ref[t], j, slot):
                c.wait()
            for i in range(n_w):
                wb_ref[i] = wf_ref[slot, i].astype(bf16)
            nxt = next_ref[t]

            @pl.when(nxt >= 0)
            def _():
                for c in tile_copies(te_ref[nxt], j, 1 - slot):
                    c.start()

            @pl.when((nxt < 0) & (j + 1 < pl.num_programs(0)))
            def _():
                for c in tile_copies(te_ref[0], j + 1, 1 - slot):
                    c.start()

            run_ref[0] = run_ref[0] + 1

        @pl.when(t < nv_ref[0])
        def _():
            a = a_ref[rows, :]
            if swiglu:
                g = jnp.dot(a, wb_ref[0], preferred_element_type=f32)
                u = jnp.dot(a, wb_ref[1], preferred_element_type=f32)
                res = g * jax.nn.sigmoid(g) * u
            else:
                res = jnp.dot(a, wb_ref[0], preferred_element_type=f32)
            o_ref[out_rows(rows)] = res.astype(o_ref.dtype)

        @pl.when(t >= nv_ref[0])
        def _():
            o_ref[out_rows(rows)] = jnp.zeros((tm, tn), o_ref.dtype)


def _gmm(a_sorted, weights, tables, *, tm, tn, n_sub, swiglu, out_dtype, name, row_major_out=False):
    tile_expert, tile_first, tile_next, n_valid = tables
    r, k = a_sorted.shape
    n = weights[0].shape[-1]
    n_w = len(weights)
    assert (r // tm) % n_sub == 0
    if row_major_out:
        out_spec = pl.BlockSpec((tm * n_sub, 1, tn), lambda j, t, *_: (t, 0, j))
        out_shape = jax.ShapeDtypeStruct((r, 1, n), out_dtype)
    else:
        out_spec = pl.BlockSpec((tm * n_sub, tn), lambda j, t, *_: (t, j))
        out_shape = jax.ShapeDtypeStruct((r, n), out_dtype)
    return pl.pallas_call(
        functools.partial(_gmm_body, n_w=n_w, swiglu=swiglu, n_sub=n_sub),
        grid_spec=pltpu.PrefetchScalarGridSpec(
            num_scalar_prefetch=4, grid=(n // tn, r // (tm * n_sub)),
            in_specs=([pl.BlockSpec((tm * n_sub, k), lambda j, t, *_: (t, 0))]
                      + [pl.BlockSpec(memory_space=pl.ANY)] * n_w),
            out_specs=out_spec,
            scratch_shapes=[pltpu.VMEM((2, n_w, k, tn), f32), pltpu.VMEM((n_w, k, tn), bf16),
                            pltpu.SemaphoreType.DMA((2, n_w)), pltpu.SMEM((1,), jnp.int32)]),
        out_shape=out_shape,
        compiler_params=_params("arbitrary", "arbitrary", vmem=_VMEM_LIMIT_GMM), name=name,
    )(tile_expert, tile_first, tile_next, n_valid, a_sorted, *weights)


def _combine_body(pos_ref, y_hbm, gate_ref, r_ref, gn_ref, o_ref, e_ref, buf, sem, *, tc):
    i = pl.program_id(0)

    def issue(tile, slot):
        base = tile * tc

        def start(r, carry):
            for kk in range(TOP_K):
                _row_copy(y_hbm, buf.at[slot, kk], sem.at[slot], pos_ref[(base + r) * TOP_K + kk], r).start()
            return carry

        lax.fori_loop(0, tc, start, 0, unroll=_DMA_UNROLL)

    @pl.when(i == 0)
    def _():
        issue(0, 0)

    @pl.when(i + 1 < pl.num_programs(0))
    def _():
        issue(i + 1, (i + 1) % 2)

    slot = i % 2
    for kk in range(TOP_K):
        pltpu.make_async_copy(y_hbm.at[pl.ds(0, tc)], buf.at[slot, kk], sem.at[slot]).wait()
    gate = gate_ref[...]
    out = r_ref[...] + gate[:, 0:1] * buf[slot, 0, :, 0, :] + gate[:, 1:2] * buf[slot, 1, :, 0, :]
    o_ref[...] = out
    e_ref[...] = _rms(out, gn_ref[...]).astype(e_ref.dtype)


def _combine(y_sorted, pos_flat, gates, resid, g_next, *, tc):
    m, d = resid.shape
    rows = lambda i, pos: (i, 0)
    return pl.pallas_call(
        functools.partial(_combine_body, tc=tc),
        grid_spec=pltpu.PrefetchScalarGridSpec(
            num_scalar_prefetch=1, grid=(m // tc,),
            in_specs=[pl.BlockSpec(memory_space=pl.ANY), pl.BlockSpec((tc, TOP_K), rows),
                      pl.BlockSpec((tc, d), rows), pl.BlockSpec((1, d), lambda i, pos: (0, 0))],
            out_specs=[pl.BlockSpec((tc, d), rows), pl.BlockSpec((tc, d), rows)],
            scratch_shapes=[pltpu.VMEM((2, TOP_K, tc, 1, d), f32), pltpu.SemaphoreType.DMA((2,))]),
        out_shape=[jax.ShapeDtypeStruct((m, d), f32), jax.ShapeDtypeStruct((m, d), bf16)],
        compiler_params=_params("arbitrary"), name="moe_combine",
    )(pos_flat, y_sorted, gates, resid, g_next.reshape(1, d))


def _routing_tables(top_i, n_exp, tm, n_tiles):
    m = top_i.shape[0]
    e_flat = top_i.reshape(-1)
    onehot = (e_flat[:, None] == jnp.arange(n_exp, dtype=jnp.int32)[None, :]).astype(jnp.int32)
    rank = jnp.take_along_axis(jnp.cumsum(onehot, axis=0), e_flat[:, None], axis=1)[:, 0] - 1
    counts = jnp.sum(onehot, axis=0)
    tiles_per = (counts + tm - 1) // tm
    tile_end = jnp.cumsum(tiles_per)
    tile_start = tile_end - tiles_per
    pos = tile_start[e_flat] * tm + rank
    token = jnp.arange(m * TOP_K, dtype=jnp.int32) // TOP_K
    row_token = jnp.zeros((n_tiles * tm,), jnp.int32).at[pos].set(token)
    n_valid = tile_end[-1]
    tid = jnp.minimum(jnp.arange(n_tiles, dtype=jnp.int32), n_valid - 1)
    tile_expert = jnp.sum((tid[:, None] >= tile_end[None, :]).astype(jnp.int32), axis=1)
    tile_first = jnp.concatenate([jnp.ones((1,), jnp.int32),
                                  (tile_expert[1:] != tile_expert[:-1]).astype(jnp.int32)])
    ids = jnp.arange(n_tiles, dtype=jnp.int32)
    starts = jnp.where((tile_first == 1) & (ids < n_valid), ids, n_tiles)
    later = jnp.concatenate([lax.cummin(starts, reverse=True)[1:], jnp.full((1,), n_tiles, jnp.int32)])
    tile_next = jnp.where(later >= n_tiles, -1, later).astype(jnp.int32)
    return (pos.astype(jnp.int32), row_token,
            (tile_expert.astype(jnp.int32), tile_first, tile_next, n_valid.reshape(1).astype(jnp.int32)))


def _moe_ffn(h, g_ffn, w_router, b_router, wg, wu, wd, g_next, *, tm_tok, tm, tn_up, tn_down):
    m = h.shape[0]
    n_exp = wg.shape[0]
    c_f32, top_i, top_g = _router(h, g_ffn, w_router, b_router, tm=tm_tok)
    n_tiles = 2 * (((m * TOP_K) // tm + n_exp + 1) // 2)
    pos, row_token, tables = _routing_tables(top_i, n_exp, tm, n_tiles)
    x_sorted = _gather_rows(c_f32, row_token, tg=tm, out_dtype=bf16)
    h_sorted = _gmm(x_sorted, (wg, wu), tables, tm=tm, tn=tn_up, n_sub=2, swiglu=True, out_dtype=bf16,
                    name="moe_gate_up")
    y_sorted = _gmm(h_sorted, (wd,), tables, tm=tm, tn=tn_down, n_sub=1, swiglu=False, out_dtype=f32,
                    name="moe_down", row_major_out=True)
    return _combine(y_sorted, pos, top_g, h, g_next, tc=tm_tok // 2)


def kernel(x_prompt, x_sample, state_mlstm_C, state_mlstm_n, state_mlstm_m, state_s5_re, state_s5_im, state_ssd, cache_conv, p_prompt, p_sample, g_mix, w_in, b_igate, b_fgate, g_ml, s5_lam_re, s5_lam_im, s5_log_dt, s5_b_re, s5_b_im, s5_c_re, s5_c_im, s5_d, s5_w_glu, s5_b_glu, g_s5, ssd_conv_w, ssd_conv_b, ssd_dt_bias, ssd_a_log, ssd_d, g_ssd, w_out, g_ffn, ffn_w_gate, ffn_w_up, ffn_w_down, w_router, b_router, moe_w_gate, moe_w_up, moe_w_down, g_ple, w_ple, w_ple_gate, g_final):
    bp, seq, d = x_prompt.shape
    bs = x_sample.shape[0]
    tp = bp * seq
    m = tp + bs
    tm = _token_tile(tp, bs)
    heads = ML_HEADS
    t5 = math.gcd(seq, S5_CHUNK)
    cl = math.gcd(seq, CHUNK)
    nc = seq // cl

    p_p = p_prompt.reshape(DEPTH, tp, -1)
    p_s = p_sample.reshape(DEPTH, bs, -1)
    w_in_t = jnp.swapaxes(w_in, 1, 2)
    outs_p = [[] for _ in range(7)]
    outs_s = [[] for _ in range(5)]
    c_stack = s_stack = None
    mats_p, mats_s = jax.vmap(functools.partial(_s5_matrices, t=t5))(
        s5_lam_re, s5_lam_im, s5_log_dt, s5_b_re, s5_b_im, s5_c_re, s5_c_im)

    for i in range(DEPTH):
        if i == 0:
            h, a = _rmsnorm_in(x_prompt.reshape(tp, d), x_sample.reshape(bs, d), g_mix[i], tm)
        else:
            a = _rmsnorm(h, g_mix[i], bf16, tm)
        qkvo = _mm_nt(a, w_in_t, tm=tm, tn=1024, layer=i, row_off=0, n_rows=_OFF_GATES, name="in_proj_qkvo")
        uzx = _mm_nt(a, w_in_t, tm=tm, tn=1024, layer=i, row_off=_OFF_U, n_rows=_OFF_DT - _OFF_U, name="in_proj_uzx")
        gd = _mm_gates(a, w_in_t, tm=tm, layer=i, off1=_OFF_GATES, n1=2 * heads, off2=_OFF_DT, n2=SSD_HEADS,
                       name="in_proj_gates")

        gates_t = jnp.transpose(gd[:tp, :2 * heads].reshape(bp, nc, cl, 2 * heads), (0, 1, 3, 2))
        bias8 = jnp.concatenate([b_igate[i], b_fgate[i]])
        h_ml_p, c_p, n_p, m_p = _mlstm_prompt(qkvo, gates_t, bias8.reshape(2 * heads, 1), g_ml[i], batch=bp, seq=seq)
        h_ml_s, c_stack, n_s, m_s = _mlstm_sample(qkvo, gd, bias8.reshape(1, 2 * heads), g_ml[i], state_mlstm_C,
                                                  state_mlstm_n, state_mlstm_m, layer=i, row0=tp, c_stack=c_stack)

        y_p, x5_p = _s5_prompt(uzx, mats_p, batch=bp, seq=seq, layer=i)
        s5re_p, s5im_p = x5_p[..., :S5_STATE], x5_p[..., S5_STATE:]
        u_s = jnp.transpose(uzx[tp:, :S5_WIDTH].reshape(bs, S5_GROUPS, S5_CH), (1, 0, 2)).astype(bf16)
        x0_s = jnp.swapaxes(jnp.concatenate([state_s5_re[i], state_s5_im[i]], axis=-1), 0, 1)
        y_s, x5_s = _s5_scan(u_s, mats_s, x0_s, nc=1, batch=bs, layer=i)
        s5re_s, s5im_s = x5_s[..., :S5_STATE], x5_s[..., S5_STATE:]
        y_s = jnp.transpose(y_s, (1, 0, 2)).reshape(bs, S5_WIDTH)
        y5 = _s5_glu(y_p, y_s, uzx, s5_d[i].reshape(1, S5_WIDTH), s5_w_glu,
                     s5_b_glu[i].reshape(1, S5_WIDTH), g_s5[i].reshape(1, S5_WIDTH), layer=i, tm=tm)

        dt_t = jnp.transpose(gd[:tp, 2 * heads:2 * heads + SSD_HEADS].reshape(bp, nc, cl, SSD_HEADS), (0, 1, 3, 2))
        y_ssd_p, ssd_p = _ssd_prompt(uzx, gd, dt_t, ssd_conv_w[i], ssd_conv_b[i], ssd_dt_bias[i], ssd_a_log[i],
                                     ssd_d[i], g_ssd[i], batch=bp, seq=seq)
        y_ssd_s, s_stack = _ssd_sample(uzx, cache_conv, gd, ssd_conv_w[i], ssd_conv_b[i], ssd_dt_bias[i],
                                       ssd_a_log[i], ssd_d[i], g_ssd[i], state_ssd, layer=i, row0=tp,
                                       s_stack=s_stack)
        conv_p = jnp.stack([uzx[(b + 1) * seq - (SSD_CONV - 1):(b + 1) * seq, S5_WIDTH + SSD_WIDTH:]
                            for b in range(bp)])
        conv_s = jnp.concatenate([cache_conv[i][:, 1:], uzx[tp:, S5_WIDTH + SSD_WIDTH:].reshape(bs, 1, -1)], axis=1)

        for lst, s in zip(outs_p, (c_p, n_p.reshape(bp, heads, ML_DK), m_p.reshape(bp, heads),
                                   jnp.swapaxes(s5re_p, 0, 1), jnp.swapaxes(s5im_p, 0, 1),
                                   ssd_p.reshape(bp, SSD_HEADS, SSD_HEAD_DIM, SSD_STATE), conv_p)):
            lst.append(s)
        for lst, s in zip(outs_s, (n_s, m_s, jnp.swapaxes(s5re_s, 0, 1), jnp.swapaxes(s5im_s, 0, 1), conv_s)):
            lst.append(s)

        h = _mm_mix(h_ml_p.reshape(tp, -1), h_ml_s, y5, y_ssd_p.reshape(tp, -1), y_ssd_s, w_out, h, tm=tm, tn=1024,
                    layer=i, name="out_proj")

        j = i // 2
        if i % 2 == 0:
            cn = _rmsnorm(h, g_ffn[i], bf16, tm)
            tm_up = 2 * tm if m % (2 * tm) == 0 else tm
            hid = _mm_swiglu(cn, ffn_w_gate, ffn_w_up, tm=tm_up, tn=512, layer=j, name="ffn_gate_up")
            h = _mm(hid, ffn_w_down, tm=tm // 2, tn=512, layer=j, resid=h, name="ffn_down")
            e = _rmsnorm(h, g_ple[i], bf16, tm)
        else:
            n_moe = moe_w_gate.shape[0]
            sel = lambda w: w.reshape((n_moe * N_EXPERTS,) + w.shape[2:])[j * N_EXPERTS:(j + 1) * N_EXPERTS] if n_moe > 1 else w.reshape(w.shape[1:])
            h, e = _moe_ffn(h, g_ffn[i], w_router[j], b_router[j], sel(moe_w_gate), sel(moe_w_up), sel(moe_w_down),
                            g_ple[i], tm_tok=tm, tm=256, tn_up=1024, tn_down=512)

        h = _mm_ple(e, p_p, p_s, w_ple_gate, w_ple, h, tm=tm, tn=1024, layer=i)

    y_p, y_s = _rmsnorm_out(h, g_final, tp, tm)
    n_s, m_s, s5re_s, s5im_s, conv_s = (jnp.stack(l) for l in outs_s)
    return ((y_p.reshape(bp, seq, d), y_s.reshape(bs, 1, d)) + tuple(jnp.stack(l) for l in outs_p)
            + (c_stack, n_s, m_s, s5re_s, s5im_s, s_stack, conv_s))
```

```python
import functools
import math

import jax
import jax.numpy as jnp
from jax import lax
from jax.experimental import pallas as pl
from jax.experimental.pallas import tpu as pltpu

f32 = jnp.float32
bf16 = jnp.bfloat16
HI = lax.Precision.HIGHEST

DEPTH = 2
ML_HEADS = 4
ML_DK = 256
ML_DV = 256
ML_WIDTH = ML_HEADS * ML_DV
S5_CH = 16
S5_WIDTH = 512
S5_GROUPS = 32
S5_STATE = 64
SSD_HEAD_DIM = 64
SSD_WIDTH = 512
SSD_HEADS = 8
SSD_GROUPS = 2
SSD_STATE = 128
SSD_CONV = 4
SSD_CONV_CH = 1024
CHUNK = 64
S5_CHUNK = 32
N_EXPERTS = 8
TOP_K = 2
RMS_EPS = 1e-6

_OFF_GATES = 4 * ML_WIDTH
_OFF_U = _OFF_GATES + 2 * ML_HEADS
_OFF_DT = _OFF_U + S5_WIDTH + SSD_WIDTH + SSD_CONV_CH

_VMEM_LIMIT = 56 * 1024 * 1024
_VMEM_LIMIT_GMM = 60 * 1024 * 1024
_NT = (((1,), (1,)), ((), ()))
_TN = (((0,), (0,)), ((), ()))


def _params(*sem, vmem=_VMEM_LIMIT):
    return pltpu.CompilerParams(dimension_semantics=sem, vmem_limit_bytes=vmem)


def _rms(x, g):
    return x * lax.rsqrt(jnp.mean(x * x, axis=-1, keepdims=True) + RMS_EPS) * g


def _rmsnorm_body(x_ref, g_ref, o_ref):
    o_ref[...] = _rms(x_ref[...], g_ref[...]).astype(o_ref.dtype)


def _last_tile(p_tile, s_rows, n_keep):
    return jnp.concatenate([p_tile[0:n_keep, :], s_rows], axis=0)


def _split_rows(tp, bs, tm):
    n_tiles = (tp + bs) // tm
    n_keep = tp - (n_tiles - 1) * tm
    assert n_tiles * tm == tp + bs and 0 < n_keep and n_keep + bs == tm and n_keep % 16 == 0
    return n_tiles, n_keep


def _token_tile(tp, bs):
    m = tp + bs
    for n_tiles in (10, 8, 5, 4, 2, 1):
        tm = m // n_tiles
        if m % n_tiles == 0 and tm % 32 == 0 and bs < tm and (tp - (n_tiles - 1) * tm) % 16 == 0:
            return tm
    raise ValueError("no row tiling for these token counts")


def _rmsnorm_in_body(xp_ref, xs_ref, g_ref, h_ref, a_ref, *, n_keep):
    last = pl.num_programs(0) - 1

    def run(x):
        h_ref[...] = x
        a_ref[...] = _rms(x, g_ref[...]).astype(a_ref.dtype)

    @pl.when(pl.program_id(0) < last)
    def _():
        run(xp_ref[...])

    @pl.when(pl.program_id(0) == last)
    def _():
        run(_last_tile(xp_ref, xs_ref[...], n_keep))


def _rmsnorm_in(xp, xs, g, tm):
    tp, d = xp.shape
    bs = xs.shape[0]
    n_tiles, n_keep = _split_rows(tp, bs, tm)
    row = lambda i: (i, 0)
    fix = lambda i: (0, 0)
    return pl.pallas_call(
        functools.partial(_rmsnorm_in_body, n_keep=n_keep), grid=(n_tiles,),
        in_specs=[pl.BlockSpec((tm, d), row), pl.BlockSpec((bs, d), fix), pl.BlockSpec((1, d), fix)],
        out_specs=[pl.BlockSpec((tm, d), row), pl.BlockSpec((tm, d), row)],
        out_shape=[jax.ShapeDtypeStruct((tp + bs, d), f32), jax.ShapeDtypeStruct((tp + bs, d), bf16)],
        compiler_params=_params("parallel"), name="rmsnorm_in")(xp, xs, g.reshape(1, d))


def _rmsnorm_out_body(x_ref, g_ref, yp_ref, ys_ref, *, n_keep):
    y = _rms(x_ref[...], g_ref[...])
    yp_ref[...] = y

    @pl.when(pl.program_id(0) == pl.num_programs(0) - 1)
    def _():
        ys_ref[...] = y[n_keep:, :]


def _rmsnorm_out(x, g, tp, tm):
    m, d = x.shape
    bs = m - tp
    n_tiles, n_keep = _split_rows(tp, bs, tm)
    return pl.pallas_call(
        functools.partial(_rmsnorm_out_body, n_keep=n_keep), grid=(n_tiles,),
        in_specs=[pl.BlockSpec((tm, d), lambda i: (i, 0)), pl.BlockSpec((1, d), lambda i: (0, 0))],
        out_specs=[pl.BlockSpec((tm, d), lambda i: (i, 0)), pl.BlockSpec((bs, d), lambda i: (0, 0))],
        out_shape=[jax.ShapeDtypeStruct((tp, d), f32), jax.ShapeDtypeStruct((bs, d), f32)],
        compiler_params=_params("arbitrary"), name="rmsnorm_out")(x, g.reshape(1, d))


def _rmsnorm(x, g, out_dtype, tm):
    m, d = x.shape
    return pl.pallas_call(
        _rmsnorm_body, grid=(m // tm,),
        in_specs=[pl.BlockSpec((tm, d), lambda i: (i, 0)), pl.BlockSpec((1, d), lambda i: (0, 0))],
        out_specs=pl.BlockSpec((tm, d), lambda i: (i, 0)),
        out_shape=jax.ShapeDtypeStruct((m, d), out_dtype),
        compiler_params=_params("parallel"), name="rmsnorm")(x, g.reshape(1, d))


def _cast_weight_once(w_ref, wb_ref):
    @pl.when(pl.program_id(1) == 0)
    def _():
        wb_ref[...] = w_ref[...].astype(bf16)


def _mm_plain_body(a_ref, w_ref, o_ref, wb_ref):
    _cast_weight_once(w_ref, wb_ref)
    o_ref[...] = jnp.dot(a_ref[...], wb_ref[...], preferred_element_type=f32).astype(o_ref.dtype)


def _mm_resid_body(a_ref, w_ref, r_ref, o_ref, wb_ref):
    _cast_weight_once(w_ref, wb_ref)
    o_ref[...] = r_ref[...] + jnp.dot(a_ref[...], wb_ref[...], preferred_element_type=f32)


def _mm_swiglu_body(a_ref, wg_ref, wu_ref, o_ref, wgb_ref, wub_ref):
    _cast_weight_once(wg_ref, wgb_ref)
    _cast_weight_once(wu_ref, wub_ref)
    a = a_ref[...]
    g = jnp.dot(a, wgb_ref[...], preferred_element_type=f32)
    u = jnp.dot(a, wub_ref[...], preferred_element_type=f32)
    o_ref[...] = (g * jax.nn.sigmoid(g) * u).astype(o_ref.dtype)


def _mm_ple_body(e_ref, pp_ref, ps_ref, wg_ref, wp_ref, r_ref, o_ref, wgb_ref, wpb_ref, *, n_keep):
    _cast_weight_once(wg_ref, wgb_ref)
    _cast_weight_once(wp_ref, wpb_ref)
    last = pl.num_programs(1) - 1

    def run(p):
        gate = jnp.dot(e_ref[...], wgb_ref[...], preferred_element_type=f32)
        emb = jnp.dot(p.astype(bf16), wpb_ref[...], preferred_element_type=f32)
        o_ref[...] = r_ref[...] + emb * jax.nn.sigmoid(gate)

    @pl.when(pl.program_id(1) < last)
    def _():
        run(pp_ref[...])

    @pl.when(pl.program_id(1) == last)
    def _():
        run(_last_tile(pp_ref, ps_ref[...], n_keep))


def _mm_mix_body(a1p_ref, a1s_ref, a2_ref, a3p_ref, a3s_ref, w_ref, r_ref, o_ref, wb_ref, *, n_keep):
    _cast_weight_once(w_ref, wb_ref)
    last = pl.num_programs(1) - 1
    k1 = a1p_ref.shape[1]
    k2 = a2_ref.shape[1]

    def run(a1, a3):
        acc = jnp.dot(a1, wb_ref[0:k1, :], preferred_element_type=f32)
        acc += jnp.dot(a2_ref[...], wb_ref[k1:k1 + k2, :], preferred_element_type=f32)
        acc += jnp.dot(a3, wb_ref[k1 + k2:, :], preferred_element_type=f32)
        o_ref[...] = r_ref[...] + acc

    @pl.when(pl.program_id(1) < last)
    def _():
        run(a1p_ref[...], a3p_ref[...])

    @pl.when(pl.program_id(1) == last)
    def _():
        run(_last_tile(a1p_ref, a1s_ref[...].astype(bf16), n_keep),
            _last_tile(a3p_ref, a3s_ref[...].astype(bf16), n_keep))


def _mm_mix(a1p, a1s, a2, a3p, a3s, w, resid, *, tm, tn, layer, name):
    m, n = resid.shape
    tp, k1 = a1p.shape
    bs = a1s.shape[0]
    k2, k3 = a2.shape[1], a3p.shape[1]
    n_tiles, n_keep = _split_rows(tp, bs, tm)
    rows = lambda j, i: (i, 0)
    fix = lambda j, i: (0, 0)
    o_spec = pl.BlockSpec((tm, tn), lambda j, i: (i, j))
    return pl.pallas_call(
        functools.partial(_mm_mix_body, n_keep=n_keep), grid=(n // tn, n_tiles),
        in_specs=[pl.BlockSpec((tm, k1), rows), pl.BlockSpec((bs, k1), fix), pl.BlockSpec((tm, k2), rows),
                  pl.BlockSpec((tm, k3), rows), pl.BlockSpec((bs, k3), fix),
                  _wspec(w, layer, k1 + k2 + k3, tn), o_spec],
        out_specs=o_spec, out_shape=jax.ShapeDtypeStruct((m, n), f32),
        scratch_shapes=[pltpu.VMEM((k1 + k2 + k3, tn), bf16)],
        compiler_params=_params("arbitrary", "arbitrary"), name=name)(a1p, a1s, a2, a3p, a3s, w, resid)


def _wspec(w, layer, k, tn, col_block_off=0):
    if w.ndim == 2:
        return pl.BlockSpec((k, tn), lambda j, i: (0, j + col_block_off))
    return pl.BlockSpec((None, k, tn), lambda j, i: (layer, 0, j + col_block_off))


def _mm(a, w, *, tm, tn, layer=0, n_cols=None, col_off=0, resid=None, out_dtype=f32, name="mm"):
    m, k = a.shape
    n = n_cols if n_cols is not None else w.shape[-1]
    grid = (n // tn, m // tm)
    a_spec = pl.BlockSpec((tm, k), lambda j, i: (i, 0))
    o_spec = pl.BlockSpec((tm, tn), lambda j, i: (i, j))
    w_spec = _wspec(w, layer, k, tn, col_off // tn)
    scratch = [pltpu.VMEM((k, tn), bf16)]
    if resid is None:
        body, ins, specs = _mm_plain_body, (a, w), [a_spec, w_spec]
    else:
        body, ins, specs = _mm_resid_body, (a, w, resid), [a_spec, w_spec, o_spec]
    return pl.pallas_call(
        body, grid=grid, in_specs=specs, out_specs=o_spec,
        out_shape=jax.ShapeDtypeStruct((m, n), out_dtype), scratch_shapes=scratch,
        compiler_params=_params("arbitrary", "arbitrary"), name=name)(*ins)


def _mm_nt_body(a_ref, wt_ref, o_ref, wb_ref):
    @pl.when(pl.program_id(1) == 0)
    def _():
        wb_ref[...] = wt_ref[0].T.astype(bf16)

    o_ref[...] = jnp.dot(a_ref[...], wb_ref[...], preferred_element_type=f32)


def _wt_rows(layer, k, n_rows, row_of):
    return pl.BlockSpec((pl.Element(1), pl.Element(n_rows), pl.Element(k)),
                        lambda *idx: (layer, row_of(*idx), 0))


def _mm_nt(a, wt, *, tm, tn, layer, row_off, n_rows, name):
    m, k = a.shape
    return pl.pallas_call(
        _mm_nt_body, grid=(n_rows // tn, m // tm),
        in_specs=[pl.BlockSpec((tm, k), lambda j, i: (i, 0)),
                  _wt_rows(layer, k, tn, lambda j, i: pl.multiple_of(row_off + tn * j, 8))],
        out_specs=pl.BlockSpec((tm, tn), lambda j, i: (i, j)),
        out_shape=jax.ShapeDtypeStruct((m, n_rows), f32), scratch_shapes=[pltpu.VMEM((k, tn), bf16)],
        compiler_params=_params("arbitrary", "arbitrary"), name=name)(a, wt)


def _mm_gates_body(a_ref, w1_ref, w2_ref, o_ref):
    a = a_ref[...]
    n1 = w1_ref.shape[1]
    o_ref[:, 0:n1] = lax.dot_general(a, w1_ref[0].astype(bf16), _NT, preferred_element_type=f32)
    o_ref[:, n1:] = lax.dot_general(a, w2_ref[0].astype(bf16), _NT, preferred_element_type=f32)


def _mm_gates(a, wt, *, tm, layer, off1, n1, off2, n2, name):
    m, k = a.shape
    return pl.pallas_call(
        _mm_gates_body, grid=(m // tm,),
        in_specs=[pl.BlockSpec((tm, k), lambda i: (i, 0)),
                  _wt_rows(layer, k, n1, lambda i: off1), _wt_rows(layer, k, n2, lambda i: off2)],
        out_specs=pl.BlockSpec((tm, n1 + n2), lambda i: (i, 0)),
        out_shape=jax.ShapeDtypeStruct((m, n1 + n2), f32),
        compiler_params=_params("parallel"), name=name)(a, wt, wt)


def _mm_swiglu(a, wg, wu, *, tm, tn, layer=0, name="mm_swiglu"):
    m, k = a.shape
    n = wg.shape[-1]
    a_spec = pl.BlockSpec((tm, k), lambda j, i: (i, 0))
    o_spec = pl.BlockSpec((tm, tn), lambda j, i: (i, j))
    return pl.pallas_call(
        _mm_swiglu_body, grid=(n // tn, m // tm),
        in_specs=[a_spec, _wspec(wg, layer, k, tn), _wspec(wu, layer, k, tn)], out_specs=o_spec,
        out_shape=jax.ShapeDtypeStruct((m, n), bf16),
        scratch_shapes=[pltpu.VMEM((k, tn), bf16), pltpu.VMEM((k, tn), bf16)],
        compiler_params=_params("arbitrary", "arbitrary"), name=name)(a, wg, wu)


def _mm_ple(e, p_prompt, p_sample, w_gate, w_ple, resid, *, tm, tn, layer, name="mm_ple"):
    m, k = e.shape
    _, tp, kp = p_prompt.shape
    bs = p_sample.shape[1]
    n = w_gate.shape[-1]
    n_tiles, n_keep = _split_rows(tp, bs, tm)
    o_spec = pl.BlockSpec((tm, tn), lambda j, i: (i, j))
    return pl.pallas_call(
        functools.partial(_mm_ple_body, n_keep=n_keep), grid=(n // tn, n_tiles),
        in_specs=[pl.BlockSpec((tm, k), lambda j, i: (i, 0)),
                  pl.BlockSpec((None, tm, kp), lambda j, i: (layer, i, 0)),
                  pl.BlockSpec((None, bs, kp), lambda j, i: (layer, 0, 0)),
                  _wspec(w_gate, layer, k, tn), _wspec(w_ple, layer, kp, tn), o_spec],
        out_specs=o_spec, out_shape=jax.ShapeDtypeStruct((m, n), f32),
        scratch_shapes=[pltpu.VMEM((k, tn), bf16), pltpu.VMEM((kp, tn), bf16)],
        compiler_params=_params("arbitrary", "arbitrary"), name=name)(e, p_prompt, p_sample, w_gate, w_ple, resid)


def _col_from_row(row, eye):
    n = row.shape[1]
    return jnp.sum(jnp.where(eye, jnp.broadcast_to(row, (n, n)), 0.0), axis=1, keepdims=True)


def _mlstm_prompt_body(*refs, heads, dk, dv, cl, nb):
    q_refs, k_refs, v_refs, o_refs = (refs[i * nb:(i + 1) * nb] for i in range(4))
    gt_ref, bias_ref, gml_ref, h_ref, c_out, n_out, m_out, c_s, n_s, m_s = refs[4 * nb:]
    c = pl.program_id(1)

    @pl.when(c == 0)
    def _():
        c_s[...] = jnp.zeros_like(c_s)
        n_s[...] = jnp.zeros_like(n_s)
        m_s[...] = jnp.zeros_like(m_s)

    row = lax.broadcasted_iota(jnp.int32, (cl, cl), 0)
    col = lax.broadcasted_iota(jnp.int32, (cl, cl), 1)
    causal = col <= row
    eye = col == row
    triu = (row <= col).astype(f32)
    for r in range(nb):
        gt = gt_ref[r, 0] + bias_ref[...]
        lf = jax.nn.log_sigmoid(gt[heads:2 * heads])
        b_all = jnp.dot(lf, triu, precision=HI, preferred_element_type=f32)
        for hh in range(heads):
            st = r * heads + hh
            ig = gt[hh:hh + 1]
            b_row = b_all[hh:hh + 1]
            b_col = _col_from_row(b_row, eye)
            m_prev = m_s[st]
            d = jnp.where(causal, b_col - b_row + ig, -jnp.inf)
            inter = b_col + m_prev
            m_t = jnp.maximum(inter, jnp.max(d, axis=1, keepdims=True))
            w = jnp.exp(d - m_t)
            g = jnp.exp(inter - m_t)
            q = q_refs[r][:, hh * dk:(hh + 1) * dk]
            k = k_refs[r][:, hh * dk:(hh + 1) * dk] * (dk ** -0.5)
            vb = v_refs[r][:, hh * dv:(hh + 1) * dv].astype(bf16)
            qb = q.astype(bf16)
            cmat = c_s[st]
            n_row = n_s[st]
            s = lax.dot_general(qb, k.astype(bf16), _NT, preferred_element_type=f32) * w
            num = (jnp.dot(s.astype(bf16), vb, preferred_element_type=f32)
                   + g * jnp.dot(qb, cmat.astype(bf16), preferred_element_type=f32))
            den = jnp.sum(s, axis=1, keepdims=True) + g * jnp.sum(q * n_row, axis=1, keepdims=True)
            hraw = num / jnp.maximum(jnp.abs(den), jnp.exp(-m_t))
            hn = _rms(hraw, gml_ref[hh:hh + 1, :])
            ogate = jax.nn.sigmoid(o_refs[r][:, hh * dv:(hh + 1) * dv])
            h_ref[r, :, hh * dv:(hh + 1) * dv] = (ogate * hn).astype(h_ref.dtype)
            b_last = b_row[:, cl - 1:cl]
            dl = b_last - b_row + ig
            m_new = jnp.maximum(b_last + m_prev, jnp.max(dl, axis=1, keepdims=True))
            ws_col = _col_from_row(jnp.exp(dl - m_new), eye)
            gl = jnp.exp(b_last + m_prev - m_new)
            kw = k * ws_col
            c_s[st] = gl * cmat + lax.dot_general(kw.astype(bf16), vb, _TN, preferred_element_type=f32)
            n_s[st] = gl * n_row + jnp.sum(kw, axis=0, keepdims=True)
            m_s[st] = m_new

    @pl.when(c == pl.num_programs(1) - 1)
    def _():
        for r in range(nb):
            c_out[r] = c_s[r * heads:(r + 1) * heads]
            n_out[r] = n_s[r * heads:(r + 1) * heads]
            m_out[r] = m_s[r * heads:(r + 1) * heads]


_PROMPT_SEQS_PER_STEP = 4


def _mlstm_prompt(qkvo, gates_t, bias_col, g_ml, *, batch, seq):
    heads, dk, dv, cl = ML_HEADS, ML_DK, ML_DV, math.gcd(seq, CHUNK)
    nc = seq // cl
    wq = heads * dk
    nb = math.gcd(batch, _PROMPT_SEQS_PER_STEP)
    body = functools.partial(_mlstm_prompt_body, heads=heads, dk=dk, dv=dv, cl=cl, nb=nb)
    qkvo_specs = [pl.BlockSpec((cl, wq), lambda b, c, r=r, part=part: ((b * nb + r) * nc + c, part))
                  for part in range(4) for r in range(nb)]
    return pl.pallas_call(
        body, grid=(batch // nb, nc),
        in_specs=qkvo_specs + [pl.BlockSpec((nb, 1, 2 * heads, cl), lambda b, c: (b, c, 0, 0)),
                               pl.BlockSpec((2 * heads, 1), lambda b, c: (0, 0)),
                               pl.BlockSpec((heads, dv), lambda b, c: (0, 0))],
        out_specs=[pl.BlockSpec((nb, cl, heads * dv), lambda b, c: (b, c, 0)),
                   pl.BlockSpec((nb, heads, dk, dv), lambda b, c: (b, 0, 0, 0)),
                   pl.BlockSpec((nb, heads, 1, dk), lambda b, c: (b, 0, 0, 0)),
                   pl.BlockSpec((nb, heads, 1, 1), lambda b, c: (b, 0, 0, 0))],
        out_shape=[jax.ShapeDtypeStruct((batch, seq, heads * dv), bf16),
                   jax.ShapeDtypeStruct((batch, heads, dk, dv), f32),
                   jax.ShapeDtypeStruct((batch, heads, 1, dk), f32),
                   jax.ShapeDtypeStruct((batch, heads, 1, 1), f32)],
        scratch_shapes=[pltpu.VMEM((nb * heads, dk, dv), f32), pltpu.VMEM((nb * heads, 1, dk), f32),
                        pltpu.VMEM((nb * heads, 1, 1), f32)],
        compiler_params=_params("arbitrary", "arbitrary"), name="mlstm_prompt",
    )(*([qkvo] * (4 * nb)), gates_t, bias_col, g_ml)


_SAMPLE_BLOCK = 8


def _mlstm_sample_body(*refs, heads, dk, dv, aliased):
    if aliased:
        refs = refs[:10] + refs[11:]
    (q_ref, k_ref, v_ref, o_ref, gd_ref, bias_ref, gml_ref, c0_ref, n0_ref, m0_ref,
     h_ref, c_out, n_out, m_out) = refs
    bt = q_ref.shape[0]

    @pl.when(pl.program_id(0) > 0)
    def _():
        c_out[...] = jnp.zeros_like(c_out)

    @pl.when(pl.program_id(0) == 0)
    def _():
        g8 = gd_ref[:, 0:2 * heads] + bias_ref[...]
        lane = lax.broadcasted_iota(jnp.int32, (bt, heads), 1)
        m_new = jnp.zeros((bt, heads), f32)
        for hh in range(heads):
            ig = g8[:, hh:hh + 1]
            lf = jax.nn.log_sigmoid(g8[:, heads + hh:heads + hh + 1])
            m0 = m0_ref[:, hh:hh + 1]
            m_t = jnp.maximum(lf + m0, ig)
            w = jnp.exp(ig - m_t)
            g = jnp.exp(lf + m0 - m_t)
            q = q_ref[:, hh * dk:(hh + 1) * dk]
            k = k_ref[:, hh * dk:(hh + 1) * dk] * (dk ** -0.5)
            v = v_ref[:, hh * dv:(hh + 1) * dv]
            o = o_ref[:, hh * dv:(hh + 1) * dv]
            n0 = n0_ref[:, hh, :]
            kw = k * w
            q_t = q.T
            kw_t = kw.T
            qc_rows = []
            for b in range(bt):
                cmat = c0_ref[b, hh]
                qc_rows.append(jnp.sum(q_t[:, b:b + 1] * cmat, axis=0, keepdims=True))
                c_out[b, hh] = g[b:b + 1, :] * cmat + kw_t[:, b:b + 1] * v[b:b + 1, :]
            qc = jnp.concatenate(qc_rows, axis=0)
            s = jnp.sum(q * k, axis=1, keepdims=True) * w
            num = s * v + g * qc
            den = s + g * jnp.sum(q * n0, axis=1, keepdims=True)
            hraw = num / jnp.maximum(jnp.abs(den), jnp.exp(-m_t))
            hn = _rms(hraw, gml_ref[hh:hh + 1, :])
            h_ref[:, hh * dv:(hh + 1) * dv] = jax.nn.sigmoid(o) * hn
            n_out[:, hh, :] = g * n0 + kw
            m_new = jnp.where(lane == hh, m_t, m_new)
        m_out[...] = m_new


def _stacked_state_grid(layer, b, stack):
    n_l = DEPTH if stack is None else 1
    row = lambda l, i: jnp.where(l == 0, i, b - 1)
    out_layer = lambda l: (layer + l) % DEPTH
    return (n_l, b), row, out_layer


def _mlstm_sample(qkvo, gd, bias_row, g_ml, c0, n0, m0, *, layer, row0, c_stack=None):
    heads, dk, dv = ML_HEADS, ML_DK, ML_DV
    b = c0.shape[1]
    bt = _SAMPLE_BLOCK
    wq = heads * dk
    blk0 = row0 // bt
    grid, row, out_layer = _stacked_state_grid(layer, b // bt, c_stack)
    aliased = c_stack is not None
    body = functools.partial(_mlstm_sample_body, heads=heads, dk=dk, dv=dv, aliased=aliased)
    in_specs = [pl.BlockSpec((bt, wq), lambda l, i, c=c: (blk0 + row(l, i), c)) for c in range(4)]
    in_specs += [pl.BlockSpec((bt, gd.shape[-1]), lambda l, i: (blk0 + row(l, i), 0)),
                 pl.BlockSpec((1, 2 * heads), lambda l, i: (0, 0)),
                 pl.BlockSpec((heads, dv), lambda l, i: (0, 0)),
                 pl.BlockSpec((None, bt, heads, dk, dv), lambda l, i: (layer, row(l, i), 0, 0, 0)),
                 pl.BlockSpec((None, bt, heads, dk), lambda l, i: (layer, row(l, i), 0, 0)),
                 pl.BlockSpec((None, bt, heads), lambda l, i: (layer, row(l, i), 0))]
    args = [qkvo, qkvo, qkvo, qkvo, gd, bias_row, g_ml, c0, n0, m0]
    if aliased:
        in_specs.append(pl.BlockSpec(memory_space=pl.ANY))
        args.append(c_stack)
    return pl.pallas_call(
        body, grid=grid, in_specs=in_specs,
        out_specs=[pl.BlockSpec((bt, heads * dv), lambda l, i: (row(l, i), 0)),
                   pl.BlockSpec((None, bt, heads, dk, dv), lambda l, i: (out_layer(l), i, 0, 0, 0)),
                   pl.BlockSpec((bt, heads, dk), lambda l, i: (row(l, i), 0, 0)),
                   pl.BlockSpec((bt, heads), lambda l, i: (row(l, i), 0))],
        out_shape=[jax.ShapeDtypeStruct((b, heads * dv), f32),
                   jax.ShapeDtypeStruct((DEPTH, b, heads, dk, dv), f32),
                   jax.ShapeDtypeStruct((b, heads, dk), f32),
                   jax.ShapeDtypeStruct((b, heads), f32)],
        input_output_aliases={10: 1} if aliased else {},
        compiler_params=_params("arbitrary", "arbitrary"), name="mlstm_sample",
    )(*args)


def _s5_advance(x, l1, l2):
    return l1 * x + l2 * pltpu.roll(x, x.shape[-1] // 2, axis=1)


def _toeplitz_operator(krow):
    ch, tc = krow.shape
    lane = lax.broadcasted_iota(jnp.int32, krow.shape, 1)
    blocks = [krow] + [jnp.where(lane >= ch * s, pltpu.roll(krow, ch * s, axis=1), 0.0)
                       for s in range(1, tc // ch)]
    return jnp.concatenate(blocks, axis=0).astype(bf16)


def _s5_body(u_ref, m_ref, w_ref, v_ref, l1_ref, l2_ref, x0_ref, y_ref, x_out, xs_s, *, nc, batch):
    u = u_ref[0]
    xin = jnp.dot(u, w_ref[0], preferred_element_type=f32)
    l1 = l1_ref[0]
    l2 = l2_ref[0]
    x = x0_ref[0]
    for k in range(nc):
        sl = slice(k * batch, (k + 1) * batch)
        xs_s[sl, :] = x
        x = _s5_advance(x, l1, l2) + xin[sl, :]
    x_out[0] = x
    y_ref[0] = (jnp.dot(u, _toeplitz_operator(m_ref[0]), preferred_element_type=f32)
                + jnp.dot(xs_s[...].astype(bf16), v_ref[0], preferred_element_type=f32))


def _s5_scan(u_g, mats, x0, *, nc, batch, layer):
    m_mat, w_pk, v_pk, l1, l2 = mats
    g, rows, tc = u_g.shape
    p2 = w_pk.shape[-1]
    blk = lambda *s: pl.BlockSpec((1,) + s, lambda i: (i,) + (0,) * len(s))
    op = lambda *s: pl.BlockSpec((None, 1) + s, lambda i: (layer, i) + (0,) * len(s))
    body = functools.partial(_s5_body, nc=nc, batch=batch)
    return pl.pallas_call(
        body, grid=(g,),
        in_specs=[blk(rows, tc), op(S5_CH, tc), op(tc, p2), op(p2, tc), op(1, p2), op(1, p2), blk(batch, p2)],
        out_specs=[blk(rows, tc), blk(batch, p2)],
        out_shape=[jax.ShapeDtypeStruct((g, rows, tc), f32), jax.ShapeDtypeStruct((g, batch, p2), f32)],
        scratch_shapes=[pltpu.VMEM((rows, p2), f32)],
        compiler_params=_params("parallel"), name=f"s5_scan_t{tc // S5_CH}",
    )(u_g, m_mat, w_pk, v_pk, l1, l2, x0)


def _s5_prompt_body(x_ref, m_ref, w_ref, v_ref, l1_ref, l2_ref, y_ref, x_out, u_s, y_s, xin_s, xs_s,
                    *, gw, batch, seq, t, ch):
    nc = seq // t
    for b in range(batch):
        for tt in range(t):
            blk = x_ref[pl.ds(b * seq + tt, nc, stride=t), :]
            for gl in range(gw):
                u_s[gl, b * nc:(b + 1) * nc, tt * ch:(tt + 1) * ch] = blk[:, gl * ch:(gl + 1) * ch]
    for gl in range(gw):
        xin_s[gl] = jnp.dot(u_s[gl].astype(bf16), w_ref[gl], preferred_element_type=f32)
    xs = [jnp.zeros((batch, xin_s.shape[-1]), f32)] * gw
    for k in range(nc):
        for gl in range(gw):
            xs_s[gl, pl.ds(k, batch, stride=nc), :] = xs[gl]
            xs[gl] = _s5_advance(xs[gl], l1_ref[gl], l2_ref[gl]) + xin_s[gl, pl.ds(k, batch, stride=nc), :]
    for gl in range(gw):
        x_out[gl] = xs[gl]
        y_s[gl] = (jnp.dot(u_s[gl].astype(bf16), _toeplitz_operator(m_ref[gl]), preferred_element_type=f32)
                   + jnp.dot(xs_s[gl].astype(bf16), v_ref[gl], preferred_element_type=f32))
    for b in range(batch):
        for tt in range(t):
            y_ref[pl.ds(b * seq + tt, nc, stride=t), :] = jnp.concatenate(
                [y_s[gl, b * nc:(b + 1) * nc, tt * ch:(tt + 1) * ch] for gl in range(gw)], axis=1)


def _s5_prompt(uzx, mats, *, batch, seq, layer):
    m_mat, w_pk, v_pk, l1, l2 = mats
    _, g, tc, p2 = w_pk.shape
    ch = S5_CH
    t = tc // ch
    gw = 128 // ch
    rows = batch * (seq // t)
    tp = batch * seq
    win = lambda *s: pl.BlockSpec((gw,) + s, lambda i: (i,) + (0,) * len(s))
    op = lambda *s: pl.BlockSpec((None, gw) + s, lambda i: (layer, i) + (0,) * len(s))
    body = functools.partial(_s5_prompt_body, gw=gw, batch=batch, seq=seq, t=t, ch=ch)
    return pl.pallas_call(
        body, grid=(g // gw,),
        in_specs=[pl.BlockSpec((tp, gw * ch), lambda i: (0, i)), op(ch, tc), op(tc, p2), op(p2, tc),
                  op(1, p2), op(1, p2)],
        out_specs=[pl.BlockSpec((tp, gw * ch), lambda i: (0, i)), win(batch, p2)],
        out_shape=[jax.ShapeDtypeStruct((tp, g * ch), f32), jax.ShapeDtypeStruct((g, batch, p2), f32)],
        scratch_shapes=[pltpu.VMEM((gw, rows, tc), f32), pltpu.VMEM((gw, rows, tc), f32),
                        pltpu.VMEM((gw, rows, p2), f32), pltpu.VMEM((gw, rows, p2), f32)],
        compiler_params=_params("parallel"), name="s5_prompt",
    )(uzx, m_mat, w_pk, v_pk, l1, l2)


def _s5_matrices(lam_re, lam_im, log_dt, b_re, b_im, c_re, c_im, t):
    g, p = lam_re.shape
    ch = b_re.shape[-1]
    dt = jnp.exp(log_dt)[:, None]
    ar, ai = lam_re * dt, lam_im * dt

    def powers(tau):
        mag = jnp.exp(ar[:, None, :] * tau[None, :, None])
        ang = ai[:, None, :] * tau[None, :, None]
        return mag * jnp.cos(ang), mag * jnp.sin(ang)

    lbr, lbi = jnp.exp(ar) * jnp.cos(ai), jnp.exp(ar) * jnp.sin(ai)
    den = lam_re * lam_re + lam_im * lam_im
    fr = ((lbr - 1.0) * lam_re + lbi * lam_im) / den
    fi = (lbi * lam_re - (lbr - 1.0) * lam_im) / den
    bbr = jnp.swapaxes(fr[..., None] * b_re - fi[..., None] * b_im, 1, 2)
    bbi = jnp.swapaxes(fr[..., None] * b_im + fi[..., None] * b_re, 1, 2)
    cbr = c_re[:, :, None, :] * bbr[:, None, :, :] - c_im[:, :, None, :] * bbi[:, None, :, :]
    cbi = c_re[:, :, None, :] * bbi[:, None, :, :] + c_im[:, :, None, :] * bbr[:, None, :, :]
    steps = jnp.arange(t, dtype=f32)
    lr, li = powers(steps)
    krow = jnp.einsum("gcdp,gtp->gdtc", jnp.concatenate([cbr, -cbi], axis=-1),
                      jnp.concatenate([lr, li], axis=-1), precision=lax.Precision.HIGH).reshape(g, ch, t * ch)
    pr, pi = powers(t - 1.0 - steps)
    w_re = (pr[:, :, None, :] * bbr[:, None, :, :] - pi[:, :, None, :] * bbi[:, None, :, :]).reshape(g, t * ch, p)
    w_im = (pr[:, :, None, :] * bbi[:, None, :, :] + pi[:, :, None, :] * bbr[:, None, :, :]).reshape(g, t * ch, p)
    qr, qi = powers(steps + 1.0)
    qr, qi = jnp.swapaxes(qr, 1, 2)[..., None], jnp.swapaxes(qi, 1, 2)[..., None]
    ctr, cti = jnp.swapaxes(c_re, 1, 2)[:, :, None, :], jnp.swapaxes(c_im, 1, 2)[:, :, None, :]
    v_re = (ctr * qr - cti * qi).reshape(g, p, t * ch)
    v_im = -(ctr * qi + cti * qr).reshape(g, p, t * ch)
    ltr, lti = powers(jnp.full((1,), float(t), f32))
    w_pk = jnp.concatenate([w_re, w_im], axis=-1).astype(bf16)
    v_pk = jnp.concatenate([v_re, v_im], axis=1).astype(bf16)
    chunk_ops = (krow, w_pk, v_pk, jnp.concatenate([ltr, ltr], axis=-1), jnp.concatenate([-lti, lti], axis=-1))
    lb_r, lb_i = lbr[:, None, :], lbi[:, None, :]
    step_ops = (krow[:, :, :ch], w_pk[:, (t - 1) * ch:, :], v_pk[:, :, :ch],
                jnp.concatenate([lb_r, lb_r], axis=-1), jnp.concatenate([-lb_i, lb_i], axis=-1))
    return chunk_ops, step_ops


def _s5_glu_body(yp_ref, ys_ref, u_ref, d_ref, w_ref, b_ref, g_ref, o_ref, *, n_keep):
    last = pl.num_programs(0) - 1

    def run(y_raw):
        y5 = jax.nn.gelu(y_raw + d_ref[...] * u_ref[...])
        gate = jax.nn.sigmoid(jnp.dot(y5.astype(bf16), w_ref[...].astype(bf16), preferred_element_type=f32)
                              + b_ref[...])
        o_ref[...] = _rms(y5 * gate, g_ref[...]).astype(o_ref.dtype)

    @pl.when(pl.program_id(0) < last)
    def _():
        run(yp_ref[...])

    @pl.when(pl.program_id(0) == last)
    def _():
        run(_last_tile(yp_ref, ys_ref[...], n_keep))


def _s5_glu(y_prompt, y_sample, uzx, d_skip, w_glu, b_glu, g_s5, *, layer, tm):
    tp, wdt = y_prompt.shape
    bs = y_sample.shape[0]
    n_tiles, n_keep = _split_rows(tp, bs, tm)
    row = lambda i: (i, 0)
    fix = lambda i: (0, 0)
    return pl.pallas_call(
        functools.partial(_s5_glu_body, n_keep=n_keep), grid=(n_tiles,),
        in_specs=[pl.BlockSpec((tm, wdt), row), pl.BlockSpec((bs, wdt), fix), pl.BlockSpec((tm, wdt), row),
                  pl.BlockSpec((1, wdt), fix), pl.BlockSpec((None, wdt, wdt), lambda i: (layer, 0, 0)),
                  pl.BlockSpec((1, wdt), fix), pl.BlockSpec((1, wdt), fix)],
        out_specs=pl.BlockSpec((tm, wdt), row), out_shape=jax.ShapeDtypeStruct((tp + bs, wdt), bf16),
        compiler_params=_params("parallel"), name="s5_glu",
    )(y_prompt, y_sample, uzx, d_skip, w_glu, b_glu, g_s5)


def _ssd_prompt_body(xbc_ref, z_ref, gd_ref, dtt_ref, cw_ref, cb_ref, dtb_row, dtb_col, alog_row, alog_col,
                     dskip_ref, gssd_ref, y_ref, s_out, s_s, xp_s, ys_s, *, heads, hd, ns, groups, cl, width):
    c = pl.program_id(1)

    @pl.when(c == 0)
    def _():
        s_s[...] = jnp.zeros_like(s_s)
        xp_s[0:8, :] = jnp.zeros((8, xp_s.shape[1]), f32)

    xp_s[8:8 + cl, :] = xbc_ref[...]
    xc = cb_ref[...] + sum(cw_ref[j:j + 1, :] * xp_s[5 + j:5 + j + cl, :] for j in range(SSD_CONV))
    xp_s[0:8, :] = xp_s[cl:cl + 8, :]
    xc = xc * jax.nn.sigmoid(xc)
    row = lax.broadcasted_iota(jnp.int32, (cl, cl), 0)
    col = lax.broadcasted_iota(jnp.int32, (cl, cl), 1)
    tril = (col <= row).astype(f32)
    triu = (row <= col).astype(f32)
    row2 = lax.broadcasted_iota(jnp.int32, (cl, 2 * cl), 0)
    lane2 = lax.broadcasted_iota(jnp.int32, (cl, 2 * cl), 1)
    left = lane2 < cl
    causal2 = jnp.where(left, lane2, lane2 - cl) <= row2
    left_row = left[0:1, :]
    top = lax.broadcasted_iota(jnp.int32, (2 * hd, 1), 0) < hd
    dt_col = jax.nn.softplus(gd_ref[:, 8:8 + heads] + dtb_row[...])
    dt_row = jax.nn.softplus(dtt_ref[0, 0] + dtb_col[...])
    cum_col = jnp.dot(tril, dt_col * -jnp.exp(alog_row[...]), precision=HI, preferred_element_type=f32)
    cum_row = jnp.dot(dt_row * -jnp.exp(alog_col[...]), triu, precision=HI, preferred_element_type=f32)
    exp_col = jnp.exp(cum_col)
    pick = lambda cols, h0: jnp.where(left, cols[:, h0:h0 + 1], cols[:, h0 + 1:h0 + 2])
    rep = heads // groups
    for gi in range(groups):
        bm = xc[:, width + gi * ns:width + (gi + 1) * ns].astype(bf16)
        cm = xc[:, width + (groups + gi) * ns:width + (groups + gi + 1) * ns].astype(bf16)
        scores = lax.dot_general(cm, bm, _NT, preferred_element_type=f32)
        scores2 = jnp.concatenate([scores, scores], axis=1)
        for h0 in range(gi * rep, (gi + 1) * rep, 2):
            lo, hi = h0 * hd, (h0 + 2) * hd
            cc2 = pick(cum_col, h0)
            cr2 = jnp.concatenate([cum_row[h0:h0 + 1, :], cum_row[h0 + 1:h0 + 2, :]], axis=1)
            seg2 = jnp.exp(jnp.where(causal2, cc2 - cr2, -jnp.inf))
            x2 = xc[:, lo:hi]
            xdt2 = x2 * pick(dt_col, h0)
            xbd = jnp.concatenate([jnp.where(left, xdt2, 0.0), jnp.where(left, 0.0, xdt2)], axis=0)
            smat2 = s_s[lo:hi, :]
            y2 = (jnp.dot((scores2 * seg2).astype(bf16), xbd.astype(bf16), preferred_element_type=f32)
                  + pick(exp_col, h0) * lax.dot_general(cm, smat2.astype(bf16), _NT, preferred_element_type=f32))
            last0 = cum_row[h0:h0 + 1, cl - 1:cl]
            last1 = cum_row[h0 + 1:h0 + 2, cl - 1:cl]
            xw2 = (xdt2 * jnp.exp(jnp.where(left_row, last0, last1) - cc2)).astype(bf16)
            s_s[lo:hi, :] = (jnp.where(top, jnp.exp(last0), jnp.exp(last1)) * smat2
                             + lax.dot_general(xw2, bm, _TN, preferred_element_type=f32))
            dsk2 = jnp.where(left_row, dskip_ref[:, h0:h0 + 1], dskip_ref[:, h0 + 1:h0 + 2])
            ys_s[:, lo:hi] = y2 + dsk2 * x2
    z = z_ref[...]
    y_ref[...] = _rms(ys_s[...] * (z * jax.nn.sigmoid(z)), gssd_ref[...]).astype(y_ref.dtype)

    @pl.when(c == pl.num_programs(1) - 1)
    def _():
        s_out[0] = s_s[...]


def _ssd_prompt(uzx, gd, dt_t, conv_w, conv_b, dt_bias, a_log, d_skip, g_ssd, *, batch, seq):
    heads, hd, ns, groups, width = SSD_HEADS, SSD_HEAD_DIM, SSD_STATE, SSD_GROUPS, SSD_WIDTH
    cl = math.gcd(seq, CHUNK)
    assert cl == hd and (heads // groups) % 2 == 0
    nc = seq // cl
    cch = SSD_CONV_CH
    rows = lambda b, c: (b * nc + c, 0)
    fix = lambda b, c: (0, 0)
    body = functools.partial(_ssd_prompt_body, heads=heads, hd=hd, ns=ns, groups=groups, cl=cl, width=width)
    return pl.pallas_call(
        body, grid=(batch, nc),
        in_specs=[pl.BlockSpec((cl, cch), lambda b, c: (b * nc + c, 1)),
                  pl.BlockSpec((cl, width), lambda b, c: (b * nc + c, 1)),
                  pl.BlockSpec((cl, gd.shape[1]), rows),
                  pl.BlockSpec((1, 1, heads, cl), lambda b, c: (b, c, 0, 0)),
                  pl.BlockSpec((SSD_CONV, cch), fix), pl.BlockSpec((1, cch), fix),
                  pl.BlockSpec((1, heads), fix), pl.BlockSpec((heads, 1), fix),
                  pl.BlockSpec((1, heads), fix), pl.BlockSpec((heads, 1), fix),
                  pl.BlockSpec((1, heads), fix), pl.BlockSpec((1, width), fix)],
        out_specs=[pl.BlockSpec((cl, width), rows),
                   pl.BlockSpec((1, heads * hd, ns), lambda b, c: (b, 0, 0))],
        out_shape=[jax.ShapeDtypeStruct((batch * seq, width), bf16),
                   jax.ShapeDtypeStruct((batch, heads * hd, ns), f32)],
        scratch_shapes=[pltpu.VMEM((heads * hd, ns), f32), pltpu.VMEM((cl + 8, cch), f32),
                        pltpu.VMEM((cl, width), f32)],
        compiler_params=_params("arbitrary", "arbitrary"), name="ssd_prompt",
    )(uzx, uzx, gd, dt_t, conv_w, conv_b.reshape(1, cch), dt_bias.reshape(1, heads), dt_bias.reshape(heads, 1),
      a_log.reshape(1, heads), a_log.reshape(heads, 1), d_skip.reshape(1, heads), g_ssd.reshape(1, width))


def _ssd_sample_body(*refs, heads, hd, ns, groups, width, aliased):
    if aliased:
        refs = refs[:10] + refs[11:]
    (x_ref, conv0_ref, gd_ref, cw_ref, cb_ref, dtb_ref, alog_ref, dskip_ref, gssd_ref, s0_ref,
     y_ref, s_out, ys_s) = refs
    bt = x_ref.shape[0]

    @pl.when(pl.program_id(0) > 0)
    def _():
        s_out[...] = jnp.zeros_like(s_out)

    @pl.when(pl.program_id(0) == 0)
    def _():
        z = x_ref[:, width:2 * width]
        xc = cb_ref[...] + cw_ref[SSD_CONV - 1:SSD_CONV, :] * x_ref[:, 2 * width:]
        for j in range(SSD_CONV - 1):
            xc = xc + cw_ref[j:j + 1, :] * conv0_ref[:, j, :]
        xc = xc * jax.nn.sigmoid(xc)
        dt = jax.nn.softplus(gd_ref[:, 8:8 + heads] + dtb_ref[...])
        ea = jnp.exp(dt * -jnp.exp(alog_ref[...]))
        rep = heads // groups
        for gi in range(groups):
            bm = xc[:, width + gi * ns:width + (gi + 1) * ns]
            cm = xc[:, width + (groups + gi) * ns:width + (groups + gi + 1) * ns]
            cb_dot = jnp.sum(cm * bm, axis=1, keepdims=True)
            cmb = cm.astype(bf16)
            for hh in range(gi * rep, (gi + 1) * rep):
                xh = xc[:, hh * hd:(hh + 1) * hd]
                xdt = xh * dt[:, hh:hh + 1]
                eah = ea[:, hh:hh + 1]
                xdt_t = xdt.T
                sc_rows = []
                for b in range(bt):
                    smat = s0_ref[b, hh]
                    sc_rows.append(lax.dot_general(cmb, smat.astype(bf16), _NT,
                                                   preferred_element_type=f32)[b:b + 1, :])
                    s_out[b, hh] = eah[b:b + 1, :] * smat + xdt_t[:, b:b + 1] * bm[b:b + 1, :]
                sc = jnp.concatenate(sc_rows, axis=0)
                ys_s[:, hh * hd:(hh + 1) * hd] = cb_dot * xdt + eah * sc + dskip_ref[:, hh:hh + 1] * xh
        y_ref[...] = _rms(ys_s[...] * (z * jax.nn.sigmoid(z)), gssd_ref[...])


def _ssd_sample(uzx, conv0, gd, conv_w, conv_b, dt_bias, a_log, d_skip, g_ssd, s0, *, layer, row0, s_stack=None):
    heads, hd, ns, groups, width = SSD_HEADS, SSD_HEAD_DIM, SSD_STATE, SSD_GROUPS, SSD_WIDTH
    cch = SSD_CONV_CH
    b = s0.shape[1]
    bt = _SAMPLE_BLOCK
    blk0 = row0 // bt
    fix = lambda l, i: (0, 0)
    grid, row, out_layer = _stacked_state_grid(layer, b // bt, s_stack)
    aliased = s_stack is not None
    body = functools.partial(_ssd_sample_body, heads=heads, hd=hd, ns=ns, groups=groups, width=width,
                             aliased=aliased)
    in_specs = [pl.BlockSpec((bt, uzx.shape[-1]), lambda l, i: (blk0 + row(l, i), 0)),
                pl.BlockSpec((None, bt, SSD_CONV - 1, cch), lambda l, i: (layer, row(l, i), 0, 0)),
                pl.BlockSpec((bt, gd.shape[-1]), lambda l, i: (blk0 + row(l, i), 0)),
                pl.BlockSpec((SSD_CONV, cch), fix), pl.BlockSpec((1, cch), fix),
                pl.BlockSpec((1, heads), fix), pl.BlockSpec((1, heads), fix), pl.BlockSpec((1, heads), fix),
                pl.BlockSpec((1, width), fix),
                pl.BlockSpec((None, bt, heads, hd, ns), lambda l, i: (layer, row(l, i), 0, 0, 0))]
    args = [uzx, conv0, gd, conv_w, conv_b.reshape(1, cch), dt_bias.reshape(1, heads), a_log.reshape(1, heads),
            d_skip.reshape(1, heads), g_ssd.reshape(1, width), s0]
    if aliased:
        in_specs.append(pl.BlockSpec(memory_space=pl.ANY))
        args.append(s_stack)
    return pl.pallas_call(
        body, grid=grid, in_specs=in_specs,
        out_specs=[pl.BlockSpec((bt, width), lambda l, i: (row(l, i), 0)),
                   pl.BlockSpec((None, bt, heads, hd, ns), lambda l, i: (out_layer(l), i, 0, 0, 0))],
        out_shape=[jax.ShapeDtypeStruct((b, width), f32),
                   jax.ShapeDtypeStruct((DEPTH, b, heads, hd, ns), f32)],
        scratch_shapes=[pltpu.VMEM((bt, width), f32)],
        input_output_aliases={10: 1} if aliased else {},
        compiler_params=_params("arbitrary", "arbitrary"), name="ssd_sample",
    )(*args)


def _router_body(h_ref, g_ref, wr_ref, br_ref, cf_ref, idx_ref, gate_ref, *, n_exp):
    cf = _rms(h_ref[...], g_ref[...])
    cf_ref[...] = cf
    logits = jnp.dot(cf, wr_ref[...], precision=HI, preferred_element_type=f32) + br_ref[...]
    lane = lax.broadcasted_iota(jnp.int32, logits.shape, 1)
    m1 = jnp.max(logits, axis=1, keepdims=True)
    i1 = jnp.min(jnp.where(logits == m1, lane, n_exp), axis=1, keepdims=True)
    rest = jnp.where(lane == i1, -jnp.inf, logits)
    m2 = jnp.max(rest, axis=1, keepdims=True)
    i2 = jnp.min(jnp.where(rest == m2, lane, n_exp), axis=1, keepdims=True)
    e2 = jnp.exp(m2 - m1)
    g1 = 1.0 / (1.0 + e2)
    two = lax.broadcasted_iota(jnp.int32, (logits.shape[0], TOP_K), 1)
    idx_ref[...] = jnp.where(two == 0, i1, i2)
    gate_ref[...] = jnp.where(two == 0, g1, e2 * g1)


def _router(h, g_ffn, w_router, b_router, *, tm):
    m, d = h.shape
    e = w_router.shape[-1]
    row = lambda i: (i, 0)
    fix = lambda i: (0, 0)
    return pl.pallas_call(
        functools.partial(_router_body, n_exp=e), grid=(m // tm,),
        in_specs=[pl.BlockSpec((tm, d), row), pl.BlockSpec((1, d), fix), pl.BlockSpec((d, e), fix),
                  pl.BlockSpec((1, e), fix)],
        out_specs=[pl.BlockSpec((tm, d), row), pl.BlockSpec((tm, TOP_K), row),
                   pl.BlockSpec((tm, TOP_K), row)],
        out_shape=[jax.ShapeDtypeStruct((m, d), f32),
                   jax.ShapeDtypeStruct((m, TOP_K), jnp.int32), jax.ShapeDtypeStruct((m, TOP_K), f32)],
        compiler_params=_params("parallel"), name="router",
    )(h, g_ffn.reshape(1, d), w_router, b_router.reshape(1, e))


def _row_copy(src_hbm, dst, sem, src_row, dst_row):
    return pltpu.make_async_copy(src_hbm.at[pl.ds(src_row, 1)], dst.at[pl.ds(dst_row, 1)], sem)


_DMA_UNROLL = 8


def _gather_body(idx_ref, x_hbm, o_ref, buf, sem, *, tg):
    i = pl.program_id(0)

    def issue(tile, slot):
        base = tile * tg

        def start(grp, carry):
            for u in range(_DMA_UNROLL):
                r = grp * _DMA_UNROLL + u
                _row_copy(x_hbm, buf.at[slot], sem.at[slot], idx_ref[base + r], r).start(priority=u % 2)
            return carry

        lax.fori_loop(0, tg // _DMA_UNROLL, start, 0)

    @pl.when(i == 0)
    def _():
        issue(0, 0)

    @pl.when(i + 1 < pl.num_programs(0))
    def _():
        issue(i + 1, (i + 1) % 2)

    slot = i % 2
    pltpu.make_async_copy(x_hbm.at[pl.ds(0, tg)], buf.at[slot], sem.at[slot]).wait()
    o_ref[...] = buf[slot].astype(o_ref.dtype)


def _gather_rows(x, row_idx, *, tg, out_dtype):
    r = row_idx.shape[0]
    d = x.shape[1]
    return pl.pallas_call(
        functools.partial(_gather_body, tg=tg),
        grid_spec=pltpu.PrefetchScalarGridSpec(
            num_scalar_prefetch=1, grid=(r // tg,),
            in_specs=[pl.BlockSpec(memory_space=pl.ANY)],
            out_specs=pl.BlockSpec((tg, d), lambda i, idx: (i, 0)),
            scratch_shapes=[pltpu.VMEM((2, tg, d), x.dtype), pltpu.SemaphoreType.DMA((2,))]),
        out_shape=jax.ShapeDtypeStruct((r, d), out_dtype),
        compiler_params=_params("arbitrary"), name="gather_rows")(row_idx, x)


def _gmm_body(te_ref, first_ref, next_ref, nv_ref, a_ref, *rest, n_w, swiglu, n_sub):
    w_hbm = rest[:n_w]
    o_ref, wf_ref, wb_ref, sem, run_ref = rest[n_w:]
    j = pl.program_id(0)
    tn = o_ref.shape[1]
    tm = o_ref.shape[0] // n_sub

    def tile_copies(expert, col_tile, slot):
        col = pl.multiple_of(col_tile * tn, 128)
        return [pltpu.make_async_copy(w_hbm[i].at[expert, :, pl.ds(col, tn)], wf_ref.at[slot, i], sem.at[slot, i])
                for i in range(n_w)]

    @pl.when((j == 0) & (pl.program_id(1) == 0))
    def _():
        run_ref[0] = 0
        for c in tile_copies(te_ref[0], 0, 0):
            c.start()

    for sub in range(n_sub):
        t = pl.program_id(1) * n_sub + sub
        rows = slice(sub * tm, (sub + 1) * tm)

        @pl.when((t < nv_ref[0]) & (first_ref[t] == 1))
        def _():
            slot = run_ref[0] % 2
            for c in tile_copies(te_ref[t], j, slot):
                c.wait()
            for i in range(n_w):
                wb_ref[i] = wf_ref[slot, i].astype(bf16)
            nxt = next_ref[t]

            @pl.when(nxt >= 0)
            def _():
                for c in tile_copies(te_ref[nxt], j, 1 - slot):
                    c.start()

            @pl.when((nxt < 0) & (j + 1 < pl.num_programs(0)))
            def _():
                for c in tile_copies(te_ref[0], j + 1, 1 - slot):
                    c.start()

            run_ref[0] = run_ref[0] + 1

        @pl.when(t < nv_ref[0])
        def _():
            a = a_ref[rows, :]
            if swiglu:
                g = jnp.dot(a, wb_ref[0], preferred_element_type=f32)
                u = jnp.dot(a, wb_ref[1], preferred_element_type=f32)
                o_ref[rows, :] = (g * jax.nn.sigmoid(g) * u).astype(o_ref.dtype)
            else:
                o_ref[rows, :] = jnp.dot(a, wb_ref[0], preferred_element_type=f32).astype(o_ref.dtype)

        @pl.when(t >= nv_ref[0])
        def _():
            o_ref[rows, :] = jnp.zeros((tm, tn), o_ref.dtype)


def _gmm(a_sorted, weights, tables, *, tm, tn, n_sub, swiglu, out_dtype, name):
    tile_expert, tile_first, tile_next, n_valid = tables
    r, k = a_sorted.shape
    n = weights[0].shape[-1]
    n_w = len(weights)
    assert (r // tm) % n_sub == 0
    return pl.pallas_call(
        functools.partial(_gmm_body, n_w=n_w, swiglu=swiglu, n_sub=n_sub),
        grid_spec=pltpu.PrefetchScalarGridSpec(
            num_scalar_prefetch=4, grid=(n // tn, r // (tm * n_sub)),
            in_specs=([pl.BlockSpec((tm * n_sub, k), lambda j, t, *_: (t, 0))]
                      + [pl.BlockSpec(memory_space=pl.ANY)] * n_w),
            out_specs=pl.BlockSpec((tm * n_sub, tn), lambda j, t, *_: (t, j)),
            scratch_shapes=[pltpu.VMEM((2, n_w, k, tn), f32), pltpu.VMEM((n_w, k, tn), bf16),
                            pltpu.SemaphoreType.DMA((2, n_w)), pltpu.SMEM((1,), jnp.int32)]),
        out_shape=jax.ShapeDtypeStruct((r, n), out_dtype),
        compiler_params=_params("arbitrary", "arbitrary", vmem=_VMEM_LIMIT_GMM), name=name,
    )(tile_expert, tile_first, tile_next, n_valid, a_sorted, *weights)


def _combine_body(pos_ref, y_hbm, gate_ref, r_ref, gn_ref, o_ref, e_ref, buf, sem, *, tc):
    i = pl.program_id(0)

    def issue(tile, slot):
        base = tile * tc

        def start(r, carry):
            for kk in range(TOP_K):
                _row_copy(y_hbm, buf.at[slot, kk], sem.at[slot], pos_ref[(base + r) * TOP_K + kk], r).start()
            return carry

        lax.fori_loop(0, tc, start, 0, unroll=_DMA_UNROLL)

    @pl.when(i == 0)
    def _():
        issue(0, 0)

    @pl.when(i + 1 < pl.num_programs(0))
    def _():
        issue(i + 1, (i + 1) % 2)

    slot = i % 2
    for kk in range(TOP_K):
        pltpu.make_async_copy(y_hbm.at[pl.ds(0, tc)], buf.at[slot, kk], sem.at[slot]).wait()
    gate = gate_ref[...]
    out = r_ref[...] + gate[:, 0:1] * buf[slot, 0] + gate[:, 1:2] * buf[slot, 1]
    o_ref[...] = out
    e_ref[...] = _rms(out, gn_ref[...]).astype(e_ref.dtype)


def _combine(y_sorted, pos_flat, gates, resid, g_next, *, tc):
    m, d = resid.shape
    rows = lambda i, pos: (i, 0)
    return pl.pallas_call(
        functools.partial(_combine_body, tc=tc),
        grid_spec=pltpu.PrefetchScalarGridSpec(
            num_scalar_prefetch=1, grid=(m // tc,),
            in_specs=[pl.BlockSpec(memory_space=pl.ANY), pl.BlockSpec((tc, TOP_K), rows),
                      pl.BlockSpec((tc, d), rows), pl.BlockSpec((1, d), lambda i, pos: (0, 0))],
            out_specs=[pl.BlockSpec((tc, d), rows), pl.BlockSpec((tc, d), rows)],
            scratch_shapes=[pltpu.VMEM((2, TOP_K, tc, d), f32), pltpu.SemaphoreType.DMA((2,))]),
        out_shape=[jax.ShapeDtypeStruct((m, d), f32), jax.ShapeDtypeStruct((m, d), bf16)],
        compiler_params=_params("arbitrary"), name="moe_combine",
    )(pos_flat, y_sorted, gates, resid, g_next.reshape(1, d))


def _routing_tables(top_i, n_exp, tm, n_tiles):
    m = top_i.shape[0]
    e_flat = top_i.reshape(-1)
    onehot = (e_flat[:, None] == jnp.arange(n_exp, dtype=jnp.int32)[None, :]).astype(jnp.int32)
    rank = jnp.take_along_axis(jnp.cumsum(onehot, axis=0), e_flat[:, None], axis=1)[:, 0] - 1
    counts = jnp.sum(onehot, axis=0)
    tiles_per = (counts + tm - 1) // tm
    tile_end = jnp.cumsum(tiles_per)
    tile_start = tile_end - tiles_per
    pos = tile_start[e_flat] * tm + rank
    token = jnp.arange(m * TOP_K, dtype=jnp.int32) // TOP_K
    row_token = jnp.zeros((n_tiles * tm,), jnp.int32).at[pos].set(token)
    n_valid = tile_end[-1]
    tid = jnp.minimum(jnp.arange(n_tiles, dtype=jnp.int32), n_valid - 1)
    tile_expert = jnp.sum((tid[:, None] >= tile_end[None, :]).astype(jnp.int32), axis=1)
    tile_first = jnp.concatenate([jnp.ones((1,), jnp.int32),
                                  (tile_expert[1:] != tile_expert[:-1]).astype(jnp.int32)])
    ids = jnp.arange(n_tiles, dtype=jnp.int32)
    starts = jnp.where((tile_first == 1) & (ids < n_valid), ids, n_tiles)
    later = jnp.concatenate([lax.cummin(starts, reverse=True)[1:], jnp.full((1,), n_tiles, jnp.int32)])
    tile_next = jnp.where(later >= n_tiles, -1, later).astype(jnp.int32)
    return (pos.astype(jnp.int32), row_token,
            (tile_expert.astype(jnp.int32), tile_first, tile_next, n_valid.reshape(1).astype(jnp.int32)))


def _moe_ffn(h, g_ffn, w_router, b_router, wg, wu, wd, g_next, *, tm_tok, tm, tn_up, tn_down):
    m = h.shape[0]
    n_exp = wg.shape[0]
    c_f32, top_i, top_g = _router(h, g_ffn, w_router, b_router, tm=tm_tok)
    n_tiles = 2 * (((m * TOP_K) // tm + n_exp + 1) // 2)
    pos, row_token, tables = _routing_tables(top_i, n_exp, tm, n_tiles)
    x_sorted = _gather_rows(c_f32, row_token, tg=tm, out_dtype=bf16)
    h_sorted = _gmm(x_sorted, (wg, wu), tables, tm=tm, tn=tn_up, n_sub=2, swiglu=True, out_dtype=bf16,
                    name="moe_gate_up")
    y_sorted = _gmm(h_sorted, (wd,), tables, tm=tm, tn=tn_down, n_sub=1, swiglu=False, out_dtype=f32,
                    name="moe_down")
    return _combine(y_sorted, pos, top_g, h, g_next, tc=tm_tok // 2)


def kernel(x_prompt, x_sample, state_mlstm_C, state_mlstm_n, state_mlstm_m, state_s5_re, state_s5_im, state_ssd, cache_conv, p_prompt, p_sample, g_mix, w_in, b_igate, b_fgate, g_ml, s5_lam_re, s5_lam_im, s5_log_dt, s5_b_re, s5_b_im, s5_c_re, s5_c_im, s5_d, s5_w_glu, s5_b_glu, g_s5, ssd_conv_w, ssd_conv_b, ssd_dt_bias, ssd_a_log, ssd_d, g_ssd, w_out, g_ffn, ffn_w_gate, ffn_w_up, ffn_w_down, w_router, b_router, moe_w_gate, moe_w_up, moe_w_down, g_ple, w_ple, w_ple_gate, g_final):
    bp, seq, d = x_prompt.shape
    bs = x_sample.shape[0]
    tp = bp * seq
    m = tp + bs
    tm = _token_tile(tp, bs)
    heads = ML_HEADS
    t5 = math.gcd(seq, S5_CHUNK)
    cl = math.gcd(seq, CHUNK)
    nc = seq // cl

    p_p = p_prompt.reshape(DEPTH, tp, -1)
    p_s = p_sample.reshape(DEPTH, bs, -1)
    w_in_t = jnp.swapaxes(w_in, 1, 2)
    outs_p = [[] for _ in range(7)]
    outs_s = [[] for _ in range(5)]
    c_stack = s_stack = None
    mats_p, mats_s = jax.vmap(functools.partial(_s5_matrices, t=t5))(
        s5_lam_re, s5_lam_im, s5_log_dt, s5_b_re, s5_b_im, s5_c_re, s5_c_im)

    for i in range(DEPTH):
        if i == 0:
            h, a = _rmsnorm_in(x_prompt.reshape(tp, d), x_sample.reshape(bs, d), g_mix[i], tm)
        else:
            a = _rmsnorm(h, g_mix[i], bf16, tm)
        qkvo = _mm_nt(a, w_in_t, tm=tm, tn=1024, layer=i, row_off=0, n_rows=_OFF_GATES, name="in_proj_qkvo")
        uzx = _mm_nt(a, w_in_t, tm=tm, tn=1024, layer=i, row_off=_OFF_U, n_rows=_OFF_DT - _OFF_U, name="in_proj_uzx")
        gd = _mm_gates(a, w_in_t, tm=tm, layer=i, off1=_OFF_GATES, n1=2 * heads, off2=_OFF_DT, n2=SSD_HEADS,
                       name="in_proj_gates")

        gates_t = jnp.transpose(gd[:tp, :2 * heads].reshape(bp, nc, cl, 2 * heads), (0, 1, 3, 2))
        bias8 = jnp.concatenate([b_igate[i], b_fgate[i]])
        h_ml_p, c_p, n_p, m_p = _mlstm_prompt(qkvo, gates_t, bias8.reshape(2 * heads, 1), g_ml[i], batch=bp, seq=seq)
        h_ml_s, c_stack, n_s, m_s = _mlstm_sample(qkvo, gd, bias8.reshape(1, 2 * heads), g_ml[i], state_mlstm_C,
                                                  state_mlstm_n, state_mlstm_m, layer=i, row0=tp, c_stack=c_stack)

        y_p, x5_p = _s5_prompt(uzx, mats_p, batch=bp, seq=seq, layer=i)
        s5re_p, s5im_p = x5_p[..., :S5_STATE], x5_p[..., S5_STATE:]
        u_s = jnp.transpose(uzx[tp:, :S5_WIDTH].reshape(bs, S5_GROUPS, S5_CH), (1, 0, 2)).astype(bf16)
        x0_s = jnp.swapaxes(jnp.concatenate([state_s5_re[i], state_s5_im[i]], axis=-1), 0, 1)
        y_s, x5_s = _s5_scan(u_s, mats_s, x0_s, nc=1, batch=bs, layer=i)
        s5re_s, s5im_s = x5_s[..., :S5_STATE], x5_s[..., S5_STATE:]
        y_s = jnp.transpose(y_s, (1, 0, 2)).reshape(bs, S5_WIDTH)
        y5 = _s5_glu(y_p, y_s, uzx, s5_d[i].reshape(1, S5_WIDTH), s5_w_glu,
                     s5_b_glu[i].reshape(1, S5_WIDTH), g_s5[i].reshape(1, S5_WIDTH), layer=i, tm=tm)

        dt_t = jnp.transpose(gd[:tp, 2 * heads:2 * heads + SSD_HEADS].reshape(bp, nc, cl, SSD_HEADS), (0, 1, 3, 2))
        y_ssd_p, ssd_p = _ssd_prompt(uzx, gd, dt_t, ssd_conv_w[i], ssd_conv_b[i], ssd_dt_bias[i], ssd_a_log[i],
                                     ssd_d[i], g_ssd[i], batch=bp, seq=seq)
        y_ssd_s, s_stack = _ssd_sample(uzx, cache_conv, gd, ssd_conv_w[i], ssd_conv_b[i], ssd_dt_bias[i],
                                       ssd_a_log[i], ssd_d[i], g_ssd[i], state_ssd, layer=i, row0=tp,
                                       s_stack=s_stack)
        conv_p = jnp.stack([uzx[(b + 1) * seq - (SSD_CONV - 1):(b + 1) * seq, S5_WIDTH + SSD_WIDTH:]
                            for b in range(bp)])
        conv_s = jnp.concatenate([cache_conv[i][:, 1:], uzx[tp:, S5_WIDTH + SSD_WIDTH:].reshape(bs, 1, -1)], axis=1)

        for lst, s in zip(outs_p, (c_p, n_p.reshape(bp, heads, ML_DK), m_p.reshape(bp, heads),
                                   jnp.swapaxes(s5re_p, 0, 1), jnp.swapaxes(s5im_p, 0, 1),
                                   ssd_p.reshape(bp, SSD_HEADS, SSD_HEAD_DIM, SSD_STATE), conv_p)):
            lst.append(s)
        for lst, s in zip(outs_s, (n_s, m_s, jnp.swapaxes(s5re_s, 0, 1), jnp.swapaxes(s5im_s, 0, 1), conv_s)):
            lst.append(s)

        h = _mm_mix(h_ml_p.reshape(tp, -1), h_ml_s, y5, y_ssd_p.reshape(tp, -1), y_ssd_s, w_out, h, tm=tm, tn=1024,
                    layer=i, name="out_proj")

        j = i // 2
        if i % 2 == 0:
            cn = _rmsnorm(h, g_ffn[i], bf16, tm)
            tm_up = 2 * tm if m % (2 * tm) == 0 else tm
            hid = _mm_swiglu(cn, ffn_w_gate, ffn_w_up, tm=tm_up, tn=512, layer=j, name="ffn_gate_up")
            h = _mm(hid, ffn_w_down, tm=tm // 2, tn=512, layer=j, resid=h, name="ffn_down")
            e = _rmsnorm(h, g_ple[i], bf16, tm)
        else:
            n_moe = moe_w_gate.shape[0]
            sel = lambda w: w.reshape((n_moe * N_EXPERTS,) + w.shape[2:])[j * N_EXPERTS:(j + 1) * N_EXPERTS] if n_moe > 1 else w.reshape(w.shape[1:])
            h, e = _moe_ffn(h, g_ffn[i], w_router[j], b_router[j], sel(moe_w_gate), sel(moe_w_up), sel(moe_w_down),
                            g_ple[i], tm_tok=tm, tm=256, tn_up=1024, tn_down=512)

        h = _mm_ple(e, p_p, p_s, w_ple_gate, w_ple, h, tm=tm, tn=1024, layer=i)

    y_p, y_s = _rmsnorm_out(h, g_final, tp, tm)
    n_s, m_s, s5re_s, s5im_s, conv_s = (jnp.stack(l) for l in outs_s)
    return ((y_p.reshape(bp, seq, d), y_s.reshape(bs, 1, d)) + tuple(jnp.stack(l) for l in outs_p)
            + (c_stack, n_s, m_s, s5re_s, s5im_s, s_stack, conv_s))
```

```python
import functools
import math

import jax
import jax.numpy as jnp
from jax import lax
from jax.experimental import pallas as pl
from jax.experimental.pallas import tpu as pltpu

f32 = jnp.float32
bf16 = jnp.bfloat16
HI = lax.Precision.HIGHEST

DEPTH = 2
ML_HEADS = 4
ML_DK = 256
ML_DV = 256
ML_WIDTH = ML_HEADS * ML_DV
S5_CH = 16
S5_WIDTH = 512
S5_GROUPS = 32
S5_STATE = 64
SSD_HEAD_DIM = 64
SSD_WIDTH = 512
SSD_HEADS = 8
SSD_GROUPS = 2
SSD_STATE = 128
SSD_CONV = 4
SSD_CONV_CH = 1024
CHUNK = 64
S5_CHUNK = 32
N_EXPERTS = 8
TOP_K = 2
RMS_EPS = 1e-6

_OFF_GATES = 4 * ML_WIDTH
_OFF_U = _OFF_GATES + 2 * ML_HEADS
_OFF_DT = _OFF_U + S5_WIDTH + SSD_WIDTH + SSD_CONV_CH

_VMEM_LIMIT = 56 * 1024 * 1024
_VMEM_LIMIT_GMM = 60 * 1024 * 1024
_NT = (((1,), (1,)), ((), ()))
_TN = (((0,), (0,)), ((), ()))


def _params(*sem, vmem=_VMEM_LIMIT):
    return pltpu.CompilerParams(dimension_semantics=sem, vmem_limit_bytes=vmem)


def _rms(x, g):
    return x * lax.rsqrt(jnp.mean(x * x, axis=-1, keepdims=True) + RMS_EPS) * g


def _rmsnorm_body(x_ref, g_ref, o_ref):
    o_ref[...] = _rms(x_ref[...], g_ref[...]).astype(o_ref.dtype)


def _last_tile(p_tile, s_rows, n_keep):
    return jnp.concatenate([p_tile[0:n_keep, :], s_rows], axis=0)


def _split_rows(tp, bs, tm):
    n_tiles = (tp + bs) // tm
    n_keep = tp - (n_tiles - 1) * tm
    assert n_tiles * tm == tp + bs and 0 < n_keep and n_keep + bs == tm and n_keep % 16 == 0
    return n_tiles, n_keep


def _token_tile(tp, bs):
    m = tp + bs
    for n_tiles in (10, 8, 5, 4, 2, 1):
        tm = m // n_tiles
        if m % n_tiles == 0 and tm % 32 == 0 and bs < tm and (tp - (n_tiles - 1) * tm) % 16 == 0:
            return tm
    raise ValueError("no row tiling for these token counts")


def _rmsnorm_in_body(xp_ref, xs_ref, g_ref, h_ref, a_ref, *, n_keep):
    last = pl.num_programs(0) - 1

    def run(x):
        h_ref[...] = x
        a_ref[...] = _rms(x, g_ref[...]).astype(a_ref.dtype)

    @pl.when(pl.program_id(0) < last)
    def _():
        run(xp_ref[...])

    @pl.when(pl.program_id(0) == last)
    def _():
        run(_last_tile(xp_ref, xs_ref[...], n_keep))


def _rmsnorm_in(xp, xs, g, tm):
    tp, d = xp.shape
    bs = xs.shape[0]
    n_tiles, n_keep = _split_rows(tp, bs, tm)
    row = lambda i: (i, 0)
    fix = lambda i: (0, 0)
    return pl.pallas_call(
        functools.partial(_rmsnorm_in_body, n_keep=n_keep), grid=(n_tiles,),
        in_specs=[pl.BlockSpec((tm, d), row), pl.BlockSpec((bs, d), fix), pl.BlockSpec((1, d), fix)],
        out_specs=[pl.BlockSpec((tm, d), row), pl.BlockSpec((tm, d), row)],
        out_shape=[jax.ShapeDtypeStruct((tp + bs, d), f32), jax.ShapeDtypeStruct((tp + bs, d), bf16)],
        compiler_params=_params("parallel"), name="rmsnorm_in")(xp, xs, g.reshape(1, d))


def _rmsnorm_out_body(x_ref, g_ref, yp_ref, ys_ref, *, n_keep):
    y = _rms(x_ref[...], g_ref[...])
    yp_ref[...] = y

    @pl.when(pl.program_id(0) == pl.num_programs(0) - 1)
    def _():
        ys_ref[...] = y[n_keep:, :]


def _rmsnorm_out(x, g, tp, tm):
    m, d = x.shape
    bs = m - tp
    n_tiles, n_keep = _split_rows(tp, bs, tm)
    return pl.pallas_call(
        functools.partial(_rmsnorm_out_body, n_keep=n_keep), grid=(n_tiles,),
        in_specs=[pl.BlockSpec((tm, d), lambda i: (i, 0)), pl.BlockSpec((1, d), lambda i: (0, 0))],
        out_specs=[pl.BlockSpec((tm, d), lambda i: (i, 0)), pl.BlockSpec((bs, d), lambda i: (0, 0))],
        out_shape=[jax.ShapeDtypeStruct((tp, d), f32), jax.ShapeDtypeStruct((bs, d), f32)],
        compiler_params=_params("arbitrary"), name="rmsnorm_out")(x, g.reshape(1, d))


def _rmsnorm(x, g, out_dtype, tm):
    m, d = x.shape
    return pl.pallas_call(
        _rmsnorm_body, grid=(m // tm,),
        in_specs=[pl.BlockSpec((tm, d), lambda i: (i, 0)), pl.BlockSpec((1, d), lambda i: (0, 0))],
        out_specs=pl.BlockSpec((tm, d), lambda i: (i, 0)),
        out_shape=jax.ShapeDtypeStruct((m, d), out_dtype),
        compiler_params=_params("parallel"), name="rmsnorm")(x, g.reshape(1, d))


def _cast_weight_once(w_ref, wb_ref):
    @pl.when(pl.program_id(1) == 0)
    def _():
        wb_ref[...] = w_ref[...].astype(bf16)


def _mm_plain_body(a_ref, w_ref, o_ref, wb_ref):
    _cast_weight_once(w_ref, wb_ref)
    o_ref[...] = jnp.dot(a_ref[...], wb_ref[...], preferred_element_type=f32).astype(o_ref.dtype)


def _mm_resid_body(a_ref, w_ref, r_ref, o_ref, wb_ref):
    _cast_weight_once(w_ref, wb_ref)
    o_ref[...] = r_ref[...] + jnp.dot(a_ref[...], wb_ref[...], preferred_element_type=f32)


def _mm_swiglu_body(a_ref, wg_ref, wu_ref, o_ref, wgb_ref, wub_ref):
    _cast_weight_once(wg_ref, wgb_ref)
    _cast_weight_once(wu_ref, wub_ref)
    a = a_ref[...]
    g = jnp.dot(a, wgb_ref[...], preferred_element_type=f32)
    u = jnp.dot(a, wub_ref[...], preferred_element_type=f32)
    o_ref[...] = (g * jax.nn.sigmoid(g) * u).astype(o_ref.dtype)


def _mm_ple_body(e_ref, pp_ref, ps_ref, wg_ref, wp_ref, r_ref, o_ref, wgb_ref, wpb_ref, *, n_keep):
    _cast_weight_once(wg_ref, wgb_ref)
    _cast_weight_once(wp_ref, wpb_ref)
    last = pl.num_programs(1) - 1

    def run(p):
        gate = jnp.dot(e_ref[...], wgb_ref[...], preferred_element_type=f32)
        emb = jnp.dot(p.astype(bf16), wpb_ref[...], preferred_element_type=f32)
        o_ref[...] = r_ref[...] + emb * jax.nn.sigmoid(gate)

    @pl.when(pl.program_id(1) < last)
    def _():
        run(pp_ref[...])

    @pl.when(pl.program_id(1) == last)
    def _():
        run(_last_tile(pp_ref, ps_ref[...], n_keep))


def _mm_mix_body(a1p_ref, a1s_ref, a2_ref, a3p_ref, a3s_ref, w_ref, r_ref, o_ref, wb_ref, *, n_keep):
    _cast_weight_once(w_ref, wb_ref)
    last = pl.num_programs(1) - 1
    k1 = a1p_ref.shape[1]
    k2 = a2_ref.shape[1]

    def run(a1, a3):
        acc = jnp.dot(a1, wb_ref[0:k1, :], preferred_element_type=f32)
        acc += jnp.dot(a2_ref[...], wb_ref[k1:k1 + k2, :], preferred_element_type=f32)
        acc += jnp.dot(a3, wb_ref[k1 + k2:, :], preferred_element_type=f32)
        o_ref[...] = r_ref[...] + acc

    @pl.when(pl.program_id(1) < last)
    def _():
        run(a1p_ref[...], a3p_ref[...])

    @pl.when(pl.program_id(1) == last)
    def _():
        run(_last_tile(a1p_ref, a1s_ref[...].astype(bf16), n_keep),
            _last_tile(a3p_ref, a3s_ref[...].astype(bf16), n_keep))


def _mm_mix(a1p, a1s, a2, a3p, a3s, w, resid, *, tm, tn, layer, name):
    m, n = resid.shape
    tp, k1 = a1p.shape
    bs = a1s.shape[0]
    k2, k3 = a2.shape[1], a3p.shape[1]
    n_tiles, n_keep = _split_rows(tp, bs, tm)
    rows = lambda j, i: (i, 0)
    fix = lambda j, i: (0, 0)
    o_spec = pl.BlockSpec((tm, tn), lambda j, i: (i, j))
    return pl.pallas_call(
        functools.partial(_mm_mix_body, n_keep=n_keep), grid=(n // tn, n_tiles),
        in_specs=[pl.BlockSpec((tm, k1), rows), pl.BlockSpec((bs, k1), fix), pl.BlockSpec((tm, k2), rows),
                  pl.BlockSpec((tm, k3), rows), pl.BlockSpec((bs, k3), fix),
                  _wspec(w, layer, k1 + k2 + k3, tn), o_spec],
        out_specs=o_spec, out_shape=jax.ShapeDtypeStruct((m, n), f32),
        scratch_shapes=[pltpu.VMEM((k1 + k2 + k3, tn), bf16)],
        compiler_params=_params("arbitrary", "arbitrary"), name=name)(a1p, a1s, a2, a3p, a3s, w, resid)


def _wspec(w, layer, k, tn, col_block_off=0):
    if w.ndim == 2:
        return pl.BlockSpec((k, tn), lambda j, i: (0, j + col_block_off))
    return pl.BlockSpec((None, k, tn), lambda j, i: (layer, 0, j + col_block_off))


def _mm(a, w, *, tm, tn, layer=0, n_cols=None, col_off=0, resid=None, out_dtype=f32, name="mm"):
    m, k = a.shape
    n = n_cols if n_cols is not None else w.shape[-1]
    grid = (n // tn, m // tm)
    a_spec = pl.BlockSpec((tm, k), lambda j, i: (i, 0))
    o_spec = pl.BlockSpec((tm, tn), lambda j, i: (i, j))
    w_spec = _wspec(w, layer, k, tn, col_off // tn)
    scratch = [pltpu.VMEM((k, tn), bf16)]
    if resid is None:
        body, ins, specs = _mm_plain_body, (a, w), [a_spec, w_spec]
    else:
        body, ins, specs = _mm_resid_body, (a, w, resid), [a_spec, w_spec, o_spec]
    return pl.pallas_call(
        body, grid=grid, in_specs=specs, out_specs=o_spec,
        out_shape=jax.ShapeDtypeStruct((m, n), out_dtype), scratch_shapes=scratch,
        compiler_params=_params("arbitrary", "arbitrary"), name=name)(*ins)


def _mm_nt_body(a_ref, wt_ref, o_ref, wb_ref):
    @pl.when(pl.program_id(1) == 0)
    def _():
        wb_ref[...] = wt_ref[0].T.astype(bf16)

    o_ref[...] = jnp.dot(a_ref[...], wb_ref[...], preferred_element_type=f32)


def _wt_rows(layer, k, n_rows, row_of):
    return pl.BlockSpec((pl.Element(1), pl.Element(n_rows), pl.Element(k)),
                        lambda *idx: (layer, row_of(*idx), 0))


def _mm_nt(a, wt, *, tm, tn, layer, row_off, n_rows, name):
    m, k = a.shape
    return pl.pallas_call(
        _mm_nt_body, grid=(n_rows // tn, m // tm),
        in_specs=[pl.BlockSpec((tm, k), lambda j, i: (i, 0)),
                  _wt_rows(layer, k, tn, lambda j, i: pl.multiple_of(row_off + tn * j, 8))],
        out_specs=pl.BlockSpec((tm, tn), lambda j, i: (i, j)),
        out_shape=jax.ShapeDtypeStruct((m, n_rows), f32), scratch_shapes=[pltpu.VMEM((k, tn), bf16)],
        compiler_params=_params("arbitrary", "arbitrary"), name=name)(a, wt)


def _mm_gates_body(a_ref, w1_ref, w2_ref, o_ref):
    a = a_ref[...]
    n1 = w1_ref.shape[1]
    o_ref[:, 0:n1] = lax.dot_general(a, w1_ref[0].astype(bf16), _NT, preferred_element_type=f32)
    o_ref[:, n1:] = lax.dot_general(a, w2_ref[0].astype(bf16), _NT, preferred_element_type=f32)


def _mm_gates(a, wt, *, tm, layer, off1, n1, off2, n2, name):
    m, k = a.shape
    return pl.pallas_call(
        _mm_gates_body, grid=(m // tm,),
        in_specs=[pl.BlockSpec((tm, k), lambda i: (i, 0)),
                  _wt_rows(layer, k, n1, lambda i: off1), _wt_rows(layer, k, n2, lambda i: off2)],
        out_specs=pl.BlockSpec((tm, n1 + n2), lambda i: (i, 0)),
        out_shape=jax.ShapeDtypeStruct((m, n1 + n2), f32),
        compiler_params=_params("parallel"), name=name)(a, wt, wt)


def _mm_swiglu(a, wg, wu, *, tm, tn, layer=0, name="mm_swiglu"):
    m, k = a.shape
    n = wg.shape[-1]
    a_spec = pl.BlockSpec((tm, k), lambda j, i: (i, 0))
    o_spec = pl.BlockSpec((tm, tn), lambda j, i: (i, j))
    return pl.pallas_call(
        _mm_swiglu_body, grid=(n // tn, m // tm),
        in_specs=[a_spec, _wspec(wg, layer, k, tn), _wspec(wu, layer, k, tn)], out_specs=o_spec,
        out_shape=jax.ShapeDtypeStruct((m, n), bf16),
        scratch_shapes=[pltpu.VMEM((k, tn), bf16), pltpu.VMEM((k, tn), bf16)],
        compiler_params=_params("arbitrary", "arbitrary"), name=name)(a, wg, wu)


def _mm_ple(e, p_prompt, p_sample, w_gate, w_ple, resid, *, tm, tn, layer, name="mm_ple"):
    m, k = e.shape
    _, tp, kp = p_prompt.shape
    bs = p_sample.shape[1]
    n = w_gate.shape[-1]
    n_tiles, n_keep = _split_rows(tp, bs, tm)
    o_spec = pl.BlockSpec((tm, tn), lambda j, i: (i, j))
    return pl.pallas_call(
        functools.partial(_mm_ple_body, n_keep=n_keep), grid=(n // tn, n_tiles),
        in_specs=[pl.BlockSpec((tm, k), lambda j, i: (i, 0)),
                  pl.BlockSpec((None, tm, kp), lambda j, i: (layer, i, 0)),
                  pl.BlockSpec((None, bs, kp), lambda j, i: (layer, 0, 0)),
                  _wspec(w_gate, layer, k, tn), _wspec(w_ple, layer, kp, tn), o_spec],
        out_specs=o_spec, out_shape=jax.ShapeDtypeStruct((m, n), f32),
        scratch_shapes=[pltpu.VMEM((k, tn), bf16), pltpu.VMEM((kp, tn), bf16)],
        compiler_params=_params("arbitrary", "arbitrary"), name=name)(e, p_prompt, p_sample, w_gate, w_ple, resid)


def _col_from_row(row, eye):
    n = row.shape[1]
    return jnp.sum(jnp.where(eye, jnp.broadcast_to(row, (n, n)), 0.0), axis=1, keepdims=True)


def _mlstm_prompt_body(*refs, heads, dk, dv, cl, nb):
    q_refs, k_refs, v_refs, o_refs = (refs[i * nb:(i + 1) * nb] for i in range(4))
    gt_ref, bias_ref, gml_ref, h_ref, c_out, n_out, m_out, c_s, n_s, m_s = refs[4 * nb:]
    c = pl.program_id(1)

    @pl.when(c == 0)
    def _():
        c_s[...] = jnp.zeros_like(c_s)
        n_s[...] = jnp.zeros_like(n_s)
        m_s[...] = jnp.zeros_like(m_s)

    row = lax.broadcasted_iota(jnp.int32, (cl, cl), 0)
    col = lax.broadcasted_iota(jnp.int32, (cl, cl), 1)
    causal = col <= row
    eye = col == row
    triu = (row <= col).astype(f32)
    for r in range(nb):
        gt = gt_ref[r, 0] + bias_ref[...]
        lf = jax.nn.log_sigmoid(gt[heads:2 * heads])
        b_all = jnp.dot(lf, triu, precision=HI, preferred_element_type=f32)
        for hh in range(heads):
            st = r * heads + hh
            ig = gt[hh:hh + 1]
            b_row = b_all[hh:hh + 1]
            b_col = _col_from_row(b_row, eye)
            m_prev = m_s[st]
            d = jnp.where(causal, b_col - b_row + ig, -jnp.inf)
            inter = b_col + m_prev
            m_t = jnp.maximum(inter, jnp.max(d, axis=1, keepdims=True))
            w = jnp.exp(d - m_t)
            g = jnp.exp(inter - m_t)
            q = q_refs[r][:, hh * dk:(hh + 1) * dk]
            k = k_refs[r][:, hh * dk:(hh + 1) * dk] * (dk ** -0.5)
            vb = v_refs[r][:, hh * dv:(hh + 1) * dv].astype(bf16)
            qb = q.astype(bf16)
            cmat = c_s[st]
            n_row = n_s[st]
            s = lax.dot_general(qb, k.astype(bf16), _NT, preferred_element_type=f32) * w
            num = (jnp.dot(s.astype(bf16), vb, preferred_element_type=f32)
                   + g * jnp.dot(qb, cmat.astype(bf16), preferred_element_type=f32))
            den = jnp.sum(s, axis=1, keepdims=True) + g * jnp.sum(q * n_row, axis=1, keepdims=True)
            hraw = num / jnp.maximum(jnp.abs(den), jnp.exp(-m_t))
            hn = _rms(hraw, gml_ref[hh:hh + 1, :])
            ogate = jax.nn.sigmoid(o_refs[r][:, hh * dv:(hh + 1) * dv])
            h_ref[r, :, hh * dv:(hh + 1) * dv] = (ogate * hn).astype(h_ref.dtype)
            b_last = b_row[:, cl - 1:cl]
            dl = b_last - b_row + ig
            m_new = jnp.maximum(b_last + m_prev, jnp.max(dl, axis=1, keepdims=True))
            ws_col = _col_from_row(jnp.exp(dl - m_new), eye)
            gl = jnp.exp(b_last + m_prev - m_new)
            kw = k * ws_col
            c_s[st] = gl * cmat + lax.dot_general(kw.astype(bf16), vb, _TN, preferred_element_type=f32)
            n_s[st] = gl * n_row + jnp.sum(kw, axis=0, keepdims=True)
            m_s[st] = m_new

    @pl.when(c == pl.num_programs(1) - 1)
    def _():
        for r in range(nb):
            c_out[r] = c_s[r * heads:(r + 1) * heads]
            n_out[r] = n_s[r * heads:(r + 1) * heads]
            m_out[r] = m_s[r * heads:(r + 1) * heads]


_PROMPT_SEQS_PER_STEP = 4


def _mlstm_prompt(qkvo, gates_t, bias_col, g_ml, *, batch, seq):
    heads, dk, dv, cl = ML_HEADS, ML_DK, ML_DV, math.gcd(seq, CHUNK)
    nc = seq // cl
    wq = heads * dk
    nb = math.gcd(batch, _PROMPT_SEQS_PER_STEP)
    body = functools.partial(_mlstm_prompt_body, heads=heads, dk=dk, dv=dv, cl=cl, nb=nb)
    qkvo_specs = [pl.BlockSpec((cl, wq), lambda b, c, r=r, part=part: ((b * nb + r) * nc + c, part))
                  for part in range(4) for r in range(nb)]
    return pl.pallas_call(
        body, grid=(batch // nb, nc),
        in_specs=qkvo_specs + [pl.BlockSpec((nb, 1, 2 * heads, cl), lambda b, c: (b, c, 0, 0)),
                               pl.BlockSpec((2 * heads, 1), lambda b, c: (0, 0)),
                               pl.BlockSpec((heads, dv), lambda b, c: (0, 0))],
        out_specs=[pl.BlockSpec((nb, cl, heads * dv), lambda b, c: (b, c, 0)),
                   pl.BlockSpec((nb, heads, dk, dv), lambda b, c: (b, 0, 0, 0)),
                   pl.BlockSpec((nb, heads, 1, dk), lambda b, c: (b, 0, 0, 0)),
                   pl.BlockSpec((nb, heads, 1, 1), lambda b, c: (b, 0, 0, 0))],
        out_shape=[jax.ShapeDtypeStruct((batch, seq, heads * dv), bf16),
                   jax.ShapeDtypeStruct((batch, heads, dk, dv), f32),
                   jax.ShapeDtypeStruct((batch, heads, 1, dk), f32),
                   jax.ShapeDtypeStruct((batch, heads, 1, 1), f32)],
        scratch_shapes=[pltpu.VMEM((nb * heads, dk, dv), f32), pltpu.VMEM((nb * heads, 1, dk), f32),
                        pltpu.VMEM((nb * heads, 1, 1), f32)],
        compiler_params=_params("arbitrary", "arbitrary"), name="mlstm_prompt",
    )(*([qkvo] * (4 * nb)), gates_t, bias_col, g_ml)


_SAMPLE_BLOCK = 8


def _mlstm_sample_body(*refs, heads, dk, dv, aliased):
    if aliased:
        refs = refs[:10] + refs[11:]
    (q_ref, k_ref, v_ref, o_ref, gd_ref, bias_ref, gml_ref, c0_ref, n0_ref, m0_ref,
     h_ref, c_out, n_out, m_out) = refs
    bt = q_ref.shape[0]

    @pl.when(pl.program_id(0) > 0)
    def _():
        c_out[...] = jnp.zeros_like(c_out)

    @pl.when(pl.program_id(0) == 0)
    def _():
        g8 = gd_ref[:, 0:2 * heads] + bias_ref[...]
        lane = lax.broadcasted_iota(jnp.int32, (bt, heads), 1)
        m_new = jnp.zeros((bt, heads), f32)
        for hh in range(heads):
            ig = g8[:, hh:hh + 1]
            lf = jax.nn.log_sigmoid(g8[:, heads + hh:heads + hh + 1])
            m0 = m0_ref[:, hh:hh + 1]
            m_t = jnp.maximum(lf + m0, ig)
            w = jnp.exp(ig - m_t)
            g = jnp.exp(lf + m0 - m_t)
            q = q_ref[:, hh * dk:(hh + 1) * dk]
            k = k_ref[:, hh * dk:(hh + 1) * dk] * (dk ** -0.5)
            v = v_ref[:, hh * dv:(hh + 1) * dv]
            o = o_ref[:, hh * dv:(hh + 1) * dv]
            n0 = n0_ref[:, hh, :]
            kw = k * w
            q_t = q.T
            kw_t = kw.T
            qc_rows = []
            for b in range(bt):
                cmat = c0_ref[b, hh]
                qc_rows.append(jnp.sum(q_t[:, b:b + 1] * cmat, axis=0, keepdims=True))
                c_out[b, hh] = g[b:b + 1, :] * cmat + kw_t[:, b:b + 1] * v[b:b + 1, :]
            qc = jnp.concatenate(qc_rows, axis=0)
            s = jnp.sum(q * k, axis=1, keepdims=True) * w
            num = s * v + g * qc
            den = s + g * jnp.sum(q * n0, axis=1, keepdims=True)
            hraw = num / jnp.maximum(jnp.abs(den), jnp.exp(-m_t))
            hn = _rms(hraw, gml_ref[hh:hh + 1, :])
            h_ref[:, hh * dv:(hh + 1) * dv] = jax.nn.sigmoid(o) * hn
            n_out[:, hh, :] = g * n0 + kw
            m_new = jnp.where(lane == hh, m_t, m_new)
        m_out[...] = m_new


def _stacked_state_grid(layer, b, stack):
    n_l = DEPTH if stack is None else 1
    row = lambda l, i: jnp.where(l == 0, i, b - 1)
    out_layer = lambda l: (layer + l) % DEPTH
    return (n_l, b), row, out_layer


def _mlstm_sample(qkvo, gd, bias_row, g_ml, c0, n0, m0, *, layer, row0, c_stack=None):
    heads, dk, dv = ML_HEADS, ML_DK, ML_DV
    b = c0.shape[1]
    bt = _SAMPLE_BLOCK
    wq = heads * dk
    blk0 = row0 // bt
    grid, row, out_layer = _stacked_state_grid(layer, b // bt, c_stack)
    aliased = c_stack is not None
    body = functools.partial(_mlstm_sample_body, heads=heads, dk=dk, dv=dv, aliased=aliased)
    in_specs = [pl.BlockSpec((bt, wq), lambda l, i, c=c: (blk0 + row(l, i), c)) for c in range(4)]
    in_specs += [pl.BlockSpec((bt, gd.shape[-1]), lambda l, i: (blk0 + row(l, i), 0)),
                 pl.BlockSpec((1, 2 * heads), lambda l, i: (0, 0)),
                 pl.BlockSpec((heads, dv), lambda l, i: (0, 0)),
                 pl.BlockSpec((None, bt, heads, dk, dv), lambda l, i: (layer, row(l, i), 0, 0, 0)),
                 pl.BlockSpec((None, bt, heads, dk), lambda l, i: (layer, row(l, i), 0, 0)),
                 pl.BlockSpec((None, bt, heads), lambda l, i: (layer, row(l, i), 0))]
    args = [qkvo, qkvo, qkvo, qkvo, gd, bias_row, g_ml, c0, n0, m0]
    if aliased:
        in_specs.append(pl.BlockSpec(memory_space=pl.ANY))
        args.append(c_stack)
    return pl.pallas_call(
        body, grid=grid, in_specs=in_specs,
        out_specs=[pl.BlockSpec((bt, heads * dv), lambda l, i: (row(l, i), 0)),
                   pl.BlockSpec((None, bt, heads, dk, dv), lambda l, i: (out_layer(l), i, 0, 0, 0)),
                   pl.BlockSpec((bt, heads, dk), lambda l, i: (row(l, i), 0, 0)),
                   pl.BlockSpec((bt, heads), lambda l, i: (row(l, i), 0))],
        out_shape=[jax.ShapeDtypeStruct((b, heads * dv), f32),
                   jax.ShapeDtypeStruct((DEPTH, b, heads, dk, dv), f32),
                   jax.ShapeDtypeStruct((b, heads, dk), f32),
                   jax.ShapeDtypeStruct((b, heads), f32)],
        input_output_aliases={10: 1} if aliased else {},
        compiler_params=_params("arbitrary", "arbitrary"), name="mlstm_sample",
    )(*args)


def _s5_advance(x, l1, l2):
    return l1 * x + l2 * pltpu.roll(x, x.shape[-1] // 2, axis=1)


def _toeplitz_operator(krow):
    ch, tc = krow.shape
    lane = lax.broadcasted_iota(jnp.int32, krow.shape, 1)
    blocks = [krow] + [jnp.where(lane >= ch * s, pltpu.roll(krow, ch * s, axis=1), 0.0)
                       for s in range(1, tc // ch)]
    return jnp.concatenate(blocks, axis=0).astype(bf16)


def _s5_body(u_ref, m_ref, w_ref, v_ref, l1_ref, l2_ref, x0_ref, y_ref, x_out, xs_s, *, nc, batch):
    u = u_ref[0]
    xin = jnp.dot(u, w_ref[0], preferred_element_type=f32)
    l1 = l1_ref[0]
    l2 = l2_ref[0]
    x = x0_ref[0]
    for k in range(nc):
        sl = slice(k * batch, (k + 1) * batch)
        xs_s[sl, :] = x
        x = _s5_advance(x, l1, l2) + xin[sl, :]
    x_out[0] = x
    y_ref[0] = (jnp.dot(u, _toeplitz_operator(m_ref[0]), preferred_element_type=f32)
                + jnp.dot(xs_s[...].astype(bf16), v_ref[0], preferred_element_type=f32))


def _s5_scan(u_g, mats, x0, *, nc, batch, layer):
    m_mat, w_pk, v_pk, l1, l2 = mats
    g, rows, tc = u_g.shape
    p2 = w_pk.shape[-1]
    blk = lambda *s: pl.BlockSpec((1,) + s, lambda i: (i,) + (0,) * len(s))
    op = lambda *s: pl.BlockSpec((None, 1) + s, lambda i: (layer, i) + (0,) * len(s))
    body = functools.partial(_s5_body, nc=nc, batch=batch)
    return pl.pallas_call(
        body, grid=(g,),
        in_specs=[blk(rows, tc), op(S5_CH, tc), op(tc, p2), op(p2, tc), op(1, p2), op(1, p2), blk(batch, p2)],
        out_specs=[blk(rows, tc), blk(batch, p2)],
        out_shape=[jax.ShapeDtypeStruct((g, rows, tc), f32), jax.ShapeDtypeStruct((g, batch, p2), f32)],
        scratch_shapes=[pltpu.VMEM((rows, p2), f32)],
        compiler_params=_params("parallel"), name=f"s5_scan_t{tc // S5_CH}",
    )(u_g, m_mat, w_pk, v_pk, l1, l2, x0)


def _s5_prompt_body(x_ref, m_ref, w_ref, v_ref, l1_ref, l2_ref, y_ref, x_out, u_s, y_s, xin_s, xs_s,
                    *, gw, batch, seq, t, ch):
    nc = seq // t
    for b in range(batch):
        for tt in range(t):
            blk = x_ref[pl.ds(b * seq + tt, nc, stride=t), :]
            for gl in range(gw):
                u_s[gl, b * nc:(b + 1) * nc, tt * ch:(tt + 1) * ch] = blk[:, gl * ch:(gl + 1) * ch]
    for gl in range(gw):
        xin_s[gl] = jnp.dot(u_s[gl].astype(bf16), w_ref[gl], preferred_element_type=f32)
    xs = [jnp.zeros((batch, xin_s.shape[-1]), f32)] * gw
    for k in range(nc):
        for gl in range(gw):
            xs_s[gl, pl.ds(k, batch, stride=nc), :] = xs[gl]
            xs[gl] = _s5_advance(xs[gl], l1_ref[gl], l2_ref[gl]) + xin_s[gl, pl.ds(k, batch, stride=nc), :]
    for gl in range(gw):
        x_out[gl] = xs[gl]
        y_s[gl] = (jnp.dot(u_s[gl].astype(bf16), _toeplitz_operator(m_ref[gl]), preferred_element_type=f32)
                   + jnp.dot(xs_s[gl].astype(bf16), v_ref[gl], preferred_element_type=f32))
    for b in range(batch):
        for tt in range(t):
            y_ref[pl.ds(b * seq + tt, nc, stride=t), :] = jnp.concatenate(
                [y_s[gl, b * nc:(b + 1) * nc, tt * ch:(tt + 1) * ch] for gl in range(gw)], axis=1)


def _s5_prompt(uzx, mats, *, batch, seq, layer):
    m_mat, w_pk, v_pk, l1, l2 = mats
    _, g, tc, p2 = w_pk.shape
    ch = S5_CH
    t = tc // ch
    gw = 128 // ch
    rows = batch * (seq // t)
    tp = batch * seq
    win = lambda *s: pl.BlockSpec((gw,) + s, lambda i: (i,) + (0,) * len(s))
    op = lambda *s: pl.BlockSpec((None, gw) + s, lambda i: (layer, i) + (0,) * len(s))
    body = functools.partial(_s5_prompt_body, gw=gw, batch=batch, seq=seq, t=t, ch=ch)
    return pl.pallas_call(
        body, grid=(g // gw,),
        in_specs=[pl.BlockSpec((tp, gw * ch), lambda i: (0, i)), op(ch, tc), op(tc, p2), op(p2, tc),
                  op(1, p2), op(1, p2)],
        out_specs=[pl.BlockSpec((tp, gw * ch), lambda i: (0, i)), win(batch, p2)],
        out_shape=[jax.ShapeDtypeStruct((tp, g * ch), f32), jax.ShapeDtypeStruct((g, batch, p2), f32)],
        scratch_shapes=[pltpu.VMEM((gw, rows, tc), f32), pltpu.VMEM((gw, rows, tc), f32),
                        pltpu.VMEM((gw, rows, p2), f32), pltpu.VMEM((gw, rows, p2), f32)],
        compiler_params=_params("parallel"), name="s5_prompt",
    )(uzx, m_mat, w_pk, v_pk, l1, l2)


def _s5_matrices(lam_re, lam_im, log_dt, b_re, b_im, c_re, c_im, t):
    g, p = lam_re.shape
    ch = b_re.shape[-1]
    dt = jnp.exp(log_dt)[:, None]
    ar, ai = lam_re * dt, lam_im * dt

    def powers(tau):
        mag = jnp.exp(ar[:, None, :] * tau[None, :, None])
        ang = ai[:, None, :] * tau[None, :, None]
        return mag * jnp.cos(ang), mag * jnp.sin(ang)

    lbr, lbi = jnp.exp(ar) * jnp.cos(ai), jnp.exp(ar) * jnp.sin(ai)
    den = lam_re * lam_re + lam_im * lam_im
    fr = ((lbr - 1.0) * lam_re + lbi * lam_im) / den
    fi = (lbi * lam_re - (lbr - 1.0) * lam_im) / den
    bbr = jnp.swapaxes(fr[..., None] * b_re - fi[..., None] * b_im, 1, 2)
    bbi = jnp.swapaxes(fr[..., None] * b_im + fi[..., None] * b_re, 1, 2)
    cbr = c_re[:, :, None, :] * bbr[:, None, :, :] - c_im[:, :, None, :] * bbi[:, None, :, :]
    cbi = c_re[:, :, None, :] * bbi[:, None, :, :] + c_im[:, :, None, :] * bbr[:, None, :, :]
    steps = jnp.arange(t, dtype=f32)
    lr, li = powers(steps)
    krow = jnp.einsum("gcdp,gtp->gdtc", jnp.concatenate([cbr, -cbi], axis=-1),
                      jnp.concatenate([lr, li], axis=-1), precision=lax.Precision.HIGH).reshape(g, ch, t * ch)
    pr, pi = powers(t - 1.0 - steps)
    w_re = (pr[:, :, None, :] * bbr[:, None, :, :] - pi[:, :, None, :] * bbi[:, None, :, :]).reshape(g, t * ch, p)
    w_im = (pr[:, :, None, :] * bbi[:, None, :, :] + pi[:, :, None, :] * bbr[:, None, :, :]).reshape(g, t * ch, p)
    qr, qi = powers(steps + 1.0)
    qr, qi = jnp.swapaxes(qr, 1, 2)[..., None], jnp.swapaxes(qi, 1, 2)[..., None]
    ctr, cti = jnp.swapaxes(c_re, 1, 2)[:, :, None, :], jnp.swapaxes(c_im, 1, 2)[:, :, None, :]
    v_re = (ctr * qr - cti * qi).reshape(g, p, t * ch)
    v_im = -(ctr * qi + cti * qr).reshape(g, p, t * ch)
    ltr, lti = powers(jnp.full((1,), float(t), f32))
    w_pk = jnp.concatenate([w_re, w_im], axis=-1).astype(bf16)
    v_pk = jnp.concatenate([v_re, v_im], axis=1).astype(bf16)
    chunk_ops = (krow, w_pk, v_pk, jnp.concatenate([ltr, ltr], axis=-1), jnp.concatenate([-lti, lti], axis=-1))
    lb_r, lb_i = lbr[:, None, :], lbi[:, None, :]
    step_ops = (krow[:, :, :ch], w_pk[:, (t - 1) * ch:, :], v_pk[:, :, :ch],
                jnp.concatenate([lb_r, lb_r], axis=-1), jnp.concatenate([-lb_i, lb_i], axis=-1))
    return chunk_ops, step_ops


def _s5_glu_body(yp_ref, ys_ref, u_ref, d_ref, w_ref, b_ref, g_ref, o_ref, *, n_keep):
    last = pl.num_programs(0) - 1

    def run(y_raw):
        y5 = jax.nn.gelu(y_raw + d_ref[...] * u_ref[...])
        gate = jax.nn.sigmoid(jnp.dot(y5.astype(bf16), w_ref[...].astype(bf16), preferred_element_type=f32)
                              + b_ref[...])
        o_ref[...] = _rms(y5 * gate, g_ref[...]).astype(o_ref.dtype)

    @pl.when(pl.program_id(0) < last)
    def _():
        run(yp_ref[...])

    @pl.when(pl.program_id(0) == last)
    def _():
        run(_last_tile(yp_ref, ys_ref[...], n_keep))


def _s5_glu(y_prompt, y_sample, uzx, d_skip, w_glu, b_glu, g_s5, *, layer, tm):
    tp, wdt = y_prompt.shape
    bs = y_sample.shape[0]
    n_tiles, n_keep = _split_rows(tp, bs, tm)
    row = lambda i: (i, 0)
    fix = lambda i: (0, 0)
    return pl.pallas_call(
        functools.partial(_s5_glu_body, n_keep=n_keep), grid=(n_tiles,),
        in_specs=[pl.BlockSpec((tm, wdt), row), pl.BlockSpec((bs, wdt), fix), pl.BlockSpec((tm, wdt), row),
                  pl.BlockSpec((1, wdt), fix), pl.BlockSpec((None, wdt, wdt), lambda i: (layer, 0, 0)),
                  pl.BlockSpec((1, wdt), fix), pl.BlockSpec((1, wdt), fix)],
        out_specs=pl.BlockSpec((tm, wdt), row), out_shape=jax.ShapeDtypeStruct((tp + bs, wdt), bf16),
        compiler_params=_params("parallel"), name="s5_glu",
    )(y_prompt, y_sample, uzx, d_skip, w_glu, b_glu, g_s5)


def _ssd_prompt_body(xbc_ref, z_ref, gd_ref, dtt_ref, cw_ref, cb_ref, dtb_row, dtb_col, alog_row, alog_col,
                     dskip_ref, gssd_ref, y_ref, s_out, s_s, xp_s, ys_s, *, heads, hd, ns, groups, cl, width):
    c = pl.program_id(1)

    @pl.when(c == 0)
    def _():
        s_s[...] = jnp.zeros_like(s_s)
        xp_s[0:8, :] = jnp.zeros((8, xp_s.shape[1]), f32)

    xp_s[8:8 + cl, :] = xbc_ref[...]
    xc = cb_ref[...] + sum(cw_ref[j:j + 1, :] * xp_s[5 + j:5 + j + cl, :] for j in range(SSD_CONV))
    xp_s[0:8, :] = xp_s[cl:cl + 8, :]
    xc = xc * jax.nn.sigmoid(xc)
    row = lax.broadcasted_iota(jnp.int32, (cl, cl), 0)
    col = lax.broadcasted_iota(jnp.int32, (cl, cl), 1)
    tril = (col <= row).astype(f32)
    triu = (row <= col).astype(f32)
    row2 = lax.broadcasted_iota(jnp.int32, (cl, 2 * cl), 0)
    lane2 = lax.broadcasted_iota(jnp.int32, (cl, 2 * cl), 1)
    left = lane2 < cl
    causal2 = jnp.where(left, lane2, lane2 - cl) <= row2
    left_row = left[0:1, :]
    top = lax.broadcasted_iota(jnp.int32, (2 * hd, 1), 0) < hd
    dt_col = jax.nn.softplus(gd_ref[:, 8:8 + heads] + dtb_row[...])
    dt_row = jax.nn.softplus(dtt_ref[0, 0] + dtb_col[...])
    cum_col = jnp.dot(tril, dt_col * -jnp.exp(alog_row[...]), precision=HI, preferred_element_type=f32)
    cum_row = jnp.dot(dt_row * -jnp.exp(alog_col[...]), triu, precision=HI, preferred_element_type=f32)
    exp_col = jnp.exp(cum_col)
    pick = lambda cols, h0: jnp.where(left, cols[:, h0:h0 + 1], cols[:, h0 + 1:h0 + 2])
    rep = heads // groups
    for gi in range(groups):
        bm = xc[:, width + gi * ns:width + (gi + 1) * ns].astype(bf16)
        cm = xc[:, width + (groups + gi) * ns:width + (groups + gi + 1) * ns].astype(bf16)
        scores = lax.dot_general(cm, bm, _NT, preferred_element_type=f32)
        scores2 = jnp.concatenate([scores, scores], axis=1)
        for h0 in range(gi * rep, (gi + 1) * rep, 2):
            lo, hi = h0 * hd, (h0 + 2) * hd
            cc2 = pick(cum_col, h0)
            cr2 = jnp.concatenate([cum_row[h0:h0 + 1, :], cum_row[h0 + 1:h0 + 2, :]], axis=1)
            seg2 = jnp.exp(jnp.where(causal2, cc2 - cr2, -jnp.inf))
            x2 = xc[:, lo:hi]
            xdt2 = x2 * pick(dt_col, h0)
            xbd = jnp.concatenate([jnp.where(left, xdt2, 0.0), jnp.where(left, 0.0, xdt2)], axis=0)
            smat2 = s_s[lo:hi, :]
            y2 = (jnp.dot((scores2 * seg2).astype(bf16), xbd.astype(bf16), preferred_element_type=f32)
                  + pick(exp_col, h0) * lax.dot_general(cm, smat2.astype(bf16), _NT, preferred_element_type=f32))
            last0 = cum_row[h0:h0 + 1, cl - 1:cl]
            last1 = cum_row[h0 + 1:h0 + 2, cl - 1:cl]
            xw2 = (xdt2 * jnp.exp(jnp.where(left_row, last0, last1) - cc2)).astype(bf16)
            s_s[lo:hi, :] = (jnp.where(top, jnp.exp(last0), jnp.exp(last1)) * smat2
                             + lax.dot_general(xw2, bm, _TN, preferred_element_type=f32))
            dsk2 = jnp.where(left_row, dskip_ref[:, h0:h0 + 1], dskip_ref[:, h0 + 1:h0 + 2])
            ys_s[:, lo:hi] = y2 + dsk2 * x2
    z = z_ref[...]
    y_ref[...] = _rms(ys_s[...] * (z * jax.nn.sigmoid(z)), gssd_ref[...]).astype(y_ref.dtype)

    @pl.when(c == pl.num_programs(1) - 1)
    def _():
        s_out[0] = s_s[...]


def _ssd_prompt(uzx, gd, dt_t, conv_w, conv_b, dt_bias, a_log, d_skip, g_ssd, *, batch, seq):
    heads, hd, ns, groups, width = SSD_HEADS, SSD_HEAD_DIM, SSD_STATE, SSD_GROUPS, SSD_WIDTH
    cl = math.gcd(seq, CHUNK)
    assert cl == hd and (heads // groups) % 2 == 0
    nc = seq // cl
    cch = SSD_CONV_CH
    rows = lambda b, c: (b * nc + c, 0)
    fix = lambda b, c: (0, 0)
    body = functools.partial(_ssd_prompt_body, heads=heads, hd=hd, ns=ns, groups=groups, cl=cl, width=width)
    return pl.pallas_call(
        body, grid=(batch, nc),
        in_specs=[pl.BlockSpec((cl, cch), lambda b, c: (b * nc + c, 1)),
                  pl.BlockSpec((cl, width), lambda b, c: (b * nc + c, 1)),
                  pl.BlockSpec((cl, gd.shape[1]), rows),
                  pl.BlockSpec((1, 1, heads, cl), lambda b, c: (b, c, 0, 0)),
                  pl.BlockSpec((SSD_CONV, cch), fix), pl.BlockSpec((1, cch), fix),
                  pl.BlockSpec((1, heads), fix), pl.BlockSpec((heads, 1), fix),
                  pl.BlockSpec((1, heads), fix), pl.BlockSpec((heads, 1), fix),
                  pl.BlockSpec((1, heads), fix), pl.BlockSpec((1, width), fix)],
        out_specs=[pl.BlockSpec((cl, width), rows),
                   pl.BlockSpec((1, heads * hd, ns), lambda b, c: (b, 0, 0))],
        out_shape=[jax.ShapeDtypeStruct((batch * seq, width), bf16),
                   jax.ShapeDtypeStruct((batch, heads * hd, ns), f32)],
        scratch_shapes=[pltpu.VMEM((heads * hd, ns), f32), pltpu.VMEM((cl + 8, cch), f32),
                        pltpu.VMEM((cl, width), f32)],
        compiler_params=_params("arbitrary", "arbitrary"), name="ssd_prompt",
    )(uzx, uzx, gd, dt_t, conv_w, conv_b.reshape(1, cch), dt_bias.reshape(1, heads), dt_bias.reshape(heads, 1),
      a_log.reshape(1, heads), a_log.reshape(heads, 1), d_skip.reshape(1, heads), g_ssd.reshape(1, width))


def _ssd_sample_body(*refs, heads, hd, ns, groups, width, aliased):
    if aliased:
        refs = refs[:10] + refs[11:]
    (x_ref, conv0_ref, gd_ref, cw_ref, cb_ref, dtb_ref, alog_ref, dskip_ref, gssd_ref, s0_ref,
     y_ref, s_out, ys_s) = refs
    bt = x_ref.shape[0]

    @pl.when(pl.program_id(0) > 0)
    def _():
        s_out[...] = jnp.zeros_like(s_out)

    @pl.when(pl.program_id(0) == 0)
    def _():
        z = x_ref[:, width:2 * width]
        xc = cb_ref[...] + cw_ref[SSD_CONV - 1:SSD_CONV, :] * x_ref[:, 2 * width:]
        for j in range(SSD_CONV - 1):
            xc = xc + cw_ref[j:j + 1, :] * conv0_ref[:, j, :]
        xc = xc * jax.nn.sigmoid(xc)
        dt = jax.nn.softplus(gd_ref[:, 8:8 + heads] + dtb_ref[...])
        ea = jnp.exp(dt * -jnp.exp(alog_ref[...]))
        rep = heads // groups
        for gi in range(groups):
            bm = xc[:, width + gi * ns:width + (gi + 1) * ns]
            cm = xc[:, width + (groups + gi) * ns:width + (groups + gi + 1) * ns]
            cb_dot = jnp.sum(cm * bm, axis=1, keepdims=True)
            cmb = cm.astype(bf16)
            for hh in range(gi * rep, (gi + 1) * rep):
                xh = xc[:, hh * hd:(hh + 1) * hd]
                xdt = xh * dt[:, hh:hh + 1]
                eah = ea[:, hh:hh + 1]
                xdt_t = xdt.T
                sc_rows = []
                for b in range(bt):
                    smat = s0_ref[b, hh]
                    sc_rows.append(lax.dot_general(cmb, smat.astype(bf16), _NT,
                                                   preferred_element_type=f32)[b:b + 1, :])
                    s_out[b, hh] = eah[b:b + 1, :] * smat + xdt_t[:, b:b + 1] * bm[b:b + 1, :]
                sc = jnp.concatenate(sc_rows, axis=0)
                ys_s[:, hh * hd:(hh + 1) * hd] = cb_dot * xdt + eah * sc + dskip_ref[:, hh:hh + 1] * xh
        y_ref[...] = _rms(ys_s[...] * (z * jax.nn.sigmoid(z)), gssd_ref[...])


def _ssd_sample(uzx, conv0, gd, conv_w, conv_b, dt_bias, a_log, d_skip, g_ssd, s0, *, layer, row0, s_stack=None):
    heads, hd, ns, groups, width = SSD_HEADS, SSD_HEAD_DIM, SSD_STATE, SSD_GROUPS, SSD_WIDTH
    cch = SSD_CONV_CH
    b = s0.shape[1]
    bt = _SAMPLE_BLOCK
    blk0 = row0 // bt
    fix = lambda l, i: (0, 0)
    grid, row, out_layer = _stacked_state_grid(layer, b // bt, s_stack)
    aliased = s_stack is not None
    body = functools.partial(_ssd_sample_body, heads=heads, hd=hd, ns=ns, groups=groups, width=width,
                             aliased=aliased)
    in_specs = [pl.BlockSpec((bt, uzx.shape[-1]), lambda l, i: (blk0 + row(l, i), 0)),
                pl.BlockSpec((None, bt, SSD_CONV - 1, cch), lambda l, i: (layer, row(l, i), 0, 0)),
                pl.BlockSpec((bt, gd.shape[-1]), lambda l, i: (blk0 + row(l, i), 0)),
                pl.BlockSpec((SSD_CONV, cch), fix), pl.BlockSpec((1, cch), fix),
                pl.BlockSpec((1, heads), fix), pl.BlockSpec((1, heads), fix), pl.BlockSpec((1, heads), fix),
                pl.BlockSpec((1, width), fix),
                pl.BlockSpec((None, bt, heads, hd, ns), lambda l, i: (layer, row(l, i), 0, 0, 0))]
    args = [uzx, conv0, gd, conv_w, conv_b.reshape(1, cch), dt_bias.reshape(1, heads), a_log.reshape(1, heads),
            d_skip.reshape(1, heads), g_ssd.reshape(1, width), s0]
    if aliased:
        in_specs.append(pl.BlockSpec(memory_space=pl.ANY))
        args.append(s_stack)
    return pl.pallas_call(
        body, grid=grid, in_specs=in_specs,
        out_specs=[pl.BlockSpec((bt, width), lambda l, i: (row(l, i), 0)),
                   pl.BlockSpec((None, bt, heads, hd, ns), lambda l, i: (out_layer(l), i, 0, 0, 0))],
        out_shape=[jax.ShapeDtypeStruct((b, width), f32),
                   jax.ShapeDtypeStruct((DEPTH, b, heads, hd, ns), f32)],
        scratch_shapes=[pltpu.VMEM((bt, width), f32)],
        input_output_aliases={10: 1} if aliased else {},
        compiler_params=_params("arbitrary", "arbitrary"), name="ssd_sample",
    )(*args)


def _router_body(h_ref, g_ref, wr_ref, br_ref, cf_ref, idx_ref, gate_ref, *, n_exp):
    cf = _rms(h_ref[...], g_ref[...])
    cf_ref[...] = cf
    logits = jnp.dot(cf, wr_ref[...], precision=HI, preferred_element_type=f32) + br_ref[...]
    lane = lax.broadcasted_iota(jnp.int32, logits.shape, 1)
    m1 = jnp.max(logits, axis=1, keepdims=True)
    i1 = jnp.min(jnp.where(logits == m1, lane, n_exp), axis=1, keepdims=True)
    rest = jnp.where(lane == i1, -jnp.inf, logits)
    m2 = jnp.max(rest, axis=1, keepdims=True)
    i2 = jnp.min(jnp.where(rest == m2, lane, n_exp), axis=1, keepdims=True)
    e2 = jnp.exp(m2 - m1)
    g1 = 1.0 / (1.0 + e2)
    two = lax.broadcasted_iota(jnp.int32, (logits.shape[0], TOP_K), 1)
    idx_ref[...] = jnp.where(two == 0, i1, i2)
    gate_ref[...] = jnp.where(two == 0, g1, e2 * g1)


def _router(h, g_ffn, w_router, b_router, *, tm):
    m, d = h.shape
    e = w_router.shape[-1]
    row = lambda i: (i, 0)
    fix = lambda i: (0, 0)
    return pl.pallas_call(
        functools.partial(_router_body, n_exp=e), grid=(m // tm,),
        in_specs=[pl.BlockSpec((tm, d), row), pl.BlockSpec((1, d), fix), pl.BlockSpec((d, e), fix),
                  pl.BlockSpec((1, e), fix)],
        out_specs=[pl.BlockSpec((tm, d), row), pl.BlockSpec((tm, TOP_K), row),
                   pl.BlockSpec((tm, TOP_K), row)],
        out_shape=[jax.ShapeDtypeStruct((m, d), f32),
                   jax.ShapeDtypeStruct((m, TOP_K), jnp.int32), jax.ShapeDtypeStruct((m, TOP_K), f32)],
        compiler_params=_params("parallel"), name="router",
    )(h, g_ffn.reshape(1, d), w_router, b_router.reshape(1, e))


def _row_copy(src_hbm, dst, sem, src_row, dst_row):
    return pltpu.make_async_copy(src_hbm.at[pl.ds(src_row, 1)], dst.at[pl.ds(dst_row, 1)], sem)


_DMA_UNROLL = 8


def _gather_body(idx_ref, x_hbm, o_ref, buf, sem, *, tg):
    i = pl.program_id(0)

    def issue(tile, slot):
        base = tile * tg

        def start(grp, carry):
            for u in range(_DMA_UNROLL):
                r = u * (tg // _DMA_UNROLL) + grp
                _row_copy(x_hbm, buf.at[slot], sem.at[slot], idx_ref[base + r], r).start(priority=u % 2)
            return carry

        lax.fori_loop(0, tg // _DMA_UNROLL, start, 0)

    @pl.when(i == 0)
    def _():
        issue(0, 0)

    @pl.when(i + 1 < pl.num_programs(0))
    def _():
        issue(i + 1, (i + 1) % 2)

    slot = i % 2
    pltpu.make_async_copy(x_hbm.at[pl.ds(0, tg)], buf.at[slot], sem.at[slot]).wait()
    o_ref[...] = buf[slot].astype(o_ref.dtype)


def _gather_rows(x, row_idx, *, tg, out_dtype):
    r = row_idx.shape[0]
    d = x.shape[1]
    return pl.pallas_call(
        functools.partial(_gather_body, tg=tg),
        grid_spec=pltpu.PrefetchScalarGridSpec(
            num_scalar_prefetch=1, grid=(r // tg,),
            in_specs=[pl.BlockSpec(memory_space=pl.ANY)],
            out_specs=pl.BlockSpec((tg, d), lambda i, idx: (i, 0)),
            scratch_shapes=[pltpu.VMEM((2, tg, d), x.dtype), pltpu.SemaphoreType.DMA((2,))]),
        out_shape=jax.ShapeDtypeStruct((r, d), out_dtype),
        compiler_params=_params("arbitrary"), name="gather_rows")(row_idx, x)


def _gmm_body(te_ref, first_ref, next_ref, nv_ref, a_ref, *rest, n_w, swiglu, n_sub):
    w_hbm = rest[:n_w]
    o_ref, wf_ref, wb_ref, sem, run_ref = rest[n_w:]
    j = pl.program_id(0)
    tn = o_ref.shape[1]
    tm = o_ref.shape[0] // n_sub

    def tile_copies(expert, col_tile, slot):
        col = pl.multiple_of(col_tile * tn, 128)
        return [pltpu.make_async_copy(w_hbm[i].at[expert, :, pl.ds(col, tn)], wf_ref.at[slot, i], sem.at[slot, i])
                for i in range(n_w)]

    @pl.when((j == 0) & (pl.program_id(1) == 0))
    def _():
        run_ref[0] = 0
        for c in tile_copies(te_ref[0], 0, 0):
            c.start()

    for sub in range(n_sub):
        t = pl.program_id(1) * n_sub + sub
        rows = slice(sub * tm, (sub + 1) * tm)

        @pl.when((t < nv_ref[0]) & (first_ref[t] == 1))
        def _():
            slot = run_ref[0] % 2
            for c in tile_copies(te_ref[t], j, slot):
                c.wait()
            for i in range(n_w):
                wb_ref[i] = wf_ref[slot, i].astype(bf16)
            nxt = next_ref[t]

            @pl.when(nxt >= 0)
            def _():
                for c in tile_copies(te_ref[nxt], j, 1 - slot):
                    c.start()

            @pl.when((nxt < 0) & (j + 1 < pl.num_programs(0)))
            def _():
                for c in tile_copies(te_ref[0], j + 1, 1 - slot):
                    c.start()

            run_ref[0] = run_ref[0] + 1

        @pl.when(t < nv_ref[0])
        def _():
            a = a_ref[rows, :]
            if swiglu:
                g = jnp.dot(a, wb_ref[0], preferred_element_type=f32)
                u = jnp.dot(a, wb_ref[1], preferred_element_type=f32)
                o_ref[rows, :] = (g * jax.nn.sigmoid(g) * u).astype(o_ref.dtype)
            else:
                o_ref[rows, :] = jnp.dot(a, wb_ref[0], preferred_element_type=f32).astype(o_ref.dtype)

        @pl.when(t >= nv_ref[0])
        def _():
            o_ref[rows, :] = jnp.zeros((tm, tn), o_ref.dtype)


def _gmm(a_sorted, weights, tables, *, tm, tn, n_sub, swiglu, out_dtype, name):
    tile_expert, tile_first, tile_next, n_valid = tables
    r, k = a_sorted.shape
    n = weights[0].shape[-1]
    n_w = len(weights)
    assert (r // tm) % n_sub == 0
    return pl.pallas_call(
        functools.partial(_gmm_body, n_w=n_w, swiglu=swiglu, n_sub=n_sub),
        grid_spec=pltpu.PrefetchScalarGridSpec(
            num_scalar_prefetch=4, grid=(n // tn, r // (tm * n_sub)),
            in_specs=([pl.BlockSpec((tm * n_sub, k), lambda j, t, *_: (t, 0))]
                      + [pl.BlockSpec(memory_space=pl.ANY)] * n_w),
            out_specs=pl.BlockSpec((tm * n_sub, tn), lambda j, t, *_: (t, j)),
            scratch_shapes=[pltpu.VMEM((2, n_w, k, tn), f32), pltpu.VMEM((n_w, k, tn), bf16),
                            pltpu.SemaphoreType.DMA((2, n_w)), pltpu.SMEM((1,), jnp.int32)]),
        out_shape=jax.ShapeDtypeStruct((r, n), out_dtype),
        compiler_params=_params("arbitrary", "arbitrary", vmem=_VMEM_LIMIT_GMM), name=name,
    )(tile_expert, tile_first, tile_next, n_valid, a_sorted, *weights)


def _combine_body(pos_ref, y_hbm, gate_ref, r_ref, gn_ref, o_ref, e_ref, buf, sem, *, tc):
    i = pl.program_id(0)

    def issue(tile, slot):
        base = tile * tc

        def start(r, carry):
            for kk in range(TOP_K):
                _row_copy(y_hbm, buf.at[slot, kk], sem.at[slot], pos_ref[(base + r) * TOP_K + kk], r).start()
            return carry

        lax.fori_loop(0, tc, start, 0, unroll=_DMA_UNROLL)

    @pl.when(i == 0)
    def _():
        issue(0, 0)

    @pl.when(i + 1 < pl.num_programs(0))
    def _():
        issue(i + 1, (i + 1) % 2)

    slot = i % 2
    for kk in range(TOP_K):
        pltpu.make_async_copy(y_hbm.at[pl.ds(0, tc)], buf.at[slot, kk], sem.at[slot]).wait()
    gate = gate_ref[...]
    out = r_ref[...] + gate[:, 0:1] * buf[slot, 0] + gate[:, 1:2] * buf[slot, 1]
    o_ref[...] = out
    e_ref[...] = _rms(out, gn_ref[...]).astype(e_ref.dtype)


def _combine(y_sorted, pos_flat, gates, resid, g_next, *, tc):
    m, d = resid.shape
    rows = lambda i, pos: (i, 0)
    return pl.pallas_call(
        functools.partial(_combine_body, tc=tc),
        grid_spec=pltpu.PrefetchScalarGridSpec(
            num_scalar_prefetch=1, grid=(m // tc,),
            in_specs=[pl.BlockSpec(memory_space=pl.ANY), pl.BlockSpec((tc, TOP_K), rows),
                      pl.BlockSpec((tc, d), rows), pl.BlockSpec((1, d), lambda i, pos: (0, 0))],
            out_specs=[pl.BlockSpec((tc, d), rows), pl.BlockSpec((tc, d), rows)],
            scratch_shapes=[pltpu.VMEM((2, TOP_K, tc, d), f32), pltpu.SemaphoreType.DMA((2,))]),
        out_shape=[jax.ShapeDtypeStruct((m, d), f32), jax.ShapeDtypeStruct((m, d), bf16)],
        compiler_params=_params("arbitrary"), name="moe_combine",
    )(pos_flat, y_sorted, gates, resid, g_next.reshape(1, d))


def _routing_tables(top_i, n_exp, tm, n_tiles):
    m = top_i.shape[0]
    e_flat = top_i.reshape(-1)
    onehot = (e_flat[:, None] == jnp.arange(n_exp, dtype=jnp.int32)[None, :]).astype(jnp.int32)
    rank = jnp.take_along_axis(jnp.cumsum(onehot, axis=0), e_flat[:, None], axis=1)[:, 0] - 1
    counts = jnp.sum(onehot, axis=0)
    tiles_per = (counts + tm - 1) // tm
    tile_end = jnp.cumsum(tiles_per)
    tile_start = tile_end - tiles_per
    pos = tile_start[e_flat] * tm + rank
    token = jnp.arange(m * TOP_K, dtype=jnp.int32) // TOP_K
    row_token = jnp.zeros((n_tiles * tm,), jnp.int32).at[pos].set(token)
    n_valid = tile_end[-1]
    tid = jnp.minimum(jnp.arange(n_tiles, dtype=jnp.int32), n_valid - 1)
    tile_expert = jnp.sum((tid[:, None] >= tile_end[None, :]).astype(jnp.int32), axis=1)
    tile_first = jnp.concatenate([jnp.ones((1,), jnp.int32),
                                  (tile_expert[1:] != tile_expert[:-1]).astype(jnp.int32)])
    ids = jnp.arange(n_tiles, dtype=jnp.int32)
    starts = jnp.where((tile_first == 1) & (ids < n_valid), ids, n_tiles)
    later = jnp.concatenate([lax.cummin(starts, reverse=True)[1:], jnp.full((1,), n_tiles, jnp.int32)])
    tile_next = jnp.where(later >= n_tiles, -1, later).astype(jnp.int32)
    return (pos.astype(jnp.int32), row_token,
            (tile_expert.astype(jnp.int32), tile_first, tile_next, n_valid.reshape(1).astype(jnp.int32)))


def _moe_ffn(h, g_ffn, w_router, b_router, wg, wu, wd, g_next, *, tm_tok, tm, tn_up, tn_down):
    m = h.shape[0]
    n_exp = wg.shape[0]
    c_f32, top_i, top_g = _router(h, g_ffn, w_router, b_router, tm=tm_tok)
    n_tiles = 2 * (((m * TOP_K) // tm + n_exp + 1) // 2)
    pos, row_token, tables = _routing_tables(top_i, n_exp, tm, n_tiles)
    x_sorted = _gather_rows(c_f32, row_token, tg=tm, out_dtype=bf16)
    h_sorted = _gmm(x_sorted, (wg, wu), tables, tm=tm, tn=tn_up, n_sub=2, swiglu=True, out_dtype=bf16,
                    name="moe_gate_up")
    y_sorted = _gmm(h_sorted, (wd,), tables, tm=tm, tn=tn_down, n_sub=1, swiglu=False, out_dtype=f32,
                    name="moe_down")
    return _combine(y_sorted, pos, top_g, h, g_next, tc=tm_tok // 2)


def kernel(x_prompt, x_sample, state_mlstm_C, state_mlstm_n, state_mlstm_m, state_s5_re, state_s5_im, state_ssd, cache_conv, p_prompt, p_sample, g_mix, w_in, b_igate, b_fgate, g_ml, s5_lam_re, s5_lam_im, s5_log_dt, s5_b_re, s5_b_im, s5_c_re, s5_c_im, s5_d, s5_w_glu, s5_b_glu, g_s5, ssd_conv_w, ssd_conv_b, ssd_dt_bias, ssd_a_log, ssd_d, g_ssd, w_out, g_ffn, ffn_w_gate, ffn_w_up, ffn_w_down, w_router, b_router, moe_w_gate, moe_w_up, moe_w_down, g_ple, w_ple, w_ple_gate, g_final):
    bp, seq, d = x_prompt.shape
    bs = x_sample.shape[0]
    tp = bp * seq
    m = tp + bs
    tm = _token_tile(tp, bs)
    heads = ML_HEADS
    t5 = math.gcd(seq, S5_CHUNK)
    cl = math.gcd(seq, CHUNK)
    nc = seq // cl

    p_p = p_prompt.reshape(DEPTH, tp, -1)
    p_s = p_sample.reshape(DEPTH, bs, -1)
    w_in_t = jnp.swapaxes(w_in, 1, 2)
    outs_p = [[] for _ in range(7)]
    outs_s = [[] for _ in range(5)]
    c_stack = s_stack = None
    mats_p, mats_s = jax.vmap(functools.partial(_s5_matrices, t=t5))(
        s5_lam_re, s5_lam_im, s5_log_dt, s5_b_re, s5_b_im, s5_c_re, s5_c_im)

    for i in range(DEPTH):
        if i == 0:
            h, a = _rmsnorm_in(x_prompt.reshape(tp, d), x_sample.reshape(bs, d), g_mix[i], tm)
        else:
            a = _rmsnorm(h, g_mix[i], bf16, tm)
        qkvo = _mm_nt(a, w_in_t, tm=tm, tn=1024, layer=i, row_off=0, n_rows=_OFF_GATES, name="in_proj_qkvo")
        uzx = _mm_nt(a, w_in_t, tm=tm, tn=1024, layer=i, row_off=_OFF_U, n_rows=_OFF_DT - _OFF_U, name="in_proj_uzx")
        gd = _mm_gates(a, w_in_t, tm=tm, layer=i, off1=_OFF_GATES, n1=2 * heads, off2=_OFF_DT, n2=SSD_HEADS,
                       name="in_proj_gates")

        gates_t = jnp.transpose(gd[:tp, :2 * heads].reshape(bp, nc, cl, 2 * heads), (0, 1, 3, 2))
        bias8 = jnp.concatenate([b_igate[i], b_fgate[i]])
        h_ml_p, c_p, n_p, m_p = _mlstm_prompt(qkvo, gates_t, bias8.reshape(2 * heads, 1), g_ml[i], batch=bp, seq=seq)
        h_ml_s, c_stack, n_s, m_s = _mlstm_sample(qkvo, gd, bias8.reshape(1, 2 * heads), g_ml[i], state_mlstm_C,
                                                  state_mlstm_n, state_mlstm_m, layer=i, row0=tp, c_stack=c_stack)

        y_p, x5_p = _s5_prompt(uzx, mats_p, batch=bp, seq=seq, layer=i)
        s5re_p, s5im_p = x5_p[..., :S5_STATE], x5_p[..., S5_STATE:]
        u_s = jnp.transpose(uzx[tp:, :S5_WIDTH].reshape(bs, S5_GROUPS, S5_CH), (1, 0, 2)).astype(bf16)
        x0_s = jnp.swapaxes(jnp.concatenate([state_s5_re[i], state_s5_im[i]], axis=-1), 0, 1)
        y_s, x5_s = _s5_scan(u_s, mats_s, x0_s, nc=1, batch=bs, layer=i)
        s5re_s, s5im_s = x5_s[..., :S5_STATE], x5_s[..., S5_STATE:]
        y_s = jnp.transpose(y_s, (1, 0, 2)).reshape(bs, S5_WIDTH)
        y5 = _s5_glu(y_p, y_s, uzx, s5_d[i].reshape(1, S5_WIDTH), s5_w_glu,
                     s5_b_glu[i].reshape(1, S5_WIDTH), g_s5[i].reshape(1, S5_WIDTH), layer=i, tm=tm)

        dt_t = jnp.transpose(gd[:tp, 2 * heads:2 * heads + SSD_HEADS].reshape(bp, nc, cl, SSD_HEADS), (0, 1, 3, 2))
        y_ssd_p, ssd_p = _ssd_prompt(uzx, gd, dt_t, ssd_conv_w[i], ssd_conv_b[i], ssd_dt_bias[i], ssd_a_log[i],
                                     ssd_d[i], g_ssd[i], batch=bp, seq=seq)
        y_ssd_s, s_stack = _ssd_sample(uzx, cache_conv, gd, ssd_conv_w[i], ssd_conv_b[i], ssd_dt_bias[i],
                                       ssd_a_log[i], ssd_d[i], g_ssd[i], state_ssd, layer=i, row0=tp,
                                       s_stack=s_stack)
        conv_p = jnp.stack([uzx[(b + 1) * seq - (SSD_CONV - 1):(b + 1) * seq, S5_WIDTH + SSD_WIDTH:]
                            for b in range(bp)])
        conv_s = jnp.concatenate([cache_conv[i][:, 1:], uzx[tp:, S5_WIDTH + SSD_WIDTH:].reshape(bs, 1, -1)], axis=1)

        for lst, s in zip(outs_p, (c_p, n_p.reshape(bp, heads, ML_DK), m_p.reshape(bp, heads),
                                   jnp.swapaxes(s5re_p, 0, 1), jnp.swapaxes(s5im_p, 0, 1),
                                   ssd_p.reshape(bp, SSD_HEADS, SSD_HEAD_DIM, SSD_STATE), conv_p)):
            lst.append(s)
        for lst, s in zip(outs_s, (n_s, m_s, jnp.swapaxes(s5re_s, 0, 1), jnp.swapaxes(s5im_s, 0, 1), conv_s)):
            lst.append(s)

        h = _mm_mix(h_ml_p.reshape(tp, -1), h_ml_s, y5, y_ssd_p.reshape(tp, -1), y_ssd_s, w_out, h, tm=tm, tn=1024,
                    layer=i, name="out_proj")

        j = i // 2
        if i % 2 == 0:
            cn = _rmsnorm(h, g_ffn[i], bf16, tm)
            tm_up = 2 * tm if m % (2 * tm) == 0 else tm
            hid = _mm_swiglu(cn, ffn_w_gate, ffn_w_up, tm=tm_up, tn=512, layer=j, name="ffn_gate_up")
            h = _mm(hid, ffn_w_down, tm=tm // 2, tn=512, layer=j, resid=h, name="ffn_down")
            e = _rmsnorm(h, g_ple[i], bf16, tm)
        else:
            n_moe = moe_w_gate.shape[0]
            sel = lambda w: w.reshape((n_moe * N_EXPERTS,) + w.shape[2:])[j * N_EXPERTS:(j + 1) * N_EXPERTS] if n_moe > 1 else w.reshape(w.shape[1:])
            h, e = _moe_ffn(h, g_ffn[i], w_router[j], b_router[j], sel(moe_w_gate), sel(moe_w_up), sel(moe_w_down),
                            g_ple[i], tm_tok=tm, tm=256, tn_up=1024, tn_down=512)

        h = _mm_ple(e, p_p, p_s, w_ple_gate, w_ple, h, tm=tm, tn=1024, layer=i)

    y_p, y_s = _rmsnorm_out(h, g_final, tp, tm)
    n_s, m_s, s5re_s, s5im_s, conv_s = (jnp.stack(l) for l in outs_s)
    return ((y_p.reshape(bp, seq, d), y_s.reshape(bs, 1, d)) + tuple(jnp.stack(l) for l in outs_p)
            + (c_stack, n_s, m_s, s5re_s, s5im_s, s_stack, conv_s))
```

```python
import functools
import math

import jax
import jax.numpy as jnp
from jax import lax
from jax.experimental import pallas as pl
from jax.experimental.pallas import tpu as pltpu

f32 = jnp.float32
bf16 = jnp.bfloat16
HI = lax.Precision.HIGHEST

DEPTH = 2
ML_HEADS = 4
ML_DK = 256
ML_DV = 256
ML_WIDTH = ML_HEADS * ML_DV
S5_CH = 16
S5_WIDTH = 512
S5_GROUPS = 32
S5_STATE = 64
SSD_HEAD_DIM = 64
SSD_WIDTH = 512
SSD_HEADS = 8
SSD_GROUPS = 2
SSD_STATE = 128
SSD_CONV = 4
SSD_CONV_CH = 1024
CHUNK = 64
S5_CHUNK = 32
N_EXPERTS = 8
TOP_K = 2
RMS_EPS = 1e-6

_OFF_GATES = 4 * ML_WIDTH
_OFF_U = _OFF_GATES + 2 * ML_HEADS
_OFF_DT = _OFF_U + S5_WIDTH + SSD_WIDTH + SSD_CONV_CH

_VMEM_LIMIT = 56 * 1024 * 1024
_VMEM_LIMIT_GMM = 60 * 1024 * 1024
_NT = (((1,), (1,)), ((), ()))
_TN = (((0,), (0,)), ((), ()))


def _params(*sem, vmem=_VMEM_LIMIT):
    return pltpu.CompilerParams(dimension_semantics=sem, vmem_limit_bytes=vmem)


def _rms(x, g):
    return x * lax.rsqrt(jnp.mean(x * x, axis=-1, keepdims=True) + RMS_EPS) * g


def _rmsnorm_body(x_ref, g_ref, o_ref):
    o_ref[...] = _rms(x_ref[...], g_ref[...]).astype(o_ref.dtype)


def _last_tile(p_tile, s_rows, n_keep):
    return jnp.concatenate([p_tile[0:n_keep, :], s_rows], axis=0)


def _split_rows(tp, bs, tm):
    n_tiles = (tp + bs) // tm
    n_keep = tp - (n_tiles - 1) * tm
    assert n_tiles * tm == tp + bs and 0 < n_keep and n_keep + bs == tm and n_keep % 16 == 0
    return n_tiles, n_keep


def _token_tile(tp, bs):
    m = tp + bs
    for n_tiles in (10, 8, 5, 4, 2, 1):
        tm = m // n_tiles
        if m % n_tiles == 0 and tm % 32 == 0 and bs < tm and (tp - (n_tiles - 1) * tm) % 16 == 0:
            return tm
    raise ValueError("no row tiling for these token counts")


def _rmsnorm_in_body(xp_ref, xs_ref, g_ref, h_ref, a_ref, *, n_keep):
    last = pl.num_programs(0) - 1

    def run(x):
        h_ref[...] = x
        a_ref[...] = _rms(x, g_ref[...]).astype(a_ref.dtype)

    @pl.when(pl.program_id(0) < last)
    def _():
        run(xp_ref[...])

    @pl.when(pl.program_id(0) == last)
    def _():
        run(_last_tile(xp_ref, xs_ref[...], n_keep))


def _rmsnorm_in(xp, xs, g, tm):
    tp, d = xp.shape
    bs = xs.shape[0]
    n_tiles, n_keep = _split_rows(tp, bs, tm)
    row = lambda i: (i, 0)
    fix = lambda i: (0, 0)
    return pl.pallas_call(
        functools.partial(_rmsnorm_in_body, n_keep=n_keep), grid=(n_tiles,),
        in_specs=[pl.BlockSpec((tm, d), row), pl.BlockSpec((bs, d), fix), pl.BlockSpec((1, d), fix)],
        out_specs=[pl.BlockSpec((tm, d), row), pl.BlockSpec((tm, d), row)],
        out_shape=[jax.ShapeDtypeStruct((tp + bs, d), f32), jax.ShapeDtypeStruct((tp + bs, d), bf16)],
        compiler_params=_params("parallel"), name="rmsnorm_in")(xp, xs, g.reshape(1, d))


def _rmsnorm_out_body(x_ref, g_ref, yp_ref, ys_ref, *, n_keep):
    y = _rms(x_ref[...], g_ref[...])
    yp_ref[...] = y

    @pl.when(pl.program_id(0) == pl.num_programs(0) - 1)
    def _():
        ys_ref[...] = y[n_keep:, :]


def _rmsnorm_out(x, g, tp, tm):
    m, d = x.shape
    bs = m - tp
    n_tiles, n_keep = _split_rows(tp, bs, tm)
    return pl.pallas_call(
        functools.partial(_rmsnorm_out_body, n_keep=n_keep), grid=(n_tiles,),
        in_specs=[pl.BlockSpec((tm, d), lambda i: (i, 0)), pl.BlockSpec((1, d), lambda i: (0, 0))],
        out_specs=[pl.BlockSpec((tm, d), lambda i: (i, 0)), pl.BlockSpec((bs, d), lambda i: (0, 0))],
        out_shape=[jax.ShapeDtypeStruct((tp, d), f32), jax.ShapeDtypeStruct((bs, d), f32)],
        compiler_params=_params("arbitrary"), name="rmsnorm_out")(x, g.reshape(1, d))


def _rmsnorm(x, g, out_dtype, tm):
    m, d = x.shape
    return pl.pallas_call(
        _rmsnorm_body, grid=(m // tm,),
        in_specs=[pl.BlockSpec((tm, d), lambda i: (i, 0)), pl.BlockSpec((1, d), lambda i: (0, 0))],
        out_specs=pl.BlockSpec((tm, d), lambda i: (i, 0)),
        out_shape=jax.ShapeDtypeStruct((m, d), out_dtype),
        compiler_params=_params("parallel"), name="rmsnorm")(x, g.reshape(1, d))


def _cast_weight_once(w_ref, wb_ref):
    @pl.when(pl.program_id(1) == 0)
    def _():
        wb_ref[...] = w_ref[...].astype(bf16)


def _mm_plain_body(a_ref, w_ref, o_ref, wb_ref):
    _cast_weight_once(w_ref, wb_ref)
    o_ref[...] = jnp.dot(a_ref[...], wb_ref[...], preferred_element_type=f32).astype(o_ref.dtype)


def _mm_resid_body(a_ref, w_ref, r_ref, o_ref, wb_ref):
    _cast_weight_once(w_ref, wb_ref)
    o_ref[...] = r_ref[...] + jnp.dot(a_ref[...], wb_ref[...], preferred_element_type=f32)


def _mm_swiglu_body(a_ref, wg_ref, wu_ref, o_ref, wgb_ref, wub_ref):
    _cast_weight_once(wg_ref, wgb_ref)
    _cast_weight_once(wu_ref, wub_ref)
    a = a_ref[...]
    g = jnp.dot(a, wgb_ref[...], preferred_element_type=f32)
    u = jnp.dot(a, wub_ref[...], preferred_element_type=f32)
    o_ref[...] = (g * jax.nn.sigmoid(g) * u).astype(o_ref.dtype)


def _mm_ple_body(e_ref, pp_ref, ps_ref, wg_ref, wp_ref, r_ref, o_ref, wgb_ref, wpb_ref, *, n_keep):
    _cast_weight_once(wg_ref, wgb_ref)
    _cast_weight_once(wp_ref, wpb_ref)
    last = pl.num_programs(1) - 1

    def run(p):
        gate = jnp.dot(e_ref[...], wgb_ref[...], preferred_element_type=f32)
        emb = jnp.dot(p.astype(bf16), wpb_ref[...], preferred_element_type=f32)
        o_ref[...] = r_ref[...] + emb * jax.nn.sigmoid(gate)

    @pl.when(pl.program_id(1) < last)
    def _():
        run(pp_ref[...])

    @pl.when(pl.program_id(1) == last)
    def _():
        run(_last_tile(pp_ref, ps_ref[...], n_keep))


def _mm_mix_body(a1p_ref, a1s_ref, a2_ref, a3p_ref, a3s_ref, w_ref, r_ref, o_ref, wb_ref, *, n_keep):
    _cast_weight_once(w_ref, wb_ref)
    last = pl.num_programs(1) - 1
    k1 = a1p_ref.shape[1]
    k2 = a2_ref.shape[1]

    def run(a1, a3):
        acc = jnp.dot(a1, wb_ref[0:k1, :], preferred_element_type=f32)
        acc += jnp.dot(a2_ref[...], wb_ref[k1:k1 + k2, :], preferred_element_type=f32)
        acc += jnp.dot(a3, wb_ref[k1 + k2:, :], preferred_element_type=f32)
        o_ref[...] = r_ref[...] + acc

    @pl.when(pl.program_id(1) < last)
    def _():
        run(a1p_ref[...], a3p_ref[...])

    @pl.when(pl.program_id(1) == last)
    def _():
        run(_last_tile(a1p_ref, a1s_ref[...].astype(bf16), n_keep),
            _last_tile(a3p_ref, a3s_ref[...].astype(bf16), n_keep))


def _mm_mix(a1p, a1s, a2, a3p, a3s, w, resid, *, tm, tn, layer, name):
    m, n = resid.shape
    tp, k1 = a1p.shape
    bs = a1s.shape[0]
    k2, k3 = a2.shape[1], a3p.shape[1]
    n_tiles, n_keep = _split_rows(tp, bs, tm)
    rows = lambda j, i: (i, 0)
    fix = lambda j, i: (0, 0)
    o_spec = pl.BlockSpec((tm, tn), lambda j, i: (i, j))
    return pl.pallas_call(
        functools.partial(_mm_mix_body, n_keep=n_keep), grid=(n // tn, n_tiles),
        in_specs=[pl.BlockSpec((tm, k1), rows), pl.BlockSpec((bs, k1), fix), pl.BlockSpec((tm, k2), rows),
                  pl.BlockSpec((tm, k3), rows), pl.BlockSpec((bs, k3), fix),
                  _wspec(w, layer, k1 + k2 + k3, tn), o_spec],
        out_specs=o_spec, out_shape=jax.ShapeDtypeStruct((m, n), f32),
        scratch_shapes=[pltpu.VMEM((k1 + k2 + k3, tn), bf16)],
        compiler_params=_params("arbitrary", "arbitrary"), name=name)(a1p, a1s, a2, a3p, a3s, w, resid)


def _wspec(w, layer, k, tn, col_block_off=0):
    if w.ndim == 2:
        return pl.BlockSpec((k, tn), lambda j, i: (0, j + col_block_off))
    return pl.BlockSpec((None, k, tn), lambda j, i: (layer, 0, j + col_block_off))


def _mm(a, w, *, tm, tn, layer=0, n_cols=None, col_off=0, resid=None, out_dtype=f32, name="mm"):
    m, k = a.shape
    n = n_cols if n_cols is not None else w.shape[-1]
    grid = (n // tn, m // tm)
    a_spec = pl.BlockSpec((tm, k), lambda j, i: (i, 0))
    o_spec = pl.BlockSpec((tm, tn), lambda j, i: (i, j))
    w_spec = _wspec(w, layer, k, tn, col_off // tn)
    scratch = [pltpu.VMEM((k, tn), bf16)]
    if resid is None:
        body, ins, specs = _mm_plain_body, (a, w), [a_spec, w_spec]
    else:
        body, ins, specs = _mm_resid_body, (a, w, resid), [a_spec, w_spec, o_spec]
    return pl.pallas_call(
        body, grid=grid, in_specs=specs, out_specs=o_spec,
        out_shape=jax.ShapeDtypeStruct((m, n), out_dtype), scratch_shapes=scratch,
        compiler_params=_params("arbitrary", "arbitrary"), name=name)(*ins)


def _mm_nt_body(a_ref, wt_ref, o_ref, wb_ref):
    @pl.when(pl.program_id(1) == 0)
    def _():
        wb_ref[...] = wt_ref[0].T.astype(bf16)

    o_ref[...] = jnp.dot(a_ref[...], wb_ref[...], preferred_element_type=f32)


def _wt_rows(layer, k, n_rows, row_of):
    return pl.BlockSpec((pl.Element(1), pl.Element(n_rows), pl.Element(k)),
                        lambda *idx: (layer, row_of(*idx), 0))


def _mm_nt(a, wt, *, tm, tn, layer, row_off, n_rows, name):
    m, k = a.shape
    return pl.pallas_call(
        _mm_nt_body, grid=(n_rows // tn, m // tm),
        in_specs=[pl.BlockSpec((tm, k), lambda j, i: (i, 0)),
                  _wt_rows(layer, k, tn, lambda j, i: pl.multiple_of(row_off + tn * j, 8))],
        out_specs=pl.BlockSpec((tm, tn), lambda j, i: (i, j)),
        out_shape=jax.ShapeDtypeStruct((m, n_rows), f32), scratch_shapes=[pltpu.VMEM((k, tn), bf16)],
        compiler_params=_params("arbitrary", "arbitrary"), name=name)(a, wt)


def _mm_gates_body(a_ref, w1_ref, w2_ref, o_ref):
    a = a_ref[...]
    n1 = w1_ref.shape[1]
    o_ref[:, 0:n1] = lax.dot_general(a, w1_ref[0].astype(bf16), _NT, preferred_element_type=f32)
    o_ref[:, n1:] = lax.dot_general(a, w2_ref[0].astype(bf16), _NT, preferred_element_type=f32)


def _mm_gates(a, wt, *, tm, layer, off1, n1, off2, n2, name):
    m, k = a.shape
    return pl.pallas_call(
        _mm_gates_body, grid=(m // tm,),
        in_specs=[pl.BlockSpec((tm, k), lambda i: (i, 0)),
                  _wt_rows(layer, k, n1, lambda i: off1), _wt_rows(layer, k, n2, lambda i: off2)],
        out_specs=pl.BlockSpec((tm, n1 + n2), lambda i: (i, 0)),
        out_shape=jax.ShapeDtypeStruct((m, n1 + n2), f32),
        compiler_params=_params("parallel"), name=name)(a, wt, wt)


def _mm_swiglu(a, wg, wu, *, tm, tn, layer=0, name="mm_swiglu"):
    m, k = a.shape
    n = wg.shape[-1]
    a_spec = pl.BlockSpec((tm, k), lambda j, i: (i, 0))
    o_spec = pl.BlockSpec((tm, tn), lambda j, i: (i, j))
    return pl.pallas_call(
        _mm_swiglu_body, grid=(n // tn, m // tm),
        in_specs=[a_spec, _wspec(wg, layer, k, tn), _wspec(wu, layer, k, tn)], out_specs=o_spec,
        out_shape=jax.ShapeDtypeStruct((m, n), bf16),
        scratch_shapes=[pltpu.VMEM((k, tn), bf16), pltpu.VMEM((k, tn), bf16)],
        compiler_params=_params("arbitrary", "arbitrary"), name=name)(a, wg, wu)


def _mm_ple(e, p_prompt, p_sample, w_gate, w_ple, resid, *, tm, tn, layer, name="mm_ple"):
    m, k = e.shape
    _, tp, kp = p_prompt.shape
    bs = p_sample.shape[1]
    n = w_gate.shape[-1]
    n_tiles, n_keep = _split_rows(tp, bs, tm)
    o_spec = pl.BlockSpec((tm, tn), lambda j, i: (i, j))
    return pl.pallas_call(
        functools.partial(_mm_ple_body, n_keep=n_keep), grid=(n // tn, n_tiles),
        in_specs=[pl.BlockSpec((tm, k), lambda j, i: (i, 0)),
                  pl.BlockSpec((None, tm, kp), lambda j, i: (layer, i, 0)),
                  pl.BlockSpec((None, bs, kp), lambda j, i: (layer, 0, 0)),
                  _wspec(w_gate, layer, k, tn), _wspec(w_ple, layer, kp, tn), o_spec],
        out_specs=o_spec, out_shape=jax.ShapeDtypeStruct((m, n), f32),
        scratch_shapes=[pltpu.VMEM((k, tn), bf16), pltpu.VMEM((kp, tn), bf16)],
        compiler_params=_params("arbitrary", "arbitrary"), name=name)(e, p_prompt, p_sample, w_gate, w_ple, resid)


def _col_from_row(row, eye):
    n = row.shape[1]
    return jnp.sum(jnp.where(eye, jnp.broadcast_to(row, (n, n)), 0.0), axis=1, keepdims=True)


def _mlstm_prompt_body(*refs, heads, dk, dv, cl, nb):
    q_refs, k_refs, v_refs, o_refs = (refs[i * nb:(i + 1) * nb] for i in range(4))
    gt_ref, bias_ref, gml_ref, h_ref, c_out, n_out, m_out, c_s, n_s, m_s = refs[4 * nb:]
    c = pl.program_id(1)

    @pl.when(c == 0)
    def _():
        c_s[...] = jnp.zeros_like(c_s)
        n_s[...] = jnp.zeros_like(n_s)
        m_s[...] = jnp.zeros_like(m_s)

    row = lax.broadcasted_iota(jnp.int32, (cl, cl), 0)
    col = lax.broadcasted_iota(jnp.int32, (cl, cl), 1)
    causal = col <= row
    eye = col == row
    triu = (row <= col).astype(f32)
    for r in range(nb):
        gt = gt_ref[r, 0] + bias_ref[...]
        lf = jax.nn.log_sigmoid(gt[heads:2 * heads])
        b_all = jnp.dot(lf, triu, precision=HI, preferred_element_type=f32)
        for hh in range(heads):
            st = r * heads + hh
            ig = gt[hh:hh + 1]
            b_row = b_all[hh:hh + 1]
            b_col = _col_from_row(b_row, eye)
            m_prev = m_s[st]
            d = jnp.where(causal, b_col - b_row + ig, -jnp.inf)
            inter = b_col + m_prev
            m_t = jnp.maximum(inter, jnp.max(d, axis=1, keepdims=True))
            w = jnp.exp(d - m_t)
            g = jnp.exp(inter - m_t)
            q = q_refs[r][:, hh * dk:(hh + 1) * dk]
            k = k_refs[r][:, hh * dk:(hh + 1) * dk] * (dk ** -0.5)
            vb = v_refs[r][:, hh * dv:(hh + 1) * dv].astype(bf16)
            qb = q.astype(bf16)
            cmat = c_s[st]
            n_row = n_s[st]
            s = lax.dot_general(qb, k.astype(bf16), _NT, preferred_element_type=f32) * w
            num = (jnp.dot(s.astype(bf16), vb, preferred_element_type=f32)
                   + g * jnp.dot(qb, cmat.astype(bf16), preferred_element_type=f32))
            den = jnp.sum(s, axis=1, keepdims=True) + g * jnp.sum(q * n_row, axis=1, keepdims=True)
            hraw = num / jnp.maximum(jnp.abs(den), jnp.exp(-m_t))
            hn = _rms(hraw, gml_ref[hh:hh + 1, :])
            ogate = jax.nn.sigmoid(o_refs[r][:, hh * dv:(hh + 1) * dv])
            h_ref[r, :, hh * dv:(hh + 1) * dv] = (ogate * hn).astype(h_ref.dtype)
            b_last = b_row[:, cl - 1:cl]
            dl = b_last - b_row + ig
            m_new = jnp.maximum(b_last + m_prev, jnp.max(dl, axis=1, keepdims=True))
            ws_col = _col_from_row(jnp.exp(dl - m_new), eye)
            gl = jnp.exp(b_last + m_prev - m_new)
            kw = k * ws_col
            c_s[st] = gl * cmat + lax.dot_general(kw.astype(bf16), vb, _TN, preferred_element_type=f32)
            n_s[st] = gl * n_row + jnp.sum(kw, axis=0, keepdims=True)
            m_s[st] = m_new

    @pl.when(c == pl.num_programs(1) - 1)
    def _():
        for r in range(nb):
            c_out[r] = c_s[r * heads:(r + 1) * heads]
            n_out[r] = n_s[r * heads:(r + 1) * heads]
            m_out[r] = m_s[r * heads:(r + 1) * heads]


_PROMPT_SEQS_PER_STEP = 4


def _mlstm_prompt(qkvo, gates_t, bias_col, g_ml, *, batch, seq):
    heads, dk, dv, cl = ML_HEADS, ML_DK, ML_DV, math.gcd(seq, CHUNK)
    nc = seq // cl
    wq = heads * dk
    nb = math.gcd(batch, _PROMPT_SEQS_PER_STEP)
    body = functools.partial(_mlstm_prompt_body, heads=heads, dk=dk, dv=dv, cl=cl, nb=nb)
    qkvo_specs = [pl.BlockSpec((cl, wq), lambda b, c, r=r, part=part: ((b * nb + r) * nc + c, part))
                  for part in range(4) for r in range(nb)]
    return pl.pallas_call(
        body, grid=(batch // nb, nc),
        in_specs=qkvo_specs + [pl.BlockSpec((nb, 1, 2 * heads, cl), lambda b, c: (b, c, 0, 0)),
                               pl.BlockSpec((2 * heads, 1), lambda b, c: (0, 0)),
                               pl.BlockSpec((heads, dv), lambda b, c: (0, 0))],
        out_specs=[pl.BlockSpec((nb, cl, heads * dv), lambda b, c: (b, c, 0)),
                   pl.BlockSpec((nb, heads, dk, dv), lambda b, c: (b, 0, 0, 0)),
                   pl.BlockSpec((nb, heads, 1, dk), lambda b, c: (b, 0, 0, 0)),
                   pl.BlockSpec((nb, heads, 1, 1), lambda b, c: (b, 0, 0, 0))],
        out_shape=[jax.ShapeDtypeStruct((batch, seq, heads * dv), bf16),
                   jax.ShapeDtypeStruct((batch, heads, dk, dv), f32),
                   jax.ShapeDtypeStruct((batch, heads, 1, dk), f32),
                   jax.ShapeDtypeStruct((batch, heads, 1, 1), f32)],
        scratch_shapes=[pltpu.VMEM((nb * heads, dk, dv), f32), pltpu.VMEM((nb * heads, 1, dk), f32),
                        pltpu.VMEM((nb * heads, 1, 1), f32)],
        compiler_params=_params("arbitrary", "arbitrary"), name="mlstm_prompt",
    )(*([qkvo] * (4 * nb)), gates_t, bias_col, g_ml)


_SAMPLE_BLOCK = 8


def _mlstm_sample_body(*refs, heads, dk, dv, aliased):
    if aliased:
        refs = refs[:10] + refs[11:]
    (q_ref, k_ref, v_ref, o_ref, gd_ref, bias_ref, gml_ref, c0_ref, n0_ref, m0_ref,
     h_ref, c_out, n_out, m_out) = refs
    bt = q_ref.shape[0]

    @pl.when(pl.program_id(0) > 0)
    def _():
        c_out[...] = jnp.zeros_like(c_out)

    @pl.when(pl.program_id(0) == 0)
    def _():
        g8 = gd_ref[:, 0:2 * heads] + bias_ref[...]
        lane = lax.broadcasted_iota(jnp.int32, (bt, heads), 1)
        m_new = jnp.zeros((bt, heads), f32)
        for hh in range(heads):
            ig = g8[:, hh:hh + 1]
            lf = jax.nn.log_sigmoid(g8[:, heads + hh:heads + hh + 1])
            m0 = m0_ref[:, hh:hh + 1]
            m_t = jnp.maximum(lf + m0, ig)
            w = jnp.exp(ig - m_t)
            g = jnp.exp(lf + m0 - m_t)
            q = q_ref[:, hh * dk:(hh + 1) * dk]
            k = k_ref[:, hh * dk:(hh + 1) * dk] * (dk ** -0.5)
            v = v_ref[:, hh * dv:(hh + 1) * dv]
            o = o_ref[:, hh * dv:(hh + 1) * dv]
            n0 = n0_ref[:, hh, :]
            kw = k * w
            q_t = q.T
            kw_t = kw.T
            qc_rows = []
            for b in range(bt):
                cmat = c0_ref[b, hh]
                qc_rows.append(jnp.sum(q_t[:, b:b + 1] * cmat, axis=0, keepdims=True))
                c_out[b, hh] = g[b:b + 1, :] * cmat + kw_t[:, b:b + 1] * v[b:b + 1, :]
            qc = jnp.concatenate(qc_rows, axis=0)
            s = jnp.sum(q * k, axis=1, keepdims=True) * w
            num = s * v + g * qc
            den = s + g * jnp.sum(q * n0, axis=1, keepdims=True)
            hraw = num / jnp.maximum(jnp.abs(den), jnp.exp(-m_t))
            hn = _rms(hraw, gml_ref[hh:hh + 1, :])
            h_ref[:, hh * dv:(hh + 1) * dv] = jax.nn.sigmoid(o) * hn
            n_out[:, hh, :] = g * n0 + kw
            m_new = jnp.where(lane == hh, m_t, m_new)
        m_out[...] = m_new


def _stacked_state_grid(layer, b, stack):
    n_l = DEPTH if stack is None else 1
    row = lambda l, i: jnp.where(l == 0, i, b - 1)
    out_layer = lambda l: (layer + l) % DEPTH
    return (n_l, b), row, out_layer


def _mlstm_sample(qkvo, gd, bias_row, g_ml, c0, n0, m0, *, layer, row0, c_stack=None):
    heads, dk, dv = ML_HEADS, ML_DK, ML_DV
    b = c0.shape[1]
    bt = _SAMPLE_BLOCK
    wq = heads * dk
    blk0 = row0 // bt
    grid, row, out_layer = _stacked_state_grid(layer, b // bt, c_stack)
    aliased = c_stack is not None
    body = functools.partial(_mlstm_sample_body, heads=heads, dk=dk, dv=dv, aliased=aliased)
    in_specs = [pl.BlockSpec((bt, wq), lambda l, i, c=c: (blk0 + row(l, i), c)) for c in range(4)]
    in_specs += [pl.BlockSpec((bt, gd.shape[-1]), lambda l, i: (blk0 + row(l, i), 0)),
                 pl.BlockSpec((1, 2 * heads), lambda l, i: (0, 0)),
                 pl.BlockSpec((heads, dv), lambda l, i: (0, 0)),
                 pl.BlockSpec((None, bt, heads, dk, dv), lambda l, i: (layer, row(l, i), 0, 0, 0)),
                 pl.BlockSpec((None, bt, heads, dk), lambda l, i: (layer, row(l, i), 0, 0)),
                 pl.BlockSpec((None, bt, heads), lambda l, i: (layer, row(l, i), 0))]
    args = [qkvo, qkvo, qkvo, qkvo, gd, bias_row, g_ml, c0, n0, m0]
    if aliased:
        in_specs.append(pl.BlockSpec(memory_space=pl.ANY))
        args.append(c_stack)
    return pl.pallas_call(
        body, grid=grid, in_specs=in_specs,
        out_specs=[pl.BlockSpec((bt, heads * dv), lambda l, i: (row(l, i), 0)),
                   pl.BlockSpec((None, bt, heads, dk, dv), lambda l, i: (out_layer(l), i, 0, 0, 0)),
                   pl.BlockSpec((bt, heads, dk), lambda l, i: (row(l, i), 0, 0)),
                   pl.BlockSpec((bt, heads), lambda l, i: (row(l, i), 0))],
        out_shape=[jax.ShapeDtypeStruct((b, heads * dv), f32),
                   jax.ShapeDtypeStruct((DEPTH, b, heads, dk, dv), f32),
                   jax.ShapeDtypeStruct((b, heads, dk), f32),
                   jax.ShapeDtypeStruct((b, heads), f32)],
        input_output_aliases={10: 1} if aliased else {},
        compiler_params=_params("arbitrary", "arbitrary"), name="mlstm_sample",
    )(*args)


def _s5_advance(x, l1, l2):
    return l1 * x + l2 * pltpu.roll(x, x.shape[-1] // 2, axis=1)


def _toeplitz_operator(krow):
    ch, tc = krow.shape
    lane = lax.broadcasted_iota(jnp.int32, krow.shape, 1)
    blocks = [krow] + [jnp.where(lane >= ch * s, pltpu.roll(krow, ch * s, axis=1), 0.0)
                       for s in range(1, tc // ch)]
    return jnp.concatenate(blocks, axis=0).astype(bf16)


def _s5_body(u_ref, m_ref, w_ref, v_ref, l1_ref, l2_ref, x0_ref, y_ref, x_out, xs_s, *, nc, batch):
    u = u_ref[0]
    xin = jnp.dot(u, w_ref[0], preferred_element_type=f32)
    l1 = l1_ref[0]
    l2 = l2_ref[0]
    x = x0_ref[0]
    for k in range(nc):
        sl = slice(k * batch, (k + 1) * batch)
        xs_s[sl, :] = x
        x = _s5_advance(x, l1, l2) + xin[sl, :]
    x_out[0] = x
    y_ref[0] = (jnp.dot(u, _toeplitz_operator(m_ref[0]), preferred_element_type=f32)
                + jnp.dot(xs_s[...].astype(bf16), v_ref[0], preferred_element_type=f32))


def _s5_scan(u_g, mats, x0, *, nc, batch, layer):
    m_mat, w_pk, v_pk, l1, l2 = mats
    g, rows, tc = u_g.shape
    p2 = w_pk.shape[-1]
    blk = lambda *s: pl.BlockSpec((1,) + s, lambda i: (i,) + (0,) * len(s))
    op = lambda *s: pl.BlockSpec((None, 1) + s, lambda i: (layer, i) + (0,) * len(s))
    body = functools.partial(_s5_body, nc=nc, batch=batch)
    return pl.pallas_call(
        body, grid=(g,),
        in_specs=[blk(rows, tc), op(S5_CH, tc), op(tc, p2), op(p2, tc), op(1, p2), op(1, p2), blk(batch, p2)],
        out_specs=[blk(rows, tc), blk(batch, p2)],
        out_shape=[jax.ShapeDtypeStruct((g, rows, tc), f32), jax.ShapeDtypeStruct((g, batch, p2), f32)],
        scratch_shapes=[pltpu.VMEM((rows, p2), f32)],
        compiler_params=_params("parallel"), name=f"s5_scan_t{tc // S5_CH}",
    )(u_g, m_mat, w_pk, v_pk, l1, l2, x0)


def _s5_prompt_body(x_ref, m_ref, w_ref, v_ref, l1_ref, l2_ref, y_ref, x_out, u_s, y_s, xin_s, xs_s,
                    *, gw, batch, seq, t, ch):
    nc = seq // t
    for b in range(batch):
        for tt in range(t):
            blk = x_ref[pl.ds(b * seq + tt, nc, stride=t), :]
            for gl in range(gw):
                u_s[gl, b * nc:(b + 1) * nc, tt * ch:(tt + 1) * ch] = blk[:, gl * ch:(gl + 1) * ch]
    for gl in range(gw):
        xin_s[gl] = jnp.dot(u_s[gl].astype(bf16), w_ref[gl], preferred_element_type=f32)
    xs = [jnp.zeros((batch, xin_s.shape[-1]), f32)] * gw
    for k in range(nc):
        for gl in range(gw):
            xs_s[gl, pl.ds(k, batch, stride=nc), :] = xs[gl]
            xs[gl] = _s5_advance(xs[gl], l1_ref[gl], l2_ref[gl]) + xin_s[gl, pl.ds(k, batch, stride=nc), :]
    for gl in range(gw):
        x_out[gl] = xs[gl]
        y_s[gl] = (jnp.dot(u_s[gl].astype(bf16), _toeplitz_operator(m_ref[gl]), preferred_element_type=f32)
                   + jnp.dot(xs_s[gl].astype(bf16), v_ref[gl], preferred_element_type=f32))
    for b in range(batch):
        for tt in range(t):
            y_ref[pl.ds(b * seq + tt, nc, stride=t), :] = jnp.concatenate(
                [y_s[gl, b * nc:(b + 1) * nc, tt * ch:(tt + 1) * ch] for gl in range(gw)], axis=1)


def _s5_prompt(uzx, mats, *, batch, seq, layer):
    m_mat, w_pk, v_pk, l1, l2 = mats
    _, g, tc, p2 = w_pk.shape
    ch = S5_CH
    t = tc // ch
    gw = 128 // ch
    rows = batch * (seq // t)
    tp = batch * seq
    win = lambda *s: pl.BlockSpec((gw,) + s, lambda i: (i,) + (0,) * len(s))
    op = lambda *s: pl.BlockSpec((None, gw) + s, lambda i: (layer, i) + (0,) * len(s))
    body = functools.partial(_s5_prompt_body, gw=gw, batch=batch, seq=seq, t=t, ch=ch)
    return pl.pallas_call(
        body, grid=(g // gw,),
        in_specs=[pl.BlockSpec((tp, gw * ch), lambda i: (0, i)), op(ch, tc), op(tc, p2), op(p2, tc),
                  op(1, p2), op(1, p2)],
        out_specs=[pl.BlockSpec((tp, gw * ch), lambda i: (0, i)), win(batch, p2)],
        out_shape=[jax.ShapeDtypeStruct((tp, g * ch), f32), jax.ShapeDtypeStruct((g, batch, p2), f32)],
        scratch_shapes=[pltpu.VMEM((gw, rows, tc), f32), pltpu.VMEM((gw, rows, tc), f32),
                        pltpu.VMEM((gw, rows, p2), f32), pltpu.VMEM((gw, rows, p2), f32)],
        compiler_params=_params("parallel"), name="s5_prompt",
    )(uzx, m_mat, w_pk, v_pk, l1, l2)


def _s5_matrices(lam_re, lam_im, log_dt, b_re, b_im, c_re, c_im, t):
    g, p = lam_re.shape
    ch = b_re.shape[-1]
    dt = jnp.exp(log_dt)[:, None]
    ar, ai = lam_re * dt, lam_im * dt

    def powers(tau):
        mag = jnp.exp(ar[:, None, :] * tau[None, :, None])
        ang = ai[:, None, :] * tau[None, :, None]
        return mag * jnp.cos(ang), mag * jnp.sin(ang)

    lbr, lbi = jnp.exp(ar) * jnp.cos(ai), jnp.exp(ar) * jnp.sin(ai)
    den = lam_re * lam_re + lam_im * lam_im
    fr = ((lbr - 1.0) * lam_re + lbi * lam_im) / den
    fi = (lbi * lam_re - (lbr - 1.0) * lam_im) / den
    bbr = jnp.swapaxes(fr[..., None] * b_re - fi[..., None] * b_im, 1, 2)
    bbi = jnp.swapaxes(fr[..., None] * b_im + fi[..., None] * b_re, 1, 2)
    cbr = c_re[:, :, None, :] * bbr[:, None, :, :] - c_im[:, :, None, :] * bbi[:, None, :, :]
    cbi = c_re[:, :, None, :] * bbi[:, None, :, :] + c_im[:, :, None, :] * bbr[:, None, :, :]
    steps = jnp.arange(t, dtype=f32)
    lr, li = powers(steps)
    krow = jnp.einsum("gcdp,gtp->gdtc", jnp.concatenate([cbr, -cbi], axis=-1),
                      jnp.concatenate([lr, li], axis=-1), precision=lax.Precision.HIGH).reshape(g, ch, t * ch)
    pr, pi = powers(t - 1.0 - steps)
    w_re = (pr[:, :, None, :] * bbr[:, None, :, :] - pi[:, :, None, :] * bbi[:, None, :, :]).reshape(g, t * ch, p)
    w_im = (pr[:, :, None, :] * bbi[:, None, :, :] + pi[:, :, None, :] * bbr[:, None, :, :]).reshape(g, t * ch, p)
    qr, qi = powers(steps + 1.0)
    qr, qi = jnp.swapaxes(qr, 1, 2)[..., None], jnp.swapaxes(qi, 1, 2)[..., None]
    ctr, cti = jnp.swapaxes(c_re, 1, 2)[:, :, None, :], jnp.swapaxes(c_im, 1, 2)[:, :, None, :]
    v_re = (ctr * qr - cti * qi).reshape(g, p, t * ch)
    v_im = -(ctr * qi + cti * qr).reshape(g, p, t * ch)
    ltr, lti = powers(jnp.full((1,), float(t), f32))
    w_pk = jnp.concatenate([w_re, w_im], axis=-1).astype(bf16)
    v_pk = jnp.concatenate([v_re, v_im], axis=1).astype(bf16)
    chunk_ops = (krow, w_pk, v_pk, jnp.concatenate([ltr, ltr], axis=-1), jnp.concatenate([-lti, lti], axis=-1))
    lb_r, lb_i = lbr[:, None, :], lbi[:, None, :]
    step_ops = (krow[:, :, :ch], w_pk[:, (t - 1) * ch:, :], v_pk[:, :, :ch],
                jnp.concatenate([lb_r, lb_r], axis=-1), jnp.concatenate([-lb_i, lb_i], axis=-1))
    return chunk_ops, step_ops


def _s5_glu_body(yp_ref, ys_ref, u_ref, d_ref, w_ref, b_ref, g_ref, o_ref, *, n_keep):
    last = pl.num_programs(0) - 1

    def run(y_raw):
        y5 = jax.nn.gelu(y_raw + d_ref[...] * u_ref[...])
        gate = jax.nn.sigmoid(jnp.dot(y5.astype(bf16), w_ref[...].astype(bf16), preferred_element_type=f32)
                              + b_ref[...])
        o_ref[...] = _rms(y5 * gate, g_ref[...]).astype(o_ref.dtype)

    @pl.when(pl.program_id(0) < last)
    def _():
        run(yp_ref[...])

    @pl.when(pl.program_id(0) == last)
    def _():
        run(_last_tile(yp_ref, ys_ref[...], n_keep))


def _s5_glu(y_prompt, y_sample, uzx, d_skip, w_glu, b_glu, g_s5, *, layer, tm):
    tp, wdt = y_prompt.shape
    bs = y_sample.shape[0]
    n_tiles, n_keep = _split_rows(tp, bs, tm)
    row = lambda i: (i, 0)
    fix = lambda i: (0, 0)
    return pl.pallas_call(
        functools.partial(_s5_glu_body, n_keep=n_keep), grid=(n_tiles,),
        in_specs=[pl.BlockSpec((tm, wdt), row), pl.BlockSpec((bs, wdt), fix), pl.BlockSpec((tm, wdt), row),
                  pl.BlockSpec((1, wdt), fix), pl.BlockSpec((None, wdt, wdt), lambda i: (layer, 0, 0)),
                  pl.BlockSpec((1, wdt), fix), pl.BlockSpec((1, wdt), fix)],
        out_specs=pl.BlockSpec((tm, wdt), row), out_shape=jax.ShapeDtypeStruct((tp + bs, wdt), bf16),
        compiler_params=_params("parallel"), name="s5_glu",
    )(y_prompt, y_sample, uzx, d_skip, w_glu, b_glu, g_s5)


def _ssd_prompt_body(xbc_ref, z_ref, gd_ref, dtt_ref, cw_ref, cb_ref, dtb_row, dtb_col, alog_row, alog_col,
                     dskip_ref, gssd_ref, y_ref, s_out, s_s, xp_s, ys_s, *, heads, hd, ns, groups, cl, width):
    c = pl.program_id(1)

    @pl.when(c == 0)
    def _():
        s_s[...] = jnp.zeros_like(s_s)
        xp_s[0:8, :] = jnp.zeros((8, xp_s.shape[1]), f32)

    xp_s[8:8 + cl, :] = xbc_ref[...]
    xc = cb_ref[...] + sum(cw_ref[j:j + 1, :] * xp_s[5 + j:5 + j + cl, :] for j in range(SSD_CONV))
    xp_s[0:8, :] = xp_s[cl:cl + 8, :]
    xc = xc * jax.nn.sigmoid(xc)
    row = lax.broadcasted_iota(jnp.int32, (cl, cl), 0)
    col = lax.broadcasted_iota(jnp.int32, (cl, cl), 1)
    tril = (col <= row).astype(f32)
    triu = (row <= col).astype(f32)
    row2 = lax.broadcasted_iota(jnp.int32, (cl, 2 * cl), 0)
    lane2 = lax.broadcasted_iota(jnp.int32, (cl, 2 * cl), 1)
    left = lane2 < cl
    causal2 = jnp.where(left, lane2, lane2 - cl) <= row2
    left_row = left[0:1, :]
    top = lax.broadcasted_iota(jnp.int32, (2 * hd, 1), 0) < hd
    dt_col = jax.nn.softplus(gd_ref[:, 8:8 + heads] + dtb_row[...])
    dt_row = jax.nn.softplus(dtt_ref[0, 0] + dtb_col[...])
    cum_col = jnp.dot(tril, dt_col * -jnp.exp(alog_row[...]), precision=HI, preferred_element_type=f32)
    cum_row = jnp.dot(dt_row * -jnp.exp(alog_col[...]), triu, precision=HI, preferred_element_type=f32)
    exp_col = jnp.exp(cum_col)
    pick = lambda cols, h0: jnp.where(left, cols[:, h0:h0 + 1], cols[:, h0 + 1:h0 + 2])
    rep = heads // groups
    for gi in range(groups):
        bm = xc[:, width + gi * ns:width + (gi + 1) * ns].astype(bf16)
        cm = xc[:, width + (groups + gi) * ns:width + (groups + gi + 1) * ns].astype(bf16)
        scores = lax.dot_general(cm, bm, _NT, preferred_element_type=f32)
        scores2 = jnp.concatenate([scores, scores], axis=1)
        for h0 in range(gi * rep, (gi + 1) * rep, 2):
            lo, hi = h0 * hd, (h0 + 2) * hd
            cc2 = pick(cum_col, h0)
            cr2 = jnp.concatenate([cum_row[h0:h0 + 1, :], cum_row[h0 + 1:h0 + 2, :]], axis=1)
            seg2 = jnp.exp(jnp.where(causal2, cc2 - cr2, -jnp.inf))
            x2 = xc[:, lo:hi]
            xdt2 = x2 * pick(dt_col, h0)
            xbd = jnp.concatenate([jnp.where(left, xdt2, 0.0), jnp.where(left, 0.0, xdt2)], axis=0)
            smat2 = s_s[lo:hi, :]
            y2 = (jnp.dot((scores2 * seg2).astype(bf16), xbd.astype(bf16), preferred_element_type=f32)
                  + pick(exp_col, h0) * lax.dot_general(cm, smat2.astype(bf16), _NT, preferred_element_type=f32))
            last0 = cum_row[h0:h0 + 1, cl - 1:cl]
            last1 = cum_row[h0 + 1:h0 + 2, cl - 1:cl]
            xw2 = (xdt2 * jnp.exp(jnp.where(left_row, last0, last1) - cc2)).astype(bf16)
            s_s[lo:hi, :] = (jnp.where(top, jnp.exp(last0), jnp.exp(last1)) * smat2
                             + lax.dot_general(xw2, bm, _TN, preferred_element_type=f32))
            dsk2 = jnp.where(left_row, dskip_ref[:, h0:h0 + 1], dskip_ref[:, h0 + 1:h0 + 2])
            ys_s[:, lo:hi] = y2 + dsk2 * x2
    z = z_ref[...]
    y_ref[...] = _rms(ys_s[...] * (z * jax.nn.sigmoid(z)), gssd_ref[...]).astype(y_ref.dtype)

    @pl.when(c == pl.num_programs(1) - 1)
    def _():
        s_out[0] = s_s[...]


def _ssd_prompt(uzx, gd, dt_t, conv_w, conv_b, dt_bias, a_log, d_skip, g_ssd, *, batch, seq):
    heads, hd, ns, groups, width = SSD_HEADS, SSD_HEAD_DIM, SSD_STATE, SSD_GROUPS, SSD_WIDTH
    cl = math.gcd(seq, CHUNK)
    assert cl == hd and (heads // groups) % 2 == 0
    nc = seq // cl
    cch = SSD_CONV_CH
    rows = lambda b, c: (b * nc + c, 0)
    fix = lambda b, c: (0, 0)
    body = functools.partial(_ssd_prompt_body, heads=heads, hd=hd, ns=ns, groups=groups, cl=cl, width=width)
    return pl.pallas_call(
        body, grid=(batch, nc),
        in_specs=[pl.BlockSpec((cl, cch), lambda b, c: (b * nc + c, 1)),
                  pl.BlockSpec((cl, width), lambda b, c: (b * nc + c, 1)),
                  pl.BlockSpec((cl, gd.shape[1]), rows),
                  pl.BlockSpec((1, 1, heads, cl), lambda b, c: (b, c, 0, 0)),
                  pl.BlockSpec((SSD_CONV, cch), fix), pl.BlockSpec((1, cch), fix),
                  pl.BlockSpec((1, heads), fix), pl.BlockSpec((heads, 1), fix),
                  pl.BlockSpec((1, heads), fix), pl.BlockSpec((heads, 1), fix),
                  pl.BlockSpec((1, heads), fix), pl.BlockSpec((1, width), fix)],
        out_specs=[pl.BlockSpec((cl, width), rows),
                   pl.BlockSpec((1, heads * hd, ns), lambda b, c: (b, 0, 0))],
        out_shape=[jax.ShapeDtypeStruct((batch * seq, width), bf16),
                   jax.ShapeDtypeStruct((batch, heads * hd, ns), f32)],
        scratch_shapes=[pltpu.VMEM((heads * hd, ns), f32), pltpu.VMEM((cl + 8, cch), f32),
                        pltpu.VMEM((cl, width), f32)],
        compiler_params=_params("arbitrary", "arbitrary"), name="ssd_prompt",
    )(uzx, uzx, gd, dt_t, conv_w, conv_b.reshape(1, cch), dt_bias.reshape(1, heads), dt_bias.reshape(heads, 1),
      a_log.reshape(1, heads), a_log.reshape(heads, 1), d_skip.reshape(1, heads), g_ssd.reshape(1, width))


def _ssd_sample_body(*refs, heads, hd, ns, groups, width, aliased):
    if aliased:
        refs = refs[:10] + refs[11:]
    (x_ref, conv0_ref, gd_ref, cw_ref, cb_ref, dtb_ref, alog_ref, dskip_ref, gssd_ref, s0_ref,
     y_ref, s_out, ys_s) = refs
    bt = x_ref.shape[0]

    @pl.when(pl.program_id(0) > 0)
    def _():
        s_out[...] = jnp.zeros_like(s_out)

    @pl.when(pl.program_id(0) == 0)
    def _():
        z = x_ref[:, width:2 * width]
        xc = cb_ref[...] + cw_ref[SSD_CONV - 1:SSD_CONV, :] * x_ref[:, 2 * width:]
        for j in range(SSD_CONV - 1):
            xc = xc + cw_ref[j:j + 1, :] * conv0_ref[:, j, :]
        xc = xc * jax.nn.sigmoid(xc)
        dt = jax.nn.softplus(gd_ref[:, 8:8 + heads] + dtb_ref[...])
        ea = jnp.exp(dt * -jnp.exp(alog_ref[...]))
        rep = heads // groups
        for gi in range(groups):
            bm = xc[:, width + gi * ns:width + (gi + 1) * ns]
            cm = xc[:, width + (groups + gi) * ns:width + (groups + gi + 1) * ns]
            cb_dot = jnp.sum(cm * bm, axis=1, keepdims=True)
            cmb = cm.astype(bf16)
            for hh in range(gi * rep, (gi + 1) * rep):
                xh = xc[:, hh * hd:(hh + 1) * hd]
                xdt = xh * dt[:, hh:hh + 1]
                eah = ea[:, hh:hh + 1]
                xdt_t = xdt.T
                sc_rows = []
                for b in range(bt):
                    smat = s0_ref[b, hh]
                    sc_rows.append(lax.dot_general(cmb, smat.astype(bf16), _NT,
                                                   preferred_element_type=f32)[b:b + 1, :])
                    s_out[b, hh] = eah[b:b + 1, :] * smat + xdt_t[:, b:b + 1] * bm[b:b + 1, :]
                sc = jnp.concatenate(sc_rows, axis=0)
                ys_s[:, hh * hd:(hh + 1) * hd] = cb_dot * xdt + eah * sc + dskip_ref[:, hh:hh + 1] * xh
        y_ref[...] = _rms(ys_s[...] * (z * jax.nn.sigmoid(z)), gssd_ref[...])


def _ssd_sample(uzx, conv0, gd, conv_w, conv_b, dt_bias, a_log, d_skip, g_ssd, s0, *, layer, row0, s_stack=None):
    heads, hd, ns, groups, width = SSD_HEADS, SSD_HEAD_DIM, SSD_STATE, SSD_GROUPS, SSD_WIDTH
    cch = SSD_CONV_CH
    b = s0.shape[1]
    bt = _SAMPLE_BLOCK
    blk0 = row0 // bt
    fix = lambda l, i: (0, 0)
    grid, row, out_layer = _stacked_state_grid(layer, b // bt, s_stack)
    aliased = s_stack is not None
    body = functools.partial(_ssd_sample_body, heads=heads, hd=hd, ns=ns, groups=groups, width=width,
                             aliased=aliased)
    in_specs = [pl.BlockSpec((bt, uzx.shape[-1]), lambda l, i: (blk0 + row(l, i), 0)),
                pl.BlockSpec((None, bt, SSD_CONV - 1, cch), lambda l, i: (layer, row(l, i), 0, 0)),
                pl.BlockSpec((bt, gd.shape[-1]), lambda l, i: (blk0 + row(l, i), 0)),
                pl.BlockSpec((SSD_CONV, cch), fix), pl.BlockSpec((1, cch), fix),
                pl.BlockSpec((1, heads), fix), pl.BlockSpec((1, heads), fix), pl.BlockSpec((1, heads), fix),
                pl.BlockSpec((1, width), fix),
                pl.BlockSpec((None, bt, heads, hd, ns), lambda l, i: (layer, row(l, i), 0, 0, 0))]
    args = [uzx, conv0, gd, conv_w, conv_b.reshape(1, cch), dt_bias.reshape(1, heads), a_log.reshape(1, heads),
            d_skip.reshape(1, heads), g_ssd.reshape(1, width), s0]
    if aliased:
        in_specs.append(pl.BlockSpec(memory_space=pl.ANY))
        args.append(s_stack)
    return pl.pallas_call(
        body, grid=grid, in_specs=in_specs,
        out_specs=[pl.BlockSpec((bt, width), lambda l, i: (row(l, i), 0)),
                   pl.BlockSpec((None, bt, heads, hd, ns), lambda l, i: (out_layer(l), i, 0, 0, 0))],
        out_shape=[jax.ShapeDtypeStruct((b, width), f32),
                   jax.ShapeDtypeStruct((DEPTH, b, heads, hd, ns), f32)],
        scratch_shapes=[pltpu.VMEM((bt, width), f32)],
        input_output_aliases={10: 1} if aliased else {},
        compiler_params=_params("arbitrary", "arbitrary"), name="ssd_sample",
    )(*args)


def _router_body(h_ref, g_ref, wr_ref, br_ref, cf_ref, idx_ref, gate_ref, *, n_exp):
    cf = _rms(h_ref[...], g_ref[...])
    cf_ref[...] = cf
    logits = jnp.dot(cf, wr_ref[...], precision=HI, preferred_element_type=f32) + br_ref[...]
    lane = lax.broadcasted_iota(jnp.int32, logits.shape, 1)
    m1 = jnp.max(logits, axis=1, keepdims=True)
    i1 = jnp.min(jnp.where(logits == m1, lane, n_exp), axis=1, keepdims=True)
    rest = jnp.where(lane == i1, -jnp.inf, logits)
    m2 = jnp.max(rest, axis=1, keepdims=True)
    i2 = jnp.min(jnp.where(rest == m2, lane, n_exp), axis=1, keepdims=True)
    e2 = jnp.exp(m2 - m1)
    g1 = 1.0 / (1.0 + e2)
    two = lax.broadcasted_iota(jnp.int32, (logits.shape[0], TOP_K), 1)
    idx_ref[...] = jnp.where(two == 0, i1, i2)
    gate_ref[...] = jnp.where(two == 0, g1, e2 * g1)


def _router(h, g_ffn, w_router, b_router, *, tm):
    m, d = h.shape
    e = w_router.shape[-1]
    row = lambda i: (i, 0)
    fix = lambda i: (0, 0)
    return pl.pallas_call(
        functools.partial(_router_body, n_exp=e), grid=(m // tm,),
        in_specs=[pl.BlockSpec((tm, d), row), pl.BlockSpec((1, d), fix), pl.BlockSpec((d, e), fix),
                  pl.BlockSpec((1, e), fix)],
        out_specs=[pl.BlockSpec((tm, d), row), pl.BlockSpec((tm, TOP_K), row),
                   pl.BlockSpec((tm, TOP_K), row)],
        out_shape=[jax.ShapeDtypeStruct((m, d), f32),
                   jax.ShapeDtypeStruct((m, TOP_K), jnp.int32), jax.ShapeDtypeStruct((m, TOP_K), f32)],
        compiler_params=_params("parallel"), name="router",
    )(h, g_ffn.reshape(1, d), w_router, b_router.reshape(1, e))


def _row_copy(src_hbm, dst, sem, src_row, dst_row):
    return pltpu.make_async_copy(src_hbm.at[pl.ds(src_row, 1)], dst.at[pl.ds(dst_row, 1)], sem)


_DMA_UNROLL = 8


def _gather_body(idx_ref, x_hbm, o_ref, buf, sem, *, tg):
    i = pl.program_id(0)

    def issue(tile, slot):
        base = tile * tg

        def start(grp, carry):
            for u in range(_DMA_UNROLL):
                r = grp * _DMA_UNROLL + u
                _row_copy(x_hbm, buf.at[slot], sem.at[slot], idx_ref[base + r], r).start(priority=u % 2)
            return carry

        lax.fori_loop(0, tg // _DMA_UNROLL, start, 0)

    @pl.when(i == 0)
    def _():
        issue(0, 0)

    @pl.when(i + 1 < pl.num_programs(0))
    def _():
        issue(i + 1, (i + 1) % 2)

    slot = i % 2
    pltpu.make_async_copy(x_hbm.at[pl.ds(0, tg)], buf.at[slot], sem.at[slot]).wait()
    o_ref[...] = buf[slot].astype(o_ref.dtype)


def _gather_rows(x, row_idx, *, tg, out_dtype):
    r = row_idx.shape[0]
    d = x.shape[1]
    return pl.pallas_call(
        functools.partial(_gather_body, tg=tg),
        grid_spec=pltpu.PrefetchScalarGridSpec(
            num_scalar_prefetch=1, grid=(r // tg,),
            in_specs=[pl.BlockSpec(memory_space=pl.ANY)],
            out_specs=pl.BlockSpec((tg, d), lambda i, idx: (i, 0)),
            scratch_shapes=[pltpu.VMEM((2, tg, d), x.dtype), pltpu.SemaphoreType.DMA((2,))]),
        out_shape=jax.ShapeDtypeStruct((r, d), out_dtype),
        compiler_params=_params("arbitrary"), name="gather_rows")(row_idx, x)


def _gmm_body(te_ref, first_ref, next_ref, nv_ref, a_ref, *rest, n_w, swiglu, n_sub):
    w_hbm = rest[:n_w]
    o_ref, wf_ref, wb_ref, sem, run_ref = rest[n_w:]
    j = pl.program_id(0)
    tn = o_ref.shape[1]
    tm = o_ref.shape[0] // n_sub

    def tile_copies(expert, col_tile, slot):
        col = pl.multiple_of(col_tile * tn, 128)
        return [pltpu.make_async_copy(w_hbm[i].at[expert, :, pl.ds(col, tn)], wf_ref.at[slot, i], sem.at[slot, i])
                for i in range(n_w)]

    @pl.when((j == 0) & (pl.program_id(1) == 0))
    def _():
        run_ref[0] = 0
        for c in tile_copies(te_ref[0], 0, 0):
            c.start()

    for sub in range(n_sub):
        t = pl.program_id(1) * n_sub + sub
        rows = slice(sub * tm, (sub + 1) * tm)

        @pl.when((t < nv_ref[0]) & (first_ref[t] == 1))
        def _():
            slot = run_ref[0] % 2
            for c in tile_copies(te_ref[t], j, slot):
                c.wait()
            for i in range(n_w):
                wb_ref[i] = wf_ref[slot, i].astype(bf16)
            nxt = next_ref[t]

            @pl.when(nxt >= 0)
            def _():
                for c in tile_copies(te_ref[nxt], j, 1 - slot):
                    c.start()

            @pl.when((nxt < 0) & (j + 1 < pl.num_programs(0)))
            def _():
                for c in tile_copies(te_ref[0], j + 1, 1 - slot):
                    c.start()

            run_ref[0] = run_ref[0] + 1

        @pl.when(t < nv_ref[0])
        def _():
            a = a_ref[rows, :]
            if swiglu:
                g = jnp.dot(a, wb_ref[0], preferred_element_type=f32)
                u = jnp.dot(a, wb_ref[1], preferred_element_type=f32)
                o_ref[rows, :] = (g * jax.nn.sigmoid(g) * u).astype(o_ref.dtype)
            else:
                o_ref[rows, :] = jnp.dot(a, wb_ref[0], preferred_element_type=f32).astype(o_ref.dtype)

        @pl.when(t >= nv_ref[0])
        def _():
            o_ref[rows, :] = jnp.zeros((tm, tn), o_ref.dtype)


def _gmm(a_sorted, weights, tables, *, tm, tn, n_sub, swiglu, out_dtype, name):
    tile_expert, tile_first, tile_next, n_valid = tables
    r, k = a_sorted.shape
    n = weights[0].shape[-1]
    n_w = len(weights)
    assert (r // tm) % n_sub == 0
    return pl.pallas_call(
        functools.partial(_gmm_body, n_w=n_w, swiglu=swiglu, n_sub=n_sub),
        grid_spec=pltpu.PrefetchScalarGridSpec(
            num_scalar_prefetch=4, grid=(n // tn, r // (tm * n_sub)),
            in_specs=([pl.BlockSpec((tm * n_sub, k), lambda j, t, *_: (t, 0))]
                      + [pl.BlockSpec(memory_space=pl.ANY)] * n_w),
            out_specs=pl.BlockSpec((tm * n_sub, tn), lambda j, t, *_: (t, j)),
            scratch_shapes=[pltpu.VMEM((2, n_w, k, tn), f32), pltpu.VMEM((n_w, k, tn), bf16),
                            pltpu.SemaphoreType.DMA((2, n_w)), pltpu.SMEM((1,), jnp.int32)]),
        out_shape=jax.ShapeDtypeStruct((r, n), out_dtype),
        compiler_params=_params("arbitrary", "arbitrary", vmem=_VMEM_LIMIT_GMM), name=name,
    )(tile_expert, tile_first, tile_next, n_valid, a_sorted, *weights)


def _combine_body(pos_ref, y_hbm, gate_ref, r_ref, gn_ref, o_ref, e_ref, buf, sem, *, tc):
    i = pl.program_id(0)

    def issue(tile, slot):
        base = tile * tc

        def start(r, carry):
            for kk in range(TOP_K):
                _row_copy(y_hbm, buf.at[slot, kk], sem.at[slot], pos_ref[(base + r) * TOP_K + kk], r).start()
            return carry

        lax.fori_loop(0, tc, start, 0, unroll=_DMA_UNROLL)

    @pl.when(i == 0)
    def _():
        issue(0, 0)

    @pl.when(i + 1 < pl.num_programs(0))
    def _():
        issue(i + 1, (i + 1) % 2)

    slot = i % 2
    for kk in range(TOP_K):
        pltpu.make_async_copy(y_hbm.at[pl.ds(0, tc)], buf.at[slot, kk], sem.at[slot]).wait()
    gate = gate_ref[...]
    out = r_ref[...] + gate[:, 0:1] * buf[slot, 0] + gate[:, 1:2] * buf[slot, 1]
    o_ref[...] = out
    e_ref[...] = _rms(out, gn_ref[...]).astype(e_ref.dtype)


def _combine(y_sorted, pos_flat, gates, resid, g_next, *, tc):
    m, d = resid.shape
    rows = lambda i, pos: (i, 0)
    return pl.pallas_call(
        functools.partial(_combine_body, tc=tc),
        grid_spec=pltpu.PrefetchScalarGridSpec(
            num_scalar_prefetch=1, grid=(m // tc,),
            in_specs=[pl.BlockSpec(memory_space=pl.ANY), pl.BlockSpec((tc, TOP_K), rows),
                      pl.BlockSpec((tc, d), rows), pl.BlockSpec((1, d), lambda i, pos: (0, 0))],
            out_specs=[pl.BlockSpec((tc, d), rows), pl.BlockSpec((tc, d), rows)],
            scratch_shapes=[pltpu.VMEM((2, TOP_K, tc, d), f32), pltpu.SemaphoreType.DMA((2,))]),
        out_shape=[jax.ShapeDtypeStruct((m, d), f32), jax.ShapeDtypeStruct((m, d), bf16)],
        compiler_params=_params("arbitrary"), name="moe_combine",
    )(pos_flat, y_sorted, gates, resid, g_next.reshape(1, d))


def _routing_tables(top_i, n_exp, tm, n_tiles):
    m = top_i.shape[0]
    e_flat = top_i.reshape(-1)
    onehot = (e_flat[:, None] == jnp.arange(n_exp, dtype=jnp.int32)[None, :]).astype(jnp.int32)
    rank = jnp.take_along_axis(jnp.cumsum(onehot, axis=0), e_flat[:, None], axis=1)[:, 0] - 1
    counts = jnp.sum(onehot, axis=0)
    tiles_per = (counts + tm - 1) // tm
    tile_end = jnp.cumsum(tiles_per)
    tile_start = tile_end - tiles_per
    pos = tile_start[e_flat] * tm + rank
    token = jnp.arange(m * TOP_K, dtype=jnp.int32) // TOP_K
    row_token = jnp.zeros((n_tiles * tm,), jnp.int32).at[pos].set(token)
    n_valid = tile_end[-1]
    tid = jnp.minimum(jnp.arange(n_tiles, dtype=jnp.int32), n_valid - 1)
    tile_expert = jnp.sum((tid[:, None] >= tile_end[None, :]).astype(jnp.int32), axis=1)
    tile_first = jnp.concatenate([jnp.ones((1,), jnp.int32),
                                  (tile_expert[1:] != tile_expert[:-1]).astype(jnp.int32)])
    ids = jnp.arange(n_tiles, dtype=jnp.int32)
    starts = jnp.where((tile_first == 1) & (ids < n_valid), ids, n_tiles)
    later = jnp.concatenate([lax.cummin(starts, reverse=True)[1:], jnp.full((1,), n_tiles, jnp.int32)])
    tile_next = jnp.where(later >= n_tiles, -1, later).astype(jnp.int32)
    return (pos.astype(jnp.int32), row_token,
            (tile_expert.astype(jnp.int32), tile_first, tile_next, n_valid.reshape(1).astype(jnp.int32)))


def _moe_ffn(h, g_ffn, w_router, b_router, wg, wu, wd, g_next, *, tm_tok, tm, tn_up, tn_down):
    m = h.shape[0]
    n_exp = wg.shape[0]
    c_f32, top_i, top_g = _router(h, g_ffn, w_router, b_router, tm=tm_tok)
    n_tiles = 2 * (((m * TOP_K) // tm + n_exp + 1) // 2)
    pos, row_token, tables = _routing_tables(top_i, n_exp, tm, n_tiles)
    x_sorted = _gather_rows(c_f32, row_token, tg=2 * tm, out_dtype=bf16)
    h_sorted = _gmm(x_sorted, (wg, wu), tables, tm=tm, tn=tn_up, n_sub=2, swiglu=True, out_dtype=bf16,
                    name="moe_gate_up")
    y_sorted = _gmm(h_sorted, (wd,), tables, tm=tm, tn=tn_down, n_sub=1, swiglu=False, out_dtype=f32,
                    name="moe_down")
    return _combine(y_sorted, pos, top_g, h, g_next, tc=tm_tok // 2)


def kernel(x_prompt, x_sample, state_mlstm_C, state_mlstm_n, state_mlstm_m, state_s5_re, state_s5_im, state_ssd, cache_conv, p_prompt, p_sample, g_mix, w_in, b_igate, b_fgate, g_ml, s5_lam_re, s5_lam_im, s5_log_dt, s5_b_re, s5_b_im, s5_c_re, s5_c_im, s5_d, s5_w_glu, s5_b_glu, g_s5, ssd_conv_w, ssd_conv_b, ssd_dt_bias, ssd_a_log, ssd_d, g_ssd, w_out, g_ffn, ffn_w_gate, ffn_w_up, ffn_w_down, w_router, b_router, moe_w_gate, moe_w_up, moe_w_down, g_ple, w_ple, w_ple_gate, g_final):
    bp, seq, d = x_prompt.shape
    bs = x_sample.shape[0]
    tp = bp * seq
    m = tp + bs
    tm = _token_tile(tp, bs)
    heads = ML_HEADS
    t5 = math.gcd(seq, S5_CHUNK)
    cl = math.gcd(seq, CHUNK)
    nc = seq // cl

    p_p = p_prompt.reshape(DEPTH, tp, -1)
    p_s = p_sample.reshape(DEPTH, bs, -1)
    w_in_t = jnp.swapaxes(w_in, 1, 2)
    outs_p = [[] for _ in range(7)]
    outs_s = [[] for _ in range(5)]
    c_stack = s_stack = None
    mats_p, mats_s = jax.vmap(functools.partial(_s5_matrices, t=t5))(
        s5_lam_re, s5_lam_im, s5_log_dt, s5_b_re, s5_b_im, s5_c_re, s5_c_im)

    for i in range(DEPTH):
        if i == 0:
            h, a = _rmsnorm_in(x_prompt.reshape(tp, d), x_sample.reshape(bs, d), g_mix[i], tm)
        else:
            a = _rmsnorm(h, g_mix[i], bf16, tm)
        qkvo = _mm_nt(a, w_in_t, tm=tm, tn=1024, layer=i, row_off=0, n_rows=_OFF_GATES, name="in_proj_qkvo")
        uzx = _mm_nt(a, w_in_t, tm=tm, tn=1024, layer=i, row_off=_OFF_U, n_rows=_OFF_DT - _OFF_U, name="in_proj_uzx")
        gd = _mm_gates(a, w_in_t, tm=tm, layer=i, off1=_OFF_GATES, n1=2 * heads, off2=_OFF_DT, n2=SSD_HEADS,
                       name="in_proj_gates")

        gates_t = jnp.transpose(gd[:tp, :2 * heads].reshape(bp, nc, cl, 2 * heads), (0, 1, 3, 2))
        bias8 = jnp.concatenate([b_igate[i], b_fgate[i]])
        h_ml_p, c_p, n_p, m_p = _mlstm_prompt(qkvo, gates_t, bias8.reshape(2 * heads, 1), g_ml[i], batch=bp, seq=seq)
        h_ml_s, c_stack, n_s, m_s = _mlstm_sample(qkvo, gd, bias8.reshape(1, 2 * heads), g_ml[i], state_mlstm_C,
                                                  state_mlstm_n, state_mlstm_m, layer=i, row0=tp, c_stack=c_stack)

        y_p, x5_p = _s5_prompt(uzx, mats_p, batch=bp, seq=seq, layer=i)
        s5re_p, s5im_p = x5_p[..., :S5_STATE], x5_p[..., S5_STATE:]
        u_s = jnp.transpose(uzx[tp:, :S5_WIDTH].reshape(bs, S5_GROUPS, S5_CH), (1, 0, 2)).astype(bf16)
        x0_s = jnp.swapaxes(jnp.concatenate([state_s5_re[i], state_s5_im[i]], axis=-1), 0, 1)
        y_s, x5_s = _s5_scan(u_s, mats_s, x0_s, nc=1, batch=bs, layer=i)
        s5re_s, s5im_s = x5_s[..., :S5_STATE], x5_s[..., S5_STATE:]
        y_s = jnp.transpose(y_s, (1, 0, 2)).reshape(bs, S5_WIDTH)
        y5 = _s5_glu(y_p, y_s, uzx, s5_d[i].reshape(1, S5_WIDTH), s5_w_glu,
                     s5_b_glu[i].reshape(1, S5_WIDTH), g_s5[i].reshape(1, S5_WIDTH), layer=i, tm=tm)

        dt_t = jnp.transpose(gd[:tp, 2 * heads:2 * heads + SSD_HEADS].reshape(bp, nc, cl, SSD_HEADS), (0, 1, 3, 2))
        y_ssd_p, ssd_p = _ssd_prompt(uzx, gd, dt_t, ssd_conv_w[i], ssd_conv_b[i], ssd_dt_bias[i], ssd_a_log[i],
                                     ssd_d[i], g_ssd[i], batch=bp, seq=seq)
        y_ssd_s, s_stack = _ssd_sample(uzx, cache_conv, gd, ssd_conv_w[i], ssd_conv_b[i], ssd_dt_bias[i],
                                       ssd_a_log[i], ssd_d[i], g_ssd[i], state_ssd, layer=i, row0=tp,
                                       s_stack=s_stack)
        conv_p = jnp.stack([uzx[(b + 1) * seq - (SSD_CONV - 1):(b + 1) * seq, S5_WIDTH + SSD_WIDTH:]
                            for b in range(bp)])
        conv_s = jnp.concatenate([cache_conv[i][:, 1:], uzx[tp:, S5_WIDTH + SSD_WIDTH:].reshape(bs, 1, -1)], axis=1)

        for lst, s in zip(outs_p, (c_p, n_p.reshape(bp, heads, ML_DK), m_p.reshape(bp, heads),
                                   jnp.swapaxes(s5re_p, 0, 1), jnp.swapaxes(s5im_p, 0, 1),
                                   ssd_p.reshape(bp, SSD_HEADS, SSD_HEAD_DIM, SSD_STATE), conv_p)):
            lst.append(s)
        for lst, s in zip(outs_s, (n_s, m_s, jnp.swapaxes(s5re_s, 0, 1), jnp.swapaxes(s5im_s, 0, 1), conv_s)):
            lst.append(s)

        h = _mm_mix(h_ml_p.reshape(tp, -1), h_ml_s, y5, y_ssd_p.reshape(tp, -1), y_ssd_s, w_out, h, tm=tm, tn=1024,
                    layer=i, name="out_proj")

        j = i // 2
        if i % 2 == 0:
            cn = _rmsnorm(h, g_ffn[i], bf16, tm)
            tm_up = 2 * tm if m % (2 * tm) == 0 else tm
            hid = _mm_swiglu(cn, ffn_w_gate, ffn_w_up, tm=tm_up, tn=512, layer=j, name="ffn_gate_up")
            h = _mm(hid, ffn_w_down, tm=tm // 2, tn=512, layer=j, resid=h, name="ffn_down")
            e = _rmsnorm(h, g_ple[i], bf16, tm)
        else:
            n_moe = moe_w_gate.shape[0]
            sel = lambda w: w.reshape((n_moe * N_EXPERTS,) + w.shape[2:])[j * N_EXPERTS:(j + 1) * N_EXPERTS] if n_moe > 1 else w.reshape(w.shape[1:])
            h, e = _moe_ffn(h, g_ffn[i], w_router[j], b_router[j], sel(moe_w_gate), sel(moe_w_up), sel(moe_w_down),
                            g_ple[i], tm_tok=tm, tm=256, tn_up=1024, tn_down=512)

        h = _mm_ple(e, p_p, p_s, w_ple_gate, w_ple, h, tm=tm, tn=1024, layer=i)

    y_p, y_s = _rmsnorm_out(h, g_final, tp, tm)
    n_s, m_s, s5re_s, s5im_s, conv_s = (jnp.stack(l) for l in outs_s)
    return ((y_p.reshape(bp, seq, d), y_s.reshape(bs, 1, d)) + tuple(jnp.stack(l) for l in outs_p)
            + (c_stack, n_s, m_s, s5re_s, s5im_s, s_stack, conv_s))
```

```python
import functools
import math

import jax
import jax.numpy as jnp
from jax import lax
from jax.experimental import pallas as pl
from jax.experimental.pallas import tpu as pltpu

f32 = jnp.float32
bf16 = jnp.bfloat16
HI = lax.Precision.HIGHEST

DEPTH = 2
ML_HEADS = 4
ML_DK = 256
ML_DV = 256
ML_WIDTH = ML_HEADS * ML_DV
S5_CH = 16
S5_WIDTH = 512
S5_GROUPS = 32
S5_STATE = 64
SSD_HEAD_DIM = 64
SSD_WIDTH = 512
SSD_HEADS = 8
SSD_GROUPS = 2
SSD_STATE = 128
SSD_CONV = 4
SSD_CONV_CH = 1024
CHUNK = 64
S5_CHUNK = 32
N_EXPERTS = 8
TOP_K = 2
RMS_EPS = 1e-6

_OFF_GATES = 4 * ML_WIDTH
_OFF_U = _OFF_GATES + 2 * ML_HEADS
_OFF_DT = _OFF_U + S5_WIDTH + SSD_WIDTH + SSD_CONV_CH

_VMEM_LIMIT = 56 * 1024 * 1024
_VMEM_LIMIT_GMM = 60 * 1024 * 1024
_NT = (((1,), (1,)), ((), ()))
_TN = (((0,), (0,)), ((), ()))


def _params(*sem, vmem=_VMEM_LIMIT):
    return pltpu.CompilerParams(dimension_semantics=sem, vmem_limit_bytes=vmem)


def _rms(x, g):
    return x * lax.rsqrt(jnp.mean(x * x, axis=-1, keepdims=True) + RMS_EPS) * g


def _rmsnorm_body(x_ref, g_ref, o_ref):
    o_ref[...] = _rms(x_ref[...], g_ref[...]).astype(o_ref.dtype)


def _last_tile(p_tile, s_rows, n_keep):
    return jnp.concatenate([p_tile[0:n_keep, :], s_rows], axis=0)


def _split_rows(tp, bs, tm):
    n_tiles = (tp + bs) // tm
    n_keep = tp - (n_tiles - 1) * tm
    assert n_tiles * tm == tp + bs and 0 < n_keep and n_keep + bs == tm and n_keep % 16 == 0
    return n_tiles, n_keep


def _token_tile(tp, bs):
    m = tp + bs
    for n_tiles in (10, 8, 5, 4, 2, 1):
        tm = m // n_tiles
        if m % n_tiles == 0 and tm % 32 == 0 and bs < tm and (tp - (n_tiles - 1) * tm) % 16 == 0:
            return tm
    raise ValueError("no row tiling for these token counts")


def _rmsnorm_in_body(xp_ref, xs_ref, g_ref, h_ref, a_ref, *, n_keep):
    last = pl.num_programs(0) - 1

    def run(x):
        h_ref[...] = x
        a_ref[...] = _rms(x, g_ref[...]).astype(a_ref.dtype)

    @pl.when(pl.program_id(0) < last)
    def _():
        run(xp_ref[...])

    @pl.when(pl.program_id(0) == last)
    def _():
        run(_last_tile(xp_ref, xs_ref[...], n_keep))


def _rmsnorm_in(xp, xs, g, tm):
    tp, d = xp.shape
    bs = xs.shape[0]
    n_tiles, n_keep = _split_rows(tp, bs, tm)
    row = lambda i: (i, 0)
    fix = lambda i: (0, 0)
    return pl.pallas_call(
        functools.partial(_rmsnorm_in_body, n_keep=n_keep), grid=(n_tiles,),
        in_specs=[pl.BlockSpec((tm, d), row), pl.BlockSpec((bs, d), fix), pl.BlockSpec((1, d), fix)],
        out_specs=[pl.BlockSpec((tm, d), row), pl.BlockSpec((tm, d), row)],
        out_shape=[jax.ShapeDtypeStruct((tp + bs, d), f32), jax.ShapeDtypeStruct((tp + bs, d), bf16)],
        compiler_params=_params("parallel"), name="rmsnorm_in")(xp, xs, g.reshape(1, d))


def _rmsnorm_out_body(x_ref, g_ref, yp_ref, ys_ref, *, n_keep):
    y = _rms(x_ref[...], g_ref[...])
    yp_ref[...] = y

    @pl.when(pl.program_id(0) == pl.num_programs(0) - 1)
    def _():
        ys_ref[...] = y[n_keep:, :]


def _rmsnorm_out(x, g, tp, tm):
    m, d = x.shape
    bs = m - tp
    n_tiles, n_keep = _split_rows(tp, bs, tm)
    return pl.pallas_call(
        functools.partial(_rmsnorm_out_body, n_keep=n_keep), grid=(n_tiles,),
        in_specs=[pl.BlockSpec((tm, d), lambda i: (i, 0)), pl.BlockSpec((1, d), lambda i: (0, 0))],
        out_specs=[pl.BlockSpec((tm, d), lambda i: (i, 0)), pl.BlockSpec((bs, d), lambda i: (0, 0))],
        out_shape=[jax.ShapeDtypeStruct((tp, d), f32), jax.ShapeDtypeStruct((bs, d), f32)],
        compiler_params=_params("arbitrary"), name="rmsnorm_out")(x, g.reshape(1, d))


def _rmsnorm(x, g, out_dtype, tm):
    m, d = x.shape
    return pl.pallas_call(
        _rmsnorm_body, grid=(m // tm,),
        in_specs=[pl.BlockSpec((tm, d), lambda i: (i, 0)), pl.BlockSpec((1, d), lambda i: (0, 0))],
        out_specs=pl.BlockSpec((tm, d), lambda i: (i, 0)),
        out_shape=jax.ShapeDtypeStruct((m, d), out_dtype),
        compiler_params=_params("parallel"), name="rmsnorm")(x, g.reshape(1, d))


def _cast_weight_once(w_ref, wb_ref):
    @pl.when(pl.program_id(1) == 0)
    def _():
        wb_ref[...] = w_ref[...].astype(bf16)


def _mm_plain_body(a_ref, w_ref, o_ref, wb_ref):
    _cast_weight_once(w_ref, wb_ref)
    o_ref[...] = jnp.dot(a_ref[...], wb_ref[...], preferred_element_type=f32).astype(o_ref.dtype)


def _mm_resid_body(a_ref, w_ref, r_ref, o_ref, wb_ref):
    _cast_weight_once(w_ref, wb_ref)
    o_ref[...] = r_ref[...] + jnp.dot(a_ref[...], wb_ref[...], preferred_element_type=f32)


def _mm_swiglu_body(a_ref, wg_ref, wu_ref, o_ref, wgb_ref, wub_ref):
    _cast_weight_once(wg_ref, wgb_ref)
    _cast_weight_once(wu_ref, wub_ref)
    a = a_ref[...]
    g = jnp.dot(a, wgb_ref[...], preferred_element_type=f32)
    u = jnp.dot(a, wub_ref[...], preferred_element_type=f32)
    o_ref[...] = (g * jax.nn.sigmoid(g) * u).astype(o_ref.dtype)


def _mm_ple_body(e_ref, pp_ref, ps_ref, wg_ref, wp_ref, r_ref, o_ref, wgb_ref, wpb_ref, *, n_keep):
    _cast_weight_once(wg_ref, wgb_ref)
    _cast_weight_once(wp_ref, wpb_ref)
    last = pl.num_programs(1) - 1

    def run(p):
        gate = jnp.dot(e_ref[...], wgb_ref[...], preferred_element_type=f32)
        emb = jnp.dot(p.astype(bf16), wpb_ref[...], preferred_element_type=f32)
        o_ref[...] = r_ref[...] + emb * jax.nn.sigmoid(gate)

    @pl.when(pl.program_id(1) < last)
    def _():
        run(pp_ref[...])

    @pl.when(pl.program_id(1) == last)
    def _():
        run(_last_tile(pp_ref, ps_ref[...], n_keep))


def _mm_mix_body(a1p_ref, a1s_ref, a2_ref, a3p_ref, a3s_ref, w_ref, r_ref, o_ref, wb_ref, *, n_keep):
    _cast_weight_once(w_ref, wb_ref)
    last = pl.num_programs(1) - 1
    k1 = a1p_ref.shape[1]
    k2 = a2_ref.shape[1]

    def run(a1, a3):
        acc = jnp.dot(a1, wb_ref[0:k1, :], preferred_element_type=f32)
        acc += jnp.dot(a2_ref[...], wb_ref[k1:k1 + k2, :], preferred_element_type=f32)
        acc += jnp.dot(a3, wb_ref[k1 + k2:, :], preferred_element_type=f32)
        o_ref[...] = r_ref[...] + acc

    @pl.when(pl.program_id(1) < last)
    def _():
        run(a1p_ref[...], a3p_ref[...])

    @pl.when(pl.program_id(1) == last)
    def _():
        run(_last_tile(a1p_ref, a1s_ref[...].astype(bf16), n_keep),
            _last_tile(a3p_ref, a3s_ref[...].astype(bf16), n_keep))


def _mm_mix(a1p, a1s, a2, a3p, a3s, w, resid, *, tm, tn, layer, name):
    m, n = resid.shape
    tp, k1 = a1p.shape
    bs = a1s.shape[0]
    k2, k3 = a2.shape[1], a3p.shape[1]
    n_tiles, n_keep = _split_rows(tp, bs, tm)
    rows = lambda j, i: (i, 0)
    fix = lambda j, i: (0, 0)
    o_spec = pl.BlockSpec((tm, tn), lambda j, i: (i, j))
    return pl.pallas_call(
        functools.partial(_mm_mix_body, n_keep=n_keep), grid=(n // tn, n_tiles),
        in_specs=[pl.BlockSpec((tm, k1), rows), pl.BlockSpec((bs, k1), fix), pl.BlockSpec((tm, k2), rows),
                  pl.BlockSpec((tm, k3), rows), pl.BlockSpec((bs, k3), fix),
                  _wspec(w, layer, k1 + k2 + k3, tn), o_spec],
        out_specs=o_spec, out_shape=jax.ShapeDtypeStruct((m, n), f32),
        scratch_shapes=[pltpu.VMEM((k1 + k2 + k3, tn), bf16)],
        compiler_params=_params("arbitrary", "arbitrary"), name=name)(a1p, a1s, a2, a3p, a3s, w, resid)


def _wspec(w, layer, k, tn, col_block_off=0):
    if w.ndim == 2:
        return pl.BlockSpec((k, tn), lambda j, i: (0, j + col_block_off))
    return pl.BlockSpec((None, k, tn), lambda j, i: (layer, 0, j + col_block_off))


def _mm(a, w, *, tm, tn, layer=0, n_cols=None, col_off=0, resid=None, out_dtype=f32, name="mm"):
    m, k = a.shape
    n = n_cols if n_cols is not None else w.shape[-1]
    grid = (n // tn, m // tm)
    a_spec = pl.BlockSpec((tm, k), lambda j, i: (i, 0))
    o_spec = pl.BlockSpec((tm, tn), lambda j, i: (i, j))
    w_spec = _wspec(w, layer, k, tn, col_off // tn)
    scratch = [pltpu.VMEM((k, tn), bf16)]
    if resid is None:
        body, ins, specs = _mm_plain_body, (a, w), [a_spec, w_spec]
    else:
        body, ins, specs = _mm_resid_body, (a, w, resid), [a_spec, w_spec, o_spec]
    return pl.pallas_call(
        body, grid=grid, in_specs=specs, out_specs=o_spec,
        out_shape=jax.ShapeDtypeStruct((m, n), out_dtype), scratch_shapes=scratch,
        compiler_params=_params("arbitrary", "arbitrary"), name=name)(*ins)


def _mm_nt_body(a_ref, wt_ref, o_ref, wb_ref):
    @pl.when(pl.program_id(1) == 0)
    def _():
        wb_ref[...] = wt_ref[0].T.astype(bf16)

    o_ref[...] = jnp.dot(a_ref[...], wb_ref[...], preferred_element_type=f32)


def _wt_rows(layer, k, n_rows, row_of):
    return pl.BlockSpec((pl.Element(1), pl.Element(n_rows), pl.Element(k)),
                        lambda *idx: (layer, row_of(*idx), 0))


def _mm_nt(a, wt, *, tm, tn, layer, row_off, n_rows, name):
    m, k = a.shape
    return pl.pallas_call(
        _mm_nt_body, grid=(n_rows // tn, m // tm),
        in_specs=[pl.BlockSpec((tm, k), lambda j, i: (i, 0)),
                  _wt_rows(layer, k, tn, lambda j, i: pl.multiple_of(row_off + tn * j, 8))],
        out_specs=pl.BlockSpec((tm, tn), lambda j, i: (i, j)),
        out_shape=jax.ShapeDtypeStruct((m, n_rows), f32), scratch_shapes=[pltpu.VMEM((k, tn), bf16)],
        compiler_params=_params("arbitrary", "arbitrary"), name=name)(a, wt)


def _mm_nt_gates_body(a_ref, wt_ref, w1_ref, w2_ref, o_ref, g_ref, wb_ref):
    _mm_nt_body(a_ref, wt_ref, o_ref, wb_ref)
    _mm_gates_body(a_ref, w1_ref, w2_ref, g_ref)


def _mm_nt_gates(a, wt, *, tm, tn, layer, row_off, n_rows, off1, n1, off2, n2, name):
    m, k = a.shape
    return pl.pallas_call(
        _mm_nt_gates_body, grid=(n_rows // tn, m // tm),
        in_specs=[pl.BlockSpec((tm, k), lambda j, i: (i, 0)),
                  _wt_rows(layer, k, tn, lambda j, i: pl.multiple_of(row_off + tn * j, 8)),
                  _wt_rows(layer, k, n1, lambda j, i: off1), _wt_rows(layer, k, n2, lambda j, i: off2)],
        out_specs=[pl.BlockSpec((tm, tn), lambda j, i: (i, j)),
                   pl.BlockSpec((None, tm, n1 + n2), lambda j, i: (j, i, 0))],
        out_shape=[jax.ShapeDtypeStruct((m, n_rows), f32),
                   jax.ShapeDtypeStruct((n_rows // tn, m, n1 + n2), f32)],
        scratch_shapes=[pltpu.VMEM((k, tn), bf16)],
        compiler_params=_params("arbitrary", "arbitrary"), name=name)(a, wt, wt, wt)


def _mm_gates_body(a_ref, w1_ref, w2_ref, o_ref):
    a = a_ref[...]
    n1 = w1_ref.shape[1]
    o_ref[:, 0:n1] = lax.dot_general(a, w1_ref[0].astype(bf16), _NT, preferred_element_type=f32)
    o_ref[:, n1:] = lax.dot_general(a, w2_ref[0].astype(bf16), _NT, preferred_element_type=f32)


def _mm_gates(a, wt, *, tm, layer, off1, n1, off2, n2, name):
    m, k = a.shape
    return pl.pallas_call(
        _mm_gates_body, grid=(m // tm,),
        in_specs=[pl.BlockSpec((tm, k), lambda i: (i, 0)),
                  _wt_rows(layer, k, n1, lambda i: off1), _wt_rows(layer, k, n2, lambda i: off2)],
        out_specs=pl.BlockSpec((tm, n1 + n2), lambda i: (i, 0)),
        out_shape=jax.ShapeDtypeStruct((m, n1 + n2), f32),
        compiler_params=_params("parallel"), name=name)(a, wt, wt)


def _mm_swiglu(a, wg, wu, *, tm, tn, layer=0, name="mm_swiglu"):
    m, k = a.shape
    n = wg.shape[-1]
    a_spec = pl.BlockSpec((tm, k), lambda j, i: (i, 0))
    o_spec = pl.BlockSpec((tm, tn), lambda j, i: (i, j))
    return pl.pallas_call(
        _mm_swiglu_body, grid=(n // tn, m // tm),
        in_specs=[a_spec, _wspec(wg, layer, k, tn), _wspec(wu, layer, k, tn)], out_specs=o_spec,
        out_shape=jax.ShapeDtypeStruct((m, n), bf16),
        scratch_shapes=[pltpu.VMEM((k, tn), bf16), pltpu.VMEM((k, tn), bf16)],
        compiler_params=_params("arbitrary", "arbitrary"), name=name)(a, wg, wu)


def _mm_ple(e, p_prompt, p_sample, w_gate, w_ple, resid, *, tm, tn, layer, name="mm_ple"):
    m, k = e.shape
    _, tp, kp = p_prompt.shape
    bs = p_sample.shape[1]
    n = w_gate.shape[-1]
    n_tiles, n_keep = _split_rows(tp, bs, tm)
    o_spec = pl.BlockSpec((tm, tn), lambda j, i: (i, j))
    return pl.pallas_call(
        functools.partial(_mm_ple_body, n_keep=n_keep), grid=(n // tn, n_tiles),
        in_specs=[pl.BlockSpec((tm, k), lambda j, i: (i, 0)),
                  pl.BlockSpec((None, tm, kp), lambda j, i: (layer, i, 0)),
                  pl.BlockSpec((None, bs, kp), lambda j, i: (layer, 0, 0)),
                  _wspec(w_gate, layer, k, tn), _wspec(w_ple, layer, kp, tn), o_spec],
        out_specs=o_spec, out_shape=jax.ShapeDtypeStruct((m, n), f32),
        scratch_shapes=[pltpu.VMEM((k, tn), bf16), pltpu.VMEM((kp, tn), bf16)],
        compiler_params=_params("arbitrary", "arbitrary"), name=name)(e, p_prompt, p_sample, w_gate, w_ple, resid)


def _col_from_row(row, eye):
    n = row.shape[1]
    return jnp.sum(jnp.where(eye, jnp.broadcast_to(row, (n, n)), 0.0), axis=1, keepdims=True)


def _mlstm_prompt_body(*refs, heads, dk, dv, cl, nb):
    q_refs, k_refs, v_refs, o_refs = (refs[i * nb:(i + 1) * nb] for i in range(4))
    gt_ref, bias_ref, gml_ref, h_ref, c_out, n_out, m_out, c_s, n_s, m_s = refs[4 * nb:]
    c = pl.program_id(1)

    @pl.when(c == 0)
    def _():
        c_s[...] = jnp.zeros_like(c_s)
        n_s[...] = jnp.zeros_like(n_s)
        m_s[...] = jnp.zeros_like(m_s)

    row = lax.broadcasted_iota(jnp.int32, (cl, cl), 0)
    col = lax.broadcasted_iota(jnp.int32, (cl, cl), 1)
    causal = col <= row
    eye = col == row
    triu = (row <= col).astype(f32)
    for r in range(nb):
        gt = gt_ref[r, 0] + bias_ref[...]
        lf = jax.nn.log_sigmoid(gt[heads:2 * heads])
        b_all = jnp.dot(lf, triu, precision=HI, preferred_element_type=f32)
        for hh in range(heads):
            st = r * heads + hh
            ig = gt[hh:hh + 1]
            b_row = b_all[hh:hh + 1]
            b_col = _col_from_row(b_row, eye)
            m_prev = m_s[st]
            d = jnp.where(causal, b_col - b_row + ig, -jnp.inf)
            inter = b_col + m_prev
            m_t = jnp.maximum(inter, jnp.max(d, axis=1, keepdims=True))
            w = jnp.exp(d - m_t)
            g = jnp.exp(inter - m_t)
            q = q_refs[r][:, hh * dk:(hh + 1) * dk]
            k = k_refs[r][:, hh * dk:(hh + 1) * dk] * (dk ** -0.5)
            vb = v_refs[r][:, hh * dv:(hh + 1) * dv].astype(bf16)
            qb = q.astype(bf16)
            cmat = c_s[st]
            n_row = n_s[st]
            s = lax.dot_general(qb, k.astype(bf16), _NT, preferred_element_type=f32) * w
            num = (jnp.dot(s.astype(bf16), vb, preferred_element_type=f32)
                   + g * jnp.dot(qb, cmat.astype(bf16), preferred_element_type=f32))
            den = jnp.sum(s, axis=1, keepdims=True) + g * jnp.sum(q * n_row, axis=1, keepdims=True)
            hraw = num / jnp.maximum(jnp.abs(den), jnp.exp(-m_t))
            hn = _rms(hraw, gml_ref[hh:hh + 1, :])
            ogate = jax.nn.sigmoid(o_refs[r][:, hh * dv:(hh + 1) * dv])
            h_ref[r, :, hh * dv:(hh + 1) * dv] = (ogate * hn).astype(h_ref.dtype)
            b_last = b_row[:, cl - 1:cl]
            dl = b_last - b_row + ig
            m_new = jnp.maximum(b_last + m_prev, jnp.max(dl, axis=1, keepdims=True))
            ws_col = _col_from_row(jnp.exp(dl - m_new), eye)
            gl = jnp.exp(b_last + m_prev - m_new)
            kw = k * ws_col
            c_s[st] = gl * cmat + lax.dot_general(kw.astype(bf16), vb, _TN, preferred_element_type=f32)
            n_s[st] = gl * n_row + jnp.sum(kw, axis=0, keepdims=True)
            m_s[st] = m_new

    @pl.when(c == pl.num_programs(1) - 1)
    def _():
        for r in range(nb):
            c_out[r] = c_s[r * heads:(r + 1) * heads]
            n_out[r] = n_s[r * heads:(r + 1) * heads]
            m_out[r] = m_s[r * heads:(r + 1) * heads]


_PROMPT_SEQS_PER_STEP = 4


def _mlstm_prompt(qkvo, gates_t, bias_col, g_ml, *, batch, seq):
    heads, dk, dv, cl = ML_HEADS, ML_DK, ML_DV, math.gcd(seq, CHUNK)
    nc = seq // cl
    wq = heads * dk
    nb = math.gcd(batch, _PROMPT_SEQS_PER_STEP)
    body = functools.partial(_mlstm_prompt_body, heads=heads, dk=dk, dv=dv, cl=cl, nb=nb)
    qkvo_specs = [pl.BlockSpec((cl, wq), lambda b, c, r=r, part=part: ((b * nb + r) * nc + c, part))
                  for part in range(4) for r in range(nb)]
    return pl.pallas_call(
        body, grid=(batch // nb, nc),
        in_specs=qkvo_specs + [pl.BlockSpec((nb, 1, 2 * heads, cl), lambda b, c: (b, c, 0, 0)),
                               pl.BlockSpec((2 * heads, 1), lambda b, c: (0, 0)),
                               pl.BlockSpec((heads, dv), lambda b, c: (0, 0))],
        out_specs=[pl.BlockSpec((nb, cl, heads * dv), lambda b, c: (b, c, 0)),
                   pl.BlockSpec((nb, heads, dk, dv), lambda b, c: (b, 0, 0, 0)),
                   pl.BlockSpec((nb, heads, 1, dk), lambda b, c: (b, 0, 0, 0)),
                   pl.BlockSpec((nb, heads, 1, 1), lambda b, c: (b, 0, 0, 0))],
        out_shape=[jax.ShapeDtypeStruct((batch, seq, heads * dv), bf16),
                   jax.ShapeDtypeStruct((batch, heads, dk, dv), f32),
                   jax.ShapeDtypeStruct((batch, heads, 1, dk), f32),
                   jax.ShapeDtypeStruct((batch, heads, 1, 1), f32)],
        scratch_shapes=[pltpu.VMEM((nb * heads, dk, dv), f32), pltpu.VMEM((nb * heads, 1, dk), f32),
                        pltpu.VMEM((nb * heads, 1, 1), f32)],
        compiler_params=_params("arbitrary", "arbitrary"), name="mlstm_prompt",
    )(*([qkvo] * (4 * nb)), gates_t, bias_col, g_ml)


_SAMPLE_BLOCK = 8


def _mlstm_sample_body(*refs, heads, dk, dv, aliased):
    if aliased:
        refs = refs[:10] + refs[11:]
    (q_ref, k_ref, v_ref, o_ref, gd_ref, bias_ref, gml_ref, c0_ref, n0_ref, m0_ref,
     h_ref, c_out, n_out, m_out) = refs
    bt = q_ref.shape[0]

    @pl.when(pl.program_id(0) > 0)
    def _():
        c_out[...] = jnp.zeros_like(c_out)

    @pl.when(pl.program_id(0) == 0)
    def _():
        g8 = gd_ref[:, 0:2 * heads] + bias_ref[...]
        lane = lax.broadcasted_iota(jnp.int32, (bt, heads), 1)
        m_new = jnp.zeros((bt, heads), f32)
        for hh in range(heads):
            ig = g8[:, hh:hh + 1]
            lf = jax.nn.log_sigmoid(g8[:, heads + hh:heads + hh + 1])
            m0 = m0_ref[:, hh:hh + 1]
            m_t = jnp.maximum(lf + m0, ig)
            w = jnp.exp(ig - m_t)
            g = jnp.exp(lf + m0 - m_t)
            q = q_ref[:, hh * dk:(hh + 1) * dk]
            k = k_ref[:, hh * dk:(hh + 1) * dk] * (dk ** -0.5)
            v = v_ref[:, hh * dv:(hh + 1) * dv]
            o = o_ref[:, hh * dv:(hh + 1) * dv]
            n0 = n0_ref[:, hh, :]
            kw = k * w
            q_t = q.T
            kw_t = kw.T
            qc_rows = []
            for b in range(bt):
                cmat = c0_ref[b, hh]
                qc_rows.append(jnp.sum(q_t[:, b:b + 1] * cmat, axis=0, keepdims=True))
                c_out[b, hh] = g[b:b + 1, :] * cmat + kw_t[:, b:b + 1] * v[b:b + 1, :]
            qc = jnp.concatenate(qc_rows, axis=0)
            s = jnp.sum(q * k, axis=1, keepdims=True) * w
            num = s * v + g * qc
            den = s + g * jnp.sum(q * n0, axis=1, keepdims=True)
            hraw = num / jnp.maximum(jnp.abs(den), jnp.exp(-m_t))
            hn = _rms(hraw, gml_ref[hh:hh + 1, :])
            h_ref[:, hh * dv:(hh + 1) * dv] = jax.nn.sigmoid(o) * hn
            n_out[:, hh, :] = g * n0 + kw
            m_new = jnp.where(lane == hh, m_t, m_new)
        m_out[...] = m_new


def _stacked_state_grid(layer, b, stack):
    n_l = DEPTH if stack is None else 1
    row = lambda l, i: jnp.where(l == 0, i, b - 1)
    out_layer = lambda l: (layer + l) % DEPTH
    return (n_l, b), row, out_layer


def _mlstm_sample(qkvo, gd, bias_row, g_ml, c0, n0, m0, *, layer, row0, c_stack=None):
    heads, dk, dv = ML_HEADS, ML_DK, ML_DV
    b = c0.shape[1]
    bt = _SAMPLE_BLOCK
    wq = heads * dk
    blk0 = row0 // bt
    grid, row, out_layer = _stacked_state_grid(layer, b // bt, c_stack)
    aliased = c_stack is not None
    body = functools.partial(_mlstm_sample_body, heads=heads, dk=dk, dv=dv, aliased=aliased)
    in_specs = [pl.BlockSpec((bt, wq), lambda l, i, c=c: (blk0 + row(l, i), c)) for c in range(4)]
    in_specs += [pl.BlockSpec((bt, gd.shape[-1]), lambda l, i: (blk0 + row(l, i), 0)),
                 pl.BlockSpec((1, 2 * heads), lambda l, i: (0, 0)),
                 pl.BlockSpec((heads, dv), lambda l, i: (0, 0)),
                 pl.BlockSpec((None, bt, heads, dk, dv), lambda l, i: (layer, row(l, i), 0, 0, 0)),
                 pl.BlockSpec((None, bt, heads, dk), lambda l, i: (layer, row(l, i), 0, 0)),
                 pl.BlockSpec((None, bt, heads), lambda l, i: (layer, row(l, i), 0))]
    args = [qkvo, qkvo, qkvo, qkvo, gd, bias_row, g_ml, c0, n0, m0]
    if aliased:
        in_specs.append(pl.BlockSpec(memory_space=pl.ANY))
        args.append(c_stack)
    return pl.pallas_call(
        body, grid=grid, in_specs=in_specs,
        out_specs=[pl.BlockSpec((bt, heads * dv), lambda l, i: (row(l, i), 0)),
                   pl.BlockSpec((None, bt, heads, dk, dv), lambda l, i: (out_layer(l), i, 0, 0, 0)),
                   pl.BlockSpec((bt, heads, dk), lambda l, i: (row(l, i), 0, 0)),
                   pl.BlockSpec((bt, heads), lambda l, i: (row(l, i), 0))],
        out_shape=[jax.ShapeDtypeStruct((b, heads * dv), f32),
                   jax.ShapeDtypeStruct((DEPTH, b, heads, dk, dv), f32),
                   jax.ShapeDtypeStruct((b, heads, dk), f32),
                   jax.ShapeDtypeStruct((b, heads), f32)],
        input_output_aliases={10: 1} if aliased else {},
        compiler_params=_params("arbitrary", "arbitrary"), name="mlstm_sample",
    )(*args)


def _s5_advance(x, l1, l2):
    return l1 * x + l2 * pltpu.roll(x, x.shape[-1] // 2, axis=1)


def _toeplitz_operator(krow):
    ch, tc = krow.shape
    lane = lax.broadcasted_iota(jnp.int32, krow.shape, 1)
    blocks = [krow] + [jnp.where(lane >= ch * s, pltpu.roll(krow, ch * s, axis=1), 0.0)
                       for s in range(1, tc // ch)]
    return jnp.concatenate(blocks, axis=0).astype(bf16)


def _s5_body(u_ref, m_ref, w_ref, v_ref, l1_ref, l2_ref, x0_ref, y_ref, x_out, xs_s, *, nc, batch):
    u = u_ref[0]
    xin = jnp.dot(u, w_ref[0], preferred_element_type=f32)
    l1 = l1_ref[0]
    l2 = l2_ref[0]
    x = x0_ref[0]
    for k in range(nc):
        sl = slice(k * batch, (k + 1) * batch)
        xs_s[sl, :] = x
        x = _s5_advance(x, l1, l2) + xin[sl, :]
    x_out[0] = x
    y_ref[0] = (jnp.dot(u, _toeplitz_operator(m_ref[0]), preferred_element_type=f32)
                + jnp.dot(xs_s[...].astype(bf16), v_ref[0], preferred_element_type=f32))


def _s5_scan(u_g, mats, x0, *, nc, batch, layer):
    m_mat, w_pk, v_pk, l1, l2 = mats
    g, rows, tc = u_g.shape
    p2 = w_pk.shape[-1]
    blk = lambda *s: pl.BlockSpec((1,) + s, lambda i: (i,) + (0,) * len(s))
    op = lambda *s: pl.BlockSpec((None, 1) + s, lambda i: (layer, i) + (0,) * len(s))
    body = functools.partial(_s5_body, nc=nc, batch=batch)
    return pl.pallas_call(
        body, grid=(g,),
        in_specs=[blk(rows, tc), op(S5_CH, tc), op(tc, p2), op(p2, tc), op(1, p2), op(1, p2), blk(batch, p2)],
        out_specs=[blk(rows, tc), blk(batch, p2)],
        out_shape=[jax.ShapeDtypeStruct((g, rows, tc), f32), jax.ShapeDtypeStruct((g, batch, p2), f32)],
        scratch_shapes=[pltpu.VMEM((rows, p2), f32)],
        compiler_params=_params("parallel"), name=f"s5_scan_t{tc // S5_CH}",
    )(u_g, m_mat, w_pk, v_pk, l1, l2, x0)


def _s5_prompt_body(x_ref, m_ref, w_ref, v_ref, l1_ref, l2_ref, y_ref, x_out, u_s, y_s, xin_s, xs_s,
                    *, gw, batch, seq, t, ch):
    nc = seq // t
    for b in range(batch):
        for tt in range(t):
            blk = x_ref[pl.ds(b * seq + tt, nc, stride=t), :]
            for gl in range(gw):
                u_s[gl, b * nc:(b + 1) * nc, tt * ch:(tt + 1) * ch] = blk[:, gl * ch:(gl + 1) * ch]
    for gl in range(gw):
        xin_s[gl] = jnp.dot(u_s[gl].astype(bf16), w_ref[gl], preferred_element_type=f32)
    xs = [jnp.zeros((batch, xin_s.shape[-1]), f32)] * gw
    for k in range(nc):
        for gl in range(gw):
            xs_s[gl, pl.ds(k, batch, stride=nc), :] = xs[gl]
            xs[gl] = _s5_advance(xs[gl], l1_ref[gl], l2_ref[gl]) + xin_s[gl, pl.ds(k, batch, stride=nc), :]
    for gl in range(gw):
        x_out[gl] = xs[gl]
        y_s[gl] = (jnp.dot(u_s[gl].astype(bf16), _toeplitz_operator(m_ref[gl]), preferred_element_type=f32)
                   + jnp.dot(xs_s[gl].astype(bf16), v_ref[gl], preferred_element_type=f32))
    for b in range(batch):
        for tt in range(t):
            y_ref[pl.ds(b * seq + tt, nc, stride=t), :] = jnp.concatenate(
                [y_s[gl, b * nc:(b + 1) * nc, tt * ch:(tt + 1) * ch] for gl in range(gw)], axis=1)


def _s5_prompt(uzx, mats, *, batch, seq, layer):
    m_mat, w_pk, v_pk, l1, l2 = mats
    _, g, tc, p2 = w_pk.shape
    ch = S5_CH
    t = tc // ch
    gw = 128 // ch
    rows = batch * (seq // t)
    tp = batch * seq
    win = lambda *s: pl.BlockSpec((gw,) + s, lambda i: (i,) + (0,) * len(s))
    op = lambda *s: pl.BlockSpec((None, gw) + s, lambda i: (layer, i) + (0,) * len(s))
    body = functools.partial(_s5_prompt_body, gw=gw, batch=batch, seq=seq, t=t, ch=ch)
    return pl.pallas_call(
        body, grid=(g // gw,),
        in_specs=[pl.BlockSpec((tp, gw * ch), lambda i: (0, i)), op(ch, tc), op(tc, p2), op(p2, tc),
                  op(1, p2), op(1, p2)],
        out_specs=[pl.BlockSpec((tp, gw * ch), lambda i: (0, i)), win(batch, p2)],
        out_shape=[jax.ShapeDtypeStruct((tp, g * ch), f32), jax.ShapeDtypeStruct((g, batch, p2), f32)],
        scratch_shapes=[pltpu.VMEM((gw, rows, tc), f32), pltpu.VMEM((gw, rows, tc), f32),
                        pltpu.VMEM((gw, rows, p2), f32), pltpu.VMEM((gw, rows, p2), f32)],
        compiler_params=_params("parallel"), name="s5_prompt",
    )(uzx, m_mat, w_pk, v_pk, l1, l2)


def _s5_matrices(lam_re, lam_im, log_dt, b_re, b_im, c_re, c_im, t):
    g, p = lam_re.shape
    ch = b_re.shape[-1]
    dt = jnp.exp(log_dt)[:, None]
    ar, ai = lam_re * dt, lam_im * dt

    def powers(tau):
        mag = jnp.exp(ar[:, None, :] * tau[None, :, None])
        ang = ai[:, None, :] * tau[None, :, None]
        return mag * jnp.cos(ang), mag * jnp.sin(ang)

    lbr, lbi = jnp.exp(ar) * jnp.cos(ai), jnp.exp(ar) * jnp.sin(ai)
    den = lam_re * lam_re + lam_im * lam_im
    fr = ((lbr - 1.0) * lam_re + lbi * lam_im) / den
    fi = (lbi * lam_re - (lbr - 1.0) * lam_im) / den
    bbr = jnp.swapaxes(fr[..., None] * b_re - fi[..., None] * b_im, 1, 2)
    bbi = jnp.swapaxes(fr[..., None] * b_im + fi[..., None] * b_re, 1, 2)
    cbr = c_re[:, :, None, :] * bbr[:, None, :, :] - c_im[:, :, None, :] * bbi[:, None, :, :]
    cbi = c_re[:, :, None, :] * bbi[:, None, :, :] + c_im[:, :, None, :] * bbr[:, None, :, :]
    steps = jnp.arange(t, dtype=f32)
    lr, li = powers(steps)
    krow = jnp.einsum("gcdp,gtp->gdtc", jnp.concatenate([cbr, -cbi], axis=-1),
                      jnp.concatenate([lr, li], axis=-1), precision=lax.Precision.HIGH).reshape(g, ch, t * ch)
    pr, pi = powers(t - 1.0 - steps)
    w_re = (pr[:, :, None, :] * bbr[:, None, :, :] - pi[:, :, None, :] * bbi[:, None, :, :]).reshape(g, t * ch, p)
    w_im = (pr[:, :, None, :] * bbi[:, None, :, :] + pi[:, :, None, :] * bbr[:, None, :, :]).reshape(g, t * ch, p)
    qr, qi = powers(steps + 1.0)
    qr, qi = jnp.swapaxes(qr, 1, 2)[..., None], jnp.swapaxes(qi, 1, 2)[..., None]
    ctr, cti = jnp.swapaxes(c_re, 1, 2)[:, :, None, :], jnp.swapaxes(c_im, 1, 2)[:, :, None, :]
    v_re = (ctr * qr - cti * qi).reshape(g, p, t * ch)
    v_im = -(ctr * qi + cti * qr).reshape(g, p, t * ch)
    ltr, lti = powers(jnp.full((1,), float(t), f32))
    w_pk = jnp.concatenate([w_re, w_im], axis=-1).astype(bf16)
    v_pk = jnp.concatenate([v_re, v_im], axis=1).astype(bf16)
    chunk_ops = (krow, w_pk, v_pk, jnp.concatenate([ltr, ltr], axis=-1), jnp.concatenate([-lti, lti], axis=-1))
    lb_r, lb_i = lbr[:, None, :], lbi[:, None, :]
    step_ops = (krow[:, :, :ch], w_pk[:, (t - 1) * ch:, :], v_pk[:, :, :ch],
                jnp.concatenate([lb_r, lb_r], axis=-1), jnp.concatenate([-lb_i, lb_i], axis=-1))
    return chunk_ops, step_ops


def _s5_glu_body(yp_ref, ys_ref, u_ref, d_ref, w_ref, b_ref, g_ref, o_ref, *, n_keep):
    last = pl.num_programs(0) - 1

    def run(y_raw):
        y5 = jax.nn.gelu(y_raw + d_ref[...] * u_ref[...])
        gate = jax.nn.sigmoid(jnp.dot(y5.astype(bf16), w_ref[...].astype(bf16), preferred_element_type=f32)
                              + b_ref[...])
        o_ref[...] = _rms(y5 * gate, g_ref[...]).astype(o_ref.dtype)

    @pl.when(pl.program_id(0) < last)
    def _():
        run(yp_ref[...])

    @pl.when(pl.program_id(0) == last)
    def _():
        run(_last_tile(yp_ref, ys_ref[...], n_keep))


def _s5_glu(y_prompt, y_sample, uzx, d_skip, w_glu, b_glu, g_s5, *, layer, tm):
    tp, wdt = y_prompt.shape
    bs = y_sample.shape[0]
    n_tiles, n_keep = _split_rows(tp, bs, tm)
    row = lambda i: (i, 0)
    fix = lambda i: (0, 0)
    return pl.pallas_call(
        functools.partial(_s5_glu_body, n_keep=n_keep), grid=(n_tiles,),
        in_specs=[pl.BlockSpec((tm, wdt), row), pl.BlockSpec((bs, wdt), fix), pl.BlockSpec((tm, wdt), row),
                  pl.BlockSpec((1, wdt), fix), pl.BlockSpec((None, wdt, wdt), lambda i: (layer, 0, 0)),
                  pl.BlockSpec((1, wdt), fix), pl.BlockSpec((1, wdt), fix)],
        out_specs=pl.BlockSpec((tm, wdt), row), out_shape=jax.ShapeDtypeStruct((tp + bs, wdt), bf16),
        compiler_params=_params("parallel"), name="s5_glu",
    )(y_prompt, y_sample, uzx, d_skip, w_glu, b_glu, g_s5)


def _ssd_prompt_body(xbc_ref, z_ref, gd_ref, dtt_ref, cw_ref, cb_ref, dtb_row, dtb_col, alog_row, alog_col,
                     dskip_ref, gssd_ref, y_ref, s_out, s_s, xp_s, ys_s, *, heads, hd, ns, groups, cl, width):
    c = pl.program_id(1)

    @pl.when(c == 0)
    def _():
        s_s[...] = jnp.zeros_like(s_s)
        xp_s[0:8, :] = jnp.zeros((8, xp_s.shape[1]), f32)

    xp_s[8:8 + cl, :] = xbc_ref[...]
    xc = cb_ref[...] + sum(cw_ref[j:j + 1, :] * xp_s[5 + j:5 + j + cl, :] for j in range(SSD_CONV))
    xp_s[0:8, :] = xp_s[cl:cl + 8, :]
    xc = xc * jax.nn.sigmoid(xc)
    row = lax.broadcasted_iota(jnp.int32, (cl, cl), 0)
    col = lax.broadcasted_iota(jnp.int32, (cl, cl), 1)
    tril = (col <= row).astype(f32)
    triu = (row <= col).astype(f32)
    row2 = lax.broadcasted_iota(jnp.int32, (cl, 2 * cl), 0)
    lane2 = lax.broadcasted_iota(jnp.int32, (cl, 2 * cl), 1)
    left = lane2 < cl
    causal2 = jnp.where(left, lane2, lane2 - cl) <= row2
    left_row = left[0:1, :]
    top = lax.broadcasted_iota(jnp.int32, (2 * hd, 1), 0) < hd
    dt_col = jax.nn.softplus(gd_ref[:, 8:8 + heads] + dtb_row[...])
    dt_row = jax.nn.softplus(dtt_ref[0, 0] + dtb_col[...])
    cum_col = jnp.dot(tril, dt_col * -jnp.exp(alog_row[...]), precision=HI, preferred_element_type=f32)
    cum_row = jnp.dot(dt_row * -jnp.exp(alog_col[...]), triu, precision=HI, preferred_element_type=f32)
    exp_col = jnp.exp(cum_col)
    pick = lambda cols, h0: jnp.where(left, cols[:, h0:h0 + 1], cols[:, h0 + 1:h0 + 2])
    rep = heads // groups
    for gi in range(groups):
        bm = xc[:, width + gi * ns:width + (gi + 1) * ns].astype(bf16)
        cm = xc[:, width + (groups + gi) * ns:width + (groups + gi + 1) * ns].astype(bf16)
        scores = lax.dot_general(cm, bm, _NT, preferred_element_type=f32)
        scores2 = jnp.concatenate([scores, scores], axis=1)
        for h0 in range(gi * rep, (gi + 1) * rep, 2):
            lo, hi = h0 * hd, (h0 + 2) * hd
            cc2 = pick(cum_col, h0)
            cr2 = jnp.concatenate([cum_row[h0:h0 + 1, :], cum_row[h0 + 1:h0 + 2, :]], axis=1)
            seg2 = jnp.exp(jnp.where(causal2, cc2 - cr2, -jnp.inf))
            x2 = xc[:, lo:hi]
            xdt2 = x2 * pick(dt_col, h0)
            xbd = jnp.concatenate([jnp.where(left, xdt2, 0.0), jnp.where(left, 0.0, xdt2)], axis=0)
            smat2 = s_s[lo:hi, :]
            y2 = (jnp.dot((scores2 * seg2).astype(bf16), xbd.astype(bf16), preferred_element_type=f32)
                  + pick(exp_col, h0) * lax.dot_general(cm, smat2.astype(bf16), _NT, preferred_element_type=f32))
            last0 = cum_row[h0:h0 + 1, cl - 1:cl]
            last1 = cum_row[h0 + 1:h0 + 2, cl - 1:cl]
            xw2 = (xdt2 * jnp.exp(jnp.where(left_row, last0, last1) - cc2)).astype(bf16)
            s_s[lo:hi, :] = (jnp.where(top, jnp.exp(last0), jnp.exp(last1)) * smat2
                             + lax.dot_general(xw2, bm, _TN, preferred_element_type=f32))
            dsk2 = jnp.where(left_row, dskip_ref[:, h0:h0 + 1], dskip_ref[:, h0 + 1:h0 + 2])
            ys_s[:, lo:hi] = y2 + dsk2 * x2
    z = z_ref[...]
    y_ref[...] = _rms(ys_s[...] * (z * jax.nn.sigmoid(z)), gssd_ref[...]).astype(y_ref.dtype)

    @pl.when(c == pl.num_programs(1) - 1)
    def _():
        s_out[0] = s_s[...]


def _ssd_prompt(uzx, gd, dt_t, conv_w, conv_b, dt_bias, a_log, d_skip, g_ssd, *, batch, seq):
    heads, hd, ns, groups, width = SSD_HEADS, SSD_HEAD_DIM, SSD_STATE, SSD_GROUPS, SSD_WIDTH
    cl = math.gcd(seq, CHUNK)
    assert cl == hd and (heads // groups) % 2 == 0
    nc = seq // cl
    cch = SSD_CONV_CH
    rows = lambda b, c: (b * nc + c, 0)
    fix = lambda b, c: (0, 0)
    body = functools.partial(_ssd_prompt_body, heads=heads, hd=hd, ns=ns, groups=groups, cl=cl, width=width)
    return pl.pallas_call(
        body, grid=(batch, nc),
        in_specs=[pl.BlockSpec((cl, cch), lambda b, c: (b * nc + c, 1)),
                  pl.BlockSpec((cl, width), lambda b, c: (b * nc + c, 1)),
                  pl.BlockSpec((cl, gd.shape[1]), rows),
                  pl.BlockSpec((1, 1, heads, cl), lambda b, c: (b, c, 0, 0)),
                  pl.BlockSpec((SSD_CONV, cch), fix), pl.BlockSpec((1, cch), fix),
                  pl.BlockSpec((1, heads), fix), pl.BlockSpec((heads, 1), fix),
                  pl.BlockSpec((1, heads), fix), pl.BlockSpec((heads, 1), fix),
                  pl.BlockSpec((1, heads), fix), pl.BlockSpec((1, width), fix)],
        out_specs=[pl.BlockSpec((cl, width), rows),
                   pl.BlockSpec((1, heads * hd, ns), lambda b, c: (b, 0, 0))],
        out_shape=[jax.ShapeDtypeStruct((batch * seq, width), bf16),
                   jax.ShapeDtypeStruct((batch, heads * hd, ns), f32)],
        scratch_shapes=[pltpu.VMEM((heads * hd, ns), f32), pltpu.VMEM((cl + 8, cch), f32),
                        pltpu.VMEM((cl, width), f32)],
        compiler_params=_params("arbitrary", "arbitrary"), name="ssd_prompt",
    )(uzx, uzx, gd, dt_t, conv_w, conv_b.reshape(1, cch), dt_bias.reshape(1, heads), dt_bias.reshape(heads, 1),
      a_log.reshape(1, heads), a_log.reshape(heads, 1), d_skip.reshape(1, heads), g_ssd.reshape(1, width))


def _ssd_sample_body(*refs, heads, hd, ns, groups, width, aliased):
    if aliased:
        refs = refs[:10] + refs[11:]
    (x_ref, conv0_ref, gd_ref, cw_ref, cb_ref, dtb_ref, alog_ref, dskip_ref, gssd_ref, s0_ref,
     y_ref, s_out, ys_s) = refs
    bt = x_ref.shape[0]

    @pl.when(pl.program_id(0) > 0)
    def _():
        s_out[...] = jnp.zeros_like(s_out)

    @pl.when(pl.program_id(0) == 0)
    def _():
        z = x_ref[:, width:2 * width]
        xc = cb_ref[...] + cw_ref[SSD_CONV - 1:SSD_CONV, :] * x_ref[:, 2 * width:]
        for j in range(SSD_CONV - 1):
            xc = xc + cw_ref[j:j + 1, :] * conv0_ref[:, j, :]
        xc = xc * jax.nn.sigmoid(xc)
        dt = jax.nn.softplus(gd_ref[:, 8:8 + heads] + dtb_ref[...])
        ea = jnp.exp(dt * -jnp.exp(alog_ref[...]))
        rep = heads // groups
        for gi in range(groups):
            bm = xc[:, width + gi * ns:width + (gi + 1) * ns]
            cm = xc[:, width + (groups + gi) * ns:width + (groups + gi + 1) * ns]
            cb_dot = jnp.sum(cm * bm, axis=1, keepdims=True)
            cmb = cm.astype(bf16)
            for hh in range(gi * rep, (gi + 1) * rep):
                xh = xc[:, hh * hd:(hh + 1) * hd]
                xdt = xh * dt[:, hh:hh + 1]
                eah = ea[:, hh:hh + 1]
                xdt_t = xdt.T
                sc_rows = []
                for b in range(bt):
                    smat = s0_ref[b, hh]
                    sc_rows.append(lax.dot_general(cmb, smat.astype(bf16), _NT,
                                                   preferred_element_type=f32)[b:b + 1, :])
                    s_out[b, hh] = eah[b:b + 1, :] * smat + xdt_t[:, b:b + 1] * bm[b:b + 1, :]
                sc = jnp.concatenate(sc_rows, axis=0)
                ys_s[:, hh * hd:(hh + 1) * hd] = cb_dot * xdt + eah * sc + dskip_ref[:, hh:hh + 1] * xh
        y_ref[...] = _rms(ys_s[...] * (z * jax.nn.sigmoid(z)), gssd_ref[...])


def _ssd_sample(uzx, conv0, gd, conv_w, conv_b, dt_bias, a_log, d_skip, g_ssd, s0, *, layer, row0, s_stack=None):
    heads, hd, ns, groups, width = SSD_HEADS, SSD_HEAD_DIM, SSD_STATE, SSD_GROUPS, SSD_WIDTH
    cch = SSD_CONV_CH
    b = s0.shape[1]
    bt = _SAMPLE_BLOCK
    blk0 = row0 // bt
    fix = lambda l, i: (0, 0)
    grid, row, out_layer = _stacked_state_grid(layer, b // bt, s_stack)
    aliased = s_stack is not None
    body = functools.partial(_ssd_sample_body, heads=heads, hd=hd, ns=ns, groups=groups, width=width,
                             aliased=aliased)
    in_specs = [pl.BlockSpec((bt, uzx.shape[-1]), lambda l, i: (blk0 + row(l, i), 0)),
                pl.BlockSpec((None, bt, SSD_CONV - 1, cch), lambda l, i: (layer, row(l, i), 0, 0)),
                pl.BlockSpec((bt, gd.shape[-1]), lambda l, i: (blk0 + row(l, i), 0)),
                pl.BlockSpec((SSD_CONV, cch), fix), pl.BlockSpec((1, cch), fix),
                pl.BlockSpec((1, heads), fix), pl.BlockSpec((1, heads), fix), pl.BlockSpec((1, heads), fix),
                pl.BlockSpec((1, width), fix),
                pl.BlockSpec((None, bt, heads, hd, ns), lambda l, i: (layer, row(l, i), 0, 0, 0))]
    args = [uzx, conv0, gd, conv_w, conv_b.reshape(1, cch), dt_bias.reshape(1, heads), a_log.reshape(1, heads),
            d_skip.reshape(1, heads), g_ssd.reshape(1, width), s0]
    if aliased:
        in_specs.append(pl.BlockSpec(memory_space=pl.ANY))
        args.append(s_stack)
    return pl.pallas_call(
        body, grid=grid, in_specs=in_specs,
        out_specs=[pl.BlockSpec((bt, width), lambda l, i: (row(l, i), 0)),
                   pl.BlockSpec((None, bt, heads, hd, ns), lambda l, i: (out_layer(l), i, 0, 0, 0))],
        out_shape=[jax.ShapeDtypeStruct((b, width), f32),
                   jax.ShapeDtypeStruct((DEPTH, b, heads, hd, ns), f32)],
        scratch_shapes=[pltpu.VMEM((bt, width), f32)],
        input_output_aliases={10: 1} if aliased else {},
        compiler_params=_params("arbitrary", "arbitrary"), name="ssd_sample",
    )(*args)


def _router_body(h_ref, g_ref, wr_ref, br_ref, cf_ref, idx_ref, gate_ref, *, n_exp):
    cf = _rms(h_ref[...], g_ref[...])
    cf_ref[...] = cf
    logits = jnp.dot(cf, wr_ref[...], precision=HI, preferred_element_type=f32) + br_ref[...]
    lane = lax.broadcasted_iota(jnp.int32, logits.shape, 1)
    m1 = jnp.max(logits, axis=1, keepdims=True)
    i1 = jnp.min(jnp.where(logits == m1, lane, n_exp), axis=1, keepdims=True)
    rest = jnp.where(lane == i1, -jnp.inf, logits)
    m2 = jnp.max(rest, axis=1, keepdims=True)
    i2 = jnp.min(jnp.where(rest == m2, lane, n_exp), axis=1, keepdims=True)
    e2 = jnp.exp(m2 - m1)
    g1 = 1.0 / (1.0 + e2)
    two = lax.broadcasted_iota(jnp.int32, (logits.shape[0], TOP_K), 1)
    idx_ref[...] = jnp.where(two == 0, i1, i2)
    gate_ref[...] = jnp.where(two == 0, g1, e2 * g1)


def _router(h, g_ffn, w_router, b_router, *, tm):
    m, d = h.shape
    e = w_router.shape[-1]
    row = lambda i: (i, 0)
    fix = lambda i: (0, 0)
    return pl.pallas_call(
        functools.partial(_router_body, n_exp=e), grid=(m // tm,),
        in_specs=[pl.BlockSpec((tm, d), row), pl.BlockSpec((1, d), fix), pl.BlockSpec((d, e), fix),
                  pl.BlockSpec((1, e), fix)],
        out_specs=[pl.BlockSpec((tm, d), row), pl.BlockSpec((tm, TOP_K), row),
                   pl.BlockSpec((tm, TOP_K), row)],
        out_shape=[jax.ShapeDtypeStruct((m, d), f32),
                   jax.ShapeDtypeStruct((m, TOP_K), jnp.int32), jax.ShapeDtypeStruct((m, TOP_K), f32)],
        compiler_params=_params("parallel"), name="router",
    )(h, g_ffn.reshape(1, d), w_router, b_router.reshape(1, e))


def _row_copy(src_hbm, dst, sem, src_row, dst_row):
    return pltpu.make_async_copy(src_hbm.at[pl.ds(src_row, 1)], dst.at[pl.ds(dst_row, 1)], sem)


_DMA_UNROLL = 8


def _gather_body(idx_ref, x_hbm, o_ref, buf, sem, *, tg):
    i = pl.program_id(0)

    def issue(tile, slot):
        base = tile * tg

        def start(grp, carry):
            for u in range(_DMA_UNROLL):
                r = grp * _DMA_UNROLL + u
                _row_copy(x_hbm, buf.at[slot], sem.at[slot], idx_ref[base + r], r).start(priority=u % 2)
            return carry

        lax.fori_loop(0, tg // _DMA_UNROLL, start, 0)

    @pl.when(i == 0)
    def _():
        issue(0, 0)

    @pl.when(i + 1 < pl.num_programs(0))
    def _():
        issue(i + 1, (i + 1) % 2)

    slot = i % 2
    pltpu.make_async_copy(x_hbm.at[pl.ds(0, tg)], buf.at[slot], sem.at[slot]).wait()
    o_ref[...] = buf[slot].astype(o_ref.dtype)


def _gather_rows(x, row_idx, *, tg, out_dtype):
    r = row_idx.shape[0]
    d = x.shape[1]
    return pl.pallas_call(
        functools.partial(_gather_body, tg=tg),
        grid_spec=pltpu.PrefetchScalarGridSpec(
            num_scalar_prefetch=1, grid=(r // tg,),
            in_specs=[pl.BlockSpec(memory_space=pl.ANY)],
            out_specs=pl.BlockSpec((tg, d), lambda i, idx: (i, 0)),
            scratch_shapes=[pltpu.VMEM((2, tg, d), x.dtype), pltpu.SemaphoreType.DMA((2,))]),
        out_shape=jax.ShapeDtypeStruct((r, d), out_dtype),
        compiler_params=_params("arbitrary"), name="gather_rows")(row_idx, x)


def _gmm_body(te_ref, first_ref, next_ref, nv_ref, a_ref, *rest, n_w, swiglu, n_sub):
    w_hbm = rest[:n_w]
    o_ref, wf_ref, wb_ref, sem, run_ref = rest[n_w:]
    j = pl.program_id(0)
    tn = o_ref.shape[1]
    tm = o_ref.shape[0] // n_sub

    def tile_copies(expert, col_tile, slot):
        col = pl.multiple_of(col_tile * tn, 128)
        return [pltpu.make_async_copy(w_hbm[i].at[expert, :, pl.ds(col, tn)], wf_ref.at[slot, i], sem.at[slot, i])
                for i in range(n_w)]

    @pl.when((j == 0) & (pl.program_id(1) == 0))
    def _():
        run_ref[0] = 0
        for c in tile_copies(te_ref[0], 0, 0):
            c.start()

    for sub in range(n_sub):
        t = pl.program_id(1) * n_sub + sub
        rows = slice(sub * tm, (sub + 1) * tm)

        @pl.when((t < nv_ref[0]) & (first_ref[t] == 1))
        def _():
            slot = run_ref[0] % 2
            for c in tile_copies(te_ref[t], j, slot):
                c.wait()
            for i in range(n_w):
                wb_ref[i] = wf_ref[slot, i].astype(bf16)
            nxt = next_ref[t]

            @pl.when(nxt >= 0)
            def _():
                for c in tile_copies(te_ref[nxt], j, 1 - slot):
                    c.start()

            @pl.when((nxt < 0) & (j + 1 < pl.num_programs(0)))
            def _():
                for c in tile_copies(te_ref[0], j + 1, 1 - slot):
                    c.start()

            run_ref[0] = run_ref[0] + 1

        @pl.when(t < nv_ref[0])
        def _():
            a = a_ref[rows, :]
            if swiglu:
                g = jnp.dot(a, wb_ref[0], preferred_element_type=f32)
                u = jnp.dot(a, wb_ref[1], preferred_element_type=f32)
                o_ref[rows, :] = (g * jax.nn.sigmoid(g) * u).astype(o_ref.dtype)
            else:
                o_ref[rows, :] = jnp.dot(a, wb_ref[0], preferred_element_type=f32).astype(o_ref.dtype)

        @pl.when(t >= nv_ref[0])
        def _():
            o_ref[rows, :] = jnp.zeros((tm, tn), o_ref.dtype)


def _gmm(a_sorted, weights, tables, *, tm, tn, n_sub, swiglu, out_dtype, name):
    tile_expert, tile_first, tile_next, n_valid = tables
    r, k = a_sorted.shape
    n = weights[0].shape[-1]
    n_w = len(weights)
    assert (r // tm) % n_sub == 0
    return pl.pallas_call(
        functools.partial(_gmm_body, n_w=n_w, swiglu=swiglu, n_sub=n_sub),
        grid_spec=pltpu.PrefetchScalarGridSpec(
            num_scalar_prefetch=4, grid=(n // tn, r // (tm * n_sub)),
            in_specs=([pl.BlockSpec((tm * n_sub, k), lambda j, t, *_: (t, 0))]
                      + [pl.BlockSpec(memory_space=pl.ANY)] * n_w),
            out_specs=pl.BlockSpec((tm * n_sub, tn), lambda j, t, *_: (t, j)),
            scratch_shapes=[pltpu.VMEM((2, n_w, k, tn), f32), pltpu.VMEM((n_w, k, tn), bf16),
                            pltpu.SemaphoreType.DMA((2, n_w)), pltpu.SMEM((1,), jnp.int32)]),
        out_shape=jax.ShapeDtypeStruct((r, n), out_dtype),
        compiler_params=_params("arbitrary", "arbitrary", vmem=_VMEM_LIMIT_GMM), name=name,
    )(tile_expert, tile_first, tile_next, n_valid, a_sorted, *weights)


def _combine_body(pos_ref, y_hbm, gate_ref, r_ref, gn_ref, o_ref, e_ref, buf, sem, *, tc):
    i = pl.program_id(0)

    def issue(tile, slot):
        base = tile * tc

        def start(r, carry):
            for kk in range(TOP_K):
                _row_copy(y_hbm, buf.at[slot, kk], sem.at[slot], pos_ref[(base + r) * TOP_K + kk], r).start()
            return carry

        lax.fori_loop(0, tc, start, 0, unroll=_DMA_UNROLL)

    @pl.when(i == 0)
    def _():
        issue(0, 0)

    @pl.when(i + 1 < pl.num_programs(0))
    def _():
        issue(i + 1, (i + 1) % 2)

    slot = i % 2
    for kk in range(TOP_K):
        pltpu.make_async_copy(y_hbm.at[pl.ds(0, tc)], buf.at[slot, kk], sem.at[slot]).wait()
    gate = gate_ref[...]
    out = r_ref[...] + gate[:, 0:1] * buf[slot, 0] + gate[:, 1:2] * buf[slot, 1]
    o_ref[...] = out
    e_ref[...] = _rms(out, gn_ref[...]).astype(e_ref.dtype)


def _combine(y_sorted, pos_flat, gates, resid, g_next, *, tc):
    m, d = resid.shape
    rows = lambda i, pos: (i, 0)
    return pl.pallas_call(
        functools.partial(_combine_body, tc=tc),
        grid_spec=pltpu.PrefetchScalarGridSpec(
            num_scalar_prefetch=1, grid=(m // tc,),
            in_specs=[pl.BlockSpec(memory_space=pl.ANY), pl.BlockSpec((tc, TOP_K), rows),
                      pl.BlockSpec((tc, d), rows), pl.BlockSpec((1, d), lambda i, pos: (0, 0))],
            out_specs=[pl.BlockSpec((tc, d), rows), pl.BlockSpec((tc, d), rows)],
            scratch_shapes=[pltpu.VMEM((2, TOP_K, tc, d), f32), pltpu.SemaphoreType.DMA((2,))]),
        out_shape=[jax.ShapeDtypeStruct((m, d), f32), jax.ShapeDtypeStruct((m, d), bf16)],
        compiler_params=_params("arbitrary"), name="moe_combine",
    )(pos_flat, y_sorted, gates, resid, g_next.reshape(1, d))


def _routing_tables(top_i, n_exp, tm, n_tiles):
    m = top_i.shape[0]
    e_flat = top_i.reshape(-1)
    onehot = (e_flat[:, None] == jnp.arange(n_exp, dtype=jnp.int32)[None, :]).astype(jnp.int32)
    rank = jnp.take_along_axis(jnp.cumsum(onehot, axis=0), e_flat[:, None], axis=1)[:, 0] - 1
    counts = jnp.sum(onehot, axis=0)
    tiles_per = (counts + tm - 1) // tm
    tile_end = jnp.cumsum(tiles_per)
    tile_start = tile_end - tiles_per
    pos = tile_start[e_flat] * tm + rank
    token = jnp.arange(m * TOP_K, dtype=jnp.int32) // TOP_K
    row_token = jnp.zeros((n_tiles * tm,), jnp.int32).at[pos].set(token)
    n_valid = tile_end[-1]
    tid = jnp.minimum(jnp.arange(n_tiles, dtype=jnp.int32), n_valid - 1)
    tile_expert = jnp.sum((tid[:, None] >= tile_end[None, :]).astype(jnp.int32), axis=1)
    tile_first = jnp.concatenate([jnp.ones((1,), jnp.int32),
                                  (tile_expert[1:] != tile_expert[:-1]).astype(jnp.int32)])
    ids = jnp.arange(n_tiles, dtype=jnp.int32)
    starts = jnp.where((tile_first == 1) & (ids < n_valid), ids, n_tiles)
    later = jnp.concatenate([lax.cummin(starts, reverse=True)[1:], jnp.full((1,), n_tiles, jnp.int32)])
    tile_next = jnp.where(later >= n_tiles, -1, later).astype(jnp.int32)
    return (pos.astype(jnp.int32), row_token,
            (tile_expert.astype(jnp.int32), tile_first, tile_next, n_valid.reshape(1).astype(jnp.int32)))


def _moe_ffn(h, g_ffn, w_router, b_router, wg, wu, wd, g_next, *, tm_tok, tm, tn_up, tn_down):
    m = h.shape[0]
    n_exp = wg.shape[0]
    c_f32, top_i, top_g = _router(h, g_ffn, w_router, b_router, tm=tm_tok)
    n_tiles = 2 * (((m * TOP_K) // tm + n_exp + 1) // 2)
    pos, row_token, tables = _routing_tables(top_i, n_exp, tm, n_tiles)
    x_sorted = _gather_rows(c_f32, row_token, tg=tm, out_dtype=bf16)
    h_sorted = _gmm(x_sorted, (wg, wu), tables, tm=tm, tn=tn_up, n_sub=2, swiglu=True, out_dtype=bf16,
                    name="moe_gate_up")
    y_sorted = _gmm(h_sorted, (wd,), tables, tm=tm, tn=tn_down, n_sub=1, swiglu=False, out_dtype=f32,
                    name="moe_down")
    return _combine(y_sorted, pos, top_g, h, g_next, tc=tm_tok // 2)


def kernel(x_prompt, x_sample, state_mlstm_C, state_mlstm_n, state_mlstm_m, state_s5_re, state_s5_im, state_ssd, cache_conv, p_prompt, p_sample, g_mix, w_in, b_igate, b_fgate, g_ml, s5_lam_re, s5_lam_im, s5_log_dt, s5_b_re, s5_b_im, s5_c_re, s5_c_im, s5_d, s5_w_glu, s5_b_glu, g_s5, ssd_conv_w, ssd_conv_b, ssd_dt_bias, ssd_a_log, ssd_d, g_ssd, w_out, g_ffn, ffn_w_gate, ffn_w_up, ffn_w_down, w_router, b_router, moe_w_gate, moe_w_up, moe_w_down, g_ple, w_ple, w_ple_gate, g_final):
    bp, seq, d = x_prompt.shape
    bs = x_sample.shape[0]
    tp = bp * seq
    m = tp + bs
    tm = _token_tile(tp, bs)
    heads = ML_HEADS
    t5 = math.gcd(seq, S5_CHUNK)
    cl = math.gcd(seq, CHUNK)
    nc = seq // cl

    p_p = p_prompt.reshape(DEPTH, tp, -1)
    p_s = p_sample.reshape(DEPTH, bs, -1)
    w_in_t = jnp.swapaxes(w_in, 1, 2)
    outs_p = [[] for _ in range(7)]
    outs_s = [[] for _ in range(5)]
    c_stack = s_stack = None
    mats_p, mats_s = jax.vmap(functools.partial(_s5_matrices, t=t5))(
        s5_lam_re, s5_lam_im, s5_log_dt, s5_b_re, s5_b_im, s5_c_re, s5_c_im)

    for i in range(DEPTH):
        if i == 0:
            h, a = _rmsnorm_in(x_prompt.reshape(tp, d), x_sample.reshape(bs, d), g_mix[i], tm)
        else:
            a = _rmsnorm(h, g_mix[i], bf16, tm)
        qkvo = _mm_nt(a, w_in_t, tm=tm, tn=1024, layer=i, row_off=0, n_rows=_OFF_GATES, name="in_proj_qkvo")
        uzx, gd_tiles = _mm_nt_gates(a, w_in_t, tm=tm, tn=1024, layer=i, row_off=_OFF_U, n_rows=_OFF_DT - _OFF_U,
                               off1=_OFF_GATES, n1=2 * heads, off2=_OFF_DT, n2=SSD_HEADS,
                               name="in_proj_uzx")
        gd = gd_tiles[0]

        gates_t = jnp.transpose(gd[:tp, :2 * heads].reshape(bp, nc, cl, 2 * heads), (0, 1, 3, 2))
        bias8 = jnp.concatenate([b_igate[i], b_fgate[i]])
        h_ml_p, c_p, n_p, m_p = _mlstm_prompt(qkvo, gates_t, bias8.reshape(2 * heads, 1), g_ml[i], batch=bp, seq=seq)
        h_ml_s, c_stack, n_s, m_s = _mlstm_sample(qkvo, gd, bias8.reshape(1, 2 * heads), g_ml[i], state_mlstm_C,
                                                  state_mlstm_n, state_mlstm_m, layer=i, row0=tp, c_stack=c_stack)

        y_p, x5_p = _s5_prompt(uzx, mats_p, batch=bp, seq=seq, layer=i)
        s5re_p, s5im_p = x5_p[..., :S5_STATE], x5_p[..., S5_STATE:]
        u_s = jnp.transpose(uzx[tp:, :S5_WIDTH].reshape(bs, S5_GROUPS, S5_CH), (1, 0, 2)).astype(bf16)
        x0_s = jnp.swapaxes(jnp.concatenate([state_s5_re[i], state_s5_im[i]], axis=-1), 0, 1)
        y_s, x5_s = _s5_scan(u_s, mats_s, x0_s, nc=1, batch=bs, layer=i)
        s5re_s, s5im_s = x5_s[..., :S5_STATE], x5_s[..., S5_STATE:]
        y_s = jnp.transpose(y_s, (1, 0, 2)).reshape(bs, S5_WIDTH)
        y5 = _s5_glu(y_p, y_s, uzx, s5_d[i].reshape(1, S5_WIDTH), s5_w_glu,
                     s5_b_glu[i].reshape(1, S5_WIDTH), g_s5[i].reshape(1, S5_WIDTH), layer=i, tm=tm)

        dt_t = jnp.transpose(gd[:tp, 2 * heads:2 * heads + SSD_HEADS].reshape(bp, nc, cl, SSD_HEADS), (0, 1, 3, 2))
        y_ssd_p, ssd_p = _ssd_prompt(uzx, gd, dt_t, ssd_conv_w[i], ssd_conv_b[i], ssd_dt_bias[i], ssd_a_log[i],
                                     ssd_d[i], g_ssd[i], batch=bp, seq=seq)
        y_ssd_s, s_stack = _ssd_sample(uzx, cache_conv, gd, ssd_conv_w[i], ssd_conv_b[i], ssd_dt_bias[i],
                                       ssd_a_log[i], ssd_d[i], g_ssd[i], state_ssd, layer=i, row0=tp,
                                       s_stack=s_stack)
        conv_p = jnp.stack([uzx[(b + 1) * seq - (SSD_CONV - 1):(b + 1) * seq, S5_WIDTH + SSD_WIDTH:]
                            for b in range(bp)])
        conv_s = jnp.concatenate([cache_conv[i][:, 1:], uzx[tp:, S5_WIDTH + SSD_WIDTH:].reshape(bs, 1, -1)], axis=1)

        for lst, s in zip(outs_p, (c_p, n_p.reshape(bp, heads, ML_DK), m_p.reshape(bp, heads),
                                   jnp.swapaxes(s5re_p, 0, 1), jnp.swapaxes(s5im_p, 0, 1),
                                   ssd_p.reshape(bp, SSD_HEADS, SSD_HEAD_DIM, SSD_STATE), conv_p)):
            lst.append(s)
        for lst, s in zip(outs_s, (n_s, m_s, jnp.swapaxes(s5re_s, 0, 1), jnp.swapaxes(s5im_s, 0, 1), conv_s)):
            lst.append(s)

        h = _mm_mix(h_ml_p.reshape(tp, -1), h_ml_s, y5, y_ssd_p.reshape(tp, -1), y_ssd_s, w_out, h, tm=tm, tn=1024,
                    layer=i, name="out_proj")

        j = i // 2
        if i % 2 == 0:
            cn = _rmsnorm(h, g_ffn[i], bf16, tm)
            tm_up = 2 * tm if m % (2 * tm) == 0 else tm
            hid = _mm_swiglu(cn, ffn_w_gate, ffn_w_up, tm=tm_up, tn=512, layer=j, name="ffn_gate_up")
            h = _mm(hid, ffn_w_down, tm=tm // 2, tn=512, layer=j, resid=h, name="ffn_down")
            e = _rmsnorm(h, g_ple[i], bf16, tm)
        else:
            n_moe = moe_w_gate.shape[0]
            sel = lambda w: w.reshape((n_moe * N_EXPERTS,) + w.shape[2:])[j * N_EXPERTS:(j + 1) * N_EXPERTS] if n_moe > 1 else w.reshape(w.shape[1:])
            h, e = _moe_ffn(h, g_ffn[i], w_router[j], b_router[j], sel(moe_w_gate), sel(moe_w_up), sel(moe_w_down),
                            g_ple[i], tm_tok=tm, tm=256, tn_up=1024, tn_down=512)

        h = _mm_ple(e, p_p, p_s, w_ple_gate, w_ple, h, tm=tm, tn=1024, layer=i)

    y_p, y_s = _rmsnorm_out(h, g_final, tp, tm)
    n_s, m_s, s5re_s, s5im_s, conv_s = (jnp.stack(l) for l in outs_s)
    return ((y_p.reshape(bp, seq, d), y_s.reshape(bs, 1, d)) + tuple(jnp.stack(l) for l in outs_p)
            + (c_stack, n_s, m_s, s5re_s, s5im_s, s_stack, conv_s))
```
